```python
import math
import jax
import jax.numpy as jnp
from jax import lax
import numpy as np

D_MODEL = 1024
BATCH = 8
SEQ = 2048
DEPTH = 4
DEC_BATCH = 128
DEC_SEQ = 4
PAST_LEN = 8192
PAGE_SIZE = 128

N_MIXERS = 3
LAYER_KIND = tuple(i % N_MIXERS for i in range(DEPTH))
LAYER_SLOT = tuple(LAYER_KIND[:i].count(LAYER_KIND[i]) for i in range(DEPTH))
N_ATTN = LAYER_KIND.count(0)
N_HGRN = LAYER_KIND.count(1)
N_SSD = LAYER_KIND.count(2)
RMS_EPS = 1e-6

ATT_HEAD_DIM = 64
ATT_HEADS = D_MODEL // ATT_HEAD_DIM
ATT_KV_HEADS = 4
ATT_GROUP = ATT_HEADS // ATT_KV_HEADS
ATT_Q_DIM = ATT_HEADS * ATT_HEAD_DIM
ATT_KV_DIM = ATT_KV_HEADS * ATT_HEAD_DIM
WINDOW = 128

HG_HEADS = 8
HG_EXPAND = 128
HG_HEAD_V = D_MODEL // HG_HEADS
HG_F_DIM = HG_HEADS * HG_EXPAND
HG_V_DIM = HG_HEADS * HG_HEAD_V
HG_CHUNK = 16

SSD_D_INNER = 2 * D_MODEL
SSD_HEAD_DIM = 64
SSD_HEADS = SSD_D_INNER // SSD_HEAD_DIM
SSD_STATE = 128
SSD_GROUPS = 4
SSD_HPG = SSD_HEADS // SSD_GROUPS
SSD_CONV = 4
SSD_CONV_DIM = SSD_D_INNER + 2 * SSD_GROUPS * SSD_STATE
SSD_IN_DIM = SSD_D_INNER + SSD_CONV_DIM + SSD_HEADS
SSD_CHUNK = 64

D_FF = 2816
FFN_CONV = 3

kernel_name = 'hybrid_swa_hgrn2_ssd_convffn_step'


def rmsnorm(x, w):
    xf = x.astype(jnp.float32)
    y = xf * lax.rsqrt(jnp.mean(xf * xf, axis=-1, keepdims=True) + RMS_EPS)
    return (y * w.astype(jnp.float32)).astype(x.dtype)


def group_rmsnorm(x, w):
    xf = x.astype(jnp.float32)
    y = xf * lax.rsqrt(jnp.mean(xf * xf, axis=-1, keepdims=True) + RMS_EPS)
    y = y.reshape(x.shape[:-2] + (-1,))
    return (y * w.astype(jnp.float32)).astype(x.dtype)


def pad_time(a, pad):
    return jnp.pad(a, [(0, 0), (0, pad)] + [(0, 0)] * (a.ndim - 2))


def causal_dwconv(u, w, b, buf):
    width = w.shape[0]
    bsz, t, ch = u.shape
    if buf is None:
        buf = jnp.zeros((bsz, width - 1, ch), u.dtype)
    full = jnp.concatenate([buf.astype(u.dtype), u], axis=1)
    y = b.astype(u.dtype)
    for k in range(width):
        y = y + full[:, k:k + t] * w[k]
    return y, full[:, t:]


def sink_attend(q, k, v, mask, sinks):
    s = jnp.einsum('bnqkgd,bnskd->bnkgqs', q, k).astype(jnp.float32) * (ATT_HEAD_DIM ** -0.5)
    s = jnp.where(mask[None, :, None, None], s, -jnp.inf)
    sink = sinks.astype(jnp.float32).reshape(ATT_KV_HEADS, ATT_GROUP)[None, None, :, :, None, None]
    m = jnp.maximum(jnp.max(s, axis=-1, keepdims=True), sink)
    p = jnp.exp(s - m)
    p = p / (jnp.sum(p, axis=-1, keepdims=True) + jnp.exp(sink - m))
    return jnp.einsum('bnkgqs,bnskd->bnqkgd', p.astype(v.dtype), v)


def swa_mixer(h, w_qkv, b_qkv, sinks, w_o, b_o, cache_k, cache_v):
    bsz, t, _ = h.shape
    qkv = h @ w_qkv + b_qkv
    q, k, v = jnp.split(qkv, [ATT_Q_DIM, ATT_Q_DIM + ATT_KV_DIM], axis=-1)
    q = q.reshape(bsz, t, ATT_KV_HEADS, ATT_GROUP, ATT_HEAD_DIM)
    k = k.reshape(bsz, t, ATT_KV_HEADS, ATT_HEAD_DIM)
    v = v.reshape(bsz, t, ATT_KV_HEADS, ATT_HEAD_DIM)
    if cache_k is None:
        nb = t // WINDOW
        qb = q.reshape(bsz, nb, WINDOW, ATT_KV_HEADS, ATT_GROUP, ATT_HEAD_DIM)
        kb = k.reshape(bsz, nb, WINDOW, ATT_KV_HEADS, ATT_HEAD_DIM)
        vb = v.reshape(bsz, nb, WINDOW, ATT_KV_HEADS, ATT_HEAD_DIM)
        kk = jnp.concatenate([jnp.concatenate([jnp.zeros_like(kb[:, :1]), kb[:, :-1]], axis=1), kb], axis=2)
        vv = jnp.concatenate([jnp.concatenate([jnp.zeros_like(vb[:, :1]), vb[:, :-1]], axis=1), vb], axis=2)
        rel_q = jnp.arange(WINDOW) + WINDOW
        rel_k = jnp.arange(2 * WINDOW)
        diff = rel_q[:, None] - rel_k[None, :]
        band = (diff >= 0) & (diff < WINDOW)
        has_prev = jnp.arange(nb) > 0
        mask = band[None] & (has_prev[:, None, None] | (rel_k >= WINDOW)[None, None, :])
        o = sink_attend(qb, kk, vv, mask, sinks)
        new_k, new_v = k[:, t - WINDOW:], v[:, t - WINDOW:]
    else:
        kk = jnp.concatenate([cache_k.astype(k.dtype), k], axis=1)
        vv = jnp.concatenate([cache_v.astype(v.dtype), v], axis=1)
        qpos = jnp.arange(t)
        kpos = jnp.arange(WINDOW + t) - WINDOW
        diff = qpos[:, None] - kpos[None, :]
        mask = (diff >= 0) & (diff < WINDOW) & (kpos + PAST_LEN >= 0)[None, :]
        o = sink_attend(q[:, None], kk[:, None], vv[:, None], mask[None], sinks)
        new_k, new_v = kk[:, t:], vv[:, t:]
    o = o.reshape(bsz, t, ATT_Q_DIM)
    return o @ w_o + b_o, new_k, new_v


def gla_chunked(q, k, v, g, s0, chunk):
    bsz, t, nh, dk = q.shape
    dv = v.shape[-1]
    c = min(chunk, t)
    pad = (-t) % c
    q, k, v, g = (pad_time(a.astype(jnp.float32), pad) for a in (q, k, v, g))
    n = (t + pad) // c
    q = q.reshape(bsz, n, c, nh, dk)
    k = k.reshape(bsz, n, c, nh, dk)
    v = v.reshape(bsz, n, c, nh, dv)
    b = jnp.cumsum(g.reshape(bsz, n, c, nh, dk), axis=2)
    b_last = b[:, :, -1]
    q_in = q * jnp.exp(b)
    k_in = k * jnp.exp(-b)
    tril = jnp.tril(jnp.ones((c, c), bool))
    a = jnp.where(tril, jnp.einsum('bnthk,bnshk->bnhts', q_in, k_in), 0.0)
    o_intra = jnp.einsum('bnhts,bnshv->bnthv', a, v)
    k_dec = k * jnp.exp(b_last[:, :, None] - b)
    decay = jnp.exp(b_last)
    s = jnp.zeros((bsz, nh, dk, dv), jnp.float32) if s0 is None else s0.astype(jnp.float32)

    def step(state, inp):
        q_c, k_c, v_c, d_c = inp
        o_c = jnp.einsum('bthk,bhkv->bthv', q_c, state)
        state = state * d_c[..., None] + jnp.einsum('bthk,bthv->bhkv', k_c, v_c)
        return state, o_c

    xs = tuple(jnp.moveaxis(a_, 1, 0) for a_ in (q_in, k_dec, v, decay))
    s, o_inter = lax.scan(step, s, xs)
    o = (o_intra + jnp.moveaxis(o_inter, 0, 1)).reshape(bsz, n * c, nh, dv)[:, :t]
    return o, s


def hgrn2_mixer(h, w_in, lb, norm_w, w_o, s0):
    bsz, t, _ = h.shape
    q, fr, i, g = jnp.split(h @ w_in, [HG_F_DIM, 2 * HG_F_DIM, 2 * HG_F_DIM + HG_V_DIM], axis=-1)
    q = jax.nn.silu(q).reshape(bsz, t, HG_HEADS, HG_EXPAND)
    f = lb + (1.0 - lb) * jax.nn.sigmoid(fr.astype(jnp.float32))
    log_f = jnp.log(f).reshape(bsz, t, HG_HEADS, HG_EXPAND)
    k = (1.0 - f).reshape(bsz, t, HG_HEADS, HG_EXPAND)
    v = i.reshape(bsz, t, HG_HEADS, HG_HEAD_V)
    o, s = gla_chunked(q, k, v, log_f, s0, HG_CHUNK)
    o = group_rmsnorm(o, norm_w) * jax.nn.silu(g.astype(jnp.float32))
    return o.astype(h.dtype) @ w_o, s.astype(h.dtype)


def ssd_chunked(x, dt, a_neg, bm, cm, s0, chunk):
    bsz, t, nh, hp = x.shape
    c = min(chunk, t)
    pad = (-t) % c
    x, dt, bm, cm = (pad_time(a.astype(jnp.float32), pad) for a in (x, dt, bm, cm))
    nc = (t + pad) // c
    xf = x.reshape(bsz, nc, c, SSD_GROUPS, SSD_HPG, hp)
    dtf = dt.reshape(bsz, nc, c, SSD_GROUPS, SSD_HPG)
    bf = bm.reshape(bsz, nc, c, SSD_GROUPS, SSD_STATE)
    cf = cm.reshape(bsz, nc, c, SSD_GROUPS, SSD_STATE)
    acum = jnp.cumsum(dtf * a_neg.reshape(SSD_GROUPS, SSD_HPG), axis=2)
    xdt = xf * dtf[..., None]
    seg = acum[:, :, :, None] - acum[:, :, None]
    tril = jnp.tril(jnp.ones((c, c), bool))[:, :, None, None]
    lmat = jnp.exp(jnp.where(tril, seg, -jnp.inf))
    cb = jnp.einsum('bctgn,bcsgn->bctsg', cf, bf)
    y_intra = jnp.einsum('bctsg,bctsgh,bcsghp->bctghp', cb, lmat, xdt)
    a_last = acum[:, :, -1]
    in_dec = jnp.exp(a_last[:, :, None] - acum)
    out_dec = jnp.exp(acum)
    if s0 is None:
        s = jnp.zeros((bsz, SSD_GROUPS, SSD_HPG, hp, SSD_STATE), jnp.float32)
    else:
        s = s0.astype(jnp.float32).reshape(bsz, SSD_GROUPS, SSD_HPG, hp, SSD_STATE)

    def step(state, inp):
        c_c, b_c, xdt_c, in_c, out_c, al_c = inp
        y_c = jnp.einsum('btgn,bghpn->btghp', c_c, state) * out_c[..., None]
        state = state * jnp.exp(al_c)[..., None, None] + jnp.einsum('btgn,btghp->bghpn', b_c, xdt_c * in_c[..., None])
        return state, y_c

    xs = tuple(jnp.moveaxis(a_, 1, 0) for a_ in (cf, bf, xdt, in_dec, out_dec, a_last))
    s, y_inter = lax.scan(step, s, xs)
    y = (y_intra + jnp.moveaxis(y_inter, 0, 1)).reshape(bsz, nc * c, nh, hp)[:, :t]
    return y, s.reshape(bsz, nh, hp, SSD_STATE)


def ssd_mixer(h, w_in, conv_w, conv_b, dt_bias, a_log, d_skip, norm_w, w_o, s0, conv_buf):
    bsz, t, _ = h.shape
    z, xbc, dt_raw = jnp.split(h @ w_in, [SSD_D_INNER, SSD_D_INNER + SSD_CONV_DIM], axis=-1)
    xbc, new_buf = causal_dwconv(xbc, conv_w, conv_b, conv_buf)
    xbc = jax.nn.silu(xbc)
    xs, bm, cm = jnp.split(xbc, [SSD_D_INNER, SSD_D_INNER + SSD_GROUPS * SSD_STATE], axis=-1)
    xs = xs.reshape(bsz, t, SSD_HEADS, SSD_HEAD_DIM)
    bm = bm.reshape(bsz, t, SSD_GROUPS, SSD_STATE)
    cm = cm.reshape(bsz, t, SSD_GROUPS, SSD_STATE)
    dt = jax.nn.softplus(dt_raw.astype(jnp.float32) + dt_bias.astype(jnp.float32))
    a_neg = -jnp.exp(a_log.astype(jnp.float32))
    y, s = ssd_chunked(xs, dt, a_neg, bm, cm, s0, SSD_CHUNK)
    y = y + d_skip.astype(jnp.float32)[:, None] * xs.astype(jnp.float32)
    y = y.reshape(bsz, t, SSD_D_INNER) * jax.nn.silu(z.astype(jnp.float32))
    y = group_rmsnorm(y.reshape(bsz, t, SSD_GROUPS, SSD_D_INNER // SSD_GROUPS), norm_w)
    return y.astype(h.dtype) @ w_o, s.astype(h.dtype), new_buf


def conv_ffn(h, w_up, conv_w, conv_b, w_down, buf):
    a, b = jnp.split(h @ w_up, 2, axis=-1)
    a, new_buf = causal_dwconv(a, conv_w, conv_b, buf)
    return (jax.nn.silu(a) * b) @ w_down, new_buf


def setup_inputs(seed: int = 0) -> dict:
    key = jax.random.key(seed)
    keys = iter(jax.random.split(key, 48))

    def nrm(shape, scale):
        return scale * jax.random.normal(next(keys), shape, jnp.float32)

    def gain(shape):
        return 1.0 + nrm(shape, 0.02)

    x_prompt = nrm((BATCH, SEQ, D_MODEL), 1.0)
    x_sample = nrm((DEC_BATCH, DEC_SEQ, D_MODEL), 1.0)
    cache_attn_k = nrm((N_ATTN, DEC_BATCH, WINDOW, ATT_KV_HEADS, ATT_HEAD_DIM), 1.0)
    cache_attn_v = nrm((N_ATTN, DEC_BATCH, WINDOW, ATT_KV_HEADS, ATT_HEAD_DIM), 1.0)
    state_hgrn = nrm((N_HGRN, DEC_BATCH, HG_HEADS, HG_EXPAND, HG_HEAD_V), 0.5)
    state_ssm = nrm((N_SSD, DEC_BATCH, SSD_HEADS, SSD_HEAD_DIM, SSD_STATE), 0.2)
    state_ssm_conv = nrm((N_SSD, DEC_BATCH, SSD_CONV - 1, SSD_CONV_DIM), 1.0)
    state_ffn_conv = nrm((DEPTH, DEC_BATCH, FFN_CONV - 1, D_FF), 1.0)
    norm_mix_w = gain((DEPTH, D_MODEL))
    norm_ffn_w = gain((DEPTH, D_MODEL))
    norm_final_w = gain((D_MODEL,))
    attn_w_qkv = nrm((N_ATTN, D_MODEL, ATT_Q_DIM + 2 * ATT_KV_DIM), D_MODEL ** -0.5)
    attn_b_qkv = nrm((N_ATTN, ATT_Q_DIM + 2 * ATT_KV_DIM), 0.02)
    attn_sinks = nrm((N_ATTN, ATT_HEADS), 0.5)
    attn_w_o = nrm((N_ATTN, ATT_Q_DIM, D_MODEL), ATT_Q_DIM ** -0.5)
    attn_b_o = nrm((N_ATTN, D_MODEL), 0.02)
    hgrn_w_in = nrm((N_HGRN, D_MODEL, 2 * HG_F_DIM + 2 * HG_V_DIM), D_MODEL ** -0.5)
    hgrn_lb_logits = nrm((DEPTH, HG_F_DIM), 0.1)
    hgrn_norm_w = gain((N_HGRN, HG_V_DIM))
    hgrn_w_o = nrm((N_HGRN, HG_V_DIM, D_MODEL), HG_V_DIM ** -0.5)
    ssd_w_in = nrm((N_SSD, D_MODEL, SSD_IN_DIM), D_MODEL ** -0.5)
    ssd_conv_w = nrm((N_SSD, SSD_CONV, SSD_CONV_DIM), SSD_CONV ** -0.5)
    ssd_conv_b = nrm((N_SSD, SSD_CONV_DIM), 0.02)
    dt0 = jnp.exp(jax.random.uniform(next(keys), (N_SSD, SSD_HEADS), jnp.float32, math.log(1e-3), math.log(1e-1)))
    ssd_dt_bias = dt0 + jnp.log(-jnp.expm1(-dt0))
    ssd_a_log = jnp.log(jax.random.uniform(next(keys), (N_SSD, SSD_HEADS), jnp.float32, 1.0, 16.0))
    ssd_d = gain((N_SSD, SSD_HEADS))
    ssd_norm_w = gain((N_SSD, SSD_D_INNER))
    ssd_w_o = nrm((N_SSD, SSD_D_INNER, D_MODEL), SSD_D_INNER ** -0.5)
    ffn_w_up = nrm((DEPTH, D_MODEL, 2 * D_FF), D_MODEL ** -0.5)
    ffn_conv_w = nrm((DEPTH, FFN_CONV, D_FF), FFN_CONV ** -0.5)
    ffn_conv_b = nrm((DEPTH, D_FF), 0.02)
    ffn_w_down = nrm((DEPTH, D_FF, D_MODEL), D_FF ** -0.5)
    return {
        'x_prompt': x_prompt, 'x_sample': x_sample,
        'cache_attn_k': cache_attn_k, 'cache_attn_v': cache_attn_v,
        'state_hgrn': state_hgrn, 'state_ssm': state_ssm,
        'state_ssm_conv': state_ssm_conv, 'state_ffn_conv': state_ffn_conv,
        'norm_mix_w': norm_mix_w, 'norm_ffn_w': norm_ffn_w, 'norm_final_w': norm_final_w,
        'attn_w_qkv': attn_w_qkv, 'attn_b_qkv': attn_b_qkv, 'attn_sinks': attn_sinks,
        'attn_w_o': attn_w_o, 'attn_b_o': attn_b_o,
        'hgrn_w_in': hgrn_w_in, 'hgrn_lb_logits': hgrn_lb_logits,
        'hgrn_norm_w': hgrn_norm_w, 'hgrn_w_o': hgrn_w_o,
        'ssd_w_in': ssd_w_in, 'ssd_conv_w': ssd_conv_w, 'ssd_conv_b': ssd_conv_b,
        'ssd_dt_bias': ssd_dt_bias, 'ssd_a_log': ssd_a_log, 'ssd_d': ssd_d,
        'ssd_norm_w': ssd_norm_w, 'ssd_w_o': ssd_w_o,
        'ffn_w_up': ffn_w_up, 'ffn_conv_w': ffn_conv_w, 'ffn_conv_b': ffn_conv_b,
        'ffn_w_down': ffn_w_down,
    }


def reference(x_prompt, x_sample, cache_attn_k, cache_attn_v, state_hgrn, state_ssm,
              state_ssm_conv, state_ffn_conv, norm_mix_w, norm_ffn_w, norm_final_w,
              attn_w_qkv, attn_b_qkv, attn_sinks, attn_w_o, attn_b_o,
              hgrn_w_in, hgrn_lb_logits, hgrn_norm_w, hgrn_w_o,
              ssd_w_in, ssd_conv_w, ssd_conv_b, ssd_dt_bias, ssd_a_log, ssd_d, ssd_norm_w, ssd_w_o,
              ffn_w_up, ffn_conv_w, ffn_conv_b, ffn_w_down):
    lb_cum = jnp.cumsum(jax.nn.softmax(hgrn_lb_logits.astype(jnp.float32), axis=0), axis=0)
    hgrn_lb = lb_cum - lb_cum[0]

    def pick(a, j):
        return None if a is None else a[j]

    def trunk(x, c_k, c_v, s_hg, s_ssm, s_sconv, s_fconv):
        nk, nv, nhg, nssm, nsconv, nfconv = [], [], [], [], [], []
        for i in range(DEPTH):
            kind, j = LAYER_KIND[i], LAYER_SLOT[i]
            h = rmsnorm(x, norm_mix_w[i])
            if kind == 0:
                y, k_new, v_new = swa_mixer(h, attn_w_qkv[j], attn_b_qkv[j], attn_sinks[j], attn_w_o[j],
                                            attn_b_o[j], pick(c_k, j), pick(c_v, j))
                nk.append(k_new)
                nv.append(v_new)
            elif kind == 1:
                y, s_new = hgrn2_mixer(h, hgrn_w_in[j], hgrn_lb[i], hgrn_norm_w[j], hgrn_w_o[j], pick(s_hg, j))
                nhg.append(s_new)
            else:
                y, s_new, conv_new = ssd_mixer(h, ssd_w_in[j], ssd_conv_w[j], ssd_conv_b[j], ssd_dt_bias[j],
                                               ssd_a_log[j], ssd_d[j], ssd_norm_w[j], ssd_w_o[j],
                                               pick(s_ssm, j), pick(s_sconv, j))
                nssm.append(s_new)
                nsconv.append(conv_new)
            x = x + y
            h = rmsnorm(x, norm_ffn_w[i])
            y, f_new = conv_ffn(h, ffn_w_up[i], ffn_conv_w[i], ffn_conv_b[i], ffn_w_down[i], pick(s_fconv, i))
            nfconv.append(f_new)
            x = x + y
        return (rmsnorm(x, norm_final_w), jnp.stack(nk), jnp.stack(nv), jnp.stack(nhg),
                jnp.stack(nssm), jnp.stack(nsconv), jnp.stack(nfconv))

    y_prompt, k_p, v_p, hg_p, ssm_p, sconv_p, fconv_p = trunk(x_prompt, None, None, None, None, None, None)
    y_sample, k_s, v_s, hg_s, ssm_s, sconv_s, fconv_s = trunk(x_sample, cache_attn_k, cache_attn_v, state_hgrn,
                                                               state_ssm, state_ssm_conv, state_ffn_conv)
    return (y_prompt, y_sample, k_p, v_p, hg_p, ssm_p, sconv_p, fconv_p,
            k_s, v_s, hg_s, ssm_s, sconv_s, fconv_s)
```

```python
import functools

import jax
import jax.numpy as jnp
from jax import lax
from jax.experimental import pallas as pl
from jax.experimental.pallas import tpu as pltpu

bf16 = jnp.bfloat16
f32 = jnp.float32

RMS_EPS = 1e-6
N_MIXERS = 3
WINDOW = 128
PAST_LEN = 8192
ATT_HEAD_DIM = 64
ATT_KV_HEADS = 4
HG_HEADS = 8
HG_CHUNK_GROUP = 32
SSD_HEAD_DIM = 64
SSD_STATE = 128
SSD_GROUPS = 4

LANES = 128
SUBLANES = 8
VMEM_LIMIT = 56 * 1024 * 1024


def _cparams(n_axes):
    return pltpu.CompilerParams(dimension_semantics=("arbitrary",) * n_axes,
                                vmem_limit_bytes=VMEM_LIMIT)


def _dot(a, b):
    return jnp.dot(a.astype(bf16), b.astype(bf16), preferred_element_type=f32)


def _dot_nt(a, b):
    return lax.dot_general(a.astype(bf16), b.astype(bf16), (((1,), (1,)), ((), ())),
                           preferred_element_type=f32)


def _dot_tn(a, b):
    return lax.dot_general(a.astype(bf16), b.astype(bf16), (((0,), (0,)), ((), ())),
                           preferred_element_type=f32)


def _silu(x):
    return x * jax.nn.sigmoid(x)


def _rms(x, w):
    ms = jnp.mean(x * x, axis=-1, keepdims=True)
    return x * lax.rsqrt(ms + RMS_EPS) * w


def _cumsum_rows(x, tril):
    hi = x.astype(bf16)
    r1 = x - hi.astype(f32)
    mid = r1.astype(bf16)
    lo = (r1 - mid.astype(f32)).astype(bf16)
    d = functools.partial(jnp.dot, preferred_element_type=f32)
    return d(tril, hi) + d(tril, mid) + d(tril, lo)


def _tril_mask(c):
    r = lax.broadcasted_iota(jnp.int32, (c, c), 0)
    s = lax.broadcasted_iota(jnp.int32, (c, c), 1)
    return r >= s


def _norm_kernel(x_ref, w_ref, o_ref):
    o_ref[...] = _rms(x_ref[...], w_ref[...]).astype(o_ref.dtype)


def rmsnorm_call(x, w, out_dtype):
    m, d = x.shape
    tm = min(1024, m)
    return pl.pallas_call(
        _norm_kernel,
        grid=(m // tm,),
        in_specs=[pl.BlockSpec((tm, d), lambda i: (i, 0)), pl.BlockSpec((1, d), lambda i: (0, 0))],
        out_specs=pl.BlockSpec((tm, d), lambda i: (i, 0)),
        out_shape=jax.ShapeDtypeStruct((m, d), out_dtype),
        compiler_params=_cparams(1),
        name="rmsnorm",
    )(x, w.reshape(1, d))


def _proj_kernel(x_ref, w_ref, b_ref, o_ref):
    acc = jnp.dot(x_ref[...], w_ref[...].astype(bf16), preferred_element_type=f32)
    o_ref[...] = (acc + b_ref[...]).astype(o_ref.dtype)


def proj_call(xn, w, bias, n_out, tn, out_dtype, name):
    m, k = xn.shape
    tm = min(1024, m)
    if bias is None:
        bias = jnp.zeros((w.shape[1],), f32)
    return pl.pallas_call(
        _proj_kernel,
        grid=(m // tm, n_out // tn),
        in_specs=[pl.BlockSpec((tm, k), lambda i, j: (i, 0)),
                  pl.BlockSpec((k, tn), lambda i, j: (0, j)),
                  pl.BlockSpec((1, tn), lambda i, j: (0, j))],
        out_specs=pl.BlockSpec((tm, tn), lambda i, j: (i, j)),
        out_shape=jax.ShapeDtypeStruct((m, n_out), out_dtype),
        compiler_params=_cparams(2),
        name=name,
    )(xn, w, bias.reshape(1, -1))


def _outproj_kernel(a_ref, w_ref, b_ref, x_ref, nw_ref, xo_ref, xn_ref, wbf_ref):
    @pl.when(pl.program_id(0) == 0)
    def _():
        wbf_ref[...] = w_ref[...].astype(bf16)

    y = jnp.dot(a_ref[...].astype(bf16), wbf_ref[...], preferred_element_type=f32) + b_ref[...] + x_ref[...]
    xo_ref[...] = y
    xn_ref[...] = _rms(y, nw_ref[...]).astype(xn_ref.dtype)


def outproj_call(act, w, bias, x, next_w, name):
    m, k = act.shape
    d = w.shape[1]
    tm = min(512, m)
    if bias is None:
        bias = jnp.zeros((d,), f32)
    return pl.pallas_call(
        _outproj_kernel,
        grid=(m // tm,),
        in_specs=[pl.BlockSpec((tm, k), lambda i: (i, 0)),
                  pl.BlockSpec((k, d), lambda i: (0, 0)),
                  pl.BlockSpec((1, d), lambda i: (0, 0)),
                  pl.BlockSpec((tm, d), lambda i: (i, 0)),
                  pl.BlockSpec((1, d), lambda i: (0, 0))],
        out_specs=[pl.BlockSpec((tm, d), lambda i: (i, 0)), pl.BlockSpec((tm, d), lambda i: (i, 0))],
        out_shape=[jax.ShapeDtypeStruct((m, d), f32), jax.ShapeDtypeStruct((m, d), bf16)],
        scratch_shapes=[pltpu.VMEM((k, d), bf16)],
        compiler_params=_cparams(1),
        name=name,
    )(act, w, bias.reshape(1, d), x, next_w.reshape(1, d))


def _ffn_kernel(*refs, tm, tf, t_seq, tail, has_state):
    if has_state:
        (xn_ref, wa_ref, wb_ref, cw_ref, cb_ref, wd_ref, x_ref, nw_ref, e1_ref, e2_ref,
         xo_ref, xn2_ref, tail_ref, acc_ref, carry_ref) = refs
    else:
        (xn_ref, wa_ref, wb_ref, cw_ref, cb_ref, wd_ref, x_ref, nw_ref,
         xo_ref, xn2_ref, tail_ref, acc_ref, carry_ref) = refs
    i = pl.program_id(0)
    f = pl.program_id(1)
    nf = pl.num_programs(1)

    xn = xn_ref[...]
    a = jnp.dot(xn, wa_ref[...].astype(bf16), preferred_element_type=f32)
    b = jnp.dot(xn, wb_ref[...].astype(bf16), preferred_element_type=f32)
    tail_ref[0] = a[tm - tail:, :]

    rows = lax.broadcasted_iota(jnp.int32, (tm, 1), 0)
    tpos = (i * tm + rows) & (t_seq - 1)
    @pl.when(i == 0)
    def _():
        carry_ref[f] = jnp.zeros((SUBLANES, tf), f32)

    carry = carry_ref[f]
    p1 = jnp.where(rows == 0, carry[7:8, :], pltpu.roll(a, 1, 0))
    p2 = jnp.where(rows == 0, carry[6:7, :], jnp.where(rows == 1, carry[7:8, :], pltpu.roll(a, 2, 0)))
    if has_state:
        p1 = jnp.where(tpos == 0, e1_ref[...], p1)
        p2 = jnp.where(tpos < 2, e2_ref[...], p2)
    else:
        p1 = jnp.where(tpos == 0, 0.0, p1)
        p2 = jnp.where(tpos < 2, 0.0, p2)
    carry_ref[f] = a[tm - SUBLANES:, :]
    cw = cw_ref[...]
    y = cb_ref[...] + p2 * cw[0:1, :]
    y = y + p1 * cw[1:2, :]
    y = y + a * cw[2:3, :]
    g = _silu(y) * b
    contrib = jnp.dot(g.astype(bf16), wd_ref[...].astype(bf16), preferred_element_type=f32)

    @pl.when(f == 0)
    def _():
        acc_ref[...] = contrib

    @pl.when(f > 0)
    def _():
        acc_ref[...] += contrib

    @pl.when(f == nf - 1)
    def _():
        xo = acc_ref[...] + x_ref[...]
        xo_ref[...] = xo
        xn2_ref[...] = _rms(xo, nw_ref[...]).astype(xn2_ref.dtype)


def ffn_call(xn, w_up, conv_w, conv_b, w_down, x, next_w, t_seq, e1, e2, out_dtype, name):
    m, d = xn.shape
    dff = w_down.shape[0]
    tf = 256
    nf = dff // tf
    assert nf * tf == dff and (t_seq & (t_seq - 1)) == 0
    has_state = e1 is not None
    tm = min(1024, m)
    assert tm % t_seq == 0 or t_seq % tm == 0
    tail = tm if has_state else SUBLANES
    in_specs = [pl.BlockSpec((tm, d), lambda i, f: (i, 0)),
                pl.BlockSpec((d, tf), lambda i, f: (0, f)),
                pl.BlockSpec((d, tf), lambda i, f: (0, nf + f)),
                pl.BlockSpec((3, tf), lambda i, f: (0, f)),
                pl.BlockSpec((1, tf), lambda i, f: (0, f)),
                pl.BlockSpec((tf, d), lambda i, f: (f, 0)),
                pl.BlockSpec((tm, d), lambda i, f: (i, 0)),
                pl.BlockSpec((1, d), lambda i, f: (0, 0))]
    args = [xn, w_up, w_up, conv_w, conv_b.reshape(1, dff), w_down, x, next_w.reshape(1, d)]
    if has_state:
        in_specs += [pl.BlockSpec((tm, tf), lambda i, f: (i, f))] * 2
        args += [e1, e2]
    kern = functools.partial(_ffn_kernel, tm=tm, tf=tf, t_seq=t_seq, tail=tail, has_state=has_state)
    return pl.pallas_call(
        kern,
        grid=(m // tm, nf),
        in_specs=in_specs,
        out_specs=[pl.BlockSpec((tm, d), lambda i, f: (i, 0)),
                   pl.BlockSpec((tm, d), lambda i, f: (i, 0)),
                   pl.BlockSpec((1, tail, tf), lambda i, f: (i, 0, f))],
        out_shape=[jax.ShapeDtypeStruct((m, d), f32),
                   jax.ShapeDtypeStruct((m, d), out_dtype),
                   jax.ShapeDtypeStruct((m // tm, tail, dff), f32)],
        scratch_shapes=[pltpu.VMEM((tm, d), f32), pltpu.VMEM((nf, SUBLANES, tf), f32)],
        compiler_params=_cparams(2),
        name=name,
    )(*args)


def _attn_softmax_pv(s, sink_col, vj):
    m = jnp.maximum(jnp.max(s, axis=-1, keepdims=True), sink_col)
    p = jnp.exp(s - m)
    l = jnp.sum(p, axis=-1, keepdims=True) + jnp.exp(sink_col - m)
    return _dot(p, vj) / l


def _sink_col(sinks_ref, j, group, rows_per_head):
    r = lax.broadcasted_iota(jnp.int32, (group * rows_per_head, 1), 0)
    col = jnp.full((group * rows_per_head, 1), sinks_ref[j * group], f32)
    for p in range(1, group):
        col = jnp.where(r >= p * rows_per_head, sinks_ref[j * group + p], col)
    return col


def _attn_prompt_kernel(sinks_ref, q_ref, kvc_ref, kvp_ref, o_ref, *, n_heads):
    n = pl.program_id(1)
    w, hd, kvh = WINDOW, ATT_HEAD_DIM, ATT_KV_HEADS
    group = n_heads // kvh
    q = q_ref[...] * (hd ** -0.5)
    kvc = kvc_ref[...]
    kvp = kvp_ref[...]
    kv = jnp.concatenate([kvp, kvc], axis=0)
    i = lax.broadcasted_iota(jnp.int32, (w, 2 * w), 0)
    c = lax.broadcasted_iota(jnp.int32, (w, 2 * w), 1)
    band = (c > i) & (c <= i + w) & ((n > 0) | (c >= w))
    mask = jnp.concatenate([band] * group, axis=0)
    for j in range(kvh):
        kj = kv[:, j * hd:(j + 1) * hd]
        vj = kv[:, (kvh + j) * hd:(kvh + j + 1) * hd]
        qs = jnp.concatenate([q[:, (j * group + p) * hd:(j * group + p + 1) * hd] for p in range(group)], axis=0)
        s = jnp.where(mask, _dot_nt(qs, kj), -jnp.inf)
        o = _attn_softmax_pv(s, _sink_col(sinks_ref, j, group, w), vj)
        for p in range(group):
            h = j * group + p
            o_ref[:, h * hd:(h + 1) * hd] = o[p * w:(p + 1) * w, :].astype(o_ref.dtype)


def attn_prompt_call(qkv, sinks, bsz, t, n_heads):
    w, hd, kvh = WINDOW, ATT_HEAD_DIM, ATT_KV_HEADS
    nb = t // w
    qd = n_heads * hd
    kvd = 2 * kvh * hd
    assert qd % kvd == 0
    kvblk = qd // kvd
    kern = functools.partial(_attn_prompt_kernel, n_heads=n_heads)
    return pl.pallas_call(
        kern,
        grid=(bsz, nb),
        in_specs=[pl.BlockSpec(memory_space=pltpu.SMEM),
                  pl.BlockSpec((w, qd), lambda b, n: (b * nb + n, 0)),
                  pl.BlockSpec((w, kvd), lambda b, n: (b * nb + n, kvblk)),
                  pl.BlockSpec((w, kvd), lambda b, n: (b * nb + jnp.maximum(n - 1, 0), kvblk))],
        out_specs=pl.BlockSpec((w, qd), lambda b, n: (b * nb + n, 0)),
        out_shape=jax.ShapeDtypeStruct((bsz * t, qd), bf16),
        compiler_params=_cparams(2),
        name="attn_prompt",
    )(sinks, qkv, qkv, qkv)


def _attn_cache_kernel(sinks_ref, q_ref, ck_ref, cv_ref, o_ref, *, n_heads, t_valid, bt):
    w, hd, kvh = WINDOW, ATT_HEAD_DIM, ATT_KV_HEADS
    group = n_heads // kvh
    tp = q_ref.shape[1]
    qd = n_heads * hd
    tq = lax.broadcasted_iota(jnp.int32, (tp, w), 0)
    jc = lax.broadcasted_iota(jnp.int32, (tp, w), 1)
    mc = (jc > tq) & (jc - w + PAST_LEN >= 0)
    tq2 = lax.broadcasted_iota(jnp.int32, (tp, tp), 0)
    un = lax.broadcasted_iota(jnp.int32, (tp, tp), 1)
    mn = (un <= tq2) & (un < t_valid)
    mask_c = jnp.concatenate([mc] * group, axis=0)
    mask_n = jnp.concatenate([mn] * group, axis=0)

    def body(bi, carry):
        row = q_ref[bi]
        q = row[:, :qd] * (hd ** -0.5)
        ck = ck_ref[bi]
        cv = cv_ref[bi]
        for j in range(kvh):
            kc = ck[:, j * hd:(j + 1) * hd]
            vc = cv[:, j * hd:(j + 1) * hd]
            kn = row[:, qd + j * hd:qd + (j + 1) * hd]
            vn = row[:, qd + (kvh + j) * hd:qd + (kvh + j + 1) * hd]
            qs = jnp.concatenate([q[:, (j * group + p) * hd:(j * group + p + 1) * hd] for p in range(group)], axis=0)
            sc = jnp.where(mask_c, _dot_nt(qs, kc), -jnp.inf)
            sn = jnp.where(mask_n, _dot_nt(qs, kn), -jnp.inf)
            sink = _sink_col(sinks_ref, j, group, tp)
            m = jnp.maximum(jnp.maximum(jnp.max(sc, axis=-1, keepdims=True),
                                        jnp.max(sn, axis=-1, keepdims=True)), sink)
            pc = jnp.exp(sc - m)
            pn = jnp.exp(sn - m)
            l = jnp.sum(pc, axis=-1, keepdims=True) + jnp.sum(pn, axis=-1, keepdims=True) + jnp.exp(sink - m)
            o = (_dot(pc, vc) + _dot(pn, vn)) / l
            for p in range(group):
                h = j * group + p
                o_ref[bi, :, h * hd:(h + 1) * hd] = o[p * tp:(p + 1) * tp, :].astype(o_ref.dtype)
        return carry

    lax.fori_loop(0, bt, body, 0)


def attn_cache_call(qkv, cache_k, cache_v, sinks, n_heads, t_valid):
    bsz, tp, nq = qkv.shape
    w = WINDOW
    kd = cache_k.shape[-1]
    qd = n_heads * ATT_HEAD_DIM
    bt = 8
    kern = functools.partial(_attn_cache_kernel, n_heads=n_heads, t_valid=t_valid, bt=bt)
    return pl.pallas_call(
        kern,
        grid=(bsz // bt,),
        in_specs=[pl.BlockSpec(memory_space=pltpu.SMEM),
                  pl.BlockSpec((bt, tp, nq), lambda b: (b, 0, 0)),
                  pl.BlockSpec((bt, w, kd), lambda b: (b, 0, 0)),
                  pl.BlockSpec((bt, w, kd), lambda b: (b, 0, 0))],
        out_specs=pl.BlockSpec((bt, tp, qd), lambda b: (b, 0, 0)),
        out_shape=jax.ShapeDtypeStruct((bsz, tp, qd), f32),
        compiler_params=_cparams(1),
        name="attn_cache",
    )(sinks, qkv, cache_k, cache_v)


def _hgrn_kernel(*refs, c_rows, grp, t_valid, t_pad, layer, has_s0):
    if has_s0:
        p_ref, lbl_ref, nw_ref, s0_ref, y_ref, so_ref, st_ref = refs
    else:
        p_ref, lbl_ref, nw_ref, y_ref, so_ref, st_ref = refs
    c = pl.program_id(1)
    nc = pl.num_programs(1)
    nh = HG_HEADS
    dk = p_ref.shape[1] // 4 // nh
    dim = nh * dk
    ng = c_rows // grp

    @pl.when(c == 0)
    def _():
        if has_s0:
            for h in range(nh):
                st_ref[h] = s0_ref[0, h].T
        else:
            st_ref[...] = jnp.zeros_like(st_ref)

    lbl = lbl_ref[...]
    e = jnp.exp(lbl - jnp.max(lbl, axis=0, keepdims=True))
    sm = e / jnp.sum(e, axis=0, keepdims=True)
    lb = jnp.zeros((1, dim), f32)
    for r in range(1, layer + 1):
        lb = lb + sm[r:r + 1, :]

    x = p_ref[...]
    q = _silu(x[:, 0:dim])
    fg = lb + (1.0 - lb) * jax.nn.sigmoid(x[:, dim:2 * dim])
    logf = jnp.log(fg)
    k = 1.0 - fg
    v = x[:, 2 * dim:3 * dim]
    gate = x[:, 3 * dim:4 * dim]
    if t_valid < t_pad:
        valid = (c * c_rows + lax.broadcasted_iota(jnp.int32, (c_rows, 1), 0)) < t_valid
        logf = jnp.where(valid, logf, 0.0)
        k = jnp.where(valid, k, 0.0)

    tril = _tril_mask(c_rows)
    bcum = _cumsum_rows(logf, tril.astype(bf16))
    rr = lax.broadcasted_iota(jnp.int32, (c_rows, c_rows), 0)
    ss = lax.broadcasted_iota(jnp.int32, (c_rows, c_rows), 1)
    gshift = grp.bit_length() - 1
    diag_mask = tril & ((rr >> gshift) == (ss >> gshift))
    nw = nw_ref[...]

    def rows_of(vals):
        return jnp.concatenate([jnp.broadcast_to(r, (grp, dk)) for r in vals], axis=0)

    for h in range(nh):
        sl = slice(h * dk, (h + 1) * dk)
        b = bcum[:, sl]
        qh, kh, vh = q[:, sl], k[:, sl], v[:, sl]
        zero = jnp.zeros((1, dk), f32)
        r = [zero] + [b[i * grp - 1:i * grp, :] for i in range(1, ng)] + [b[c_rows - 1:c_rows, :]]
        mid = [b[i * grp + grp // 2 - 1:i * grp + grp // 2, :] for i in range(ng)]
        b_last = r[ng]
        r_start = rows_of(r[:ng])
        r_end = rows_of(r[1:])
        r_mid = rows_of(mid)
        qg = qh * jnp.exp(b - r_start)
        kt = kh * jnp.exp(r_end - b)
        qm = qh * jnp.exp(b - r_mid)
        km = kh * jnp.exp(r_mid - b)
        a = jnp.where(diag_mask, _dot_nt(qm, km), 0.0)
        if ng > 1:
            lhs, rhs = [], []
            for j in range(ng - 1):
                lhs.append(jnp.concatenate(
                    [jnp.zeros((grp, dk), f32) if i <= j else qg[i * grp:(i + 1) * grp, :] * jnp.exp(r[i] - r[j + 1])
                     for i in range(ng)], axis=0))
                rhs.append(jnp.concatenate(
                    [kt[i * grp:(i + 1) * grp, :] if i == j else jnp.zeros((grp, dk), f32) for i in range(ng)], axis=0))
            a = a + _dot_nt(jnp.concatenate(lhs, axis=1), jnp.concatenate(rhs, axis=1))
        q_inter = qg * jnp.exp(r_start)
        st = st_ref[h]
        o = _dot(a, vh) + _dot_nt(q_inter, st)
        k_dec = kt * jnp.exp(b_last - r_end)
        st_new = st * jnp.exp(b_last) + _dot_tn(vh, k_dec)
        st_ref[h] = st_new
        on = o * lax.rsqrt(jnp.mean(o * o, axis=-1, keepdims=True) + RMS_EPS)
        y_ref[:, sl] = (on * nw[:, sl] * _silu(gate[:, sl])).astype(y_ref.dtype)

        @pl.when(c == nc - 1)
        def _():
            so_ref[0, h] = st_new.T


def hgrn_call(p, lb_logits, norm_w, s0, bsz, t_pad, t_valid, layer):
    dim = p.shape[1] // 4
    nh = HG_HEADS
    dk = dim // nh
    c_rows = min(128, t_pad)
    grp = min(HG_CHUNK_GROUP, c_rows)
    nc = t_pad // c_rows
    has_s0 = s0 is not None
    in_specs = [pl.BlockSpec((c_rows, 4 * dim), lambda b, c: (b * nc + c, 0)),
                pl.BlockSpec(lb_logits.shape, lambda b, c: (0, 0)),
                pl.BlockSpec((1, dim), lambda b, c: (0, 0))]
    args = [p, lb_logits, norm_w.reshape(1, dim)]
    if has_s0:
        in_specs.append(pl.BlockSpec((1, nh, dk, dk), lambda b, c: (b, 0, 0, 0)))
        args.append(s0)
    kern = functools.partial(_hgrn_kernel, c_rows=c_rows, grp=grp, t_valid=t_valid, t_pad=t_pad,
                             layer=layer, has_s0=has_s0)
    return pl.pallas_call(
        kern,
        grid=(bsz, nc),
        in_specs=in_specs,
        out_specs=[pl.BlockSpec((c_rows, dim), lambda b, c: (b * nc + c, 0)),
                   pl.BlockSpec((1, nh, dk, dk), lambda b, c: (b, 0, 0, 0))],
        out_shape=[jax.ShapeDtypeStruct((bsz * t_pad, dim), bf16),
                   jax.ShapeDtypeStruct((bsz, nh, dk, dk), f32)],
        scratch_shapes=[pltpu.VMEM((nh, dk, dk), f32)],
        compiler_params=_cparams(2),
        name="hgrn",
    )(*args)


def _ssd_kernel(*refs, c_rows, t_valid, t_pad, has_s0):
    if has_s0:
        (zx_ref, dt_ref, cw_ref, cb_ref, dtb_ref, alog_ref, dsk_ref, nw_ref, s0_ref, buf_ref,
         y_ref, so_ref, st_ref, carry_ref, yacc_ref, xd_ref) = refs
    else:
        (zx_ref, dt_ref, cw_ref, cb_ref, dtb_ref, alog_ref, dsk_ref, nw_ref,
         y_ref, so_ref, st_ref, carry_ref, yacc_ref, xd_ref) = refs
    c = pl.program_id(1)
    nc = pl.num_programs(1)
    hp, ns, ngr = SSD_HEAD_DIM, SSD_STATE, SSD_GROUPS
    di = y_ref.shape[1]
    nh = di // hp
    hpg = nh // ngr
    gw = ngr * ns
    kconv = cw_ref.shape[0]

    @pl.when(c == 0)
    def _():
        if has_s0:
            for h in range(nh):
                st_ref[h * hp:(h + 1) * hp, :] = s0_ref[0, h]
            carry_ref[...] = jnp.zeros_like(carry_ref)
            carry_ref[SUBLANES - (kconv - 1):, :] = buf_ref[0]
        else:
            st_ref[...] = jnp.zeros_like(st_ref)
            carry_ref[...] = jnp.zeros_like(carry_ref)

    zx = zx_ref[...]
    z = zx[:, :di]
    u = zx[:, di:]
    rows = lax.broadcasted_iota(jnp.int32, (c_rows, 1), 0)
    cr = carry_ref[...]
    cw = cw_ref[...]
    yc = cb_ref[...]
    for kk in range(kconv - 1):
        back = kconv - 1 - kk
        sh = pltpu.roll(u, back, 0)
        for r0 in range(back):
            sh = jnp.where(rows == r0, cr[SUBLANES - back + r0:SUBLANES - back + r0 + 1, :], sh)
        yc = yc + sh * cw[kk:kk + 1, :]
    yc = yc + u * cw[kconv - 1:kconv, :]
    carry_ref[...] = u[c_rows - SUBLANES:, :]
    xbc = _silu(yc)
    xs = xbc[:, :di]
    bm = xbc[:, di:di + gw]
    cm = xbc[:, di + gw:di + 2 * gw]

    dt = jax.nn.softplus(dt_ref[...] + dtb_ref[...])
    if t_valid < t_pad:
        valid = (c * c_rows + rows) < t_valid
        dt = jnp.where(valid, dt, 0.0)
    a_neg = -jnp.exp(alog_ref[...])
    tril = _tril_mask(c_rows)
    acum = _cumsum_rows(dt * a_neg, tril.astype(bf16))
    if c_rows < LANES:
        acum_sq = jnp.concatenate([acum, jnp.zeros((LANES - c_rows, LANES), f32)], axis=0)
    else:
        acum_sq = acum
    acum_t = acum_sq.T[:, :c_rows]
    a_last = acum[c_rows - 1:c_rows, :]
    dsk = dsk_ref[...]

    for g in range(ngr):
        bg = bm[:, g * ns:(g + 1) * ns]
        cg = cm[:, g * ns:(g + 1) * ns]
        cb_mat = _dot_nt(cg, bg)
        gsl = slice(g * hpg * hp, (g + 1) * hpg * hp)
        y_inter = _dot_nt(cg, st_ref[gsl, :])
        for pp in range(hpg):
            h = g * hpg + pp
            hsl = slice(h * hp, (h + 1) * hp)
            col = acum[:, h:h + 1]
            row = acum_t[h:h + 1, :]
            lmat = jnp.exp(jnp.where(tril, col - row, -jnp.inf))
            xh = xs[:, hsl]
            xdt = xh * dt[:, h:h + 1]
            yh = _dot(cb_mat * lmat, xdt) + y_inter[:, pp * hp:(pp + 1) * hp] * jnp.exp(col)
            yacc_ref[:, hsl] = yh + dsk[:, hsl] * xh
            al = a_last[:, h:h + 1]
            xd_ref[:, hsl] = xdt * jnp.exp(al - col)
            st_ref[hsl, :] = st_ref[hsl, :] * jnp.exp(al)
        st_ref[gsl, :] = st_ref[gsl, :] + _dot_tn(xd_ref[:, gsl], bg)

    y = yacc_ref[...] * _silu(z)
    nw = nw_ref[...]
    gdim = di // ngr
    for g in range(ngr):
        sl = slice(g * gdim, (g + 1) * gdim)
        yg = y[:, sl]
        y_ref[:, sl] = (yg * lax.rsqrt(jnp.mean(yg * yg, axis=-1, keepdims=True) + RMS_EPS) * nw[:, sl]).astype(y_ref.dtype)

    @pl.when(c == nc - 1)
    def _():
        for h in range(nh):
            so_ref[0, h] = st_ref[h * hp:(h + 1) * hp, :]


def ssd_call(zx, dt_raw, conv_w, conv_b, dt_bias, a_log, d_skip, norm_w, s0, conv_buf, bsz, t_pad, t_valid):
    hp, ns = SSD_HEAD_DIM, SSD_STATE
    conv_dim = conv_w.shape[1]
    di = zx.shape[1] - conv_dim
    nh = di // hp
    c_rows = min(128, t_pad)
    nc = t_pad // c_rows
    has_s0 = s0 is not None
    pad_h = lambda a: jnp.pad(a.reshape(1, nh), ((0, 0), (0, LANES - nh)))
    full = lambda shape: pl.BlockSpec(shape, lambda b, c: (0,) * len(shape))
    in_specs = [pl.BlockSpec((c_rows, di + conv_dim), lambda b, c: (b * nc + c, 0)),
                pl.BlockSpec((c_rows, LANES), lambda b, c: (b * nc + c, 0)),
                full(conv_w.shape), full((1, conv_dim)), full((1, LANES)), full((1, LANES)),
                full((1, di)), full((1, di))]
    args = [zx, dt_raw, conv_w, conv_b.reshape(1, conv_dim), pad_h(dt_bias), pad_h(a_log),
            jnp.repeat(d_skip, hp).reshape(1, di), norm_w.reshape(1, di)]
    if has_s0:
        in_specs += [pl.BlockSpec((1, nh, hp, ns), lambda b, c: (b, 0, 0, 0)),
                     pl.BlockSpec((1,) + conv_buf.shape[1:], lambda b, c: (b, 0, 0))]
        args += [s0, conv_buf]
    kern = functools.partial(_ssd_kernel, c_rows=c_rows, t_valid=t_valid, t_pad=t_pad, has_s0=has_s0)
    return pl.pallas_call(
        kern,
        grid=(bsz, nc),
        in_specs=in_specs,
        out_specs=[pl.BlockSpec((c_rows, di), lambda b, c: (b * nc + c, 0)),
                   pl.BlockSpec((1, nh, hp, ns), lambda b, c: (b, 0, 0, 0))],
        out_shape=[jax.ShapeDtypeStruct((bsz * t_pad, di), bf16),
                   jax.ShapeDtypeStruct((bsz, nh, hp, ns), f32)],
        scratch_shapes=[pltpu.VMEM((nh * hp, ns), f32), pltpu.VMEM((SUBLANES, conv_dim), f32),
                        pltpu.VMEM((c_rows, di), f32), pltpu.VMEM((c_rows, di), f32)],
        compiler_params=_cparams(2),
        name="ssd",
    )(*args)


def _pad_time(a, bsz, t, t_pad):
    if t_pad == t:
        return a
    n = a.shape[-1]
    return jnp.pad(a.reshape(bsz, t, n), ((0, 0), (0, t_pad - t), (0, 0))).reshape(bsz * t_pad, n)


def _unpad_time(a, bsz, t, t_pad):
    if t_pad == t:
        return a
    n = a.shape[-1]
    return a.reshape(bsz, t_pad, n)[:, :t].reshape(bsz * t, n)


def _trunk(x, c_k, c_v, s_hg, s_ssm, s_sconv, s_fconv, prm):
    bsz, t, d = x.shape
    depth = prm["norm_mix_w"].shape[0]
    m = bsz * t
    decode = c_k is not None
    t_pad = t if t % SUBLANES == 0 else (t // SUBLANES + 1) * SUBLANES
    n_heads = d // ATT_HEAD_DIM
    kvd = ATT_KV_HEADS * ATT_HEAD_DIM
    x2 = x.reshape(m, d)
    xn = rmsnorm_call(x2, prm["norm_mix_w"][0], bf16)
    nk, nv, nhg, nssm, nsconv, nfconv = [], [], [], [], [], []
    slot = [0] * N_MIXERS
    for i in range(depth):
        kind = i % N_MIXERS
        j = slot[kind]
        slot[kind] += 1
        nw_ffn = prm["norm_ffn_w"][i]
        if kind == 0:
            w_qkv, b_qkv = prm["attn_w_qkv"][j], prm["attn_b_qkv"][j]
            nq = w_qkv.shape[1]
            if decode:
                qkv = proj_call(xn, w_qkv, b_qkv, nq, 512, f32, "attn_qkv")
                kv_new = qkv[:, n_heads * ATT_HEAD_DIM:].reshape(bsz, t, 2, kvd)
                ck = c_k[j].reshape(bsz, WINDOW, kvd)
                cv = c_v[j].reshape(bsz, WINDOW, kvd)
                o = attn_cache_call(_pad_time(qkv, bsz, t, t_pad).reshape(bsz, t_pad, nq), ck, cv,
                                    prm["attn_sinks"][j], n_heads, t)
                o = o[:, :t].reshape(m, n_heads * ATT_HEAD_DIM)
                k_new = jnp.concatenate([ck, kv_new[:, :, 0]], axis=1)[:, t:]
                v_new = jnp.concatenate([cv, kv_new[:, :, 1]], axis=1)[:, t:]
            else:
                qkv = proj_call(xn, w_qkv, b_qkv, nq, 512, bf16, "attn_qkv")
                o = attn_prompt_call(qkv, prm["attn_sinks"][j], bsz, t, n_heads)
                xn_last = xn.reshape(bsz, t, d)[:, t - WINDOW:].reshape(bsz * WINDOW, d)
                kv_last = proj_call(xn_last, w_qkv[:, n_heads * ATT_HEAD_DIM:], b_qkv[n_heads * ATT_HEAD_DIM:],
                                    2 * kvd, 2 * kvd, f32, "attn_kv_tail").reshape(bsz, WINDOW, 2, kvd)
                k_new, v_new = kv_last[:, :, 0], kv_last[:, :, 1]
            nk.append(k_new.reshape(bsz, WINDOW, ATT_KV_HEADS, ATT_HEAD_DIM))
            nv.append(v_new.reshape(bsz, WINDOW, ATT_KV_HEADS, ATT_HEAD_DIM))
            x2, xn = outproj_call(o, prm["attn_w_o"][j], prm["attn_b_o"][j], x2, nw_ffn, "attn_out")
        elif kind == 1:
            w_in = prm["hgrn_w_in"][j]
            p = proj_call(xn, w_in, None, w_in.shape[1], 512, f32, "hgrn_in")
            y, s_new = hgrn_call(_pad_time(p, bsz, t, t_pad), prm["hgrn_lb_logits"], prm["hgrn_norm_w"][j],
                                 None if s_hg is None else s_hg[j], bsz, t_pad, t, i)
            nhg.append(s_new)
            x2, xn = outproj_call(_unpad_time(y, bsz, t, t_pad), prm["hgrn_w_o"][j], None, x2, nw_ffn, "hgrn_out")
        else:
            w_in = prm["ssd_w_in"][j]
            conv_dim = prm["ssd_conv_w"].shape[2]
            di = prm["ssd_w_o"].shape[1]
            nz = di + conv_dim
            zx = proj_call(xn, w_in, None, nz, 512, f32, "ssd_in")
            w_dt = jnp.pad(w_in[:, nz:], ((0, 0), (0, LANES - (w_in.shape[1] - nz))))
            dt_raw = proj_call(xn, w_dt, None, LANES, LANES, f32, "ssd_dt")
            y, s_new = ssd_call(_pad_time(zx, bsz, t, t_pad), _pad_time(dt_raw, bsz, t, t_pad),
                                prm["ssd_conv_w"][j], prm["ssd_conv_b"][j], prm["ssd_dt_bias"][j],
                                prm["ssd_a_log"][j], prm["ssd_d"][j], prm["ssd_norm_w"][j],
                                None if s_ssm is None else s_ssm[j],
                                None if s_sconv is None else s_sconv[j], bsz, t_pad, t)
            nssm.append(s_new)
            pre = zx[:, di:].reshape(bsz, t, conv_dim)
            kc = prm["ssd_conv_w"].shape[1]
            if s_sconv is not None:
                pre = jnp.concatenate([s_sconv[j], pre], axis=1)
            nsconv.append(pre[:, pre.shape[1] - (kc - 1):])
            x2, xn = outproj_call(_unpad_time(y, bsz, t, t_pad), prm["ssd_w_o"][j], None, x2, nw_ffn, "ssd_out")
        last = i == depth - 1
        next_w = prm["norm_final_w"] if last else prm["norm_mix_w"][i + 1]
        dff = prm["ffn_w_down"].shape[1]
        if s_fconv is not None:
            buf = s_fconv[i]
            zeros = jnp.zeros((bsz, t - 1, dff), f32)
            e1 = jnp.concatenate([buf[:, 1:2], zeros], axis=1).reshape(m, dff)
            e2 = jnp.concatenate([buf, zeros[:, 1:]], axis=1).reshape(m, dff)
        else:
            e1 = e2 = None
        x2, xn, tail = ffn_call(xn, prm["ffn_w_up"][i], prm["ffn_conv_w"][i], prm["ffn_conv_b"][i],
                                prm["ffn_w_down"][i], x2, next_w, t, e1, e2, f32 if last else bf16, "ffn")
        kf = prm["ffn_conv_w"].shape[1]
        if s_fconv is not None:
            a_full = tail.reshape(bsz, t, dff)
            nfconv.append(jnp.concatenate([s_fconv[i], a_full], axis=1)[:, t:])
        else:
            tiles_per_seq = t // tail_tile_rows(m)
            tl = tail.reshape(bsz, tiles_per_seq, SUBLANES, dff)[:, -1]
            nfconv.append(tl[:, SUBLANES - (kf - 1):])
    return (xn.reshape(bsz, t, d), jnp.stack(nk), jnp.stack(nv), jnp.stack(nhg),
            jnp.stack(nssm), jnp.stack(nsconv), jnp.stack(nfconv))


def tail_tile_rows(m):
    return min(1024, m)


def kernel(x_prompt, x_sample, cache_attn_k, cache_attn_v, state_hgrn, state_ssm, state_ssm_conv, state_ffn_conv, norm_mix_w, norm_ffn_w, norm_final_w, attn_w_qkv, attn_b_qkv, attn_sinks, attn_w_o, attn_b_o, hgrn_w_in, hgrn_lb_logits, hgrn_norm_w, hgrn_w_o, ssd_w_in, ssd_conv_w, ssd_conv_b, ssd_dt_bias, ssd_a_log, ssd_d, ssd_norm_w, ssd_w_o, ffn_w_up, ffn_conv_w, ffn_conv_b, ffn_w_down):
    prm = dict(norm_mix_w=norm_mix_w, norm_ffn_w=norm_ffn_w, norm_final_w=norm_final_w,
               attn_w_qkv=attn_w_qkv, attn_b_qkv=attn_b_qkv, attn_sinks=attn_sinks,
               attn_w_o=attn_w_o, attn_b_o=attn_b_o,
               hgrn_w_in=hgrn_w_in, hgrn_lb_logits=hgrn_lb_logits, hgrn_norm_w=hgrn_norm_w, hgrn_w_o=hgrn_w_o,
               ssd_w_in=ssd_w_in, ssd_conv_w=ssd_conv_w, ssd_conv_b=ssd_conv_b, ssd_dt_bias=ssd_dt_bias,
               ssd_a_log=ssd_a_log, ssd_d=ssd_d, ssd_norm_w=ssd_norm_w, ssd_w_o=ssd_w_o,
               ffn_w_up=ffn_w_up, ffn_conv_w=ffn_conv_w, ffn_conv_b=ffn_conv_b, ffn_w_down=ffn_w_down)
    outs_p = _trunk(x_prompt, None, None, None, None, None, None, prm)
    outs_s = _trunk(x_sample, cache_attn_k, cache_attn_v, state_hgrn, state_ssm, state_ssm_conv,
                    state_ffn_conv, prm)
    return (outs_p[0], outs_s[0]) + outs_p[1:] + outs_s[1:]
```

```python
import functools

import jax
import jax.numpy as jnp
from jax import lax
from jax.experimental import pallas as pl
from jax.experimental.pallas import tpu as pltpu

bf16 = jnp.bfloat16
f32 = jnp.float32

RMS_EPS = 1e-6
N_MIXERS = 3
WINDOW = 128
PAST_LEN = 8192
ATT_HEAD_DIM = 64
ATT_KV_HEADS = 4
HG_HEADS = 8
HG_CHUNK_GROUP = 32
SSD_HEAD_DIM = 64
SSD_STATE = 128
SSD_GROUPS = 4

LANES = 128
SUBLANES = 8
VMEM_LIMIT = 56 * 1024 * 1024
PROJ_TM = 512
FFN_TM = 512


def _cparams(n_axes):
    return pltpu.CompilerParams(dimension_semantics=("arbitrary",) * n_axes,
                                vmem_limit_bytes=VMEM_LIMIT)


def _dot(a, b):
    return jnp.dot(a.astype(bf16), b.astype(bf16), preferred_element_type=f32)


def _dot_nt(a, b):
    return lax.dot_general(a.astype(bf16), b.astype(bf16), (((1,), (1,)), ((), ())),
                           preferred_element_type=f32)


def _dot_tn(a, b):
    return lax.dot_general(a.astype(bf16), b.astype(bf16), (((0,), (0,)), ((), ())),
                           preferred_element_type=f32)


def _silu(x):
    return x * jax.nn.sigmoid(x)


def _rms(x, w):
    ms = jnp.mean(x * x, axis=-1, keepdims=True)
    return x * lax.rsqrt(ms + RMS_EPS) * w


def _cumsum_rows(x, tril):
    hi = x.astype(bf16)
    r1 = x - hi.astype(f32)
    mid = r1.astype(bf16)
    lo = (r1 - mid.astype(f32)).astype(bf16)
    d = functools.partial(jnp.dot, preferred_element_type=f32)
    return d(tril, hi) + d(tril, mid) + d(tril, lo)


def _tril_mask(c):
    r = lax.broadcasted_iota(jnp.int32, (c, c), 0)
    s = lax.broadcasted_iota(jnp.int32, (c, c), 1)
    return r >= s


def _norm_kernel(x_ref, w_ref, o_ref):
    o_ref[...] = _rms(x_ref[...], w_ref[...]).astype(o_ref.dtype)


def rmsnorm_call(x, w, out_dtype):
    m, d = x.shape
    tm = min(1024, m)
    return pl.pallas_call(
        _norm_kernel,
        grid=(m // tm,),
        in_specs=[pl.BlockSpec((tm, d), lambda i: (i, 0)), pl.BlockSpec((1, d), lambda i: (0, 0))],
        out_specs=pl.BlockSpec((tm, d), lambda i: (i, 0)),
        out_shape=jax.ShapeDtypeStruct((m, d), out_dtype),
        compiler_params=_cparams(1),
        name="rmsnorm",
    )(x, w.reshape(1, d))


def _resident(shape):
    return pl.BlockSpec(shape, lambda *_: (0,) * len(shape), pipeline_mode=pl.Buffered(1))


def _proj_kernel(x_ref, w_ref, b_ref, o_ref):
    acc = jnp.dot(x_ref[...], w_ref[...], preferred_element_type=f32)
    o_ref[...] = (acc + b_ref[...]).astype(o_ref.dtype)


def proj_call(xn, w, bias, out_dtype, name):
    m, k = xn.shape
    n = w.shape[1]
    tm = min(PROJ_TM, m)
    if bias is None:
        bias = jnp.zeros((n,), f32)
    return pl.pallas_call(
        _proj_kernel,
        grid=(m // tm,),
        in_specs=[pl.BlockSpec((tm, k), lambda i: (i, 0)), _resident((k, n)), _resident((1, n))],
        out_specs=pl.BlockSpec((tm, n), lambda i: (i, 0)),
        out_shape=jax.ShapeDtypeStruct((m, n), out_dtype),
        compiler_params=_cparams(1),
        name=name,
    )(xn, w, bias.reshape(1, n))


def _outproj_kernel(a_ref, w_ref, b_ref, x_ref, nw_ref, xo_ref, xn_ref):
    y = jnp.dot(a_ref[...].astype(bf16), w_ref[...], preferred_element_type=f32) + b_ref[...] + x_ref[...]
    xo_ref[...] = y
    xn_ref[...] = _rms(y, nw_ref[...]).astype(xn_ref.dtype)


def outproj_call(act, w, bias, x, next_w, name):
    m, k = act.shape
    d = w.shape[1]
    tm = min(PROJ_TM, m)
    if bias is None:
        bias = jnp.zeros((d,), f32)
    return pl.pallas_call(
        _outproj_kernel,
        grid=(m // tm,),
        in_specs=[pl.BlockSpec((tm, k), lambda i: (i, 0)), _resident((k, d)), _resident((1, d)),
                  pl.BlockSpec((tm, d), lambda i: (i, 0)), _resident((1, d))],
        out_specs=[pl.BlockSpec((tm, d), lambda i: (i, 0)), pl.BlockSpec((tm, d), lambda i: (i, 0))],
        out_shape=[jax.ShapeDtypeStruct((m, d), f32), jax.ShapeDtypeStruct((m, d), bf16)],
        compiler_params=_cparams(1),
        name=name,
    )(act, w, bias.reshape(1, d), x, next_w.reshape(1, d))


def _ffn_kernel(*refs, tm, tf, t_seq, tail, has_state):
    if has_state:
        (xn_ref, wu_ref, cw_ref, cb_ref, wd_ref, x_ref, nw_ref, e1_ref, e2_ref,
         xo_ref, xn2_ref, tail_ref, carry_ref) = refs
    else:
        (xn_ref, wu_ref, cw_ref, cb_ref, wd_ref, x_ref, nw_ref,
         xo_ref, xn2_ref, tail_ref, carry_ref) = refs
    i = pl.program_id(0)
    dff = wd_ref.shape[0]

    @pl.when(i == 0)
    def _():
        carry_ref[...] = jnp.zeros_like(carry_ref)

    xn = xn_ref[...]
    rows = lax.broadcasted_iota(jnp.int32, (tm, 1), 0)
    tpos = (i * tm + rows) & (t_seq - 1)
    gs = []
    for c0 in range(0, dff, tf):
        cs = slice(c0, c0 + tf)
        a = jnp.dot(xn, wu_ref[:, cs], preferred_element_type=f32)
        b = jnp.dot(xn, wu_ref[:, dff + c0:dff + c0 + tf], preferred_element_type=f32)
        tail_ref[0, :, cs] = a[tm - tail:, :]
        carry = carry_ref[:, cs]
        p1 = jnp.where(rows == 0, carry[7:8, :], pltpu.roll(a, 1, 0))
        p2 = jnp.where(rows == 0, carry[6:7, :], jnp.where(rows == 1, carry[7:8, :], pltpu.roll(a, 2, 0)))
        if has_state:
            p1 = jnp.where(tpos == 0, e1_ref[:, cs], p1)
            p2 = jnp.where(tpos < 2, e2_ref[:, cs], p2)
        else:
            p1 = jnp.where(tpos == 0, 0.0, p1)
            p2 = jnp.where(tpos < 2, 0.0, p2)
        carry_ref[:, cs] = a[tm - SUBLANES:, :]
        y = cb_ref[:, cs] + p2 * cw_ref[0:1, cs]
        y = y + p1 * cw_ref[1:2, cs]
        y = y + a * cw_ref[2:3, cs]
        gs.append((_silu(y) * b).astype(bf16))
    g = jnp.concatenate(gs, axis=1)
    xo = jnp.dot(g, wd_ref[...], preferred_element_type=f32) + x_ref[...]
    xo_ref[...] = xo
    xn2_ref[...] = _rms(xo, nw_ref[...]).astype(xn2_ref.dtype)


def ffn_call(xn, w_up, conv_w, conv_b, w_down, x, next_w, t_seq, e1, e2, out_dtype, name):
    m, d = xn.shape
    dff = w_down.shape[0]
    tf = 2 * LANES
    assert dff % tf == 0 and (t_seq & (t_seq - 1)) == 0
    has_state = e1 is not None
    tm = min(FFN_TM, m)
    assert tm % t_seq == 0 or t_seq % tm == 0
    tail = tm if has_state else SUBLANES
    row = lambda w: pl.BlockSpec((tm, w), lambda i: (i, 0))
    in_specs = [row(d), _resident((d, 2 * dff)), _resident((3, dff)), _resident((1, dff)),
                _resident((dff, d)), row(d), _resident((1, d))]
    args = [xn, w_up, conv_w, conv_b.reshape(1, dff), w_down, x, next_w.reshape(1, d)]
    if has_state:
        in_specs += [row(dff), row(dff)]
        args += [e1, e2]
    kern = functools.partial(_ffn_kernel, tm=tm, tf=tf, t_seq=t_seq, tail=tail, has_state=has_state)
    return pl.pallas_call(
        kern,
        grid=(m // tm,),
        in_specs=in_specs,
        out_specs=[row(d), row(d), pl.BlockSpec((1, tail, dff), lambda i: (i, 0, 0))],
        out_shape=[jax.ShapeDtypeStruct((m, d), f32),
                   jax.ShapeDtypeStruct((m, d), out_dtype),
                   jax.ShapeDtypeStruct((m // tm, tail, dff), f32)],
        scratch_shapes=[pltpu.VMEM((SUBLANES, dff), f32)],
        compiler_params=_cparams(1),
        name=name,
    )(*args)


def _attn_softmax_pv(s, sink_col, vj):
    m = jnp.maximum(jnp.max(s, axis=-1, keepdims=True), sink_col)
    p = jnp.exp(s - m)
    l = jnp.sum(p, axis=-1, keepdims=True) + jnp.exp(sink_col - m)
    return _dot(p, vj) / l


def _sink_col(sinks_ref, j, group, rows_per_head):
    r = lax.broadcasted_iota(jnp.int32, (group * rows_per_head, 1), 0)
    col = jnp.full((group * rows_per_head, 1), sinks_ref[j * group], f32)
    for p in range(1, group):
        col = jnp.where(r >= p * rows_per_head, sinks_ref[j * group + p], col)
    return col


def _attn_prompt_kernel(sinks_ref, q_ref, kvc_ref, kvp_ref, o_ref, *, n_heads):
    n = pl.program_id(1)
    w, hd, kvh = WINDOW, ATT_HEAD_DIM, ATT_KV_HEADS
    group = n_heads // kvh
    q = q_ref[...] * (hd ** -0.5)
    kvc = kvc_ref[...]
    kvp = kvp_ref[...]
    kv = jnp.concatenate([kvp, kvc], axis=0)
    i = lax.broadcasted_iota(jnp.int32, (w, 2 * w), 0)
    c = lax.broadcasted_iota(jnp.int32, (w, 2 * w), 1)
    band = (c > i) & (c <= i + w) & ((n > 0) | (c >= w))
    mask = jnp.concatenate([band] * group, axis=0)
    for j in range(kvh):
        kj = kv[:, j * hd:(j + 1) * hd]
        vj = kv[:, (kvh + j) * hd:(kvh + j + 1) * hd]
        qs = jnp.concatenate([q[:, (j * group + p) * hd:(j * group + p + 1) * hd] for p in range(group)], axis=0)
        s = jnp.where(mask, _dot_nt(qs, kj), -jnp.inf)
        o = _attn_softmax_pv(s, _sink_col(sinks_ref, j, group, w), vj)
        for p in range(group):
            h = j * group + p
            o_ref[:, h * hd:(h + 1) * hd] = o[p * w:(p + 1) * w, :].astype(o_ref.dtype)


def attn_prompt_call(qkv, sinks, bsz, t, n_heads):
    w, hd, kvh = WINDOW, ATT_HEAD_DIM, ATT_KV_HEADS
    nb = t // w
    qd = n_heads * hd
    kvd = 2 * kvh * hd
    assert qd % kvd == 0
    kvblk = qd // kvd
    kern = functools.partial(_attn_prompt_kernel, n_heads=n_heads)
    return pl.pallas_call(
        kern,
        grid=(bsz, nb),
        in_specs=[pl.BlockSpec(memory_space=pltpu.SMEM),
                  pl.BlockSpec((w, qd), lambda b, n: (b * nb + n, 0)),
                  pl.BlockSpec((w, kvd), lambda b, n: (b * nb + n, kvblk)),
                  pl.BlockSpec((w, kvd), lambda b, n: (b * nb + jnp.maximum(n - 1, 0), kvblk))],
        out_specs=pl.BlockSpec((w, qd), lambda b, n: (b * nb + n, 0)),
        out_shape=jax.ShapeDtypeStruct((bsz * t, qd), bf16),
        compiler_params=_cparams(2),
        name="attn_prompt",
    )(sinks, qkv, qkv, qkv)


def _attn_cache_kernel(sinks_ref, q_ref, ck_ref, cv_ref, o_ref, *, n_heads, t_valid, bt):
    w, hd, kvh = WINDOW, ATT_HEAD_DIM, ATT_KV_HEADS
    group = n_heads // kvh
    tp = q_ref.shape[1]
    qd = n_heads * hd
    tq = lax.broadcasted_iota(jnp.int32, (tp, w), 0)
    jc = lax.broadcasted_iota(jnp.int32, (tp, w), 1)
    mc = (jc > tq) & (jc - w + PAST_LEN >= 0)
    tq2 = lax.broadcasted_iota(jnp.int32, (tp, tp), 0)
    un = lax.broadcasted_iota(jnp.int32, (tp, tp), 1)
    mn = (un <= tq2) & (un < t_valid)
    mask_c = jnp.concatenate([mc] * group, axis=0)
    mask_n = jnp.concatenate([mn] * group, axis=0)

    def body(bi, carry):
        row = q_ref[bi]
        q = row[:, :qd] * (hd ** -0.5)
        ck = ck_ref[bi]
        cv = cv_ref[bi]
        for j in range(kvh):
            kc = ck[:, j * hd:(j + 1) * hd]
            vc = cv[:, j * hd:(j + 1) * hd]
            kn = row[:, qd + j * hd:qd + (j + 1) * hd]
            vn = row[:, qd + (kvh + j) * hd:qd + (kvh + j + 1) * hd]
            qs = jnp.concatenate([q[:, (j * group + p) * hd:(j * group + p + 1) * hd] for p in range(group)], axis=0)
            sc = jnp.where(mask_c, _dot_nt(qs, kc), -jnp.inf)
            sn = jnp.where(mask_n, _dot_nt(qs, kn), -jnp.inf)
            sink = _sink_col(sinks_ref, j, group, tp)
            m = jnp.maximum(jnp.maximum(jnp.max(sc, axis=-1, keepdims=True),
                                        jnp.max(sn, axis=-1, keepdims=True)), sink)
            pc = jnp.exp(sc - m)
            pn = jnp.exp(sn - m)
            l = jnp.sum(pc, axis=-1, keepdims=True) + jnp.sum(pn, axis=-1, keepdims=True) + jnp.exp(sink - m)
            o = (_dot(pc, vc) + _dot(pn, vn)) / l
            for p in range(group):
                h = j * group + p
                o_ref[bi, :, h * hd:(h + 1) * hd] = o[p * tp:(p + 1) * tp, :].astype(o_ref.dtype)
        return carry

    lax.fori_loop(0, bt, body, 0)


def attn_cache_call(qkv, cache_k, cache_v, sinks, n_heads, t_valid):
    bsz, tp, nq = qkv.shape
    w = WINDOW
    kd = cache_k.shape[-1]
    qd = n_heads * ATT_HEAD_DIM
    bt = 8
    kern = functools.partial(_attn_cache_kernel, n_heads=n_heads, t_valid=t_valid, bt=bt)
    return pl.pallas_call(
        kern,
        grid=(bsz // bt,),
        in_specs=[pl.BlockSpec(memory_space=pltpu.SMEM),
                  pl.BlockSpec((bt, tp, nq), lambda b: (b, 0, 0)),
                  pl.BlockSpec((bt, w, kd), lambda b: (b, 0, 0)),
                  pl.BlockSpec((bt, w, kd), lambda b: (b, 0, 0))],
        out_specs=pl.BlockSpec((bt, tp, qd), lambda b: (b, 0, 0)),
        out_shape=jax.ShapeDtypeStruct((bsz, tp, qd), f32),
        compiler_params=_cparams(1),
        name="attn_cache",
    )(sinks, qkv, cache_k, cache_v)


def _hgrn_kernel(*refs, c_rows, grp, t_valid, t_pad, layer, has_s0):
    if has_s0:
        p_ref, lbl_ref, nw_ref, s0_ref, y_ref, so_ref, st_ref = refs
    else:
        p_ref, lbl_ref, nw_ref, y_ref, so_ref, st_ref = refs
    c = pl.program_id(1)
    nc = pl.num_programs(1)
    nh = HG_HEADS
    dk = p_ref.shape[1] // 4 // nh
    dim = nh * dk
    ng = c_rows // grp

    @pl.when(c == 0)
    def _():
        if has_s0:
            for h in range(nh):
                st_ref[h] = s0_ref[0, h].T
        else:
            st_ref[...] = jnp.zeros_like(st_ref)

    lbl = lbl_ref[...]
    e = jnp.exp(lbl - jnp.max(lbl, axis=0, keepdims=True))
    sm = e / jnp.sum(e, axis=0, keepdims=True)
    lb = jnp.zeros((1, dim), f32)
    for r in range(1, layer + 1):
        lb = lb + sm[r:r + 1, :]

    x = p_ref[...]
    q = _silu(x[:, 0:dim])
    fg = lb + (1.0 - lb) * jax.nn.sigmoid(x[:, dim:2 * dim])
    logf = jnp.log(fg)
    k = 1.0 - fg
    v = x[:, 2 * dim:3 * dim]
    gate = x[:, 3 * dim:4 * dim]
    if t_valid < t_pad:
        valid = (c * c_rows + lax.broadcasted_iota(jnp.int32, (c_rows, 1), 0)) < t_valid
        logf = jnp.where(valid, logf, 0.0)
        k = jnp.where(valid, k, 0.0)

    tril = _tril_mask(c_rows)
    bcum = _cumsum_rows(logf, tril.astype(bf16))
    rr = lax.broadcasted_iota(jnp.int32, (c_rows, c_rows), 0)
    ss = lax.broadcasted_iota(jnp.int32, (c_rows, c_rows), 1)
    gshift = grp.bit_length() - 1
    diag_mask = tril & ((rr >> gshift) == (ss >> gshift))
    nw = nw_ref[...]

    def rows_of(vals):
        return jnp.concatenate([jnp.broadcast_to(r, (grp, dk)) for r in vals], axis=0)

    for h in range(nh):
        sl = slice(h * dk, (h + 1) * dk)
        b = bcum[:, sl]
        qh, kh, vh = q[:, sl], k[:, sl], v[:, sl]
        zero = jnp.zeros((1, dk), f32)
        r = [zero] + [b[i * grp - 1:i * grp, :] for i in range(1, ng)] + [b[c_rows - 1:c_rows, :]]
        mid = [b[i * grp + grp // 2 - 1:i * grp + grp // 2, :] for i in range(ng)]
        b_last = r[ng]
        r_start = rows_of(r[:ng])
        r_end = rows_of(r[1:])
        r_mid = rows_of(mid)
        qg = qh * jnp.exp(b - r_start)
        kt = kh * jnp.exp(r_end - b)
        qm = qh * jnp.exp(b - r_mid)
        km = kh * jnp.exp(r_mid - b)
        a = jnp.where(diag_mask, _dot_nt(qm, km), 0.0)
        if ng > 1:
            lhs, rhs = [], []
            for j in range(ng - 1):
                lhs.append(jnp.concatenate(
                    [jnp.zeros((grp, dk), f32) if i <= j else qg[i * grp:(i + 1) * grp, :] * jnp.exp(r[i] - r[j + 1])
                     for i in range(ng)], axis=0))
                rhs.append(jnp.concatenate(
                    [kt[i * grp:(i + 1) * grp, :] if i == j else jnp.zeros((grp, dk), f32) for i in range(ng)], axis=0))
            a = a + _dot_nt(jnp.concatenate(lhs, axis=1), jnp.concatenate(rhs, axis=1))
        q_inter = qg * jnp.exp(r_start)
        st = st_ref[h]
        o = _dot(a, vh) + _dot_nt(q_inter, st)
        k_dec = kt * jnp.exp(b_last - r_end)
        st_new = st * jnp.exp(b_last) + _dot_tn(vh, k_dec)
        st_ref[h] = st_new
        on = o * lax.rsqrt(jnp.mean(o * o, axis=-1, keepdims=True) + RMS_EPS)
        y_ref[:, sl] = (on * nw[:, sl] * _silu(gate[:, sl])).astype(y_ref.dtype)

        @pl.when(c == nc - 1)
        def _():
            so_ref[0, h] = st_new.T


def hgrn_call(p, lb_logits, norm_w, s0, bsz, t_pad, t_valid, layer):
    dim = p.shape[1] // 4
    nh = HG_HEADS
    dk = dim // nh
    c_rows = min(128, t_pad)
    grp = min(HG_CHUNK_GROUP, c_rows)
    nc = t_pad // c_rows
    has_s0 = s0 is not None
    in_specs = [pl.BlockSpec((c_rows, 4 * dim), lambda b, c: (b * nc + c, 0)),
                pl.BlockSpec(lb_logits.shape, lambda b, c: (0, 0)),
                pl.BlockSpec((1, dim), lambda b, c: (0, 0))]
    args = [p, lb_logits, norm_w.reshape(1, dim)]
    if has_s0:
        in_specs.append(pl.BlockSpec((1, nh, dk, dk), lambda b, c: (b, 0, 0, 0)))
        args.append(s0)
    kern = functools.partial(_hgrn_kernel, c_rows=c_rows, grp=grp, t_valid=t_valid, t_pad=t_pad,
                             layer=layer, has_s0=has_s0)
    return pl.pallas_call(
        kern,
        grid=(bsz, nc),
        in_specs=in_specs,
        out_specs=[pl.BlockSpec((c_rows, dim), lambda b, c: (b * nc + c, 0)),
                   pl.BlockSpec((1, nh, dk, dk), lambda b, c: (b, 0, 0, 0))],
        out_shape=[jax.ShapeDtypeStruct((bsz * t_pad, dim), bf16),
                   jax.ShapeDtypeStruct((bsz, nh, dk, dk), f32)],
        scratch_shapes=[pltpu.VMEM((nh, dk, dk), f32)],
        compiler_params=_cparams(2),
        name="hgrn",
    )(*args)


def _ssd_kernel(*refs, c_rows, t_valid, t_pad, has_s0):
    if has_s0:
        (zx_ref, cw_ref, cb_ref, dtb_ref, alog_ref, dsk_ref, nw_ref, s0_ref, buf_ref,
         y_ref, so_ref, st_ref, carry_ref, yacc_ref, xd_ref) = refs
    else:
        (zx_ref, cw_ref, cb_ref, dtb_ref, alog_ref, dsk_ref, nw_ref,
         y_ref, so_ref, st_ref, carry_ref, yacc_ref, xd_ref) = refs
    c = pl.program_id(1)
    nc = pl.num_programs(1)
    hp, ns, ngr = SSD_HEAD_DIM, SSD_STATE, SSD_GROUPS
    di = y_ref.shape[1]
    nh = di // hp
    hpg = nh // ngr
    gw = ngr * ns
    kconv = cw_ref.shape[0]

    @pl.when(c == 0)
    def _():
        if has_s0:
            for h in range(nh):
                st_ref[h * hp:(h + 1) * hp, :] = s0_ref[0, h]
            carry_ref[...] = jnp.zeros_like(carry_ref)
            carry_ref[SUBLANES - (kconv - 1):, :] = buf_ref[0]
        else:
            st_ref[...] = jnp.zeros_like(st_ref)
            carry_ref[...] = jnp.zeros_like(carry_ref)

    zx = zx_ref[...]
    z = zx[:, :di]
    cdim = cw_ref.shape[1]
    u = zx[:, di:di + cdim]
    rows = lax.broadcasted_iota(jnp.int32, (c_rows, 1), 0)
    cr = carry_ref[...]
    cw = cw_ref[...]
    yc = cb_ref[...]
    for kk in range(kconv - 1):
        back = kconv - 1 - kk
        sh = pltpu.roll(u, back, 0)
        for r0 in range(back):
            sh = jnp.where(rows == r0, cr[SUBLANES - back + r0:SUBLANES - back + r0 + 1, :], sh)
        yc = yc + sh * cw[kk:kk + 1, :]
    yc = yc + u * cw[kconv - 1:kconv, :]
    carry_ref[...] = u[c_rows - SUBLANES:, :]
    xbc = _silu(yc)
    xs = xbc[:, :di]
    bm = xbc[:, di:di + gw]
    cm = xbc[:, di + gw:di + 2 * gw]

    dt = jax.nn.softplus(zx[:, di + cdim:] + dtb_ref[...])
    if t_valid < t_pad:
        valid = (c * c_rows + rows) < t_valid
        dt = jnp.where(valid, dt, 0.0)
    a_neg = -jnp.exp(alog_ref[...])
    tril = _tril_mask(c_rows)
    acum = _cumsum_rows(dt * a_neg, tril.astype(bf16))
    if c_rows < LANES:
        acum_sq = jnp.concatenate([acum, jnp.zeros((LANES - c_rows, LANES), f32)], axis=0)
    else:
        acum_sq = acum
    acum_t = acum_sq.T[:, :c_rows]
    a_last = acum[c_rows - 1:c_rows, :]
    dsk = dsk_ref[...]

    for g in range(ngr):
        bg = bm[:, g * ns:(g + 1) * ns]
        cg = cm[:, g * ns:(g + 1) * ns]
        cb_mat = _dot_nt(cg, bg)
        gsl = slice(g * hpg * hp, (g + 1) * hpg * hp)
        y_inter = _dot_nt(cg, st_ref[gsl, :])
        for pp in range(hpg):
            h = g * hpg + pp
            hsl = slice(h * hp, (h + 1) * hp)
            col = acum[:, h:h + 1]
            row = acum_t[h:h + 1, :]
            lmat = jnp.exp(jnp.where(tril, col - row, -jnp.inf))
            xh = xs[:, hsl]
            xdt = xh * dt[:, h:h + 1]
            yh = _dot(cb_mat * lmat, xdt) + y_inter[:, pp * hp:(pp + 1) * hp] * jnp.exp(col)
            yacc_ref[:, hsl] = yh + dsk[:, hsl] * xh
            al = a_last[:, h:h + 1]
            xd_ref[:, hsl] = xdt * jnp.exp(al - col)
            st_ref[hsl, :] = st_ref[hsl, :] * jnp.exp(al)
        st_ref[gsl, :] = st_ref[gsl, :] + _dot_tn(xd_ref[:, gsl], bg)

    y = yacc_ref[...] * _silu(z)
    nw = nw_ref[...]
    gdim = di // ngr
    for g in range(ngr):
        sl = slice(g * gdim, (g + 1) * gdim)
        yg = y[:, sl]
        y_ref[:, sl] = (yg * lax.rsqrt(jnp.mean(yg * yg, axis=-1, keepdims=True) + RMS_EPS) * nw[:, sl]).astype(y_ref.dtype)

    @pl.when(c == nc - 1)
    def _():
        for h in range(nh):
            so_ref[0, h] = st_ref[h * hp:(h + 1) * hp, :]


def ssd_call(zx, conv_w, conv_b, dt_bias, a_log, d_skip, norm_w, s0, conv_buf, bsz, t_pad, t_valid):
    hp, ns = SSD_HEAD_DIM, SSD_STATE
    conv_dim = conv_w.shape[1]
    di = zx.shape[1] - conv_dim - LANES
    nh = di // hp
    c_rows = min(128, t_pad)
    nc = t_pad // c_rows
    has_s0 = s0 is not None
    pad_h = lambda a: jnp.pad(a.reshape(1, nh), ((0, 0), (0, LANES - nh)))
    full = lambda shape: pl.BlockSpec(shape, lambda b, c: (0,) * len(shape))
    in_specs = [pl.BlockSpec((c_rows, zx.shape[1]), lambda b, c: (b * nc + c, 0)),
                full(conv_w.shape), full((1, conv_dim)), full((1, LANES)), full((1, LANES)),
                full((1, di)), full((1, di))]
    args = [zx, conv_w, conv_b.reshape(1, conv_dim), pad_h(dt_bias), pad_h(a_log),
            jnp.repeat(d_skip, hp).reshape(1, di), norm_w.reshape(1, di)]
    if has_s0:
        in_specs += [pl.BlockSpec((1, nh, hp, ns), lambda b, c: (b, 0, 0, 0)),
                     pl.BlockSpec((1,) + conv_buf.shape[1:], lambda b, c: (b, 0, 0))]
        args += [s0, conv_buf]
    kern = functools.partial(_ssd_kernel, c_rows=c_rows, t_valid=t_valid, t_pad=t_pad, has_s0=has_s0)
    return pl.pallas_call(
        kern,
        grid=(bsz, nc),
        in_specs=in_specs,
        out_specs=[pl.BlockSpec((c_rows, di), lambda b, c: (b * nc + c, 0)),
                   pl.BlockSpec((1, nh, hp, ns), lambda b, c: (b, 0, 0, 0))],
        out_shape=[jax.ShapeDtypeStruct((bsz * t_pad, di), bf16),
                   jax.ShapeDtypeStruct((bsz, nh, hp, ns), f32)],
        scratch_shapes=[pltpu.VMEM((nh * hp, ns), f32), pltpu.VMEM((SUBLANES, conv_dim), f32),
                        pltpu.VMEM((c_rows, di), f32), pltpu.VMEM((c_rows, di), f32)],
        compiler_params=_cparams(2),
        name="ssd",
    )(*args)


def _pad_time(a, bsz, t, t_pad):
    if t_pad == t:
        return a
    n = a.shape[-1]
    return jnp.pad(a.reshape(bsz, t, n), ((0, 0), (0, t_pad - t), (0, 0))).reshape(bsz * t_pad, n)


def _unpad_time(a, bsz, t, t_pad):
    if t_pad == t:
        return a
    n = a.shape[-1]
    return a.reshape(bsz, t_pad, n)[:, :t].reshape(bsz * t, n)


def _trunk(x, c_k, c_v, s_hg, s_ssm, s_sconv, s_fconv, prm):
    bsz, t, d = x.shape
    depth = prm["norm_mix_w"].shape[0]
    m = bsz * t
    decode = c_k is not None
    t_pad = t if t % SUBLANES == 0 else (t // SUBLANES + 1) * SUBLANES
    n_heads = d // ATT_HEAD_DIM
    kvd = ATT_KV_HEADS * ATT_HEAD_DIM
    x2 = x.reshape(m, d)
    xn = rmsnorm_call(x2, prm["norm_mix_w"][0], bf16)
    nk, nv, nhg, nssm, nsconv, nfconv = [], [], [], [], [], []
    slot = [0] * N_MIXERS
    for i in range(depth):
        kind = i % N_MIXERS
        j = slot[kind]
        slot[kind] += 1
        nw_ffn = prm["norm_ffn_w"][i]
        if kind == 0:
            w_qkv, b_qkv = prm["attn_w_qkv"][j], prm["attn_b_qkv"][j]
            nq = w_qkv.shape[1]
            if decode:
                qkv = proj_call(xn, w_qkv, b_qkv, f32, "attn_qkv")
                kv_new = qkv[:, n_heads * ATT_HEAD_DIM:].reshape(bsz, t, 2, kvd)
                ck = c_k[j].reshape(bsz, WINDOW, kvd)
                cv = c_v[j].reshape(bsz, WINDOW, kvd)
                o = attn_cache_call(_pad_time(qkv, bsz, t, t_pad).reshape(bsz, t_pad, nq), ck, cv,
                                    prm["attn_sinks"][j], n_heads, t)
                o = o[:, :t].reshape(m, n_heads * ATT_HEAD_DIM)
                k_new = jnp.concatenate([ck, kv_new[:, :, 0]], axis=1)[:, t:]
                v_new = jnp.concatenate([cv, kv_new[:, :, 1]], axis=1)[:, t:]
            else:
                qkv = proj_call(xn, w_qkv, b_qkv, bf16, "attn_qkv")
                o = attn_prompt_call(qkv, prm["attn_sinks"][j], bsz, t, n_heads)
                xn_last = xn.reshape(bsz, t, d)[:, t - WINDOW:].reshape(bsz * WINDOW, d)
                kv_last = proj_call(xn_last, w_qkv[:, n_heads * ATT_HEAD_DIM:], b_qkv[n_heads * ATT_HEAD_DIM:],
                                    f32, "attn_kv_tail").reshape(bsz, WINDOW, 2, kvd)
                k_new, v_new = kv_last[:, :, 0], kv_last[:, :, 1]
            nk.append(k_new.reshape(bsz, WINDOW, ATT_KV_HEADS, ATT_HEAD_DIM))
            nv.append(v_new.reshape(bsz, WINDOW, ATT_KV_HEADS, ATT_HEAD_DIM))
            x2, xn = outproj_call(o, prm["attn_w_o"][j], prm["attn_b_o"][j], x2, nw_ffn, "attn_out")
        elif kind == 1:
            w_in = prm["hgrn_w_in"][j]
            p = proj_call(xn, w_in, None, f32, "hgrn_in")
            y, s_new = hgrn_call(_pad_time(p, bsz, t, t_pad), prm["hgrn_lb_logits"], prm["hgrn_norm_w"][j],
                                 None if s_hg is None else s_hg[j], bsz, t_pad, t, i)
            nhg.append(s_new)
            x2, xn = outproj_call(_unpad_time(y, bsz, t, t_pad), prm["hgrn_w_o"][j], None, x2, nw_ffn, "hgrn_out")
        else:
            conv_dim = prm["ssd_conv_w"].shape[2]
            di = prm["ssd_w_o"].shape[1]
            zx = proj_call(xn, prm["ssd_w_in"][j], None, f32, "ssd_in")
            y, s_new = ssd_call(_pad_time(zx, bsz, t, t_pad),
                                prm["ssd_conv_w"][j], prm["ssd_conv_b"][j], prm["ssd_dt_bias"][j],
                                prm["ssd_a_log"][j], prm["ssd_d"][j], prm["ssd_norm_w"][j],
                                None if s_ssm is None else s_ssm[j],
                                None if s_sconv is None else s_sconv[j], bsz, t_pad, t)
            nssm.append(s_new)
            kc = prm["ssd_conv_w"].shape[1]
            pre = zx.reshape(bsz, t, zx.shape[1])[:, max(t - (kc - 1), 0):, di:di + conv_dim]
            if s_sconv is not None:
                pre = jnp.concatenate([s_sconv[j], pre], axis=1)
            nsconv.append(pre[:, pre.shape[1] - (kc - 1):])
            x2, xn = outproj_call(_unpad_time(y, bsz, t, t_pad), prm["ssd_w_o"][j], None, x2, nw_ffn, "ssd_out")
        last = i == depth - 1
        next_w = prm["norm_final_w"] if last else prm["norm_mix_w"][i + 1]
        dff = prm["ffn_w_down"].shape[1]
        if s_fconv is not None:
            buf = s_fconv[i]
            zeros = jnp.zeros((bsz, t - 1, dff), f32)
            e1 = jnp.concatenate([buf[:, 1:2], zeros], axis=1).reshape(m, dff)
            e2 = jnp.concatenate([buf, zeros[:, 1:]], axis=1).reshape(m, dff)
        else:
            e1 = e2 = None
        x2, xn, tail = ffn_call(xn, prm["ffn_w_up"][i], prm["ffn_conv_w"][i], prm["ffn_conv_b"][i],
                                prm["ffn_w_down"][i], x2, next_w, t, e1, e2, f32 if last else bf16, "ffn")
        kf = prm["ffn_conv_w"].shape[1]
        if s_fconv is not None:
            a_full = tail.reshape(bsz, t, dff)
            nfconv.append(jnp.concatenate([s_fconv[i], a_full], axis=1)[:, t:])
        else:
            tiles_per_seq = t // tail_tile_rows(m)
            tl = tail.reshape(bsz, tiles_per_seq, SUBLANES, dff)[:, -1]
            nfconv.append(tl[:, SUBLANES - (kf - 1):])
    return (xn.reshape(bsz, t, d), jnp.stack(nk), jnp.stack(nv), jnp.stack(nhg),
            jnp.stack(nssm), jnp.stack(nsconv), jnp.stack(nfconv))


def tail_tile_rows(m):
    return min(FFN_TM, m)


def kernel(x_prompt, x_sample, cache_attn_k, cache_attn_v, state_hgrn, state_ssm, state_ssm_conv, state_ffn_conv, norm_mix_w, norm_ffn_w, norm_final_w, attn_w_qkv, attn_b_qkv, attn_sinks, attn_w_o, attn_b_o, hgrn_w_in, hgrn_lb_logits, hgrn_norm_w, hgrn_w_o, ssd_w_in, ssd_conv_w, ssd_conv_b, ssd_dt_bias, ssd_a_log, ssd_d, ssd_norm_w, ssd_w_o, ffn_w_up, ffn_conv_w, ffn_conv_b, ffn_w_down):
    cast = lambda w: w.astype(bf16)
    ssd_in_pad = (-ssd_w_in.shape[2]) % LANES
    prm = dict(norm_mix_w=norm_mix_w, norm_ffn_w=norm_ffn_w, norm_final_w=norm_final_w,
               attn_w_qkv=cast(attn_w_qkv), attn_b_qkv=attn_b_qkv, attn_sinks=attn_sinks,
               attn_w_o=cast(attn_w_o), attn_b_o=attn_b_o,
               hgrn_w_in=cast(hgrn_w_in), hgrn_lb_logits=hgrn_lb_logits, hgrn_norm_w=hgrn_norm_w,
               hgrn_w_o=cast(hgrn_w_o),
               ssd_w_in=jnp.pad(cast(ssd_w_in), ((0, 0), (0, 0), (0, ssd_in_pad))), ssd_conv_w=ssd_conv_w,
               ssd_conv_b=ssd_conv_b, ssd_dt_bias=ssd_dt_bias,
               ssd_a_log=ssd_a_log, ssd_d=ssd_d, ssd_norm_w=ssd_norm_w, ssd_w_o=cast(ssd_w_o),
               ffn_w_up=cast(ffn_w_up), ffn_conv_w=ffn_conv_w, ffn_conv_b=ffn_conv_b, ffn_w_down=cast(ffn_w_down))
    outs_p = _trunk(x_prompt, None, None, None, None, None, None, prm)
    outs_s = _trunk(x_sample, cache_attn_k, cache_attn_v, state_hgrn, state_ssm, state_ssm_conv,
                    state_ffn_conv, prm)
    return (outs_p[0], outs_s[0]) + outs_p[1:] + outs_s[1:]
```

```python
import functools

import jax
import jax.numpy as jnp
from jax import lax
from jax.experimental import pallas as pl
from jax.experimental.pallas import tpu as pltpu

bf16 = jnp.bfloat16
f32 = jnp.float32

RMS_EPS = 1e-6
N_MIXERS = 3
WINDOW = 128
PAST_LEN = 8192
ATT_HEAD_DIM = 64
ATT_KV_HEADS = 4
HG_HEADS = 8
HG_CHUNK_GROUP = 32
SSD_HEAD_DIM = 64
SSD_STATE = 128
SSD_GROUPS = 4

LANES = 128
SUBLANES = 8
VMEM_LIMIT = 56 * 1024 * 1024
PROJ_TM = 512
FFN_TM = 512


def _cparams(n_axes):
    return pltpu.CompilerParams(dimension_semantics=("arbitrary",) * n_axes,
                                vmem_limit_bytes=VMEM_LIMIT)


def _dot(a, b):
    return jnp.dot(a.astype(bf16), b.astype(bf16), preferred_element_type=f32)


def _dot_nt(a, b):
    return lax.dot_general(a.astype(bf16), b.astype(bf16), (((1,), (1,)), ((), ())),
                           preferred_element_type=f32)


def _dot_tn(a, b):
    return lax.dot_general(a.astype(bf16), b.astype(bf16), (((0,), (0,)), ((), ())),
                           preferred_element_type=f32)


def _silu(x):
    return x * jax.nn.sigmoid(x)


def _rms(x, w):
    ms = jnp.mean(x * x, axis=-1, keepdims=True)
    return x * lax.rsqrt(ms + RMS_EPS) * w


def _cumsum_rows(x, tril):
    hi = x.astype(bf16)
    r1 = x - hi.astype(f32)
    mid = r1.astype(bf16)
    lo = (r1 - mid.astype(f32)).astype(bf16)
    d = functools.partial(jnp.dot, preferred_element_type=f32)
    return d(tril, hi) + d(tril, mid) + d(tril, lo)


def _tril_mask(c):
    r = lax.broadcasted_iota(jnp.int32, (c, c), 0)
    s = lax.broadcasted_iota(jnp.int32, (c, c), 1)
    return r >= s


def _norm_kernel(x_ref, w_ref, o_ref):
    o_ref[...] = _rms(x_ref[...], w_ref[...]).astype(o_ref.dtype)


def rmsnorm_call(x, w, out_dtype):
    m, d = x.shape
    tm = min(1024, m)
    return pl.pallas_call(
        _norm_kernel,
        grid=(m // tm,),
        in_specs=[pl.BlockSpec((tm, d), lambda i: (i, 0)), pl.BlockSpec((1, d), lambda i: (0, 0))],
        out_specs=pl.BlockSpec((tm, d), lambda i: (i, 0)),
        out_shape=jax.ShapeDtypeStruct((m, d), out_dtype),
        compiler_params=_cparams(1),
        name="rmsnorm",
    )(x, w.reshape(1, d))


def _resident(shape):
    return pl.BlockSpec(shape, lambda *_: (0,) * len(shape), pipeline_mode=pl.Buffered(1))


def _proj_kernel(x_ref, w_ref, b_ref, o_ref):
    acc = jnp.dot(x_ref[...], w_ref[...], preferred_element_type=f32)
    o_ref[...] = (acc + b_ref[...]).astype(o_ref.dtype)


def proj_call(xn, w, bias, out_dtype, name):
    m, k = xn.shape
    n = w.shape[1]
    tm = min(PROJ_TM, m)
    if bias is None:
        bias = jnp.zeros((n,), f32)
    return pl.pallas_call(
        _proj_kernel,
        grid=(m // tm,),
        in_specs=[pl.BlockSpec((tm, k), lambda i: (i, 0)), _resident((k, n)), _resident((1, n))],
        out_specs=pl.BlockSpec((tm, n), lambda i: (i, 0)),
        out_shape=jax.ShapeDtypeStruct((m, n), out_dtype),
        compiler_params=_cparams(1),
        name=name,
    )(xn, w, bias.reshape(1, n))


def _outproj_kernel(a_ref, w_ref, b_ref, x_ref, nw_ref, xo_ref, xn_ref):
    y = jnp.dot(a_ref[...].astype(bf16), w_ref[...], preferred_element_type=f32) + b_ref[...] + x_ref[...]
    xo_ref[...] = y
    xn_ref[...] = _rms(y, nw_ref[...]).astype(xn_ref.dtype)


def outproj_call(act, w, bias, x, next_w, name):
    m, k = act.shape
    d = w.shape[1]
    tm = min(PROJ_TM, m)
    if bias is None:
        bias = jnp.zeros((d,), f32)
    return pl.pallas_call(
        _outproj_kernel,
        grid=(m // tm,),
        in_specs=[pl.BlockSpec((tm, k), lambda i: (i, 0)), _resident((k, d)), _resident((1, d)),
                  pl.BlockSpec((tm, d), lambda i: (i, 0)), _resident((1, d))],
        out_specs=[pl.BlockSpec((tm, d), lambda i: (i, 0)), pl.BlockSpec((tm, d), lambda i: (i, 0))],
        out_shape=[jax.ShapeDtypeStruct((m, d), f32), jax.ShapeDtypeStruct((m, d), bf16)],
        compiler_params=_cparams(1),
        name=name,
    )(act, w, bias.reshape(1, d), x, next_w.reshape(1, d))


def _ffn_kernel(*refs, tm, tf, t_seq, tail, has_state):
    if has_state:
        (xn_ref, wu_ref, cw_ref, cb_ref, wd_ref, x_ref, nw_ref, e1_ref, e2_ref,
         xo_ref, xn2_ref, tail_ref, carry_ref) = refs
    else:
        (xn_ref, wu_ref, cw_ref, cb_ref, wd_ref, x_ref, nw_ref,
         xo_ref, xn2_ref, tail_ref, carry_ref) = refs
    i = pl.program_id(0)
    dff = wd_ref.shape[0]

    @pl.when(i == 0)
    def _():
        carry_ref[...] = jnp.zeros_like(carry_ref)

    xn = xn_ref[...]
    rows = lax.broadcasted_iota(jnp.int32, (tm, 1), 0)
    tpos = (i * tm + rows) & (t_seq - 1)
    gs = []
    for c0 in range(0, dff, tf):
        cs = slice(c0, c0 + tf)
        a = jnp.dot(xn, wu_ref[:, cs], preferred_element_type=f32)
        b = jnp.dot(xn, wu_ref[:, dff + c0:dff + c0 + tf], preferred_element_type=f32)
        tail_ref[0, :, cs] = a[tm - tail:, :]
        carry = carry_ref[:, cs]
        p1 = jnp.where(rows == 0, carry[7:8, :], pltpu.roll(a, 1, 0))
        p2 = jnp.where(rows == 0, carry[6:7, :], jnp.where(rows == 1, carry[7:8, :], pltpu.roll(a, 2, 0)))
        if has_state:
            p1 = jnp.where(tpos == 0, e1_ref[:, cs], p1)
            p2 = jnp.where(tpos < 2, e2_ref[:, cs], p2)
        else:
            p1 = jnp.where(tpos == 0, 0.0, p1)
            p2 = jnp.where(tpos < 2, 0.0, p2)
        carry_ref[:, cs] = a[tm - SUBLANES:, :]
        y = cb_ref[:, cs] + p2 * cw_ref[0:1, cs]
        y = y + p1 * cw_ref[1:2, cs]
        y = y + a * cw_ref[2:3, cs]
        gs.append((_silu(y) * b).astype(bf16))
    g = jnp.concatenate(gs, axis=1)
    xo = jnp.dot(g, wd_ref[...], preferred_element_type=f32) + x_ref[...]
    xo_ref[...] = xo
    xn2_ref[...] = _rms(xo, nw_ref[...]).astype(xn2_ref.dtype)


def ffn_call(xn, w_up, conv_w, conv_b, w_down, x, next_w, t_seq, e1, e2, out_dtype, name):
    m, d = xn.shape
    dff = w_down.shape[0]
    tf = 2 * LANES
    assert dff % tf == 0 and (t_seq & (t_seq - 1)) == 0
    has_state = e1 is not None
    tm = min(FFN_TM, m)
    assert tm % t_seq == 0 or t_seq % tm == 0
    tail = tm if has_state else SUBLANES
    row = lambda w: pl.BlockSpec((tm, w), lambda i: (i, 0))
    in_specs = [row(d), _resident((d, 2 * dff)), _resident((3, dff)), _resident((1, dff)),
                _resident((dff, d)), row(d), _resident((1, d))]
    args = [xn, w_up, conv_w, conv_b.reshape(1, dff), w_down, x, next_w.reshape(1, d)]
    if has_state:
        in_specs += [row(dff), row(dff)]
        args += [e1, e2]
    kern = functools.partial(_ffn_kernel, tm=tm, tf=tf, t_seq=t_seq, tail=tail, has_state=has_state)
    return pl.pallas_call(
        kern,
        grid=(m // tm,),
        in_specs=in_specs,
        out_specs=[row(d), row(d), pl.BlockSpec((1, tail, dff), lambda i: (i, 0, 0))],
        out_shape=[jax.ShapeDtypeStruct((m, d), f32),
                   jax.ShapeDtypeStruct((m, d), out_dtype),
                   jax.ShapeDtypeStruct((m // tm, tail, dff), f32)],
        scratch_shapes=[pltpu.VMEM((SUBLANES, dff), f32)],
        compiler_params=_cparams(1),
        name=name,
    )(*args)


def _attn_softmax_pv(s, sink_col, vj):
    m = jnp.maximum(jnp.max(s, axis=-1, keepdims=True), sink_col)
    p = jnp.exp(s - m)
    l = jnp.sum(p, axis=-1, keepdims=True) + jnp.exp(sink_col - m)
    return _dot(p, vj) / l


def _sink_col(sinks_ref, j, group, rows_per_head):
    r = lax.broadcasted_iota(jnp.int32, (group * rows_per_head, 1), 0)
    col = jnp.full((group * rows_per_head, 1), sinks_ref[j * group], f32)
    for p in range(1, group):
        col = jnp.where(r >= p * rows_per_head, sinks_ref[j * group + p], col)
    return col


def _attn_prompt_kernel(sinks_ref, q_ref, kvc_ref, kvp_ref, o_ref, *, n_heads):
    n = pl.program_id(1)
    w, hd, kvh = WINDOW, ATT_HEAD_DIM, ATT_KV_HEADS
    group = n_heads // kvh
    q = q_ref[...] * (hd ** -0.5)
    kvc = kvc_ref[...]
    kvp = kvp_ref[...]
    kv = jnp.concatenate([kvp, kvc], axis=0)
    i = lax.broadcasted_iota(jnp.int32, (w, 2 * w), 0)
    c = lax.broadcasted_iota(jnp.int32, (w, 2 * w), 1)
    band = (c > i) & (c <= i + w) & ((n > 0) | (c >= w))
    mask = jnp.concatenate([band] * group, axis=0)
    for j in range(kvh):
        kj = kv[:, j * hd:(j + 1) * hd]
        vj = kv[:, (kvh + j) * hd:(kvh + j + 1) * hd]
        qs = jnp.concatenate([q[:, (j * group + p) * hd:(j * group + p + 1) * hd] for p in range(group)], axis=0)
        s = jnp.where(mask, _dot_nt(qs, kj), -jnp.inf)
        o = _attn_softmax_pv(s, _sink_col(sinks_ref, j, group, w), vj)
        for p in range(group):
            h = j * group + p
            o_ref[:, h * hd:(h + 1) * hd] = o[p * w:(p + 1) * w, :].astype(o_ref.dtype)


def attn_prompt_call(qkv, sinks, bsz, t, n_heads):
    w, hd, kvh = WINDOW, ATT_HEAD_DIM, ATT_KV_HEADS
    nb = t // w
    qd = n_heads * hd
    kvd = 2 * kvh * hd
    assert qd % kvd == 0
    kvblk = qd // kvd
    kern = functools.partial(_attn_prompt_kernel, n_heads=n_heads)
    return pl.pallas_call(
        kern,
        grid=(bsz, nb),
        in_specs=[pl.BlockSpec(memory_space=pltpu.SMEM),
                  pl.BlockSpec((w, qd), lambda b, n: (b * nb + n, 0)),
                  pl.BlockSpec((w, kvd), lambda b, n: (b * nb + n, kvblk)),
                  pl.BlockSpec((w, kvd), lambda b, n: (b * nb + jnp.maximum(n - 1, 0), kvblk))],
        out_specs=pl.BlockSpec((w, qd), lambda b, n: (b * nb + n, 0)),
        out_shape=jax.ShapeDtypeStruct((bsz * t, qd), bf16),
        compiler_params=_cparams(2),
        name="attn_prompt",
    )(sinks, qkv, qkv, qkv)


def _attn_cache_kernel(sinks_ref, q_ref, ck_ref, cv_ref, o_ref, *, n_heads, t_valid, bt):
    w, hd, kvh = WINDOW, ATT_HEAD_DIM, ATT_KV_HEADS
    group = n_heads // kvh
    tp = q_ref.shape[1]
    qd = n_heads * hd
    tq = lax.broadcasted_iota(jnp.int32, (tp, w), 0)
    jc = lax.broadcasted_iota(jnp.int32, (tp, w), 1)
    mc = (jc > tq) & (jc - w + PAST_LEN >= 0)
    tq2 = lax.broadcasted_iota(jnp.int32, (tp, tp), 0)
    un = lax.broadcasted_iota(jnp.int32, (tp, tp), 1)
    mn = (un <= tq2) & (un < t_valid)
    mask_c = jnp.concatenate([mc] * group, axis=0)
    mask_n = jnp.concatenate([mn] * group, axis=0)

    def body(bi, carry):
        row = q_ref[bi]
        q = row[:, :qd] * (hd ** -0.5)
        ck = ck_ref[bi]
        cv = cv_ref[bi]
        for j in range(kvh):
            kc = ck[:, j * hd:(j + 1) * hd]
            vc = cv[:, j * hd:(j + 1) * hd]
            kn = row[:, qd + j * hd:qd + (j + 1) * hd]
            vn = row[:, qd + (kvh + j) * hd:qd + (kvh + j + 1) * hd]
            qs = jnp.concatenate([q[:, (j * group + p) * hd:(j * group + p + 1) * hd] for p in range(group)], axis=0)
            sc = jnp.where(mask_c, _dot_nt(qs, kc), -jnp.inf)
            sn = jnp.where(mask_n, _dot_nt(qs, kn), -jnp.inf)
            sink = _sink_col(sinks_ref, j, group, tp)
            m = jnp.maximum(jnp.maximum(jnp.max(sc, axis=-1, keepdims=True),
                                        jnp.max(sn, axis=-1, keepdims=True)), sink)
            pc = jnp.exp(sc - m)
            pn = jnp.exp(sn - m)
            l = jnp.sum(pc, axis=-1, keepdims=True) + jnp.sum(pn, axis=-1, keepdims=True) + jnp.exp(sink - m)
            o = (_dot(pc, vc) + _dot(pn, vn)) / l
            for p in range(group):
                h = j * group + p
                o_ref[bi, :, h * hd:(h + 1) * hd] = o[p * tp:(p + 1) * tp, :].astype(o_ref.dtype)
        return carry

    lax.fori_loop(0, bt, body, 0)


def attn_cache_call(qkv, cache_k, cache_v, sinks, n_heads, t_valid):
    bsz, tp, nq = qkv.shape
    w = WINDOW
    kd = cache_k.shape[-1]
    qd = n_heads * ATT_HEAD_DIM
    bt = 8
    kern = functools.partial(_attn_cache_kernel, n_heads=n_heads, t_valid=t_valid, bt=bt)
    return pl.pallas_call(
        kern,
        grid=(bsz // bt,),
        in_specs=[pl.BlockSpec(memory_space=pltpu.SMEM),
                  pl.BlockSpec((bt, tp, nq), lambda b: (b, 0, 0)),
                  pl.BlockSpec((bt, w, kd), lambda b: (b, 0, 0)),
                  pl.BlockSpec((bt, w, kd), lambda b: (b, 0, 0))],
        out_specs=pl.BlockSpec((bt, tp, qd), lambda b: (b, 0, 0)),
        out_shape=jax.ShapeDtypeStruct((bsz, tp, qd), f32),
        compiler_params=_cparams(1),
        name="attn_cache",
    )(sinks, qkv, cache_k, cache_v)


def _hgrn_kernel(*refs, c_rows, grp, t_valid, t_pad, layer, has_s0):
    if has_s0:
        p_ref, lbl_ref, nw_ref, s0_ref, y_ref, so_ref, st_ref = refs
    else:
        p_ref, lbl_ref, nw_ref, y_ref, so_ref, st_ref = refs
    c = pl.program_id(1)
    nc = pl.num_programs(1)
    nh = HG_HEADS
    dk = p_ref.shape[1] // 4 // nh
    dim = nh * dk
    ng = c_rows // grp

    @pl.when(c == 0)
    def _():
        if has_s0:
            for h in range(nh):
                st_ref[h] = s0_ref[0, h].T
        else:
            st_ref[...] = jnp.zeros_like(st_ref)

    lbl = lbl_ref[...]
    e = jnp.exp(lbl - jnp.max(lbl, axis=0, keepdims=True))
    sm = e / jnp.sum(e, axis=0, keepdims=True)
    lb = jnp.zeros((1, dim), f32)
    for r in range(1, layer + 1):
        lb = lb + sm[r:r + 1, :]

    x = p_ref[...]
    q = _silu(x[:, 0:dim])
    fg = lb + (1.0 - lb) * jax.nn.sigmoid(x[:, dim:2 * dim])
    logf = jnp.log(fg)
    k = 1.0 - fg
    v = x[:, 2 * dim:3 * dim]
    gate = x[:, 3 * dim:4 * dim]
    if t_valid < t_pad:
        valid = (c * c_rows + lax.broadcasted_iota(jnp.int32, (c_rows, 1), 0)) < t_valid
        logf = jnp.where(valid, logf, 0.0)
        k = jnp.where(valid, k, 0.0)

    tril = _tril_mask(c_rows)
    bcum = _cumsum_rows(logf, tril.astype(bf16))
    rr = lax.broadcasted_iota(jnp.int32, (c_rows, c_rows), 0)
    ss = lax.broadcasted_iota(jnp.int32, (c_rows, c_rows), 1)
    gshift = grp.bit_length() - 1
    diag_mask = tril & ((rr >> gshift) == (ss >> gshift))
    nw = nw_ref[...]

    def rows_of(vals):
        return jnp.concatenate([jnp.broadcast_to(r, (grp, dk)) for r in vals], axis=0)

    for h in range(nh):
        sl = slice(h * dk, (h + 1) * dk)
        b = bcum[:, sl]
        qh, kh, vh = q[:, sl], k[:, sl], v[:, sl]
        zero = jnp.zeros((1, dk), f32)
        r = [zero] + [b[i * grp - 1:i * grp, :] for i in range(1, ng)] + [b[c_rows - 1:c_rows, :]]
        mid = [b[i * grp + grp // 2 - 1:i * grp + grp // 2, :] for i in range(ng)]
        b_last = r[ng]
        r_start = rows_of(r[:ng])
        r_end = rows_of(r[1:])
        r_mid = rows_of(mid)
        qg = qh * jnp.exp(b - r_start)
        kt = kh * jnp.exp(r_end - b)
        qm = qh * jnp.exp(b - r_mid)
        km = kh * jnp.exp(r_mid - b)
        a = jnp.where(diag_mask, _dot_nt(qm, km), 0.0)
        if ng > 1:
            lhs, rhs = [], []
            for j in range(ng - 1):
                lhs.append(jnp.concatenate(
                    [jnp.zeros((grp, dk), f32) if i <= j else qg[i * grp:(i + 1) * grp, :] * jnp.exp(r[i] - r[j + 1])
                     for i in range(ng)], axis=0))
                rhs.append(jnp.concatenate(
                    [kt[i * grp:(i + 1) * grp, :] if i == j else jnp.zeros((grp, dk), f32) for i in range(ng)], axis=0))
            a = a + _dot_nt(jnp.concatenate(lhs, axis=1), jnp.concatenate(rhs, axis=1))
        q_inter = qg * jnp.exp(r_start)
        st = st_ref[h]
        o = _dot(a, vh) + _dot_nt(q_inter, st)
        k_dec = kt * jnp.exp(b_last - r_end)
        st_new = st * jnp.exp(b_last) + _dot_tn(vh, k_dec)
        st_ref[h] = st_new
        on = o * lax.rsqrt(jnp.mean(o * o, axis=-1, keepdims=True) + RMS_EPS)
        y_ref[:, sl] = (on * nw[:, sl] * _silu(gate[:, sl])).astype(y_ref.dtype)

    @pl.when(c == nc - 1)
    def _():
        for h in range(nh):
            so_ref[0, h] = st_ref[h].T


def hgrn_call(p, lb_logits, norm_w, s0, bsz, t_pad, t_valid, layer):
    dim = p.shape[1] // 4
    nh = HG_HEADS
    dk = dim // nh
    c_rows = min(128, t_pad)
    grp = min(HG_CHUNK_GROUP, c_rows)
    nc = t_pad // c_rows
    has_s0 = s0 is not None
    in_specs = [pl.BlockSpec((c_rows, 4 * dim), lambda b, c: (b * nc + c, 0)),
                pl.BlockSpec(lb_logits.shape, lambda b, c: (0, 0)),
                pl.BlockSpec((1, dim), lambda b, c: (0, 0))]
    args = [p, lb_logits, norm_w.reshape(1, dim)]
    if has_s0:
        in_specs.append(pl.BlockSpec((1, nh, dk, dk), lambda b, c: (b, 0, 0, 0)))
        args.append(s0)
    kern = functools.partial(_hgrn_kernel, c_rows=c_rows, grp=grp, t_valid=t_valid, t_pad=t_pad,
                             layer=layer, has_s0=has_s0)
    return pl.pallas_call(
        kern,
        grid=(bsz, nc),
        in_specs=in_specs,
        out_specs=[pl.BlockSpec((c_rows, dim), lambda b, c: (b * nc + c, 0)),
                   pl.BlockSpec((1, nh, dk, dk), lambda b, c: (b, 0, 0, 0))],
        out_shape=[jax.ShapeDtypeStruct((bsz * t_pad, dim), bf16),
                   jax.ShapeDtypeStruct((bsz, nh, dk, dk), f32)],
        scratch_shapes=[pltpu.VMEM((nh, dk, dk), f32)],
        compiler_params=_cparams(2),
        name="hgrn",
    )(*args)


def _expand_heads(w, ex_ref):
    hi = w.astype(bf16)
    r1 = w - hi.astype(f32)
    mid = r1.astype(bf16)
    lo = (r1 - mid.astype(f32)).astype(bf16)
    return jnp.dot(jnp.concatenate([hi, mid, lo], axis=1), ex_ref[...], preferred_element_type=f32)


def _ssd_kernel(*refs, c_rows, t_valid, t_pad, has_s0):
    if has_s0:
        (zx_ref, cw_ref, cb_ref, dtb_ref, alog_ref, dsk_ref, nw_ref, ex_ref, s0_ref, buf_ref,
         y_ref, so_ref, st_ref, carry_ref) = refs
    else:
        (zx_ref, cw_ref, cb_ref, dtb_ref, alog_ref, dsk_ref, nw_ref, ex_ref,
         y_ref, so_ref, st_ref, carry_ref) = refs
    c = pl.program_id(1)
    nc = pl.num_programs(1)
    hp, ns, ngr = SSD_HEAD_DIM, SSD_STATE, SSD_GROUPS
    di = y_ref.shape[1]
    nh = di // hp
    hpg = nh // ngr
    gw = ngr * ns
    kconv = cw_ref.shape[0]
    pair = 2 * hp
    assert pair == LANES and hpg % 2 == 0

    @pl.when(c == 0)
    def _():
        carry_ref[...] = jnp.zeros_like(carry_ref)
        if has_s0:
            for r in range(di // LANES):
                st_ref[:, r * LANES:(r + 1) * LANES] = s0_ref[0, r * LANES:(r + 1) * LANES, :].T
            carry_ref[SUBLANES - (kconv - 1):, :] = buf_ref[0]
        else:
            st_ref[...] = jnp.zeros_like(st_ref)

    zx = zx_ref[...]
    z = zx[:, :di]
    cdim = cw_ref.shape[1]
    u = zx[:, di:di + cdim]
    rows = lax.broadcasted_iota(jnp.int32, (c_rows, 1), 0)
    cr = carry_ref[...]
    cw = cw_ref[...]
    yc = cb_ref[...]
    for kk in range(kconv - 1):
        back = kconv - 1 - kk
        sh = pltpu.roll(u, back, 0)
        for r0 in range(back):
            sh = jnp.where(rows == r0, cr[SUBLANES - back + r0:SUBLANES - back + r0 + 1, :], sh)
        yc = yc + sh * cw[kk:kk + 1, :]
    yc = yc + u * cw[kconv - 1:kconv, :]
    carry_ref[...] = u[c_rows - SUBLANES:, :]
    xbc = _silu(yc)
    xs = xbc[:, :di]
    bm = xbc[:, di:di + gw]
    cm = xbc[:, di + gw:di + 2 * gw]

    dt = jax.nn.softplus(zx[:, di + cdim:] + dtb_ref[...])
    if t_valid < t_pad:
        valid = (c * c_rows + rows) < t_valid
        dt = jnp.where(valid, dt, 0.0)
    a_neg = -jnp.exp(alog_ref[...])
    tril = _tril_mask(c_rows)
    acum = _cumsum_rows(dt * a_neg, tril.astype(bf16))

    def head_rows(a):
        if c_rows < LANES:
            a = jnp.concatenate([a, jnp.zeros((LANES - c_rows, LANES), f32)], axis=0)
        return a.T[:, :c_rows]

    acum_t = head_rows(acum)
    dt_t = head_rows(dt)
    a_last = acum[c_rows - 1:c_rows, :]
    xd = xs * _expand_heads(dt * jnp.exp(a_last - acum), ex_ref)
    dsk = dsk_ref[...]
    lo = lax.broadcasted_iota(jnp.int32, (c_rows, LANES), 1) < hp

    ys = []
    for g in range(ngr):
        bg = bm[:, g * ns:(g + 1) * ns]
        cg = cm[:, g * ns:(g + 1) * ns]
        cb_mat = _dot_nt(cg, bg)
        gsl = slice(g * hpg * hp, (g + 1) * hpg * hp)
        y_inter = _dot(cg, st_ref[:, gsl])
        decs = []
        for q in range(hpg // 2):
            h1 = g * hpg + 2 * q
            psl = slice(h1 * hp, h1 * hp + pair)
            ms, es = [], []
            for h in (h1, h1 + 1):
                colb = jnp.broadcast_to(acum[:, h:h + 1], (c_rows, LANES))
                seg = colb[:, :c_rows] - acum_t[h:h + 1, :]
                ms.append(cb_mat * jnp.exp(jnp.where(tril, seg, -jnp.inf)) * dt_t[h:h + 1, :])
                es.append(jnp.exp(colb))
            xp = xs[:, psl]
            x_lo = jnp.where(lo, xp, 0.0)
            x_hi = jnp.where(lo, 0.0, xp)
            if (2 * c_rows) % LANES == 0:
                y_intra = _dot(jnp.concatenate(ms, axis=1), jnp.concatenate([x_lo, x_hi], axis=0))
            else:
                y_intra = _dot(ms[0], x_lo) + _dot(ms[1], x_hi)
            y_pair = y_intra + y_inter[:, q * pair:(q + 1) * pair] * jnp.where(lo, es[0], es[1])
            ys.append(y_pair + dsk[:, psl] * xp)
            decs.append(jnp.where(lo[0:1, :], jnp.exp(a_last[:, h1:h1 + 1]), jnp.exp(a_last[:, h1 + 1:h1 + 2])))
        st_ref[:, gsl] = st_ref[:, gsl] * jnp.concatenate(decs, axis=1) + _dot_tn(bg, xd[:, gsl])

    y = jnp.concatenate(ys, axis=1) * _silu(z)
    nw = nw_ref[...]
    gdim = di // ngr
    for g in range(ngr):
        sl = slice(g * gdim, (g + 1) * gdim)
        yg = y[:, sl]
        y_ref[:, sl] = (yg * lax.rsqrt(jnp.mean(yg * yg, axis=-1, keepdims=True) + RMS_EPS) * nw[:, sl]).astype(y_ref.dtype)

    @pl.when(c == nc - 1)
    def _():
        for r in range(di // LANES):
            so_ref[0, r * LANES:(r + 1) * LANES, :] = st_ref[:, r * LANES:(r + 1) * LANES].T


def ssd_call(zx, conv_w, conv_b, dt_bias, a_log, d_skip, norm_w, s0, conv_buf, bsz, t_pad, t_valid):
    hp, ns = SSD_HEAD_DIM, SSD_STATE
    conv_dim = conv_w.shape[1]
    di = zx.shape[1] - conv_dim - LANES
    nh = di // hp
    c_rows = min(128, t_pad)
    nc = t_pad // c_rows
    has_s0 = s0 is not None
    pad_h = lambda a: jnp.pad(a.reshape(1, nh), ((0, 0), (0, LANES - nh)))
    full = lambda shape: pl.BlockSpec(shape, lambda b, c: (0,) * len(shape))
    sel = (jnp.arange(di)[None, :] // hp == jnp.arange(LANES)[:, None]).astype(bf16)
    expand = jnp.concatenate([sel, sel, sel], axis=0)
    in_specs = [pl.BlockSpec((c_rows, zx.shape[1]), lambda b, c: (b * nc + c, 0)),
                full(conv_w.shape), full((1, conv_dim)), full((1, LANES)), full((1, LANES)),
                full((1, di)), full((1, di)), full(expand.shape)]
    args = [zx, conv_w, conv_b.reshape(1, conv_dim), pad_h(dt_bias), pad_h(a_log),
            jnp.repeat(d_skip, hp).reshape(1, di), norm_w.reshape(1, di), expand]
    if has_s0:
        in_specs += [pl.BlockSpec((1, di, ns), lambda b, c: (b, 0, 0)),
                     pl.BlockSpec((1,) + conv_buf.shape[1:], lambda b, c: (b, 0, 0))]
        args += [s0.reshape(bsz, di, ns), conv_buf]
    kern = functools.partial(_ssd_kernel, c_rows=c_rows, t_valid=t_valid, t_pad=t_pad, has_s0=has_s0)
    y, s_new = pl.pallas_call(
        kern,
        grid=(bsz, nc),
        in_specs=in_specs,
        out_specs=[pl.BlockSpec((c_rows, di), lambda b, c: (b * nc + c, 0)),
                   pl.BlockSpec((1, di, ns), lambda b, c: (b, 0, 0))],
        out_shape=[jax.ShapeDtypeStruct((bsz * t_pad, di), bf16),
                   jax.ShapeDtypeStruct((bsz, di, ns), f32)],
        scratch_shapes=[pltpu.VMEM((ns, di), f32), pltpu.VMEM((SUBLANES, conv_dim), f32)],
        compiler_params=_cparams(2),
        name="ssd",
    )(*args)
    return y, s_new.reshape(bsz, nh, hp, ns)


def _pad_time(a, bsz, t, t_pad):
    if t_pad == t:
        return a
    n = a.shape[-1]
    return jnp.pad(a.reshape(bsz, t, n), ((0, 0), (0, t_pad - t), (0, 0))).reshape(bsz * t_pad, n)


def _unpad_time(a, bsz, t, t_pad):
    if t_pad == t:
        return a
    n = a.shape[-1]
    return a.reshape(bsz, t_pad, n)[:, :t].reshape(bsz * t, n)


def _trunk(x, c_k, c_v, s_hg, s_ssm, s_sconv, s_fconv, prm):
    bsz, t, d = x.shape
    depth = prm["norm_mix_w"].shape[0]
    m = bsz * t
    decode = c_k is not None
    t_pad = t if t % SUBLANES == 0 else (t // SUBLANES + 1) * SUBLANES
    n_heads = d // ATT_HEAD_DIM
    kvd = ATT_KV_HEADS * ATT_HEAD_DIM
    x2 = x.reshape(m, d)
    xn = rmsnorm_call(x2, prm["norm_mix_w"][0], bf16)
    nk, nv, nhg, nssm, nsconv, nfconv = [], [], [], [], [], []
    slot = [0] * N_MIXERS
    for i in range(depth):
        kind = i % N_MIXERS
        j = slot[kind]
        slot[kind] += 1
        nw_ffn = prm["norm_ffn_w"][i]
        if kind == 0:
            w_qkv, b_qkv = prm["attn_w_qkv"][j], prm["attn_b_qkv"][j]
            nq = w_qkv.shape[1]
            if decode:
                qkv = proj_call(xn, w_qkv, b_qkv, f32, "attn_qkv")
                kv_new = qkv[:, n_heads * ATT_HEAD_DIM:].reshape(bsz, t, 2, kvd)
                ck = c_k[j].reshape(bsz, WINDOW, kvd)
                cv = c_v[j].reshape(bsz, WINDOW, kvd)
                o = attn_cache_call(_pad_time(qkv, bsz, t, t_pad).reshape(bsz, t_pad, nq), ck, cv,
                                    prm["attn_sinks"][j], n_heads, t)
                o = o[:, :t].reshape(m, n_heads * ATT_HEAD_DIM)
                to4 = lambda a: a.reshape(bsz, t, ATT_KV_HEADS, ATT_HEAD_DIM)
                k_new = jnp.concatenate([c_k[j][:, t:], to4(kv_new[:, :, 0])], axis=1)
                v_new = jnp.concatenate([c_v[j][:, t:], to4(kv_new[:, :, 1])], axis=1)
            else:
                qkv = proj_call(xn, w_qkv, b_qkv, bf16, "attn_qkv")
                o = attn_prompt_call(qkv, prm["attn_sinks"][j], bsz, t, n_heads)
                xn_last = xn.reshape(bsz, t, d)[:, t - WINDOW:].reshape(bsz * WINDOW, d)
                kv_last = proj_call(xn_last, w_qkv[:, n_heads * ATT_HEAD_DIM:], b_qkv[n_heads * ATT_HEAD_DIM:],
                                    f32, "attn_kv_tail").reshape(bsz, WINDOW, 2, kvd)
                k_new, v_new = kv_last[:, :, 0], kv_last[:, :, 1]
            nk.append(k_new.reshape(bsz, WINDOW, ATT_KV_HEADS, ATT_HEAD_DIM))
            nv.append(v_new.reshape(bsz, WINDOW, ATT_KV_HEADS, ATT_HEAD_DIM))
            x2, xn = outproj_call(o, prm["attn_w_o"][j], prm["attn_b_o"][j], x2, nw_ffn, "attn_out")
        elif kind == 1:
            w_in = prm["hgrn_w_in"][j]
            p = proj_call(xn, w_in, None, f32, "hgrn_in")
            y, s_new = hgrn_call(_pad_time(p, bsz, t, t_pad), prm["hgrn_lb_logits"], prm["hgrn_norm_w"][j],
                                 None if s_hg is None else s_hg[j], bsz, t_pad, t, i)
            nhg.append(s_new)
            x2, xn = outproj_call(_unpad_time(y, bsz, t, t_pad), prm["hgrn_w_o"][j], None, x2, nw_ffn, "hgrn_out")
        else:
            conv_dim = prm["ssd_conv_w"].shape[2]
            di = prm["ssd_w_o"].shape[1]
            zx = proj_call(xn, prm["ssd_w_in"][j], None, f32, "ssd_in")
            y, s_new = ssd_call(_pad_time(zx, bsz, t, t_pad),
                                prm["ssd_conv_w"][j], prm["ssd_conv_b"][j], prm["ssd_dt_bias"][j],
                                prm["ssd_a_log"][j], prm["ssd_d"][j], prm["ssd_norm_w"][j],
                                None if s_ssm is None else s_ssm[j],
                                None if s_sconv is None else s_sconv[j], bsz, t_pad, t)
            nssm.append(s_new)
            kc = prm["ssd_conv_w"].shape[1]
            pre = zx.reshape(bsz, t, zx.shape[1])[:, max(t - (kc - 1), 0):, di:di + conv_dim]
            if s_sconv is not None:
                pre = jnp.concatenate([s_sconv[j], pre], axis=1)
            nsconv.append(pre[:, pre.shape[1] - (kc - 1):])
            x2, xn = outproj_call(_unpad_time(y, bsz, t, t_pad), prm["ssd_w_o"][j], None, x2, nw_ffn, "ssd_out")
        last = i == depth - 1
        next_w = prm["norm_final_w"] if last else prm["norm_mix_w"][i + 1]
        dff = prm["ffn_w_down"].shape[1]
        if s_fconv is not None:
            buf = s_fconv[i]
            zeros = jnp.zeros((bsz, t - 1, dff), f32)
            e1 = jnp.concatenate([buf[:, 1:2], zeros], axis=1).reshape(m, dff)
            e2 = jnp.concatenate([buf, zeros[:, 1:]], axis=1).reshape(m, dff)
        else:
            e1 = e2 = None
        x2, xn, tail = ffn_call(xn, prm["ffn_w_up"][i], prm["ffn_conv_w"][i], prm["ffn_conv_b"][i],
                                prm["ffn_w_down"][i], x2, next_w, t, e1, e2, f32 if last else bf16, "ffn")
        kf = prm["ffn_conv_w"].shape[1]
        if s_fconv is not None:
            a_full = tail.reshape(bsz, t, dff)
            nfconv.append(jnp.concatenate([s_fconv[i], a_full], axis=1)[:, t:])
        else:
            tiles_per_seq = t // tail_tile_rows(m)
            tl = tail.reshape(bsz, tiles_per_seq, SUBLANES, dff)[:, -1]
            nfconv.append(tl[:, SUBLANES - (kf - 1):])
    stack = lambda xs: xs[0][None] if len(xs) == 1 else jnp.stack(xs)
    return (xn.reshape(bsz, t, d), stack(nk), stack(nv), stack(nhg), stack(nssm), stack(nsconv), stack(nfconv))


def tail_tile_rows(m):
    return min(FFN_TM, m)


def kernel(x_prompt, x_sample, cache_attn_k, cache_attn_v, state_hgrn, state_ssm, state_ssm_conv, state_ffn_conv, norm_mix_w, norm_ffn_w, norm_final_w, attn_w_qkv, attn_b_qkv, attn_sinks, attn_w_o, attn_b_o, hgrn_w_in, hgrn_lb_logits, hgrn_norm_w, hgrn_w_o, ssd_w_in, ssd_conv_w, ssd_conv_b, ssd_dt_bias, ssd_a_log, ssd_d, ssd_norm_w, ssd_w_o, ffn_w_up, ffn_conv_w, ffn_conv_b, ffn_w_down):
    cast = lambda w: w.astype(bf16)
    ssd_in_pad = (-ssd_w_in.shape[2]) % LANES
    prm = dict(norm_mix_w=norm_mix_w, norm_ffn_w=norm_ffn_w, norm_final_w=norm_final_w,
               attn_w_qkv=cast(attn_w_qkv), attn_b_qkv=attn_b_qkv, attn_sinks=attn_sinks,
               attn_w_o=cast(attn_w_o), attn_b_o=attn_b_o,
               hgrn_w_in=cast(hgrn_w_in), hgrn_lb_logits=hgrn_lb_logits, hgrn_norm_w=hgrn_norm_w,
               hgrn_w_o=cast(hgrn_w_o),
               ssd_w_in=jnp.pad(cast(ssd_w_in), ((0, 0), (0, 0), (0, ssd_in_pad))), ssd_conv_w=ssd_conv_w,
               ssd_conv_b=ssd_conv_b, ssd_dt_bias=ssd_dt_bias,
               ssd_a_log=ssd_a_log, ssd_d=ssd_d, ssd_norm_w=ssd_norm_w, ssd_w_o=cast(ssd_w_o),
               ffn_w_up=cast(ffn_w_up), ffn_conv_w=ffn_conv_w, ffn_conv_b=ffn_conv_b, ffn_w_down=cast(ffn_w_down))
    outs_p = _trunk(x_prompt, None, None, None, None, None, None, prm)
    outs_s = _trunk(x_sample, cache_attn_k, cache_attn_v, state_hgrn, state_ssm, state_ssm_conv,
                    state_ffn_conv, prm)
    return (outs_p[0], outs_s[0]) + outs_p[1:] + outs_s[1:]
```

```python
import functools

import jax
import jax.numpy as jnp
from jax import lax
from jax.experimental import pallas as pl
from jax.experimental.pallas import tpu as pltpu

bf16 = jnp.bfloat16
f32 = jnp.float32

RMS_EPS = 1e-6
N_MIXERS = 3
WINDOW = 128
PAST_LEN = 8192
ATT_HEAD_DIM = 64
ATT_KV_HEADS = 4
HG_HEADS = 8
HG_CHUNK_GROUP = 32
SSD_HEAD_DIM = 64
SSD_STATE = 128
SSD_GROUPS = 4

LANES = 128
SUBLANES = 8
VMEM_LIMIT = 56 * 1024 * 1024
PROJ_TM = 512
FFN_TM = 512


def _cparams(n_axes):
    return pltpu.CompilerParams(dimension_semantics=("arbitrary",) * n_axes,
                                vmem_limit_bytes=VMEM_LIMIT)


def _dot(a, b):
    return jnp.dot(a.astype(bf16), b.astype(bf16), preferred_element_type=f32)


def _dot_nt(a, b):
    return lax.dot_general(a.astype(bf16), b.astype(bf16), (((1,), (1,)), ((), ())),
                           preferred_element_type=f32)


def _dot_tn(a, b):
    return lax.dot_general(a.astype(bf16), b.astype(bf16), (((0,), (0,)), ((), ())),
                           preferred_element_type=f32)


def _silu(x):
    return x * jax.nn.sigmoid(x)


def _rms(x, w):
    ms = jnp.mean(x * x, axis=-1, keepdims=True)
    return x * lax.rsqrt(ms + RMS_EPS) * w


def _cumsum_rows(x, tril):
    hi = x.astype(bf16)
    r1 = x - hi.astype(f32)
    mid = r1.astype(bf16)
    lo = (r1 - mid.astype(f32)).astype(bf16)
    d = functools.partial(jnp.dot, preferred_element_type=f32)
    return d(tril, hi) + d(tril, mid) + d(tril, lo)


def _tril_mask(c):
    r = lax.broadcasted_iota(jnp.int32, (c, c), 0)
    s = lax.broadcasted_iota(jnp.int32, (c, c), 1)
    return r >= s


def _norm_kernel(x_ref, w_ref, o_ref):
    o_ref[...] = _rms(x_ref[...], w_ref[...]).astype(o_ref.dtype)


def rmsnorm_call(x, w, out_dtype):
    m, d = x.shape
    tm = min(1024, m)
    return pl.pallas_call(
        _norm_kernel,
        grid=(m // tm,),
        in_specs=[pl.BlockSpec((tm, d), lambda i: (i, 0)), pl.BlockSpec((1, d), lambda i: (0, 0))],
        out_specs=pl.BlockSpec((tm, d), lambda i: (i, 0)),
        out_shape=jax.ShapeDtypeStruct((m, d), out_dtype),
        compiler_params=_cparams(1),
        name="rmsnorm",
    )(x, w.reshape(1, d))


def _resident(shape):
    return pl.BlockSpec(shape, lambda *_: (0,) * len(shape), pipeline_mode=pl.Buffered(1))


def _proj_kernel(x_ref, w_ref, b_ref, o_ref):
    acc = jnp.dot(x_ref[...], w_ref[...], preferred_element_type=f32)
    o_ref[...] = (acc + b_ref[...]).astype(o_ref.dtype)


def proj_call(xn, w, bias, out_dtype, name):
    m, k = xn.shape
    n = w.shape[1]
    tm = min(PROJ_TM, m)
    if bias is None:
        bias = jnp.zeros((n,), f32)
    return pl.pallas_call(
        _proj_kernel,
        grid=(m // tm,),
        in_specs=[pl.BlockSpec((tm, k), lambda i: (i, 0)), _resident((k, n)), _resident((1, n))],
        out_specs=pl.BlockSpec((tm, n), lambda i: (i, 0)),
        out_shape=jax.ShapeDtypeStruct((m, n), out_dtype),
        compiler_params=_cparams(1),
        name=name,
    )(xn, w, bias.reshape(1, n))


def _outproj_kernel(a_ref, w_ref, b_ref, x_ref, nw_ref, xo_ref, xn_ref):
    y = jnp.dot(a_ref[...].astype(bf16), w_ref[...], preferred_element_type=f32) + b_ref[...] + x_ref[...]
    xo_ref[...] = y
    xn_ref[...] = _rms(y, nw_ref[...]).astype(xn_ref.dtype)


def outproj_call(act, w, bias, x, next_w, name):
    m, k = act.shape
    d = w.shape[1]
    tm = min(PROJ_TM, m)
    if bias is None:
        bias = jnp.zeros((d,), f32)
    return pl.pallas_call(
        _outproj_kernel,
        grid=(m // tm,),
        in_specs=[pl.BlockSpec((tm, k), lambda i: (i, 0)), _resident((k, d)), _resident((1, d)),
                  pl.BlockSpec((tm, d), lambda i: (i, 0)), _resident((1, d))],
        out_specs=[pl.BlockSpec((tm, d), lambda i: (i, 0)), pl.BlockSpec((tm, d), lambda i: (i, 0))],
        out_shape=[jax.ShapeDtypeStruct((m, d), f32), jax.ShapeDtypeStruct((m, d), bf16)],
        compiler_params=_cparams(1),
        name=name,
    )(act, w, bias.reshape(1, d), x, next_w.reshape(1, d))


def _ffn_kernel(*refs, tm, tf, t_seq, tail, has_state):
    if has_state:
        (xn_ref, wu_ref, cw_ref, cb_ref, wd_ref, x_ref, nw_ref, e1_ref, e2_ref,
         xo_ref, xn2_ref, tail_ref, carry_ref) = refs
    else:
        (xn_ref, wu_ref, cw_ref, cb_ref, wd_ref, x_ref, nw_ref,
         xo_ref, xn2_ref, tail_ref, carry_ref) = refs
    i = pl.program_id(0)
    dff = wd_ref.shape[0]

    @pl.when(i == 0)
    def _():
        carry_ref[...] = jnp.zeros_like(carry_ref)

    xn = xn_ref[...]
    rows = lax.broadcasted_iota(jnp.int32, (tm, 1), 0)
    tpos = (i * tm + rows) & (t_seq - 1)
    gs = []
    for c0 in range(0, dff, tf):
        cs = slice(c0, c0 + tf)
        a = jnp.dot(xn, wu_ref[:, cs], preferred_element_type=f32)
        b = jnp.dot(xn, wu_ref[:, dff + c0:dff + c0 + tf], preferred_element_type=f32)
        tail_ref[0, :, cs] = a[tm - tail:, :]
        carry = carry_ref[:, cs]
        p1 = jnp.where(rows == 0, carry[7:8, :], pltpu.roll(a, 1, 0))
        p2 = jnp.where(rows == 0, carry[6:7, :], jnp.where(rows == 1, carry[7:8, :], pltpu.roll(a, 2, 0)))
        if has_state:
            p1 = jnp.where(tpos == 0, e1_ref[:, cs], p1)
            p2 = jnp.where(tpos < 2, e2_ref[:, cs], p2)
        else:
            p1 = jnp.where(tpos == 0, 0.0, p1)
            p2 = jnp.where(tpos < 2, 0.0, p2)
        carry_ref[:, cs] = a[tm - SUBLANES:, :]
        y = cb_ref[:, cs] + p2 * cw_ref[0:1, cs]
        y = y + p1 * cw_ref[1:2, cs]
        y = y + a * cw_ref[2:3, cs]
        gs.append((_silu(y) * b).astype(bf16))
    g = jnp.concatenate(gs, axis=1)
    xo = jnp.dot(g, wd_ref[...], preferred_element_type=f32) + x_ref[...]
    xo_ref[...] = xo
    xn2_ref[...] = _rms(xo, nw_ref[...]).astype(xn2_ref.dtype)


def ffn_call(xn, w_up, conv_w, conv_b, w_down, x, next_w, t_seq, e1, e2, out_dtype, name):
    m, d = xn.shape
    dff = w_down.shape[0]
    tf = 2 * LANES
    assert dff % tf == 0 and (t_seq & (t_seq - 1)) == 0
    has_state = e1 is not None
    tm = min(FFN_TM // 2 if has_state else FFN_TM, m)
    assert tm % t_seq == 0 or t_seq % tm == 0
    tail = tm if has_state else SUBLANES
    row = lambda w: pl.BlockSpec((tm, w), lambda i: (i, 0))
    in_specs = [row(d), _resident((d, 2 * dff)), _resident((3, dff)), _resident((1, dff)),
                _resident((dff, d)), row(d), _resident((1, d))]
    args = [xn, w_up, conv_w, conv_b.reshape(1, dff), w_down, x, next_w.reshape(1, d)]
    if has_state:
        in_specs += [row(dff), row(dff)]
        args += [e1, e2]
    kern = functools.partial(_ffn_kernel, tm=tm, tf=tf, t_seq=t_seq, tail=tail, has_state=has_state)
    return pl.pallas_call(
        kern,
        grid=(m // tm,),
        in_specs=in_specs,
        out_specs=[row(d), row(d), pl.BlockSpec((1, tail, dff), lambda i: (i, 0, 0))],
        out_shape=[jax.ShapeDtypeStruct((m, d), f32),
                   jax.ShapeDtypeStruct((m, d), out_dtype),
                   jax.ShapeDtypeStruct((m // tm, tail, dff), f32)],
        scratch_shapes=[pltpu.VMEM((SUBLANES, dff), f32)],
        compiler_params=_cparams(1),
        name=name,
    )(*args)


def _attn_softmax_pv(s, sink_col, vj):
    m = jnp.maximum(jnp.max(s, axis=-1, keepdims=True), sink_col)
    p = jnp.exp(s - m)
    l = jnp.sum(p, axis=-1, keepdims=True) + jnp.exp(sink_col - m)
    return _dot(p, vj) / l


def _sink_col(sinks_ref, j, group, rows_per_head):
    r = lax.broadcasted_iota(jnp.int32, (group * rows_per_head, 1), 0)
    col = jnp.full((group * rows_per_head, 1), sinks_ref[j * group], f32)
    for p in range(1, group):
        col = jnp.where(r >= p * rows_per_head, sinks_ref[j * group + p], col)
    return col


def _attn_prompt_kernel(sinks_ref, q_ref, kvc_ref, kvp_ref, o_ref, *, n_heads):
    n = pl.program_id(1)
    w, hd, kvh = WINDOW, ATT_HEAD_DIM, ATT_KV_HEADS
    group = n_heads // kvh
    q = q_ref[...] * (hd ** -0.5)
    kvc = kvc_ref[...]
    kvp = kvp_ref[...]
    kv = jnp.concatenate([kvp, kvc], axis=0)
    i = lax.broadcasted_iota(jnp.int32, (w, 2 * w), 0)
    c = lax.broadcasted_iota(jnp.int32, (w, 2 * w), 1)
    band = (c > i) & (c <= i + w) & ((n > 0) | (c >= w))
    mask = jnp.concatenate([band] * group, axis=0)
    for j in range(kvh):
        kj = kv[:, j * hd:(j + 1) * hd]
        vj = kv[:, (kvh + j) * hd:(kvh + j + 1) * hd]
        qs = jnp.concatenate([q[:, (j * group + p) * hd:(j * group + p + 1) * hd] for p in range(group)], axis=0)
        s = jnp.where(mask, _dot_nt(qs, kj), -jnp.inf)
        o = _attn_softmax_pv(s, _sink_col(sinks_ref, j, group, w), vj)
        for p in range(group):
            h = j * group + p
            o_ref[:, h * hd:(h + 1) * hd] = o[p * w:(p + 1) * w, :].astype(o_ref.dtype)


def attn_prompt_call(qkv, sinks, bsz, t, n_heads):
    w, hd, kvh = WINDOW, ATT_HEAD_DIM, ATT_KV_HEADS
    nb = t // w
    qd = n_heads * hd
    kvd = 2 * kvh * hd
    assert qd % kvd == 0
    kvblk = qd // kvd
    kern = functools.partial(_attn_prompt_kernel, n_heads=n_heads)
    return pl.pallas_call(
        kern,
        grid=(bsz, nb),
        in_specs=[pl.BlockSpec(memory_space=pltpu.SMEM),
                  pl.BlockSpec((w, qd), lambda b, n: (b * nb + n, 0)),
                  pl.BlockSpec((w, kvd), lambda b, n: (b * nb + n, kvblk)),
                  pl.BlockSpec((w, kvd), lambda b, n: (b * nb + jnp.maximum(n - 1, 0), kvblk))],
        out_specs=pl.BlockSpec((w, qd), lambda b, n: (b * nb + n, 0)),
        out_shape=jax.ShapeDtypeStruct((bsz * t, qd), bf16),
        compiler_params=_cparams(2),
        name="attn_prompt",
    )(sinks, qkv, qkv, qkv)


def _attn_cache_kernel(sinks_ref, q_ref, ck_ref, cv_ref, o_ref, *, n_heads, t_valid, bt):
    w, hd, kvh = WINDOW, ATT_HEAD_DIM, ATT_KV_HEADS
    group = n_heads // kvh
    tp = q_ref.shape[1]
    qd = n_heads * hd
    tq = lax.broadcasted_iota(jnp.int32, (tp, w), 0)
    jc = lax.broadcasted_iota(jnp.int32, (tp, w), 1)
    mc = (jc > tq) & (jc - w + PAST_LEN >= 0)
    tq2 = lax.broadcasted_iota(jnp.int32, (tp, tp), 0)
    un = lax.broadcasted_iota(jnp.int32, (tp, tp), 1)
    mn = (un <= tq2) & (un < t_valid)
    mask_c = jnp.concatenate([mc] * n_heads, axis=0)
    mask_n = jnp.concatenate([mn] * n_heads, axis=0)
    kvw = kvh * hd
    assert group == kvh
    slot = lax.broadcasted_iota(jnp.int32, (tp, kvw), 1) >> (hd.bit_length() - 1)
    sink = _sink_col(sinks_ref, 0, n_heads, tp)

    for bi in range(bt):
        row = q_ref[bi]
        ck = ck_ref[bi]
        cv = cv_ref[bi]
        kn = row[:, qd:qd + kvw]
        vn = row[:, qd + kvw:]
        pieces = []
        for j in range(kvh):
            qg = row[:, j * kvw:(j + 1) * kvw] * (hd ** -0.5)
            for p in range(group):
                pieces.append(jnp.where(slot == j, pltpu.roll(qg, ((j - p) % kvh) * hd, 1), 0.0))
        qs = jnp.concatenate(pieces, axis=0)
        sc = jnp.where(mask_c, _dot_nt(qs, ck), -jnp.inf)
        sn = jnp.where(mask_n, _dot_nt(qs, kn), -jnp.inf)
        m = jnp.maximum(jnp.maximum(jnp.max(sc, axis=-1, keepdims=True),
                                    jnp.max(sn, axis=-1, keepdims=True)), sink)
        pc = jnp.exp(sc - m)
        pn = jnp.exp(sn - m)
        l = jnp.sum(pc, axis=-1, keepdims=True) + jnp.sum(pn, axis=-1, keepdims=True) + jnp.exp(sink - m)
        of = (_dot(pc, cv) + _dot(pn, vn)) / l
        for j in range(kvh):
            og = jnp.zeros((tp, kvw), f32)
            for p in range(group):
                h = j * group + p
                og = og + jnp.where(slot == p, pltpu.roll(of[h * tp:(h + 1) * tp, :], ((p - j) % kvh) * hd, 1), 0.0)
            o_ref[bi, :, j * kvw:(j + 1) * kvw] = og


def attn_cache_call(qkv, cache_k, cache_v, sinks, n_heads, t_valid):
    bsz, tp, nq = qkv.shape
    w = WINDOW
    kd = cache_k.shape[-1]
    qd = n_heads * ATT_HEAD_DIM
    bt = 8
    kern = functools.partial(_attn_cache_kernel, n_heads=n_heads, t_valid=t_valid, bt=bt)
    return pl.pallas_call(
        kern,
        grid=(bsz // bt,),
        in_specs=[pl.BlockSpec(memory_space=pltpu.SMEM),
                  pl.BlockSpec((bt, tp, nq), lambda b: (b, 0, 0)),
                  pl.BlockSpec((bt, w, kd), lambda b: (b, 0, 0)),
                  pl.BlockSpec((bt, w, kd), lambda b: (b, 0, 0))],
        out_specs=pl.BlockSpec((bt, tp, qd), lambda b: (b, 0, 0)),
        out_shape=jax.ShapeDtypeStruct((bsz, tp, qd), f32),
        compiler_params=_cparams(1),
        name="attn_cache",
    )(sinks, qkv, cache_k, cache_v)


def _hgrn_kernel(*refs, c_rows, grp, t_valid, t_pad, layer, has_s0):
    if has_s0:
        p_ref, lbl_ref, nw_ref, s0_ref, y_ref, so_ref, st_ref = refs
    else:
        p_ref, lbl_ref, nw_ref, y_ref, so_ref, st_ref = refs
    c = pl.program_id(1)
    nc = pl.num_programs(1)
    nh = HG_HEADS
    dk = p_ref.shape[1] // 4 // nh
    dim = nh * dk
    ng = c_rows // grp

    @pl.when(c == 0)
    def _():
        if has_s0:
            for h in range(nh):
                st_ref[h] = s0_ref[0, h].T
        else:
            st_ref[...] = jnp.zeros_like(st_ref)

    lbl = lbl_ref[...]
    e = jnp.exp(lbl - jnp.max(lbl, axis=0, keepdims=True))
    sm = e / jnp.sum(e, axis=0, keepdims=True)
    lb = jnp.zeros((1, dim), f32)
    for r in range(1, layer + 1):
        lb = lb + sm[r:r + 1, :]

    x = p_ref[...]
    if t_valid < t_pad:
        valid = (c * c_rows + lax.broadcasted_iota(jnp.int32, (c_rows, 1), 0)) < t_valid
        x = jnp.where(valid, x, 0.0)
    q = _silu(x[:, 0:dim])
    fg = lb + (1.0 - lb) * jax.nn.sigmoid(x[:, dim:2 * dim])
    logf = jnp.log(fg)
    k = 1.0 - fg
    v = x[:, 2 * dim:3 * dim]
    gate = x[:, 3 * dim:4 * dim]
    if t_valid < t_pad:
        logf = jnp.where(valid, logf, 0.0)
        k = jnp.where(valid, k, 0.0)

    tril = _tril_mask(c_rows)
    bcum = _cumsum_rows(logf, tril.astype(bf16))
    rr = lax.broadcasted_iota(jnp.int32, (c_rows, c_rows), 0)
    ss = lax.broadcasted_iota(jnp.int32, (c_rows, c_rows), 1)
    gshift = grp.bit_length() - 1
    diag_mask = tril & ((rr >> gshift) == (ss >> gshift))
    nw = nw_ref[...]

    def rows_of(vals):
        return jnp.concatenate([jnp.broadcast_to(r, (grp, dk)) for r in vals], axis=0)

    for h in range(nh):
        sl = slice(h * dk, (h + 1) * dk)
        b = bcum[:, sl]
        qh, kh, vh = q[:, sl], k[:, sl], v[:, sl]
        zero = jnp.zeros((1, dk), f32)
        r = [zero] + [b[i * grp - 1:i * grp, :] for i in range(1, ng)] + [b[c_rows - 1:c_rows, :]]
        mid = [b[i * grp + grp // 2 - 1:i * grp + grp // 2, :] for i in range(ng)]
        b_last = r[ng]
        r_start = rows_of(r[:ng])
        r_end = rows_of(r[1:])
        r_mid = rows_of(mid)
        qg = qh * jnp.exp(b - r_start)
        kt = kh * jnp.exp(r_end - b)
        qm = qh * jnp.exp(b - r_mid)
        km = kh * jnp.exp(r_mid - b)
        a = jnp.where(diag_mask, _dot_nt(qm, km), 0.0)
        if ng > 1:
            lhs, rhs = [], []
            for j in range(ng - 1):
                lhs.append(jnp.concatenate(
                    [jnp.zeros((grp, dk), f32) if i <= j else qg[i * grp:(i + 1) * grp, :] * jnp.exp(r[i] - r[j + 1])
                     for i in range(ng)], axis=0))
                rhs.append(jnp.concatenate(
                    [kt[i * grp:(i + 1) * grp, :] if i == j else jnp.zeros((grp, dk), f32) for i in range(ng)], axis=0))
            a = a + _dot_nt(jnp.concatenate(lhs, axis=1), jnp.concatenate(rhs, axis=1))
        q_inter = qg * jnp.exp(r_start)
        st = st_ref[h]
        o = _dot(a, vh) + _dot_nt(q_inter, st)
        k_dec = kt * jnp.exp(b_last - r_end)
        st_new = st * jnp.exp(b_last) + _dot_tn(vh, k_dec)
        st_ref[h] = st_new
        on = o * lax.rsqrt(jnp.mean(o * o, axis=-1, keepdims=True) + RMS_EPS)
        y_ref[:, sl] = (on * nw[:, sl] * _silu(gate[:, sl])).astype(y_ref.dtype)

    @pl.when(c == nc - 1)
    def _():
        for h in range(nh):
            so_ref[0, h] = st_ref[h].T


def hgrn_call(p, lb_logits, norm_w, s0, bsz, t_pad, t_valid, layer):
    dim = p.shape[1] // 4
    nh = HG_HEADS
    dk = dim // nh
    c_rows = min(128, t_pad)
    grp = min(HG_CHUNK_GROUP, c_rows)
    nc = t_pad // c_rows
    has_s0 = s0 is not None
    in_specs = [pl.BlockSpec((c_rows, 4 * dim), lambda b, c: (b * nc + c, 0)),
                pl.BlockSpec(lb_logits.shape, lambda b, c: (0, 0)),
                pl.BlockSpec((1, dim), lambda b, c: (0, 0))]
    args = [p, lb_logits, norm_w.reshape(1, dim)]
    if has_s0:
        in_specs.append(pl.BlockSpec((1, nh, dk, dk), lambda b, c: (b, 0, 0, 0)))
        args.append(s0)
    kern = functools.partial(_hgrn_kernel, c_rows=c_rows, grp=grp, t_valid=t_valid, t_pad=t_pad,
                             layer=layer, has_s0=has_s0)
    return pl.pallas_call(
        kern,
        grid=(bsz, nc),
        in_specs=in_specs,
        out_specs=[pl.BlockSpec((c_rows, dim), lambda b, c: (b * nc + c, 0)),
                   pl.BlockSpec((1, nh, dk, dk), lambda b, c: (b, 0, 0, 0))],
        out_shape=[jax.ShapeDtypeStruct((bsz * t_pad, dim), bf16),
                   jax.ShapeDtypeStruct((bsz, nh, dk, dk), f32)],
        scratch_shapes=[pltpu.VMEM((nh, dk, dk), f32)],
        compiler_params=_cparams(2),
        name="hgrn",
    )(*args)


def _expand_heads(w, ex_ref):
    hi = w.astype(bf16)
    r1 = w - hi.astype(f32)
    mid = r1.astype(bf16)
    lo = (r1 - mid.astype(f32)).astype(bf16)
    return jnp.dot(jnp.concatenate([hi, mid, lo], axis=1), ex_ref[...], preferred_element_type=f32)


def _ssd_kernel(*refs, c_rows, t_valid, t_pad, has_s0):
    if has_s0:
        (zx_ref, cw_ref, cb_ref, dtb_ref, alog_ref, dsk_ref, nw_ref, ex_ref, s0_ref, buf_ref,
         y_ref, so_ref, st_ref, carry_ref) = refs
    else:
        (zx_ref, cw_ref, cb_ref, dtb_ref, alog_ref, dsk_ref, nw_ref, ex_ref,
         y_ref, so_ref, st_ref, carry_ref) = refs
    c = pl.program_id(1)
    nc = pl.num_programs(1)
    hp, ns, ngr = SSD_HEAD_DIM, SSD_STATE, SSD_GROUPS
    di = y_ref.shape[1]
    nh = di // hp
    hpg = nh // ngr
    gw = ngr * ns
    kconv = cw_ref.shape[0]
    pair = 2 * hp
    assert pair == LANES and hpg % 2 == 0

    @pl.when(c == 0)
    def _():
        carry_ref[...] = jnp.zeros_like(carry_ref)
        if has_s0:
            for r in range(di // LANES):
                st_ref[:, r * LANES:(r + 1) * LANES] = s0_ref[0, r * LANES:(r + 1) * LANES, :].T
            carry_ref[SUBLANES - (kconv - 1):, :] = buf_ref[0]
        else:
            st_ref[...] = jnp.zeros_like(st_ref)

    zx = zx_ref[...]
    z = zx[:, :di]
    cdim = cw_ref.shape[1]
    u = zx[:, di:di + cdim]
    rows = lax.broadcasted_iota(jnp.int32, (c_rows, 1), 0)
    cr = carry_ref[...]
    cw = cw_ref[...]
    yc = cb_ref[...]
    for kk in range(kconv - 1):
        back = kconv - 1 - kk
        sh = pltpu.roll(u, back, 0)
        for r0 in range(back):
            sh = jnp.where(rows == r0, cr[SUBLANES - back + r0:SUBLANES - back + r0 + 1, :], sh)
        yc = yc + sh * cw[kk:kk + 1, :]
    yc = yc + u * cw[kconv - 1:kconv, :]
    carry_ref[...] = u[c_rows - SUBLANES:, :]
    xbc = _silu(yc)
    dt = jax.nn.softplus(zx[:, di + cdim:] + dtb_ref[...])
    if t_valid < t_pad:
        valid = (c * c_rows + rows) < t_valid
        xbc = jnp.where(valid, xbc, 0.0)
        dt = jnp.where(valid, dt, 0.0)
    xs = xbc[:, :di]
    bm = xbc[:, di:di + gw]
    cm = xbc[:, di + gw:di + 2 * gw]
    a_neg = -jnp.exp(alog_ref[...])
    tril = _tril_mask(c_rows)
    acum = _cumsum_rows(dt * a_neg, tril.astype(bf16))

    def head_rows(a):
        if c_rows < LANES:
            a = jnp.concatenate([a, jnp.zeros((LANES - c_rows, LANES), f32)], axis=0)
        return a.T[:, :c_rows]

    acum_t = head_rows(acum)
    dt_t = head_rows(dt)
    a_last = acum[c_rows - 1:c_rows, :]
    xd = xs * _expand_heads(dt * jnp.exp(a_last - acum), ex_ref)
    dsk = dsk_ref[...]
    lo = lax.broadcasted_iota(jnp.int32, (c_rows, LANES), 1) < hp

    ys = []
    for g in range(ngr):
        bg = bm[:, g * ns:(g + 1) * ns]
        cg = cm[:, g * ns:(g + 1) * ns]
        cb_mat = _dot_nt(cg, bg)
        gsl = slice(g * hpg * hp, (g + 1) * hpg * hp)
        y_inter = _dot(cg, st_ref[:, gsl])
        decs = []
        for q in range(hpg // 2):
            h1 = g * hpg + 2 * q
            psl = slice(h1 * hp, h1 * hp + pair)
            ms, es = [], []
            for h in (h1, h1 + 1):
                colb = jnp.broadcast_to(acum[:, h:h + 1], (c_rows, LANES))
                seg = colb[:, :c_rows] - acum_t[h:h + 1, :]
                ms.append(cb_mat * jnp.exp(jnp.where(tril, seg, -jnp.inf)) * dt_t[h:h + 1, :])
                es.append(jnp.exp(colb))
            xp = xs[:, psl]
            x_lo = jnp.where(lo, xp, 0.0)
            x_hi = jnp.where(lo, 0.0, xp)
            if (2 * c_rows) % LANES == 0:
                y_intra = _dot(jnp.concatenate(ms, axis=1), jnp.concatenate([x_lo, x_hi], axis=0))
            else:
                y_intra = _dot(ms[0], x_lo) + _dot(ms[1], x_hi)
            y_pair = y_intra + y_inter[:, q * pair:(q + 1) * pair] * jnp.where(lo, es[0], es[1])
            ys.append(y_pair + dsk[:, psl] * xp)
            decs.append(jnp.where(lo[0:1, :], jnp.exp(a_last[:, h1:h1 + 1]), jnp.exp(a_last[:, h1 + 1:h1 + 2])))
        st_ref[:, gsl] = st_ref[:, gsl] * jnp.concatenate(decs, axis=1) + _dot_tn(bg, xd[:, gsl])

    y = jnp.concatenate(ys, axis=1) * _silu(z)
    nw = nw_ref[...]
    gdim = di // ngr
    for g in range(ngr):
        sl = slice(g * gdim, (g + 1) * gdim)
        yg = y[:, sl]
        y_ref[:, sl] = (yg * lax.rsqrt(jnp.mean(yg * yg, axis=-1, keepdims=True) + RMS_EPS) * nw[:, sl]).astype(y_ref.dtype)

    @pl.when(c == nc - 1)
    def _():
        for r in range(di // LANES):
            so_ref[0, r * LANES:(r + 1) * LANES, :] = st_ref[:, r * LANES:(r + 1) * LANES].T


def ssd_call(zx, conv_w, conv_b, dt_bias, a_log, d_skip, norm_w, s0, conv_buf, bsz, t_pad, t_valid):
    hp, ns = SSD_HEAD_DIM, SSD_STATE
    conv_dim = conv_w.shape[1]
    di = zx.shape[1] - conv_dim - LANES
    nh = di // hp
    c_rows = min(128, t_pad)
    nc = t_pad // c_rows
    has_s0 = s0 is not None
    pad_h = lambda a: jnp.pad(a.reshape(1, nh), ((0, 0), (0, LANES - nh)))
    full = lambda shape: pl.BlockSpec(shape, lambda b, c: (0,) * len(shape))
    sel = (jnp.arange(di)[None, :] // hp == jnp.arange(LANES)[:, None]).astype(bf16)
    expand = jnp.concatenate([sel, sel, sel], axis=0)
    in_specs = [pl.BlockSpec((c_rows, zx.shape[1]), lambda b, c: (b * nc + c, 0)),
                full(conv_w.shape), full((1, conv_dim)), full((1, LANES)), full((1, LANES)),
                full((1, di)), full((1, di)), full(expand.shape)]
    args = [zx, conv_w, conv_b.reshape(1, conv_dim), pad_h(dt_bias), pad_h(a_log),
            jnp.repeat(d_skip, hp).reshape(1, di), norm_w.reshape(1, di), expand]
    if has_s0:
        in_specs += [pl.BlockSpec((1, di, ns), lambda b, c: (b, 0, 0)),
                     pl.BlockSpec((1,) + conv_buf.shape[1:], lambda b, c: (b, 0, 0))]
        args += [s0.reshape(bsz, di, ns), conv_buf]
    kern = functools.partial(_ssd_kernel, c_rows=c_rows, t_valid=t_valid, t_pad=t_pad, has_s0=has_s0)
    y, s_new = pl.pallas_call(
        kern,
        grid=(bsz, nc),
        in_specs=in_specs,
        out_specs=[pl.BlockSpec((c_rows, di), lambda b, c: (b * nc + c, 0)),
                   pl.BlockSpec((1, di, ns), lambda b, c: (b, 0, 0))],
        out_shape=[jax.ShapeDtypeStruct((bsz * t_pad, di), bf16),
                   jax.ShapeDtypeStruct((bsz, di, ns), f32)],
        scratch_shapes=[pltpu.VMEM((ns, di), f32), pltpu.VMEM((SUBLANES, conv_dim), f32)],
        compiler_params=_cparams(2),
        name="ssd",
    )(*args)
    return y, s_new.reshape(bsz, nh, hp, ns)


def _trunk(x, c_k, c_v, s_hg, s_ssm, s_sconv, s_fconv, prm):
    bsz, t, d = x.shape
    depth = prm["norm_mix_w"].shape[0]
    decode = c_k is not None
    tp = -(-t // SUBLANES) * SUBLANES
    if tp != t:
        x = jnp.pad(x, ((0, 0), (0, tp - t), (0, 0)))
    m = bsz * tp
    n_heads = d // ATT_HEAD_DIM
    qd = n_heads * ATT_HEAD_DIM
    kvd = ATT_KV_HEADS * ATT_HEAD_DIM
    x2 = x.reshape(m, d)
    xn = rmsnorm_call(x2, prm["norm_mix_w"][0], bf16)
    nk, nv, nhg, nssm, nsconv, nfconv = [], [], [], [], [], []
    slot = [0] * N_MIXERS
    for i in range(depth):
        kind = i % N_MIXERS
        j = slot[kind]
        slot[kind] += 1
        nw_ffn = prm["norm_ffn_w"][i]
        if kind == 0:
            w_qkv, b_qkv = prm["attn_w_qkv"][j], prm["attn_b_qkv"][j]
            nq = w_qkv.shape[1]
            if decode:
                qkv = proj_call(xn, w_qkv, b_qkv, f32, "attn_qkv").reshape(bsz, tp, nq)
                ck = c_k[j].reshape(bsz, WINDOW, kvd)
                cv = c_v[j].reshape(bsz, WINDOW, kvd)
                o = attn_cache_call(qkv, ck, cv, prm["attn_sinks"][j], n_heads, t).reshape(m, qd)
                to4 = lambda a: a.reshape(bsz, t, ATT_KV_HEADS, ATT_HEAD_DIM)
                k_new = jnp.concatenate([c_k[j][:, t:], to4(qkv[:, :t, qd:qd + kvd])], axis=1)
                v_new = jnp.concatenate([c_v[j][:, t:], to4(qkv[:, :t, qd + kvd:])], axis=1)
            else:
                qkv = proj_call(xn, w_qkv, b_qkv, bf16, "attn_qkv")
                o = attn_prompt_call(qkv, prm["attn_sinks"][j], bsz, t, n_heads)
                xn_last = xn.reshape(bsz, t, d)[:, t - WINDOW:].reshape(bsz * WINDOW, d)
                kv_last = proj_call(xn_last, w_qkv[:, qd:], b_qkv[qd:], f32, "attn_kv_tail").reshape(bsz, WINDOW, 2, kvd)
                to4 = lambda a: a.reshape(bsz, WINDOW, ATT_KV_HEADS, ATT_HEAD_DIM)
                k_new, v_new = to4(kv_last[:, :, 0]), to4(kv_last[:, :, 1])
            nk.append(k_new)
            nv.append(v_new)
            x2, xn = outproj_call(o, prm["attn_w_o"][j], prm["attn_b_o"][j], x2, nw_ffn, "attn_out")
        elif kind == 1:
            p = proj_call(xn, prm["hgrn_w_in"][j], None, f32, "hgrn_in")
            y, s_new = hgrn_call(p, prm["hgrn_lb_logits"], prm["hgrn_norm_w"][j],
                                 None if s_hg is None else s_hg[j], bsz, tp, t, i)
            nhg.append(s_new)
            x2, xn = outproj_call(y, prm["hgrn_w_o"][j], None, x2, nw_ffn, "hgrn_out")
        else:
            conv_dim = prm["ssd_conv_w"].shape[2]
            di = prm["ssd_w_o"].shape[1]
            zx = proj_call(xn, prm["ssd_w_in"][j], None, f32, "ssd_in")
            y, s_new = ssd_call(zx, prm["ssd_conv_w"][j], prm["ssd_conv_b"][j], prm["ssd_dt_bias"][j],
                                prm["ssd_a_log"][j], prm["ssd_d"][j], prm["ssd_norm_w"][j],
                                None if s_ssm is None else s_ssm[j],
                                None if s_sconv is None else s_sconv[j], bsz, tp, t)
            nssm.append(s_new)
            kc = prm["ssd_conv_w"].shape[1]
            pre = zx.reshape(bsz, tp, zx.shape[1])[:, max(t - (kc - 1), 0):t, di:di + conv_dim]
            if s_sconv is not None:
                pre = jnp.concatenate([s_sconv[j], pre], axis=1)
            nsconv.append(pre[:, pre.shape[1] - (kc - 1):])
            x2, xn = outproj_call(y, prm["ssd_w_o"][j], None, x2, nw_ffn, "ssd_out")
        last = i == depth - 1
        next_w = prm["norm_final_w"] if last else prm["norm_mix_w"][i + 1]
        dff = prm["ffn_w_down"].shape[1]
        kf = prm["ffn_conv_w"].shape[1]
        if s_fconv is not None:
            buf = s_fconv[i]
            e1 = jnp.pad(buf[:, 1:2], ((0, 0), (0, tp - 1), (0, 0))).reshape(m, dff)
            e2 = jnp.pad(buf, ((0, 0), (0, tp - 2), (0, 0))).reshape(m, dff)
        else:
            e1 = e2 = None
        x2, xn, tail = ffn_call(xn, prm["ffn_w_up"][i], prm["ffn_conv_w"][i], prm["ffn_conv_b"][i],
                                prm["ffn_w_down"][i], x2, next_w, tp, e1, e2, f32 if last else bf16, "ffn")
        if s_fconv is not None:
            a_full = tail.reshape(bsz, tp, dff)[:, :t]
            nfconv.append(jnp.concatenate([s_fconv[i], a_full], axis=1)[:, t:])
        else:
            tiles_per_seq = tp // tail_tile_rows(m)
            tl = tail.reshape(bsz, tiles_per_seq, SUBLANES, dff)[:, -1]
            nfconv.append(tl[:, SUBLANES - (kf - 1):])
    stack = lambda xs: xs[0][None] if len(xs) == 1 else jnp.stack(xs)
    y = xn.reshape(bsz, tp, d)[:, :t]
    return (y, stack(nk), stack(nv), stack(nhg), stack(nssm), stack(nsconv), stack(nfconv))


def tail_tile_rows(m):
    return min(FFN_TM, m)


def kernel(x_prompt, x_sample, cache_attn_k, cache_attn_v, state_hgrn, state_ssm, state_ssm_conv, state_ffn_conv, norm_mix_w, norm_ffn_w, norm_final_w, attn_w_qkv, attn_b_qkv, attn_sinks, attn_w_o, attn_b_o, hgrn_w_in, hgrn_lb_logits, hgrn_norm_w, hgrn_w_o, ssd_w_in, ssd_conv_w, ssd_conv_b, ssd_dt_bias, ssd_a_log, ssd_d, ssd_norm_w, ssd_w_o, ffn_w_up, ffn_conv_w, ffn_conv_b, ffn_w_down):
    cast = lambda w: w.astype(bf16)
    ssd_in_pad = (-ssd_w_in.shape[2]) % LANES
    prm = dict(norm_mix_w=norm_mix_w, norm_ffn_w=norm_ffn_w, norm_final_w=norm_final_w,
               attn_w_qkv=cast(attn_w_qkv), attn_b_qkv=attn_b_qkv, attn_sinks=attn_sinks,
               attn_w_o=cast(attn_w_o), attn_b_o=attn_b_o,
               hgrn_w_in=cast(hgrn_w_in), hgrn_lb_logits=hgrn_lb_logits, hgrn_norm_w=hgrn_norm_w,
               hgrn_w_o=cast(hgrn_w_o),
               ssd_w_in=jnp.pad(cast(ssd_w_in), ((0, 0), (0, 0), (0, ssd_in_pad))), ssd_conv_w=ssd_conv_w,
               ssd_conv_b=ssd_conv_b, ssd_dt_bias=ssd_dt_bias,
               ssd_a_log=ssd_a_log, ssd_d=ssd_d, ssd_norm_w=ssd_norm_w, ssd_w_o=cast(ssd_w_o),
               ffn_w_up=cast(ffn_w_up), ffn_conv_w=ffn_conv_w, ffn_conv_b=ffn_conv_b, ffn_w_down=cast(ffn_w_down))
    outs_p = _trunk(x_prompt, None, None, None, None, None, None, prm)
    outs_s = _trunk(x_sample, cache_attn_k, cache_attn_v, state_hgrn, state_ssm, state_ssm_conv,
                    state_ffn_conv, prm)
    return (outs_p[0], outs_s[0]) + outs_p[1:] + outs_s[1:]
```

```python
import functools

import jax
import jax.numpy as jnp
from jax import lax
from jax.experimental import pallas as pl
from jax.experimental.pallas import tpu as pltpu

bf16 = jnp.bfloat16
f32 = jnp.float32

RMS_EPS = 1e-6
N_MIXERS = 3
WINDOW = 128
PAST_LEN = 8192
ATT_HEAD_DIM = 64
ATT_KV_HEADS = 4
HG_HEADS = 8
HG_CHUNK_GROUP = 32
SSD_HEAD_DIM = 64
SSD_STATE = 128
SSD_GROUPS = 4

LANES = 128
SUBLANES = 8
VMEM_LIMIT = 56 * 1024 * 1024
PROJ_TM = 512
FFN_TM = 512


def _cparams(n_axes):
    return pltpu.CompilerParams(dimension_semantics=("arbitrary",) * n_axes,
                                vmem_limit_bytes=VMEM_LIMIT)


def _dot(a, b):
    return jnp.dot(a.astype(bf16), b.astype(bf16), preferred_element_type=f32)


def _dot_nt(a, b):
    return lax.dot_general(a.astype(bf16), b.astype(bf16), (((1,), (1,)), ((), ())),
                           preferred_element_type=f32)


def _dot_tn(a, b):
    return lax.dot_general(a.astype(bf16), b.astype(bf16), (((0,), (0,)), ((), ())),
                           preferred_element_type=f32)


def _silu(x):
    return x * jax.nn.sigmoid(x)


def _rms(x, w):
    ms = jnp.mean(x * x, axis=-1, keepdims=True)
    return x * lax.rsqrt(ms + RMS_EPS) * w


def _cumsum_rows(x, tril):
    hi = x.astype(bf16)
    r1 = x - hi.astype(f32)
    mid = r1.astype(bf16)
    lo = (r1 - mid.astype(f32)).astype(bf16)
    d = functools.partial(jnp.dot, preferred_element_type=f32)
    return d(tril, hi) + d(tril, mid) + d(tril, lo)


def _tril_mask(c):
    r = lax.broadcasted_iota(jnp.int32, (c, c), 0)
    s = lax.broadcasted_iota(jnp.int32, (c, c), 1)
    return r >= s


def _norm_kernel(x_ref, w_ref, o_ref):
    o_ref[...] = _rms(x_ref[...], w_ref[...]).astype(o_ref.dtype)


def rmsnorm_call(x, w, out_dtype):
    m, d = x.shape
    tm = min(1024, m)
    return pl.pallas_call(
        _norm_kernel,
        grid=(m // tm,),
        in_specs=[pl.BlockSpec((tm, d), lambda i: (i, 0)), pl.BlockSpec((1, d), lambda i: (0, 0))],
        out_specs=pl.BlockSpec((tm, d), lambda i: (i, 0)),
        out_shape=jax.ShapeDtypeStruct((m, d), out_dtype),
        compiler_params=_cparams(1),
        name="rmsnorm",
    )(x, w.reshape(1, d))


def _resident(shape):
    return pl.BlockSpec(shape, lambda *_: (0,) * len(shape), pipeline_mode=pl.Buffered(1))


def _proj_kernel(x_ref, w_ref, b_ref, o_ref):
    acc = jnp.dot(x_ref[...], w_ref[...], preferred_element_type=f32)
    o_ref[...] = (acc + b_ref[...]).astype(o_ref.dtype)


def proj_call(xn, w, bias, out_dtype, name):
    m, k = xn.shape
    n = w.shape[1]
    tm = min(PROJ_TM, m)
    if bias is None:
        bias = jnp.zeros((n,), f32)
    return pl.pallas_call(
        _proj_kernel,
        grid=(m // tm,),
        in_specs=[pl.BlockSpec((tm, k), lambda i: (i, 0)), _resident((k, n)), _resident((1, n))],
        out_specs=pl.BlockSpec((tm, n), lambda i: (i, 0)),
        out_shape=jax.ShapeDtypeStruct((m, n), out_dtype),
        compiler_params=_cparams(1),
        name=name,
    )(xn, w, bias.reshape(1, n))


def _ffn_kernel(*refs, tm, tf, t_seq, tail, has_state):
    if has_state:
        (act_ref, wo_ref, bo_ref, x_ref, nw1_ref, wu_ref, cw_ref, cb_ref, wd_ref, nw2_ref, e1_ref, e2_ref,
         xo_ref, xn2_ref, tail_ref, carry_ref) = refs
    else:
        (act_ref, wo_ref, bo_ref, x_ref, nw1_ref, wu_ref, cw_ref, cb_ref, wd_ref, nw2_ref,
         xo_ref, xn2_ref, tail_ref, carry_ref) = refs
    i = pl.program_id(0)
    dff = wd_ref.shape[0]

    x1 = jnp.dot(act_ref[...].astype(bf16), wo_ref[...], preferred_element_type=f32) + bo_ref[...] + x_ref[...]
    xn = _rms(x1, nw1_ref[...]).astype(bf16)
    if has_state:
        tpos = (lax.broadcasted_iota(jnp.int32, (tm, 1), 0)) & (t_seq - 1)
    else:
        @pl.when(((i * tm) & (t_seq - 1)) == 0)
        def _():
            carry_ref[...] = jnp.zeros_like(carry_ref)
        rows8 = lax.broadcasted_iota(jnp.int32, (SUBLANES, 1), 0)
    gs = []
    for c0 in range(0, dff, tf):
        cs = slice(c0, c0 + tf)
        a = jnp.dot(xn, wu_ref[:, cs], preferred_element_type=f32)
        b = jnp.dot(xn, wu_ref[:, dff + c0:dff + c0 + tf], preferred_element_type=f32)
        tail_ref[0, :, cs] = a[tm - tail:, :]
        r1 = pltpu.roll(a, 1, 0)
        r2 = pltpu.roll(a, 2, 0)
        if has_state:
            p1 = jnp.where(tpos == 0, e1_ref[:, cs], r1)
            p2 = jnp.where(tpos < 2, e2_ref[:, cs], r2)
        else:
            carry = carry_ref[:, cs]
            top1 = jnp.where(rows8 == 0, carry[7:8, :], r1[:SUBLANES])
            top2 = jnp.where(rows8 == 0, carry[6:7, :], jnp.where(rows8 == 1, carry[7:8, :], r2[:SUBLANES]))
            p1 = jnp.concatenate([top1, r1[SUBLANES:]], axis=0)
            p2 = jnp.concatenate([top2, r2[SUBLANES:]], axis=0)
            carry_ref[:, cs] = a[tm - SUBLANES:, :]
        y = cb_ref[:, cs] + p2 * cw_ref[0:1, cs]
        y = y + p1 * cw_ref[1:2, cs]
        y = y + a * cw_ref[2:3, cs]
        gs.append((_silu(y) * b).astype(bf16))
    g = jnp.concatenate(gs, axis=1)
    xo = jnp.dot(g, wd_ref[...], preferred_element_type=f32) + x1
    xo_ref[...] = xo
    xn2_ref[...] = _rms(xo, nw2_ref[...]).astype(xn2_ref.dtype)


def ffn_call(act, w_o, b_o, x, ffn_norm_w, w_up, conv_w, conv_b, w_down, next_w, t_seq, e1, e2, out_dtype, name):
    m, d = x.shape
    ko = act.shape[1]
    dff = w_down.shape[0]
    tf = 2 * LANES
    assert dff % tf == 0 and (t_seq & (t_seq - 1)) == 0
    has_state = e1 is not None
    tm = min(FFN_TM // 2 if has_state else FFN_TM, m)
    assert (tm % t_seq == 0) if has_state else (t_seq % tm == 0)
    tail = tm if has_state else SUBLANES
    if b_o is None:
        b_o = jnp.zeros((d,), f32)
    row = lambda w: pl.BlockSpec((tm, w), lambda i: (i, 0))
    in_specs = [row(ko), _resident((ko, d)), _resident((1, d)), row(d), _resident((1, d)),
                _resident((d, 2 * dff)), _resident((3, dff)), _resident((1, dff)),
                _resident((dff, d)), _resident((1, d))]
    args = [act, w_o, b_o.reshape(1, d), x, ffn_norm_w.reshape(1, d),
            w_up, conv_w, conv_b.reshape(1, dff), w_down, next_w.reshape(1, d)]
    if has_state:
        in_specs += [row(dff), row(dff)]
        args += [e1, e2]
    kern = functools.partial(_ffn_kernel, tm=tm, tf=tf, t_seq=t_seq, tail=tail, has_state=has_state)
    return pl.pallas_call(
        kern,
        grid=(m // tm,),
        in_specs=in_specs,
        out_specs=[row(d), row(d), pl.BlockSpec((1, tail, dff), lambda i: (i, 0, 0))],
        out_shape=[jax.ShapeDtypeStruct((m, d), f32),
                   jax.ShapeDtypeStruct((m, d), out_dtype),
                   jax.ShapeDtypeStruct((m // tm, tail, dff), f32)],
        scratch_shapes=[pltpu.VMEM((SUBLANES, dff), f32)],
        compiler_params=_cparams(1),
        name=name,
    )(*args)


def _attn_softmax_pv(s, sink_col, vj):
    m = jnp.maximum(jnp.max(s, axis=-1, keepdims=True), sink_col)
    p = jnp.exp(s - m)
    l = jnp.sum(p, axis=-1, keepdims=True) + jnp.exp(sink_col - m)
    return _dot(p, vj) / l


def _sink_col(sinks_ref, j, group, rows_per_head):
    r = lax.broadcasted_iota(jnp.int32, (group * rows_per_head, 1), 0)
    col = jnp.full((group * rows_per_head, 1), sinks_ref[j * group], f32)
    for p in range(1, group):
        col = jnp.where(r >= p * rows_per_head, sinks_ref[j * group + p], col)
    return col


def _attn_prompt_kernel(sinks_ref, q_ref, kvc_ref, kvp_ref, o_ref, *, n_heads):
    n = pl.program_id(1)
    w, hd, kvh = WINDOW, ATT_HEAD_DIM, ATT_KV_HEADS
    group = n_heads // kvh
    q = q_ref[...] * (hd ** -0.5)
    kvc = kvc_ref[...]
    kvp = kvp_ref[...]
    kv = jnp.concatenate([kvp, kvc], axis=0)
    i = lax.broadcasted_iota(jnp.int32, (w, 2 * w), 0)
    c = lax.broadcasted_iota(jnp.int32, (w, 2 * w), 1)
    band = (c > i) & (c <= i + w) & ((n > 0) | (c >= w))
    mask = jnp.concatenate([band] * group, axis=0)
    for j in range(kvh):
        kj = kv[:, j * hd:(j + 1) * hd]
        vj = kv[:, (kvh + j) * hd:(kvh + j + 1) * hd]
        qs = jnp.concatenate([q[:, (j * group + p) * hd:(j * group + p + 1) * hd] for p in range(group)], axis=0)
        s = jnp.where(mask, _dot_nt(qs, kj), -jnp.inf)
        o = _attn_softmax_pv(s, _sink_col(sinks_ref, j, group, w), vj)
        for p in range(group):
            h = j * group + p
            o_ref[:, h * hd:(h + 1) * hd] = o[p * w:(p + 1) * w, :].astype(o_ref.dtype)


def attn_prompt_call(qkv, sinks, bsz, t, n_heads):
    w, hd, kvh = WINDOW, ATT_HEAD_DIM, ATT_KV_HEADS
    nb = t // w
    qd = n_heads * hd
    kvd = 2 * kvh * hd
    assert qd % kvd == 0
    kvblk = qd // kvd
    kern = functools.partial(_attn_prompt_kernel, n_heads=n_heads)
    return pl.pallas_call(
        kern,
        grid=(bsz, nb),
        in_specs=[pl.BlockSpec(memory_space=pltpu.SMEM),
                  pl.BlockSpec((w, qd), lambda b, n: (b * nb + n, 0)),
                  pl.BlockSpec((w, kvd), lambda b, n: (b * nb + n, kvblk)),
                  pl.BlockSpec((w, kvd), lambda b, n: (b * nb + jnp.maximum(n - 1, 0), kvblk))],
        out_specs=pl.BlockSpec((w, qd), lambda b, n: (b * nb + n, 0)),
        out_shape=jax.ShapeDtypeStruct((bsz * t, qd), bf16),
        compiler_params=_cparams(2),
        name="attn_prompt",
    )(sinks, qkv, qkv, qkv)


def _attn_cache_kernel(sinks_ref, q_ref, ck_ref, cv_ref, o_ref, *, n_heads, t_valid, bt):
    w, hd, kvh = WINDOW, ATT_HEAD_DIM, ATT_KV_HEADS
    group = n_heads // kvh
    tp = q_ref.shape[1]
    qd = n_heads * hd
    tq = lax.broadcasted_iota(jnp.int32, (tp, w), 0)
    jc = lax.broadcasted_iota(jnp.int32, (tp, w), 1)
    mc = (jc > tq) & (jc - w + PAST_LEN >= 0)
    tq2 = lax.broadcasted_iota(jnp.int32, (tp, tp), 0)
    un = lax.broadcasted_iota(jnp.int32, (tp, tp), 1)
    mn = (un <= tq2) & (un < t_valid)
    mask_c = jnp.concatenate([mc] * n_heads, axis=0)
    mask_n = jnp.concatenate([mn] * n_heads, axis=0)
    kvw = kvh * hd
    assert group == kvh
    slot = lax.broadcasted_iota(jnp.int32, (tp, kvw), 1) >> (hd.bit_length() - 1)
    sink = _sink_col(sinks_ref, 0, n_heads, tp)

    for bi in range(bt):
        row = q_ref[bi]
        ck = ck_ref[bi]
        cv = cv_ref[bi]
        kn = row[:, qd:qd + kvw]
        vn = row[:, qd + kvw:]
        pieces = []
        for j in range(kvh):
            qg = row[:, j * kvw:(j + 1) * kvw] * (hd ** -0.5)
            for p in range(group):
                pieces.append(jnp.where(slot == j, pltpu.roll(qg, ((j - p) % kvh) * hd, 1), 0.0))
        qs = jnp.concatenate(pieces, axis=0)
        sc = jnp.where(mask_c, _dot(qs, ck), -jnp.inf)
        sn = jnp.where(mask_n, _dot_nt(qs, kn), -jnp.inf)
        m = jnp.maximum(jnp.maximum(jnp.max(sc, axis=-1, keepdims=True),
                                    jnp.max(sn, axis=-1, keepdims=True)), sink)
        pc = jnp.exp(sc - m)
        pn = jnp.exp(sn - m)
        l = jnp.sum(pc, axis=-1, keepdims=True) + jnp.sum(pn, axis=-1, keepdims=True) + jnp.exp(sink - m)
        of = (_dot_nt(pc, cv) + _dot(pn, vn)) / l
        for j in range(kvh):
            og = jnp.zeros((tp, kvw), f32)
            for p in range(group):
                h = j * group + p
                og = og + jnp.where(slot == p, pltpu.roll(of[h * tp:(h + 1) * tp, :], ((p - j) % kvh) * hd, 1), 0.0)
            o_ref[bi, :, j * kvw:(j + 1) * kvw] = og


def attn_cache_call(qkv, cache_k, cache_v, sinks, n_heads, t_valid):
    bsz, tp, nq = qkv.shape
    w = WINDOW
    kd = cache_k.shape[1]
    qd = n_heads * ATT_HEAD_DIM
    bt = 8
    kern = functools.partial(_attn_cache_kernel, n_heads=n_heads, t_valid=t_valid, bt=bt)
    return pl.pallas_call(
        kern,
        grid=(bsz // bt,),
        in_specs=[pl.BlockSpec(memory_space=pltpu.SMEM),
                  pl.BlockSpec((bt, tp, nq), lambda b: (b, 0, 0)),
                  pl.BlockSpec((bt, kd, w), lambda b: (b, 0, 0)),
                  pl.BlockSpec((bt, kd, w), lambda b: (b, 0, 0))],
        out_specs=pl.BlockSpec((bt, tp, qd), lambda b: (b, 0, 0)),
        out_shape=jax.ShapeDtypeStruct((bsz, tp, qd), f32),
        compiler_params=_cparams(1),
        name="attn_cache",
    )(sinks, qkv, cache_k, cache_v)


def _hgrn_kernel(*refs, c_rows, grp, t_valid, t_pad, layer, has_s0):
    if has_s0:
        p_ref, lbl_ref, nw_ref, s0_ref, y_ref, so_ref, st_ref = refs
    else:
        p_ref, lbl_ref, nw_ref, y_ref, so_ref, st_ref = refs
    c = pl.program_id(1)
    nc = pl.num_programs(1)
    nh = HG_HEADS
    dk = p_ref.shape[1] // 4 // nh
    dim = nh * dk
    ng = c_rows // grp

    @pl.when(c == 0)
    def _():
        if has_s0:
            for h in range(nh):
                st_ref[h] = s0_ref[0, h].T
        else:
            st_ref[...] = jnp.zeros_like(st_ref)

    lbl = lbl_ref[...]
    e = jnp.exp(lbl - jnp.max(lbl, axis=0, keepdims=True))
    sm = e / jnp.sum(e, axis=0, keepdims=True)
    lb = jnp.zeros((1, dim), f32)
    for r in range(1, layer + 1):
        lb = lb + sm[r:r + 1, :]

    x = p_ref[...]
    if t_valid < t_pad:
        valid = (c * c_rows + lax.broadcasted_iota(jnp.int32, (c_rows, 1), 0)) < t_valid
        x = jnp.where(valid, x, 0.0)
    q = _silu(x[:, 0:dim])
    fg = lb + (1.0 - lb) * jax.nn.sigmoid(x[:, dim:2 * dim])
    logf = jnp.log(fg)
    k = 1.0 - fg
    v = x[:, 2 * dim:3 * dim]
    gate = x[:, 3 * dim:4 * dim]
    if t_valid < t_pad:
        logf = jnp.where(valid, logf, 0.0)
        k = jnp.where(valid, k, 0.0)

    tril = _tril_mask(c_rows)
    bcum = _cumsum_rows(logf, tril.astype(bf16))
    rr = lax.broadcasted_iota(jnp.int32, (c_rows, c_rows), 0)
    ss = lax.broadcasted_iota(jnp.int32, (c_rows, c_rows), 1)
    gshift = grp.bit_length() - 1
    diag_mask = tril & ((rr >> gshift) == (ss >> gshift))
    nw = nw_ref[...]

    def rows_of(vals):
        return jnp.concatenate([jnp.broadcast_to(r, (grp, dk)) for r in vals], axis=0)

    for h in range(nh):
        sl = slice(h * dk, (h + 1) * dk)
        b = bcum[:, sl]
        qh, kh, vh = q[:, sl], k[:, sl], v[:, sl]
        zero = jnp.zeros((1, dk), f32)
        r = [zero] + [b[i * grp - 1:i * grp, :] for i in range(1, ng)] + [b[c_rows - 1:c_rows, :]]
        mid = [b[i * grp + grp // 2 - 1:i * grp + grp // 2, :] for i in range(ng)]
        b_last = r[ng]
        r_start = rows_of(r[:ng])
        r_end = rows_of(r[1:])
        r_mid = rows_of(mid)
        qg = qh * jnp.exp(b - r_start)
        kt = kh * jnp.exp(r_end - b)
        qm = qh * jnp.exp(b - r_mid)
        km = kh * jnp.exp(r_mid - b)
        a = jnp.where(diag_mask, _dot_nt(qm, km), 0.0)
        if ng > 1:
            lhs, rhs = [], []
            for j in range(ng - 1):
                lhs.append(jnp.concatenate(
                    [jnp.zeros((grp, dk), f32) if i <= j else qg[i * grp:(i + 1) * grp, :] * jnp.exp(r[i] - r[j + 1])
                     for i in range(ng)], axis=0))
                rhs.append(jnp.concatenate(
                    [kt[i * grp:(i + 1) * grp, :] if i == j else jnp.zeros((grp, dk), f32) for i in range(ng)], axis=0))
            a = a + _dot_nt(jnp.concatenate(lhs, axis=1), jnp.concatenate(rhs, axis=1))
        q_inter = qg * jnp.exp(r_start)
        st = st_ref[h]
        o = _dot(a, vh) + _dot_nt(q_inter, st)
        k_dec = kt * jnp.exp(b_last - r_end)
        st_new = st * jnp.exp(b_last) + _dot_tn(vh, k_dec)
        st_ref[h] = st_new
        on = o * lax.rsqrt(jnp.mean(o * o, axis=-1, keepdims=True) + RMS_EPS)
        y_ref[:, sl] = (on * nw[:, sl] * _silu(gate[:, sl])).astype(y_ref.dtype)

    @pl.when(c == nc - 1)
    def _():
        for h in range(nh):
            so_ref[0, h] = st_ref[h].T


def hgrn_call(p, lb_logits, norm_w, s0, bsz, t_pad, t_valid, layer):
    dim = p.shape[1] // 4
    nh = HG_HEADS
    dk = dim // nh
    c_rows = min(128, t_pad)
    grp = min(HG_CHUNK_GROUP, c_rows)
    nc = t_pad // c_rows
    has_s0 = s0 is not None
    in_specs = [pl.BlockSpec((c_rows, 4 * dim), lambda b, c: (b * nc + c, 0)),
                pl.BlockSpec(lb_logits.shape, lambda b, c: (0, 0)),
                pl.BlockSpec((1, dim), lambda b, c: (0, 0))]
    args = [p, lb_logits, norm_w.reshape(1, dim)]
    if has_s0:
        in_specs.append(pl.BlockSpec((1, nh, dk, dk), lambda b, c: (b, 0, 0, 0)))
        args.append(s0)
    kern = functools.partial(_hgrn_kernel, c_rows=c_rows, grp=grp, t_valid=t_valid, t_pad=t_pad,
                             layer=layer, has_s0=has_s0)
    return pl.pallas_call(
        kern,
        grid=(bsz, nc),
        in_specs=in_specs,
        out_specs=[pl.BlockSpec((c_rows, dim), lambda b, c: (b * nc + c, 0)),
                   pl.BlockSpec((1, nh, dk, dk), lambda b, c: (b, 0, 0, 0))],
        out_shape=[jax.ShapeDtypeStruct((bsz * t_pad, dim), bf16),
                   jax.ShapeDtypeStruct((bsz, nh, dk, dk), f32)],
        scratch_shapes=[pltpu.VMEM((nh, dk, dk), f32)],
        compiler_params=_cparams(2),
        name="hgrn",
    )(*args)


def _expand_heads(w, ex_ref):
    hi = w.astype(bf16)
    r1 = w - hi.astype(f32)
    mid = r1.astype(bf16)
    lo = (r1 - mid.astype(f32)).astype(bf16)
    return jnp.dot(jnp.concatenate([hi, mid, lo], axis=1), ex_ref[...], preferred_element_type=f32)


def _ssd_kernel(*refs, c_rows, t_valid, t_pad, has_s0):
    if has_s0:
        (zx_ref, cw_ref, cb_ref, dtb_ref, alog_ref, dsk_ref, nw_ref, ex_ref, s0_ref, buf_ref,
         y_ref, so_ref, st_ref, carry_ref) = refs
    else:
        (zx_ref, cw_ref, cb_ref, dtb_ref, alog_ref, dsk_ref, nw_ref, ex_ref,
         y_ref, so_ref, st_ref, carry_ref) = refs
    c = pl.program_id(1)
    nc = pl.num_programs(1)
    hp, ns, ngr = SSD_HEAD_DIM, SSD_STATE, SSD_GROUPS
    di = y_ref.shape[1]
    nh = di // hp
    hpg = nh // ngr
    gw = ngr * ns
    kconv = cw_ref.shape[0]
    pair = 2 * hp
    assert pair == LANES and hpg % 2 == 0

    @pl.when(c == 0)
    def _():
        carry_ref[...] = jnp.zeros_like(carry_ref)
        if has_s0:
            for r in range(di // LANES):
                st_ref[:, r * LANES:(r + 1) * LANES] = s0_ref[0, r * LANES:(r + 1) * LANES, :].T
            carry_ref[SUBLANES - (kconv - 1):, :] = buf_ref[0]
        else:
            st_ref[...] = jnp.zeros_like(st_ref)

    zx = zx_ref[...]
    z = zx[:, :di]
    cdim = cw_ref.shape[1]
    u = zx[:, di:di + cdim]
    rows = lax.broadcasted_iota(jnp.int32, (c_rows, 1), 0)
    cr = carry_ref[...]
    cw = cw_ref[...]
    yc = cb_ref[...]
    rows8 = rows[:SUBLANES]
    for kk in range(kconv - 1):
        back = kconv - 1 - kk
        sh = pltpu.roll(u, back, 0)
        top = sh[:SUBLANES]
        for r0 in range(back):
            top = jnp.where(rows8 == r0, cr[SUBLANES - back + r0:SUBLANES - back + r0 + 1, :], top)
        sh = jnp.concatenate([top, sh[SUBLANES:]], axis=0) if c_rows > SUBLANES else top
        yc = yc + sh * cw[kk:kk + 1, :]
    yc = yc + u * cw[kconv - 1:kconv, :]
    carry_ref[...] = u[c_rows - SUBLANES:, :]
    xbc = _silu(yc)
    dt = jax.nn.softplus(zx[:, di + cdim:] + dtb_ref[...])
    if t_valid < t_pad:
        valid = (c * c_rows + rows) < t_valid
        xbc = jnp.where(valid, xbc, 0.0)
        dt = jnp.where(valid, dt, 0.0)
    xs = xbc[:, :di]
    bm = xbc[:, di:di + gw]
    cm = xbc[:, di + gw:di + 2 * gw]
    a_neg = -jnp.exp(alog_ref[...])
    tril = _tril_mask(c_rows)
    acum = _cumsum_rows(dt * a_neg, tril.astype(bf16))

    def head_rows(a):
        if c_rows < LANES:
            a = jnp.concatenate([a, jnp.zeros((LANES - c_rows, LANES), f32)], axis=0)
        return a.T[:, :c_rows]

    acum_t = head_rows(acum)
    dt_t = head_rows(dt)
    a_last = acum[c_rows - 1:c_rows, :]
    xd = xs * _expand_heads(dt * jnp.exp(a_last - acum), ex_ref)
    dsk = dsk_ref[...]
    lo = lax.broadcasted_iota(jnp.int32, (c_rows, LANES), 1) < hp

    ys = []
    for g in range(ngr):
        bg = bm[:, g * ns:(g + 1) * ns]
        cg = cm[:, g * ns:(g + 1) * ns]
        cb_mat = _dot_nt(cg, bg)
        gsl = slice(g * hpg * hp, (g + 1) * hpg * hp)
        y_inter = _dot(cg, st_ref[:, gsl])
        decs = []
        for q in range(hpg // 2):
            h1 = g * hpg + 2 * q
            psl = slice(h1 * hp, h1 * hp + pair)
            ms, es = [], []
            for h in (h1, h1 + 1):
                colb = jnp.broadcast_to(acum[:, h:h + 1], (c_rows, LANES))
                seg = colb[:, :c_rows] - acum_t[h:h + 1, :]
                ms.append(cb_mat * jnp.exp(jnp.where(tril, seg, -jnp.inf)) * dt_t[h:h + 1, :])
                es.append(jnp.exp(colb))
            xp = xs[:, psl]
            x_lo = jnp.where(lo, xp, 0.0)
            x_hi = jnp.where(lo, 0.0, xp)
            if (2 * c_rows) % LANES == 0:
                y_intra = _dot(jnp.concatenate(ms, axis=1), jnp.concatenate([x_lo, x_hi], axis=0))
            else:
                y_intra = _dot(ms[0], x_lo) + _dot(ms[1], x_hi)
            y_pair = y_intra + y_inter[:, q * pair:(q + 1) * pair] * jnp.where(lo, es[0], es[1])
            ys.append(y_pair + dsk[:, psl] * xp)
            decs.append(jnp.where(lo[0:1, :], jnp.exp(a_last[:, h1:h1 + 1]), jnp.exp(a_last[:, h1 + 1:h1 + 2])))
        st_ref[:, gsl] = st_ref[:, gsl] * jnp.concatenate(decs, axis=1) + _dot_tn(bg, xd[:, gsl])

    y = jnp.concatenate(ys, axis=1) * _silu(z)
    nw = nw_ref[...]
    gdim = di // ngr
    for g in range(ngr):
        sl = slice(g * gdim, (g + 1) * gdim)
        yg = y[:, sl]
        y_ref[:, sl] = (yg * lax.rsqrt(jnp.mean(yg * yg, axis=-1, keepdims=True) + RMS_EPS) * nw[:, sl]).astype(y_ref.dtype)

    @pl.when(c == nc - 1)
    def _():
        for r in range(di // LANES):
            so_ref[0, r * LANES:(r + 1) * LANES, :] = st_ref[:, r * LANES:(r + 1) * LANES].T


def ssd_call(zx, conv_w, conv_b, dt_bias, a_log, d_skip, norm_w, s0, conv_buf, bsz, t_pad, t_valid):
    hp, ns = SSD_HEAD_DIM, SSD_STATE
    conv_dim = conv_w.shape[1]
    di = zx.shape[1] - conv_dim - LANES
    nh = di // hp
    c_rows = min(128, t_pad)
    nc = t_pad // c_rows
    has_s0 = s0 is not None
    pad_h = lambda a: jnp.pad(a.reshape(1, nh), ((0, 0), (0, LANES - nh)))
    full = lambda shape: pl.BlockSpec(shape, lambda b, c: (0,) * len(shape))
    sel = (jnp.arange(di)[None, :] // hp == jnp.arange(LANES)[:, None]).astype(bf16)
    expand = jnp.concatenate([sel, sel, sel], axis=0)
    in_specs = [pl.BlockSpec((c_rows, zx.shape[1]), lambda b, c: (b * nc + c, 0)),
                full(conv_w.shape), full((1, conv_dim)), full((1, LANES)), full((1, LANES)),
                full((1, di)), full((1, di)), full(expand.shape)]
    args = [zx, conv_w, conv_b.reshape(1, conv_dim), pad_h(dt_bias), pad_h(a_log),
            jnp.repeat(d_skip, hp).reshape(1, di), norm_w.reshape(1, di), expand]
    if has_s0:
        in_specs += [pl.BlockSpec((1, di, ns), lambda b, c: (b, 0, 0)),
                     pl.BlockSpec((1,) + conv_buf.shape[1:], lambda b, c: (b, 0, 0))]
        args += [s0.reshape(bsz, di, ns), conv_buf]
    kern = functools.partial(_ssd_kernel, c_rows=c_rows, t_valid=t_valid, t_pad=t_pad, has_s0=has_s0)
    y, s_new = pl.pallas_call(
        kern,
        grid=(bsz, nc),
        in_specs=in_specs,
        out_specs=[pl.BlockSpec((c_rows, di), lambda b, c: (b * nc + c, 0)),
                   pl.BlockSpec((1, di, ns), lambda b, c: (b, 0, 0))],
        out_shape=[jax.ShapeDtypeStruct((bsz * t_pad, di), bf16),
                   jax.ShapeDtypeStruct((bsz, di, ns), f32)],
        scratch_shapes=[pltpu.VMEM((ns, di), f32), pltpu.VMEM((SUBLANES, conv_dim), f32)],
        compiler_params=_cparams(2),
        name="ssd",
    )(*args)
    return y, s_new.reshape(bsz, nh, hp, ns)


def _trunk(x, c_k, c_v, s_hg, s_ssm, s_sconv, s_fconv, prm):
    bsz, t, d = x.shape
    depth = prm["norm_mix_w"].shape[0]
    decode = c_k is not None
    tp = -(-t // SUBLANES) * SUBLANES
    if tp != t:
        x = jnp.pad(x, ((0, 0), (0, tp - t), (0, 0)))
    m = bsz * tp
    n_heads = d // ATT_HEAD_DIM
    qd = n_heads * ATT_HEAD_DIM
    kvd = ATT_KV_HEADS * ATT_HEAD_DIM
    x2 = x.reshape(m, d)
    xn = rmsnorm_call(x2, prm["norm_mix_w"][0], bf16)
    nk, nv, nhg, nssm, nsconv, nfconv = [], [], [], [], [], []
    slot = [0] * N_MIXERS
    for i in range(depth):
        kind = i % N_MIXERS
        j = slot[kind]
        slot[kind] += 1
        nw_ffn = prm["norm_ffn_w"][i]
        if kind == 0:
            w_qkv, b_qkv = prm["attn_w_qkv"][j], prm["attn_b_qkv"][j]
            nq = w_qkv.shape[1]
            if decode:
                qkv = proj_call(xn, w_qkv, b_qkv, f32, "attn_qkv").reshape(bsz, tp, nq)
                ck = jnp.transpose(c_k[j], (0, 2, 3, 1)).reshape(bsz, kvd, WINDOW)
                cv = jnp.transpose(c_v[j], (0, 2, 3, 1)).reshape(bsz, kvd, WINDOW)
                o = attn_cache_call(qkv, ck, cv, prm["attn_sinks"][j], n_heads, t).reshape(m, qd)
                to4 = lambda a: a.reshape(bsz, t, ATT_KV_HEADS, ATT_HEAD_DIM)
                k_new = jnp.concatenate([c_k[j][:, t:], to4(qkv[:, :t, qd:qd + kvd])], axis=1)
                v_new = jnp.concatenate([c_v[j][:, t:], to4(qkv[:, :t, qd + kvd:])], axis=1)
            else:
                qkv = proj_call(xn, w_qkv, b_qkv, bf16, "attn_qkv")
                o = attn_prompt_call(qkv, prm["attn_sinks"][j], bsz, t, n_heads)
                xn_last = xn.reshape(bsz, t, d)[:, t - WINDOW:].reshape(bsz * WINDOW, d)
                kv_last = proj_call(xn_last, w_qkv[:, qd:], b_qkv[qd:], f32, "attn_kv_tail").reshape(bsz, WINDOW, 2, kvd)
                to4 = lambda a: a.reshape(bsz, WINDOW, ATT_KV_HEADS, ATT_HEAD_DIM)
                k_new, v_new = to4(kv_last[:, :, 0]), to4(kv_last[:, :, 1])
            nk.append(k_new)
            nv.append(v_new)
            act, w_o, b_o = o, prm["attn_w_o"][j], prm["attn_b_o"][j]
        elif kind == 1:
            p = proj_call(xn, prm["hgrn_w_in"][j], None, f32, "hgrn_in")
            y, s_new = hgrn_call(p, prm["hgrn_lb_logits"], prm["hgrn_norm_w"][j],
                                 None if s_hg is None else s_hg[j], bsz, tp, t, i)
            nhg.append(s_new)
            act, w_o, b_o = y, prm["hgrn_w_o"][j], None
        else:
            conv_dim = prm["ssd_conv_w"].shape[2]
            di = prm["ssd_w_o"][j].shape[0]
            zx = proj_call(xn, prm["ssd_w_in"][j], None, f32, "ssd_in")
            y, s_new = ssd_call(zx, prm["ssd_conv_w"][j], prm["ssd_conv_b"][j], prm["ssd_dt_bias"][j],
                                prm["ssd_a_log"][j], prm["ssd_d"][j], prm["ssd_norm_w"][j],
                                None if s_ssm is None else s_ssm[j],
                                None if s_sconv is None else s_sconv[j], bsz, tp, t)
            nssm.append(s_new)
            kc = prm["ssd_conv_w"].shape[1]
            pre = zx.reshape(bsz, tp, zx.shape[1])[:, max(t - (kc - 1), 0):t, di:di + conv_dim]
            if s_sconv is not None:
                pre = jnp.concatenate([s_sconv[j], pre], axis=1)
            nsconv.append(pre[:, pre.shape[1] - (kc - 1):])
            act, w_o, b_o = y, prm["ssd_w_o"][j], None
        last = i == depth - 1
        next_w = prm["norm_final_w"] if last else prm["norm_mix_w"][i + 1]
        dff = prm["ffn_w_down"][i].shape[0]
        kf = prm["ffn_conv_w"].shape[1]
        if s_fconv is not None:
            buf = s_fconv[i]
            e1 = jnp.pad(buf[:, 1:2], ((0, 0), (0, tp - 1), (0, 0))).reshape(m, dff)
            e2 = jnp.pad(buf, ((0, 0), (0, tp - 2), (0, 0))).reshape(m, dff)
        else:
            e1 = e2 = None
        x2, xn, tail = ffn_call(act, w_o, b_o, x2, nw_ffn, prm["ffn_w_up"][i], prm["ffn_conv_w"][i],
                                prm["ffn_conv_b"][i], prm["ffn_w_down"][i], next_w, tp, e1, e2,
                                f32 if last else bf16, "ffn")
        if s_fconv is not None:
            a_full = tail.reshape(bsz, tp, dff)[:, :t]
            nfconv.append(jnp.concatenate([s_fconv[i], a_full], axis=1)[:, t:])
        else:
            tiles_per_seq = tp // tail_tile_rows(m)
            tl = tail.reshape(bsz, tiles_per_seq, SUBLANES, dff)[:, -1]
            nfconv.append(tl[:, SUBLANES - (kf - 1):])
    stack = lambda xs: xs[0][None] if len(xs) == 1 else jnp.stack(xs)
    y = xn.reshape(bsz, tp, d)[:, :t]
    return (y, stack(nk), stack(nv), stack(nhg), stack(nssm), stack(nsconv), stack(nfconv))


def tail_tile_rows(m):
    return min(FFN_TM, m)


def kernel(x_prompt, x_sample, cache_attn_k, cache_attn_v, state_hgrn, state_ssm, state_ssm_conv, state_ffn_conv, norm_mix_w, norm_ffn_w, norm_final_w, attn_w_qkv, attn_b_qkv, attn_sinks, attn_w_o, attn_b_o, hgrn_w_in, hgrn_lb_logits, hgrn_norm_w, hgrn_w_o, ssd_w_in, ssd_conv_w, ssd_conv_b, ssd_dt_bias, ssd_a_log, ssd_d, ssd_norm_w, ssd_w_o, ffn_w_up, ffn_conv_w, ffn_conv_b, ffn_w_down):
    cast = lambda w: [w[l].astype(bf16) for l in range(w.shape[0])]
    ssd_in_pad = (-ssd_w_in.shape[2]) % LANES
    prm = dict(norm_mix_w=norm_mix_w, norm_ffn_w=norm_ffn_w, norm_final_w=norm_final_w,
               attn_w_qkv=cast(attn_w_qkv), attn_b_qkv=attn_b_qkv, attn_sinks=attn_sinks,
               attn_w_o=cast(attn_w_o), attn_b_o=attn_b_o,
               hgrn_w_in=cast(hgrn_w_in), hgrn_lb_logits=hgrn_lb_logits, hgrn_norm_w=hgrn_norm_w,
               hgrn_w_o=cast(hgrn_w_o),
               ssd_w_in=[jnp.pad(w, ((0, 0), (0, ssd_in_pad))) for w in cast(ssd_w_in)], ssd_conv_w=ssd_conv_w,
               ssd_conv_b=ssd_conv_b, ssd_dt_bias=ssd_dt_bias,
               ssd_a_log=ssd_a_log, ssd_d=ssd_d, ssd_norm_w=ssd_norm_w, ssd_w_o=cast(ssd_w_o),
               ffn_w_up=cast(ffn_w_up), ffn_conv_w=ffn_conv_w, ffn_conv_b=ffn_conv_b, ffn_w_down=cast(ffn_w_down))
    outs_p = _trunk(x_prompt, None, None, None, None, None, None, prm)
    outs_s = _trunk(x_sample, cache_attn_k, cache_attn_v, state_hgrn, state_ssm, state_ssm_conv,
                    state_ffn_conv, prm)
    return (outs_p[0], outs_s[0]) + outs_p[1:] + outs_s[1:]
```

```python
import functools

import jax
import jax.numpy as jnp
from jax import lax
from jax.experimental import pallas as pl
from jax.experimental.pallas import tpu as pltpu

bf16 = jnp.bfloat16
f32 = jnp.float32

RMS_EPS = 1e-6
N_MIXERS = 3
WINDOW = 128
PAST_LEN = 8192
ATT_HEAD_DIM = 64
ATT_KV_HEADS = 4
HG_HEADS = 8
HG_CHUNK_GROUP = 32
SSD_HEAD_DIM = 64
SSD_STATE = 128
SSD_GROUPS = 4

LANES = 128
SUBLANES = 8
VMEM_LIMIT = 56 * 1024 * 1024
PROJ_TM = 512
FFN_TM = 512


def _cparams(n_axes):
    return pltpu.CompilerParams(dimension_semantics=("arbitrary",) * n_axes,
                                vmem_limit_bytes=VMEM_LIMIT)


def _dot(a, b):
    return jnp.dot(a.astype(bf16), b.astype(bf16), preferred_element_type=f32)


def _dot_nt(a, b):
    return lax.dot_general(a.astype(bf16), b.astype(bf16), (((1,), (1,)), ((), ())),
                           preferred_element_type=f32)


def _dot_tn(a, b):
    return lax.dot_general(a.astype(bf16), b.astype(bf16), (((0,), (0,)), ((), ())),
                           preferred_element_type=f32)


def _silu(x):
    return x * jax.nn.sigmoid(x)


def _rms(x, w):
    ms = jnp.mean(x * x, axis=-1, keepdims=True)
    return x * lax.rsqrt(ms + RMS_EPS) * w


def _cumsum_rows(x, tril):
    hi = x.astype(bf16)
    r1 = x - hi.astype(f32)
    mid = r1.astype(bf16)
    lo = (r1 - mid.astype(f32)).astype(bf16)
    d = functools.partial(jnp.dot, preferred_element_type=f32)
    return d(tril, hi) + d(tril, mid) + d(tril, lo)


def _tril_mask(c):
    r = lax.broadcasted_iota(jnp.int32, (c, c), 0)
    s = lax.broadcasted_iota(jnp.int32, (c, c), 1)
    return r >= s


def _norm_kernel(x_ref, w_ref, o_ref):
    o_ref[...] = _rms(x_ref[...], w_ref[...]).astype(o_ref.dtype)


def rmsnorm_call(x, w, out_dtype):
    m, d = x.shape
    tm = min(1024, m)
    return pl.pallas_call(
        _norm_kernel,
        grid=(m // tm,),
        in_specs=[pl.BlockSpec((tm, d), lambda i: (i, 0)), pl.BlockSpec((1, d), lambda i: (0, 0))],
        out_specs=pl.BlockSpec((tm, d), lambda i: (i, 0)),
        out_shape=jax.ShapeDtypeStruct((m, d), out_dtype),
        compiler_params=_cparams(1),
        name="rmsnorm",
    )(x, w.reshape(1, d))


def _resident(shape):
    return pl.BlockSpec(shape, lambda *_: (0,) * len(shape), pipeline_mode=pl.Buffered(1))


def _resident_layer(w, layer):
    return pl.BlockSpec((None,) + w.shape[1:], lambda *_: (layer, 0, 0), pipeline_mode=pl.Buffered(1))


def _proj_kernel(x_ref, w_ref, b_ref, o_ref):
    acc = jnp.dot(x_ref[...], w_ref[...], preferred_element_type=f32)
    o_ref[...] = (acc + b_ref[...]).astype(o_ref.dtype)


def proj_call(xn, w, layer, bias, out_dtype, name):
    m, k = xn.shape
    n = w.shape[2]
    tm = min(PROJ_TM, m)
    if bias is None:
        bias = jnp.zeros((n,), f32)
    return pl.pallas_call(
        _proj_kernel,
        grid=(m // tm,),
        in_specs=[pl.BlockSpec((tm, k), lambda i: (i, 0)), _resident_layer(w, layer), _resident((1, n))],
        out_specs=pl.BlockSpec((tm, n), lambda i: (i, 0)),
        out_shape=jax.ShapeDtypeStruct((m, n), out_dtype),
        compiler_params=_cparams(1),
        name=name,
    )(xn, w, bias.reshape(1, n))


def _ffn_kernel(*refs, tm, tf, t_seq, tail, has_state):
    if has_state:
        (act_ref, wo_ref, bo_ref, x_ref, nw1_ref, wu_ref, cw_ref, cb_ref, wd_ref, nw2_ref, e1_ref, e2_ref,
         xo_ref, xn2_ref, tail_ref, carry_ref) = refs
    else:
        (act_ref, wo_ref, bo_ref, x_ref, nw1_ref, wu_ref, cw_ref, cb_ref, wd_ref, nw2_ref,
         xo_ref, xn2_ref, tail_ref, carry_ref) = refs
    i = pl.program_id(0)
    dff = wd_ref.shape[0]

    x1 = jnp.dot(act_ref[...].astype(bf16), wo_ref[...], preferred_element_type=f32) + bo_ref[...] + x_ref[...]
    xn = _rms(x1, nw1_ref[...]).astype(bf16)
    if has_state:
        tpos = (lax.broadcasted_iota(jnp.int32, (tm, 1), 0)) & (t_seq - 1)
    else:
        @pl.when(((i * tm) & (t_seq - 1)) == 0)
        def _():
            carry_ref[...] = jnp.zeros_like(carry_ref)
        rows8 = lax.broadcasted_iota(jnp.int32, (SUBLANES, 1), 0)
    gs = []
    for c0 in range(0, dff, tf):
        cs = slice(c0, c0 + tf)
        a = jnp.dot(xn, wu_ref[:, cs], preferred_element_type=f32)
        b = jnp.dot(xn, wu_ref[:, dff + c0:dff + c0 + tf], preferred_element_type=f32)
        tail_ref[0, :, cs] = a[tm - tail:, :]
        r1 = pltpu.roll(a, 1, 0)
        r2 = pltpu.roll(a, 2, 0)
        if has_state:
            p1 = jnp.where(tpos == 0, e1_ref[:, cs], r1)
            p2 = jnp.where(tpos < 2, e2_ref[:, cs], r2)
        else:
            carry = carry_ref[:, cs]
            top1 = jnp.where(rows8 == 0, carry[7:8, :], r1[:SUBLANES])
            top2 = jnp.where(rows8 == 0, carry[6:7, :], jnp.where(rows8 == 1, carry[7:8, :], r2[:SUBLANES]))
            p1 = jnp.concatenate([top1, r1[SUBLANES:]], axis=0)
            p2 = jnp.concatenate([top2, r2[SUBLANES:]], axis=0)
            carry_ref[:, cs] = a[tm - SUBLANES:, :]
        y = cb_ref[:, cs] + p2 * cw_ref[0:1, cs]
        y = y + p1 * cw_ref[1:2, cs]
        y = y + a * cw_ref[2:3, cs]
        gs.append((_silu(y) * b).astype(bf16))
    g = jnp.concatenate(gs, axis=1)
    xo = jnp.dot(g, wd_ref[...], preferred_element_type=f32) + x1
    xo_ref[...] = xo
    xn2_ref[...] = _rms(xo, nw2_ref[...]).astype(xn2_ref.dtype)


def ffn_call(act, w_o, b_o, x, ffn_norm_w, w_up, conv_w, conv_b, w_down, next_w, t_seq, e1, e2, out_dtype, name):
    m, d = x.shape
    ko = act.shape[1]
    dff = w_down[0].shape[1]
    tf = 2 * LANES
    assert dff % tf == 0 and (t_seq & (t_seq - 1)) == 0
    has_state = e1 is not None
    tm = min(FFN_TM // 2 if has_state else FFN_TM, m)
    assert (tm % t_seq == 0) if has_state else (t_seq % tm == 0)
    tail = tm if has_state else SUBLANES
    if b_o is None:
        b_o = jnp.zeros((d,), f32)
    row = lambda w: pl.BlockSpec((tm, w), lambda i: (i, 0))
    in_specs = [row(ko), _resident_layer(*w_o), _resident((1, d)), row(d), _resident((1, d)),
                _resident_layer(*w_up), _resident((3, dff)), _resident((1, dff)),
                _resident_layer(*w_down), _resident((1, d))]
    args = [act, w_o[0], b_o.reshape(1, d), x, ffn_norm_w.reshape(1, d),
            w_up[0], conv_w, conv_b.reshape(1, dff), w_down[0], next_w.reshape(1, d)]
    if has_state:
        in_specs += [row(dff), row(dff)]
        args += [e1, e2]
    kern = functools.partial(_ffn_kernel, tm=tm, tf=tf, t_seq=t_seq, tail=tail, has_state=has_state)
    return pl.pallas_call(
        kern,
        grid=(m // tm,),
        in_specs=in_specs,
        out_specs=[row(d), row(d), pl.BlockSpec((1, tail, dff), lambda i: (i, 0, 0))],
        out_shape=[jax.ShapeDtypeStruct((m, d), f32),
                   jax.ShapeDtypeStruct((m, d), out_dtype),
                   jax.ShapeDtypeStruct((m // tm, tail, dff), f32)],
        scratch_shapes=[pltpu.VMEM((SUBLANES, dff), f32)],
        compiler_params=_cparams(1),
        name=name,
    )(*args)


def _attn_softmax_pv(s, sink_col, vj):
    m = jnp.maximum(jnp.max(s, axis=-1, keepdims=True), sink_col)
    p = jnp.exp(s - m)
    l = jnp.sum(p, axis=-1, keepdims=True) + jnp.exp(sink_col - m)
    return _dot(p, vj) / l


def _sink_col(sinks_ref, j, group, rows_per_head):
    r = lax.broadcasted_iota(jnp.int32, (group * rows_per_head, 1), 0)
    col = jnp.full((group * rows_per_head, 1), sinks_ref[j * group], f32)
    for p in range(1, group):
        col = jnp.where(r >= p * rows_per_head, sinks_ref[j * group + p], col)
    return col


def _attn_prompt_kernel(sinks_ref, q_ref, kvc_ref, kvp_ref, o_ref, *, n_heads):
    n = pl.program_id(1)
    w, hd, kvh = WINDOW, ATT_HEAD_DIM, ATT_KV_HEADS
    group = n_heads // kvh
    kvw = kvh * hd
    assert group == kvh
    kv = jnp.concatenate([kvp_ref[...], kvc_ref[...]], axis=0)
    k2 = kv[:, :kvw]
    v2 = kv[:, kvw:]
    i = lax.broadcasted_iota(jnp.int32, (w, 2 * w), 0)
    c = lax.broadcasted_iota(jnp.int32, (w, 2 * w), 1)
    band = (c > i) & (c <= i + w) & ((n > 0) | (c >= w))
    mask = jnp.concatenate([band] * group, axis=0)
    slot = lax.broadcasted_iota(jnp.int32, (w, kvw), 1) >> (hd.bit_length() - 1)
    for j in range(kvh):
        qg = q_ref[:, j * kvw:(j + 1) * kvw] * (hd ** -0.5)
        qs = jnp.concatenate([jnp.where(slot == j, pltpu.roll(qg, ((j - p) % kvh) * hd, 1), 0)
                              for p in range(group)], axis=0)
        s = jnp.where(mask, _dot_nt(qs, k2), -jnp.inf)
        of = _attn_softmax_pv(s, _sink_col(sinks_ref, j, group, w), v2).astype(o_ref.dtype)
        og = of[0:w, :]
        for p in range(group):
            piece = of[p * w:(p + 1) * w, :]
            og = jnp.where(slot == p, pltpu.roll(piece, ((p - j) % kvh) * hd, 1), og)
        o_ref[:, j * kvw:(j + 1) * kvw] = og


def attn_prompt_call(qkv, sinks, bsz, t, n_heads):
    w, hd, kvh = WINDOW, ATT_HEAD_DIM, ATT_KV_HEADS
    nb = t // w
    qd = n_heads * hd
    kvd = 2 * kvh * hd
    assert qd % kvd == 0
    kvblk = qd // kvd
    kern = functools.partial(_attn_prompt_kernel, n_heads=n_heads)
    return pl.pallas_call(
        kern,
        grid=(bsz, nb),
        in_specs=[pl.BlockSpec(memory_space=pltpu.SMEM),
                  pl.BlockSpec((w, qd), lambda b, n: (b * nb + n, 0)),
                  pl.BlockSpec((w, kvd), lambda b, n: (b * nb + n, kvblk)),
                  pl.BlockSpec((w, kvd), lambda b, n: (b * nb + jnp.maximum(n - 1, 0), kvblk))],
        out_specs=pl.BlockSpec((w, qd), lambda b, n: (b * nb + n, 0)),
        out_shape=jax.ShapeDtypeStruct((bsz * t, qd), bf16),
        compiler_params=_cparams(2),
        name="attn_prompt",
    )(sinks, qkv, qkv, qkv)


def _attn_cache_kernel(sinks_ref, q_ref, ck_ref, cv_ref, o_ref, *, n_heads, t_valid, bt):
    w, hd, kvh = WINDOW, ATT_HEAD_DIM, ATT_KV_HEADS
    group = n_heads // kvh
    tp = q_ref.shape[1]
    qd = n_heads * hd
    tq = lax.broadcasted_iota(jnp.int32, (tp, w), 0)
    jc = lax.broadcasted_iota(jnp.int32, (tp, w), 1)
    mc = (jc > tq) & (jc - w + PAST_LEN >= 0)
    tq2 = lax.broadcasted_iota(jnp.int32, (tp, tp), 0)
    un = lax.broadcasted_iota(jnp.int32, (tp, tp), 1)
    mn = (un <= tq2) & (un < t_valid)
    mask_c = jnp.concatenate([mc] * n_heads, axis=0)
    mask_n = jnp.concatenate([mn] * n_heads, axis=0)
    kvw = kvh * hd
    assert group == kvh
    slot = lax.broadcasted_iota(jnp.int32, (tp, kvw), 1) >> (hd.bit_length() - 1)
    sink = _sink_col(sinks_ref, 0, n_heads, tp)

    for bi in range(bt):
        row = q_ref[bi]
        ck = ck_ref[bi]
        cv = cv_ref[bi]
        kn = row[:, qd:qd + kvw]
        vn = row[:, qd + kvw:]
        pieces = []
        for j in range(kvh):
            qg = row[:, j * kvw:(j + 1) * kvw] * (hd ** -0.5)
            for p in range(group):
                pieces.append(jnp.where(slot == j, pltpu.roll(qg, ((j - p) % kvh) * hd, 1), 0.0))
        qs = jnp.concatenate(pieces, axis=0)
        sc = jnp.where(mask_c, _dot(qs, ck), -jnp.inf)
        sn = jnp.where(mask_n, _dot_nt(qs, kn), -jnp.inf)
        m = jnp.maximum(jnp.maximum(jnp.max(sc, axis=-1, keepdims=True),
                                    jnp.max(sn, axis=-1, keepdims=True)), sink)
        pc = jnp.exp(sc - m)
        pn = jnp.exp(sn - m)
        l = jnp.sum(pc, axis=-1, keepdims=True) + jnp.sum(pn, axis=-1, keepdims=True) + jnp.exp(sink - m)
        of = (_dot_nt(pc, cv) + _dot(pn, vn)) / l
        for j in range(kvh):
            og = jnp.zeros((tp, kvw), f32)
            for p in range(group):
                h = j * group + p
                og = og + jnp.where(slot == p, pltpu.roll(of[h * tp:(h + 1) * tp, :], ((p - j) % kvh) * hd, 1), 0.0)
            o_ref[bi, :, j * kvw:(j + 1) * kvw] = og


def attn_cache_call(qkv, cache_k, cache_v, sinks, n_heads, t_valid):
    bsz, tp, nq = qkv.shape
    w = WINDOW
    kd = cache_k.shape[1]
    qd = n_heads * ATT_HEAD_DIM
    bt = 8
    kern = functools.partial(_attn_cache_kernel, n_heads=n_heads, t_valid=t_valid, bt=bt)
    return pl.pallas_call(
        kern,
        grid=(bsz // bt,),
        in_specs=[pl.BlockSpec(memory_space=pltpu.SMEM),
                  pl.BlockSpec((bt, tp, nq), lambda b: (b, 0, 0)),
                  pl.BlockSpec((bt, kd, w), lambda b: (b, 0, 0)),
                  pl.BlockSpec((bt, kd, w), lambda b: (b, 0, 0))],
        out_specs=pl.BlockSpec((bt, tp, qd), lambda b: (b, 0, 0)),
        out_shape=jax.ShapeDtypeStruct((bsz, tp, qd), f32),
        compiler_params=_cparams(1),
        name="attn_cache",
    )(sinks, qkv, cache_k, cache_v)


def _hgrn_kernel(*refs, c_rows, grp, t_valid, t_pad, layer, has_s0):
    if has_s0:
        p_ref, lbl_ref, nw_ref, s0_ref, y_ref, so_ref, st_ref = refs
    else:
        p_ref, lbl_ref, nw_ref, y_ref, so_ref, st_ref = refs
    c = pl.program_id(1)
    nc = pl.num_programs(1)
    nh = HG_HEADS
    dk = p_ref.shape[1] // 4 // nh
    dim = nh * dk
    ng = c_rows // grp

    @pl.when(c == 0)
    def _():
        if has_s0:
            for h in range(nh):
                st_ref[h] = s0_ref[0, h].T
        else:
            st_ref[...] = jnp.zeros_like(st_ref)

    lbl = lbl_ref[...]
    e = jnp.exp(lbl - jnp.max(lbl, axis=0, keepdims=True))
    sm = e / jnp.sum(e, axis=0, keepdims=True)
    lb = jnp.zeros((1, dim), f32)
    for r in range(1, layer + 1):
        lb = lb + sm[r:r + 1, :]

    x = p_ref[...]
    if t_valid < t_pad:
        valid = (c * c_rows + lax.broadcasted_iota(jnp.int32, (c_rows, 1), 0)) < t_valid
        x = jnp.where(valid, x, 0.0)
    q = _silu(x[:, 0:dim])
    fg = lb + (1.0 - lb) * jax.nn.sigmoid(x[:, dim:2 * dim])
    logf = jnp.log(fg)
    k = 1.0 - fg
    v = x[:, 2 * dim:3 * dim]
    gate = x[:, 3 * dim:4 * dim]
    if t_valid < t_pad:
        logf = jnp.where(valid, logf, 0.0)
        k = jnp.where(valid, k, 0.0)

    tril = _tril_mask(c_rows)
    bcum = _cumsum_rows(logf, tril.astype(bf16))
    rr = lax.broadcasted_iota(jnp.int32, (c_rows, c_rows), 0)
    ss = lax.broadcasted_iota(jnp.int32, (c_rows, c_rows), 1)
    gshift = grp.bit_length() - 1
    diag_mask = tril & ((rr >> gshift) == (ss >> gshift))
    nw = nw_ref[...]

    def rows_of(vals):
        return jnp.concatenate([jnp.broadcast_to(r, (grp, dk)) for r in vals], axis=0)

    for h in range(nh):
        sl = slice(h * dk, (h + 1) * dk)
        b = bcum[:, sl]
        qh, kh, vh = q[:, sl], k[:, sl], v[:, sl]
        zero = jnp.zeros((1, dk), f32)
        r = [zero] + [b[i * grp - 1:i * grp, :] for i in range(1, ng)] + [b[c_rows - 1:c_rows, :]]
        mid = [b[i * grp + grp // 2 - 1:i * grp + grp // 2, :] for i in range(ng)]
        b_last = r[ng]
        r_start = rows_of(r[:ng])
        r_end = rows_of(r[1:])
        r_mid = rows_of(mid)
        qg = qh * jnp.exp(b - r_start)
        kt = kh * jnp.exp(r_end - b)
        qm = qh * jnp.exp(b - r_mid)
        km = kh * jnp.exp(r_mid - b)
        a = jnp.where(diag_mask, _dot_nt(qm, km), 0.0)
        if ng > 1:
            lhs, rhs = [], []
            for j in range(ng - 1):
                lhs.append(jnp.concatenate(
                    [jnp.zeros((grp, dk), f32) if i <= j else qg[i * grp:(i + 1) * grp, :] * jnp.exp(r[i] - r[j + 1])
                     for i in range(ng)], axis=0))
                rhs.append(jnp.concatenate(
                    [kt[i * grp:(i + 1) * grp, :] if i == j else jnp.zeros((grp, dk), f32) for i in range(ng)], axis=0))
            a = a + _dot_nt(jnp.concatenate(lhs, axis=1), jnp.concatenate(rhs, axis=1))
        q_inter = qg * jnp.exp(r_start)
        st = st_ref[h]
        o = _dot(a, vh) + _dot_nt(q_inter, st)
        k_dec = kt * jnp.exp(b_last - r_end)
        st_new = st * jnp.exp(b_last) + _dot_tn(vh, k_dec)
        st_ref[h] = st_new
        on = o * lax.rsqrt(jnp.mean(o * o, axis=-1, keepdims=True) + RMS_EPS)
        y_ref[:, sl] = (on * nw[:, sl] * _silu(gate[:, sl])).astype(y_ref.dtype)

    @pl.when(c == nc - 1)
    def _():
        for h in range(nh):
            so_ref[0, h] = st_ref[h].T


def hgrn_call(p, lb_logits, norm_w, s0, bsz, t_pad, t_valid, layer):
    dim = p.shape[1] // 4
    nh = HG_HEADS
    dk = dim // nh
    c_rows = min(128, t_pad)
    grp = min(HG_CHUNK_GROUP, c_rows)
    nc = t_pad // c_rows
    has_s0 = s0 is not None
    in_specs = [pl.BlockSpec((c_rows, 4 * dim), lambda b, c: (b * nc + c, 0)),
                pl.BlockSpec(lb_logits.shape, lambda b, c: (0, 0)),
                pl.BlockSpec((1, dim), lambda b, c: (0, 0))]
    args = [p, lb_logits, norm_w.reshape(1, dim)]
    if has_s0:
        in_specs.append(pl.BlockSpec((1, nh, dk, dk), lambda b, c: (b, 0, 0, 0)))
        args.append(s0)
    kern = functools.partial(_hgrn_kernel, c_rows=c_rows, grp=grp, t_valid=t_valid, t_pad=t_pad,
                             layer=layer, has_s0=has_s0)
    return pl.pallas_call(
        kern,
        grid=(bsz, nc),
        in_specs=in_specs,
        out_specs=[pl.BlockSpec((c_rows, dim), lambda b, c: (b * nc + c, 0)),
                   pl.BlockSpec((1, nh, dk, dk), lambda b, c: (b, 0, 0, 0))],
        out_shape=[jax.ShapeDtypeStruct((bsz * t_pad, dim), bf16),
                   jax.ShapeDtypeStruct((bsz, nh, dk, dk), f32)],
        scratch_shapes=[pltpu.VMEM((nh, dk, dk), f32)],
        compiler_params=_cparams(2),
        name="hgrn",
    )(*args)


def _expand_heads(w, ex_ref):
    hi = w.astype(bf16)
    r1 = w - hi.astype(f32)
    mid = r1.astype(bf16)
    lo = (r1 - mid.astype(f32)).astype(bf16)
    return jnp.dot(jnp.concatenate([hi, mid, lo], axis=1), ex_ref[...], preferred_element_type=f32)


def _ssd_kernel(*refs, c_rows, t_valid, t_pad, has_s0):
    if has_s0:
        (zx_ref, cw_ref, cb_ref, dtb_ref, alog_ref, dsk_ref, nw_ref, ex_ref, s0_ref, buf_ref,
         y_ref, so_ref, st_ref, carry_ref) = refs
    else:
        (zx_ref, cw_ref, cb_ref, dtb_ref, alog_ref, dsk_ref, nw_ref, ex_ref,
         y_ref, so_ref, st_ref, carry_ref) = refs
    c = pl.program_id(1)
    nc = pl.num_programs(1)
    hp, ns, ngr = SSD_HEAD_DIM, SSD_STATE, SSD_GROUPS
    di = y_ref.shape[1]
    nh = di // hp
    hpg = nh // ngr
    gw = ngr * ns
    kconv = cw_ref.shape[0]
    pair = 2 * hp
    assert pair == LANES and hpg % 2 == 0

    @pl.when(c == 0)
    def _():
        carry_ref[...] = jnp.zeros_like(carry_ref)
        if has_s0:
            for r in range(di // LANES):
                st_ref[:, r * LANES:(r + 1) * LANES] = s0_ref[0, r * LANES:(r + 1) * LANES, :].T
            carry_ref[SUBLANES - (kconv - 1):, :] = buf_ref[0]
        else:
            st_ref[...] = jnp.zeros_like(st_ref)

    zx = zx_ref[...]
    z = zx[:, :di]
    cdim = cw_ref.shape[1]
    u = zx[:, di:di + cdim]
    rows = lax.broadcasted_iota(jnp.int32, (c_rows, 1), 0)
    cr = carry_ref[...]
    cw = cw_ref[...]
    yc = cb_ref[...]
    rows8 = rows[:SUBLANES]
    for kk in range(kconv - 1):
        back = kconv - 1 - kk
        sh = pltpu.roll(u, back, 0)
        top = sh[:SUBLANES]
        for r0 in range(back):
            top = jnp.where(rows8 == r0, cr[SUBLANES - back + r0:SUBLANES - back + r0 + 1, :], top)
        sh = jnp.concatenate([top, sh[SUBLANES:]], axis=0) if c_rows > SUBLANES else top
        yc = yc + sh * cw[kk:kk + 1, :]
    yc = yc + u * cw[kconv - 1:kconv, :]
    carry_ref[...] = u[c_rows - SUBLANES:, :]
    xbc = _silu(yc)
    dt = jax.nn.softplus(zx[:, di + cdim:] + dtb_ref[...])
    if t_valid < t_pad:
        valid = (c * c_rows + rows) < t_valid
        xbc = jnp.where(valid, xbc, 0.0)
        dt = jnp.where(valid, dt, 0.0)
    xs = xbc[:, :di]
    bm = xbc[:, di:di + gw]
    cm = xbc[:, di + gw:di + 2 * gw]
    a_neg = -jnp.exp(alog_ref[...])
    tril = _tril_mask(c_rows)
    acum = _cumsum_rows(dt * a_neg, tril.astype(bf16))

    def head_rows(a):
        if c_rows < LANES:
            a = jnp.concatenate([a, jnp.zeros((LANES - c_rows, LANES), f32)], axis=0)
        return a.T[:, :c_rows]

    acum_t = head_rows(acum)
    dt_t = head_rows(dt)
    a_last = acum[c_rows - 1:c_rows, :]
    xd = xs * _expand_heads(dt * jnp.exp(a_last - acum), ex_ref)
    dsk = dsk_ref[...]
    lo = lax.broadcasted_iota(jnp.int32, (c_rows, LANES), 1) < hp

    ys = []
    for g in range(ngr):
        bg = bm[:, g * ns:(g + 1) * ns]
        cg = cm[:, g * ns:(g + 1) * ns]
        cb_mat = _dot_nt(cg, bg)
        gsl = slice(g * hpg * hp, (g + 1) * hpg * hp)
        y_inter = _dot(cg, st_ref[:, gsl])
        decs = []
        for q in range(hpg // 2):
            h1 = g * hpg + 2 * q
            psl = slice(h1 * hp, h1 * hp + pair)
            ms, es = [], []
            for h in (h1, h1 + 1):
                colb = jnp.broadcast_to(acum[:, h:h + 1], (c_rows, LANES))
                seg = colb[:, :c_rows] - acum_t[h:h + 1, :]
                ms.append(cb_mat * jnp.exp(jnp.where(tril, seg, -jnp.inf)) * dt_t[h:h + 1, :])
                es.append(jnp.exp(colb))
            xp = xs[:, psl]
            x_lo = jnp.where(lo, xp, 0.0)
            x_hi = jnp.where(lo, 0.0, xp)
            if (2 * c_rows) % LANES == 0:
                y_intra = _dot(jnp.concatenate(ms, axis=1), jnp.concatenate([x_lo, x_hi], axis=0))
            else:
                y_intra = _dot(ms[0], x_lo) + _dot(ms[1], x_hi)
            y_pair = y_intra + y_inter[:, q * pair:(q + 1) * pair] * jnp.where(lo, es[0], es[1])
            ys.append(y_pair + dsk[:, psl] * xp)
            decs.append(jnp.where(lo[0:1, :], jnp.exp(a_last[:, h1:h1 + 1]), jnp.exp(a_last[:, h1 + 1:h1 + 2])))
        st_ref[:, gsl] = st_ref[:, gsl] * jnp.concatenate(decs, axis=1) + _dot_tn(bg, xd[:, gsl])

    y = jnp.concatenate(ys, axis=1) * _silu(z)
    nw = nw_ref[...]
    gdim = di // ngr
    for g in range(ngr):
        sl = slice(g * gdim, (g + 1) * gdim)
        yg = y[:, sl]
        y_ref[:, sl] = (yg * lax.rsqrt(jnp.mean(yg * yg, axis=-1, keepdims=True) + RMS_EPS) * nw[:, sl]).astype(y_ref.dtype)

    @pl.when(c == nc - 1)
    def _():
        for r in range(di // LANES):
            so_ref[0, r * LANES:(r + 1) * LANES, :] = st_ref[:, r * LANES:(r + 1) * LANES].T


def ssd_call(zx, conv_w, conv_b, dt_bias, a_log, d_skip, norm_w, s0, conv_buf, bsz, t_pad, t_valid):
    hp, ns = SSD_HEAD_DIM, SSD_STATE
    conv_dim = conv_w.shape[1]
    di = zx.shape[1] - conv_dim - LANES
    nh = di // hp
    c_rows = min(128, t_pad)
    nc = t_pad // c_rows
    has_s0 = s0 is not None
    pad_h = lambda a: jnp.pad(a.reshape(1, nh), ((0, 0), (0, LANES - nh)))
    full = lambda shape: pl.BlockSpec(shape, lambda b, c: (0,) * len(shape))
    sel = (jnp.arange(di)[None, :] // hp == jnp.arange(LANES)[:, None]).astype(bf16)
    expand = jnp.concatenate([sel, sel, sel], axis=0)
    in_specs = [pl.BlockSpec((c_rows, zx.shape[1]), lambda b, c: (b * nc + c, 0)),
                full(conv_w.shape), full((1, conv_dim)), full((1, LANES)), full((1, LANES)),
                full((1, di)), full((1, di)), full(expand.shape)]
    args = [zx, conv_w, conv_b.reshape(1, conv_dim), pad_h(dt_bias), pad_h(a_log),
            jnp.repeat(d_skip, hp).reshape(1, di), norm_w.reshape(1, di), expand]
    if has_s0:
        in_specs += [pl.BlockSpec((1, di, ns), lambda b, c: (b, 0, 0)),
                     pl.BlockSpec((1,) + conv_buf.shape[1:], lambda b, c: (b, 0, 0))]
        args += [s0.reshape(bsz, di, ns), conv_buf]
    kern = functools.partial(_ssd_kernel, c_rows=c_rows, t_valid=t_valid, t_pad=t_pad, has_s0=has_s0)
    y, s_new = pl.pallas_call(
        kern,
        grid=(bsz, nc),
        in_specs=in_specs,
        out_specs=[pl.BlockSpec((c_rows, di), lambda b, c: (b * nc + c, 0)),
                   pl.BlockSpec((1, di, ns), lambda b, c: (b, 0, 0))],
        out_shape=[jax.ShapeDtypeStruct((bsz * t_pad, di), bf16),
                   jax.ShapeDtypeStruct((bsz, di, ns), f32)],
        scratch_shapes=[pltpu.VMEM((ns, di), f32), pltpu.VMEM((SUBLANES, conv_dim), f32)],
        compiler_params=_cparams(2),
        name="ssd",
    )(*args)
    return y, s_new.reshape(bsz, nh, hp, ns)


def _trunk(x, c_k, c_v, s_hg, s_ssm, s_sconv, s_fconv, prm):
    bsz, t, d = x.shape
    depth = prm["norm_mix_w"].shape[0]
    decode = c_k is not None
    tp = -(-t // SUBLANES) * SUBLANES
    if tp != t:
        x = jnp.pad(x, ((0, 0), (0, tp - t), (0, 0)))
    m = bsz * tp
    n_heads = d // ATT_HEAD_DIM
    qd = n_heads * ATT_HEAD_DIM
    kvd = ATT_KV_HEADS * ATT_HEAD_DIM
    x2 = x.reshape(m, d)
    xn = rmsnorm_call(x2, prm["norm_mix_w"][0], bf16)
    nk, nv, nhg, nssm, nsconv, nfconv = [], [], [], [], [], []
    slot = [0] * N_MIXERS
    for i in range(depth):
        kind = i % N_MIXERS
        j = slot[kind]
        slot[kind] += 1
        nw_ffn = prm["norm_ffn_w"][i]
        if kind == 0:
            w_qkv, b_qkv = prm["attn_w_qkv"], prm["attn_b_qkv"][j]
            nq = w_qkv.shape[2]
            if decode:
                qkv = proj_call(xn, w_qkv, j, b_qkv, f32, "attn_qkv").reshape(bsz, tp, nq)
                ck = jnp.transpose(c_k[j], (0, 2, 3, 1)).reshape(bsz, kvd, WINDOW)
                cv = jnp.transpose(c_v[j], (0, 2, 3, 1)).reshape(bsz, kvd, WINDOW)
                o = attn_cache_call(qkv, ck, cv, prm["attn_sinks"][j], n_heads, t).reshape(m, qd)
                to4 = lambda a: a.reshape(bsz, t, ATT_KV_HEADS, ATT_HEAD_DIM)
                k_new = jnp.concatenate([c_k[j][:, t:], to4(qkv[:, :t, qd:qd + kvd])], axis=1)
                v_new = jnp.concatenate([c_v[j][:, t:], to4(qkv[:, :t, qd + kvd:])], axis=1)
            else:
                qkv = proj_call(xn, w_qkv, j, b_qkv, bf16, "attn_qkv")
                o = attn_prompt_call(qkv, prm["attn_sinks"][j], bsz, t, n_heads)
                xn_last = xn.reshape(bsz, t, d)[:, t - WINDOW:].reshape(bsz * WINDOW, d)
                kv_last = proj_call(xn_last, w_qkv[:, :, qd:], j, b_qkv[qd:], f32, "attn_kv_tail").reshape(bsz, WINDOW, 2, kvd)
                to4 = lambda a: a.reshape(bsz, WINDOW, ATT_KV_HEADS, ATT_HEAD_DIM)
                k_new, v_new = to4(kv_last[:, :, 0]), to4(kv_last[:, :, 1])
            nk.append(k_new)
            nv.append(v_new)
            act, w_o, b_o = o, (prm["attn_w_o"], j), prm["attn_b_o"][j]
        elif kind == 1:
            p = proj_call(xn, prm["hgrn_w_in"], j, None, f32, "hgrn_in")
            y, s_new = hgrn_call(p, prm["hgrn_lb_logits"], prm["hgrn_norm_w"][j],
                                 None if s_hg is None else s_hg[j], bsz, tp, t, i)
            nhg.append(s_new)
            act, w_o, b_o = y, (prm["hgrn_w_o"], j), None
        else:
            conv_dim = prm["ssd_conv_w"].shape[2]
            di = prm["ssd_w_o"].shape[1]
            zx = proj_call(xn, prm["ssd_w_in"], j, None, f32, "ssd_in")
            y, s_new = ssd_call(zx, prm["ssd_conv_w"][j], prm["ssd_conv_b"][j], prm["ssd_dt_bias"][j],
                                prm["ssd_a_log"][j], prm["ssd_d"][j], prm["ssd_norm_w"][j],
                                None if s_ssm is None else s_ssm[j],
                                None if s_sconv is None else s_sconv[j], bsz, tp, t)
            nssm.append(s_new)
            kc = prm["ssd_conv_w"].shape[1]
            pre = zx.reshape(bsz, tp, zx.shape[1])[:, max(t - (kc - 1), 0):t, di:di + conv_dim]
            if s_sconv is not None:
                pre = jnp.concatenate([s_sconv[j], pre], axis=1)
            nsconv.append(pre[:, pre.shape[1] - (kc - 1):])
            act, w_o, b_o = y, (prm["ssd_w_o"], j), None
        last = i == depth - 1
        next_w = prm["norm_final_w"] if last else prm["norm_mix_w"][i + 1]
        dff = prm["ffn_w_down"].shape[1]
        kf = prm["ffn_conv_w"].shape[1]
        if s_fconv is not None:
            buf = s_fconv[i]
            e1 = jnp.pad(buf[:, 1:2], ((0, 0), (0, tp - 1), (0, 0))).reshape(m, dff)
            e2 = jnp.pad(buf, ((0, 0), (0, tp - 2), (0, 0))).reshape(m, dff)
        else:
            e1 = e2 = None
        x2, xn, tail = ffn_call(act, w_o, b_o, x2, nw_ffn, (prm["ffn_w_up"], i), prm["ffn_conv_w"][i],
                                prm["ffn_conv_b"][i], (prm["ffn_w_down"], i), next_w, tp, e1, e2,
                                f32 if last else bf16, "ffn")
        if s_fconv is not None:
            a_full = tail.reshape(bsz, tp, dff)[:, :t]
            nfconv.append(jnp.concatenate([s_fconv[i], a_full], axis=1)[:, t:])
        else:
            tiles_per_seq = tp // tail_tile_rows(m)
            tl = tail.reshape(bsz, tiles_per_seq, SUBLANES, dff)[:, -1]
            nfconv.append(tl[:, SUBLANES - (kf - 1):])
    stack = lambda xs: xs[0][None] if len(xs) == 1 else jnp.stack(xs)
    y = xn.reshape(bsz, tp, d)[:, :t]
    return (y, stack(nk), stack(nv), stack(nhg), stack(nssm), stack(nsconv), stack(nfconv))


def tail_tile_rows(m):
    return min(FFN_TM, m)


def kernel(x_prompt, x_sample, cache_attn_k, cache_attn_v, state_hgrn, state_ssm, state_ssm_conv, state_ffn_conv, norm_mix_w, norm_ffn_w, norm_final_w, attn_w_qkv, attn_b_qkv, attn_sinks, attn_w_o, attn_b_o, hgrn_w_in, hgrn_lb_logits, hgrn_norm_w, hgrn_w_o, ssd_w_in, ssd_conv_w, ssd_conv_b, ssd_dt_bias, ssd_a_log, ssd_d, ssd_norm_w, ssd_w_o, ffn_w_up, ffn_conv_w, ffn_conv_b, ffn_w_down):
    cast = lambda w: w.astype(bf16)
    ssd_in_pad = (-ssd_w_in.shape[2]) % LANES
    prm = dict(norm_mix_w=norm_mix_w, norm_ffn_w=norm_ffn_w, norm_final_w=norm_final_w,
               attn_w_qkv=cast(attn_w_qkv), attn_b_qkv=attn_b_qkv, attn_sinks=attn_sinks,
               attn_w_o=cast(attn_w_o), attn_b_o=attn_b_o,
               hgrn_w_in=cast(hgrn_w_in), hgrn_lb_logits=hgrn_lb_logits, hgrn_norm_w=hgrn_norm_w,
               hgrn_w_o=cast(hgrn_w_o),
               ssd_w_in=jnp.pad(cast(ssd_w_in), ((0, 0), (0, 0), (0, ssd_in_pad))), ssd_conv_w=ssd_conv_w,
               ssd_conv_b=ssd_conv_b, ssd_dt_bias=ssd_dt_bias,
               ssd_a_log=ssd_a_log, ssd_d=ssd_d, ssd_norm_w=ssd_norm_w, ssd_w_o=cast(ssd_w_o),
               ffn_w_up=cast(ffn_w_up), ffn_conv_w=ffn_conv_w, ffn_conv_b=ffn_conv_b, ffn_w_down=cast(ffn_w_down))
    outs_p = _trunk(x_prompt, None, None, None, None, None, None, prm)
    outs_s = _trunk(x_sample, cache_attn_k, cache_attn_v, state_hgrn, state_ssm, state_ssm_conv,
                    state_ffn_conv, prm)
    return (outs_p[0], outs_s[0]) + outs_p[1:] + outs_s[1:]
```

```python
import functools

import jax
import jax.numpy as jnp
from jax import lax
from jax.experimental import pallas as pl
from jax.experimental.pallas import tpu as pltpu

bf16 = jnp.bfloat16
f32 = jnp.float32

RMS_EPS = 1e-6
N_MIXERS = 3
WINDOW = 128
PAST_LEN = 8192
ATT_HEAD_DIM = 64
ATT_KV_HEADS = 4
HG_HEADS = 8
HG_CHUNK_GROUP = 32
SSD_HEAD_DIM = 64
SSD_STATE = 128
SSD_GROUPS = 4

LANES = 128
SUBLANES = 8
VMEM_LIMIT = 56 * 1024 * 1024
PROJ_TM = 512
FFN_TM = 512


def _cparams(n_axes):
    return pltpu.CompilerParams(dimension_semantics=("arbitrary",) * n_axes,
                                vmem_limit_bytes=VMEM_LIMIT)


def _dot(a, b):
    return jnp.dot(a.astype(bf16), b.astype(bf16), preferred_element_type=f32)


def _dot_nt(a, b):
    return lax.dot_general(a.astype(bf16), b.astype(bf16), (((1,), (1,)), ((), ())),
                           preferred_element_type=f32)


def _dot_tn(a, b):
    return lax.dot_general(a.astype(bf16), b.astype(bf16), (((0,), (0,)), ((), ())),
                           preferred_element_type=f32)


def _silu(x):
    return x * jax.nn.sigmoid(x)


def _rms(x, w):
    ms = jnp.mean(x * x, axis=-1, keepdims=True)
    return x * lax.rsqrt(ms + RMS_EPS) * w


def _cumsum_rows(x, tril):
    hi = x.astype(bf16)
    r1 = x - hi.astype(f32)
    mid = r1.astype(bf16)
    lo = (r1 - mid.astype(f32)).astype(bf16)
    d = functools.partial(jnp.dot, preferred_element_type=f32)
    return d(tril, hi) + d(tril, mid) + d(tril, lo)


def _tril_mask(c):
    r = lax.broadcasted_iota(jnp.int32, (c, c), 0)
    s = lax.broadcasted_iota(jnp.int32, (c, c), 1)
    return r >= s


def _norm_kernel(x_ref, w_ref, o_ref):
    o_ref[...] = _rms(x_ref[...], w_ref[...]).astype(o_ref.dtype)


def rmsnorm_call(x, w, out_dtype):
    m, d = x.shape
    tm = min(1024, m)
    return pl.pallas_call(
        _norm_kernel,
        grid=(m // tm,),
        in_specs=[pl.BlockSpec((tm, d), lambda i: (i, 0)), pl.BlockSpec((1, d), lambda i: (0, 0))],
        out_specs=pl.BlockSpec((tm, d), lambda i: (i, 0)),
        out_shape=jax.ShapeDtypeStruct((m, d), out_dtype),
        compiler_params=_cparams(1),
        name="rmsnorm",
    )(x, w.reshape(1, d))


def _resident(shape):
    return pl.BlockSpec(shape, lambda *_: (0,) * len(shape), pipeline_mode=pl.Buffered(1))


def _resident_layer(w, layer):
    return pl.BlockSpec((None,) + w.shape[1:], lambda *_: (layer, 0, 0), pipeline_mode=pl.Buffered(1))


def _proj_kernel(x_ref, w_ref, b_ref, o_ref):
    acc = jnp.dot(x_ref[...], w_ref[...], preferred_element_type=f32)
    o_ref[...] = (acc + b_ref[...]).astype(o_ref.dtype)


def proj_call(xn, w, layer, bias, out_dtype, name):
    m, k = xn.shape
    n = w.shape[2]
    tm = min(PROJ_TM, m)
    if bias is None:
        bias = jnp.zeros((n,), f32)
    return pl.pallas_call(
        _proj_kernel,
        grid=(m // tm,),
        in_specs=[pl.BlockSpec((tm, k), lambda i: (i, 0)), _resident_layer(w, layer), _resident((1, n))],
        out_specs=pl.BlockSpec((tm, n), lambda i: (i, 0)),
        out_shape=jax.ShapeDtypeStruct((m, n), out_dtype),
        compiler_params=_cparams(1),
        name=name,
    )(xn, w, bias.reshape(1, n))


def _ffn_kernel(*refs, tm, tf, t_seq, tail, has_state):
    if has_state:
        (act_ref, wo_ref, bo_ref, x_ref, nw1_ref, wu_ref, cw_ref, cb_ref, wd_ref, nw2_ref, e1_ref, e2_ref,
         xo_ref, xn2_ref, tail_ref, carry_ref) = refs
    else:
        (act_ref, wo_ref, bo_ref, x_ref, nw1_ref, wu_ref, cw_ref, cb_ref, wd_ref, nw2_ref,
         xo_ref, xn2_ref, tail_ref, carry_ref) = refs
    i = pl.program_id(0)
    dff = wd_ref.shape[0]

    x1 = jnp.dot(act_ref[...].astype(bf16), wo_ref[...], preferred_element_type=f32) + bo_ref[...] + x_ref[...]
    xn = _rms(x1, nw1_ref[...]).astype(bf16)
    if has_state:
        tpos = (lax.broadcasted_iota(jnp.int32, (tm, 1), 0)) & (t_seq - 1)
    else:
        @pl.when(((i * tm) & (t_seq - 1)) == 0)
        def _():
            carry_ref[...] = jnp.zeros_like(carry_ref)
        rows8 = lax.broadcasted_iota(jnp.int32, (SUBLANES, 1), 0)
    gs = []
    for c0 in range(0, dff, tf):
        cs = slice(c0, c0 + tf)
        a = jnp.dot(xn, wu_ref[:, cs], preferred_element_type=f32)
        b = jnp.dot(xn, wu_ref[:, dff + c0:dff + c0 + tf], preferred_element_type=f32)
        tail_ref[0, :, cs] = a[tm - tail:, :]
        r1 = pltpu.roll(a, 1, 0)
        r2 = pltpu.roll(a, 2, 0)
        if has_state:
            p1 = jnp.where(tpos == 0, e1_ref[:, cs], r1)
            p2 = jnp.where(tpos < 2, e2_ref[:, cs], r2)
        else:
            carry = carry_ref[:, cs]
            top1 = jnp.where(rows8 == 0, carry[7:8, :], r1[:SUBLANES])
            top2 = jnp.where(rows8 == 0, carry[6:7, :], jnp.where(rows8 == 1, carry[7:8, :], r2[:SUBLANES]))
            p1 = jnp.concatenate([top1, r1[SUBLANES:]], axis=0)
            p2 = jnp.concatenate([top2, r2[SUBLANES:]], axis=0)
            carry_ref[:, cs] = a[tm - SUBLANES:, :]
        y = cb_ref[:, cs] + p2 * cw_ref[0:1, cs]
        y = y + p1 * cw_ref[1:2, cs]
        y = y + a * cw_ref[2:3, cs]
        gs.append((_silu(y) * b).astype(bf16))
    g = jnp.concatenate(gs, axis=1)
    xo = jnp.dot(g, wd_ref[...], preferred_element_type=f32) + x1
    xo_ref[...] = xo
    xn2_ref[...] = _rms(xo, nw2_ref[...]).astype(xn2_ref.dtype)


def ffn_call(act, w_o, b_o, x, ffn_norm_w, w_up, conv_w, conv_b, w_down, next_w, t_seq, e1, e2, out_dtype, name):
    m, d = x.shape
    ko = act.shape[1]
    dff = w_down[0].shape[1]
    tf = 2 * LANES
    assert dff % tf == 0 and (t_seq & (t_seq - 1)) == 0
    has_state = e1 is not None
    tm = min(FFN_TM // 2 if has_state else FFN_TM, m)
    assert (tm % t_seq == 0) if has_state else (t_seq % tm == 0)
    tail = tm if has_state else SUBLANES
    if b_o is None:
        b_o = jnp.zeros((d,), f32)
    row = lambda w: pl.BlockSpec((tm, w), lambda i: (i, 0))
    in_specs = [row(ko), _resident_layer(*w_o), _resident((1, d)), row(d), _resident((1, d)),
                _resident_layer(*w_up), _resident((3, dff)), _resident((1, dff)),
                _resident_layer(*w_down), _resident((1, d))]
    args = [act, w_o[0], b_o.reshape(1, d), x, ffn_norm_w.reshape(1, d),
            w_up[0], conv_w, conv_b.reshape(1, dff), w_down[0], next_w.reshape(1, d)]
    if has_state:
        in_specs += [row(dff), row(dff)]
        args += [e1, e2]
    kern = functools.partial(_ffn_kernel, tm=tm, tf=tf, t_seq=t_seq, tail=tail, has_state=has_state)
    return pl.pallas_call(
        kern,
        grid=(m // tm,),
        in_specs=in_specs,
        out_specs=[row(d), row(d), pl.BlockSpec((1, tail, dff), lambda i: (i, 0, 0))],
        out_shape=[jax.ShapeDtypeStruct((m, d), f32),
                   jax.ShapeDtypeStruct((m, d), out_dtype),
                   jax.ShapeDtypeStruct((m // tm, tail, dff), f32)],
        scratch_shapes=[pltpu.VMEM((SUBLANES, dff), f32)],
        compiler_params=_cparams(1),
        name=name,
    )(*args)


def _attn_softmax_pv(s, sink_col, vj):
    m = jnp.maximum(jnp.max(s, axis=-1, keepdims=True), sink_col)
    p = jnp.exp(s - m)
    l = jnp.sum(p, axis=-1, keepdims=True) + jnp.exp(sink_col - m)
    return _dot(p, vj) / l


def _sink_col(sinks_ref, j, group, rows_per_head):
    r = lax.broadcasted_iota(jnp.int32, (group * rows_per_head, 1), 0)
    col = jnp.full((group * rows_per_head, 1), sinks_ref[j * group], f32)
    for p in range(1, group):
        col = jnp.where(r >= p * rows_per_head, sinks_ref[j * group + p], col)
    return col


def _attn_prompt_kernel(sinks_ref, q_ref, kvc_ref, kvp_ref, o_ref, *, n_heads):
    n = pl.program_id(1)
    w, hd, kvh = WINDOW, ATT_HEAD_DIM, ATT_KV_HEADS
    group = n_heads // kvh
    kvw = kvh * hd
    assert group == kvh
    kv = jnp.concatenate([kvp_ref[...], kvc_ref[...]], axis=0)
    k2 = kv[:, :kvw]
    v2 = kv[:, kvw:]
    i = lax.broadcasted_iota(jnp.int32, (w, 2 * w), 0)
    c = lax.broadcasted_iota(jnp.int32, (w, 2 * w), 1)
    band = (c > i) & (c <= i + w) & ((n > 0) | (c >= w))
    mask = jnp.concatenate([band] * group, axis=0)
    slot = lax.broadcasted_iota(jnp.int32, (w, kvw), 1) >> (hd.bit_length() - 1)
    for j in range(kvh):
        qg = q_ref[:, j * kvw:(j + 1) * kvw] * (hd ** -0.5)
        qs = jnp.concatenate([jnp.where(slot == j, pltpu.roll(qg, ((j - p) % kvh) * hd, 1), 0)
                              for p in range(group)], axis=0)
        s = jnp.where(mask, _dot_nt(qs, k2), -jnp.inf)
        of = _attn_softmax_pv(s, _sink_col(sinks_ref, j, group, w), v2).astype(o_ref.dtype)
        og = of[0:w, :]
        for p in range(group):
            piece = of[p * w:(p + 1) * w, :]
            og = jnp.where(slot == p, pltpu.roll(piece, ((p - j) % kvh) * hd, 1), og)
        o_ref[:, j * kvw:(j + 1) * kvw] = og


def attn_prompt_call(qkv, sinks, bsz, t, n_heads):
    w, hd, kvh = WINDOW, ATT_HEAD_DIM, ATT_KV_HEADS
    nb = t // w
    qd = n_heads * hd
    kvd = 2 * kvh * hd
    assert qd % kvd == 0
    kvblk = qd // kvd
    kern = functools.partial(_attn_prompt_kernel, n_heads=n_heads)
    return pl.pallas_call(
        kern,
        grid=(bsz, nb),
        in_specs=[pl.BlockSpec(memory_space=pltpu.SMEM),
                  pl.BlockSpec((w, qd), lambda b, n: (b * nb + n, 0)),
                  pl.BlockSpec((w, kvd), lambda b, n: (b * nb + n, kvblk)),
                  pl.BlockSpec((w, kvd), lambda b, n: (b * nb + jnp.maximum(n - 1, 0), kvblk))],
        out_specs=pl.BlockSpec((w, qd), lambda b, n: (b * nb + n, 0)),
        out_shape=jax.ShapeDtypeStruct((bsz * t, qd), bf16),
        compiler_params=_cparams(2),
        name="attn_prompt",
    )(sinks, qkv, qkv, qkv)


def _attn_cache_kernel(sinks_ref, q_ref, ck_ref, cv_ref, o_ref, *, n_heads, t_valid, bt):
    w, hd, kvh = WINDOW, ATT_HEAD_DIM, ATT_KV_HEADS
    group = n_heads // kvh
    tp = q_ref.shape[1]
    qd = n_heads * hd
    tq = lax.broadcasted_iota(jnp.int32, (tp, w), 0)
    jc = lax.broadcasted_iota(jnp.int32, (tp, w), 1)
    mc = (jc > tq) & (jc - w + PAST_LEN >= 0)
    tq2 = lax.broadcasted_iota(jnp.int32, (tp, tp), 0)
    un = lax.broadcasted_iota(jnp.int32, (tp, tp), 1)
    mn = (un <= tq2) & (un < t_valid)
    mask_c = jnp.concatenate([mc] * n_heads, axis=0)
    mask_n = jnp.concatenate([mn] * n_heads, axis=0)
    kvw = kvh * hd
    assert group == kvh
    slot = lax.broadcasted_iota(jnp.int32, (tp, kvw), 1) >> (hd.bit_length() - 1)
    sink = _sink_col(sinks_ref, 0, n_heads, tp)

    for bi in range(bt):
        row = q_ref[bi]
        ck = ck_ref[bi]
        cv = cv_ref[bi]
        kn = row[:, qd:qd + kvw]
        vn = row[:, qd + kvw:]
        pieces = []
        for j in range(kvh):
            qg = row[:, j * kvw:(j + 1) * kvw] * (hd ** -0.5)
            for p in range(group):
                pieces.append(jnp.where(slot == j, pltpu.roll(qg, ((j - p) % kvh) * hd, 1), 0.0))
        qs = jnp.concatenate(pieces, axis=0)
        sc = jnp.where(mask_c, _dot(qs, ck), -jnp.inf)
        sn = jnp.where(mask_n, _dot_nt(qs, kn), -jnp.inf)
        m = jnp.maximum(jnp.maximum(jnp.max(sc, axis=-1, keepdims=True),
                                    jnp.max(sn, axis=-1, keepdims=True)), sink)
        pc = jnp.exp(sc - m)
        pn = jnp.exp(sn - m)
        l = jnp.sum(pc, axis=-1, keepdims=True) + jnp.sum(pn, axis=-1, keepdims=True) + jnp.exp(sink - m)
        of = (_dot_nt(pc, cv) + _dot(pn, vn)) / l
        for j in range(kvh):
            og = jnp.zeros((tp, kvw), f32)
            for p in range(group):
                h = j * group + p
                og = og + jnp.where(slot == p, pltpu.roll(of[h * tp:(h + 1) * tp, :], ((p - j) % kvh) * hd, 1), 0.0)
            o_ref[bi, :, j * kvw:(j + 1) * kvw] = og


def attn_cache_call(qkv, cache_k, cache_v, sinks, n_heads, t_valid):
    bsz, tp, nq = qkv.shape
    w = WINDOW
    kd = cache_k.shape[1]
    qd = n_heads * ATT_HEAD_DIM
    bt = 8
    kern = functools.partial(_attn_cache_kernel, n_heads=n_heads, t_valid=t_valid, bt=bt)
    return pl.pallas_call(
        kern,
        grid=(bsz // bt,),
        in_specs=[pl.BlockSpec(memory_space=pltpu.SMEM),
                  pl.BlockSpec((bt, tp, nq), lambda b: (b, 0, 0)),
                  pl.BlockSpec((bt, kd, w), lambda b: (b, 0, 0)),
                  pl.BlockSpec((bt, kd, w), lambda b: (b, 0, 0))],
        out_specs=pl.BlockSpec((bt, tp, qd), lambda b: (b, 0, 0)),
        out_shape=jax.ShapeDtypeStruct((bsz, tp, qd), f32),
        compiler_params=_cparams(1),
        name="attn_cache",
    )(sinks, qkv, cache_k, cache_v)


def _hgrn_chunk(x, st, lb, nw, tril, diag_mask, valid, grp):
    nh = HG_HEADS
    c_rows = x.shape[0]
    dim = x.shape[1] // 4
    dk = dim // nh
    ng = c_rows // grp
    if valid is not None:
        x = jnp.where(valid, x, 0.0)
    q = _silu(x[:, 0:dim])
    fg = lb + (1.0 - lb) * jax.nn.sigmoid(x[:, dim:2 * dim])
    logf = jnp.log(fg)
    k = 1.0 - fg
    v = x[:, 2 * dim:3 * dim]
    gate = x[:, 3 * dim:4 * dim]
    if valid is not None:
        logf = jnp.where(valid, logf, 0.0)
        k = jnp.where(valid, k, 0.0)
    bcum = _cumsum_rows(logf, tril.astype(bf16))

    def rows_of(vals):
        return jnp.concatenate([jnp.broadcast_to(r, (grp, dk)) for r in vals], axis=0)

    ys, st_out = [], []
    for h in range(nh):
        sl = slice(h * dk, (h + 1) * dk)
        b = bcum[:, sl]
        qh, kh, vh = q[:, sl], k[:, sl], v[:, sl]
        zero = jnp.zeros((1, dk), f32)
        r = [zero] + [b[i * grp - 1:i * grp, :] for i in range(1, ng)] + [b[c_rows - 1:c_rows, :]]
        mid = [b[i * grp + grp // 2 - 1:i * grp + grp // 2, :] for i in range(ng)]
        b_last = r[ng]
        r_start = rows_of(r[:ng])
        r_end = rows_of(r[1:])
        r_mid = rows_of(mid)
        qg = qh * jnp.exp(b - r_start)
        kt = kh * jnp.exp(r_end - b)
        qm = qh * jnp.exp(b - r_mid)
        km = kh * jnp.exp(r_mid - b)
        a = jnp.where(diag_mask, _dot_nt(qm, km), 0.0)
        if ng > 1:
            lhs, rhs = [], []
            for j in range(ng - 1):
                lhs.append(jnp.concatenate(
                    [jnp.zeros((grp, dk), f32) if i <= j else qg[i * grp:(i + 1) * grp, :] * jnp.exp(r[i] - r[j + 1])
                     for i in range(ng)], axis=0))
                rhs.append(jnp.concatenate(
                    [kt[i * grp:(i + 1) * grp, :] if i == j else jnp.zeros((grp, dk), f32) for i in range(ng)], axis=0))
            a = a + _dot_nt(jnp.concatenate(lhs, axis=1), jnp.concatenate(rhs, axis=1))
        q_inter = qg * jnp.exp(r_start)
        o = _dot(a, vh) + _dot_nt(q_inter, st[h])
        k_dec = kt * jnp.exp(b_last - r_end)
        st_out.append(st[h] * jnp.exp(b_last) + _dot_tn(vh, k_dec))
        on = o * lax.rsqrt(jnp.mean(o * o, axis=-1, keepdims=True) + RMS_EPS)
        ys.append(on * nw[:, sl] * _silu(gate[:, sl]))
    return ys, st_out


def _hgrn_kernel(*refs, c_rows, grp, t_valid, t_pad, layer, has_s0, ub, uc):
    if has_s0:
        p_ref, lbl_ref, nw_ref, s0_ref, y_ref, so_ref, st_ref = refs
    else:
        p_ref, lbl_ref, nw_ref, y_ref, so_ref, st_ref = refs
    c = pl.program_id(1)
    nc = pl.num_programs(1)
    nh = HG_HEADS
    dim = p_ref.shape[1] // 4
    dk = dim // nh

    @pl.when(c == 0)
    def _():
        if has_s0:
            for s in range(ub):
                for h in range(nh):
                    st_ref[s, h] = s0_ref[s, h].T
        else:
            st_ref[...] = jnp.zeros_like(st_ref)

    lbl = lbl_ref[...]
    e = jnp.exp(lbl - jnp.max(lbl, axis=0, keepdims=True))
    sm = e / jnp.sum(e, axis=0, keepdims=True)
    lb = jnp.zeros((1, dim), f32)
    for r in range(1, layer + 1):
        lb = lb + sm[r:r + 1, :]

    tril = _tril_mask(c_rows)
    rr = lax.broadcasted_iota(jnp.int32, (c_rows, c_rows), 0)
    ss = lax.broadcasted_iota(jnp.int32, (c_rows, c_rows), 1)
    gshift = grp.bit_length() - 1
    diag_mask = tril & ((rr >> gshift) == (ss >> gshift))
    nw = nw_ref[...]
    rows = lax.broadcasted_iota(jnp.int32, (c_rows, 1), 0)

    for s in range(ub):
        st = [st_ref[s, h] for h in range(nh)]
        for cc in range(uc):
            r0 = (s * uc + cc) * c_rows
            valid = ((c * uc + cc) * c_rows + rows) < t_valid if t_valid < t_pad else None
            ys, st = _hgrn_chunk(p_ref[r0:r0 + c_rows, :], st, lb, nw, tril, diag_mask, valid, grp)
            for h in range(nh):
                y_ref[r0:r0 + c_rows, h * dk:(h + 1) * dk] = ys[h].astype(y_ref.dtype)
        for h in range(nh):
            st_ref[s, h] = st[h]

    @pl.when(c == nc - 1)
    def _():
        for s in range(ub):
            for h in range(nh):
                so_ref[s, h] = st_ref[s, h].T


def _units_per_step(bsz, nc):
    if nc == 1:
        return (4 if bsz % 4 == 0 else 1), 1
    return 1, (2 if nc % 2 == 0 else 1)


def hgrn_call(p, lb_logits, norm_w, s0, bsz, t_pad, t_valid, layer):
    dim = p.shape[1] // 4
    nh = HG_HEADS
    dk = dim // nh
    c_rows = min(128, t_pad)
    grp = min(HG_CHUNK_GROUP, c_rows)
    nc = t_pad // c_rows
    ub, uc = _units_per_step(bsz, nc)
    rows = ub * uc * c_rows
    ncs = nc // uc
    has_s0 = s0 is not None
    in_specs = [pl.BlockSpec((rows, 4 * dim), lambda b, c: (b * ncs + c, 0)),
                pl.BlockSpec(lb_logits.shape, lambda b, c: (0, 0)),
                pl.BlockSpec((1, dim), lambda b, c: (0, 0))]
    args = [p, lb_logits, norm_w.reshape(1, dim)]
    if has_s0:
        in_specs.append(pl.BlockSpec((ub, nh, dk, dk), lambda b, c: (b, 0, 0, 0)))
        args.append(s0)
    kern = functools.partial(_hgrn_kernel, c_rows=c_rows, grp=grp, t_valid=t_valid, t_pad=t_pad,
                             layer=layer, has_s0=has_s0, ub=ub, uc=uc)
    return pl.pallas_call(
        kern,
        grid=(bsz // ub, ncs),
        in_specs=in_specs,
        out_specs=[pl.BlockSpec((rows, dim), lambda b, c: (b * ncs + c, 0)),
                   pl.BlockSpec((ub, nh, dk, dk), lambda b, c: (b, 0, 0, 0))],
        out_shape=[jax.ShapeDtypeStruct((bsz * t_pad, dim), bf16),
                   jax.ShapeDtypeStruct((bsz, nh, dk, dk), f32)],
        scratch_shapes=[pltpu.VMEM((ub, nh, dk, dk), f32)],
        compiler_params=_cparams(2),
        name="hgrn",
    )(*args)


def _expand_heads(w, ex_ref):
    hi = w.astype(bf16)
    r1 = w - hi.astype(f32)
    mid = r1.astype(bf16)
    lo = (r1 - mid.astype(f32)).astype(bf16)
    return jnp.dot(jnp.concatenate([hi, mid, lo], axis=1), ex_ref[...], preferred_element_type=f32)


def _ssd_kernel(*refs, c_rows, t_valid, t_pad, has_s0, ub, uc):
    if has_s0:
        (zx_ref, cw_ref, cb_ref, dtb_ref, alog_ref, dsk_ref, nw_ref, ex_ref, s0_ref, buf_ref,
         y_ref, so_ref, st_ref, carry_ref) = refs
    else:
        (zx_ref, cw_ref, cb_ref, dtb_ref, alog_ref, dsk_ref, nw_ref, ex_ref,
         y_ref, so_ref, st_ref, carry_ref) = refs
    c = pl.program_id(1)
    nc = pl.num_programs(1)
    hp, ns, ngr = SSD_HEAD_DIM, SSD_STATE, SSD_GROUPS
    di = y_ref.shape[1]
    nh = di // hp
    hpg = nh // ngr
    gw = ngr * ns
    kconv = cw_ref.shape[0]
    pair = 2 * hp
    assert pair == LANES and hpg % 2 == 0

    @pl.when(c == 0)
    def _():
        carry_ref[...] = jnp.zeros_like(carry_ref)
        if has_s0:
            for s in range(ub):
                for r in range(di // LANES):
                    st_ref[s, :, r * LANES:(r + 1) * LANES] = s0_ref[s, r * LANES:(r + 1) * LANES, :].T
                carry_ref[s, SUBLANES - (kconv - 1):, :] = buf_ref[s]
        else:
            st_ref[...] = jnp.zeros_like(st_ref)

    cdim = cw_ref.shape[1]
    rows = lax.broadcasted_iota(jnp.int32, (c_rows, 1), 0)
    rows8 = rows[:SUBLANES]
    cw = cw_ref[...]
    a_neg = -jnp.exp(alog_ref[...])
    tril = _tril_mask(c_rows)
    dsk = dsk_ref[...]
    nw = nw_ref[...]
    lo = lax.broadcasted_iota(jnp.int32, (c_rows, LANES), 1) < hp
    for s in range(ub):
        for cc in range(uc):
            r0 = (s * uc + cc) * c_rows
            valid = ((c * uc + cc) * c_rows + rows) < t_valid if t_valid < t_pad else None
            _ssd_chunk(zx_ref[r0:r0 + c_rows, :], st_ref.at[s], carry_ref.at[s], y_ref.at[r0:r0 + c_rows, :],
                       cw, cb_ref[...], dtb_ref[...], a_neg, dsk, nw, ex_ref, tril, lo, rows8, valid)

    @pl.when(c == nc - 1)
    def _():
        for s in range(ub):
            for r in range(di // LANES):
                so_ref[s, r * LANES:(r + 1) * LANES, :] = st_ref[s, :, r * LANES:(r + 1) * LANES].T


def _ssd_chunk(zx, st_ref, carry_ref, y_ref, cw, cb, dtb, a_neg, dsk, nw, ex_ref, tril, lo, rows8, valid):
    hp, ns, ngr = SSD_HEAD_DIM, SSD_STATE, SSD_GROUPS
    c_rows = zx.shape[0]
    di = y_ref.shape[1]
    nh = di // hp
    hpg = nh // ngr
    gw = ngr * ns
    kconv, cdim = cw.shape
    pair = 2 * hp
    z = zx[:, :di]
    u = zx[:, di:di + cdim]
    cr = carry_ref[...]
    yc = cb
    for kk in range(kconv - 1):
        back = kconv - 1 - kk
        sh = pltpu.roll(u, back, 0)
        top = sh[:SUBLANES]
        for r0 in range(back):
            top = jnp.where(rows8 == r0, cr[SUBLANES - back + r0:SUBLANES - back + r0 + 1, :], top)
        sh = jnp.concatenate([top, sh[SUBLANES:]], axis=0) if c_rows > SUBLANES else top
        yc = yc + sh * cw[kk:kk + 1, :]
    yc = yc + u * cw[kconv - 1:kconv, :]
    carry_ref[...] = u[c_rows - SUBLANES:, :]
    xbc = _silu(yc)
    dt = jax.nn.softplus(zx[:, di + cdim:] + dtb)
    if valid is not None:
        xbc = jnp.where(valid, xbc, 0.0)
        dt = jnp.where(valid, dt, 0.0)
    xs = xbc[:, :di]
    bm = xbc[:, di:di + gw]
    cm = xbc[:, di + gw:di + 2 * gw]
    acum = _cumsum_rows(dt * a_neg, tril.astype(bf16))

    def head_rows(a):
        if c_rows < LANES:
            a = jnp.concatenate([a, jnp.zeros((LANES - c_rows, LANES), f32)], axis=0)
        return a.T[:, :c_rows]

    acum_t = head_rows(acum)
    dt_t = head_rows(dt)
    a_last = acum[c_rows - 1:c_rows, :]
    xd = xs * _expand_heads(dt * jnp.exp(a_last - acum), ex_ref)

    ys = []
    for g in range(ngr):
        bg = bm[:, g * ns:(g + 1) * ns]
        cg = cm[:, g * ns:(g + 1) * ns]
        cb_mat = _dot_nt(cg, bg)
        gsl = slice(g * hpg * hp, (g + 1) * hpg * hp)
        y_inter = _dot(cg, st_ref[:, gsl])
        decs = []
        for q in range(hpg // 2):
            h1 = g * hpg + 2 * q
            psl = slice(h1 * hp, h1 * hp + pair)
            ms, es = [], []
            for h in (h1, h1 + 1):
                colb = jnp.broadcast_to(acum[:, h:h + 1], (c_rows, LANES))
                seg = colb[:, :c_rows] - acum_t[h:h + 1, :]
                ms.append(cb_mat * jnp.exp(jnp.where(tril, seg, -jnp.inf)) * dt_t[h:h + 1, :])
                es.append(jnp.exp(colb))
            xp = xs[:, psl]
            x_lo = jnp.where(lo, xp, 0.0)
            x_hi = jnp.where(lo, 0.0, xp)
            if (2 * c_rows) % LANES == 0:
                y_intra = _dot(jnp.concatenate(ms, axis=1), jnp.concatenate([x_lo, x_hi], axis=0))
            else:
                y_intra = _dot(ms[0], x_lo) + _dot(ms[1], x_hi)
            y_pair = y_intra + y_inter[:, q * pair:(q + 1) * pair] * jnp.where(lo, es[0], es[1])
            ys.append(y_pair + dsk[:, psl] * xp)
            decs.append(jnp.where(lo[0:1, :], jnp.exp(a_last[:, h1:h1 + 1]), jnp.exp(a_last[:, h1 + 1:h1 + 2])))
        st_ref[:, gsl] = st_ref[:, gsl] * jnp.concatenate(decs, axis=1) + _dot_tn(bg, xd[:, gsl])

    y = jnp.concatenate(ys, axis=1) * _silu(z)
    gdim = di // ngr
    for g in range(ngr):
        sl = slice(g * gdim, (g + 1) * gdim)
        yg = y[:, sl]
        y_ref[:, sl] = (yg * lax.rsqrt(jnp.mean(yg * yg, axis=-1, keepdims=True) + RMS_EPS) * nw[:, sl]).astype(y_ref.dtype)


def ssd_call(zx, conv_w, conv_b, dt_bias, a_log, d_skip, norm_w, s0, conv_buf, bsz, t_pad, t_valid):
    hp, ns = SSD_HEAD_DIM, SSD_STATE
    conv_dim = conv_w.shape[1]
    di = zx.shape[1] - conv_dim - LANES
    nh = di // hp
    c_rows = min(128, t_pad)
    nc = t_pad // c_rows
    ub, uc = _units_per_step(bsz, nc)
    rows = ub * uc * c_rows
    ncs = nc // uc
    has_s0 = s0 is not None
    pad_h = lambda a: jnp.pad(a.reshape(1, nh), ((0, 0), (0, LANES - nh)))
    full = lambda shape: pl.BlockSpec(shape, lambda b, c: (0,) * len(shape))
    sel = (jnp.arange(di)[None, :] // hp == jnp.arange(LANES)[:, None]).astype(bf16)
    expand = jnp.concatenate([sel, sel, sel], axis=0)
    in_specs = [pl.BlockSpec((rows, zx.shape[1]), lambda b, c: (b * ncs + c, 0)),
                full(conv_w.shape), full((1, conv_dim)), full((1, LANES)), full((1, LANES)),
                full((1, di)), full((1, di)), full(expand.shape)]
    args = [zx, conv_w, conv_b.reshape(1, conv_dim), pad_h(dt_bias), pad_h(a_log),
            jnp.repeat(d_skip, hp).reshape(1, di), norm_w.reshape(1, di), expand]
    if has_s0:
        in_specs += [pl.BlockSpec((ub, di, ns), lambda b, c: (b, 0, 0)),
                     pl.BlockSpec((ub,) + conv_buf.shape[1:], lambda b, c: (b, 0, 0))]
        args += [s0.reshape(bsz, di, ns), conv_buf]
    kern = functools.partial(_ssd_kernel, c_rows=c_rows, t_valid=t_valid, t_pad=t_pad, has_s0=has_s0, ub=ub, uc=uc)
    y, s_new = pl.pallas_call(
        kern,
        grid=(bsz // ub, ncs),
        in_specs=in_specs,
        out_specs=[pl.BlockSpec((rows, di), lambda b, c: (b * ncs + c, 0)),
                   pl.BlockSpec((ub, di, ns), lambda b, c: (b, 0, 0))],
        out_shape=[jax.ShapeDtypeStruct((bsz * t_pad, di), bf16),
                   jax.ShapeDtypeStruct((bsz, di, ns), f32)],
        scratch_shapes=[pltpu.VMEM((ub, ns, di), f32), pltpu.VMEM((ub, SUBLANES, conv_dim), f32)],
        compiler_params=_cparams(2),
        name="ssd",
    )(*args)
    return y, s_new.reshape(bsz, nh, hp, ns)


def _trunk(x, c_k, c_v, s_hg, s_ssm, s_sconv, s_fconv, prm):
    bsz, t, d = x.shape
    depth = prm["norm_mix_w"].shape[0]
    decode = c_k is not None
    tp = -(-t // SUBLANES) * SUBLANES
    if tp != t:
        x = jnp.pad(x, ((0, 0), (0, tp - t), (0, 0)))
    m = bsz * tp
    n_heads = d // ATT_HEAD_DIM
    qd = n_heads * ATT_HEAD_DIM
    kvd = ATT_KV_HEADS * ATT_HEAD_DIM
    x2 = x.reshape(m, d)
    xn = rmsnorm_call(x2, prm["norm_mix_w"][0], bf16)
    nk, nv, nhg, nssm, nsconv, nfconv = [], [], [], [], [], []
    slot = [0] * N_MIXERS
    for i in range(depth):
        kind = i % N_MIXERS
        j = slot[kind]
        slot[kind] += 1
        nw_ffn = prm["norm_ffn_w"][i]
        if kind == 0:
            w_qkv, b_qkv = prm["attn_w_qkv"], prm["attn_b_qkv"][j]
            nq = w_qkv.shape[2]
            if decode:
                qkv = proj_call(xn, w_qkv, j, b_qkv, f32, "attn_qkv").reshape(bsz, tp, nq)
                ck = jnp.transpose(c_k[j], (0, 2, 3, 1)).reshape(bsz, kvd, WINDOW)
                cv = jnp.transpose(c_v[j], (0, 2, 3, 1)).reshape(bsz, kvd, WINDOW)
                o = attn_cache_call(qkv, ck, cv, prm["attn_sinks"][j], n_heads, t).reshape(m, qd)
                to4 = lambda a: a.reshape(bsz, t, ATT_KV_HEADS, ATT_HEAD_DIM)
                k_new = to4(qkv[:, :t, qd:qd + kvd])
                v_new = to4(qkv[:, :t, qd + kvd:])
            else:
                qkv = proj_call(xn, w_qkv, j, b_qkv, bf16, "attn_qkv")
                o = attn_prompt_call(qkv, prm["attn_sinks"][j], bsz, t, n_heads)
                xn_last = xn.reshape(bsz, t, d)[:, t - WINDOW:].reshape(bsz * WINDOW, d)
                kv_last = proj_call(xn_last, w_qkv[:, :, qd:], j, b_qkv[qd:], f32, "attn_kv_tail").reshape(bsz, WINDOW, 2, kvd)
                to4 = lambda a: a.reshape(bsz, WINDOW, ATT_KV_HEADS, ATT_HEAD_DIM)
                k_new, v_new = to4(kv_last[:, :, 0]), to4(kv_last[:, :, 1])
            nk.append(k_new)
            nv.append(v_new)
            act, w_o, b_o = o, (prm["attn_w_o"], j), prm["attn_b_o"][j]
        elif kind == 1:
            p = proj_call(xn, prm["hgrn_w_in"], j, None, f32, "hgrn_in")
            y, s_new = hgrn_call(p, prm["hgrn_lb_logits"], prm["hgrn_norm_w"][j],
                                 None if s_hg is None else s_hg[j], bsz, tp, t, i)
            nhg.append(s_new)
            act, w_o, b_o = y, (prm["hgrn_w_o"], j), None
        else:
            conv_dim = prm["ssd_conv_w"].shape[2]
            di = prm["ssd_w_o"].shape[1]
            zx = proj_call(xn, prm["ssd_w_in"], j, None, f32, "ssd_in")
            y, s_new = ssd_call(zx, prm["ssd_conv_w"][j], prm["ssd_conv_b"][j], prm["ssd_dt_bias"][j],
                                prm["ssd_a_log"][j], prm["ssd_d"][j], prm["ssd_norm_w"][j],
                                None if s_ssm is None else s_ssm[j],
                                None if s_sconv is None else s_sconv[j], bsz, tp, t)
            nssm.append(s_new)
            kc = prm["ssd_conv_w"].shape[1]
            pre = zx.reshape(bsz, tp, zx.shape[1])[:, max(t - (kc - 1), 0):t, di:di + conv_dim]
            if s_sconv is not None:
                pre = jnp.concatenate([s_sconv[j], pre], axis=1)
            nsconv.append(pre[:, pre.shape[1] - (kc - 1):])
            act, w_o, b_o = y, (prm["ssd_w_o"], j), None
        last = i == depth - 1
        next_w = prm["norm_final_w"] if last else prm["norm_mix_w"][i + 1]
        dff = prm["ffn_w_down"].shape[1]
        kf = prm["ffn_conv_w"].shape[1]
        if s_fconv is not None:
            buf = s_fconv[i]
            e1 = jnp.pad(buf[:, 1:2], ((0, 0), (0, tp - 1), (0, 0))).reshape(m, dff)
            e2 = jnp.pad(buf, ((0, 0), (0, tp - 2), (0, 0))).reshape(m, dff)
        else:
            e1 = e2 = None
        x2, xn, tail = ffn_call(act, w_o, b_o, x2, nw_ffn, (prm["ffn_w_up"], i), prm["ffn_conv_w"][i],
                                prm["ffn_conv_b"][i], (prm["ffn_w_down"], i), next_w, tp, e1, e2,
                                f32 if last else bf16, "ffn")
        if s_fconv is not None:
            a_full = tail.reshape(bsz, tp, dff)[:, :t]
            nfconv.append(jnp.concatenate([s_fconv[i], a_full], axis=1)[:, t:])
        else:
            tiles_per_seq = tp // tail_tile_rows(m)
            tl = tail.reshape(bsz, tiles_per_seq, SUBLANES, dff)[:, -1]
            nfconv.append(tl[:, SUBLANES - (kf - 1):])
    stack = lambda xs: xs[0][None] if len(xs) == 1 else jnp.stack(xs)
    y = xn.reshape(bsz, tp, d)[:, :t]
    nk, nv = stack(nk), stack(nv)
    if decode:
        nk = jnp.concatenate([c_k[:, :, t:], nk], axis=2)
        nv = jnp.concatenate([c_v[:, :, t:], nv], axis=2)
    return (y, nk, nv, stack(nhg), stack(nssm), stack(nsconv), stack(nfconv))


def tail_tile_rows(m):
    return min(FFN_TM, m)


def kernel(x_prompt, x_sample, cache_attn_k, cache_attn_v, state_hgrn, state_ssm, state_ssm_conv, state_ffn_conv, norm_mix_w, norm_ffn_w, norm_final_w, attn_w_qkv, attn_b_qkv, attn_sinks, attn_w_o, attn_b_o, hgrn_w_in, hgrn_lb_logits, hgrn_norm_w, hgrn_w_o, ssd_w_in, ssd_conv_w, ssd_conv_b, ssd_dt_bias, ssd_a_log, ssd_d, ssd_norm_w, ssd_w_o, ffn_w_up, ffn_conv_w, ffn_conv_b, ffn_w_down):
    cast = lambda w: w.astype(bf16)
    ssd_in_pad = (-ssd_w_in.shape[2]) % LANES
    prm = dict(norm_mix_w=norm_mix_w, norm_ffn_w=norm_ffn_w, norm_final_w=norm_final_w,
               attn_w_qkv=cast(attn_w_qkv), attn_b_qkv=attn_b_qkv, attn_sinks=attn_sinks,
               attn_w_o=cast(attn_w_o), attn_b_o=attn_b_o,
               hgrn_w_in=cast(hgrn_w_in), hgrn_lb_logits=hgrn_lb_logits, hgrn_norm_w=hgrn_norm_w,
               hgrn_w_o=cast(hgrn_w_o),
               ssd_w_in=jnp.pad(cast(ssd_w_in), ((0, 0), (0, 0), (0, ssd_in_pad))), ssd_conv_w=ssd_conv_w,
               ssd_conv_b=ssd_conv_b, ssd_dt_bias=ssd_dt_bias,
               ssd_a_log=ssd_a_log, ssd_d=ssd_d, ssd_norm_w=ssd_norm_w, ssd_w_o=cast(ssd_w_o),
               ffn_w_up=cast(ffn_w_up), ffn_conv_w=ffn_conv_w, ffn_conv_b=ffn_conv_b, ffn_w_down=cast(ffn_w_down))
    outs_p = _trunk(x_prompt, None, None, None, None, None, None, prm)
    outs_s = _trunk(x_sample, cache_attn_k, cache_attn_v, state_hgrn, state_ssm, state_ssm_conv,
                    state_ffn_conv, prm)
    return (outs_p[0], outs_s[0]) + outs_p[1:] + outs_s[1:]
```

```python
import functools

import jax
import jax.numpy as jnp
from jax import lax
from jax.experimental import pallas as pl
from jax.experimental.pallas import tpu as pltpu

bf16 = jnp.bfloat16
f32 = jnp.float32

RMS_EPS = 1e-6
N_MIXERS = 3
WINDOW = 128
PAST_LEN = 8192
ATT_HEAD_DIM = 64
ATT_KV_HEADS = 4
HG_HEADS = 8
HG_CHUNK_GROUP = 32
SSD_HEAD_DIM = 64
SSD_STATE = 128
SSD_GROUPS = 4

LANES = 128
SUBLANES = 8
VMEM_LIMIT = 56 * 1024 * 1024
PROJ_TM = 512
FFN_TM = 512


def _cparams(n_axes):
    return pltpu.CompilerParams(dimension_semantics=("arbitrary",) * n_axes,
                                vmem_limit_bytes=VMEM_LIMIT)


def _dot(a, b):
    return jnp.dot(a.astype(bf16), b.astype(bf16), preferred_element_type=f32)


def _dot_nt(a, b):
    return lax.dot_general(a.astype(bf16), b.astype(bf16), (((1,), (1,)), ((), ())),
                           preferred_element_type=f32)


def _dot_tn(a, b):
    return lax.dot_general(a.astype(bf16), b.astype(bf16), (((0,), (0,)), ((), ())),
                           preferred_element_type=f32)


def _silu(x):
    return x * jax.nn.sigmoid(x)


def _rms(x, w):
    ms = jnp.mean(x * x, axis=-1, keepdims=True)
    return x * lax.rsqrt(ms + RMS_EPS) * w


def _cumsum_rows(x, tril):
    hi = x.astype(bf16)
    r1 = x - hi.astype(f32)
    mid = r1.astype(bf16)
    lo = (r1 - mid.astype(f32)).astype(bf16)
    d = functools.partial(jnp.dot, preferred_element_type=f32)
    return d(tril, hi) + d(tril, mid) + d(tril, lo)


def _tril_mask(c):
    r = lax.broadcasted_iota(jnp.int32, (c, c), 0)
    s = lax.broadcasted_iota(jnp.int32, (c, c), 1)
    return r >= s


def _norm_kernel(x_ref, w_ref, o_ref):
    o_ref[...] = _rms(x_ref[...], w_ref[...]).astype(o_ref.dtype)


def rmsnorm_call(x, w, out_dtype):
    m, d = x.shape
    tm = min(1024, m)
    return pl.pallas_call(
        _norm_kernel,
        grid=(m // tm,),
        in_specs=[pl.BlockSpec((tm, d), lambda i: (i, 0)), pl.BlockSpec((1, d), lambda i: (0, 0))],
        out_specs=pl.BlockSpec((tm, d), lambda i: (i, 0)),
        out_shape=jax.ShapeDtypeStruct((m, d), out_dtype),
        compiler_params=_cparams(1),
        name="rmsnorm",
    )(x, w.reshape(1, d))


def _resident(shape):
    return pl.BlockSpec(shape, lambda *_: (0,) * len(shape), pipeline_mode=pl.Buffered(1))


def _resident_layer(w, layer):
    return pl.BlockSpec((None,) + w.shape[1:], lambda *_: (layer, 0, 0), pipeline_mode=pl.Buffered(1))


def _proj_kernel(x_ref, w_ref, b_ref, o_ref):
    acc = jnp.dot(x_ref[...], w_ref[...], preferred_element_type=f32)
    o_ref[...] = (acc + b_ref[...]).astype(o_ref.dtype)


def proj_call(xn, w, layer, bias, out_dtype, name):
    m, k = xn.shape
    n = w.shape[2]
    tm = min(PROJ_TM, m)
    if bias is None:
        bias = jnp.zeros((n,), f32)
    return pl.pallas_call(
        _proj_kernel,
        grid=(m // tm,),
        in_specs=[pl.BlockSpec((tm, k), lambda i: (i, 0)), _resident_layer(w, layer), _resident((1, n))],
        out_specs=pl.BlockSpec((tm, n), lambda i: (i, 0)),
        out_shape=jax.ShapeDtypeStruct((m, n), out_dtype),
        compiler_params=_cparams(1),
        name=name,
    )(xn, w, bias.reshape(1, n))


def _causal_conv(u, prev, cw, cb, rows8):
    kconv = cw.shape[0]
    yc = cb
    for kk in range(kconv - 1):
        back = kconv - 1 - kk
        sh = pltpu.roll(u, back, 0)
        top = sh[:SUBLANES]
        for r0 in range(back):
            top = jnp.where(rows8 == r0, prev[SUBLANES - back + r0:SUBLANES - back + r0 + 1, :], top)
        sh = jnp.concatenate([top, sh[SUBLANES:]], axis=0) if u.shape[0] > SUBLANES else top
        yc = yc + sh * cw[kk:kk + 1, :]
    return yc + u * cw[kconv - 1:kconv, :]


def _ssd_in_kernel(x_ref, w_ref, cw_ref, cb_ref, o_ref, tail_ref, carry_ref, *, tm, tc, t_seq, di):
    i = pl.program_id(0)
    cdim = cw_ref.shape[1]

    @pl.when(((i * tm) & (t_seq - 1)) == 0)
    def _():
        carry_ref[...] = jnp.zeros_like(carry_ref)

    x = x_ref[...]
    rows8 = lax.broadcasted_iota(jnp.int32, (SUBLANES, 1), 0)
    nz = di // tc
    for idx, c0 in enumerate(range(0, cdim, tc)):
        cs = slice(c0, c0 + tc)
        u = jnp.dot(x, w_ref[:, di + c0:di + c0 + tc], preferred_element_type=f32)
        tail_ref[0, :, cs] = u[tm - SUBLANES:, :]
        yc = _causal_conv(u, carry_ref[:, cs], cw_ref[:, cs], cb_ref[:, cs], rows8)
        carry_ref[:, cs] = u[tm - SUBLANES:, :]
        o_ref[:, di + c0:di + c0 + tc] = _silu(yc)
        if idx < nz:
            o_ref[:, idx * tc:(idx + 1) * tc] = jnp.dot(x, w_ref[:, idx * tc:(idx + 1) * tc],
                                                        preferred_element_type=f32)
    o_ref[:, di + cdim:] = jnp.dot(x, w_ref[:, di + cdim:], preferred_element_type=f32)


def ssd_in_call(xn, w, layer, conv_w, conv_b, t_seq):
    m, k = xn.shape
    n = w.shape[2]
    cdim = conv_w.shape[1]
    di = n - cdim - LANES
    tm = min(PROJ_TM, m)
    tc = 2 * LANES
    assert t_seq % tm == 0 and cdim % tc == 0 and di % tc == 0 and di <= cdim and (t_seq & (t_seq - 1)) == 0
    kern = functools.partial(_ssd_in_kernel, tm=tm, tc=tc, t_seq=t_seq, di=di)
    return pl.pallas_call(
        kern,
        grid=(m // tm,),
        in_specs=[pl.BlockSpec((tm, k), lambda i: (i, 0)), _resident_layer(w, layer),
                  _resident(conv_w.shape), _resident((1, cdim))],
        out_specs=[pl.BlockSpec((tm, n), lambda i: (i, 0)), pl.BlockSpec((1, SUBLANES, cdim), lambda i: (i, 0, 0))],
        out_shape=[jax.ShapeDtypeStruct((m, n), f32), jax.ShapeDtypeStruct((m // tm, SUBLANES, cdim), f32)],
        scratch_shapes=[pltpu.VMEM((SUBLANES, cdim), f32)],
        compiler_params=_cparams(1),
        name="ssd_in_conv",
    )(xn, w, conv_w, conv_b.reshape(1, cdim))


def _ffn_kernel(*refs, tm, tf, t_seq, tail, has_state):
    if has_state:
        (act_ref, wo_ref, bo_ref, x_ref, nw1_ref, wu_ref, cw_ref, cb_ref, wd_ref, nw2_ref, e1_ref, e2_ref,
         xo_ref, xn2_ref, tail_ref, carry_ref) = refs
    else:
        (act_ref, wo_ref, bo_ref, x_ref, nw1_ref, wu_ref, cw_ref, cb_ref, wd_ref, nw2_ref,
         xo_ref, xn2_ref, tail_ref, carry_ref) = refs
    i = pl.program_id(0)
    dff = wd_ref.shape[0]

    x1 = jnp.dot(act_ref[...].astype(bf16), wo_ref[...], preferred_element_type=f32) + bo_ref[...] + x_ref[...]
    xn = _rms(x1, nw1_ref[...]).astype(bf16)
    if has_state:
        tpos = (lax.broadcasted_iota(jnp.int32, (tm, 1), 0)) & (t_seq - 1)
    else:
        @pl.when(((i * tm) & (t_seq - 1)) == 0)
        def _():
            carry_ref[...] = jnp.zeros_like(carry_ref)
        rows8 = lax.broadcasted_iota(jnp.int32, (SUBLANES, 1), 0)
    gs = []
    for c0 in range(0, dff, tf):
        cs = slice(c0, c0 + tf)
        a = jnp.dot(xn, wu_ref[:, cs], preferred_element_type=f32)
        b = jnp.dot(xn, wu_ref[:, dff + c0:dff + c0 + tf], preferred_element_type=f32)
        tail_ref[0, :, cs] = a[tm - tail:, :]
        r1 = pltpu.roll(a, 1, 0)
        r2 = pltpu.roll(a, 2, 0)
        if has_state:
            p1 = jnp.where(tpos == 0, e1_ref[:, cs], r1)
            p2 = jnp.where(tpos < 2, e2_ref[:, cs], r2)
        else:
            carry = carry_ref[:, cs]
            top1 = jnp.where(rows8 == 0, carry[7:8, :], r1[:SUBLANES])
            top2 = jnp.where(rows8 == 0, carry[6:7, :], jnp.where(rows8 == 1, carry[7:8, :], r2[:SUBLANES]))
            p1 = jnp.concatenate([top1, r1[SUBLANES:]], axis=0)
            p2 = jnp.concatenate([top2, r2[SUBLANES:]], axis=0)
            carry_ref[:, cs] = a[tm - SUBLANES:, :]
        y = cb_ref[:, cs] + p2 * cw_ref[0:1, cs]
        y = y + p1 * cw_ref[1:2, cs]
        y = y + a * cw_ref[2:3, cs]
        gs.append((_silu(y) * b).astype(bf16))
    g = jnp.concatenate(gs, axis=1)
    xo = jnp.dot(g, wd_ref[...], preferred_element_type=f32) + x1
    xo_ref[...] = xo
    xn2_ref[...] = _rms(xo, nw2_ref[...]).astype(xn2_ref.dtype)


def ffn_call(act, w_o, b_o, x, ffn_norm_w, w_up, conv_w, conv_b, w_down, next_w, t_seq, e1, e2, out_dtype, name):
    m, d = x.shape
    ko = act.shape[1]
    dff = w_down[0].shape[1]
    tf = 2 * LANES
    assert dff % tf == 0 and (t_seq & (t_seq - 1)) == 0
    has_state = e1 is not None
    tm = min(FFN_TM // 2 if has_state else FFN_TM, m)
    assert (tm % t_seq == 0) if has_state else (t_seq % tm == 0)
    tail = tm if has_state else SUBLANES
    if b_o is None:
        b_o = jnp.zeros((d,), f32)
    row = lambda w: pl.BlockSpec((tm, w), lambda i: (i, 0))
    in_specs = [row(ko), _resident_layer(*w_o), _resident((1, d)), row(d), _resident((1, d)),
                _resident_layer(*w_up), _resident((3, dff)), _resident((1, dff)),
                _resident_layer(*w_down), _resident((1, d))]
    args = [act, w_o[0], b_o.reshape(1, d), x, ffn_norm_w.reshape(1, d),
            w_up[0], conv_w, conv_b.reshape(1, dff), w_down[0], next_w.reshape(1, d)]
    if has_state:
        in_specs += [row(dff), row(dff)]
        args += [e1, e2]
    kern = functools.partial(_ffn_kernel, tm=tm, tf=tf, t_seq=t_seq, tail=tail, has_state=has_state)
    return pl.pallas_call(
        kern,
        grid=(m // tm,),
        in_specs=in_specs,
        out_specs=[row(d), row(d), pl.BlockSpec((1, tail, dff), lambda i: (i, 0, 0))],
        out_shape=[jax.ShapeDtypeStruct((m, d), f32),
                   jax.ShapeDtypeStruct((m, d), out_dtype),
                   jax.ShapeDtypeStruct((m // tm, tail, dff), f32)],
        scratch_shapes=[pltpu.VMEM((SUBLANES, dff), f32)],
        compiler_params=_cparams(1),
        name=name,
    )(*args)


def _attn_softmax_pv(s, sink_col, vj):
    m = jnp.maximum(jnp.max(s, axis=-1, keepdims=True), sink_col)
    p = jnp.exp(s - m)
    l = jnp.sum(p, axis=-1, keepdims=True) + jnp.exp(sink_col - m)
    return _dot(p, vj) / l


def _sink_col(sinks_ref, j, group, rows_per_head):
    r = lax.broadcasted_iota(jnp.int32, (group * rows_per_head, 1), 0)
    col = jnp.full((group * rows_per_head, 1), sinks_ref[j * group], f32)
    for p in range(1, group):
        col = jnp.where(r >= p * rows_per_head, sinks_ref[j * group + p], col)
    return col


def _attn_prompt_kernel(sinks_ref, q_ref, kvc_ref, kvp_ref, o_ref, *, n_heads):
    n = pl.program_id(1)
    w, hd, kvh = WINDOW, ATT_HEAD_DIM, ATT_KV_HEADS
    group = n_heads // kvh
    kvw = kvh * hd
    assert group == kvh
    kv = jnp.concatenate([kvp_ref[...], kvc_ref[...]], axis=0)
    k2 = kv[:, :kvw]
    v2 = kv[:, kvw:]
    i = lax.broadcasted_iota(jnp.int32, (w, 2 * w), 0)
    c = lax.broadcasted_iota(jnp.int32, (w, 2 * w), 1)
    band = (c > i) & (c <= i + w) & ((n > 0) | (c >= w))
    mask = jnp.concatenate([band] * group, axis=0)
    slot = lax.broadcasted_iota(jnp.int32, (w, kvw), 1) >> (hd.bit_length() - 1)
    for j in range(kvh):
        qg = q_ref[:, j * kvw:(j + 1) * kvw] * (hd ** -0.5)
        qs = jnp.concatenate([jnp.where(slot == j, pltpu.roll(qg, ((j - p) % kvh) * hd, 1), 0)
                              for p in range(group)], axis=0)
        s = jnp.where(mask, _dot_nt(qs, k2), -jnp.inf)
        of = _attn_softmax_pv(s, _sink_col(sinks_ref, j, group, w), v2).astype(o_ref.dtype)
        og = of[0:w, :]
        for p in range(group):
            piece = of[p * w:(p + 1) * w, :]
            og = jnp.where(slot == p, pltpu.roll(piece, ((p - j) % kvh) * hd, 1), og)
        o_ref[:, j * kvw:(j + 1) * kvw] = og


def attn_prompt_call(qkv, sinks, bsz, t, n_heads):
    w, hd, kvh = WINDOW, ATT_HEAD_DIM, ATT_KV_HEADS
    nb = t // w
    qd = n_heads * hd
    kvd = 2 * kvh * hd
    assert qd % kvd == 0
    kvblk = qd // kvd
    kern = functools.partial(_attn_prompt_kernel, n_heads=n_heads)
    return pl.pallas_call(
        kern,
        grid=(bsz, nb),
        in_specs=[pl.BlockSpec(memory_space=pltpu.SMEM),
                  pl.BlockSpec((w, qd), lambda b, n: (b * nb + n, 0)),
                  pl.BlockSpec((w, kvd), lambda b, n: (b * nb + n, kvblk)),
                  pl.BlockSpec((w, kvd), lambda b, n: (b * nb + jnp.maximum(n - 1, 0), kvblk))],
        out_specs=pl.BlockSpec((w, qd), lambda b, n: (b * nb + n, 0)),
        out_shape=jax.ShapeDtypeStruct((bsz * t, qd), bf16),
        compiler_params=_cparams(2),
        name="attn_prompt",
    )(sinks, qkv, qkv, qkv)


def _attn_cache_kernel(sinks_ref, q_ref, ck_ref, cv_ref, o_ref, *, n_heads, t_valid, bt):
    w, hd, kvh = WINDOW, ATT_HEAD_DIM, ATT_KV_HEADS
    group = n_heads // kvh
    tp = q_ref.shape[1]
    qd = n_heads * hd
    tq = lax.broadcasted_iota(jnp.int32, (tp, w), 0)
    jc = lax.broadcasted_iota(jnp.int32, (tp, w), 1)
    mc = (jc > tq) & (jc - w + PAST_LEN >= 0)
    tq2 = lax.broadcasted_iota(jnp.int32, (tp, tp), 0)
    un = lax.broadcasted_iota(jnp.int32, (tp, tp), 1)
    mn = (un <= tq2) & (un < t_valid)
    mask_c = jnp.concatenate([mc] * n_heads, axis=0)
    mask_n = jnp.concatenate([mn] * n_heads, axis=0)
    kvw = kvh * hd
    assert group == kvh
    slot = lax.broadcasted_iota(jnp.int32, (tp, kvw), 1) >> (hd.bit_length() - 1)
    sink = _sink_col(sinks_ref, 0, n_heads, tp)

    for bi in range(bt):
        row = q_ref[bi]
        ck = ck_ref[bi]
        cv = cv_ref[bi]
        kn = row[:, qd:qd + kvw]
        vn = row[:, qd + kvw:]
        pieces = []
        for j in range(kvh):
            qg = row[:, j * kvw:(j + 1) * kvw] * (hd ** -0.5)
            for p in range(group):
                pieces.append(jnp.where(slot == j, pltpu.roll(qg, ((j - p) % kvh) * hd, 1), 0.0))
        qs = jnp.concatenate(pieces, axis=0)
        sc = jnp.where(mask_c, _dot(qs, ck), -jnp.inf)
        sn = jnp.where(mask_n, _dot_nt(qs, kn), -jnp.inf)
        m = jnp.maximum(jnp.maximum(jnp.max(sc, axis=-1, keepdims=True),
                                    jnp.max(sn, axis=-1, keepdims=True)), sink)
        pc = jnp.exp(sc - m)
        pn = jnp.exp(sn - m)
        l = jnp.sum(pc, axis=-1, keepdims=True) + jnp.sum(pn, axis=-1, keepdims=True) + jnp.exp(sink - m)
        of = (_dot_nt(pc, cv) + _dot(pn, vn)) / l
        for j in range(kvh):
            og = jnp.zeros((tp, kvw), f32)
            for p in range(group):
                h = j * group + p
                og = og + jnp.where(slot == p, pltpu.roll(of[h * tp:(h + 1) * tp, :], ((p - j) % kvh) * hd, 1), 0.0)
            o_ref[bi, :, j * kvw:(j + 1) * kvw] = og


def attn_cache_call(qkv, cache_k, cache_v, sinks, n_heads, t_valid):
    bsz, tp, nq = qkv.shape
    w = WINDOW
    kd = cache_k.shape[1]
    qd = n_heads * ATT_HEAD_DIM
    bt = 8
    kern = functools.partial(_attn_cache_kernel, n_heads=n_heads, t_valid=t_valid, bt=bt)
    return pl.pallas_call(
        kern,
        grid=(bsz // bt,),
        in_specs=[pl.BlockSpec(memory_space=pltpu.SMEM),
                  pl.BlockSpec((bt, tp, nq), lambda b: (b, 0, 0)),
                  pl.BlockSpec((bt, kd, w), lambda b: (b, 0, 0)),
                  pl.BlockSpec((bt, kd, w), lambda b: (b, 0, 0))],
        out_specs=pl.BlockSpec((bt, tp, qd), lambda b: (b, 0, 0)),
        out_shape=jax.ShapeDtypeStruct((bsz, tp, qd), f32),
        compiler_params=_cparams(1),
        name="attn_cache",
    )(sinks, qkv, cache_k, cache_v)


def _hgrn_chunk(x, st, lb, nw, tril, diag_mask, valid, grp):
    nh = HG_HEADS
    c_rows = x.shape[0]
    dim = x.shape[1] // 4
    dk = dim // nh
    ng = c_rows // grp
    if valid is not None:
        x = jnp.where(valid, x, 0.0)
    q = _silu(x[:, 0:dim])
    fg = lb + (1.0 - lb) * jax.nn.sigmoid(x[:, dim:2 * dim])
    logf = jnp.log(fg)
    k = 1.0 - fg
    v = x[:, 2 * dim:3 * dim]
    gate = x[:, 3 * dim:4 * dim]
    if valid is not None:
        logf = jnp.where(valid, logf, 0.0)
        k = jnp.where(valid, k, 0.0)
    bcum = _cumsum_rows(logf, tril.astype(bf16))

    def rows_of(vals):
        return jnp.concatenate([jnp.broadcast_to(r, (grp, dk)) for r in vals], axis=0)

    ys, st_out = [], []
    for h in range(nh):
        sl = slice(h * dk, (h + 1) * dk)
        b = bcum[:, sl]
        qh, kh, vh = q[:, sl], k[:, sl], v[:, sl]
        zero = jnp.zeros((1, dk), f32)
        r = [zero] + [b[i * grp - 1:i * grp, :] for i in range(1, ng)] + [b[c_rows - 1:c_rows, :]]
        mid = [b[i * grp + grp // 2 - 1:i * grp + grp // 2, :] for i in range(ng)]
        b_last = r[ng]
        r_start = rows_of(r[:ng])
        r_end = rows_of(r[1:])
        r_mid = rows_of(mid)
        qg = qh * jnp.exp(b - r_start)
        kt = kh * jnp.exp(r_end - b)
        qm = qh * jnp.exp(b - r_mid)
        km = kh * jnp.exp(r_mid - b)
        a = jnp.where(diag_mask, _dot_nt(qm, km), 0.0)
        if ng > 1:
            lhs, rhs = [], []
            for j in range(ng - 1):
                lhs.append(jnp.concatenate(
                    [jnp.zeros((grp, dk), f32) if i <= j else qg[i * grp:(i + 1) * grp, :] * jnp.exp(r[i] - r[j + 1])
                     for i in range(ng)], axis=0))
                rhs.append(jnp.concatenate(
                    [kt[i * grp:(i + 1) * grp, :] if i == j else jnp.zeros((grp, dk), f32) for i in range(ng)], axis=0))
            a = a + _dot_nt(jnp.concatenate(lhs, axis=1), jnp.concatenate(rhs, axis=1))
        q_inter = qg * jnp.exp(r_start)
        o = _dot(a, vh) + _dot_nt(q_inter, st[h])
        k_dec = kt * jnp.exp(b_last - r_end)
        st_out.append(st[h] * jnp.exp(b_last) + _dot_tn(vh, k_dec))
        on = o * lax.rsqrt(jnp.mean(o * o, axis=-1, keepdims=True) + RMS_EPS)
        ys.append(on * nw[:, sl] * _silu(gate[:, sl]))
    return ys, st_out


def _hgrn_kernel(*refs, c_rows, grp, t_valid, t_pad, layer, has_s0, ub, uc):
    if has_s0:
        p_ref, lbl_ref, nw_ref, s0_ref, y_ref, so_ref, st_ref = refs
    else:
        p_ref, lbl_ref, nw_ref, y_ref, so_ref, st_ref = refs
    c = pl.program_id(1)
    nc = pl.num_programs(1)
    nh = HG_HEADS
    dim = p_ref.shape[1] // 4
    dk = dim // nh

    @pl.when(c == 0)
    def _():
        if has_s0:
            for s in range(ub):
                for h in range(nh):
                    st_ref[s, h] = s0_ref[s, h].T
        else:
            st_ref[...] = jnp.zeros_like(st_ref)

    lbl = lbl_ref[...]
    e = jnp.exp(lbl - jnp.max(lbl, axis=0, keepdims=True))
    sm = e / jnp.sum(e, axis=0, keepdims=True)
    lb = jnp.zeros((1, dim), f32)
    for r in range(1, layer + 1):
        lb = lb + sm[r:r + 1, :]

    tril = _tril_mask(c_rows)
    rr = lax.broadcasted_iota(jnp.int32, (c_rows, c_rows), 0)
    ss = lax.broadcasted_iota(jnp.int32, (c_rows, c_rows), 1)
    gshift = grp.bit_length() - 1
    diag_mask = tril & ((rr >> gshift) == (ss >> gshift))
    nw = nw_ref[...]
    rows = lax.broadcasted_iota(jnp.int32, (c_rows, 1), 0)

    for s in range(ub):
        st = [st_ref[s, h] for h in range(nh)]
        for cc in range(uc):
            r0 = (s * uc + cc) * c_rows
            valid = ((c * uc + cc) * c_rows + rows) < t_valid if t_valid < t_pad else None
            ys, st = _hgrn_chunk(p_ref[r0:r0 + c_rows, :], st, lb, nw, tril, diag_mask, valid, grp)
            for h in range(nh):
                y_ref[r0:r0 + c_rows, h * dk:(h + 1) * dk] = ys[h].astype(y_ref.dtype)
        for h in range(nh):
            st_ref[s, h] = st[h]

    @pl.when(c == nc - 1)
    def _():
        for s in range(ub):
            for h in range(nh):
                so_ref[s, h] = st_ref[s, h].T


def _units_per_step(bsz, nc):
    if nc == 1:
        return (4 if bsz % 4 == 0 else 1), 1
    return 1, (2 if nc % 2 == 0 else 1)


def hgrn_call(p, lb_logits, norm_w, s0, bsz, t_pad, t_valid, layer):
    dim = p.shape[1] // 4
    nh = HG_HEADS
    dk = dim // nh
    c_rows = min(128, t_pad)
    grp = min(HG_CHUNK_GROUP, c_rows)
    nc = t_pad // c_rows
    ub, uc = _units_per_step(bsz, nc)
    rows = ub * uc * c_rows
    ncs = nc // uc
    has_s0 = s0 is not None
    in_specs = [pl.BlockSpec((rows, 4 * dim), lambda b, c: (b * ncs + c, 0)),
                pl.BlockSpec(lb_logits.shape, lambda b, c: (0, 0)),
                pl.BlockSpec((1, dim), lambda b, c: (0, 0))]
    args = [p, lb_logits, norm_w.reshape(1, dim)]
    if has_s0:
        in_specs.append(pl.BlockSpec((ub, nh, dk, dk), lambda b, c: (b, 0, 0, 0)))
        args.append(s0)
    kern = functools.partial(_hgrn_kernel, c_rows=c_rows, grp=grp, t_valid=t_valid, t_pad=t_pad,
                             layer=layer, has_s0=has_s0, ub=ub, uc=uc)
    return pl.pallas_call(
        kern,
        grid=(bsz // ub, ncs),
        in_specs=in_specs,
        out_specs=[pl.BlockSpec((rows, dim), lambda b, c: (b * ncs + c, 0)),
                   pl.BlockSpec((ub, nh, dk, dk), lambda b, c: (b, 0, 0, 0))],
        out_shape=[jax.ShapeDtypeStruct((bsz * t_pad, dim), bf16),
                   jax.ShapeDtypeStruct((bsz, nh, dk, dk), f32)],
        scratch_shapes=[pltpu.VMEM((ub, nh, dk, dk), f32)],
        compiler_params=_cparams(2),
        name="hgrn",
    )(*args)


def _expand_heads(w, ex_ref):
    hi = w.astype(bf16)
    r1 = w - hi.astype(f32)
    mid = r1.astype(bf16)
    lo = (r1 - mid.astype(f32)).astype(bf16)
    return jnp.dot(jnp.concatenate([hi, mid, lo], axis=1), ex_ref[...], preferred_element_type=f32)


def _ssd_kernel(*refs, c_rows, t_valid, t_pad, has_s0, ub, uc, conv_done):
    if has_s0:
        (zx_ref, cw_ref, cb_ref, dtb_ref, alog_ref, dsk_ref, nw_ref, ex_ref, s0_ref, buf_ref,
         y_ref, so_ref, st_ref, carry_ref) = refs
    else:
        (zx_ref, cw_ref, cb_ref, dtb_ref, alog_ref, dsk_ref, nw_ref, ex_ref,
         y_ref, so_ref, st_ref, carry_ref) = refs
    c = pl.program_id(1)
    nc = pl.num_programs(1)
    hp, ns, ngr = SSD_HEAD_DIM, SSD_STATE, SSD_GROUPS
    di = y_ref.shape[1]
    nh = di // hp
    hpg = nh // ngr
    gw = ngr * ns
    kconv = cw_ref.shape[0]
    pair = 2 * hp
    assert pair == LANES and hpg % 2 == 0

    @pl.when(c == 0)
    def _():
        carry_ref[...] = jnp.zeros_like(carry_ref)
        if has_s0:
            for s in range(ub):
                for r in range(di // LANES):
                    st_ref[s, :, r * LANES:(r + 1) * LANES] = s0_ref[s, r * LANES:(r + 1) * LANES, :].T
                carry_ref[s, SUBLANES - (kconv - 1):, :] = buf_ref[s]
        else:
            st_ref[...] = jnp.zeros_like(st_ref)

    cdim = cw_ref.shape[1]
    rows = lax.broadcasted_iota(jnp.int32, (c_rows, 1), 0)
    rows8 = rows[:SUBLANES]
    cw = cw_ref[...]
    a_neg = -jnp.exp(alog_ref[...])
    tril = _tril_mask(c_rows)
    dsk = dsk_ref[...]
    nw = nw_ref[...]
    lo = lax.broadcasted_iota(jnp.int32, (c_rows, LANES), 1) < hp
    for s in range(ub):
        for cc in range(uc):
            r0 = (s * uc + cc) * c_rows
            valid = ((c * uc + cc) * c_rows + rows) < t_valid if t_valid < t_pad else None
            _ssd_chunk(zx_ref[r0:r0 + c_rows, :], st_ref.at[s], carry_ref.at[s], y_ref.at[r0:r0 + c_rows, :],
                       cw, cb_ref[...], dtb_ref[...], a_neg, dsk, nw, ex_ref, tril, lo, rows8, valid, conv_done)

    @pl.when(c == nc - 1)
    def _():
        for s in range(ub):
            for r in range(di // LANES):
                so_ref[s, r * LANES:(r + 1) * LANES, :] = st_ref[s, :, r * LANES:(r + 1) * LANES].T


def _ssd_chunk(zx, st_ref, carry_ref, y_ref, cw, cb, dtb, a_neg, dsk, nw, ex_ref, tril, lo, rows8, valid, conv_done):
    hp, ns, ngr = SSD_HEAD_DIM, SSD_STATE, SSD_GROUPS
    c_rows = zx.shape[0]
    di = y_ref.shape[1]
    nh = di // hp
    hpg = nh // ngr
    gw = ngr * ns
    kconv, cdim = cw.shape
    pair = 2 * hp
    z = zx[:, :di]
    u = zx[:, di:di + cdim]
    if conv_done:
        xbc = u
    else:
        xbc = _silu(_causal_conv(u, carry_ref[...], cw, cb, rows8))
        carry_ref[...] = u[c_rows - SUBLANES:, :]
    dt = jax.nn.softplus(zx[:, di + cdim:] + dtb)
    if valid is not None:
        xbc = jnp.where(valid, xbc, 0.0)
        dt = jnp.where(valid, dt, 0.0)
    xs = xbc[:, :di]
    bm = xbc[:, di:di + gw]
    cm = xbc[:, di + gw:di + 2 * gw]
    acum = _cumsum_rows(dt * a_neg, tril.astype(bf16))

    def head_rows(a):
        if c_rows < LANES:
            a = jnp.concatenate([a, jnp.zeros((LANES - c_rows, LANES), f32)], axis=0)
        return a.T[:, :c_rows]

    acum_t = head_rows(acum)
    dt_t = head_rows(dt)
    a_last = acum[c_rows - 1:c_rows, :]
    xd = xs * _expand_heads(dt * jnp.exp(a_last - acum), ex_ref)

    ys = []
    for g in range(ngr):
        bg = bm[:, g * ns:(g + 1) * ns]
        cg = cm[:, g * ns:(g + 1) * ns]
        cb_mat = _dot_nt(cg, bg)
        gsl = slice(g * hpg * hp, (g + 1) * hpg * hp)
        y_inter = _dot(cg, st_ref[:, gsl])
        decs = []
        for q in range(hpg // 2):
            h1 = g * hpg + 2 * q
            psl = slice(h1 * hp, h1 * hp + pair)
            ms, es = [], []
            for h in (h1, h1 + 1):
                colb = jnp.broadcast_to(acum[:, h:h + 1], (c_rows, LANES))
                seg = colb[:, :c_rows] - acum_t[h:h + 1, :]
                ms.append(cb_mat * jnp.exp(jnp.where(tril, seg, -jnp.inf)) * dt_t[h:h + 1, :])
                es.append(jnp.exp(colb))
            xp = xs[:, psl]
            x_lo = jnp.where(lo, xp, 0.0)
            x_hi = jnp.where(lo, 0.0, xp)
            if (2 * c_rows) % LANES == 0:
                y_intra = _dot(jnp.concatenate(ms, axis=1), jnp.concatenate([x_lo, x_hi], axis=0))
            else:
                y_intra = _dot(ms[0], x_lo) + _dot(ms[1], x_hi)
            y_pair = y_intra + y_inter[:, q * pair:(q + 1) * pair] * jnp.where(lo, es[0], es[1])
            ys.append(y_pair + dsk[:, psl] * xp)
            decs.append(jnp.where(lo[0:1, :], jnp.exp(a_last[:, h1:h1 + 1]), jnp.exp(a_last[:, h1 + 1:h1 + 2])))
        st_ref[:, gsl] = st_ref[:, gsl] * jnp.concatenate(decs, axis=1) + _dot_tn(bg, xd[:, gsl])

    y = jnp.concatenate(ys, axis=1) * _silu(z)
    gdim = di // ngr
    for g in range(ngr):
        sl = slice(g * gdim, (g + 1) * gdim)
        yg = y[:, sl]
        y_ref[:, sl] = (yg * lax.rsqrt(jnp.mean(yg * yg, axis=-1, keepdims=True) + RMS_EPS) * nw[:, sl]).astype(y_ref.dtype)


def ssd_call(zx, conv_w, conv_b, dt_bias, a_log, d_skip, norm_w, s0, conv_buf, bsz, t_pad, t_valid, conv_done):
    hp, ns = SSD_HEAD_DIM, SSD_STATE
    conv_dim = conv_w.shape[1]
    di = zx.shape[1] - conv_dim - LANES
    nh = di // hp
    c_rows = min(128, t_pad)
    nc = t_pad // c_rows
    ub, uc = _units_per_step(bsz, nc)
    rows = ub * uc * c_rows
    ncs = nc // uc
    has_s0 = s0 is not None
    assert not (conv_done and has_s0)
    pad_h = lambda a: jnp.pad(a.reshape(1, nh), ((0, 0), (0, LANES - nh)))
    full = lambda shape: pl.BlockSpec(shape, lambda b, c: (0,) * len(shape))
    sel = (jnp.arange(di)[None, :] // hp == jnp.arange(LANES)[:, None]).astype(bf16)
    expand = jnp.concatenate([sel, sel, sel], axis=0)
    in_specs = [pl.BlockSpec((rows, zx.shape[1]), lambda b, c: (b * ncs + c, 0)),
                full(conv_w.shape), full((1, conv_dim)), full((1, LANES)), full((1, LANES)),
                full((1, di)), full((1, di)), full(expand.shape)]
    args = [zx, conv_w, conv_b.reshape(1, conv_dim), pad_h(dt_bias), pad_h(a_log),
            jnp.repeat(d_skip, hp).reshape(1, di), norm_w.reshape(1, di), expand]
    if has_s0:
        in_specs += [pl.BlockSpec((ub, di, ns), lambda b, c: (b, 0, 0)),
                     pl.BlockSpec((ub,) + conv_buf.shape[1:], lambda b, c: (b, 0, 0))]
        args += [s0.reshape(bsz, di, ns), conv_buf]
    kern = functools.partial(_ssd_kernel, c_rows=c_rows, t_valid=t_valid, t_pad=t_pad, has_s0=has_s0, ub=ub, uc=uc,
                             conv_done=conv_done)
    y, s_new = pl.pallas_call(
        kern,
        grid=(bsz // ub, ncs),
        in_specs=in_specs,
        out_specs=[pl.BlockSpec((rows, di), lambda b, c: (b * ncs + c, 0)),
                   pl.BlockSpec((ub, di, ns), lambda b, c: (b, 0, 0))],
        out_shape=[jax.ShapeDtypeStruct((bsz * t_pad, di), bf16),
                   jax.ShapeDtypeStruct((bsz, di, ns), f32)],
        scratch_shapes=[pltpu.VMEM((ub, ns, di), f32), pltpu.VMEM((ub, SUBLANES, conv_dim), f32)],
        compiler_params=_cparams(2),
        name="ssd",
    )(*args)
    return y, s_new.reshape(bsz, nh, hp, ns)


def _trunk(x, c_k, c_v, s_hg, s_ssm, s_sconv, s_fconv, prm):
    bsz, t, d = x.shape
    depth = prm["norm_mix_w"].shape[0]
    decode = c_k is not None
    tp = -(-t // SUBLANES) * SUBLANES
    if tp != t:
        x = jnp.pad(x, ((0, 0), (0, tp - t), (0, 0)))
    m = bsz * tp
    n_heads = d // ATT_HEAD_DIM
    qd = n_heads * ATT_HEAD_DIM
    kvd = ATT_KV_HEADS * ATT_HEAD_DIM
    x2 = x.reshape(m, d)
    xn = rmsnorm_call(x2, prm["norm_mix_w"][0], bf16)
    nk, nv, nhg, nssm, nsconv, nfconv = [], [], [], [], [], []
    slot = [0] * N_MIXERS
    for i in range(depth):
        kind = i % N_MIXERS
        j = slot[kind]
        slot[kind] += 1
        nw_ffn = prm["norm_ffn_w"][i]
        if kind == 0:
            w_qkv, b_qkv = prm["attn_w_qkv"], prm["attn_b_qkv"][j]
            nq = w_qkv.shape[2]
            if decode:
                qkv = proj_call(xn, w_qkv, j, b_qkv, f32, "attn_qkv").reshape(bsz, tp, nq)
                ck = jnp.transpose(c_k[j], (0, 2, 3, 1)).reshape(bsz, kvd, WINDOW)
                cv = jnp.transpose(c_v[j], (0, 2, 3, 1)).reshape(bsz, kvd, WINDOW)
                o = attn_cache_call(qkv, ck, cv, prm["attn_sinks"][j], n_heads, t).reshape(m, qd)
                to4 = lambda a: a.reshape(bsz, t, ATT_KV_HEADS, ATT_HEAD_DIM)
                k_new = to4(qkv[:, :t, qd:qd + kvd])
                v_new = to4(qkv[:, :t, qd + kvd:])
            else:
                qkv = proj_call(xn, w_qkv, j, b_qkv, bf16, "attn_qkv")
                o = attn_prompt_call(qkv, prm["attn_sinks"][j], bsz, t, n_heads)
                xn_last = xn.reshape(bsz, t, d)[:, t - WINDOW:].reshape(bsz * WINDOW, d)
                kv_last = proj_call(xn_last, w_qkv[:, :, qd:], j, b_qkv[qd:], f32, "attn_kv_tail").reshape(bsz, WINDOW, 2, kvd)
                to4 = lambda a: a.reshape(bsz, WINDOW, ATT_KV_HEADS, ATT_HEAD_DIM)
                k_new, v_new = to4(kv_last[:, :, 0]), to4(kv_last[:, :, 1])
            nk.append(k_new)
            nv.append(v_new)
            act, w_o, b_o = o, (prm["attn_w_o"], j), prm["attn_b_o"][j]
        elif kind == 1:
            p = proj_call(xn, prm["hgrn_w_in"], j, None, f32, "hgrn_in")
            y, s_new = hgrn_call(p, prm["hgrn_lb_logits"], prm["hgrn_norm_w"][j],
                                 None if s_hg is None else s_hg[j], bsz, tp, t, i)
            nhg.append(s_new)
            act, w_o, b_o = y, (prm["hgrn_w_o"], j), None
        else:
            conv_dim = prm["ssd_conv_w"].shape[2]
            di = prm["ssd_w_o"].shape[1]
            kc = prm["ssd_conv_w"].shape[1]
            if s_sconv is None:
                zx, pre = ssd_in_call(xn, prm["ssd_w_in"], j, prm["ssd_conv_w"][j], prm["ssd_conv_b"][j], tp)
                pre = pre.reshape(bsz, tp // min(PROJ_TM, m), SUBLANES, conv_dim)[:, -1, SUBLANES - (kc - 1):]
            else:
                zx = proj_call(xn, prm["ssd_w_in"], j, None, f32, "ssd_in")
                pre = zx.reshape(bsz, tp, zx.shape[1])[:, max(t - (kc - 1), 0):t, di:di + conv_dim]
                pre = jnp.concatenate([s_sconv[j], pre], axis=1)
                pre = pre[:, pre.shape[1] - (kc - 1):]
            nsconv.append(pre)
            y, s_new = ssd_call(zx, prm["ssd_conv_w"][j], prm["ssd_conv_b"][j], prm["ssd_dt_bias"][j],
                                prm["ssd_a_log"][j], prm["ssd_d"][j], prm["ssd_norm_w"][j],
                                None if s_ssm is None else s_ssm[j],
                                None if s_sconv is None else s_sconv[j], bsz, tp, t, s_sconv is None)
            nssm.append(s_new)
            act, w_o, b_o = y, (prm["ssd_w_o"], j), None
        last = i == depth - 1
        next_w = prm["norm_final_w"] if last else prm["norm_mix_w"][i + 1]
        dff = prm["ffn_w_down"].shape[1]
        kf = prm["ffn_conv_w"].shape[1]
        if s_fconv is not None:
            buf = s_fconv[i]
            e1 = jnp.pad(buf[:, 1:2], ((0, 0), (0, tp - 1), (0, 0))).reshape(m, dff)
            e2 = jnp.pad(buf, ((0, 0), (0, tp - 2), (0, 0))).reshape(m, dff)
        else:
            e1 = e2 = None
        x2, xn, tail = ffn_call(act, w_o, b_o, x2, nw_ffn, (prm["ffn_w_up"], i), prm["ffn_conv_w"][i],
                                prm["ffn_conv_b"][i], (prm["ffn_w_down"], i), next_w, tp, e1, e2,
                                f32 if last else bf16, "ffn")
        if s_fconv is not None:
            a_full = tail.reshape(bsz, tp, dff)[:, :t]
            nfconv.append(jnp.concatenate([s_fconv[i], a_full], axis=1)[:, t:])
        else:
            tiles_per_seq = tp // tail_tile_rows(m)
            tl = tail.reshape(bsz, tiles_per_seq, SUBLANES, dff)[:, -1]
            nfconv.append(tl[:, SUBLANES - (kf - 1):])
    stack = lambda xs: xs[0][None] if len(xs) == 1 else jnp.stack(xs)
    y = xn.reshape(bsz, tp, d)[:, :t]
    nk, nv = stack(nk), stack(nv)
    if decode:
        nk = jnp.concatenate([c_k[:, :, t:], nk], axis=2)
        nv = jnp.concatenate([c_v[:, :, t:], nv], axis=2)
    return (y, nk, nv, stack(nhg), stack(nssm), stack(nsconv), stack(nfconv))


def tail_tile_rows(m):
    return min(FFN_TM, m)


def kernel(x_prompt, x_sample, cache_attn_k, cache_attn_v, state_hgrn, state_ssm, state_ssm_conv, state_ffn_conv, norm_mix_w, norm_ffn_w, norm_final_w, attn_w_qkv, attn_b_qkv, attn_sinks, attn_w_o, attn_b_o, hgrn_w_in, hgrn_lb_logits, hgrn_norm_w, hgrn_w_o, ssd_w_in, ssd_conv_w, ssd_conv_b, ssd_dt_bias, ssd_a_log, ssd_d, ssd_norm_w, ssd_w_o, ffn_w_up, ffn_conv_w, ffn_conv_b, ffn_w_down):
    cast = lambda w: w.astype(bf16)
    ssd_in_pad = (-ssd_w_in.shape[2]) % LANES
    prm = dict(norm_mix_w=norm_mix_w, norm_ffn_w=norm_ffn_w, norm_final_w=norm_final_w,
               attn_w_qkv=cast(attn_w_qkv), attn_b_qkv=attn_b_qkv, attn_sinks=attn_sinks,
               attn_w_o=cast(attn_w_o), attn_b_o=attn_b_o,
               hgrn_w_in=cast(hgrn_w_in), hgrn_lb_logits=hgrn_lb_logits, hgrn_norm_w=hgrn_norm_w,
               hgrn_w_o=cast(hgrn_w_o),
               ssd_w_in=jnp.pad(cast(ssd_w_in), ((0, 0), (0, 0), (0, ssd_in_pad))), ssd_conv_w=ssd_conv_w,
               ssd_conv_b=ssd_conv_b, ssd_dt_bias=ssd_dt_bias,
               ssd_a_log=ssd_a_log, ssd_d=ssd_d, ssd_norm_w=ssd_norm_w, ssd_w_o=cast(ssd_w_o),
               ffn_w_up=cast(ffn_w_up), ffn_conv_w=ffn_conv_w, ffn_conv_b=ffn_conv_b, ffn_w_down=cast(ffn_w_down))
    outs_p = _trunk(x_prompt, None, None, None, None, None, None, prm)
    outs_s = _trunk(x_sample, cache_attn_k, cache_attn_v, state_hgrn, state_ssm, state_ssm_conv,
                    state_ffn_conv, prm)
    return (outs_p[0], outs_s[0]) + outs_p[1:] + outs_s[1:]
```

```python
import functools

import jax
import jax.numpy as jnp
from jax import lax
from jax.experimental import pallas as pl
from jax.experimental.pallas import tpu as pltpu

bf16 = jnp.bfloat16
f32 = jnp.float32

RMS_EPS = 1e-6
N_MIXERS = 3
WINDOW = 128
PAST_LEN = 8192
ATT_HEAD_DIM = 64
ATT_KV_HEADS = 4
HG_HEADS = 8
HG_CHUNK_GROUP = 32
SSD_HEAD_DIM = 64
SSD_STATE = 128
SSD_GROUPS = 4

LANES = 128
SUBLANES = 8
VMEM_LIMIT = 56 * 1024 * 1024
PROJ_TM = 512
FFN_TM = 512


def _cparams(n_axes):
    return pltpu.CompilerParams(dimension_semantics=("arbitrary",) * n_axes,
                                vmem_limit_bytes=VMEM_LIMIT)


def _dot(a, b):
    return jnp.dot(a.astype(bf16), b.astype(bf16), preferred_element_type=f32)


def _dot_nt(a, b):
    return lax.dot_general(a.astype(bf16), b.astype(bf16), (((1,), (1,)), ((), ())),
                           preferred_element_type=f32)


def _dot_tn(a, b):
    return lax.dot_general(a.astype(bf16), b.astype(bf16), (((0,), (0,)), ((), ())),
                           preferred_element_type=f32)


def _silu(x):
    return x * jax.nn.sigmoid(x)


def _rms(x, w):
    ms = jnp.mean(x * x, axis=-1, keepdims=True)
    return x * lax.rsqrt(ms + RMS_EPS) * w


def _cumsum_rows(x, tril):
    hi = x.astype(bf16)
    r1 = x - hi.astype(f32)
    mid = r1.astype(bf16)
    lo = (r1 - mid.astype(f32)).astype(bf16)
    d = functools.partial(jnp.dot, preferred_element_type=f32)
    return d(tril, hi) + d(tril, mid) + d(tril, lo)


def _tril_mask(c):
    r = lax.broadcasted_iota(jnp.int32, (c, c), 0)
    s = lax.broadcasted_iota(jnp.int32, (c, c), 1)
    return r >= s


def _resident(shape):
    return pl.BlockSpec(shape, lambda *_: (0,) * len(shape), pipeline_mode=pl.Buffered(1))


def _resident_layer(w, layer):
    return pl.BlockSpec((None,) + w.shape[1:], lambda *_: (layer, 0, 0), pipeline_mode=pl.Buffered(1))


def _proj_kernel(x_ref, nw_ref, w_ref, b_ref, o_ref):
    x = x_ref[...]
    if x.dtype != bf16:
        x = _rms(x, nw_ref[...]).astype(bf16)
    acc = jnp.dot(x, w_ref[...], preferred_element_type=f32)
    o_ref[...] = (acc + b_ref[...]).astype(o_ref.dtype)


def proj_call(x, norm_w, w, layer, bias, out_dtype, name):
    m, k = x.shape
    n = w.shape[2]
    tm = min(PROJ_TM, m)
    if bias is None:
        bias = jnp.zeros((n,), f32)
    if norm_w is None:
        assert x.dtype == bf16
        norm_w = jnp.ones((k,), f32)
    return pl.pallas_call(
        _proj_kernel,
        grid=(m // tm,),
        in_specs=[pl.BlockSpec((tm, k), lambda i: (i, 0)), _resident((1, k)), _resident_layer(w, layer),
                  _resident((1, n))],
        out_specs=pl.BlockSpec((tm, n), lambda i: (i, 0)),
        out_shape=jax.ShapeDtypeStruct((m, n), out_dtype),
        compiler_params=_cparams(1),
        name=name,
    )(x, norm_w.reshape(1, k), w, bias.reshape(1, n))


def _causal_conv(u, prev, cw, cb, rows8):
    kconv = cw.shape[0]
    yc = cb
    for kk in range(kconv - 1):
        back = kconv - 1 - kk
        sh = pltpu.roll(u, back, 0)
        top = sh[:SUBLANES]
        for r0 in range(back):
            top = jnp.where(rows8 == r0, prev[SUBLANES - back + r0:SUBLANES - back + r0 + 1, :], top)
        sh = jnp.concatenate([top, sh[SUBLANES:]], axis=0) if u.shape[0] > SUBLANES else top
        yc = yc + sh * cw[kk:kk + 1, :]
    return yc + u * cw[kconv - 1:kconv, :]


def _ssd_in_kernel(x_ref, w_ref, cw_ref, cb_ref, o_ref, tail_ref, carry_ref, *, tm, tc, t_seq, di):
    i = pl.program_id(0)
    cdim = cw_ref.shape[1]

    @pl.when(((i * tm) & (t_seq - 1)) == 0)
    def _():
        carry_ref[...] = jnp.zeros_like(carry_ref)

    x = x_ref[...]
    rows8 = lax.broadcasted_iota(jnp.int32, (SUBLANES, 1), 0)
    nz = di // tc
    for idx, c0 in enumerate(range(0, cdim, tc)):
        cs = slice(c0, c0 + tc)
        u = jnp.dot(x, w_ref[:, di + c0:di + c0 + tc], preferred_element_type=f32)
        tail_ref[0, :, cs] = u[tm - SUBLANES:, :]
        yc = _causal_conv(u, carry_ref[:, cs], cw_ref[:, cs], cb_ref[:, cs], rows8)
        carry_ref[:, cs] = u[tm - SUBLANES:, :]
        o_ref[:, di + c0:di + c0 + tc] = _silu(yc)
        if idx < nz:
            o_ref[:, idx * tc:(idx + 1) * tc] = jnp.dot(x, w_ref[:, idx * tc:(idx + 1) * tc],
                                                        preferred_element_type=f32)
    o_ref[:, di + cdim:] = jnp.dot(x, w_ref[:, di + cdim:], preferred_element_type=f32)


def ssd_in_call(xn, w, layer, conv_w, conv_b, t_seq):
    m, k = xn.shape
    n = w.shape[2]
    cdim = conv_w.shape[1]
    di = n - cdim - LANES
    tm = min(PROJ_TM, m)
    tc = 2 * LANES
    assert t_seq % tm == 0 and cdim % tc == 0 and di % tc == 0 and di <= cdim and (t_seq & (t_seq - 1)) == 0
    kern = functools.partial(_ssd_in_kernel, tm=tm, tc=tc, t_seq=t_seq, di=di)
    return pl.pallas_call(
        kern,
        grid=(m // tm,),
        in_specs=[pl.BlockSpec((tm, k), lambda i: (i, 0)), _resident_layer(w, layer),
                  _resident(conv_w.shape), _resident((1, cdim))],
        out_specs=[pl.BlockSpec((tm, n), lambda i: (i, 0)), pl.BlockSpec((1, SUBLANES, cdim), lambda i: (i, 0, 0))],
        out_shape=[jax.ShapeDtypeStruct((m, n), f32), jax.ShapeDtypeStruct((m // tm, SUBLANES, cdim), f32)],
        scratch_shapes=[pltpu.VMEM((SUBLANES, cdim), f32)],
        compiler_params=_cparams(1),
        name="ssd_in_conv",
    )(xn, w, conv_w, conv_b.reshape(1, cdim))


def _ffn_kernel(*refs, tm, tf, t_seq, tail, has_state):
    if has_state:
        (act_ref, wo_ref, bo_ref, x_ref, nw1_ref, wu_ref, cw_ref, cb_ref, wd_ref, nw2_ref, e1_ref, e2_ref,
         xo_ref, xn2_ref, tail_ref, carry_ref) = refs
    else:
        (act_ref, wo_ref, bo_ref, x_ref, nw1_ref, wu_ref, cw_ref, cb_ref, wd_ref, nw2_ref,
         xo_ref, xn2_ref, tail_ref, carry_ref) = refs
    i = pl.program_id(0)
    dff = wd_ref.shape[0]

    x1 = jnp.dot(act_ref[...].astype(bf16), wo_ref[...], preferred_element_type=f32) + bo_ref[...] + x_ref[...]
    xn = _rms(x1, nw1_ref[...]).astype(bf16)
    if has_state:
        tpos = (lax.broadcasted_iota(jnp.int32, (tm, 1), 0)) & (t_seq - 1)
    else:
        @pl.when(((i * tm) & (t_seq - 1)) == 0)
        def _():
            carry_ref[...] = jnp.zeros_like(carry_ref)
        rows8 = lax.broadcasted_iota(jnp.int32, (SUBLANES, 1), 0)
    gs = []
    for c0 in range(0, dff, tf):
        cs = slice(c0, c0 + tf)
        a = jnp.dot(xn, wu_ref[:, cs], preferred_element_type=f32)
        b = jnp.dot(xn, wu_ref[:, dff + c0:dff + c0 + tf], preferred_element_type=f32)
        tail_ref[0, :, cs] = a[tm - tail:, :]
        r1 = pltpu.roll(a, 1, 0)
        r2 = pltpu.roll(a, 2, 0)
        if has_state:
            p1 = jnp.where(tpos == 0, e1_ref[:, cs], r1)
            p2 = jnp.where(tpos < 2, e2_ref[:, cs], r2)
        else:
            carry = carry_ref[:, cs]
            top1 = jnp.where(rows8 == 0, carry[7:8, :], r1[:SUBLANES])
            top2 = jnp.where(rows8 == 0, carry[6:7, :], jnp.where(rows8 == 1, carry[7:8, :], r2[:SUBLANES]))
            p1 = jnp.concatenate([top1, r1[SUBLANES:]], axis=0)
            p2 = jnp.concatenate([top2, r2[SUBLANES:]], axis=0)
            carry_ref[:, cs] = a[tm - SUBLANES:, :]
        y = cb_ref[:, cs] + p2 * cw_ref[0:1, cs]
        y = y + p1 * cw_ref[1:2, cs]
        y = y + a * cw_ref[2:3, cs]
        gs.append((_silu(y) * b).astype(bf16))
    g = jnp.concatenate(gs, axis=1)
    xo = jnp.dot(g, wd_ref[...], preferred_element_type=f32) + x1
    xo_ref[...] = xo
    xn2_ref[...] = _rms(xo, nw2_ref[...]).astype(xn2_ref.dtype)


def ffn_call(act, w_o, b_o, x, ffn_norm_w, w_up, conv_w, conv_b, w_down, next_w, t_seq, e1, e2, out_dtype, name):
    m, d = x.shape
    ko = act.shape[1]
    dff = w_down[0].shape[1]
    tf = 2 * LANES
    assert dff % tf == 0 and (t_seq & (t_seq - 1)) == 0
    has_state = e1 is not None
    tm = min(FFN_TM // 2 if has_state else FFN_TM, m)
    assert (tm % t_seq == 0) if has_state else (t_seq % tm == 0)
    tail = tm if has_state else SUBLANES
    if b_o is None:
        b_o = jnp.zeros((d,), f32)
    row = lambda w: pl.BlockSpec((tm, w), lambda i: (i, 0))
    in_specs = [row(ko), _resident_layer(*w_o), _resident((1, d)), row(d), _resident((1, d)),
                _resident_layer(*w_up), _resident((3, dff)), _resident((1, dff)),
                _resident_layer(*w_down), _resident((1, d))]
    args = [act, w_o[0], b_o.reshape(1, d), x, ffn_norm_w.reshape(1, d),
            w_up[0], conv_w, conv_b.reshape(1, dff), w_down[0], next_w.reshape(1, d)]
    if has_state:
        in_specs += [row(dff), row(dff)]
        args += [e1, e2]
    kern = functools.partial(_ffn_kernel, tm=tm, tf=tf, t_seq=t_seq, tail=tail, has_state=has_state)
    return pl.pallas_call(
        kern,
        grid=(m // tm,),
        in_specs=in_specs,
        out_specs=[row(d), row(d), pl.BlockSpec((1, tail, dff), lambda i: (i, 0, 0))],
        out_shape=[jax.ShapeDtypeStruct((m, d), f32),
                   jax.ShapeDtypeStruct((m, d), out_dtype),
                   jax.ShapeDtypeStruct((m // tm, tail, dff), f32)],
        scratch_shapes=[pltpu.VMEM((SUBLANES, dff), f32)],
        compiler_params=_cparams(1),
        name=name,
    )(*args)


def _attn_softmax_pv(s, sink_col, vj):
    m = jnp.maximum(jnp.max(s, axis=-1, keepdims=True), sink_col)
    p = jnp.exp(s - m)
    l = jnp.sum(p, axis=-1, keepdims=True) + jnp.exp(sink_col - m)
    return _dot(p, vj) / l


def _sink_col(sinks_ref, j, group, rows_per_head):
    r = lax.broadcasted_iota(jnp.int32, (group * rows_per_head, 1), 0)
    col = jnp.full((group * rows_per_head, 1), sinks_ref[j * group], f32)
    for p in range(1, group):
        col = jnp.where(r >= p * rows_per_head, sinks_ref[j * group + p], col)
    return col


def _attn_prompt_kernel(sinks_ref, q_ref, kvc_ref, kvp_ref, o_ref, *, n_heads):
    n = pl.program_id(1)
    w, hd, kvh = WINDOW, ATT_HEAD_DIM, ATT_KV_HEADS
    group = n_heads // kvh
    kvw = kvh * hd
    assert group == kvh
    kv = jnp.concatenate([kvp_ref[...], kvc_ref[...]], axis=0)
    k2 = kv[:, :kvw]
    v2 = kv[:, kvw:]
    i = lax.broadcasted_iota(jnp.int32, (w, 2 * w), 0)
    c = lax.broadcasted_iota(jnp.int32, (w, 2 * w), 1)
    band = (c > i) & (c <= i + w) & ((n > 0) | (c >= w))
    mask = jnp.concatenate([band] * group, axis=0)
    slot = lax.broadcasted_iota(jnp.int32, (w, kvw), 1) >> (hd.bit_length() - 1)
    for j in range(kvh):
        qg = q_ref[:, j * kvw:(j + 1) * kvw] * (hd ** -0.5)
        qs = jnp.concatenate([jnp.where(slot == j, pltpu.roll(qg, ((j - p) % kvh) * hd, 1), 0)
                              for p in range(group)], axis=0)
        s = jnp.where(mask, _dot_nt(qs, k2), -jnp.inf)
        of = _attn_softmax_pv(s, _sink_col(sinks_ref, j, group, w), v2).astype(o_ref.dtype)
        og = of[0:w, :]
        for p in range(group):
            piece = of[p * w:(p + 1) * w, :]
            og = jnp.where(slot == p, pltpu.roll(piece, ((p - j) % kvh) * hd, 1), og)
        o_ref[:, j * kvw:(j + 1) * kvw] = og


def attn_prompt_call(qkv, sinks, bsz, t, n_heads):
    w, hd, kvh = WINDOW, ATT_HEAD_DIM, ATT_KV_HEADS
    nb = t // w
    qd = n_heads * hd
    kvd = 2 * kvh * hd
    assert qd % kvd == 0
    kvblk = qd // kvd
    kern = functools.partial(_attn_prompt_kernel, n_heads=n_heads)
    return pl.pallas_call(
        kern,
        grid=(bsz, nb),
        in_specs=[pl.BlockSpec(memory_space=pltpu.SMEM),
                  pl.BlockSpec((w, qd), lambda b, n: (b * nb + n, 0)),
                  pl.BlockSpec((w, kvd), lambda b, n: (b * nb + n, kvblk)),
                  pl.BlockSpec((w, kvd), lambda b, n: (b * nb + jnp.maximum(n - 1, 0), kvblk))],
        out_specs=pl.BlockSpec((w, qd), lambda b, n: (b * nb + n, 0)),
        out_shape=jax.ShapeDtypeStruct((bsz * t, qd), bf16),
        compiler_params=_cparams(2),
        name="attn_prompt",
    )(sinks, qkv, qkv, qkv)


def _attn_cache_kernel(sinks_ref, q_ref, ck_ref, cv_ref, o_ref, *, n_heads, t_valid, bt):
    w, hd, kvh = WINDOW, ATT_HEAD_DIM, ATT_KV_HEADS
    group = n_heads // kvh
    tp = q_ref.shape[1]
    qd = n_heads * hd
    tq = lax.broadcasted_iota(jnp.int32, (tp, w), 0)
    jc = lax.broadcasted_iota(jnp.int32, (tp, w), 1)
    mc = (jc > tq) & (jc - w + PAST_LEN >= 0)
    tq2 = lax.broadcasted_iota(jnp.int32, (tp, tp), 0)
    un = lax.broadcasted_iota(jnp.int32, (tp, tp), 1)
    mn = (un <= tq2) & (un < t_valid)
    mask_c = jnp.concatenate([mc] * n_heads, axis=0)
    mask_n = jnp.concatenate([mn] * n_heads, axis=0)
    kvw = kvh * hd
    assert group == kvh
    slot = lax.broadcasted_iota(jnp.int32, (tp, kvw), 1) >> (hd.bit_length() - 1)
    sink = _sink_col(sinks_ref, 0, n_heads, tp)

    for bi in range(bt):
        row = q_ref[bi]
        ck = ck_ref[bi]
        cv = cv_ref[bi]
        kn = row[:, qd:qd + kvw]
        vn = row[:, qd + kvw:]
        pieces = []
        for j in range(kvh):
            qg = row[:, j * kvw:(j + 1) * kvw] * (hd ** -0.5)
            for p in range(group):
                pieces.append(jnp.where(slot == j, pltpu.roll(qg, ((j - p) % kvh) * hd, 1), 0.0))
        qs = jnp.concatenate(pieces, axis=0)
        sc = jnp.where(mask_c, _dot(qs, ck), -jnp.inf)
        sn = jnp.where(mask_n, _dot_nt(qs, kn), -jnp.inf)
        m = jnp.maximum(jnp.maximum(jnp.max(sc, axis=-1, keepdims=True),
                                    jnp.max(sn, axis=-1, keepdims=True)), sink)
        pc = jnp.exp(sc - m)
        pn = jnp.exp(sn - m)
        l = jnp.sum(pc, axis=-1, keepdims=True) + jnp.sum(pn, axis=-1, keepdims=True) + jnp.exp(sink - m)
        of = (_dot_nt(pc, cv) + _dot(pn, vn)) / l
        for j in range(kvh):
            og = jnp.zeros((tp, kvw), f32)
            for p in range(group):
                h = j * group + p
                og = og + jnp.where(slot == p, pltpu.roll(of[h * tp:(h + 1) * tp, :], ((p - j) % kvh) * hd, 1), 0.0)
            o_ref[bi, :, j * kvw:(j + 1) * kvw] = og


def attn_cache_call(qkv, cache_k, cache_v, sinks, n_heads, t_valid):
    bsz, tp, nq = qkv.shape
    w = WINDOW
    kd = cache_k.shape[1]
    qd = n_heads * ATT_HEAD_DIM
    bt = 8
    kern = functools.partial(_attn_cache_kernel, n_heads=n_heads, t_valid=t_valid, bt=bt)
    return pl.pallas_call(
        kern,
        grid=(bsz // bt,),
        in_specs=[pl.BlockSpec(memory_space=pltpu.SMEM),
                  pl.BlockSpec((bt, tp, nq), lambda b: (b, 0, 0)),
                  pl.BlockSpec((bt, kd, w), lambda b: (b, 0, 0)),
                  pl.BlockSpec((bt, kd, w), lambda b: (b, 0, 0))],
        out_specs=pl.BlockSpec((bt, tp, qd), lambda b: (b, 0, 0)),
        out_shape=jax.ShapeDtypeStruct((bsz, tp, qd), f32),
        compiler_params=_cparams(1),
        name="attn_cache",
    )(sinks, qkv, cache_k, cache_v)


def _hgrn_chunk(x, st, lb, nw, tril, diag_mask, valid, grp):
    nh = HG_HEADS
    c_rows = x.shape[0]
    dim = x.shape[1] // 4
    dk = dim // nh
    ng = c_rows // grp
    if valid is not None:
        x = jnp.where(valid, x, 0.0)
    q = _silu(x[:, 0:dim])
    fg = lb + (1.0 - lb) * jax.nn.sigmoid(x[:, dim:2 * dim])
    logf = jnp.log(fg)
    k = 1.0 - fg
    v = x[:, 2 * dim:3 * dim]
    gate = x[:, 3 * dim:4 * dim]
    if valid is not None:
        logf = jnp.where(valid, logf, 0.0)
        k = jnp.where(valid, k, 0.0)
    bcum = _cumsum_rows(logf, tril.astype(bf16))

    def rows_of(vals):
        return jnp.concatenate([jnp.broadcast_to(r, (grp, dk)) for r in vals], axis=0)

    ys, st_out = [], []
    for h in range(nh):
        sl = slice(h * dk, (h + 1) * dk)
        b = bcum[:, sl]
        qh, kh, vh = q[:, sl], k[:, sl], v[:, sl]
        zero = jnp.zeros((1, dk), f32)
        r = [zero] + [b[i * grp - 1:i * grp, :] for i in range(1, ng)] + [b[c_rows - 1:c_rows, :]]
        mid = [b[i * grp + grp // 2 - 1:i * grp + grp // 2, :] for i in range(ng)]
        b_last = r[ng]
        r_start = rows_of(r[:ng])
        r_end = rows_of(r[1:])
        r_mid = rows_of(mid)
        qg = qh * jnp.exp(b - r_start)
        kt = kh * jnp.exp(r_end - b)
        qm = qh * jnp.exp(b - r_mid)
        km = kh * jnp.exp(r_mid - b)
        a = jnp.where(diag_mask, _dot_nt(qm, km), 0.0)
        if ng > 1:
            lhs, rhs = [], []
            for j in range(ng - 1):
                lhs.append(jnp.concatenate(
                    [jnp.zeros((grp, dk), f32) if i <= j else qg[i * grp:(i + 1) * grp, :] * jnp.exp(r[i] - r[j + 1])
                     for i in range(ng)], axis=0))
                rhs.append(jnp.concatenate(
                    [kt[i * grp:(i + 1) * grp, :] if i == j else jnp.zeros((grp, dk), f32) for i in range(ng)], axis=0))
            a = a + _dot_nt(jnp.concatenate(lhs, axis=1), jnp.concatenate(rhs, axis=1))
        q_inter = qg * jnp.exp(r_start)
        o = _dot(a, vh) + _dot_nt(q_inter, st[h])
        k_dec = kt * jnp.exp(b_last - r_end)
        st_out.append(st[h] * jnp.exp(b_last) + _dot_tn(vh, k_dec))
        on = o * lax.rsqrt(jnp.mean(o * o, axis=-1, keepdims=True) + RMS_EPS)
        ys.append(on * nw[:, sl] * _silu(gate[:, sl]))
    return ys, st_out


def _hgrn_kernel(*refs, c_rows, grp, t_valid, t_pad, layer, has_s0, ub, uc):
    if has_s0:
        p_ref, lbl_ref, nw_ref, s0_ref, y_ref, so_ref, st_ref = refs
    else:
        p_ref, lbl_ref, nw_ref, y_ref, so_ref, st_ref = refs
    c = pl.program_id(1)
    nc = pl.num_programs(1)
    nh = HG_HEADS
    dim = p_ref.shape[1] // 4
    dk = dim // nh

    @pl.when(c == 0)
    def _():
        if has_s0:
            for s in range(ub):
                for h in range(nh):
                    st_ref[s, h] = s0_ref[s, h].T
        else:
            st_ref[...] = jnp.zeros_like(st_ref)

    lbl = lbl_ref[...]
    e = jnp.exp(lbl - jnp.max(lbl, axis=0, keepdims=True))
    sm = e / jnp.sum(e, axis=0, keepdims=True)
    lb = jnp.zeros((1, dim), f32)
    for r in range(1, layer + 1):
        lb = lb + sm[r:r + 1, :]

    tril = _tril_mask(c_rows)
    rr = lax.broadcasted_iota(jnp.int32, (c_rows, c_rows), 0)
    ss = lax.broadcasted_iota(jnp.int32, (c_rows, c_rows), 1)
    gshift = grp.bit_length() - 1
    diag_mask = tril & ((rr >> gshift) == (ss >> gshift))
    nw = nw_ref[...]
    rows = lax.broadcasted_iota(jnp.int32, (c_rows, 1), 0)

    for s in range(ub):
        st = [st_ref[s, h] for h in range(nh)]
        for cc in range(uc):
            r0 = (s * uc + cc) * c_rows
            valid = ((c * uc + cc) * c_rows + rows) < t_valid if t_valid < t_pad else None
            ys, st = _hgrn_chunk(p_ref[r0:r0 + c_rows, :], st, lb, nw, tril, diag_mask, valid, grp)
            for h in range(nh):
                y_ref[r0:r0 + c_rows, h * dk:(h + 1) * dk] = ys[h].astype(y_ref.dtype)
        for h in range(nh):
            st_ref[s, h] = st[h]

    @pl.when(c == nc - 1)
    def _():
        for s in range(ub):
            for h in range(nh):
                so_ref[s, h] = st_ref[s, h].T


def _units_per_step(bsz, nc, max_seqs, max_chunks):
    if nc == 1:
        return max(u for u in (1, 2, 4, 8) if u <= max_seqs and bsz % u == 0), 1
    return 1, max(u for u in (1, 2, 4, 8) if u <= max_chunks and nc % u == 0)


def hgrn_call(p, lb_logits, norm_w, s0, bsz, t_pad, t_valid, layer):
    dim = p.shape[1] // 4
    nh = HG_HEADS
    dk = dim // nh
    c_rows = min(128, t_pad)
    grp = min(HG_CHUNK_GROUP, c_rows)
    nc = t_pad // c_rows
    ub, uc = _units_per_step(bsz, nc, 8, 4)
    rows = ub * uc * c_rows
    ncs = nc // uc
    has_s0 = s0 is not None
    in_specs = [pl.BlockSpec((rows, 4 * dim), lambda b, c: (b * ncs + c, 0)),
                pl.BlockSpec(lb_logits.shape, lambda b, c: (0, 0)),
                pl.BlockSpec((1, dim), lambda b, c: (0, 0))]
    args = [p, lb_logits, norm_w.reshape(1, dim)]
    if has_s0:
        in_specs.append(pl.BlockSpec((ub, nh, dk, dk), lambda b, c: (b, 0, 0, 0)))
        args.append(s0)
    kern = functools.partial(_hgrn_kernel, c_rows=c_rows, grp=grp, t_valid=t_valid, t_pad=t_pad,
                             layer=layer, has_s0=has_s0, ub=ub, uc=uc)
    return pl.pallas_call(
        kern,
        grid=(bsz // ub, ncs),
        in_specs=in_specs,
        out_specs=[pl.BlockSpec((rows, dim), lambda b, c: (b * ncs + c, 0)),
                   pl.BlockSpec((ub, nh, dk, dk), lambda b, c: (b, 0, 0, 0))],
        out_shape=[jax.ShapeDtypeStruct((bsz * t_pad, dim), bf16),
                   jax.ShapeDtypeStruct((bsz, nh, dk, dk), f32)],
        scratch_shapes=[pltpu.VMEM((ub, nh, dk, dk), f32)],
        compiler_params=_cparams(2),
        name="hgrn",
    )(*args)


def _expand_heads(w, ex_ref):
    hi = w.astype(bf16)
    r1 = w - hi.astype(f32)
    mid = r1.astype(bf16)
    lo = (r1 - mid.astype(f32)).astype(bf16)
    return jnp.dot(jnp.concatenate([hi, mid, lo], axis=1), ex_ref[...], preferred_element_type=f32)


def _ssd_kernel(*refs, c_rows, t_valid, t_pad, has_s0, ub, uc, conv_done):
    if has_s0:
        (zx_ref, cw_ref, cb_ref, dtb_ref, alog_ref, dsk_ref, nw_ref, ex_ref, s0_ref, buf_ref,
         y_ref, so_ref, st_ref, carry_ref) = refs
    else:
        (zx_ref, cw_ref, cb_ref, dtb_ref, alog_ref, dsk_ref, nw_ref, ex_ref,
         y_ref, so_ref, st_ref, carry_ref) = refs
    c = pl.program_id(1)
    nc = pl.num_programs(1)
    hp, ns, ngr = SSD_HEAD_DIM, SSD_STATE, SSD_GROUPS
    di = y_ref.shape[1]
    nh = di // hp
    hpg = nh // ngr
    gw = ngr * ns
    kconv = cw_ref.shape[0]
    pair = 2 * hp
    assert pair == LANES and hpg % 2 == 0

    @pl.when(c == 0)
    def _():
        carry_ref[...] = jnp.zeros_like(carry_ref)
        if has_s0:
            for s in range(ub):
                for r in range(di // LANES):
                    st_ref[s, :, r * LANES:(r + 1) * LANES] = s0_ref[s, r * LANES:(r + 1) * LANES, :].T
                carry_ref[s, SUBLANES - (kconv - 1):, :] = buf_ref[s]
        else:
            st_ref[...] = jnp.zeros_like(st_ref)

    cdim = cw_ref.shape[1]
    rows = lax.broadcasted_iota(jnp.int32, (c_rows, 1), 0)
    rows8 = rows[:SUBLANES]
    cw = cw_ref[...]
    a_neg = -jnp.exp(alog_ref[...])
    tril = _tril_mask(c_rows)
    dsk = dsk_ref[...]
    nw = nw_ref[...]
    lo = lax.broadcasted_iota(jnp.int32, (c_rows, LANES), 1) < hp
    for s in range(ub):
        for cc in range(uc):
            r0 = (s * uc + cc) * c_rows
            valid = ((c * uc + cc) * c_rows + rows) < t_valid if t_valid < t_pad else None
            _ssd_chunk(zx_ref[r0:r0 + c_rows, :], st_ref.at[s], carry_ref.at[s], y_ref.at[r0:r0 + c_rows, :],
                       cw, cb_ref[...], dtb_ref[...], a_neg, dsk, nw, ex_ref, tril, lo, rows8, valid, conv_done)

    @pl.when(c == nc - 1)
    def _():
        for s in range(ub):
            for r in range(di // LANES):
                so_ref[s, r * LANES:(r + 1) * LANES, :] = st_ref[s, :, r * LANES:(r + 1) * LANES].T


def _ssd_chunk(zx, st_ref, carry_ref, y_ref, cw, cb, dtb, a_neg, dsk, nw, ex_ref, tril, lo, rows8, valid, conv_done):
    hp, ns, ngr = SSD_HEAD_DIM, SSD_STATE, SSD_GROUPS
    c_rows = zx.shape[0]
    di = y_ref.shape[1]
    nh = di // hp
    hpg = nh // ngr
    gw = ngr * ns
    kconv, cdim = cw.shape
    pair = 2 * hp
    z = zx[:, :di]
    u = zx[:, di:di + cdim]
    if conv_done:
        xbc = u
    else:
        xbc = _silu(_causal_conv(u, carry_ref[...], cw, cb, rows8))
        carry_ref[...] = u[c_rows - SUBLANES:, :]
    dt = jax.nn.softplus(zx[:, di + cdim:] + dtb)
    if valid is not None:
        xbc = jnp.where(valid, xbc, 0.0)
        dt = jnp.where(valid, dt, 0.0)
    xs = xbc[:, :di]
    bm = xbc[:, di:di + gw]
    cm = xbc[:, di + gw:di + 2 * gw]
    acum = _cumsum_rows(dt * a_neg, tril.astype(bf16))

    def head_rows(a):
        if c_rows < LANES:
            a = jnp.concatenate([a, jnp.zeros((LANES - c_rows, LANES), f32)], axis=0)
        return a.T[:, :c_rows]

    acum_t = head_rows(acum)
    dt_t = head_rows(dt)
    a_last = acum[c_rows - 1:c_rows, :]
    xd = xs * _expand_heads(dt * jnp.exp(a_last - acum), ex_ref)

    ys = []
    for g in range(ngr):
        bg = bm[:, g * ns:(g + 1) * ns]
        cg = cm[:, g * ns:(g + 1) * ns]
        cb_mat = _dot_nt(cg, bg)
        gsl = slice(g * hpg * hp, (g + 1) * hpg * hp)
        y_inter = _dot(cg, st_ref[:, gsl])
        decs = []
        for q in range(hpg // 2):
            h1 = g * hpg + 2 * q
            psl = slice(h1 * hp, h1 * hp + pair)
            ms, es = [], []
            for h in (h1, h1 + 1):
                colb = jnp.broadcast_to(acum[:, h:h + 1], (c_rows, LANES))
                seg = colb[:, :c_rows] - acum_t[h:h + 1, :]
                ms.append(cb_mat * jnp.exp(jnp.where(tril, seg, -jnp.inf)) * dt_t[h:h + 1, :])
                es.append(jnp.exp(colb))
            xp = xs[:, psl]
            x_lo = jnp.where(lo, xp, 0.0)
            x_hi = jnp.where(lo, 0.0, xp)
            if (2 * c_rows) % LANES == 0:
                y_intra = _dot(jnp.concatenate(ms, axis=1), jnp.concatenate([x_lo, x_hi], axis=0))
            else:
                y_intra = _dot(ms[0], x_lo) + _dot(ms[1], x_hi)
            y_pair = y_intra + y_inter[:, q * pair:(q + 1) * pair] * jnp.where(lo, es[0], es[1])
            ys.append(y_pair + dsk[:, psl] * xp)
            decs.append(jnp.where(lo[0:1, :], jnp.exp(a_last[:, h1:h1 + 1]), jnp.exp(a_last[:, h1 + 1:h1 + 2])))
        st_ref[:, gsl] = st_ref[:, gsl] * jnp.concatenate(decs, axis=1) + _dot_tn(bg, xd[:, gsl])

    y = jnp.concatenate(ys, axis=1) * _silu(z)
    gdim = di // ngr
    for g in range(ngr):
        sl = slice(g * gdim, (g + 1) * gdim)
        yg = y[:, sl]
        y_ref[:, sl] = (yg * lax.rsqrt(jnp.mean(yg * yg, axis=-1, keepdims=True) + RMS_EPS) * nw[:, sl]).astype(y_ref.dtype)


def ssd_call(zx, conv_w, conv_b, dt_bias, a_log, d_skip, norm_w, s0, conv_buf, bsz, t_pad, t_valid, conv_done):
    hp, ns = SSD_HEAD_DIM, SSD_STATE
    conv_dim = conv_w.shape[1]
    di = zx.shape[1] - conv_dim - LANES
    nh = di // hp
    c_rows = min(128, t_pad)
    nc = t_pad // c_rows
    ub, uc = _units_per_step(bsz, nc, 4, 2)
    rows = ub * uc * c_rows
    ncs = nc // uc
    has_s0 = s0 is not None
    assert not (conv_done and has_s0)
    pad_h = lambda a: jnp.pad(a.reshape(1, nh), ((0, 0), (0, LANES - nh)))
    full = lambda shape: pl.BlockSpec(shape, lambda b, c: (0,) * len(shape))
    sel = (jnp.arange(di)[None, :] // hp == jnp.arange(LANES)[:, None]).astype(bf16)
    expand = jnp.concatenate([sel, sel, sel], axis=0)
    in_specs = [pl.BlockSpec((rows, zx.shape[1]), lambda b, c: (b * ncs + c, 0)),
                full(conv_w.shape), full((1, conv_dim)), full((1, LANES)), full((1, LANES)),
                full((1, di)), full((1, di)), full(expand.shape)]
    args = [zx, conv_w, conv_b.reshape(1, conv_dim), pad_h(dt_bias), pad_h(a_log),
            jnp.repeat(d_skip, hp).reshape(1, di), norm_w.reshape(1, di), expand]
    if has_s0:
        in_specs += [pl.BlockSpec((ub, di, ns), lambda b, c: (b, 0, 0)),
                     pl.BlockSpec((ub,) + conv_buf.shape[1:], lambda b, c: (b, 0, 0))]
        args += [s0.reshape(bsz, di, ns), conv_buf]
    kern = functools.partial(_ssd_kernel, c_rows=c_rows, t_valid=t_valid, t_pad=t_pad, has_s0=has_s0, ub=ub, uc=uc,
                             conv_done=conv_done)
    y, s_new = pl.pallas_call(
        kern,
        grid=(bsz // ub, ncs),
        in_specs=in_specs,
        out_specs=[pl.BlockSpec((rows, di), lambda b, c: (b * ncs + c, 0)),
                   pl.BlockSpec((ub, di, ns), lambda b, c: (b, 0, 0))],
        out_shape=[jax.ShapeDtypeStruct((bsz * t_pad, di), bf16),
                   jax.ShapeDtypeStruct((bsz, di, ns), f32)],
        scratch_shapes=[pltpu.VMEM((ub, ns, di), f32), pltpu.VMEM((ub, SUBLANES, conv_dim), f32)],
        compiler_params=_cparams(2),
        name="ssd",
    )(*args)
    return y, s_new.reshape(bsz, nh, hp, ns)


def _trunk(x, c_k, c_v, s_hg, s_ssm, s_sconv, s_fconv, prm):
    bsz, t, d = x.shape
    depth = prm["norm_mix_w"].shape[0]
    decode = c_k is not None
    tp = -(-t // SUBLANES) * SUBLANES
    if tp != t:
        x = jnp.pad(x, ((0, 0), (0, tp - t), (0, 0)))
    m = bsz * tp
    n_heads = d // ATT_HEAD_DIM
    qd = n_heads * ATT_HEAD_DIM
    kvd = ATT_KV_HEADS * ATT_HEAD_DIM
    x2 = x.reshape(m, d)
    xn, xn_w = x2, prm["norm_mix_w"][0]
    nk, nv, nhg, nssm, nsconv, nfconv = [], [], [], [], [], []
    slot = [0] * N_MIXERS
    for i in range(depth):
        kind = i % N_MIXERS
        j = slot[kind]
        slot[kind] += 1
        nw_ffn = prm["norm_ffn_w"][i]
        if kind == 0:
            w_qkv, b_qkv = prm["attn_w_qkv"], prm["attn_b_qkv"][j]
            nq = w_qkv.shape[2]
            if decode:
                qkv = proj_call(xn, xn_w, w_qkv, j, b_qkv, f32, "attn_qkv").reshape(bsz, tp, nq)
                ck = jnp.transpose(c_k[j], (0, 2, 3, 1)).reshape(bsz, kvd, WINDOW)
                cv = jnp.transpose(c_v[j], (0, 2, 3, 1)).reshape(bsz, kvd, WINDOW)
                o = attn_cache_call(qkv, ck, cv, prm["attn_sinks"][j], n_heads, t).reshape(m, qd)
                to4 = lambda a: a.reshape(bsz, t, ATT_KV_HEADS, ATT_HEAD_DIM)
                k_new = to4(qkv[:, :t, qd:qd + kvd])
                v_new = to4(qkv[:, :t, qd + kvd:])
            else:
                qkv = proj_call(xn, xn_w, w_qkv, j, b_qkv, bf16, "attn_qkv")
                o = attn_prompt_call(qkv, prm["attn_sinks"][j], bsz, t, n_heads)
                xn_last = xn.reshape(bsz, t, d)[:, t - WINDOW:].reshape(bsz * WINDOW, d)
                kv_last = proj_call(xn_last, xn_w, w_qkv[:, :, qd:], j, b_qkv[qd:], f32,
                                    "attn_kv_tail").reshape(bsz, WINDOW, 2, kvd)
                to4 = lambda a: a.reshape(bsz, WINDOW, ATT_KV_HEADS, ATT_HEAD_DIM)
                k_new, v_new = to4(kv_last[:, :, 0]), to4(kv_last[:, :, 1])
            nk.append(k_new)
            nv.append(v_new)
            act, w_o, b_o = o, (prm["attn_w_o"], j), prm["attn_b_o"][j]
        elif kind == 1:
            p = proj_call(xn, xn_w, prm["hgrn_w_in"], j, None, f32, "hgrn_in")
            y, s_new = hgrn_call(p, prm["hgrn_lb_logits"], prm["hgrn_norm_w"][j],
                                 None if s_hg is None else s_hg[j], bsz, tp, t, i)
            nhg.append(s_new)
            act, w_o, b_o = y, (prm["hgrn_w_o"], j), None
        else:
            conv_dim = prm["ssd_conv_w"].shape[2]
            di = prm["ssd_w_o"].shape[1]
            kc = prm["ssd_conv_w"].shape[1]
            if s_sconv is None:
                zx, pre = ssd_in_call(xn, prm["ssd_w_in"], j, prm["ssd_conv_w"][j], prm["ssd_conv_b"][j], tp)
                pre = pre.reshape(bsz, tp // min(PROJ_TM, m), SUBLANES, conv_dim)[:, -1, SUBLANES - (kc - 1):]
            else:
                zx = proj_call(xn, xn_w, prm["ssd_w_in"], j, None, f32, "ssd_in")
                pre = zx.reshape(bsz, tp, zx.shape[1])[:, max(t - (kc - 1), 0):t, di:di + conv_dim]
                pre = jnp.concatenate([s_sconv[j], pre], axis=1)
                pre = pre[:, pre.shape[1] - (kc - 1):]
            nsconv.append(pre)
            y, s_new = ssd_call(zx, prm["ssd_conv_w"][j], prm["ssd_conv_b"][j], prm["ssd_dt_bias"][j],
                                prm["ssd_a_log"][j], prm["ssd_d"][j], prm["ssd_norm_w"][j],
                                None if s_ssm is None else s_ssm[j],
                                None if s_sconv is None else s_sconv[j], bsz, tp, t, s_sconv is None)
            nssm.append(s_new)
            act, w_o, b_o = y, (prm["ssd_w_o"], j), None
        last = i == depth - 1
        next_w = prm["norm_final_w"] if last else prm["norm_mix_w"][i + 1]
        dff = prm["ffn_w_down"].shape[1]
        kf = prm["ffn_conv_w"].shape[1]
        if s_fconv is not None:
            buf = s_fconv[i]
            e1 = jnp.pad(buf[:, 1:2], ((0, 0), (0, tp - 1), (0, 0))).reshape(m, dff)
            e2 = jnp.pad(buf, ((0, 0), (0, tp - 2), (0, 0))).reshape(m, dff)
        else:
            e1 = e2 = None
        x2, xn, tail = ffn_call(act, w_o, b_o, x2, nw_ffn, (prm["ffn_w_up"], i), prm["ffn_conv_w"][i],
                                prm["ffn_conv_b"][i], (prm["ffn_w_down"], i), next_w, tp, e1, e2,
                                f32 if last else bf16, "ffn")
        xn_w = None
        if s_fconv is not None:
            a_full = tail.reshape(bsz, tp, dff)[:, :t]
            nfconv.append(jnp.concatenate([s_fconv[i], a_full], axis=1)[:, t:])
        else:
            tiles_per_seq = tp // tail_tile_rows(m)
            tl = tail.reshape(bsz, tiles_per_seq, SUBLANES, dff)[:, -1]
            nfconv.append(tl[:, SUBLANES - (kf - 1):])
    stack = lambda xs: xs[0][None] if len(xs) == 1 else jnp.stack(xs)
    y = xn.reshape(bsz, tp, d)[:, :t]
    nk, nv = stack(nk), stack(nv)
    if decode:
        nk = jnp.concatenate([c_k[:, :, t:], nk], axis=2)
        nv = jnp.concatenate([c_v[:, :, t:], nv], axis=2)
    return (y, nk, nv, stack(nhg), stack(nssm), stack(nsconv), stack(nfconv))


def tail_tile_rows(m):
    return min(FFN_TM, m)


def kernel(x_prompt, x_sample, cache_attn_k, cache_attn_v, state_hgrn, state_ssm, state_ssm_conv, state_ffn_conv, norm_mix_w, norm_ffn_w, norm_final_w, attn_w_qkv, attn_b_qkv, attn_sinks, attn_w_o, attn_b_o, hgrn_w_in, hgrn_lb_logits, hgrn_norm_w, hgrn_w_o, ssd_w_in, ssd_conv_w, ssd_conv_b, ssd_dt_bias, ssd_a_log, ssd_d, ssd_norm_w, ssd_w_o, ffn_w_up, ffn_conv_w, ffn_conv_b, ffn_w_down):
    cast = lambda w: w.astype(bf16)
    ssd_in_pad = (-ssd_w_in.shape[2]) % LANES
    prm = dict(norm_mix_w=norm_mix_w, norm_ffn_w=norm_ffn_w, norm_final_w=norm_final_w,
               attn_w_qkv=cast(attn_w_qkv), attn_b_qkv=attn_b_qkv, attn_sinks=attn_sinks,
               attn_w_o=cast(attn_w_o), attn_b_o=attn_b_o,
               hgrn_w_in=cast(hgrn_w_in), hgrn_lb_logits=hgrn_lb_logits, hgrn_norm_w=hgrn_norm_w,
               hgrn_w_o=cast(hgrn_w_o),
               ssd_w_in=jnp.pad(cast(ssd_w_in), ((0, 0), (0, 0), (0, ssd_in_pad))), ssd_conv_w=ssd_conv_w,
               ssd_conv_b=ssd_conv_b, ssd_dt_bias=ssd_dt_bias,
               ssd_a_log=ssd_a_log, ssd_d=ssd_d, ssd_norm_w=ssd_norm_w, ssd_w_o=cast(ssd_w_o),
               ffn_w_up=cast(ffn_w_up), ffn_conv_w=ffn_conv_w, ffn_conv_b=ffn_conv_b, ffn_w_down=cast(ffn_w_down))
    outs_p = _trunk(x_prompt, None, None, None, None, None, None, prm)
    outs_s = _trunk(x_sample, cache_attn_k, cache_attn_v, state_hgrn, state_ssm, state_ssm_conv,
                    state_ffn_conv, prm)
    return (outs_p[0], outs_s[0]) + outs_p[1:] + outs_s[1:]
```

```python
import functools

import jax
import jax.numpy as jnp
from jax import lax
from jax.experimental import pallas as pl
from jax.experimental.pallas import tpu as pltpu

bf16 = jnp.bfloat16
f32 = jnp.float32

RMS_EPS = 1e-6
N_MIXERS = 3
WINDOW = 128
PAST_LEN = 8192
ATT_HEAD_DIM = 64
ATT_KV_HEADS = 4
HG_HEADS = 8
HG_CHUNK_GROUP = 32
SSD_HEAD_DIM = 64
SSD_STATE = 128
SSD_GROUPS = 4

LANES = 128
SUBLANES = 8
VMEM_LIMIT = 56 * 1024 * 1024
PROJ_TM = 512
FFN_TM = 512


def _cparams(n_axes):
    return pltpu.CompilerParams(dimension_semantics=("arbitrary",) * n_axes,
                                vmem_limit_bytes=VMEM_LIMIT)


def _dot(a, b):
    return jnp.dot(a.astype(bf16), b.astype(bf16), preferred_element_type=f32)


def _dot_nt(a, b):
    return lax.dot_general(a.astype(bf16), b.astype(bf16), (((1,), (1,)), ((), ())),
                           preferred_element_type=f32)


def _dot_tn(a, b):
    return lax.dot_general(a.astype(bf16), b.astype(bf16), (((0,), (0,)), ((), ())),
                           preferred_element_type=f32)


def _silu(x):
    return x * jax.nn.sigmoid(x)


def _rms(x, w):
    ms = jnp.mean(x * x, axis=-1, keepdims=True)
    return x * lax.rsqrt(ms + RMS_EPS) * w


def _cumsum_rows(x, tril):
    hi = x.astype(bf16)
    r1 = x - hi.astype(f32)
    mid = r1.astype(bf16)
    lo = (r1 - mid.astype(f32)).astype(bf16)
    d = functools.partial(jnp.dot, preferred_element_type=f32)
    return d(tril, hi) + d(tril, mid) + d(tril, lo)


def _tril_mask(c):
    r = lax.broadcasted_iota(jnp.int32, (c, c), 0)
    s = lax.broadcasted_iota(jnp.int32, (c, c), 1)
    return r >= s


def _rows_to_top(x, off):
    return x if off == 0 else pltpu.roll(x, x.shape[0] - off, 0)


def _rows_back(y, off):
    return y if off == 0 else pltpu.roll(y, off, 0)


def _tile_layout(seq_rows):
    if seq_rows % SUBLANES == 0:
        c_rows = min(128, seq_rows)
        return c_rows, 1, seq_rows // c_rows
    assert SUBLANES % seq_rows == 0
    return SUBLANES, SUBLANES // seq_rows, 1


def _resident(shape):
    return pl.BlockSpec(shape, lambda *_: (0,) * len(shape), pipeline_mode=pl.Buffered(1))


def _resident_layer(w, layer):
    return pl.BlockSpec((None,) + w.shape[1:], lambda *_: (layer, 0, 0), pipeline_mode=pl.Buffered(1))


def _proj_kernel(x_ref, nw_ref, w_ref, b_ref, o_ref):
    x = x_ref[...]
    if x.dtype != bf16:
        x = _rms(x, nw_ref[...]).astype(bf16)
    acc = jnp.dot(x, w_ref[...], preferred_element_type=f32)
    o_ref[...] = (acc + b_ref[...]).astype(o_ref.dtype)


def proj_call(x, norm_w, w, layer, bias, out_dtype, name):
    m, k = x.shape
    n = w.shape[2]
    tm = min(PROJ_TM, m)
    if bias is None:
        bias = jnp.zeros((n,), f32)
    if norm_w is None:
        assert x.dtype == bf16
        norm_w = jnp.ones((k,), f32)
    return pl.pallas_call(
        _proj_kernel,
        grid=(m // tm,),
        in_specs=[pl.BlockSpec((tm, k), lambda i: (i, 0)), _resident((1, k)), _resident_layer(w, layer),
                  _resident((1, n))],
        out_specs=pl.BlockSpec((tm, n), lambda i: (i, 0)),
        out_shape=jax.ShapeDtypeStruct((m, n), out_dtype),
        compiler_params=_cparams(1),
        name=name,
    )(x, norm_w.reshape(1, k), w, bias.reshape(1, n))


def _causal_conv(u, prev, cw, cb, rows8):
    kconv = cw.shape[0]
    yc = cb
    for kk in range(kconv - 1):
        back = kconv - 1 - kk
        sh = pltpu.roll(u, back, 0)
        top = sh[:SUBLANES]
        for r0 in range(back):
            top = jnp.where(rows8 == r0, prev[SUBLANES - back + r0:SUBLANES - back + r0 + 1, :], top)
        sh = jnp.concatenate([top, sh[SUBLANES:]], axis=0) if u.shape[0] > SUBLANES else top
        yc = yc + sh * cw[kk:kk + 1, :]
    return yc + u * cw[kconv - 1:kconv, :]


def _ssd_in_kernel(x_ref, w_ref, cw_ref, cb_ref, o_ref, tail_ref, carry_ref, *, tm, tc, t_seq, di):
    i = pl.program_id(0)
    cdim = cw_ref.shape[1]

    @pl.when(((i * tm) & (t_seq - 1)) == 0)
    def _():
        carry_ref[...] = jnp.zeros_like(carry_ref)

    x = x_ref[...]
    rows8 = lax.broadcasted_iota(jnp.int32, (SUBLANES, 1), 0)
    nz = di // tc
    for idx, c0 in enumerate(range(0, cdim, tc)):
        cs = slice(c0, c0 + tc)
        u = jnp.dot(x, w_ref[:, di + c0:di + c0 + tc], preferred_element_type=f32)
        tail_ref[0, :, cs] = u[tm - SUBLANES:, :]
        yc = _causal_conv(u, carry_ref[:, cs], cw_ref[:, cs], cb_ref[:, cs], rows8)
        carry_ref[:, cs] = u[tm - SUBLANES:, :]
        o_ref[:, di + c0:di + c0 + tc] = _silu(yc)
        if idx < nz:
            o_ref[:, idx * tc:(idx + 1) * tc] = jnp.dot(x, w_ref[:, idx * tc:(idx + 1) * tc],
                                                        preferred_element_type=f32)
    o_ref[:, di + cdim:] = jnp.dot(x, w_ref[:, di + cdim:], preferred_element_type=f32)


def ssd_in_call(xn, w, layer, conv_w, conv_b, t_seq):
    m, k = xn.shape
    n = w.shape[2]
    cdim = conv_w.shape[1]
    di = n - cdim - LANES
    tm = min(PROJ_TM, m)
    tc = 2 * LANES
    assert t_seq % tm == 0 and cdim % tc == 0 and di % tc == 0 and di <= cdim and (t_seq & (t_seq - 1)) == 0
    kern = functools.partial(_ssd_in_kernel, tm=tm, tc=tc, t_seq=t_seq, di=di)
    return pl.pallas_call(
        kern,
        grid=(m // tm,),
        in_specs=[pl.BlockSpec((tm, k), lambda i: (i, 0)), _resident_layer(w, layer),
                  _resident(conv_w.shape), _resident((1, cdim))],
        out_specs=[pl.BlockSpec((tm, n), lambda i: (i, 0)), pl.BlockSpec((1, SUBLANES, cdim), lambda i: (i, 0, 0))],
        out_shape=[jax.ShapeDtypeStruct((m, n), f32), jax.ShapeDtypeStruct((m // tm, SUBLANES, cdim), f32)],
        scratch_shapes=[pltpu.VMEM((SUBLANES, cdim), f32)],
        compiler_params=_cparams(1),
        name="ssd_in_conv",
    )(xn, w, conv_w, conv_b.reshape(1, cdim))


def _ffn_kernel(*refs, tm, tf, t_seq, tail, has_state):
    if has_state:
        (act_ref, wo_ref, bo_ref, x_ref, nw1_ref, wu_ref, cw_ref, cb_ref, wd_ref, nw2_ref, e1_ref, e2_ref,
         xo_ref, xn2_ref, tail_ref, carry_ref) = refs
    else:
        (act_ref, wo_ref, bo_ref, x_ref, nw1_ref, wu_ref, cw_ref, cb_ref, wd_ref, nw2_ref,
         xo_ref, xn2_ref, tail_ref, carry_ref) = refs
    i = pl.program_id(0)
    dff = wd_ref.shape[0]

    x1 = jnp.dot(act_ref[...].astype(bf16), wo_ref[...], preferred_element_type=f32) + bo_ref[...] + x_ref[...]
    xn = _rms(x1, nw1_ref[...]).astype(bf16)
    if has_state:
        tpos = (lax.broadcasted_iota(jnp.int32, (tm, 1), 0)) & (t_seq - 1)
    else:
        @pl.when(((i * tm) & (t_seq - 1)) == 0)
        def _():
            carry_ref[...] = jnp.zeros_like(carry_ref)
        rows8 = lax.broadcasted_iota(jnp.int32, (SUBLANES, 1), 0)
    gs = []
    for c0 in range(0, dff, tf):
        cs = slice(c0, c0 + tf)
        a = jnp.dot(xn, wu_ref[:, cs], preferred_element_type=f32)
        b = jnp.dot(xn, wu_ref[:, dff + c0:dff + c0 + tf], preferred_element_type=f32)
        tail_ref[0, :, cs] = a[tm - tail:, :]
        r1 = pltpu.roll(a, 1, 0)
        r2 = pltpu.roll(a, 2, 0)
        if has_state:
            p1 = jnp.where(tpos == 0, e1_ref[:, cs], r1)
            p2 = jnp.where(tpos < 2, e2_ref[:, cs], r2)
        else:
            carry = carry_ref[:, cs]
            top1 = jnp.where(rows8 == 0, carry[7:8, :], r1[:SUBLANES])
            top2 = jnp.where(rows8 == 0, carry[6:7, :], jnp.where(rows8 == 1, carry[7:8, :], r2[:SUBLANES]))
            p1 = jnp.concatenate([top1, r1[SUBLANES:]], axis=0)
            p2 = jnp.concatenate([top2, r2[SUBLANES:]], axis=0)
            carry_ref[:, cs] = a[tm - SUBLANES:, :]
        y = cb_ref[:, cs] + p2 * cw_ref[0:1, cs]
        y = y + p1 * cw_ref[1:2, cs]
        y = y + a * cw_ref[2:3, cs]
        gs.append((_silu(y) * b).astype(bf16))
    g = jnp.concatenate(gs, axis=1)
    xo = jnp.dot(g, wd_ref[...], preferred_element_type=f32) + x1
    xo_ref[...] = xo
    xn2_ref[...] = _rms(xo, nw2_ref[...]).astype(xn2_ref.dtype)


def ffn_call(act, w_o, b_o, x, ffn_norm_w, w_up, conv_w, conv_b, w_down, next_w, t_seq, e1, e2, out_dtype, name):
    m, d = x.shape
    ko = act.shape[1]
    dff = w_down[0].shape[1]
    tf = 2 * LANES
    assert dff % tf == 0 and (t_seq & (t_seq - 1)) == 0
    has_state = e1 is not None
    tm = min(FFN_TM // 2 if has_state else FFN_TM, m)
    assert (tm % t_seq == 0) if has_state else (t_seq % tm == 0)
    tail = tm if has_state else SUBLANES
    if b_o is None:
        b_o = jnp.zeros((d,), f32)
    row = lambda w: pl.BlockSpec((tm, w), lambda i: (i, 0))
    in_specs = [row(ko), _resident_layer(*w_o), _resident((1, d)), row(d), _resident((1, d)),
                _resident_layer(*w_up), _resident((3, dff)), _resident((1, dff)),
                _resident_layer(*w_down), _resident((1, d))]
    args = [act, w_o[0], b_o.reshape(1, d), x, ffn_norm_w.reshape(1, d),
            w_up[0], conv_w, conv_b.reshape(1, dff), w_down[0], next_w.reshape(1, d)]
    if has_state:
        in_specs += [row(dff), row(dff)]
        args += [e1, e2]
    kern = functools.partial(_ffn_kernel, tm=tm, tf=tf, t_seq=t_seq, tail=tail, has_state=has_state)
    return pl.pallas_call(
        kern,
        grid=(m // tm,),
        in_specs=in_specs,
        out_specs=[row(d), row(d), pl.BlockSpec((1, tail, dff), lambda i: (i, 0, 0))],
        out_shape=[jax.ShapeDtypeStruct((m, d), f32),
                   jax.ShapeDtypeStruct((m, d), out_dtype),
                   jax.ShapeDtypeStruct((m // tm, tail, dff), f32)],
        scratch_shapes=[pltpu.VMEM((SUBLANES, dff), f32)],
        compiler_params=_cparams(1),
        name=name,
    )(*args)


def _attn_softmax_pv(s, sink_col, vj):
    m = jnp.maximum(jnp.max(s, axis=-1, keepdims=True), sink_col)
    p = jnp.exp(s - m)
    l = jnp.sum(p, axis=-1, keepdims=True) + jnp.exp(sink_col - m)
    return _dot(p, vj) / l


def _sink_col(sinks_ref, j, group, rows_per_head):
    r = lax.broadcasted_iota(jnp.int32, (group * rows_per_head, 1), 0)
    col = jnp.full((group * rows_per_head, 1), sinks_ref[j * group], f32)
    for p in range(1, group):
        col = jnp.where(r >= p * rows_per_head, sinks_ref[j * group + p], col)
    return col


def _attn_prompt_kernel(sinks_ref, q_ref, kvc_ref, kvp_ref, o_ref, *, n_heads):
    n = pl.program_id(1)
    w, hd, kvh = WINDOW, ATT_HEAD_DIM, ATT_KV_HEADS
    group = n_heads // kvh
    kvw = kvh * hd
    assert group == kvh
    kv = jnp.concatenate([kvp_ref[...], kvc_ref[...]], axis=0)
    k2 = kv[:, :kvw]
    v2 = kv[:, kvw:]
    i = lax.broadcasted_iota(jnp.int32, (w, 2 * w), 0)
    c = lax.broadcasted_iota(jnp.int32, (w, 2 * w), 1)
    band = (c > i) & (c <= i + w) & ((n > 0) | (c >= w))
    mask = jnp.concatenate([band] * group, axis=0)
    slot = lax.broadcasted_iota(jnp.int32, (w, kvw), 1) >> (hd.bit_length() - 1)
    for j in range(kvh):
        qg = q_ref[:, j * kvw:(j + 1) * kvw] * (hd ** -0.5)
        qs = jnp.concatenate([jnp.where(slot == j, pltpu.roll(qg, ((j - p) % kvh) * hd, 1), 0)
                              for p in range(group)], axis=0)
        s = jnp.where(mask, _dot_nt(qs, k2), -jnp.inf)
        of = _attn_softmax_pv(s, _sink_col(sinks_ref, j, group, w), v2).astype(o_ref.dtype)
        og = of[0:w, :]
        for p in range(group):
            piece = of[p * w:(p + 1) * w, :]
            og = jnp.where(slot == p, pltpu.roll(piece, ((p - j) % kvh) * hd, 1), og)
        o_ref[:, j * kvw:(j + 1) * kvw] = og


def attn_prompt_call(qkv, sinks, bsz, t, n_heads):
    w, hd, kvh = WINDOW, ATT_HEAD_DIM, ATT_KV_HEADS
    nb = t // w
    qd = n_heads * hd
    kvd = 2 * kvh * hd
    assert qd % kvd == 0
    kvblk = qd // kvd
    kern = functools.partial(_attn_prompt_kernel, n_heads=n_heads)
    return pl.pallas_call(
        kern,
        grid=(bsz, nb),
        in_specs=[pl.BlockSpec(memory_space=pltpu.SMEM),
                  pl.BlockSpec((w, qd), lambda b, n: (b * nb + n, 0)),
                  pl.BlockSpec((w, kvd), lambda b, n: (b * nb + n, kvblk)),
                  pl.BlockSpec((w, kvd), lambda b, n: (b * nb + jnp.maximum(n - 1, 0), kvblk))],
        out_specs=pl.BlockSpec((w, qd), lambda b, n: (b * nb + n, 0)),
        out_shape=jax.ShapeDtypeStruct((bsz * t, qd), bf16),
        compiler_params=_cparams(2),
        name="attn_prompt",
    )(sinks, qkv, qkv, qkv)


def _attn_cache_kernel(sinks_ref, q_ref, ck_ref, cv_ref, o_ref, *, n_heads, t_valid, bt, spt):
    w, hd, kvh = WINDOW, ATT_HEAD_DIM, ATT_KV_HEADS
    group = n_heads // kvh
    tp = SUBLANES
    qd = n_heads * hd
    tq = lax.broadcasted_iota(jnp.int32, (tp, w), 0)
    jc = lax.broadcasted_iota(jnp.int32, (tp, w), 1)
    mc = (jc > tq) & (jc - w + PAST_LEN >= 0)
    tq2 = lax.broadcasted_iota(jnp.int32, (tp, tp), 0)
    un = lax.broadcasted_iota(jnp.int32, (tp, tp), 1)
    mn = (un <= tq2) & (un < t_valid)
    mask_c = jnp.concatenate([mc] * n_heads, axis=0)
    mask_n = jnp.concatenate([mn] * n_heads, axis=0)
    kvw = kvh * hd
    assert group == kvh
    slot = lax.broadcasted_iota(jnp.int32, (tp, kvw), 1) >> (hd.bit_length() - 1)
    sink = _sink_col(sinks_ref, 0, n_heads, tp)

    rows = lax.broadcasted_iota(jnp.int32, (tp, 1), 0)
    held = None
    for bi in range(bt):
        r0 = (bi // spt) * tp
        off = (bi % spt) * (tp // spt)
        row = _rows_to_top(q_ref[r0:r0 + tp, :], off)
        ck = ck_ref[bi]
        cv = cv_ref[bi]
        kn = row[:, qd:qd + kvw]
        vn = row[:, qd + kvw:]
        pieces = []
        for j in range(kvh):
            qg = row[:, j * kvw:(j + 1) * kvw] * (hd ** -0.5)
            for p in range(group):
                pieces.append(jnp.where(slot == j, pltpu.roll(qg, ((j - p) % kvh) * hd, 1), 0.0))
        qs = jnp.concatenate(pieces, axis=0)
        sc = jnp.where(mask_c, _dot(qs, ck), -jnp.inf)
        sn = jnp.where(mask_n, _dot_nt(qs, kn), -jnp.inf)
        m = jnp.maximum(jnp.maximum(jnp.max(sc, axis=-1, keepdims=True),
                                    jnp.max(sn, axis=-1, keepdims=True)), sink)
        pc = jnp.exp(sc - m)
        pn = jnp.exp(sn - m)
        l = jnp.sum(pc, axis=-1, keepdims=True) + jnp.sum(pn, axis=-1, keepdims=True) + jnp.exp(sink - m)
        of = (_dot_nt(pc, cv) + _dot(pn, vn)) / l
        ogs = []
        for j in range(kvh):
            og = jnp.zeros((tp, kvw), f32)
            for p in range(group):
                h = j * group + p
                og = og + jnp.where(slot == p, pltpu.roll(of[h * tp:(h + 1) * tp, :], ((p - j) % kvh) * hd, 1), 0.0)
            og = _rows_back(og, off)
            ogs.append(og if off == 0 else jnp.where(rows < off, held[j], og))
        if (bi + 1) % spt == 0:
            for j in range(kvh):
                o_ref[r0:r0 + tp, j * kvw:(j + 1) * kvw] = ogs[j]
        held = ogs


def attn_cache_call(qkv, cache_k, cache_v, sinks, n_heads, bsz, seq_rows, t_valid):
    nq = qkv.shape[1]
    w = WINDOW
    kd = cache_k.shape[1]
    qd = n_heads * ATT_HEAD_DIM
    bt = 8
    tile, spt, nc = _tile_layout(seq_rows)
    assert tile == SUBLANES and nc == 1 and bt % spt == 0
    rows = (bt // spt) * tile
    kern = functools.partial(_attn_cache_kernel, n_heads=n_heads, t_valid=t_valid, bt=bt, spt=spt)
    return pl.pallas_call(
        kern,
        grid=(bsz // bt,),
        in_specs=[pl.BlockSpec(memory_space=pltpu.SMEM),
                  pl.BlockSpec((rows, nq), lambda b: (b, 0)),
                  pl.BlockSpec((bt, kd, w), lambda b: (b, 0, 0)),
                  pl.BlockSpec((bt, kd, w), lambda b: (b, 0, 0))],
        out_specs=pl.BlockSpec((rows, qd), lambda b: (b, 0)),
        out_shape=jax.ShapeDtypeStruct((bsz * seq_rows, qd), f32),
        compiler_params=_cparams(1),
        name="attn_cache",
    )(sinks, qkv, cache_k, cache_v)


def _hgrn_chunk(x, st, lb, nw, tril, diag_mask, valid, grp):
    nh = HG_HEADS
    c_rows = x.shape[0]
    dim = x.shape[1] // 4
    dk = dim // nh
    ng = c_rows // grp
    if valid is not None:
        x = jnp.where(valid, x, 0.0)
    q = _silu(x[:, 0:dim])
    fg = lb + (1.0 - lb) * jax.nn.sigmoid(x[:, dim:2 * dim])
    logf = jnp.log(fg)
    k = 1.0 - fg
    v = x[:, 2 * dim:3 * dim]
    gate = x[:, 3 * dim:4 * dim]
    if valid is not None:
        logf = jnp.where(valid, logf, 0.0)
        k = jnp.where(valid, k, 0.0)
    bcum = _cumsum_rows(logf, tril.astype(bf16))

    def rows_of(vals):
        return jnp.concatenate([jnp.broadcast_to(r, (grp, dk)) for r in vals], axis=0)

    ys, st_out = [], []
    for h in range(nh):
        sl = slice(h * dk, (h + 1) * dk)
        b = bcum[:, sl]
        qh, kh, vh = q[:, sl], k[:, sl], v[:, sl]
        zero = jnp.zeros((1, dk), f32)
        r = [zero] + [b[i * grp - 1:i * grp, :] for i in range(1, ng)] + [b[c_rows - 1:c_rows, :]]
        mid = [b[i * grp + grp // 2 - 1:i * grp + grp // 2, :] for i in range(ng)]
        b_last = r[ng]
        r_start = rows_of(r[:ng])
        r_end = rows_of(r[1:])
        r_mid = rows_of(mid)
        qg = qh * jnp.exp(b - r_start)
        kt = kh * jnp.exp(r_end - b)
        qm = qh * jnp.exp(b - r_mid)
        km = kh * jnp.exp(r_mid - b)
        a = jnp.where(diag_mask, _dot_nt(qm, km), 0.0)
        if ng > 1:
            lhs, rhs = [], []
            for j in range(ng - 1):
                lhs.append(jnp.concatenate(
                    [jnp.zeros((grp, dk), f32) if i <= j else qg[i * grp:(i + 1) * grp, :] * jnp.exp(r[i] - r[j + 1])
                     for i in range(ng)], axis=0))
                rhs.append(jnp.concatenate(
                    [kt[i * grp:(i + 1) * grp, :] if i == j else jnp.zeros((grp, dk), f32) for i in range(ng)], axis=0))
            a = a + _dot_nt(jnp.concatenate(lhs, axis=1), jnp.concatenate(rhs, axis=1))
        q_inter = qg * jnp.exp(r_start)
        o = _dot(a, vh) + _dot_nt(q_inter, st[h])
        k_dec = kt * jnp.exp(b_last - r_end)
        st_out.append(st[h] * jnp.exp(b_last) + _dot_tn(vh, k_dec))
        on = o * lax.rsqrt(jnp.mean(o * o, axis=-1, keepdims=True) + RMS_EPS)
        ys.append(on * nw[:, sl] * _silu(gate[:, sl]))
    return ys, st_out


def _hgrn_kernel(*refs, c_rows, grp, t_valid, t_pad, layer, has_s0, ub, uc, spt):
    if has_s0:
        p_ref, lbl_ref, nw_ref, s0_ref, y_ref, so_ref, st_ref = refs
    else:
        p_ref, lbl_ref, nw_ref, y_ref, so_ref, st_ref = refs
    c = pl.program_id(1)
    nc = pl.num_programs(1)
    nh = HG_HEADS
    dim = p_ref.shape[1] // 4
    dk = dim // nh

    @pl.when(c == 0)
    def _():
        if has_s0:
            for s in range(ub):
                for h in range(nh):
                    st_ref[s, h] = s0_ref[s, h].T
        else:
            st_ref[...] = jnp.zeros_like(st_ref)

    lbl = lbl_ref[...]
    e = jnp.exp(lbl - jnp.max(lbl, axis=0, keepdims=True))
    sm = e / jnp.sum(e, axis=0, keepdims=True)
    lb = jnp.zeros((1, dim), f32)
    for r in range(1, layer + 1):
        lb = lb + sm[r:r + 1, :]

    tril = _tril_mask(c_rows)
    rr = lax.broadcasted_iota(jnp.int32, (c_rows, c_rows), 0)
    ss = lax.broadcasted_iota(jnp.int32, (c_rows, c_rows), 1)
    gshift = grp.bit_length() - 1
    diag_mask = tril & ((rr >> gshift) == (ss >> gshift))
    nw = nw_ref[...]
    rows = lax.broadcasted_iota(jnp.int32, (c_rows, 1), 0)

    held = None
    for s in range(ub):
        st = [st_ref[s, h] for h in range(nh)]
        off = (s % spt) * (c_rows // spt)
        for cc in range(uc):
            r0 = ((s // spt) * uc + cc) * c_rows
            valid = ((c * uc + cc) * c_rows + rows) < t_valid if t_valid < t_pad else None
            ys, st = _hgrn_chunk(_rows_to_top(p_ref[r0:r0 + c_rows, :], off), st, lb, nw, tril, diag_mask, valid, grp)
            ys = [_rows_back(y, off) for y in ys]
            if off:
                ys = [jnp.where(rows < off, y0, y) for y0, y in zip(held, ys)]
            if (s + 1) % spt == 0:
                for h in range(nh):
                    y_ref[r0:r0 + c_rows, h * dk:(h + 1) * dk] = ys[h].astype(y_ref.dtype)
            held = ys
        for h in range(nh):
            st_ref[s, h] = st[h]

    @pl.when(c == nc - 1)
    def _():
        for s in range(ub):
            for h in range(nh):
                so_ref[s, h] = st_ref[s, h].T


def _units_per_step(bsz, nc, max_seqs, max_chunks):
    if nc == 1:
        return max(u for u in (1, 2, 4, 8) if u <= max_seqs and bsz % u == 0), 1
    return 1, max(u for u in (1, 2, 4, 8) if u <= max_chunks and nc % u == 0)


def hgrn_call(p, lb_logits, norm_w, s0, bsz, seq_rows, t_valid, layer):
    dim = p.shape[1] // 4
    nh = HG_HEADS
    dk = dim // nh
    c_rows, spt, nc = _tile_layout(seq_rows)
    t_pad = nc * c_rows
    grp = min(HG_CHUNK_GROUP, c_rows)
    ub, uc = _units_per_step(bsz, nc, 8, 4)
    assert ub % spt == 0
    rows = (ub // spt) * uc * c_rows
    ncs = nc // uc
    has_s0 = s0 is not None
    in_specs = [pl.BlockSpec((rows, 4 * dim), lambda b, c: (b * ncs + c, 0)),
                pl.BlockSpec(lb_logits.shape, lambda b, c: (0, 0)),
                pl.BlockSpec((1, dim), lambda b, c: (0, 0))]
    args = [p, lb_logits, norm_w.reshape(1, dim)]
    if has_s0:
        in_specs.append(pl.BlockSpec((ub, nh, dk, dk), lambda b, c: (b, 0, 0, 0)))
        args.append(s0)
    kern = functools.partial(_hgrn_kernel, c_rows=c_rows, grp=grp, t_valid=t_valid, t_pad=t_pad,
                             layer=layer, has_s0=has_s0, ub=ub, uc=uc, spt=spt)
    return pl.pallas_call(
        kern,
        grid=(bsz // ub, ncs),
        in_specs=in_specs,
        out_specs=[pl.BlockSpec((rows, dim), lambda b, c: (b * ncs + c, 0)),
                   pl.BlockSpec((ub, nh, dk, dk), lambda b, c: (b, 0, 0, 0))],
        out_shape=[jax.ShapeDtypeStruct((bsz * seq_rows, dim), bf16),
                   jax.ShapeDtypeStruct((bsz, nh, dk, dk), f32)],
        scratch_shapes=[pltpu.VMEM((ub, nh, dk, dk), f32)],
        compiler_params=_cparams(2),
        name="hgrn",
    )(*args)


def _expand_heads(w, ex_ref):
    hi = w.astype(bf16)
    r1 = w - hi.astype(f32)
    mid = r1.astype(bf16)
    lo = (r1 - mid.astype(f32)).astype(bf16)
    return jnp.dot(jnp.concatenate([hi, mid, lo], axis=1), ex_ref[...], preferred_element_type=f32)


def _ssd_kernel(*refs, c_rows, t_valid, t_pad, has_s0, ub, uc, spt, conv_done):
    if has_s0:
        (zx_ref, cw_ref, cb_ref, dtb_ref, alog_ref, dsk_ref, nw_ref, ex_ref, s0_ref, buf_ref,
         y_ref, so_ref, st_ref, carry_ref) = refs
    else:
        (zx_ref, cw_ref, cb_ref, dtb_ref, alog_ref, dsk_ref, nw_ref, ex_ref,
         y_ref, so_ref, st_ref, carry_ref) = refs
    c = pl.program_id(1)
    nc = pl.num_programs(1)
    hp, ns, ngr = SSD_HEAD_DIM, SSD_STATE, SSD_GROUPS
    di = y_ref.shape[1]
    nh = di // hp
    hpg = nh // ngr
    gw = ngr * ns
    kconv = cw_ref.shape[0]
    pair = 2 * hp
    assert pair == LANES and hpg % 2 == 0

    @pl.when(c == 0)
    def _():
        carry_ref[...] = jnp.zeros_like(carry_ref)
        if has_s0:
            for s in range(ub):
                for r in range(di // LANES):
                    st_ref[s, :, r * LANES:(r + 1) * LANES] = s0_ref[s, r * LANES:(r + 1) * LANES, :].T
                carry_ref[s, SUBLANES - (kconv - 1):, :] = buf_ref[s]
        else:
            st_ref[...] = jnp.zeros_like(st_ref)

    cdim = cw_ref.shape[1]
    rows = lax.broadcasted_iota(jnp.int32, (c_rows, 1), 0)
    rows8 = rows[:SUBLANES]
    cw = cw_ref[...]
    a_neg = -jnp.exp(alog_ref[...])
    tril = _tril_mask(c_rows)
    dsk = dsk_ref[...]
    nw = nw_ref[...]
    lo = lax.broadcasted_iota(jnp.int32, (c_rows, LANES), 1) < hp
    held = None
    for s in range(ub):
        off = (s % spt) * (c_rows // spt)
        for cc in range(uc):
            r0 = ((s // spt) * uc + cc) * c_rows
            valid = ((c * uc + cc) * c_rows + rows) < t_valid if t_valid < t_pad else None
            ys = _ssd_chunk(_rows_to_top(zx_ref[r0:r0 + c_rows, :], off), st_ref.at[s], carry_ref.at[s], di,
                            cw, cb_ref[...], dtb_ref[...], a_neg, dsk, nw, ex_ref, tril, lo, rows8, valid, conv_done)
            ys = [_rows_back(y, off) for y in ys]
            if off:
                ys = [jnp.where(rows < off, y0, y) for y0, y in zip(held, ys)]
            if (s + 1) % spt == 0:
                gdim = di // len(ys)
                for g, y in enumerate(ys):
                    y_ref[r0:r0 + c_rows, g * gdim:(g + 1) * gdim] = y.astype(y_ref.dtype)
            held = ys

    @pl.when(c == nc - 1)
    def _():
        for s in range(ub):
            for r in range(di // LANES):
                so_ref[s, r * LANES:(r + 1) * LANES, :] = st_ref[s, :, r * LANES:(r + 1) * LANES].T


def _ssd_chunk(zx, st_ref, carry_ref, di, cw, cb, dtb, a_neg, dsk, nw, ex_ref, tril, lo, rows8, valid, conv_done):
    hp, ns, ngr = SSD_HEAD_DIM, SSD_STATE, SSD_GROUPS
    c_rows = zx.shape[0]
    nh = di // hp
    hpg = nh // ngr
    gw = ngr * ns
    kconv, cdim = cw.shape
    pair = 2 * hp
    z = zx[:, :di]
    u = zx[:, di:di + cdim]
    if conv_done:
        xbc = u
    else:
        xbc = _silu(_causal_conv(u, carry_ref[...], cw, cb, rows8))
        carry_ref[...] = u[c_rows - SUBLANES:, :]
    dt = jax.nn.softplus(zx[:, di + cdim:] + dtb)
    if valid is not None:
        xbc = jnp.where(valid, xbc, 0.0)
        dt = jnp.where(valid, dt, 0.0)
    xs = xbc[:, :di]
    bm = xbc[:, di:di + gw]
    cm = xbc[:, di + gw:di + 2 * gw]
    acum = _cumsum_rows(dt * a_neg, tril.astype(bf16))

    def head_rows(a):
        if c_rows < LANES:
            a = jnp.concatenate([a, jnp.zeros((LANES - c_rows, LANES), f32)], axis=0)
        return a.T[:, :c_rows]

    acum_t = head_rows(acum)
    dt_t = head_rows(dt)
    a_last = acum[c_rows - 1:c_rows, :]
    xd = xs * _expand_heads(dt * jnp.exp(a_last - acum), ex_ref)

    ys = []
    for g in range(ngr):
        bg = bm[:, g * ns:(g + 1) * ns]
        cg = cm[:, g * ns:(g + 1) * ns]
        cb_mat = _dot_nt(cg, bg)
        gsl = slice(g * hpg * hp, (g + 1) * hpg * hp)
        y_inter = _dot(cg, st_ref[:, gsl])
        decs = []
        for q in range(hpg // 2):
            h1 = g * hpg + 2 * q
            psl = slice(h1 * hp, h1 * hp + pair)
            ms, es = [], []
            for h in (h1, h1 + 1):
                colb = jnp.broadcast_to(acum[:, h:h + 1], (c_rows, LANES))
                seg = colb[:, :c_rows] - acum_t[h:h + 1, :]
                ms.append(cb_mat * jnp.exp(jnp.where(tril, seg, -jnp.inf)) * dt_t[h:h + 1, :])
                es.append(jnp.exp(colb))
            xp = xs[:, psl]
            x_lo = jnp.where(lo, xp, 0.0)
            x_hi = jnp.where(lo, 0.0, xp)
            if (2 * c_rows) % LANES == 0:
                y_intra = _dot(jnp.concatenate(ms, axis=1), jnp.concatenate([x_lo, x_hi], axis=0))
            else:
                y_intra = _dot(ms[0], x_lo) + _dot(ms[1], x_hi)
            y_pair = y_intra + y_inter[:, q * pair:(q + 1) * pair] * jnp.where(lo, es[0], es[1])
            ys.append(y_pair + dsk[:, psl] * xp)
            decs.append(jnp.where(lo[0:1, :], jnp.exp(a_last[:, h1:h1 + 1]), jnp.exp(a_last[:, h1 + 1:h1 + 2])))
        st_ref[:, gsl] = st_ref[:, gsl] * jnp.concatenate(decs, axis=1) + _dot_tn(bg, xd[:, gsl])

    y = jnp.concatenate(ys, axis=1) * _silu(z)
    gdim = di // ngr
    outs = []
    for g in range(ngr):
        sl = slice(g * gdim, (g + 1) * gdim)
        yg = y[:, sl]
        outs.append(yg * lax.rsqrt(jnp.mean(yg * yg, axis=-1, keepdims=True) + RMS_EPS) * nw[:, sl])
    return outs


def ssd_call(zx, conv_w, conv_b, dt_bias, a_log, d_skip, norm_w, s0, conv_buf, bsz, seq_rows, t_valid, conv_done):
    hp, ns = SSD_HEAD_DIM, SSD_STATE
    conv_dim = conv_w.shape[1]
    di = zx.shape[1] - conv_dim - LANES
    nh = di // hp
    c_rows, spt, nc = _tile_layout(seq_rows)
    t_pad = nc * c_rows
    ub, uc = _units_per_step(bsz, nc, 4, 2)
    assert ub % spt == 0
    rows = (ub // spt) * uc * c_rows
    ncs = nc // uc
    has_s0 = s0 is not None
    assert not (conv_done and has_s0)
    pad_h = lambda a: jnp.pad(a.reshape(1, nh), ((0, 0), (0, LANES - nh)))
    full = lambda shape: pl.BlockSpec(shape, lambda b, c: (0,) * len(shape))
    sel = (jnp.arange(di)[None, :] // hp == jnp.arange(LANES)[:, None]).astype(bf16)
    expand = jnp.concatenate([sel, sel, sel], axis=0)
    in_specs = [pl.BlockSpec((rows, zx.shape[1]), lambda b, c: (b * ncs + c, 0)),
                full(conv_w.shape), full((1, conv_dim)), full((1, LANES)), full((1, LANES)),
                full((1, di)), full((1, di)), full(expand.shape)]
    args = [zx, conv_w, conv_b.reshape(1, conv_dim), pad_h(dt_bias), pad_h(a_log),
            jnp.repeat(d_skip, hp).reshape(1, di), norm_w.reshape(1, di), expand]
    if has_s0:
        in_specs += [pl.BlockSpec((ub, di, ns), lambda b, c: (b, 0, 0)),
                     pl.BlockSpec((ub,) + conv_buf.shape[1:], lambda b, c: (b, 0, 0))]
        args += [s0.reshape(bsz, di, ns), conv_buf]
    kern = functools.partial(_ssd_kernel, c_rows=c_rows, t_valid=t_valid, t_pad=t_pad, has_s0=has_s0, ub=ub, uc=uc,
                             spt=spt, conv_done=conv_done)
    y, s_new = pl.pallas_call(
        kern,
        grid=(bsz // ub, ncs),
        in_specs=in_specs,
        out_specs=[pl.BlockSpec((rows, di), lambda b, c: (b * ncs + c, 0)),
                   pl.BlockSpec((ub, di, ns), lambda b, c: (b, 0, 0))],
        out_shape=[jax.ShapeDtypeStruct((bsz * seq_rows, di), bf16),
                   jax.ShapeDtypeStruct((bsz, di, ns), f32)],
        scratch_shapes=[pltpu.VMEM((ub, ns, di), f32), pltpu.VMEM((ub, SUBLANES, conv_dim), f32)],
        compiler_params=_cparams(2),
        name="ssd",
    )(*args)
    return y, s_new.reshape(bsz, nh, hp, ns)


def _trunk(x, c_k, c_v, s_hg, s_ssm, s_sconv, s_fconv, prm):
    bsz, t, d = x.shape
    depth = prm["norm_mix_w"].shape[0]
    decode = c_k is not None
    tp = t if (t % SUBLANES == 0 or SUBLANES % t == 0) else -(-t // SUBLANES) * SUBLANES
    if tp != t:
        x = jnp.pad(x, ((0, 0), (0, tp - t), (0, 0)))
    m = bsz * tp
    n_heads = d // ATT_HEAD_DIM
    qd = n_heads * ATT_HEAD_DIM
    kvd = ATT_KV_HEADS * ATT_HEAD_DIM
    x2 = x.reshape(m, d)
    xn, xn_w = x2, prm["norm_mix_w"][0]
    nk, nv, nhg, nssm, nsconv, nfconv = [], [], [], [], [], []
    slot = [0] * N_MIXERS
    for i in range(depth):
        kind = i % N_MIXERS
        j = slot[kind]
        slot[kind] += 1
        nw_ffn = prm["norm_ffn_w"][i]
        if kind == 0:
            w_qkv, b_qkv = prm["attn_w_qkv"], prm["attn_b_qkv"][j]
            nq = w_qkv.shape[2]
            if decode:
                qkv = proj_call(xn, xn_w, w_qkv, j, b_qkv, f32, "attn_qkv")
                ck = jnp.transpose(c_k[j], (0, 2, 3, 1)).reshape(bsz, kvd, WINDOW)
                cv = jnp.transpose(c_v[j], (0, 2, 3, 1)).reshape(bsz, kvd, WINDOW)
                o = attn_cache_call(qkv, ck, cv, prm["attn_sinks"][j], n_heads, bsz, tp, t)
                qkv = qkv.reshape(bsz, tp, nq)
                to4 = lambda a: a.reshape(bsz, t, ATT_KV_HEADS, ATT_HEAD_DIM)
                k_new = to4(qkv[:, :t, qd:qd + kvd])
                v_new = to4(qkv[:, :t, qd + kvd:])
            else:
                qkv = proj_call(xn, xn_w, w_qkv, j, b_qkv, bf16, "attn_qkv")
                o = attn_prompt_call(qkv, prm["attn_sinks"][j], bsz, t, n_heads)
                xn_last = xn.reshape(bsz, t, d)[:, t - WINDOW:].reshape(bsz * WINDOW, d)
                kv_last = proj_call(xn_last, xn_w, w_qkv[:, :, qd:], j, b_qkv[qd:], f32,
                                    "attn_kv_tail").reshape(bsz, WINDOW, 2, kvd)
                to4 = lambda a: a.reshape(bsz, WINDOW, ATT_KV_HEADS, ATT_HEAD_DIM)
                k_new, v_new = to4(kv_last[:, :, 0]), to4(kv_last[:, :, 1])
            nk.append(k_new)
            nv.append(v_new)
            act, w_o, b_o = o, (prm["attn_w_o"], j), prm["attn_b_o"][j]
        elif kind == 1:
            p = proj_call(xn, xn_w, prm["hgrn_w_in"], j, None, f32, "hgrn_in")
            y, s_new = hgrn_call(p, prm["hgrn_lb_logits"], prm["hgrn_norm_w"][j],
                                 None if s_hg is None else s_hg[j], bsz, tp, t, i)
            nhg.append(s_new)
            act, w_o, b_o = y, (prm["hgrn_w_o"], j), None
        else:
            conv_dim = prm["ssd_conv_w"].shape[2]
            di = prm["ssd_w_o"].shape[1]
            kc = prm["ssd_conv_w"].shape[1]
            if s_sconv is None:
                zx, pre = ssd_in_call(xn, prm["ssd_w_in"], j, prm["ssd_conv_w"][j], prm["ssd_conv_b"][j], tp)
                pre = pre.reshape(bsz, tp // min(PROJ_TM, m), SUBLANES, conv_dim)[:, -1, SUBLANES - (kc - 1):]
            else:
                zx = proj_call(xn, xn_w, prm["ssd_w_in"], j, None, f32, "ssd_in")
                pre = zx.reshape(bsz, tp, zx.shape[1])[:, max(t - (kc - 1), 0):t, di:di + conv_dim]
                pre = jnp.concatenate([s_sconv[j], pre], axis=1)
                pre = pre[:, pre.shape[1] - (kc - 1):]
            nsconv.append(pre)
            y, s_new = ssd_call(zx, prm["ssd_conv_w"][j], prm["ssd_conv_b"][j], prm["ssd_dt_bias"][j],
                                prm["ssd_a_log"][j], prm["ssd_d"][j], prm["ssd_norm_w"][j],
                                None if s_ssm is None else s_ssm[j],
                                None if s_sconv is None else s_sconv[j], bsz, tp, t, s_sconv is None)
            nssm.append(s_new)
            act, w_o, b_o = y, (prm["ssd_w_o"], j), None
        last = i == depth - 1
        next_w = prm["norm_final_w"] if last else prm["norm_mix_w"][i + 1]
        dff = prm["ffn_w_down"].shape[1]
        kf = prm["ffn_conv_w"].shape[1]
        if s_fconv is not None:
            buf = s_fconv[i]
            e1 = jnp.pad(buf[:, 1:2], ((0, 0), (0, tp - 1), (0, 0))).reshape(m, dff)
            e2 = jnp.pad(buf, ((0, 0), (0, tp - 2), (0, 0))).reshape(m, dff)
        else:
            e1 = e2 = None
        x2, xn, tail = ffn_call(act, w_o, b_o, x2, nw_ffn, (prm["ffn_w_up"], i), prm["ffn_conv_w"][i],
                                prm["ffn_conv_b"][i], (prm["ffn_w_down"], i), next_w, tp, e1, e2,
                                f32 if last else bf16, "ffn")
        xn_w = None
        if s_fconv is not None:
            a_full = tail.reshape(bsz, tp, dff)[:, :t]
            nfconv.append(jnp.concatenate([s_fconv[i], a_full], axis=1)[:, t:])
        else:
            tiles_per_seq = tp // tail_tile_rows(m)
            tl = tail.reshape(bsz, tiles_per_seq, SUBLANES, dff)[:, -1]
            nfconv.append(tl[:, SUBLANES - (kf - 1):])
    stack = lambda xs: xs[0][None] if len(xs) == 1 else jnp.stack(xs)
    y = xn.reshape(bsz, tp, d)[:, :t]
    nk, nv = stack(nk), stack(nv)
    if decode:
        nk = jnp.concatenate([c_k[:, :, t:], nk], axis=2)
        nv = jnp.concatenate([c_v[:, :, t:], nv], axis=2)
    return (y, nk, nv, stack(nhg), stack(nssm), stack(nsconv), stack(nfconv))


def tail_tile_rows(m):
    return min(FFN_TM, m)


def kernel(x_prompt, x_sample, cache_attn_k, cache_attn_v, state_hgrn, state_ssm, state_ssm_conv, state_ffn_conv, norm_mix_w, norm_ffn_w, norm_final_w, attn_w_qkv, attn_b_qkv, attn_sinks, attn_w_o, attn_b_o, hgrn_w_in, hgrn_lb_logits, hgrn_norm_w, hgrn_w_o, ssd_w_in, ssd_conv_w, ssd_conv_b, ssd_dt_bias, ssd_a_log, ssd_d, ssd_norm_w, ssd_w_o, ffn_w_up, ffn_conv_w, ffn_conv_b, ffn_w_down):
    cast = lambda w: w.astype(bf16)
    ssd_in_pad = (-ssd_w_in.shape[2]) % LANES
    prm = dict(norm_mix_w=norm_mix_w, norm_ffn_w=norm_ffn_w, norm_final_w=norm_final_w,
               attn_w_qkv=cast(attn_w_qkv), attn_b_qkv=attn_b_qkv, attn_sinks=attn_sinks,
               attn_w_o=cast(attn_w_o), attn_b_o=attn_b_o,
               hgrn_w_in=cast(hgrn_w_in), hgrn_lb_logits=hgrn_lb_logits, hgrn_norm_w=hgrn_norm_w,
               hgrn_w_o=cast(hgrn_w_o),
               ssd_w_in=jnp.pad(cast(ssd_w_in), ((0, 0), (0, 0), (0, ssd_in_pad))), ssd_conv_w=ssd_conv_w,
               ssd_conv_b=ssd_conv_b, ssd_dt_bias=ssd_dt_bias,
               ssd_a_log=ssd_a_log, ssd_d=ssd_d, ssd_norm_w=ssd_norm_w, ssd_w_o=cast(ssd_w_o),
               ffn_w_up=cast(ffn_w_up), ffn_conv_w=ffn_conv_w, ffn_conv_b=ffn_conv_b, ffn_w_down=cast(ffn_w_down))
    outs_p = _trunk(x_prompt, None, None, None, None, None, None, prm)
    outs_s = _trunk(x_sample, cache_attn_k, cache_attn_v, state_hgrn, state_ssm, state_ssm_conv,
                    state_ffn_conv, prm)
    return (outs_p[0], outs_s[0]) + outs_p[1:] + outs_s[1:]
```

```python
import functools

import jax
import jax.numpy as jnp
from jax import lax
from jax.experimental import pallas as pl
from jax.experimental.pallas import tpu as pltpu

bf16 = jnp.bfloat16
f32 = jnp.float32

RMS_EPS = 1e-6
N_MIXERS = 3
WINDOW = 128
PAST_LEN = 8192
ATT_HEAD_DIM = 64
ATT_KV_HEADS = 4
HG_HEADS = 8
HG_CHUNK_GROUP = 32
SSD_HEAD_DIM = 64
SSD_STATE = 128
SSD_GROUPS = 4

LANES = 128
SUBLANES = 8
VMEM_LIMIT = 56 * 1024 * 1024
PROJ_TM = 512
FFN_TM = 512


def _cparams(n_axes):
    return pltpu.CompilerParams(dimension_semantics=("arbitrary",) * n_axes,
                                vmem_limit_bytes=VMEM_LIMIT)


def _dot(a, b):
    return jnp.dot(a.astype(bf16), b.astype(bf16), preferred_element_type=f32)


def _dot_nt(a, b):
    return lax.dot_general(a.astype(bf16), b.astype(bf16), (((1,), (1,)), ((), ())),
                           preferred_element_type=f32)


def _dot_tn(a, b):
    return lax.dot_general(a.astype(bf16), b.astype(bf16), (((0,), (0,)), ((), ())),
                           preferred_element_type=f32)


def _silu(x):
    return x * jax.nn.sigmoid(x)


def _rms(x, w):
    ms = jnp.mean(x * x, axis=-1, keepdims=True)
    return x * lax.rsqrt(ms + RMS_EPS) * w


def _cumsum_rows(x, tril):
    hi = x.astype(bf16)
    r1 = x - hi.astype(f32)
    mid = r1.astype(bf16)
    lo = (r1 - mid.astype(f32)).astype(bf16)
    d = functools.partial(jnp.dot, preferred_element_type=f32)
    return d(tril, hi) + d(tril, mid) + d(tril, lo)


def _tril_mask(c):
    r = lax.broadcasted_iota(jnp.int32, (c, c), 0)
    s = lax.broadcasted_iota(jnp.int32, (c, c), 1)
    return r >= s


def _rows_to_top(x, off):
    return x if off == 0 else pltpu.roll(x, x.shape[0] - off, 0)


def _rows_back(y, off):
    return y if off == 0 else pltpu.roll(y, off, 0)


def _tile_layout(seq_rows):
    if seq_rows % SUBLANES == 0:
        c_rows = min(128, seq_rows)
        return c_rows, 1, seq_rows // c_rows
    assert SUBLANES % seq_rows == 0
    return SUBLANES, SUBLANES // seq_rows, 1


def _resident(shape):
    return pl.BlockSpec(shape, lambda *_: (0,) * len(shape), pipeline_mode=pl.Buffered(1))


def _resident_layer(w, layer):
    return pl.BlockSpec((None,) + w.shape[1:], lambda *_: (layer, 0, 0), pipeline_mode=pl.Buffered(1))


def _proj_kernel(x_ref, nw_ref, w_ref, b_ref, o_ref):
    x = x_ref[...]
    if x.dtype != bf16:
        x = _rms(x, nw_ref[...]).astype(bf16)
    acc = jnp.dot(x, w_ref[...], preferred_element_type=f32)
    o_ref[...] = (acc + b_ref[...]).astype(o_ref.dtype)


def proj_call(x, norm_w, w, layer, bias, out_dtype, name):
    m, k = x.shape
    n = w.shape[2]
    tm = min(PROJ_TM, m)
    if bias is None:
        bias = jnp.zeros((n,), f32)
    if norm_w is None:
        assert x.dtype == bf16
        norm_w = jnp.ones((k,), f32)
    return pl.pallas_call(
        _proj_kernel,
        grid=(m // tm,),
        in_specs=[pl.BlockSpec((tm, k), lambda i: (i, 0)), _resident((1, k)), _resident_layer(w, layer),
                  _resident((1, n))],
        out_specs=pl.BlockSpec((tm, n), lambda i: (i, 0)),
        out_shape=jax.ShapeDtypeStruct((m, n), out_dtype),
        compiler_params=_cparams(1),
        name=name,
    )(x, norm_w.reshape(1, k), w, bias.reshape(1, n))


def _causal_conv(u, prev, cw, cb, rows8):
    kconv = cw.shape[0]
    yc = cb
    for kk in range(kconv - 1):
        back = kconv - 1 - kk
        sh = pltpu.roll(u, back, 0)
        top = sh[:SUBLANES]
        for r0 in range(back):
            top = jnp.where(rows8 == r0, prev[SUBLANES - back + r0:SUBLANES - back + r0 + 1, :], top)
        sh = jnp.concatenate([top, sh[SUBLANES:]], axis=0) if u.shape[0] > SUBLANES else top
        yc = yc + sh * cw[kk:kk + 1, :]
    return yc + u * cw[kconv - 1:kconv, :]


def _ssd_in_kernel(x_ref, w_ref, cw_ref, cb_ref, o_ref, tail_ref, carry_ref, *, tm, tc, t_seq, di):
    i = pl.program_id(0)
    cdim = cw_ref.shape[1]

    @pl.when(((i * tm) & (t_seq - 1)) == 0)
    def _():
        carry_ref[...] = jnp.zeros_like(carry_ref)

    x = x_ref[...]
    rows8 = lax.broadcasted_iota(jnp.int32, (SUBLANES, 1), 0)
    nz = di // tc
    for idx, c0 in enumerate(range(0, cdim, tc)):
        cs = slice(c0, c0 + tc)
        u = jnp.dot(x, w_ref[:, di + c0:di + c0 + tc], preferred_element_type=f32)
        tail_ref[0, :, cs] = u[tm - SUBLANES:, :]
        yc = _causal_conv(u, carry_ref[:, cs], cw_ref[:, cs], cb_ref[:, cs], rows8)
        carry_ref[:, cs] = u[tm - SUBLANES:, :]
        o_ref[:, di + c0:di + c0 + tc] = _silu(yc)
        if idx < nz:
            o_ref[:, idx * tc:(idx + 1) * tc] = jnp.dot(x, w_ref[:, idx * tc:(idx + 1) * tc],
                                                        preferred_element_type=f32)
    o_ref[:, di + cdim:] = jnp.dot(x, w_ref[:, di + cdim:], preferred_element_type=f32)


def ssd_in_call(xn, w, layer, conv_w, conv_b, t_seq):
    m, k = xn.shape
    n = w.shape[2]
    cdim = conv_w.shape[1]
    di = n - cdim - LANES
    tm = min(PROJ_TM, m)
    tc = 2 * LANES
    assert t_seq % tm == 0 and cdim % tc == 0 and di % tc == 0 and di <= cdim and (t_seq & (t_seq - 1)) == 0
    kern = functools.partial(_ssd_in_kernel, tm=tm, tc=tc, t_seq=t_seq, di=di)
    return pl.pallas_call(
        kern,
        grid=(m // tm,),
        in_specs=[pl.BlockSpec((tm, k), lambda i: (i, 0)), _resident_layer(w, layer),
                  _resident(conv_w.shape), _resident((1, cdim))],
        out_specs=[pl.BlockSpec((tm, n), lambda i: (i, 0)), pl.BlockSpec((1, SUBLANES, cdim), lambda i: (i, 0, 0))],
        out_shape=[jax.ShapeDtypeStruct((m, n), f32), jax.ShapeDtypeStruct((m // tm, SUBLANES, cdim), f32)],
        scratch_shapes=[pltpu.VMEM((SUBLANES, cdim), f32)],
        compiler_params=_cparams(1),
        name="ssd_in_conv",
    )(xn, w, conv_w, conv_b.reshape(1, cdim))


def _ffn_kernel(*refs, tm, tf, t_seq, tail, has_state):
    if has_state:
        (act_ref, wo_ref, bo_ref, x_ref, nw1_ref, wu_ref, cw_ref, cb_ref, wd_ref, nw2_ref, e1_ref, e2_ref,
         xo_ref, xn2_ref, tail_ref, carry_ref) = refs
    else:
        (act_ref, wo_ref, bo_ref, x_ref, nw1_ref, wu_ref, cw_ref, cb_ref, wd_ref, nw2_ref,
         xo_ref, xn2_ref, tail_ref, carry_ref) = refs
    i = pl.program_id(0)
    dff = wd_ref.shape[0]

    x1 = jnp.dot(act_ref[...].astype(bf16), wo_ref[...], preferred_element_type=f32) + bo_ref[...] + x_ref[...]
    xn = _rms(x1, nw1_ref[...]).astype(bf16)
    if has_state:
        tpos = (lax.broadcasted_iota(jnp.int32, (tm, 1), 0)) & (t_seq - 1)
    else:
        @pl.when(((i * tm) & (t_seq - 1)) == 0)
        def _():
            carry_ref[...] = jnp.zeros_like(carry_ref)
        rows8 = lax.broadcasted_iota(jnp.int32, (SUBLANES, 1), 0)
    gs = []
    for c0 in range(0, dff, tf):
        cs = slice(c0, c0 + tf)
        a = jnp.dot(xn, wu_ref[:, cs], preferred_element_type=f32)
        b = jnp.dot(xn, wu_ref[:, dff + c0:dff + c0 + tf], preferred_element_type=f32)
        tail_ref[0, :, cs] = a[tm - tail:, :]
        r1 = pltpu.roll(a, 1, 0)
        r2 = pltpu.roll(a, 2, 0)
        if has_state:
            p1 = jnp.where(tpos == 0, e1_ref[:, cs], r1)
            p2 = jnp.where(tpos < 2, e2_ref[:, cs], r2)
        else:
            carry = carry_ref[:, cs]
            top1 = jnp.where(rows8 == 0, carry[7:8, :], r1[:SUBLANES])
            top2 = jnp.where(rows8 == 0, carry[6:7, :], jnp.where(rows8 == 1, carry[7:8, :], r2[:SUBLANES]))
            p1 = jnp.concatenate([top1, r1[SUBLANES:]], axis=0)
            p2 = jnp.concatenate([top2, r2[SUBLANES:]], axis=0)
            carry_ref[:, cs] = a[tm - SUBLANES:, :]
        y = cb_ref[:, cs] + p2 * cw_ref[0:1, cs]
        y = y + p1 * cw_ref[1:2, cs]
        y = y + a * cw_ref[2:3, cs]
        gs.append((_silu(y) * b).astype(bf16))
    g = jnp.concatenate(gs, axis=1)
    xo = jnp.dot(g, wd_ref[...], preferred_element_type=f32) + x1
    xo_ref[...] = xo
    xn2_ref[...] = _rms(xo, nw2_ref[...]).astype(xn2_ref.dtype)


def ffn_call(act, w_o, b_o, x, ffn_norm_w, w_up, conv_w, conv_b, w_down, next_w, t_seq, e1, e2, out_dtype, name):
    m, d = x.shape
    ko = act.shape[1]
    dff = w_down[0].shape[1]
    tf = 2 * LANES
    assert dff % tf == 0 and (t_seq & (t_seq - 1)) == 0
    has_state = e1 is not None
    tm = min(FFN_TM // 2 if has_state else FFN_TM, m)
    assert (tm % t_seq == 0) if has_state else (t_seq % tm == 0)
    tail = tm if has_state else SUBLANES
    if b_o is None:
        b_o = jnp.zeros((d,), f32)
    row = lambda w: pl.BlockSpec((tm, w), lambda i: (i, 0))
    in_specs = [row(ko), _resident_layer(*w_o), _resident((1, d)), row(d), _resident((1, d)),
                _resident_layer(*w_up), _resident((3, dff)), _resident((1, dff)),
                _resident_layer(*w_down), _resident((1, d))]
    args = [act, w_o[0], b_o.reshape(1, d), x, ffn_norm_w.reshape(1, d),
            w_up[0], conv_w, conv_b.reshape(1, dff), w_down[0], next_w.reshape(1, d)]
    if has_state:
        in_specs += [row(dff), row(dff)]
        args += [e1, e2]
    kern = functools.partial(_ffn_kernel, tm=tm, tf=tf, t_seq=t_seq, tail=tail, has_state=has_state)
    return pl.pallas_call(
        kern,
        grid=(m // tm,),
        in_specs=in_specs,
        out_specs=[row(d), row(d), pl.BlockSpec((1, tail, dff), lambda i: (i, 0, 0))],
        out_shape=[jax.ShapeDtypeStruct((m, d), f32),
                   jax.ShapeDtypeStruct((m, d), out_dtype),
                   jax.ShapeDtypeStruct((m // tm, tail, dff), f32)],
        scratch_shapes=[pltpu.VMEM((SUBLANES, dff), f32)],
        compiler_params=_cparams(1),
        name=name,
    )(*args)


def _attn_softmax_pv(s, sink_col, vj):
    m = jnp.maximum(jnp.max(s, axis=-1, keepdims=True), sink_col)
    p = jnp.exp(s - m)
    l = jnp.sum(p, axis=-1, keepdims=True) + jnp.exp(sink_col - m)
    return _dot(p, vj) / l


def _sink_col(sinks_ref, j, group, rows_per_head):
    r = lax.broadcasted_iota(jnp.int32, (group * rows_per_head, 1), 0)
    col = jnp.full((group * rows_per_head, 1), sinks_ref[j * group], f32)
    for p in range(1, group):
        col = jnp.where(r >= p * rows_per_head, sinks_ref[j * group + p], col)
    return col


def _attn_prompt_kernel(sinks_ref, q_ref, kvc_ref, kvp_ref, o_ref, *, n_heads):
    n = pl.program_id(1)
    w, hd, kvh = WINDOW, ATT_HEAD_DIM, ATT_KV_HEADS
    group = n_heads // kvh
    kvw = kvh * hd
    assert group == kvh
    kv = jnp.concatenate([kvp_ref[...], kvc_ref[...]], axis=0)
    k2 = kv[:, :kvw]
    v2 = kv[:, kvw:]
    i = lax.broadcasted_iota(jnp.int32, (w, 2 * w), 0)
    c = lax.broadcasted_iota(jnp.int32, (w, 2 * w), 1)
    band = (c > i) & (c <= i + w) & ((n > 0) | (c >= w))
    mask = jnp.concatenate([band] * group, axis=0)
    slot = lax.broadcasted_iota(jnp.int32, (w, kvw), 1) >> (hd.bit_length() - 1)
    for j in range(kvh):
        qg = q_ref[:, j * kvw:(j + 1) * kvw] * (hd ** -0.5)
        qs = jnp.concatenate([jnp.where(slot == j, pltpu.roll(qg, ((j - p) % kvh) * hd, 1), 0)
                              for p in range(group)], axis=0)
        s = jnp.where(mask, _dot_nt(qs, k2), -jnp.inf)
        of = _attn_softmax_pv(s, _sink_col(sinks_ref, j, group, w), v2).astype(o_ref.dtype)
        og = of[0:w, :]
        for p in range(group):
            piece = of[p * w:(p + 1) * w, :]
            og = jnp.where(slot == p, pltpu.roll(piece, ((p - j) % kvh) * hd, 1), og)
        o_ref[:, j * kvw:(j + 1) * kvw] = og


def attn_prompt_call(qkv, sinks, bsz, t, n_heads):
    w, hd, kvh = WINDOW, ATT_HEAD_DIM, ATT_KV_HEADS
    nb = t // w
    qd = n_heads * hd
    kvd = 2 * kvh * hd
    assert qd % kvd == 0
    kvblk = qd // kvd
    kern = functools.partial(_attn_prompt_kernel, n_heads=n_heads)
    return pl.pallas_call(
        kern,
        grid=(bsz, nb),
        in_specs=[pl.BlockSpec(memory_space=pltpu.SMEM),
                  pl.BlockSpec((w, qd), lambda b, n: (b * nb + n, 0)),
                  pl.BlockSpec((w, kvd), lambda b, n: (b * nb + n, kvblk)),
                  pl.BlockSpec((w, kvd), lambda b, n: (b * nb + jnp.maximum(n - 1, 0), kvblk))],
        out_specs=pl.BlockSpec((w, qd), lambda b, n: (b * nb + n, 0)),
        out_shape=jax.ShapeDtypeStruct((bsz * t, qd), bf16),
        compiler_params=_cparams(2),
        name="attn_prompt",
    )(sinks, qkv, qkv, qkv)


def _attn_cache_kernel(sinks_ref, q_ref, ck_ref, cv_ref, o_ref, *, n_heads, t_valid, bt, spt):
    w, hd, kvh = WINDOW, ATT_HEAD_DIM, ATT_KV_HEADS
    group = n_heads // kvh
    tp = SUBLANES
    qd = n_heads * hd
    tq = lax.broadcasted_iota(jnp.int32, (tp, w), 0)
    jc = lax.broadcasted_iota(jnp.int32, (tp, w), 1)
    mc = (jc > tq) & (jc - w + PAST_LEN >= 0)
    tq2 = lax.broadcasted_iota(jnp.int32, (tp, tp), 0)
    un = lax.broadcasted_iota(jnp.int32, (tp, tp), 1)
    mn = (un <= tq2) & (un < t_valid)
    mask_c = jnp.concatenate([mc] * n_heads, axis=0)
    mask_n = jnp.concatenate([mn] * n_heads, axis=0)
    kvw = kvh * hd
    assert group == kvh
    slot = lax.broadcasted_iota(jnp.int32, (tp, kvw), 1) >> (hd.bit_length() - 1)
    sink = _sink_col(sinks_ref, 0, n_heads, tp)

    rows = lax.broadcasted_iota(jnp.int32, (tp, 1), 0)
    held = None
    for bi in range(bt):
        r0 = (bi // spt) * tp
        off = (bi % spt) * (tp // spt)
        row = _rows_to_top(q_ref[r0:r0 + tp, :], off)
        ck = ck_ref[bi]
        cv = cv_ref[bi]
        kn = row[:, qd:qd + kvw]
        vn = row[:, qd + kvw:]
        pieces = []
        for j in range(kvh):
            qg = row[:, j * kvw:(j + 1) * kvw] * (hd ** -0.5)
            for p in range(group):
                pieces.append(jnp.where(slot == j, pltpu.roll(qg, ((j - p) % kvh) * hd, 1), 0.0))
        qs = jnp.concatenate(pieces, axis=0)
        sc = jnp.where(mask_c, _dot(qs, ck), -jnp.inf)
        sn = jnp.where(mask_n, _dot_nt(qs, kn), -jnp.inf)
        m = jnp.maximum(jnp.maximum(jnp.max(sc, axis=-1, keepdims=True),
                                    jnp.max(sn, axis=-1, keepdims=True)), sink)
        pc = jnp.exp(sc - m)
        pn = jnp.exp(sn - m)
        l = jnp.sum(pc, axis=-1, keepdims=True) + jnp.sum(pn, axis=-1, keepdims=True) + jnp.exp(sink - m)
        of = (_dot_nt(pc, cv) + _dot(pn, vn)) / l
        ogs = []
        for j in range(kvh):
            og = jnp.zeros((tp, kvw), f32)
            for p in range(group):
                h = j * group + p
                og = og + jnp.where(slot == p, pltpu.roll(of[h * tp:(h + 1) * tp, :], ((p - j) % kvh) * hd, 1), 0.0)
            og = _rows_back(og, off)
            ogs.append(og if off == 0 else jnp.where(rows < off, held[j], og))
        if (bi + 1) % spt == 0:
            for j in range(kvh):
                o_ref[r0:r0 + tp, j * kvw:(j + 1) * kvw] = ogs[j]
        held = ogs


def attn_cache_call(qkv, cache_k, cache_v, sinks, n_heads, bsz, seq_rows, t_valid):
    nq = qkv.shape[1]
    w = WINDOW
    kd = cache_k.shape[1]
    qd = n_heads * ATT_HEAD_DIM
    bt = 8
    tile, spt, nc = _tile_layout(seq_rows)
    assert tile == SUBLANES and nc == 1 and bt % spt == 0
    rows = (bt // spt) * tile
    kern = functools.partial(_attn_cache_kernel, n_heads=n_heads, t_valid=t_valid, bt=bt, spt=spt)
    return pl.pallas_call(
        kern,
        grid=(bsz // bt,),
        in_specs=[pl.BlockSpec(memory_space=pltpu.SMEM),
                  pl.BlockSpec((rows, nq), lambda b: (b, 0)),
                  pl.BlockSpec((bt, kd, w), lambda b: (b, 0, 0)),
                  pl.BlockSpec((bt, kd, w), lambda b: (b, 0, 0))],
        out_specs=pl.BlockSpec((rows, qd), lambda b: (b, 0)),
        out_shape=jax.ShapeDtypeStruct((bsz * seq_rows, qd), f32),
        compiler_params=_cparams(1),
        name="attn_cache",
    )(sinks, qkv, cache_k, cache_v)


def _hgrn_chunk(x, st, lb, nw, tril, diag_mask, valid, grp):
    nh = HG_HEADS
    c_rows = x.shape[0]
    dim = x.shape[1] // 4
    dk = dim // nh
    ng = c_rows // grp
    if valid is not None:
        x = jnp.where(valid, x, 0.0)
    q = _silu(x[:, 0:dim])
    fg = lb + (1.0 - lb) * jax.nn.sigmoid(x[:, dim:2 * dim])
    logf = jnp.log(fg)
    k = 1.0 - fg
    v = x[:, 2 * dim:3 * dim]
    gate = x[:, 3 * dim:4 * dim]
    if valid is not None:
        logf = jnp.where(valid, logf, 0.0)
        k = jnp.where(valid, k, 0.0)
    bcum = _cumsum_rows(logf, tril.astype(bf16))

    def rows_of(vals):
        return jnp.concatenate([jnp.broadcast_to(r, (grp, dk)) for r in vals], axis=0)

    ys, st_out = [], []
    for h in range(nh):
        sl = slice(h * dk, (h + 1) * dk)
        b = bcum[:, sl]
        qh, kh, vh = q[:, sl], k[:, sl], v[:, sl]
        zero = jnp.zeros((1, dk), f32)
        r = [zero] + [b[i * grp - 1:i * grp, :] for i in range(1, ng)] + [b[c_rows - 1:c_rows, :]]
        mid = [b[i * grp + grp // 2 - 1:i * grp + grp // 2, :] for i in range(ng)]
        b_last = r[ng]
        r_start = rows_of(r[:ng])
        r_end = rows_of(r[1:])
        r_mid = rows_of(mid)
        qg = qh * jnp.exp(b - r_start)
        kt = kh * jnp.exp(r_end - b)
        qm = qh * jnp.exp(b - r_mid)
        km = kh * jnp.exp(r_mid - b)
        a = jnp.where(diag_mask, _dot_nt(qm, km), 0.0)
        if ng > 1:
            lhs, rhs = [], []
            for j in range(ng - 1):
                lhs.append(jnp.concatenate(
                    [jnp.zeros((grp, dk), f32) if i <= j else qg[i * grp:(i + 1) * grp, :] * jnp.exp(r[i] - r[j + 1])
                     for i in range(ng)], axis=0))
                rhs.append(jnp.concatenate(
                    [kt[i * grp:(i + 1) * grp, :] if i == j else jnp.zeros((grp, dk), f32) for i in range(ng)], axis=0))
            a = a + _dot_nt(jnp.concatenate(lhs, axis=1), jnp.concatenate(rhs, axis=1))
        q_inter = qg * jnp.exp(r_start)
        o = _dot(a, vh) + _dot_nt(q_inter, st[h])
        k_dec = kt * jnp.exp(b_last - r_end)
        st_out.append(st[h] * jnp.exp(b_last) + _dot_tn(vh, k_dec))
        on = o * lax.rsqrt(jnp.mean(o * o, axis=-1, keepdims=True) + RMS_EPS)
        ys.append(on * nw[:, sl] * _silu(gate[:, sl]))
    return ys, st_out


def _hgrn_kernel(*refs, c_rows, grp, t_valid, t_pad, layer, has_s0, ub, uc, spt):
    if has_s0:
        p_ref, lbl_ref, nw_ref, s0_ref, y_ref, so_ref, st_ref = refs
    else:
        p_ref, lbl_ref, nw_ref, y_ref, so_ref, st_ref = refs
    c = pl.program_id(1)
    nc = pl.num_programs(1)
    nh = HG_HEADS
    dim = p_ref.shape[1] // 4
    dk = dim // nh

    @pl.when(c == 0)
    def _():
        if has_s0:
            for s in range(ub):
                for h in range(nh):
                    st_ref[s, h] = s0_ref[s, h].T
        else:
            st_ref[...] = jnp.zeros_like(st_ref)

    lbl = lbl_ref[...]
    e = jnp.exp(lbl - jnp.max(lbl, axis=0, keepdims=True))
    sm = e / jnp.sum(e, axis=0, keepdims=True)
    lb = jnp.zeros((1, dim), f32)
    for r in range(1, layer + 1):
        lb = lb + sm[r:r + 1, :]

    tril = _tril_mask(c_rows)
    rr = lax.broadcasted_iota(jnp.int32, (c_rows, c_rows), 0)
    ss = lax.broadcasted_iota(jnp.int32, (c_rows, c_rows), 1)
    gshift = grp.bit_length() - 1
    diag_mask = tril & ((rr >> gshift) == (ss >> gshift))
    nw = nw_ref[...]
    rows = lax.broadcasted_iota(jnp.int32, (c_rows, 1), 0)

    held = None
    for s in range(ub):
        st = [st_ref[s, h] for h in range(nh)]
        off = (s % spt) * (c_rows // spt)
        for cc in range(uc):
            r0 = ((s // spt) * uc + cc) * c_rows
            valid = ((c * uc + cc) * c_rows + rows) < t_valid if t_valid < t_pad else None
            ys, st = _hgrn_chunk(_rows_to_top(p_ref[r0:r0 + c_rows, :], off), st, lb, nw, tril, diag_mask, valid, grp)
            ys = [_rows_back(y, off) for y in ys]
            if off:
                ys = [jnp.where(rows < off, y0, y) for y0, y in zip(held, ys)]
            if (s + 1) % spt == 0:
                for h in range(nh):
                    y_ref[r0:r0 + c_rows, h * dk:(h + 1) * dk] = ys[h].astype(y_ref.dtype)
            held = ys
        for h in range(nh):
            st_ref[s, h] = st[h]

    @pl.when(c == nc - 1)
    def _():
        for s in range(ub):
            for h in range(nh):
                so_ref[s, h] = st_ref[s, h].T


def _units_per_step(bsz, nc, max_seqs, max_chunks):
    if nc == 1:
        return max(u for u in (1, 2, 4, 8) if u <= max_seqs and bsz % u == 0), 1
    return 1, max(u for u in (1, 2, 4, 8) if u <= max_chunks and nc % u == 0)


def hgrn_call(p, lb_logits, norm_w, s0, bsz, seq_rows, t_valid, layer):
    dim = p.shape[1] // 4
    nh = HG_HEADS
    dk = dim // nh
    c_rows, spt, nc = _tile_layout(seq_rows)
    t_pad = nc * c_rows
    grp = min(HG_CHUNK_GROUP, c_rows)
    ub, uc = _units_per_step(bsz, nc, 8, 4)
    assert ub % spt == 0
    rows = (ub // spt) * uc * c_rows
    ncs = nc // uc
    has_s0 = s0 is not None
    in_specs = [pl.BlockSpec((rows, 4 * dim), lambda b, c: (b * ncs + c, 0)),
                pl.BlockSpec(lb_logits.shape, lambda b, c: (0, 0)),
                pl.BlockSpec((1, dim), lambda b, c: (0, 0))]
    args = [p, lb_logits, norm_w.reshape(1, dim)]
    if has_s0:
        in_specs.append(pl.BlockSpec((ub, nh, dk, dk), lambda b, c: (b, 0, 0, 0)))
        args.append(s0)
    kern = functools.partial(_hgrn_kernel, c_rows=c_rows, grp=grp, t_valid=t_valid, t_pad=t_pad,
                             layer=layer, has_s0=has_s0, ub=ub, uc=uc, spt=spt)
    return pl.pallas_call(
        kern,
        grid=(bsz // ub, ncs),
        in_specs=in_specs,
        out_specs=[pl.BlockSpec((rows, dim), lambda b, c: (b * ncs + c, 0)),
                   pl.BlockSpec((ub, nh, dk, dk), lambda b, c: (b, 0, 0, 0))],
        out_shape=[jax.ShapeDtypeStruct((bsz * seq_rows, dim), bf16),
                   jax.ShapeDtypeStruct((bsz, nh, dk, dk), f32)],
        scratch_shapes=[pltpu.VMEM((ub, nh, dk, dk), f32)],
        compiler_params=_cparams(2),
        name="hgrn",
    )(*args)


def _expand_heads(w, ex_ref):
    hi = w.astype(bf16)
    r1 = w - hi.astype(f32)
    mid = r1.astype(bf16)
    lo = (r1 - mid.astype(f32)).astype(bf16)
    return jnp.dot(jnp.concatenate([hi, mid, lo], axis=1), ex_ref[...], preferred_element_type=f32)


def _ssd_kernel(*refs, c_rows, t_valid, t_pad, has_s0, ub, uc, spt, conv_done):
    if has_s0:
        (zx_ref, cw_ref, cb_ref, dtb_ref, alog_ref, dsk_ref, nw_ref, ex_ref, s0_ref, buf_ref,
         y_ref, so_ref, st_ref, carry_ref) = refs
    else:
        (zx_ref, cw_ref, cb_ref, dtb_ref, alog_ref, dsk_ref, nw_ref, ex_ref,
         y_ref, so_ref, st_ref, carry_ref) = refs
    c = pl.program_id(1)
    nc = pl.num_programs(1)
    hp, ns, ngr = SSD_HEAD_DIM, SSD_STATE, SSD_GROUPS
    di = y_ref.shape[1]
    nh = di // hp
    hpg = nh // ngr
    gw = ngr * ns
    kconv = cw_ref.shape[0]
    pair = 2 * hp
    assert pair == LANES and hpg % 2 == 0

    @pl.when(c == 0)
    def _():
        carry_ref[...] = jnp.zeros_like(carry_ref)
        if has_s0:
            for s in range(ub):
                for r in range(di // LANES):
                    st_ref[s, :, r * LANES:(r + 1) * LANES] = s0_ref[s, r * LANES:(r + 1) * LANES, :].T
                carry_ref[s, SUBLANES - (kconv - 1):, :] = buf_ref[s]
        else:
            st_ref[...] = jnp.zeros_like(st_ref)

    cdim = cw_ref.shape[1]
    rows = lax.broadcasted_iota(jnp.int32, (c_rows, 1), 0)
    rows8 = rows[:SUBLANES]
    cw = cw_ref[...]
    a_neg = -jnp.exp(alog_ref[...])
    tril = _tril_mask(c_rows)
    dsk = dsk_ref[...]
    nw = nw_ref[...]
    lo = lax.broadcasted_iota(jnp.int32, (c_rows, LANES), 1) < hp
    held = None
    for s in range(ub):
        off = (s % spt) * (c_rows // spt)
        for cc in range(uc):
            r0 = ((s // spt) * uc + cc) * c_rows
            valid = ((c * uc + cc) * c_rows + rows) < t_valid if t_valid < t_pad else None
            ys = _ssd_chunk(_rows_to_top(zx_ref[r0:r0 + c_rows, :], off), st_ref.at[s], carry_ref.at[s], di,
                            cw, cb_ref[...], dtb_ref[...], a_neg, dsk, nw, ex_ref, tril, lo, rows8, valid, conv_done)
            ys = [_rows_back(y, off) for y in ys]
            if off:
                ys = [jnp.where(rows < off, y0, y) for y0, y in zip(held, ys)]
            if (s + 1) % spt == 0:
                gdim = di // len(ys)
                for g, y in enumerate(ys):
                    y_ref[r0:r0 + c_rows, g * gdim:(g + 1) * gdim] = y.astype(y_ref.dtype)
            held = ys

    @pl.when(c == nc - 1)
    def _():
        for s in range(ub):
            for r in range(di // LANES):
                so_ref[s, r * LANES:(r + 1) * LANES, :] = st_ref[s, :, r * LANES:(r + 1) * LANES].T


def _ssd_chunk(zx, st_ref, carry_ref, di, cw, cb, dtb, a_neg, dsk, nw, ex_ref, tril, lo, rows8, valid, conv_done):
    hp, ns, ngr = SSD_HEAD_DIM, SSD_STATE, SSD_GROUPS
    c_rows = zx.shape[0]
    nh = di // hp
    hpg = nh // ngr
    gw = ngr * ns
    kconv, cdim = cw.shape
    pair = 2 * hp
    z = zx[:, :di]
    u = zx[:, di:di + cdim]
    if conv_done:
        xbc = u
    else:
        xbc = _silu(_causal_conv(u, carry_ref[...], cw, cb, rows8))
        carry_ref[...] = u[c_rows - SUBLANES:, :]
    dt = jax.nn.softplus(zx[:, di + cdim:] + dtb)
    if valid is not None:
        xbc = jnp.where(valid, xbc, 0.0)
        dt = jnp.where(valid, dt, 0.0)
    xs = xbc[:, :di]
    bm = xbc[:, di:di + gw]
    cm = xbc[:, di + gw:di + 2 * gw]
    acum = _cumsum_rows(dt * a_neg, tril.astype(bf16))

    def head_rows(a):
        if c_rows < LANES:
            a = jnp.concatenate([a, jnp.zeros((LANES - c_rows, LANES), f32)], axis=0)
        return a.T[:, :c_rows]

    acum_t = head_rows(acum)
    dt_t = head_rows(dt)
    a_last = acum[c_rows - 1:c_rows, :]
    xd = xs * _expand_heads(dt * jnp.exp(a_last - acum), ex_ref)

    ys = []
    for g in range(ngr):
        bg = bm[:, g * ns:(g + 1) * ns]
        cg = cm[:, g * ns:(g + 1) * ns]
        cb_mat = _dot_nt(cg, bg)
        gsl = slice(g * hpg * hp, (g + 1) * hpg * hp)
        y_inter = _dot(cg, st_ref[:, gsl])
        decs = []
        for q in range(hpg // 2):
            h1 = g * hpg + 2 * q
            psl = slice(h1 * hp, h1 * hp + pair)
            ms, es = [], []
            for h in (h1, h1 + 1):
                colb = jnp.broadcast_to(acum[:, h:h + 1], (c_rows, LANES))
                seg = colb[:, :c_rows] - acum_t[h:h + 1, :]
                ms.append(cb_mat * jnp.exp(jnp.where(tril, seg, -jnp.inf)) * dt_t[h:h + 1, :])
                es.append(jnp.exp(colb))
            xp = xs[:, psl]
            x_lo = jnp.where(lo, xp, 0.0)
            x_hi = jnp.where(lo, 0.0, xp)
            if (2 * c_rows) % LANES == 0:
                y_intra = _dot(jnp.concatenate(ms, axis=1), jnp.concatenate([x_lo, x_hi], axis=0))
            else:
                y_intra = _dot(ms[0], x_lo) + _dot(ms[1], x_hi)
            y_pair = y_intra + y_inter[:, q * pair:(q + 1) * pair] * jnp.where(lo, es[0], es[1])
            ys.append(y_pair + dsk[:, psl] * xp)
            decs.append(jnp.where(lo[0:1, :], jnp.exp(a_last[:, h1:h1 + 1]), jnp.exp(a_last[:, h1 + 1:h1 + 2])))
        st_ref[:, gsl] = st_ref[:, gsl] * jnp.concatenate(decs, axis=1) + _dot_tn(bg, xd[:, gsl])

    y = jnp.concatenate(ys, axis=1) * _silu(z)
    gdim = di // ngr
    outs = []
    for g in range(ngr):
        sl = slice(g * gdim, (g + 1) * gdim)
        yg = y[:, sl]
        outs.append(yg * lax.rsqrt(jnp.mean(yg * yg, axis=-1, keepdims=True) + RMS_EPS) * nw[:, sl])
    return outs


def ssd_call(zx, conv_w, conv_b, dt_bias, a_log, d_skip, norm_w, s0, conv_buf, bsz, seq_rows, t_valid, conv_done):
    hp, ns = SSD_HEAD_DIM, SSD_STATE
    conv_dim = conv_w.shape[1]
    di = zx.shape[1] - conv_dim - LANES
    nh = di // hp
    c_rows, spt, nc = _tile_layout(seq_rows)
    t_pad = nc * c_rows
    ub, uc = _units_per_step(bsz, nc, 4, 2)
    assert ub % spt == 0
    rows = (ub // spt) * uc * c_rows
    ncs = nc // uc
    has_s0 = s0 is not None
    assert not (conv_done and has_s0)
    pad_h = lambda a: jnp.pad(a.reshape(1, nh), ((0, 0), (0, LANES - nh)))
    full = lambda shape: pl.BlockSpec(shape, lambda b, c: (0,) * len(shape))
    sel = (jnp.arange(di)[None, :] // hp == jnp.arange(LANES)[:, None]).astype(bf16)
    expand = jnp.concatenate([sel, sel, sel], axis=0)
    in_specs = [pl.BlockSpec((rows, zx.shape[1]), lambda b, c: (b * ncs + c, 0)),
                full(conv_w.shape), full((1, conv_dim)), full((1, LANES)), full((1, LANES)),
                full((1, di)), full((1, di)), full(expand.shape)]
    args = [zx, conv_w, conv_b.reshape(1, conv_dim), pad_h(dt_bias), pad_h(a_log),
            jnp.repeat(d_skip, hp).reshape(1, di), norm_w.reshape(1, di), expand]
    if has_s0:
        in_specs += [pl.BlockSpec((ub, di, ns), lambda b, c: (b, 0, 0)),
                     pl.BlockSpec((ub,) + conv_buf.shape[1:], lambda b, c: (b, 0, 0))]
        args += [s0.reshape(bsz, di, ns), conv_buf]
    kern = functools.partial(_ssd_kernel, c_rows=c_rows, t_valid=t_valid, t_pad=t_pad, has_s0=has_s0, ub=ub, uc=uc,
                             spt=spt, conv_done=conv_done)
    y, s_new = pl.pallas_call(
        kern,
        grid=(bsz // ub, ncs),
        in_specs=in_specs,
        out_specs=[pl.BlockSpec((rows, di), lambda b, c: (b * ncs + c, 0)),
                   pl.BlockSpec((ub, di, ns), lambda b, c: (b, 0, 0))],
        out_shape=[jax.ShapeDtypeStruct((bsz * seq_rows, di), bf16),
                   jax.ShapeDtypeStruct((bsz, di, ns), f32)],
        scratch_shapes=[pltpu.VMEM((ub, ns, di), f32), pltpu.VMEM((ub, SUBLANES, conv_dim), f32)],
        compiler_params=_cparams(2),
        name="ssd",
    )(*args)
    return y, s_new.reshape(bsz, nh, hp, ns)


def _seq_rows(a, tp, t):
    return [a[r::tp] for r in range(t)]


def _spread_rows(rows, tp, r):
    return lax.pad(rows, jnp.zeros((), rows.dtype), ((r, tp - 1 - r, tp - 1), (0, 0, 0)))


def _trunk(x, c_k, c_v, s_hg, s_ssm, s_sconv, s_fconv, prm):
    bsz, t, d = x.shape
    depth = prm["norm_mix_w"].shape[0]
    decode = c_k is not None
    tp = t if (t % SUBLANES == 0 or SUBLANES % t == 0) else -(-t // SUBLANES) * SUBLANES
    if tp != t:
        x = jnp.pad(x, ((0, 0), (0, tp - t), (0, 0)))
    m = bsz * tp
    n_heads = d // ATT_HEAD_DIM
    qd = n_heads * ATT_HEAD_DIM
    kvd = ATT_KV_HEADS * ATT_HEAD_DIM
    x2 = x.reshape(m, d)
    xn, xn_w = x2, prm["norm_mix_w"][0]
    nk, nv, nhg, nssm, nsconv, nfconv = [], [], [], [], [], []
    slot = [0] * N_MIXERS
    for i in range(depth):
        kind = i % N_MIXERS
        j = slot[kind]
        slot[kind] += 1
        nw_ffn = prm["norm_ffn_w"][i]
        if kind == 0:
            w_qkv, b_qkv = prm["attn_w_qkv"], prm["attn_b_qkv"][j]
            nq = w_qkv.shape[2]
            if decode:
                qkv = proj_call(xn, xn_w, w_qkv, j, b_qkv, f32, "attn_qkv")
                ck = jnp.transpose(c_k[j], (0, 2, 3, 1)).reshape(bsz, kvd, WINDOW)
                cv = jnp.transpose(c_v[j], (0, 2, 3, 1)).reshape(bsz, kvd, WINDOW)
                o = attn_cache_call(qkv, ck, cv, prm["attn_sinks"][j], n_heads, bsz, tp, t)
                to4 = lambda rows: jnp.stack(rows, axis=1).reshape(bsz, t, ATT_KV_HEADS, ATT_HEAD_DIM)
                k_new = to4(_seq_rows(qkv[:, qd:qd + kvd], tp, t))
                v_new = to4(_seq_rows(qkv[:, qd + kvd:], tp, t))
            else:
                qkv = proj_call(xn, xn_w, w_qkv, j, b_qkv, bf16, "attn_qkv")
                o = attn_prompt_call(qkv, prm["attn_sinks"][j], bsz, t, n_heads)
                xn_last = xn.reshape(bsz, t, d)[:, t - WINDOW:].reshape(bsz * WINDOW, d)
                kv_last = proj_call(xn_last, xn_w, w_qkv[:, :, qd:], j, b_qkv[qd:], f32,
                                    "attn_kv_tail").reshape(bsz, WINDOW, 2, kvd)
                to4 = lambda a: a.reshape(bsz, WINDOW, ATT_KV_HEADS, ATT_HEAD_DIM)
                k_new, v_new = to4(kv_last[:, :, 0]), to4(kv_last[:, :, 1])
            nk.append(k_new)
            nv.append(v_new)
            act, w_o, b_o = o, (prm["attn_w_o"], j), prm["attn_b_o"][j]
        elif kind == 1:
            p = proj_call(xn, xn_w, prm["hgrn_w_in"], j, None, f32, "hgrn_in")
            y, s_new = hgrn_call(p, prm["hgrn_lb_logits"], prm["hgrn_norm_w"][j],
                                 None if s_hg is None else s_hg[j], bsz, tp, t, i)
            nhg.append(s_new)
            act, w_o, b_o = y, (prm["hgrn_w_o"], j), None
        else:
            conv_dim = prm["ssd_conv_w"].shape[2]
            di = prm["ssd_w_o"].shape[1]
            kc = prm["ssd_conv_w"].shape[1]
            if s_sconv is None:
                zx, pre = ssd_in_call(xn, prm["ssd_w_in"], j, prm["ssd_conv_w"][j], prm["ssd_conv_b"][j], tp)
                pre = pre.reshape(bsz, tp // min(PROJ_TM, m), SUBLANES, conv_dim)[:, -1, SUBLANES - (kc - 1):]
            else:
                zx = proj_call(xn, xn_w, prm["ssd_w_in"], j, None, f32, "ssd_in")
                rows = [s_sconv[j][:, r] for r in range(kc - 1)] + _seq_rows(zx[:, di:di + conv_dim], tp, t)
                pre = jnp.stack(rows[len(rows) - (kc - 1):], axis=1)
            nsconv.append(pre)
            y, s_new = ssd_call(zx, prm["ssd_conv_w"][j], prm["ssd_conv_b"][j], prm["ssd_dt_bias"][j],
                                prm["ssd_a_log"][j], prm["ssd_d"][j], prm["ssd_norm_w"][j],
                                None if s_ssm is None else s_ssm[j],
                                None if s_sconv is None else s_sconv[j], bsz, tp, t, s_sconv is None)
            nssm.append(s_new)
            act, w_o, b_o = y, (prm["ssd_w_o"], j), None
        last = i == depth - 1
        next_w = prm["norm_final_w"] if last else prm["norm_mix_w"][i + 1]
        dff = prm["ffn_w_down"].shape[1]
        kf = prm["ffn_conv_w"].shape[1]
        if s_fconv is not None:
            buf = s_fconv[i]
            e1 = _spread_rows(buf[:, 1], tp, 0)
            e2 = _spread_rows(buf[:, 0], tp, 0) + _spread_rows(buf[:, 1], tp, 1)
        else:
            e1 = e2 = None
        x2, xn, tail = ffn_call(act, w_o, b_o, x2, nw_ffn, (prm["ffn_w_up"], i), prm["ffn_conv_w"][i],
                                prm["ffn_conv_b"][i], (prm["ffn_w_down"], i), next_w, tp, e1, e2,
                                f32 if last else bf16, "ffn")
        xn_w = None
        if s_fconv is not None:
            rows = [s_fconv[i][:, r] for r in range(kf - 1)] + _seq_rows(tail.reshape(m, dff), tp, t)
            nfconv.append(jnp.stack(rows[len(rows) - (kf - 1):], axis=1))
        else:
            tiles_per_seq = tp // tail_tile_rows(m)
            tl = tail.reshape(bsz, tiles_per_seq, SUBLANES, dff)[:, -1]
            nfconv.append(tl[:, SUBLANES - (kf - 1):])
    stack = lambda xs: xs[0][None] if len(xs) == 1 else jnp.stack(xs)
    y = xn.reshape(bsz, tp, d)[:, :t]
    nk, nv = stack(nk), stack(nv)
    if decode:
        nk = jnp.concatenate([c_k[:, :, t:], nk], axis=2)
        nv = jnp.concatenate([c_v[:, :, t:], nv], axis=2)
    return (y, nk, nv, stack(nhg), stack(nssm), stack(nsconv), stack(nfconv))


def tail_tile_rows(m):
    return min(FFN_TM, m)


def kernel(x_prompt, x_sample, cache_attn_k, cache_attn_v, state_hgrn, state_ssm, state_ssm_conv, state_ffn_conv, norm_mix_w, norm_ffn_w, norm_final_w, attn_w_qkv, attn_b_qkv, attn_sinks, attn_w_o, attn_b_o, hgrn_w_in, hgrn_lb_logits, hgrn_norm_w, hgrn_w_o, ssd_w_in, ssd_conv_w, ssd_conv_b, ssd_dt_bias, ssd_a_log, ssd_d, ssd_norm_w, ssd_w_o, ffn_w_up, ffn_conv_w, ffn_conv_b, ffn_w_down):
    cast = lambda w: w.astype(bf16)
    ssd_in_pad = (-ssd_w_in.shape[2]) % LANES
    prm = dict(norm_mix_w=norm_mix_w, norm_ffn_w=norm_ffn_w, norm_final_w=norm_final_w,
               attn_w_qkv=cast(attn_w_qkv), attn_b_qkv=attn_b_qkv, attn_sinks=attn_sinks,
               attn_w_o=cast(attn_w_o), attn_b_o=attn_b_o,
               hgrn_w_in=cast(hgrn_w_in), hgrn_lb_logits=hgrn_lb_logits, hgrn_norm_w=hgrn_norm_w,
               hgrn_w_o=cast(hgrn_w_o),
               ssd_w_in=jnp.pad(cast(ssd_w_in), ((0, 0), (0, 0), (0, ssd_in_pad))), ssd_conv_w=ssd_conv_w,
               ssd_conv_b=ssd_conv_b, ssd_dt_bias=ssd_dt_bias,
               ssd_a_log=ssd_a_log, ssd_d=ssd_d, ssd_norm_w=ssd_norm_w, ssd_w_o=cast(ssd_w_o),
               ffn_w_up=cast(ffn_w_up), ffn_conv_w=ffn_conv_w, ffn_conv_b=ffn_conv_b, ffn_w_down=cast(ffn_w_down))
    outs_p = _trunk(x_prompt, None, None, None, None, None, None, prm)
    outs_s = _trunk(x_sample, cache_attn_k, cache_attn_v, state_hgrn, state_ssm, state_ssm_conv,
                    state_ffn_conv, prm)
    return (outs_p[0], outs_s[0]) + outs_p[1:] + outs_s[1:]
```

```python
import functools

import jax
import jax.numpy as jnp
from jax import lax
from jax.experimental import pallas as pl
from jax.experimental.pallas import tpu as pltpu

bf16 = jnp.bfloat16
f32 = jnp.float32

RMS_EPS = 1e-6
N_MIXERS = 3
WINDOW = 128
PAST_LEN = 8192
ATT_HEAD_DIM = 64
ATT_KV_HEADS = 4
HG_HEADS = 8
HG_CHUNK_GROUP = 32
SSD_HEAD_DIM = 64
SSD_STATE = 128
SSD_GROUPS = 4

LANES = 128
SUBLANES = 8
VMEM_LIMIT = 56 * 1024 * 1024
PROJ_TM = 512
FFN_TM = 512


def _cparams(n_axes):
    return pltpu.CompilerParams(dimension_semantics=("arbitrary",) * n_axes,
                                vmem_limit_bytes=VMEM_LIMIT)


def _dot(a, b):
    return jnp.dot(a.astype(bf16), b.astype(bf16), preferred_element_type=f32)


def _dot_nt(a, b):
    return lax.dot_general(a.astype(bf16), b.astype(bf16), (((1,), (1,)), ((), ())),
                           preferred_element_type=f32)


def _dot_tn(a, b):
    return lax.dot_general(a.astype(bf16), b.astype(bf16), (((0,), (0,)), ((), ())),
                           preferred_element_type=f32)


def _silu(x):
    return x * jax.nn.sigmoid(x)


def _rms(x, w):
    ms = jnp.mean(x * x, axis=-1, keepdims=True)
    return x * lax.rsqrt(ms + RMS_EPS) * w


def _cumsum_rows(x, tril):
    hi = x.astype(bf16)
    r1 = x - hi.astype(f32)
    mid = r1.astype(bf16)
    lo = (r1 - mid.astype(f32)).astype(bf16)
    d = functools.partial(jnp.dot, preferred_element_type=f32)
    return d(tril, hi) + d(tril, mid) + d(tril, lo)


def _tril_mask(c):
    r = lax.broadcasted_iota(jnp.int32, (c, c), 0)
    s = lax.broadcasted_iota(jnp.int32, (c, c), 1)
    return r >= s


def _resident(shape):
    return pl.BlockSpec(shape, lambda *_: (0,) * len(shape), pipeline_mode=pl.Buffered(1))


def _resident_layer(w, layer):
    return pl.BlockSpec((None,) + w.shape[1:], lambda *_: (layer, 0, 0), pipeline_mode=pl.Buffered(1))


def _proj_kernel(x_ref, nw_ref, w_ref, b_ref, o_ref):
    x = x_ref[...]
    if x.dtype != bf16:
        x = _rms(x, nw_ref[...]).astype(bf16)
    acc = jnp.dot(x, w_ref[...], preferred_element_type=f32)
    o_ref[...] = (acc + b_ref[...]).astype(o_ref.dtype)


def proj_call(x, norm_w, w, layer, bias, out_dtype, name):
    m, k = x.shape
    n = w.shape[2]
    tm = min(PROJ_TM, m)
    if bias is None:
        bias = jnp.zeros((n,), f32)
    if norm_w is None:
        assert x.dtype == bf16
        norm_w = jnp.ones((k,), f32)
    return pl.pallas_call(
        _proj_kernel,
        grid=(m // tm,),
        in_specs=[pl.BlockSpec((tm, k), lambda i: (i, 0)), _resident((1, k)), _resident_layer(w, layer),
                  _resident((1, n))],
        out_specs=pl.BlockSpec((tm, n), lambda i: (i, 0)),
        out_shape=jax.ShapeDtypeStruct((m, n), out_dtype),
        compiler_params=_cparams(1),
        name=name,
    )(x, norm_w.reshape(1, k), w, bias.reshape(1, n))


def _causal_conv(u, prev, cw, cb, rows8):
    kconv = cw.shape[0]
    yc = cb
    for kk in range(kconv - 1):
        back = kconv - 1 - kk
        sh = pltpu.roll(u, back, 0)
        top = sh[:SUBLANES]
        for r0 in range(back):
            top = jnp.where(rows8 == r0, prev[SUBLANES - back + r0:SUBLANES - back + r0 + 1, :], top)
        sh = jnp.concatenate([top, sh[SUBLANES:]], axis=0) if u.shape[0] > SUBLANES else top
        yc = yc + sh * cw[kk:kk + 1, :]
    return yc + u * cw[kconv - 1:kconv, :]


def _ssd_in_kernel(x_ref, w_ref, cw_ref, cb_ref, o_ref, tail_ref, carry_ref, *, tm, tc, t_seq, di):
    i = pl.program_id(0)
    cdim = cw_ref.shape[1]

    @pl.when(((i * tm) & (t_seq - 1)) == 0)
    def _():
        carry_ref[...] = jnp.zeros_like(carry_ref)

    x = x_ref[...]
    rows8 = lax.broadcasted_iota(jnp.int32, (SUBLANES, 1), 0)
    nz = di // tc
    for idx, c0 in enumerate(range(0, cdim, tc)):
        cs = slice(c0, c0 + tc)
        u = jnp.dot(x, w_ref[:, di + c0:di + c0 + tc], preferred_element_type=f32)
        tail_ref[0, :, cs] = u[tm - SUBLANES:, :]
        yc = _causal_conv(u, carry_ref[:, cs], cw_ref[:, cs], cb_ref[:, cs], rows8)
        carry_ref[:, cs] = u[tm - SUBLANES:, :]
        o_ref[:, di + c0:di + c0 + tc] = _silu(yc)
        if idx < nz:
            o_ref[:, idx * tc:(idx + 1) * tc] = jnp.dot(x, w_ref[:, idx * tc:(idx + 1) * tc],
                                                        preferred_element_type=f32)
    o_ref[:, di + cdim:] = jnp.dot(x, w_ref[:, di + cdim:], preferred_element_type=f32)


def ssd_in_call(xn, w, layer, conv_w, conv_b, t_seq):
    m, k = xn.shape
    n = w.shape[2]
    cdim = conv_w.shape[1]
    di = n - cdim - LANES
    tm = min(PROJ_TM, m)
    tc = 2 * LANES
    assert t_seq % tm == 0 and cdim % tc == 0 and di % tc == 0 and di <= cdim and (t_seq & (t_seq - 1)) == 0
    kern = functools.partial(_ssd_in_kernel, tm=tm, tc=tc, t_seq=t_seq, di=di)
    return pl.pallas_call(
        kern,
        grid=(m // tm,),
        in_specs=[pl.BlockSpec((tm, k), lambda i: (i, 0)), _resident_layer(w, layer),
                  _resident(conv_w.shape), _resident((1, cdim))],
        out_specs=[pl.BlockSpec((tm, n), lambda i: (i, 0)), pl.BlockSpec((1, SUBLANES, cdim), lambda i: (i, 0, 0))],
        out_shape=[jax.ShapeDtypeStruct((m, n), f32), jax.ShapeDtypeStruct((m // tm, SUBLANES, cdim), f32)],
        scratch_shapes=[pltpu.VMEM((SUBLANES, cdim), f32)],
        compiler_params=_cparams(1),
        name="ssd_in_conv",
    )(xn, w, conv_w, conv_b.reshape(1, cdim))


def _ffn_kernel(*refs, tm, tf, t_seq, t_valid, has_state):
    if has_state:
        (act_ref, wo_ref, bo_ref, x_ref, nw1_ref, wu_ref, cw_ref, cb_ref, wd_ref, nw2_ref, buf_ref,
         xo_ref, xn2_ref, nst_ref, carry_ref) = refs
    else:
        (act_ref, wo_ref, bo_ref, x_ref, nw1_ref, wu_ref, cw_ref, cb_ref, wd_ref, nw2_ref,
         xo_ref, xn2_ref, tail_ref, carry_ref) = refs
    i = pl.program_id(0)
    dff = wd_ref.shape[0]

    x1 = jnp.dot(act_ref[...].astype(bf16), wo_ref[...], preferred_element_type=f32) + bo_ref[...] + x_ref[...]
    xn = _rms(x1, nw1_ref[...]).astype(bf16)
    rows8 = lax.broadcasted_iota(jnp.int32, (SUBLANES, 1), 0)
    if not has_state:
        @pl.when(((i * tm) & (t_seq - 1)) == 0)
        def _():
            carry_ref[...] = jnp.zeros_like(carry_ref)
    gs = []
    for c0 in range(0, dff, tf):
        cs = slice(c0, c0 + tf)
        a = jnp.dot(xn, wu_ref[:, cs], preferred_element_type=f32)
        b = jnp.dot(xn, wu_ref[:, dff + c0:dff + c0 + tf], preferred_element_type=f32)
        r1 = pltpu.roll(a, 1, 0)
        r2 = pltpu.roll(a, 2, 0)
        if has_state:
            p1s, p2s = [], []
            for s in range(tm // SUBLANES):
                rs = slice(s * SUBLANES, (s + 1) * SUBLANES)
                b0, b1 = buf_ref[s, 0:1, cs], buf_ref[s, 1:2, cs]
                p1s.append(jnp.where(rows8 == 0, b1, r1[rs, :]))
                p2s.append(jnp.where(rows8 == 0, b0, jnp.where(rows8 == 1, b1, r2[rs, :])))
                nst_ref[s, :, cs] = a[s * SUBLANES + t_valid - 2:s * SUBLANES + t_valid, :]
            p1 = jnp.concatenate(p1s, axis=0)
            p2 = jnp.concatenate(p2s, axis=0)
        else:
            tail_ref[0, :, cs] = a[tm - SUBLANES:, :]
            carry = carry_ref[:, cs]
            top1 = jnp.where(rows8 == 0, carry[7:8, :], r1[:SUBLANES])
            top2 = jnp.where(rows8 == 0, carry[6:7, :], jnp.where(rows8 == 1, carry[7:8, :], r2[:SUBLANES]))
            p1 = jnp.concatenate([top1, r1[SUBLANES:]], axis=0)
            p2 = jnp.concatenate([top2, r2[SUBLANES:]], axis=0)
            carry_ref[:, cs] = a[tm - SUBLANES:, :]
        y = cb_ref[:, cs] + p2 * cw_ref[0:1, cs]
        y = y + p1 * cw_ref[1:2, cs]
        y = y + a * cw_ref[2:3, cs]
        gs.append((_silu(y) * b).astype(bf16))
    g = jnp.concatenate(gs, axis=1)
    xo = jnp.dot(g, wd_ref[...], preferred_element_type=f32) + x1
    xo_ref[...] = xo
    xn2_ref[...] = _rms(xo, nw2_ref[...]).astype(xn2_ref.dtype)


def ffn_call(act, w_o, b_o, x, ffn_norm_w, w_up, conv_w, conv_b, w_down, next_w, t_seq, buf, t_valid, out_dtype, name):
    m, d = x.shape
    ko = act.shape[1]
    dff = w_down[0].shape[1]
    tf = 2 * LANES
    assert dff % tf == 0 and (t_seq & (t_seq - 1)) == 0
    has_state = buf is not None
    tm = min(FFN_TM // 2 if has_state else FFN_TM, m)
    assert (t_seq == SUBLANES and t_valid >= 2 and conv_w.shape[0] == 3) if has_state else (t_seq % tm == 0)
    if b_o is None:
        b_o = jnp.zeros((d,), f32)
    row = lambda w: pl.BlockSpec((tm, w), lambda i: (i, 0))
    in_specs = [row(ko), _resident_layer(*w_o), _resident((1, d)), row(d), _resident((1, d)),
                _resident_layer(*w_up), _resident((3, dff)), _resident((1, dff)),
                _resident_layer(*w_down), _resident((1, d))]
    args = [act, w_o[0], b_o.reshape(1, d), x, ffn_norm_w.reshape(1, d),
            w_up[0], conv_w, conv_b.reshape(1, dff), w_down[0], next_w.reshape(1, d)]
    if has_state:
        nseq = tm // t_seq
        in_specs.append(pl.BlockSpec((nseq, 2, dff), lambda i: (i, 0, 0)))
        args.append(buf)
        third_spec = pl.BlockSpec((nseq, 2, dff), lambda i: (i, 0, 0))
        third_shape = jax.ShapeDtypeStruct((m // t_seq, 2, dff), f32)
    else:
        third_spec = pl.BlockSpec((1, SUBLANES, dff), lambda i: (i, 0, 0))
        third_shape = jax.ShapeDtypeStruct((m // tm, SUBLANES, dff), f32)
    kern = functools.partial(_ffn_kernel, tm=tm, tf=tf, t_seq=t_seq, t_valid=t_valid, has_state=has_state)
    return pl.pallas_call(
        kern,
        grid=(m // tm,),
        in_specs=in_specs,
        out_specs=[row(d), row(d), third_spec],
        out_shape=[jax.ShapeDtypeStruct((m, d), f32), jax.ShapeDtypeStruct((m, d), out_dtype), third_shape],
        scratch_shapes=[pltpu.VMEM((SUBLANES, dff), f32)],
        compiler_params=_cparams(1),
        name=name,
    )(*args)


def _attn_softmax_pv(s, sink_col, vj):
    m = jnp.maximum(jnp.max(s, axis=-1, keepdims=True), sink_col)
    p = jnp.exp(s - m)
    l = jnp.sum(p, axis=-1, keepdims=True) + jnp.exp(sink_col - m)
    return _dot(p, vj) / l


def _sink_col(sinks_ref, j, group, rows_per_head):
    r = lax.broadcasted_iota(jnp.int32, (group * rows_per_head, 1), 0)
    col = jnp.full((group * rows_per_head, 1), sinks_ref[j * group], f32)
    for p in range(1, group):
        col = jnp.where(r >= p * rows_per_head, sinks_ref[j * group + p], col)
    return col


def _attn_prompt_kernel(sinks_ref, q_ref, kvc_ref, kvp_ref, o_ref, *, n_heads):
    n = pl.program_id(1)
    w, hd, kvh = WINDOW, ATT_HEAD_DIM, ATT_KV_HEADS
    group = n_heads // kvh
    kvw = kvh * hd
    assert group == kvh
    kv = jnp.concatenate([kvp_ref[...], kvc_ref[...]], axis=0)
    k2 = kv[:, :kvw]
    v2 = kv[:, kvw:]
    i = lax.broadcasted_iota(jnp.int32, (w, 2 * w), 0)
    c = lax.broadcasted_iota(jnp.int32, (w, 2 * w), 1)
    band = (c > i) & (c <= i + w) & ((n > 0) | (c >= w))
    mask = jnp.concatenate([band] * group, axis=0)
    slot = lax.broadcasted_iota(jnp.int32, (w, kvw), 1) >> (hd.bit_length() - 1)
    for j in range(kvh):
        qg = q_ref[:, j * kvw:(j + 1) * kvw] * (hd ** -0.5)
        qs = jnp.concatenate([jnp.where(slot == j, pltpu.roll(qg, ((j - p) % kvh) * hd, 1), 0)
                              for p in range(group)], axis=0)
        s = jnp.where(mask, _dot_nt(qs, k2), -jnp.inf)
        of = _attn_softmax_pv(s, _sink_col(sinks_ref, j, group, w), v2).astype(o_ref.dtype)
        og = of[0:w, :]
        for p in range(group):
            piece = of[p * w:(p + 1) * w, :]
            og = jnp.where(slot == p, pltpu.roll(piece, ((p - j) % kvh) * hd, 1), og)
        o_ref[:, j * kvw:(j + 1) * kvw] = og


def attn_prompt_call(qkv, sinks, bsz, t, n_heads):
    w, hd, kvh = WINDOW, ATT_HEAD_DIM, ATT_KV_HEADS
    nb = t // w
    qd = n_heads * hd
    kvd = 2 * kvh * hd
    assert qd % kvd == 0
    kvblk = qd // kvd
    kern = functools.partial(_attn_prompt_kernel, n_heads=n_heads)
    return pl.pallas_call(
        kern,
        grid=(bsz, nb),
        in_specs=[pl.BlockSpec(memory_space=pltpu.SMEM),
                  pl.BlockSpec((w, qd), lambda b, n: (b * nb + n, 0)),
                  pl.BlockSpec((w, kvd), lambda b, n: (b * nb + n, kvblk)),
                  pl.BlockSpec((w, kvd), lambda b, n: (b * nb + jnp.maximum(n - 1, 0), kvblk))],
        out_specs=pl.BlockSpec((w, qd), lambda b, n: (b * nb + n, 0)),
        out_shape=jax.ShapeDtypeStruct((bsz * t, qd), bf16),
        compiler_params=_cparams(2),
        name="attn_prompt",
    )(sinks, qkv, qkv, qkv)


def _attn_cache_kernel(sinks_ref, q_ref, ck_ref, cv_ref, o_ref, *, n_heads, t_valid, bt):
    w, hd, kvh = WINDOW, ATT_HEAD_DIM, ATT_KV_HEADS
    group = n_heads // kvh
    tp = q_ref.shape[1]
    qd = n_heads * hd
    tq = lax.broadcasted_iota(jnp.int32, (tp, w), 0)
    jc = lax.broadcasted_iota(jnp.int32, (tp, w), 1)
    mc = (jc > tq) & (jc - w + PAST_LEN >= 0)
    tq2 = lax.broadcasted_iota(jnp.int32, (tp, tp), 0)
    un = lax.broadcasted_iota(jnp.int32, (tp, tp), 1)
    mn = (un <= tq2) & (un < t_valid)
    mask_c = jnp.concatenate([mc] * n_heads, axis=0)
    mask_n = jnp.concatenate([mn] * n_heads, axis=0)
    kvw = kvh * hd
    assert group == kvh
    slot = lax.broadcasted_iota(jnp.int32, (tp, kvw), 1) >> (hd.bit_length() - 1)
    sink = _sink_col(sinks_ref, 0, n_heads, tp)

    for bi in range(bt):
        row = q_ref[bi]
        ck = ck_ref[bi]
        cv = cv_ref[bi]
        kn = row[:, qd:qd + kvw]
        vn = row[:, qd + kvw:]
        pieces = []
        for j in range(kvh):
            qg = row[:, j * kvw:(j + 1) * kvw] * (hd ** -0.5)
            for p in range(group):
                pieces.append(jnp.where(slot == j, pltpu.roll(qg, ((j - p) % kvh) * hd, 1), 0.0))
        qs = jnp.concatenate(pieces, axis=0)
        sc = jnp.where(mask_c, _dot(qs, ck), -jnp.inf)
        sn = jnp.where(mask_n, _dot_nt(qs, kn), -jnp.inf)
        m = jnp.maximum(jnp.maximum(jnp.max(sc, axis=-1, keepdims=True),
                                    jnp.max(sn, axis=-1, keepdims=True)), sink)
        pc = jnp.exp(sc - m)
        pn = jnp.exp(sn - m)
        l = jnp.sum(pc, axis=-1, keepdims=True) + jnp.sum(pn, axis=-1, keepdims=True) + jnp.exp(sink - m)
        of = (_dot_nt(pc, cv) + _dot(pn, vn)) / l
        for j in range(kvh):
            og = jnp.zeros((tp, kvw), f32)
            for p in range(group):
                h = j * group + p
                og = og + jnp.where(slot == p, pltpu.roll(of[h * tp:(h + 1) * tp, :], ((p - j) % kvh) * hd, 1), 0.0)
            o_ref[bi, :, j * kvw:(j + 1) * kvw] = og


def attn_cache_call(qkv, cache_k, cache_v, sinks, n_heads, t_valid):
    bsz, tp, nq = qkv.shape
    w = WINDOW
    kd = cache_k.shape[1]
    qd = n_heads * ATT_HEAD_DIM
    bt = 8
    kern = functools.partial(_attn_cache_kernel, n_heads=n_heads, t_valid=t_valid, bt=bt)
    return pl.pallas_call(
        kern,
        grid=(bsz // bt,),
        in_specs=[pl.BlockSpec(memory_space=pltpu.SMEM),
                  pl.BlockSpec((bt, tp, nq), lambda b: (b, 0, 0)),
                  pl.BlockSpec((bt, kd, w), lambda b: (b, 0, 0)),
                  pl.BlockSpec((bt, kd, w), lambda b: (b, 0, 0))],
        out_specs=pl.BlockSpec((bt, tp, qd), lambda b: (b, 0, 0)),
        out_shape=jax.ShapeDtypeStruct((bsz, tp, qd), f32),
        compiler_params=_cparams(1),
        name="attn_cache",
    )(sinks, qkv, cache_k, cache_v)


def _hgrn_chunk(x, st, lb, nw, tril, diag_mask, valid, grp):
    nh = HG_HEADS
    c_rows = x.shape[0]
    dim = x.shape[1] // 4
    dk = dim // nh
    ng = c_rows // grp
    if valid is not None:
        x = jnp.where(valid, x, 0.0)
    q = _silu(x[:, 0:dim])
    fg = lb + (1.0 - lb) * jax.nn.sigmoid(x[:, dim:2 * dim])
    logf = jnp.log(fg)
    k = 1.0 - fg
    v = x[:, 2 * dim:3 * dim]
    gate = x[:, 3 * dim:4 * dim]
    if valid is not None:
        logf = jnp.where(valid, logf, 0.0)
        k = jnp.where(valid, k, 0.0)
    bcum = _cumsum_rows(logf, tril.astype(bf16))

    def rows_of(vals):
        return jnp.concatenate([jnp.broadcast_to(r, (grp, dk)) for r in vals], axis=0)

    ys, st_out = [], []
    for h in range(nh):
        sl = slice(h * dk, (h + 1) * dk)
        b = bcum[:, sl]
        qh, kh, vh = q[:, sl], k[:, sl], v[:, sl]
        zero = jnp.zeros((1, dk), f32)
        r = [zero] + [b[i * grp - 1:i * grp, :] for i in range(1, ng)] + [b[c_rows - 1:c_rows, :]]
        mid = [b[i * grp + grp // 2 - 1:i * grp + grp // 2, :] for i in range(ng)]
        b_last = r[ng]
        r_start = rows_of(r[:ng])
        r_end = rows_of(r[1:])
        r_mid = rows_of(mid)
        qg = qh * jnp.exp(b - r_start)
        kt = kh * jnp.exp(r_end - b)
        qm = qh * jnp.exp(b - r_mid)
        km = kh * jnp.exp(r_mid - b)
        a = jnp.where(diag_mask, _dot_nt(qm, km), 0.0)
        if ng > 1:
            lhs, rhs = [], []
            for j in range(ng - 1):
                lhs.append(jnp.concatenate(
                    [jnp.zeros((grp, dk), f32) if i <= j else qg[i * grp:(i + 1) * grp, :] * jnp.exp(r[i] - r[j + 1])
                     for i in range(ng)], axis=0))
                rhs.append(jnp.concatenate(
                    [kt[i * grp:(i + 1) * grp, :] if i == j else jnp.zeros((grp, dk), f32) for i in range(ng)], axis=0))
            a = a + _dot_nt(jnp.concatenate(lhs, axis=1), jnp.concatenate(rhs, axis=1))
        q_inter = qg * jnp.exp(r_start)
        o = _dot(a, vh) + _dot_nt(q_inter, st[h])
        k_dec = kt * jnp.exp(b_last - r_end)
        st_out.append(st[h] * jnp.exp(b_last) + _dot_tn(vh, k_dec))
        on = o * lax.rsqrt(jnp.mean(o * o, axis=-1, keepdims=True) + RMS_EPS)
        ys.append(on * nw[:, sl] * _silu(gate[:, sl]))
    return ys, st_out


def _hgrn_kernel(*refs, c_rows, grp, t_valid, t_pad, layer, has_s0, ub, uc):
    if has_s0:
        p_ref, lbl_ref, nw_ref, s0_ref, y_ref, so_ref, st_ref = refs
    else:
        p_ref, lbl_ref, nw_ref, y_ref, so_ref, st_ref = refs
    c = pl.program_id(1)
    nc = pl.num_programs(1)
    nh = HG_HEADS
    dim = p_ref.shape[1] // 4
    dk = dim // nh

    @pl.when(c == 0)
    def _():
        if has_s0:
            for s in range(ub):
                for h in range(nh):
                    st_ref[s, h] = s0_ref[s, h].T
        else:
            st_ref[...] = jnp.zeros_like(st_ref)

    lbl = lbl_ref[...]
    e = jnp.exp(lbl - jnp.max(lbl, axis=0, keepdims=True))
    sm = e / jnp.sum(e, axis=0, keepdims=True)
    lb = jnp.zeros((1, dim), f32)
    for r in range(1, layer + 1):
        lb = lb + sm[r:r + 1, :]

    tril = _tril_mask(c_rows)
    rr = lax.broadcasted_iota(jnp.int32, (c_rows, c_rows), 0)
    ss = lax.broadcasted_iota(jnp.int32, (c_rows, c_rows), 1)
    gshift = grp.bit_length() - 1
    diag_mask = tril & ((rr >> gshift) == (ss >> gshift))
    nw = nw_ref[...]
    rows = lax.broadcasted_iota(jnp.int32, (c_rows, 1), 0)

    for s in range(ub):
        st = [st_ref[s, h] for h in range(nh)]
        for cc in range(uc):
            r0 = (s * uc + cc) * c_rows
            valid = ((c * uc + cc) * c_rows + rows) < t_valid if t_valid < t_pad else None
            ys, st = _hgrn_chunk(p_ref[r0:r0 + c_rows, :], st, lb, nw, tril, diag_mask, valid, grp)
            for h in range(nh):
                y_ref[r0:r0 + c_rows, h * dk:(h + 1) * dk] = ys[h].astype(y_ref.dtype)
        for h in range(nh):
            st_ref[s, h] = st[h]

    @pl.when(c == nc - 1)
    def _():
        for s in range(ub):
            for h in range(nh):
                so_ref[s, h] = st_ref[s, h].T


def _units_per_step(bsz, nc, max_seqs, max_chunks):
    if nc == 1:
        return max(u for u in (1, 2, 4, 8) if u <= max_seqs and bsz % u == 0), 1
    return 1, max(u for u in (1, 2, 4, 8) if u <= max_chunks and nc % u == 0)


def hgrn_call(p, lb_logits, norm_w, s0, bsz, t_pad, t_valid, layer):
    dim = p.shape[1] // 4
    nh = HG_HEADS
    dk = dim // nh
    c_rows = min(128, t_pad)
    grp = min(HG_CHUNK_GROUP, c_rows)
    nc = t_pad // c_rows
    ub, uc = _units_per_step(bsz, nc, 8, 4)
    rows = ub * uc * c_rows
    ncs = nc // uc
    has_s0 = s0 is not None
    in_specs = [pl.BlockSpec((rows, 4 * dim), lambda b, c: (b * ncs + c, 0)),
                pl.BlockSpec(lb_logits.shape, lambda b, c: (0, 0)),
                pl.BlockSpec((1, dim), lambda b, c: (0, 0))]
    args = [p, lb_logits, norm_w.reshape(1, dim)]
    if has_s0:
        in_specs.append(pl.BlockSpec((ub, nh, dk, dk), lambda b, c: (b, 0, 0, 0)))
        args.append(s0)
    kern = functools.partial(_hgrn_kernel, c_rows=c_rows, grp=grp, t_valid=t_valid, t_pad=t_pad,
                             layer=layer, has_s0=has_s0, ub=ub, uc=uc)
    return pl.pallas_call(
        kern,
        grid=(bsz // ub, ncs),
        in_specs=in_specs,
        out_specs=[pl.BlockSpec((rows, dim), lambda b, c: (b * ncs + c, 0)),
                   pl.BlockSpec((ub, nh, dk, dk), lambda b, c: (b, 0, 0, 0))],
        out_shape=[jax.ShapeDtypeStruct((bsz * t_pad, dim), bf16),
                   jax.ShapeDtypeStruct((bsz, nh, dk, dk), f32)],
        scratch_shapes=[pltpu.VMEM((ub, nh, dk, dk), f32)],
        compiler_params=_cparams(2),
        name="hgrn",
    )(*args)


def _expand_heads(w, ex_ref):
    hi = w.astype(bf16)
    r1 = w - hi.astype(f32)
    mid = r1.astype(bf16)
    lo = (r1 - mid.astype(f32)).astype(bf16)
    return jnp.dot(jnp.concatenate([hi, mid, lo], axis=1), ex_ref[...], preferred_element_type=f32)


def _ssd_kernel(*refs, c_rows, t_valid, t_pad, has_s0, ub, uc, conv_done):
    if has_s0:
        (zx_ref, cw_ref, cb_ref, dtb_ref, alog_ref, dsk_ref, nw_ref, ex_ref, s0_ref, buf_ref,
         y_ref, so_ref, st_ref, carry_ref) = refs
    else:
        (zx_ref, cw_ref, cb_ref, dtb_ref, alog_ref, dsk_ref, nw_ref, ex_ref,
         y_ref, so_ref, st_ref, carry_ref) = refs
    c = pl.program_id(1)
    nc = pl.num_programs(1)
    hp, ns, ngr = SSD_HEAD_DIM, SSD_STATE, SSD_GROUPS
    di = y_ref.shape[1]
    nh = di // hp
    hpg = nh // ngr
    gw = ngr * ns
    kconv = cw_ref.shape[0]
    pair = 2 * hp
    assert pair == LANES and hpg % 2 == 0

    @pl.when(c == 0)
    def _():
        carry_ref[...] = jnp.zeros_like(carry_ref)
        if has_s0:
            for s in range(ub):
                for r in range(di // LANES):
                    st_ref[s, :, r * LANES:(r + 1) * LANES] = s0_ref[s, r * LANES:(r + 1) * LANES, :].T
                carry_ref[s, SUBLANES - (kconv - 1):, :] = buf_ref[s]
        else:
            st_ref[...] = jnp.zeros_like(st_ref)

    cdim = cw_ref.shape[1]
    rows = lax.broadcasted_iota(jnp.int32, (c_rows, 1), 0)
    rows8 = rows[:SUBLANES]
    cw = cw_ref[...]
    a_neg = -jnp.exp(alog_ref[...])
    tril = _tril_mask(c_rows)
    dsk = dsk_ref[...]
    nw = nw_ref[...]
    lo = lax.broadcasted_iota(jnp.int32, (c_rows, LANES), 1) < hp
    for s in range(ub):
        for cc in range(uc):
            r0 = (s * uc + cc) * c_rows
            valid = ((c * uc + cc) * c_rows + rows) < t_valid if t_valid < t_pad else None
            _ssd_chunk(zx_ref[r0:r0 + c_rows, :], st_ref.at[s], carry_ref.at[s], y_ref.at[r0:r0 + c_rows, :],
                       cw, cb_ref[...], dtb_ref[...], a_neg, dsk, nw, ex_ref, tril, lo, rows8, valid, conv_done)

    @pl.when(c == nc - 1)
    def _():
        for s in range(ub):
            for r in range(di // LANES):
                so_ref[s, r * LANES:(r + 1) * LANES, :] = st_ref[s, :, r * LANES:(r + 1) * LANES].T


def _ssd_chunk(zx, st_ref, carry_ref, y_ref, cw, cb, dtb, a_neg, dsk, nw, ex_ref, tril, lo, rows8, valid, conv_done):
    hp, ns, ngr = SSD_HEAD_DIM, SSD_STATE, SSD_GROUPS
    c_rows = zx.shape[0]
    di = y_ref.shape[1]
    nh = di // hp
    hpg = nh // ngr
    gw = ngr * ns
    kconv, cdim = cw.shape
    pair = 2 * hp
    z = zx[:, :di]
    u = zx[:, di:di + cdim]
    if conv_done:
        xbc = u
    else:
        xbc = _silu(_causal_conv(u, carry_ref[...], cw, cb, rows8))
        carry_ref[...] = u[c_rows - SUBLANES:, :]
    dt = jax.nn.softplus(zx[:, di + cdim:] + dtb)
    if valid is not None:
        xbc = jnp.where(valid, xbc, 0.0)
        dt = jnp.where(valid, dt, 0.0)
    xs = xbc[:, :di]
    bm = xbc[:, di:di + gw]
    cm = xbc[:, di + gw:di + 2 * gw]
    acum = _cumsum_rows(dt * a_neg, tril.astype(bf16))

    def head_rows(a):
        if c_rows < LANES:
            a = jnp.concatenate([a, jnp.zeros((LANES - c_rows, LANES), f32)], axis=0)
        return a.T[:, :c_rows]

    acum_t = head_rows(acum)
    dt_t = head_rows(dt)
    a_last = acum[c_rows - 1:c_rows, :]
    xd = xs * _expand_heads(dt * jnp.exp(a_last - acum), ex_ref)

    ys = []
    for g in range(ngr):
        bg = bm[:, g * ns:(g + 1) * ns]
        cg = cm[:, g * ns:(g + 1) * ns]
        cb_mat = _dot_nt(cg, bg)
        gsl = slice(g * hpg * hp, (g + 1) * hpg * hp)
        y_inter = _dot(cg, st_ref[:, gsl])
        decs = []
        for q in range(hpg // 2):
            h1 = g * hpg + 2 * q
            psl = slice(h1 * hp, h1 * hp + pair)
            ms, es = [], []
            for h in (h1, h1 + 1):
                colb = jnp.broadcast_to(acum[:, h:h + 1], (c_rows, LANES))
                seg = colb[:, :c_rows] - acum_t[h:h + 1, :]
                ms.append(cb_mat * jnp.exp(jnp.where(tril, seg, -jnp.inf)) * dt_t[h:h + 1, :])
                es.append(jnp.exp(colb))
            xp = xs[:, psl]
            x_lo = jnp.where(lo, xp, 0.0)
            x_hi = jnp.where(lo, 0.0, xp)
            if (2 * c_rows) % LANES == 0:
                y_intra = _dot(jnp.concatenate(ms, axis=1), jnp.concatenate([x_lo, x_hi], axis=0))
            else:
                y_intra = _dot(ms[0], x_lo) + _dot(ms[1], x_hi)
            y_pair = y_intra + y_inter[:, q * pair:(q + 1) * pair] * jnp.where(lo, es[0], es[1])
            ys.append(y_pair + dsk[:, psl] * xp)
            decs.append(jnp.where(lo[0:1, :], jnp.exp(a_last[:, h1:h1 + 1]), jnp.exp(a_last[:, h1 + 1:h1 + 2])))
        st_ref[:, gsl] = st_ref[:, gsl] * jnp.concatenate(decs, axis=1) + _dot_tn(bg, xd[:, gsl])

    y = jnp.concatenate(ys, axis=1) * _silu(z)
    gdim = di // ngr
    for g in range(ngr):
        sl = slice(g * gdim, (g + 1) * gdim)
        yg = y[:, sl]
        y_ref[:, sl] = (yg * lax.rsqrt(jnp.mean(yg * yg, axis=-1, keepdims=True) + RMS_EPS) * nw[:, sl]).astype(y_ref.dtype)


def ssd_call(zx, conv_w, conv_b, dt_bias, a_log, d_skip, norm_w, s0, conv_buf, bsz, t_pad, t_valid, conv_done):
    hp, ns = SSD_HEAD_DIM, SSD_STATE
    conv_dim = conv_w.shape[1]
    di = zx.shape[1] - conv_dim - LANES
    nh = di // hp
    c_rows = min(128, t_pad)
    nc = t_pad // c_rows
    ub, uc = _units_per_step(bsz, nc, 4, 2)
    rows = ub * uc * c_rows
    ncs = nc // uc
    has_s0 = s0 is not None
    assert not (conv_done and has_s0)
    pad_h = lambda a: jnp.pad(a.reshape(1, nh), ((0, 0), (0, LANES - nh)))
    full = lambda shape: pl.BlockSpec(shape, lambda b, c: (0,) * len(shape))
    sel = (jnp.arange(di)[None, :] // hp == jnp.arange(LANES)[:, None]).astype(bf16)
    expand = jnp.concatenate([sel, sel, sel], axis=0)
    in_specs = [pl.BlockSpec((rows, zx.shape[1]), lambda b, c: (b * ncs + c, 0)),
                full(conv_w.shape), full((1, conv_dim)), full((1, LANES)), full((1, LANES)),
                full((1, di)), full((1, di)), full(expand.shape)]
    args = [zx, conv_w, conv_b.reshape(1, conv_dim), pad_h(dt_bias), pad_h(a_log),
            jnp.repeat(d_skip, hp).reshape(1, di), norm_w.reshape(1, di), expand]
    if has_s0:
        in_specs += [pl.BlockSpec((ub, di, ns), lambda b, c: (b, 0, 0)),
                     pl.BlockSpec((ub,) + conv_buf.shape[1:], lambda b, c: (b, 0, 0))]
        args += [s0.reshape(bsz, di, ns), conv_buf]
    kern = functools.partial(_ssd_kernel, c_rows=c_rows, t_valid=t_valid, t_pad=t_pad, has_s0=has_s0, ub=ub, uc=uc,
                             conv_done=conv_done)
    y, s_new = pl.pallas_call(
        kern,
        grid=(bsz // ub, ncs),
        in_specs=in_specs,
        out_specs=[pl.BlockSpec((rows, di), lambda b, c: (b * ncs + c, 0)),
                   pl.BlockSpec((ub, di, ns), lambda b, c: (b, 0, 0))],
        out_shape=[jax.ShapeDtypeStruct((bsz * t_pad, di), bf16),
                   jax.ShapeDtypeStruct((bsz, di, ns), f32)],
        scratch_shapes=[pltpu.VMEM((ub, ns, di), f32), pltpu.VMEM((ub, SUBLANES, conv_dim), f32)],
        compiler_params=_cparams(2),
        name="ssd",
    )(*args)
    return y, s_new.reshape(bsz, nh, hp, ns)


def _trunk(x, c_k, c_v, s_hg, s_ssm, s_sconv, s_fconv, prm):
    bsz, t, d = x.shape
    depth = prm["norm_mix_w"].shape[0]
    decode = c_k is not None
    tp = -(-t // SUBLANES) * SUBLANES
    if tp != t:
        x = jnp.pad(x, ((0, 0), (0, tp - t), (0, 0)))
    m = bsz * tp
    n_heads = d // ATT_HEAD_DIM
    qd = n_heads * ATT_HEAD_DIM
    kvd = ATT_KV_HEADS * ATT_HEAD_DIM
    x2 = x.reshape(m, d)
    xn, xn_w = x2, prm["norm_mix_w"][0]
    nk, nv, nhg, nssm, nsconv, nfconv = [], [], [], [], [], []
    slot = [0] * N_MIXERS
    for i in range(depth):
        kind = i % N_MIXERS
        j = slot[kind]
        slot[kind] += 1
        nw_ffn = prm["norm_ffn_w"][i]
        if kind == 0:
            w_qkv, b_qkv = prm["attn_w_qkv"], prm["attn_b_qkv"][j]
            nq = w_qkv.shape[2]
            if decode:
                qkv = proj_call(xn, xn_w, w_qkv, j, b_qkv, f32, "attn_qkv").reshape(bsz, tp, nq)
                ck = jnp.transpose(c_k[j], (0, 2, 3, 1)).reshape(bsz, kvd, WINDOW)
                cv = jnp.transpose(c_v[j], (0, 2, 3, 1)).reshape(bsz, kvd, WINDOW)
                o = attn_cache_call(qkv, ck, cv, prm["attn_sinks"][j], n_heads, t).reshape(m, qd)
                to4 = lambda a: a.reshape(bsz, t, ATT_KV_HEADS, ATT_HEAD_DIM)
                k_new = to4(qkv[:, :t, qd:qd + kvd])
                v_new = to4(qkv[:, :t, qd + kvd:])
            else:
                qkv = proj_call(xn, xn_w, w_qkv, j, b_qkv, bf16, "attn_qkv")
                o = attn_prompt_call(qkv, prm["attn_sinks"][j], bsz, t, n_heads)
                xn_last = xn.reshape(bsz, t, d)[:, t - WINDOW:].reshape(bsz * WINDOW, d)
                kv_last = proj_call(xn_last, xn_w, w_qkv[:, :, qd:], j, b_qkv[qd:], f32,
                                    "attn_kv_tail").reshape(bsz, WINDOW, 2, kvd)
                to4 = lambda a: a.reshape(bsz, WINDOW, ATT_KV_HEADS, ATT_HEAD_DIM)
                k_new, v_new = to4(kv_last[:, :, 0]), to4(kv_last[:, :, 1])
            nk.append(k_new)
            nv.append(v_new)
            act, w_o, b_o = o, (prm["attn_w_o"], j), prm["attn_b_o"][j]
        elif kind == 1:
            p = proj_call(xn, xn_w, prm["hgrn_w_in"], j, None, f32, "hgrn_in")
            y, s_new = hgrn_call(p, prm["hgrn_lb_logits"], prm["hgrn_norm_w"][j],
                                 None if s_hg is None else s_hg[j], bsz, tp, t, i)
            nhg.append(s_new)
            act, w_o, b_o = y, (prm["hgrn_w_o"], j), None
        else:
            conv_dim = prm["ssd_conv_w"].shape[2]
            di = prm["ssd_w_o"].shape[1]
            kc = prm["ssd_conv_w"].shape[1]
            if s_sconv is None:
                zx, pre = ssd_in_call(xn, prm["ssd_w_in"], j, prm["ssd_conv_w"][j], prm["ssd_conv_b"][j], tp)
                pre = pre.reshape(bsz, tp // min(PROJ_TM, m), SUBLANES, conv_dim)[:, -1, SUBLANES - (kc - 1):]
            else:
                zx = proj_call(xn, xn_w, prm["ssd_w_in"], j, None, f32, "ssd_in")
                pre = zx.reshape(bsz, tp, zx.shape[1])[:, max(t - (kc - 1), 0):t, di:di + conv_dim]
                pre = jnp.concatenate([s_sconv[j], pre], axis=1)
                pre = pre[:, pre.shape[1] - (kc - 1):]
            nsconv.append(pre)
            y, s_new = ssd_call(zx, prm["ssd_conv_w"][j], prm["ssd_conv_b"][j], prm["ssd_dt_bias"][j],
                                prm["ssd_a_log"][j], prm["ssd_d"][j], prm["ssd_norm_w"][j],
                                None if s_ssm is None else s_ssm[j],
                                None if s_sconv is None else s_sconv[j], bsz, tp, t, s_sconv is None)
            nssm.append(s_new)
            act, w_o, b_o = y, (prm["ssd_w_o"], j), None
        last = i == depth - 1
        next_w = prm["norm_final_w"] if last else prm["norm_mix_w"][i + 1]
        dff = prm["ffn_w_down"].shape[1]
        kf = prm["ffn_conv_w"].shape[1]
        x2, xn, tail = ffn_call(act, w_o, b_o, x2, nw_ffn, (prm["ffn_w_up"], i), prm["ffn_conv_w"][i],
                                prm["ffn_conv_b"][i], (prm["ffn_w_down"], i), next_w, tp,
                                None if s_fconv is None else s_fconv[i], t, f32 if last else bf16, "ffn")
        xn_w = None
        if s_fconv is not None:
            nfconv.append(tail)
        else:
            tiles_per_seq = tp // tail_tile_rows(m)
            tl = tail.reshape(bsz, tiles_per_seq, SUBLANES, dff)[:, -1]
            nfconv.append(tl[:, SUBLANES - (kf - 1):])
    stack = lambda xs: xs[0][None] if len(xs) == 1 else jnp.stack(xs)
    y = xn.reshape(bsz, tp, d)[:, :t]
    nk, nv = stack(nk), stack(nv)
    if decode:
        nk = jnp.concatenate([c_k[:, :, t:], nk], axis=2)
        nv = jnp.concatenate([c_v[:, :, t:], nv], axis=2)
    return (y, nk, nv, stack(nhg), stack(nssm), stack(nsconv), stack(nfconv))


def tail_tile_rows(m):
    return min(FFN_TM, m)


def kernel(x_prompt, x_sample, cache_attn_k, cache_attn_v, state_hgrn, state_ssm, state_ssm_conv, state_ffn_conv, norm_mix_w, norm_ffn_w, norm_final_w, attn_w_qkv, attn_b_qkv, attn_sinks, attn_w_o, attn_b_o, hgrn_w_in, hgrn_lb_logits, hgrn_norm_w, hgrn_w_o, ssd_w_in, ssd_conv_w, ssd_conv_b, ssd_dt_bias, ssd_a_log, ssd_d, ssd_norm_w, ssd_w_o, ffn_w_up, ffn_conv_w, ffn_conv_b, ffn_w_down):
    cast = lambda w: w.astype(bf16)
    ssd_in_pad = (-ssd_w_in.shape[2]) % LANES
    prm = dict(norm_mix_w=norm_mix_w, norm_ffn_w=norm_ffn_w, norm_final_w=norm_final_w,
               attn_w_qkv=cast(attn_w_qkv), attn_b_qkv=attn_b_qkv, attn_sinks=attn_sinks,
               attn_w_o=cast(attn_w_o), attn_b_o=attn_b_o,
               hgrn_w_in=cast(hgrn_w_in), hgrn_lb_logits=hgrn_lb_logits, hgrn_norm_w=hgrn_norm_w,
               hgrn_w_o=cast(hgrn_w_o),
               ssd_w_in=jnp.pad(cast(ssd_w_in), ((0, 0), (0, 0), (0, ssd_in_pad))), ssd_conv_w=ssd_conv_w,
               ssd_conv_b=ssd_conv_b, ssd_dt_bias=ssd_dt_bias,
               ssd_a_log=ssd_a_log, ssd_d=ssd_d, ssd_norm_w=ssd_norm_w, ssd_w_o=cast(ssd_w_o),
               ffn_w_up=cast(ffn_w_up), ffn_conv_w=ffn_conv_w, ffn_conv_b=ffn_conv_b, ffn_w_down=cast(ffn_w_down))
    outs_p = _trunk(x_prompt, None, None, None, None, None, None, prm)
    outs_s = _trunk(x_sample, cache_attn_k, cache_attn_v, state_hgrn, state_ssm, state_ssm_conv,
                    state_ffn_conv, prm)
    return (outs_p[0], outs_s[0]) + outs_p[1:] + outs_s[1:]
```

```python
import functools

import jax
import jax.numpy as jnp
from jax import lax
from jax.experimental import pallas as pl
from jax.experimental.pallas import tpu as pltpu

bf16 = jnp.bfloat16
f32 = jnp.float32

RMS_EPS = 1e-6
N_MIXERS = 3
WINDOW = 128
PAST_LEN = 8192
ATT_HEAD_DIM = 64
ATT_KV_HEADS = 4
HG_HEADS = 8
HG_CHUNK_GROUP = 32
SSD_HEAD_DIM = 64
SSD_STATE = 128
SSD_GROUPS = 4

LANES = 128
SUBLANES = 8
VMEM_LIMIT = 56 * 1024 * 1024
PROJ_TM = 512
FFN_TM = 512


def _cparams(n_axes):
    return pltpu.CompilerParams(dimension_semantics=("arbitrary",) * n_axes,
                                vmem_limit_bytes=VMEM_LIMIT)


def _dot(a, b):
    return jnp.dot(a.astype(bf16), b.astype(bf16), preferred_element_type=f32)


def _dot_nt(a, b):
    return lax.dot_general(a.astype(bf16), b.astype(bf16), (((1,), (1,)), ((), ())),
                           preferred_element_type=f32)


def _dot_tn(a, b):
    return lax.dot_general(a.astype(bf16), b.astype(bf16), (((0,), (0,)), ((), ())),
                           preferred_element_type=f32)


def _silu(x):
    return x * jax.nn.sigmoid(x)


def _rms(x, w):
    ms = jnp.mean(x * x, axis=-1, keepdims=True)
    return x * lax.rsqrt(ms + RMS_EPS) * w


def _cumsum_rows(x, tril):
    hi = x.astype(bf16)
    r1 = x - hi.astype(f32)
    mid = r1.astype(bf16)
    lo = (r1 - mid.astype(f32)).astype(bf16)
    d = functools.partial(jnp.dot, preferred_element_type=f32)
    return d(tril, hi) + d(tril, mid) + d(tril, lo)


def _tril_mask(c):
    r = lax.broadcasted_iota(jnp.int32, (c, c), 0)
    s = lax.broadcasted_iota(jnp.int32, (c, c), 1)
    return r >= s


def _resident(shape):
    return pl.BlockSpec(shape, lambda *_: (0,) * len(shape), pipeline_mode=pl.Buffered(1))


def _resident_layer(w, layer):
    return pl.BlockSpec((None,) + w.shape[1:], lambda *_: (layer, 0, 0), pipeline_mode=pl.Buffered(1))


def _proj_kernel(x_ref, nw_ref, w_ref, b_ref, o_ref):
    x = x_ref[...]
    if x.dtype != bf16:
        x = _rms(x, nw_ref[...]).astype(bf16)
    acc = jnp.dot(x, w_ref[...], preferred_element_type=f32)
    o_ref[...] = (acc + b_ref[...]).astype(o_ref.dtype)


def proj_call(x, norm_w, w, layer, bias, out_dtype, name):
    m, k = x.shape
    n = w.shape[2]
    tm = min(PROJ_TM, m)
    if bias is None:
        bias = jnp.zeros((n,), f32)
    if norm_w is None:
        assert x.dtype == bf16
        norm_w = jnp.ones((k,), f32)
    return pl.pallas_call(
        _proj_kernel,
        grid=(m // tm,),
        in_specs=[pl.BlockSpec((tm, k), lambda i: (i, 0)), _resident((1, k)), _resident_layer(w, layer),
                  _resident((1, n))],
        out_specs=pl.BlockSpec((tm, n), lambda i: (i, 0)),
        out_shape=jax.ShapeDtypeStruct((m, n), out_dtype),
        compiler_params=_cparams(1),
        name=name,
    )(x, norm_w.reshape(1, k), w, bias.reshape(1, n))


def _causal_conv(u, prev, cw, cb, rows8):
    kconv = cw.shape[0]
    yc = cb
    for kk in range(kconv - 1):
        back = kconv - 1 - kk
        sh = pltpu.roll(u, back, 0)
        top = sh[:SUBLANES]
        for r0 in range(back):
            top = jnp.where(rows8 == r0, prev[SUBLANES - back + r0:SUBLANES - back + r0 + 1, :], top)
        sh = jnp.concatenate([top, sh[SUBLANES:]], axis=0) if u.shape[0] > SUBLANES else top
        yc = yc + sh * cw[kk:kk + 1, :]
    return yc + u * cw[kconv - 1:kconv, :]


def _ssd_in_kernel(x_ref, w_ref, cw_ref, cb_ref, o_ref, tail_ref, carry_ref, *, tm, tc, t_seq, di):
    i = pl.program_id(0)
    cdim = cw_ref.shape[1]

    @pl.when(((i * tm) & (t_seq - 1)) == 0)
    def _():
        carry_ref[...] = jnp.zeros_like(carry_ref)

    x = x_ref[...]
    rows8 = lax.broadcasted_iota(jnp.int32, (SUBLANES, 1), 0)
    nz = di // tc
    for idx, c0 in enumerate(range(0, cdim, tc)):
        cs = slice(c0, c0 + tc)
        u = jnp.dot(x, w_ref[:, di + c0:di + c0 + tc], preferred_element_type=f32)
        tail_ref[0, :, cs] = u[tm - SUBLANES:, :]
        yc = _causal_conv(u, carry_ref[:, cs], cw_ref[:, cs], cb_ref[:, cs], rows8)
        carry_ref[:, cs] = u[tm - SUBLANES:, :]
        o_ref[:, di + c0:di + c0 + tc] = _silu(yc)
        if idx < nz:
            o_ref[:, idx * tc:(idx + 1) * tc] = jnp.dot(x, w_ref[:, idx * tc:(idx + 1) * tc],
                                                        preferred_element_type=f32)
    o_ref[:, di + cdim:] = jnp.dot(x, w_ref[:, di + cdim:], preferred_element_type=f32)


def ssd_in_call(xn, w, layer, conv_w, conv_b, t_seq):
    m, k = xn.shape
    n = w.shape[2]
    cdim = conv_w.shape[1]
    di = n - cdim - LANES
    tm = min(PROJ_TM, m)
    tc = 2 * LANES
    assert t_seq % tm == 0 and cdim % tc == 0 and di % tc == 0 and di <= cdim and (t_seq & (t_seq - 1)) == 0
    kern = functools.partial(_ssd_in_kernel, tm=tm, tc=tc, t_seq=t_seq, di=di)
    return pl.pallas_call(
        kern,
        grid=(m // tm,),
        in_specs=[pl.BlockSpec((tm, k), lambda i: (i, 0)), _resident_layer(w, layer),
                  _resident(conv_w.shape), _resident((1, cdim))],
        out_specs=[pl.BlockSpec((tm, n), lambda i: (i, 0)), pl.BlockSpec((1, SUBLANES, cdim), lambda i: (i, 0, 0))],
        out_shape=[jax.ShapeDtypeStruct((m, n), f32), jax.ShapeDtypeStruct((m // tm, SUBLANES, cdim), f32)],
        scratch_shapes=[pltpu.VMEM((SUBLANES, cdim), f32)],
        compiler_params=_cparams(1),
        name="ssd_in_conv",
    )(xn, w, conv_w, conv_b.reshape(1, cdim))


def _ffn_kernel(*refs, tm, tf, t_seq, t_valid, has_state):
    if has_state:
        (act_ref, wo_ref, bo_ref, x_ref, nw1_ref, wu_ref, cw_ref, cb_ref, wd_ref, nw2_ref, buf_ref,
         xo_ref, xn2_ref, nst_ref, carry_ref) = refs
    else:
        (act_ref, wo_ref, bo_ref, x_ref, nw1_ref, wu_ref, cw_ref, cb_ref, wd_ref, nw2_ref,
         xo_ref, xn2_ref, tail_ref, carry_ref) = refs
    i = pl.program_id(0)
    dff = wd_ref.shape[0]

    x1 = jnp.dot(act_ref[...].astype(bf16), wo_ref[...], preferred_element_type=f32) + bo_ref[...] + x_ref[...]
    xn = _rms(x1, nw1_ref[...]).astype(bf16)
    rows8 = lax.broadcasted_iota(jnp.int32, (SUBLANES, 1), 0)
    if not has_state:
        @pl.when(((i * tm) & (t_seq - 1)) == 0)
        def _():
            carry_ref[...] = jnp.zeros_like(carry_ref)
    gs = []
    for c0 in range(0, dff, tf):
        cs = slice(c0, c0 + tf)
        a = jnp.dot(xn, wu_ref[:, cs], preferred_element_type=f32)
        b = jnp.dot(xn, wu_ref[:, dff + c0:dff + c0 + tf], preferred_element_type=f32)
        r1 = pltpu.roll(a, 1, 0)
        r2 = pltpu.roll(a, 2, 0)
        if has_state:
            p1s, p2s = [], []
            for s in range(tm // SUBLANES):
                rs = slice(s * SUBLANES, (s + 1) * SUBLANES)
                b0, b1 = buf_ref[s, 0:1, cs], buf_ref[s, 1:2, cs]
                p1s.append(jnp.where(rows8 == 0, b1, r1[rs, :]))
                p2s.append(jnp.where(rows8 == 0, b0, jnp.where(rows8 == 1, b1, r2[rs, :])))
                nst_ref[s, :, cs] = a[s * SUBLANES + t_valid - 2:s * SUBLANES + t_valid, :]
            p1 = jnp.concatenate(p1s, axis=0)
            p2 = jnp.concatenate(p2s, axis=0)
        else:
            tail_ref[0, :, cs] = a[tm - SUBLANES:, :]
            carry = carry_ref[:, cs]
            top1 = jnp.where(rows8 == 0, carry[7:8, :], r1[:SUBLANES])
            top2 = jnp.where(rows8 == 0, carry[6:7, :], jnp.where(rows8 == 1, carry[7:8, :], r2[:SUBLANES]))
            p1 = jnp.concatenate([top1, r1[SUBLANES:]], axis=0)
            p2 = jnp.concatenate([top2, r2[SUBLANES:]], axis=0)
            carry_ref[:, cs] = a[tm - SUBLANES:, :]
        y = cb_ref[:, cs] + p2 * cw_ref[0:1, cs]
        y = y + p1 * cw_ref[1:2, cs]
        y = y + a * cw_ref[2:3, cs]
        gs.append((_silu(y) * b).astype(bf16))
    g = jnp.concatenate(gs, axis=1)
    xo = jnp.dot(g, wd_ref[...], preferred_element_type=f32) + x1
    xo_ref[...] = xo
    xn2_ref[...] = _rms(xo, nw2_ref[...]).astype(xn2_ref.dtype)


def ffn_call(act, w_o, b_o, x, ffn_norm_w, w_up, conv_w, conv_b, w_down, next_w, t_seq, buf, t_valid, out_dtype, name):
    m, d = x.shape
    ko = act.shape[1]
    dff = w_down[0].shape[1]
    tf = 2 * LANES
    assert dff % tf == 0 and (t_seq & (t_seq - 1)) == 0
    has_state = buf is not None
    tm = min(FFN_TM // 2 if has_state else FFN_TM, m)
    assert (t_seq == SUBLANES and t_valid >= 2 and conv_w.shape[0] == 3) if has_state else (t_seq % tm == 0)
    if b_o is None:
        b_o = jnp.zeros((d,), f32)
    row = lambda w: pl.BlockSpec((tm, w), lambda i: (i, 0))
    in_specs = [row(ko), _resident_layer(*w_o), _resident((1, d)), row(d), _resident((1, d)),
                _resident_layer(*w_up), _resident((3, dff)), _resident((1, dff)),
                _resident_layer(*w_down), _resident((1, d))]
    args = [act, w_o[0], b_o.reshape(1, d), x, ffn_norm_w.reshape(1, d),
            w_up[0], conv_w, conv_b.reshape(1, dff), w_down[0], next_w.reshape(1, d)]
    if has_state:
        nseq = tm // t_seq
        in_specs.append(pl.BlockSpec((nseq, 2, dff), lambda i: (i, 0, 0)))
        args.append(buf)
        third_spec = pl.BlockSpec((nseq, 2, dff), lambda i: (i, 0, 0))
        third_shape = jax.ShapeDtypeStruct((m // t_seq, 2, dff), f32)
    else:
        third_spec = pl.BlockSpec((1, SUBLANES, dff), lambda i: (i, 0, 0))
        third_shape = jax.ShapeDtypeStruct((m // tm, SUBLANES, dff), f32)
    kern = functools.partial(_ffn_kernel, tm=tm, tf=tf, t_seq=t_seq, t_valid=t_valid, has_state=has_state)
    return pl.pallas_call(
        kern,
        grid=(m // tm,),
        in_specs=in_specs,
        out_specs=[row(d), row(d), third_spec],
        out_shape=[jax.ShapeDtypeStruct((m, d), f32), jax.ShapeDtypeStruct((m, d), out_dtype), third_shape],
        scratch_shapes=[pltpu.VMEM((SUBLANES, dff), f32)],
        compiler_params=_cparams(1),
        name=name,
    )(*args)


def _attn_softmax_pv(s, sink_col, vj):
    m = jnp.maximum(jnp.max(s, axis=-1, keepdims=True), sink_col)
    p = jnp.exp(s - m)
    l = jnp.sum(p, axis=-1, keepdims=True) + jnp.exp(sink_col - m)
    return _dot(p, vj) / l


def _sink_col(sinks_ref, j, group, rows_per_head):
    r = lax.broadcasted_iota(jnp.int32, (group * rows_per_head, 1), 0)
    col = jnp.full((group * rows_per_head, 1), sinks_ref[j * group], f32)
    for p in range(1, group):
        col = jnp.where(r >= p * rows_per_head, sinks_ref[j * group + p], col)
    return col


def _attn_prompt_kernel(sinks_ref, q_ref, kvc_ref, kvp_ref, o_ref, *, n_heads):
    n = pl.program_id(1)
    w, hd, kvh = WINDOW, ATT_HEAD_DIM, ATT_KV_HEADS
    group = n_heads // kvh
    kvw = kvh * hd
    assert group == kvh
    kv = jnp.concatenate([kvp_ref[...], kvc_ref[...]], axis=0)
    k2 = kv[:, :kvw]
    v2 = kv[:, kvw:]
    i = lax.broadcasted_iota(jnp.int32, (w, 2 * w), 0)
    c = lax.broadcasted_iota(jnp.int32, (w, 2 * w), 1)
    band = (c > i) & (c <= i + w) & ((n > 0) | (c >= w))
    mask = jnp.concatenate([band] * group, axis=0)
    slot = lax.broadcasted_iota(jnp.int32, (w, kvw), 1) >> (hd.bit_length() - 1)
    for j in range(kvh):
        qg = q_ref[:, j * kvw:(j + 1) * kvw] * (hd ** -0.5)
        qs = jnp.concatenate([jnp.where(slot == j, pltpu.roll(qg, ((j - p) % kvh) * hd, 1), 0)
                              for p in range(group)], axis=0)
        s = jnp.where(mask, _dot_nt(qs, k2), -jnp.inf)
        of = _attn_softmax_pv(s, _sink_col(sinks_ref, j, group, w), v2).astype(o_ref.dtype)
        og = of[0:w, :]
        for p in range(group):
            piece = of[p * w:(p + 1) * w, :]
            og = jnp.where(slot == p, pltpu.roll(piece, ((p - j) % kvh) * hd, 1), og)
        o_ref[:, j * kvw:(j + 1) * kvw] = og


def attn_prompt_call(qkv, sinks, bsz, t, n_heads):
    w, hd, kvh = WINDOW, ATT_HEAD_DIM, ATT_KV_HEADS
    nb = t // w
    qd = n_heads * hd
    kvd = 2 * kvh * hd
    assert qd % kvd == 0
    kvblk = qd // kvd
    kern = functools.partial(_attn_prompt_kernel, n_heads=n_heads)
    return pl.pallas_call(
        kern,
        grid=(bsz, nb),
        in_specs=[pl.BlockSpec(memory_space=pltpu.SMEM),
                  pl.BlockSpec((w, qd), lambda b, n: (b * nb + n, 0)),
                  pl.BlockSpec((w, kvd), lambda b, n: (b * nb + n, kvblk)),
                  pl.BlockSpec((w, kvd), lambda b, n: (b * nb + jnp.maximum(n - 1, 0), kvblk))],
        out_specs=pl.BlockSpec((w, qd), lambda b, n: (b * nb + n, 0)),
        out_shape=jax.ShapeDtypeStruct((bsz * t, qd), bf16),
        compiler_params=_cparams(2),
        name="attn_prompt",
    )(sinks, qkv, qkv, qkv)


def _shifted_window(cache, new_rows, t_valid):
    tp, w = new_rows.shape[0], cache.shape[1]
    hi = new_rows.astype(bf16)
    r1 = new_rows - hi.astype(f32)
    mid = r1.astype(bf16)
    lo = (r1 - mid.astype(f32)).astype(bf16)
    t = lax.broadcasted_iota(jnp.int32, (tp, w), 0)
    col = lax.broadcasted_iota(jnp.int32, (tp, w), 1)
    place = ((col == w - t_valid + t) & (t < t_valid)).astype(bf16)
    placed = _dot_tn(jnp.concatenate([hi, mid, lo], axis=0), jnp.concatenate([place] * 3, axis=0))
    lane = lax.broadcasted_iota(jnp.int32, cache.shape, 1)
    return jnp.where(lane >= w - t_valid, placed, pltpu.roll(cache, w - t_valid, 1))


def _attn_cache_kernel(sinks_ref, q_ref, ck_ref, cv_ref, o_ref, nk_ref, nv_ref, *, n_heads, t_valid, bt):
    w, hd, kvh = WINDOW, ATT_HEAD_DIM, ATT_KV_HEADS
    group = n_heads // kvh
    tp = q_ref.shape[1]
    qd = n_heads * hd
    tq = lax.broadcasted_iota(jnp.int32, (tp, w), 0)
    jc = lax.broadcasted_iota(jnp.int32, (tp, w), 1)
    mc = (jc > tq) & (jc - w + PAST_LEN >= 0)
    tq2 = lax.broadcasted_iota(jnp.int32, (tp, tp), 0)
    un = lax.broadcasted_iota(jnp.int32, (tp, tp), 1)
    mn = (un <= tq2) & (un < t_valid)
    mask_c = jnp.concatenate([mc] * n_heads, axis=0)
    mask_n = jnp.concatenate([mn] * n_heads, axis=0)
    kvw = kvh * hd
    assert group == kvh
    slot = lax.broadcasted_iota(jnp.int32, (tp, kvw), 1) >> (hd.bit_length() - 1)
    sink = _sink_col(sinks_ref, 0, n_heads, tp)

    for bi in range(bt):
        row = q_ref[bi]
        ck = ck_ref[bi]
        cv = cv_ref[bi]
        kn = row[:, qd:qd + kvw]
        vn = row[:, qd + kvw:]
        nk_ref[bi] = _shifted_window(ck, kn, t_valid)
        nv_ref[bi] = _shifted_window(cv, vn, t_valid)
        pieces = []
        for j in range(kvh):
            qg = row[:, j * kvw:(j + 1) * kvw] * (hd ** -0.5)
            for p in range(group):
                pieces.append(jnp.where(slot == j, pltpu.roll(qg, ((j - p) % kvh) * hd, 1), 0.0))
        qs = jnp.concatenate(pieces, axis=0)
        sc = jnp.where(mask_c, _dot(qs, ck), -jnp.inf)
        sn = jnp.where(mask_n, _dot_nt(qs, kn), -jnp.inf)
        m = jnp.maximum(jnp.maximum(jnp.max(sc, axis=-1, keepdims=True),
                                    jnp.max(sn, axis=-1, keepdims=True)), sink)
        pc = jnp.exp(sc - m)
        pn = jnp.exp(sn - m)
        l = jnp.sum(pc, axis=-1, keepdims=True) + jnp.sum(pn, axis=-1, keepdims=True) + jnp.exp(sink - m)
        of = (_dot_nt(pc, cv) + _dot(pn, vn)) / l
        for j in range(kvh):
            og = jnp.zeros((tp, kvw), f32)
            for p in range(group):
                h = j * group + p
                og = og + jnp.where(slot == p, pltpu.roll(of[h * tp:(h + 1) * tp, :], ((p - j) % kvh) * hd, 1), 0.0)
            o_ref[bi, :, j * kvw:(j + 1) * kvw] = og


def attn_cache_call(qkv, cache_k, cache_v, layer, sinks, n_heads, t_valid):
    bsz, tp, nq = qkv.shape
    w = WINDOW
    kd = cache_k.shape[2]
    qd = n_heads * ATT_HEAD_DIM
    bt = 8
    kern = functools.partial(_attn_cache_kernel, n_heads=n_heads, t_valid=t_valid, bt=bt)
    return pl.pallas_call(
        kern,
        grid=(bsz // bt,),
        in_specs=[pl.BlockSpec(memory_space=pltpu.SMEM),
                  pl.BlockSpec((bt, tp, nq), lambda b: (b, 0, 0)),
                  pl.BlockSpec((None, bt, kd, w), lambda b: (layer, b, 0, 0)),
                  pl.BlockSpec((None, bt, kd, w), lambda b: (layer, b, 0, 0))],
        out_specs=[pl.BlockSpec((bt, tp, qd), lambda b: (b, 0, 0)),
                   pl.BlockSpec((bt, kd, w), lambda b: (b, 0, 0)),
                   pl.BlockSpec((bt, kd, w), lambda b: (b, 0, 0))],
        out_shape=[jax.ShapeDtypeStruct((bsz, tp, qd), f32),
                   jax.ShapeDtypeStruct((bsz, kd, w), f32),
                   jax.ShapeDtypeStruct((bsz, kd, w), f32)],
        compiler_params=_cparams(1),
        name="attn_cache",
    )(sinks, qkv, cache_k, cache_v)


def _hgrn_chunk(x, st, lb, nw, tril, diag_mask, valid, grp):
    nh = HG_HEADS
    c_rows = x.shape[0]
    dim = x.shape[1] // 4
    dk = dim // nh
    ng = c_rows // grp
    if valid is not None:
        x = jnp.where(valid, x, 0.0)
    q = _silu(x[:, 0:dim])
    fg = lb + (1.0 - lb) * jax.nn.sigmoid(x[:, dim:2 * dim])
    logf = jnp.log(fg)
    k = 1.0 - fg
    v = x[:, 2 * dim:3 * dim]
    gate = x[:, 3 * dim:4 * dim]
    if valid is not None:
        logf = jnp.where(valid, logf, 0.0)
        k = jnp.where(valid, k, 0.0)
    bcum = _cumsum_rows(logf, tril.astype(bf16))

    def rows_of(vals):
        return jnp.concatenate([jnp.broadcast_to(r, (grp, dk)) for r in vals], axis=0)

    ys, st_out = [], []
    for h in range(nh):
        sl = slice(h * dk, (h + 1) * dk)
        b = bcum[:, sl]
        qh, kh, vh = q[:, sl], k[:, sl], v[:, sl]
        zero = jnp.zeros((1, dk), f32)
        r = [zero] + [b[i * grp - 1:i * grp, :] for i in range(1, ng)] + [b[c_rows - 1:c_rows, :]]
        mid = [b[i * grp + grp // 2 - 1:i * grp + grp // 2, :] for i in range(ng)]
        b_last = r[ng]
        r_start = rows_of(r[:ng])
        r_end = rows_of(r[1:])
        r_mid = rows_of(mid)
        qg = qh * jnp.exp(b - r_start)
        kt = kh * jnp.exp(r_end - b)
        qm = qh * jnp.exp(b - r_mid)
        km = kh * jnp.exp(r_mid - b)
        a = jnp.where(diag_mask, _dot_nt(qm, km), 0.0)
        if ng > 1:
            lhs, rhs = [], []
            for j in range(ng - 1):
                lhs.append(jnp.concatenate(
                    [jnp.zeros((grp, dk), f32) if i <= j else qg[i * grp:(i + 1) * grp, :] * jnp.exp(r[i] - r[j + 1])
                     for i in range(ng)], axis=0))
                rhs.append(jnp.concatenate(
                    [kt[i * grp:(i + 1) * grp, :] if i == j else jnp.zeros((grp, dk), f32) for i in range(ng)], axis=0))
            a = a + _dot_nt(jnp.concatenate(lhs, axis=1), jnp.concatenate(rhs, axis=1))
        q_inter = qg * jnp.exp(r_start)
        o = _dot(a, vh) + _dot_nt(q_inter, st[h])
        k_dec = kt * jnp.exp(b_last - r_end)
        st_out.append(st[h] * jnp.exp(b_last) + _dot_tn(vh, k_dec))
        on = o * lax.rsqrt(jnp.mean(o * o, axis=-1, keepdims=True) + RMS_EPS)
        ys.append(on * nw[:, sl] * _silu(gate[:, sl]))
    return ys, st_out


def _hgrn_kernel(*refs, c_rows, grp, t_valid, t_pad, layer, has_s0, ub, uc):
    if has_s0:
        p_ref, lbl_ref, nw_ref, s0_ref, y_ref, so_ref, st_ref = refs
    else:
        p_ref, lbl_ref, nw_ref, y_ref, so_ref, st_ref = refs
    c = pl.program_id(1)
    nc = pl.num_programs(1)
    nh = HG_HEADS
    dim = p_ref.shape[1] // 4
    dk = dim // nh

    @pl.when(c == 0)
    def _():
        if has_s0:
            for s in range(ub):
                for h in range(nh):
                    st_ref[s, h] = s0_ref[s, h].T
        else:
            st_ref[...] = jnp.zeros_like(st_ref)

    lbl = lbl_ref[...]
    e = jnp.exp(lbl - jnp.max(lbl, axis=0, keepdims=True))
    sm = e / jnp.sum(e, axis=0, keepdims=True)
    lb = jnp.zeros((1, dim), f32)
    for r in range(1, layer + 1):
        lb = lb + sm[r:r + 1, :]

    tril = _tril_mask(c_rows)
    rr = lax.broadcasted_iota(jnp.int32, (c_rows, c_rows), 0)
    ss = lax.broadcasted_iota(jnp.int32, (c_rows, c_rows), 1)
    gshift = grp.bit_length() - 1
    diag_mask = tril & ((rr >> gshift) == (ss >> gshift))
    nw = nw_ref[...]
    rows = lax.broadcasted_iota(jnp.int32, (c_rows, 1), 0)

    for s in range(ub):
        st = [st_ref[s, h] for h in range(nh)]
        for cc in range(uc):
            r0 = (s * uc + cc) * c_rows
            valid = ((c * uc + cc) * c_rows + rows) < t_valid if t_valid < t_pad else None
            ys, st = _hgrn_chunk(p_ref[r0:r0 + c_rows, :], st, lb, nw, tril, diag_mask, valid, grp)
            for h in range(nh):
                y_ref[r0:r0 + c_rows, h * dk:(h + 1) * dk] = ys[h].astype(y_ref.dtype)
        for h in range(nh):
            st_ref[s, h] = st[h]

    @pl.when(c == nc - 1)
    def _():
        for s in range(ub):
            for h in range(nh):
                so_ref[s, h] = st_ref[s, h].T


def _units_per_step(bsz, nc, max_seqs, max_chunks):
    if nc == 1:
        return max(u for u in (1, 2, 4, 8) if u <= max_seqs and bsz % u == 0), 1
    return 1, max(u for u in (1, 2, 4, 8) if u <= max_chunks and nc % u == 0)


def hgrn_call(p, lb_logits, norm_w, s0, bsz, t_pad, t_valid, layer):
    dim = p.shape[1] // 4
    nh = HG_HEADS
    dk = dim // nh
    c_rows = min(128, t_pad)
    grp = min(HG_CHUNK_GROUP, c_rows)
    nc = t_pad // c_rows
    ub, uc = _units_per_step(bsz, nc, 8, 4)
    rows = ub * uc * c_rows
    ncs = nc // uc
    has_s0 = s0 is not None
    in_specs = [pl.BlockSpec((rows, 4 * dim), lambda b, c: (b * ncs + c, 0)),
                pl.BlockSpec(lb_logits.shape, lambda b, c: (0, 0)),
                pl.BlockSpec((1, dim), lambda b, c: (0, 0))]
    args = [p, lb_logits, norm_w.reshape(1, dim)]
    if has_s0:
        in_specs.append(pl.BlockSpec((ub, nh, dk, dk), lambda b, c: (b, 0, 0, 0)))
        args.append(s0)
    kern = functools.partial(_hgrn_kernel, c_rows=c_rows, grp=grp, t_valid=t_valid, t_pad=t_pad,
                             layer=layer, has_s0=has_s0, ub=ub, uc=uc)
    return pl.pallas_call(
        kern,
        grid=(bsz // ub, ncs),
        in_specs=in_specs,
        out_specs=[pl.BlockSpec((rows, dim), lambda b, c: (b * ncs + c, 0)),
                   pl.BlockSpec((ub, nh, dk, dk), lambda b, c: (b, 0, 0, 0))],
        out_shape=[jax.ShapeDtypeStruct((bsz * t_pad, dim), bf16),
                   jax.ShapeDtypeStruct((bsz, nh, dk, dk), f32)],
        scratch_shapes=[pltpu.VMEM((ub, nh, dk, dk), f32)],
        compiler_params=_cparams(2),
        name="hgrn",
    )(*args)


def _expand_heads(w, ex_ref):
    hi = w.astype(bf16)
    r1 = w - hi.astype(f32)
    mid = r1.astype(bf16)
    lo = (r1 - mid.astype(f32)).astype(bf16)
    return jnp.dot(jnp.concatenate([hi, mid, lo], axis=1), ex_ref[...], preferred_element_type=f32)


def _ssd_kernel(*refs, c_rows, t_valid, t_pad, has_s0, ub, uc, conv_done):
    if has_s0:
        (zx_ref, cw_ref, cb_ref, dtb_ref, alog_ref, dsk_ref, nw_ref, ex_ref, s0_ref, buf_ref,
         y_ref, so_ref, st_ref, carry_ref) = refs
    else:
        (zx_ref, cw_ref, cb_ref, dtb_ref, alog_ref, dsk_ref, nw_ref, ex_ref,
         y_ref, so_ref, st_ref, carry_ref) = refs
    c = pl.program_id(1)
    nc = pl.num_programs(1)
    hp, ns, ngr = SSD_HEAD_DIM, SSD_STATE, SSD_GROUPS
    di = y_ref.shape[1]
    nh = di // hp
    hpg = nh // ngr
    gw = ngr * ns
    kconv = cw_ref.shape[0]
    pair = 2 * hp
    assert pair == LANES and hpg % 2 == 0

    @pl.when(c == 0)
    def _():
        carry_ref[...] = jnp.zeros_like(carry_ref)
        if has_s0:
            for s in range(ub):
                for r in range(di // LANES):
                    st_ref[s, :, r * LANES:(r + 1) * LANES] = s0_ref[s, r * LANES:(r + 1) * LANES, :].T
                carry_ref[s, SUBLANES - (kconv - 1):, :] = buf_ref[s]
        else:
            st_ref[...] = jnp.zeros_like(st_ref)

    cdim = cw_ref.shape[1]
    rows = lax.broadcasted_iota(jnp.int32, (c_rows, 1), 0)
    rows8 = rows[:SUBLANES]
    cw = cw_ref[...]
    a_neg = -jnp.exp(alog_ref[...])
    tril = _tril_mask(c_rows)
    dsk = dsk_ref[...]
    nw = nw_ref[...]
    lo = lax.broadcasted_iota(jnp.int32, (c_rows, LANES), 1) < hp
    for s in range(ub):
        for cc in range(uc):
            r0 = (s * uc + cc) * c_rows
            valid = ((c * uc + cc) * c_rows + rows) < t_valid if t_valid < t_pad else None
            _ssd_chunk(zx_ref[r0:r0 + c_rows, :], st_ref.at[s], carry_ref.at[s], y_ref.at[r0:r0 + c_rows, :],
                       cw, cb_ref[...], dtb_ref[...], a_neg, dsk, nw, ex_ref, tril, lo, rows8, valid, conv_done)

    @pl.when(c == nc - 1)
    def _():
        for s in range(ub):
            for r in range(di // LANES):
                so_ref[s, r * LANES:(r + 1) * LANES, :] = st_ref[s, :, r * LANES:(r + 1) * LANES].T


def _ssd_chunk(zx, st_ref, carry_ref, y_ref, cw, cb, dtb, a_neg, dsk, nw, ex_ref, tril, lo, rows8, valid, conv_done):
    hp, ns, ngr = SSD_HEAD_DIM, SSD_STATE, SSD_GROUPS
    c_rows = zx.shape[0]
    di = y_ref.shape[1]
    nh = di // hp
    hpg = nh // ngr
    gw = ngr * ns
    kconv, cdim = cw.shape
    pair = 2 * hp
    z = zx[:, :di]
    u = zx[:, di:di + cdim]
    if conv_done:
        xbc = u
    else:
        xbc = _silu(_causal_conv(u, carry_ref[...], cw, cb, rows8))
        carry_ref[...] = u[c_rows - SUBLANES:, :]
    dt = jax.nn.softplus(zx[:, di + cdim:] + dtb)
    if valid is not None:
        xbc = jnp.where(valid, xbc, 0.0)
        dt = jnp.where(valid, dt, 0.0)
    xs = xbc[:, :di]
    bm = xbc[:, di:di + gw]
    cm = xbc[:, di + gw:di + 2 * gw]
    acum = _cumsum_rows(dt * a_neg, tril.astype(bf16))

    def head_rows(a):
        if c_rows < LANES:
            a = jnp.concatenate([a, jnp.zeros((LANES - c_rows, LANES), f32)], axis=0)
        return a.T[:, :c_rows]

    acum_t = head_rows(acum)
    dt_t = head_rows(dt)
    a_last = acum[c_rows - 1:c_rows, :]
    xd = xs * _expand_heads(dt * jnp.exp(a_last - acum), ex_ref)

    ys = []
    for g in range(ngr):
        bg = bm[:, g * ns:(g + 1) * ns]
        cg = cm[:, g * ns:(g + 1) * ns]
        cb_mat = _dot_nt(cg, bg)
        gsl = slice(g * hpg * hp, (g + 1) * hpg * hp)
        y_inter = _dot(cg, st_ref[:, gsl])
        decs = []
        for q in range(hpg // 2):
            h1 = g * hpg + 2 * q
            psl = slice(h1 * hp, h1 * hp + pair)
            ms, es = [], []
            for h in (h1, h1 + 1):
                colb = jnp.broadcast_to(acum[:, h:h + 1], (c_rows, LANES))
                seg = colb[:, :c_rows] - acum_t[h:h + 1, :]
                ms.append(cb_mat * jnp.exp(jnp.where(tril, seg, -jnp.inf)) * dt_t[h:h + 1, :])
                es.append(jnp.exp(colb))
            xp = xs[:, psl]
            x_lo = jnp.where(lo, xp, 0.0)
            x_hi = jnp.where(lo, 0.0, xp)
            if (2 * c_rows) % LANES == 0:
                y_intra = _dot(jnp.concatenate(ms, axis=1), jnp.concatenate([x_lo, x_hi], axis=0))
            else:
                y_intra = _dot(ms[0], x_lo) + _dot(ms[1], x_hi)
            y_pair = y_intra + y_inter[:, q * pair:(q + 1) * pair] * jnp.where(lo, es[0], es[1])
            ys.append(y_pair + dsk[:, psl] * xp)
            decs.append(jnp.where(lo[0:1, :], jnp.exp(a_last[:, h1:h1 + 1]), jnp.exp(a_last[:, h1 + 1:h1 + 2])))
        st_ref[:, gsl] = st_ref[:, gsl] * jnp.concatenate(decs, axis=1) + _dot_tn(bg, xd[:, gsl])

    y = jnp.concatenate(ys, axis=1) * _silu(z)
    gdim = di // ngr
    for g in range(ngr):
        sl = slice(g * gdim, (g + 1) * gdim)
        yg = y[:, sl]
        y_ref[:, sl] = (yg * lax.rsqrt(jnp.mean(yg * yg, axis=-1, keepdims=True) + RMS_EPS) * nw[:, sl]).astype(y_ref.dtype)


def ssd_call(zx, conv_w, conv_b, dt_bias, a_log, d_skip, norm_w, s0, conv_buf, bsz, t_pad, t_valid, conv_done):
    hp, ns = SSD_HEAD_DIM, SSD_STATE
    conv_dim = conv_w.shape[1]
    di = zx.shape[1] - conv_dim - LANES
    nh = di // hp
    c_rows = min(128, t_pad)
    nc = t_pad // c_rows
    ub, uc = _units_per_step(bsz, nc, 4, 2)
    rows = ub * uc * c_rows
    ncs = nc // uc
    has_s0 = s0 is not None
    assert not (conv_done and has_s0)
    pad_h = lambda a: jnp.pad(a.reshape(1, nh), ((0, 0), (0, LANES - nh)))
    full = lambda shape: pl.BlockSpec(shape, lambda b, c: (0,) * len(shape))
    sel = (jnp.arange(di)[None, :] // hp == jnp.arange(LANES)[:, None]).astype(bf16)
    expand = jnp.concatenate([sel, sel, sel], axis=0)
    in_specs = [pl.BlockSpec((rows, zx.shape[1]), lambda b, c: (b * ncs + c, 0)),
                full(conv_w.shape), full((1, conv_dim)), full((1, LANES)), full((1, LANES)),
                full((1, di)), full((1, di)), full(expand.shape)]
    args = [zx, conv_w, conv_b.reshape(1, conv_dim), pad_h(dt_bias), pad_h(a_log),
            jnp.repeat(d_skip, hp).reshape(1, di), norm_w.reshape(1, di), expand]
    if has_s0:
        in_specs += [pl.BlockSpec((ub, di, ns), lambda b, c: (b, 0, 0)),
                     pl.BlockSpec((ub,) + conv_buf.shape[1:], lambda b, c: (b, 0, 0))]
        args += [s0.reshape(bsz, di, ns), conv_buf]
    kern = functools.partial(_ssd_kernel, c_rows=c_rows, t_valid=t_valid, t_pad=t_pad, has_s0=has_s0, ub=ub, uc=uc,
                             conv_done=conv_done)
    y, s_new = pl.pallas_call(
        kern,
        grid=(bsz // ub, ncs),
        in_specs=in_specs,
        out_specs=[pl.BlockSpec((rows, di), lambda b, c: (b * ncs + c, 0)),
                   pl.BlockSpec((ub, di, ns), lambda b, c: (b, 0, 0))],
        out_shape=[jax.ShapeDtypeStruct((bsz * t_pad, di), bf16),
                   jax.ShapeDtypeStruct((bsz, di, ns), f32)],
        scratch_shapes=[pltpu.VMEM((ub, ns, di), f32), pltpu.VMEM((ub, SUBLANES, conv_dim), f32)],
        compiler_params=_cparams(2),
        name="ssd",
    )(*args)
    return y, s_new.reshape(bsz, nh, hp, ns)


def _trunk(x, c_k, c_v, s_hg, s_ssm, s_sconv, s_fconv, prm):
    bsz, t, d = x.shape
    depth = prm["norm_mix_w"].shape[0]
    decode = c_k is not None
    tp = -(-t // SUBLANES) * SUBLANES
    if tp != t:
        x = jnp.pad(x, ((0, 0), (0, tp - t), (0, 0)))
    m = bsz * tp
    n_heads = d // ATT_HEAD_DIM
    qd = n_heads * ATT_HEAD_DIM
    kvd = ATT_KV_HEADS * ATT_HEAD_DIM
    x2 = x.reshape(m, d)
    xn, xn_w = x2, prm["norm_mix_w"][0]
    nk, nv, nhg, nssm, nsconv, nfconv = [], [], [], [], [], []
    if decode:
        ck_all = jnp.transpose(c_k, (0, 1, 3, 4, 2)).reshape(c_k.shape[0], bsz, kvd, WINDOW)
        cv_all = jnp.transpose(c_v, (0, 1, 3, 4, 2)).reshape(c_v.shape[0], bsz, kvd, WINDOW)
    slot = [0] * N_MIXERS
    for i in range(depth):
        kind = i % N_MIXERS
        j = slot[kind]
        slot[kind] += 1
        nw_ffn = prm["norm_ffn_w"][i]
        if kind == 0:
            w_qkv, b_qkv = prm["attn_w_qkv"], prm["attn_b_qkv"][j]
            nq = w_qkv.shape[2]
            if decode:
                qkv = proj_call(xn, xn_w, w_qkv, j, b_qkv, f32, "attn_qkv").reshape(bsz, tp, nq)
                o, k_new, v_new = attn_cache_call(qkv, ck_all, cv_all, j, prm["attn_sinks"][j], n_heads, t)
                o = o.reshape(m, qd)
            else:
                qkv = proj_call(xn, xn_w, w_qkv, j, b_qkv, bf16, "attn_qkv")
                o = attn_prompt_call(qkv, prm["attn_sinks"][j], bsz, t, n_heads)
                xn_last = xn.reshape(bsz, t, d)[:, t - WINDOW:].reshape(bsz * WINDOW, d)
                kv_last = proj_call(xn_last, xn_w, w_qkv[:, :, qd:], j, b_qkv[qd:], f32,
                                    "attn_kv_tail").reshape(bsz, WINDOW, 2, kvd)
                to4 = lambda a: a.reshape(bsz, WINDOW, ATT_KV_HEADS, ATT_HEAD_DIM)
                k_new, v_new = to4(kv_last[:, :, 0]), to4(kv_last[:, :, 1])
            nk.append(k_new)
            nv.append(v_new)
            act, w_o, b_o = o, (prm["attn_w_o"], j), prm["attn_b_o"][j]
        elif kind == 1:
            p = proj_call(xn, xn_w, prm["hgrn_w_in"], j, None, f32, "hgrn_in")
            y, s_new = hgrn_call(p, prm["hgrn_lb_logits"], prm["hgrn_norm_w"][j],
                                 None if s_hg is None else s_hg[j], bsz, tp, t, i)
            nhg.append(s_new)
            act, w_o, b_o = y, (prm["hgrn_w_o"], j), None
        else:
            conv_dim = prm["ssd_conv_w"].shape[2]
            di = prm["ssd_w_o"].shape[1]
            kc = prm["ssd_conv_w"].shape[1]
            if s_sconv is None:
                zx, pre = ssd_in_call(xn, prm["ssd_w_in"], j, prm["ssd_conv_w"][j], prm["ssd_conv_b"][j], tp)
                pre = pre.reshape(bsz, tp // min(PROJ_TM, m), SUBLANES, conv_dim)[:, -1, SUBLANES - (kc - 1):]
            else:
                zx = proj_call(xn, xn_w, prm["ssd_w_in"], j, None, f32, "ssd_in")
                pre = zx.reshape(bsz, tp, zx.shape[1])[:, max(t - (kc - 1), 0):t, di:di + conv_dim]
                pre = jnp.concatenate([s_sconv[j], pre], axis=1)
                pre = pre[:, pre.shape[1] - (kc - 1):]
            nsconv.append(pre)
            y, s_new = ssd_call(zx, prm["ssd_conv_w"][j], prm["ssd_conv_b"][j], prm["ssd_dt_bias"][j],
                                prm["ssd_a_log"][j], prm["ssd_d"][j], prm["ssd_norm_w"][j],
                                None if s_ssm is None else s_ssm[j],
                                None if s_sconv is None else s_sconv[j], bsz, tp, t, s_sconv is None)
            nssm.append(s_new)
            act, w_o, b_o = y, (prm["ssd_w_o"], j), None
        last = i == depth - 1
        next_w = prm["norm_final_w"] if last else prm["norm_mix_w"][i + 1]
        dff = prm["ffn_w_down"].shape[1]
        kf = prm["ffn_conv_w"].shape[1]
        x2, xn, tail = ffn_call(act, w_o, b_o, x2, nw_ffn, (prm["ffn_w_up"], i), prm["ffn_conv_w"][i],
                                prm["ffn_conv_b"][i], (prm["ffn_w_down"], i), next_w, tp,
                                None if s_fconv is None else s_fconv[i], t, f32 if last else bf16, "ffn")
        xn_w = None
        if s_fconv is not None:
            nfconv.append(tail)
        else:
            tiles_per_seq = tp // tail_tile_rows(m)
            tl = tail.reshape(bsz, tiles_per_seq, SUBLANES, dff)[:, -1]
            nfconv.append(tl[:, SUBLANES - (kf - 1):])
    stack = lambda xs: xs[0][None] if len(xs) == 1 else jnp.stack(xs)
    y = xn.reshape(bsz, tp, d)[:, :t]
    nk, nv = stack(nk), stack(nv)
    if decode:
        unview = lambda a: jnp.transpose(a.reshape(a.shape[0], bsz, ATT_KV_HEADS, ATT_HEAD_DIM, WINDOW), (0, 1, 4, 2, 3))
        nk, nv = unview(nk), unview(nv)
    return (y, nk, nv, stack(nhg), stack(nssm), stack(nsconv), stack(nfconv))


def tail_tile_rows(m):
    return min(FFN_TM, m)


def kernel(x_prompt, x_sample, cache_attn_k, cache_attn_v, state_hgrn, state_ssm, state_ssm_conv, state_ffn_conv, norm_mix_w, norm_ffn_w, norm_final_w, attn_w_qkv, attn_b_qkv, attn_sinks, attn_w_o, attn_b_o, hgrn_w_in, hgrn_lb_logits, hgrn_norm_w, hgrn_w_o, ssd_w_in, ssd_conv_w, ssd_conv_b, ssd_dt_bias, ssd_a_log, ssd_d, ssd_norm_w, ssd_w_o, ffn_w_up, ffn_conv_w, ffn_conv_b, ffn_w_down):
    cast = lambda w: w.astype(bf16)
    ssd_in_pad = (-ssd_w_in.shape[2]) % LANES
    prm = dict(norm_mix_w=norm_mix_w, norm_ffn_w=norm_ffn_w, norm_final_w=norm_final_w,
               attn_w_qkv=cast(attn_w_qkv), attn_b_qkv=attn_b_qkv, attn_sinks=attn_sinks,
               attn_w_o=cast(attn_w_o), attn_b_o=attn_b_o,
               hgrn_w_in=cast(hgrn_w_in), hgrn_lb_logits=hgrn_lb_logits, hgrn_norm_w=hgrn_norm_w,
               hgrn_w_o=cast(hgrn_w_o),
               ssd_w_in=jnp.pad(cast(ssd_w_in), ((0, 0), (0, 0), (0, ssd_in_pad))), ssd_conv_w=ssd_conv_w,
               ssd_conv_b=ssd_conv_b, ssd_dt_bias=ssd_dt_bias,
               ssd_a_log=ssd_a_log, ssd_d=ssd_d, ssd_norm_w=ssd_norm_w, ssd_w_o=cast(ssd_w_o),
               ffn_w_up=cast(ffn_w_up), ffn_conv_w=ffn_conv_w, ffn_conv_b=ffn_conv_b, ffn_w_down=cast(ffn_w_down))
    outs_p = _trunk(x_prompt, None, None, None, None, None, None, prm)
    outs_s = _trunk(x_sample, cache_attn_k, cache_attn_v, state_hgrn, state_ssm, state_ssm_conv,
                    state_ffn_conv, prm)
    return (outs_p[0], outs_s[0]) + outs_p[1:] + outs_s[1:]
```

```python
import functools

import jax
import jax.numpy as jnp
from jax import lax
from jax.experimental import pallas as pl
from jax.experimental.pallas import tpu as pltpu

bf16 = jnp.bfloat16
f32 = jnp.float32

RMS_EPS = 1e-6
N_MIXERS = 3
WINDOW = 128
PAST_LEN = 8192
ATT_HEAD_DIM = 64
ATT_KV_HEADS = 4
HG_HEADS = 8
HG_CHUNK_GROUP = 32
SSD_HEAD_DIM = 64
SSD_STATE = 128
SSD_GROUPS = 4

LANES = 128
SUBLANES = 8
VMEM_LIMIT = 56 * 1024 * 1024
PROJ_TM = 512
FFN_TM = 512


def _cparams(n_axes):
    return pltpu.CompilerParams(dimension_semantics=("arbitrary",) * n_axes,
                                vmem_limit_bytes=VMEM_LIMIT)


def _dot(a, b):
    return jnp.dot(a.astype(bf16), b.astype(bf16), preferred_element_type=f32)


def _dot_nt(a, b):
    return lax.dot_general(a.astype(bf16), b.astype(bf16), (((1,), (1,)), ((), ())),
                           preferred_element_type=f32)


def _dot_tn(a, b):
    return lax.dot_general(a.astype(bf16), b.astype(bf16), (((0,), (0,)), ((), ())),
                           preferred_element_type=f32)


def _silu(x):
    return x * jax.nn.sigmoid(x)


def _rms(x, w):
    ms = jnp.mean(x * x, axis=-1, keepdims=True)
    return x * lax.rsqrt(ms + RMS_EPS) * w


def _cumsum_rows(x, tril):
    hi = x.astype(bf16)
    r1 = x - hi.astype(f32)
    mid = r1.astype(bf16)
    lo = (r1 - mid.astype(f32)).astype(bf16)
    d = functools.partial(jnp.dot, preferred_element_type=f32)
    return d(tril, hi) + d(tril, mid) + d(tril, lo)


def _tril_mask(c):
    r = lax.broadcasted_iota(jnp.int32, (c, c), 0)
    s = lax.broadcasted_iota(jnp.int32, (c, c), 1)
    return r >= s


def _resident(shape):
    return pl.BlockSpec(shape, lambda *_: (0,) * len(shape), pipeline_mode=pl.Buffered(1))


def _resident_layer(w, layer):
    return pl.BlockSpec((None,) + w.shape[1:], lambda *_: (layer, 0, 0), pipeline_mode=pl.Buffered(1))


def _proj_kernel(x_ref, nw_ref, w_ref, b_ref, o_ref):
    x = x_ref[...]
    if x.dtype != bf16:
        x = _rms(x, nw_ref[...]).astype(bf16)
    acc = jnp.dot(x, w_ref[...], preferred_element_type=f32)
    o_ref[...] = (acc + b_ref[...]).astype(o_ref.dtype)


def proj_call(x, norm_w, w, layer, bias, out_dtype, name):
    m, k = x.shape
    n = w.shape[2]
    tm = min(PROJ_TM, m)
    if bias is None:
        bias = jnp.zeros((n,), f32)
    if norm_w is None:
        assert x.dtype == bf16
        norm_w = jnp.ones((k,), f32)
    return pl.pallas_call(
        _proj_kernel,
        grid=(m // tm,),
        in_specs=[pl.BlockSpec((tm, k), lambda i: (i, 0)), _resident((1, k)), _resident_layer(w, layer),
                  _resident((1, n))],
        out_specs=pl.BlockSpec((tm, n), lambda i: (i, 0)),
        out_shape=jax.ShapeDtypeStruct((m, n), out_dtype),
        compiler_params=_cparams(1),
        name=name,
    )(x, norm_w.reshape(1, k), w, bias.reshape(1, n))


def _causal_conv(u, prev, cw, cb, rows8):
    kconv = cw.shape[0]
    yc = cb
    for kk in range(kconv - 1):
        back = kconv - 1 - kk
        sh = pltpu.roll(u, back, 0)
        top = sh[:SUBLANES]
        for r0 in range(back):
            top = jnp.where(rows8 == r0, prev[SUBLANES - back + r0:SUBLANES - back + r0 + 1, :], top)
        sh = jnp.concatenate([top, sh[SUBLANES:]], axis=0) if u.shape[0] > SUBLANES else top
        yc = yc + sh * cw[kk:kk + 1, :]
    return yc + u * cw[kconv - 1:kconv, :]


def _ssd_in_kernel(x_ref, w_ref, cw_ref, cb_ref, o_ref, tail_ref, carry_ref, *, tm, tc, t_seq, di):
    i = pl.program_id(0)
    cdim = cw_ref.shape[1]

    @pl.when(((i * tm) & (t_seq - 1)) == 0)
    def _():
        carry_ref[...] = jnp.zeros_like(carry_ref)

    x = x_ref[...]
    rows8 = lax.broadcasted_iota(jnp.int32, (SUBLANES, 1), 0)
    nz = di // tc
    for idx, c0 in enumerate(range(0, cdim, tc)):
        cs = slice(c0, c0 + tc)
        u = jnp.dot(x, w_ref[:, di + c0:di + c0 + tc], preferred_element_type=f32)
        tail_ref[0, :, cs] = u[tm - SUBLANES:, :]
        yc = _causal_conv(u, carry_ref[:, cs], cw_ref[:, cs], cb_ref[:, cs], rows8)
        carry_ref[:, cs] = u[tm - SUBLANES:, :]
        o_ref[:, di + c0:di + c0 + tc] = _silu(yc)
        if idx < nz:
            o_ref[:, idx * tc:(idx + 1) * tc] = jnp.dot(x, w_ref[:, idx * tc:(idx + 1) * tc],
                                                        preferred_element_type=f32)
    o_ref[:, di + cdim:] = jnp.dot(x, w_ref[:, di + cdim:], preferred_element_type=f32)


def ssd_in_call(xn, w, layer, conv_w, conv_b, t_seq):
    m, k = xn.shape
    n = w.shape[2]
    cdim = conv_w.shape[1]
    di = n - cdim - LANES
    tm = min(PROJ_TM, m)
    tc = 2 * LANES
    assert t_seq % tm == 0 and cdim % tc == 0 and di % tc == 0 and di <= cdim and (t_seq & (t_seq - 1)) == 0
    kern = functools.partial(_ssd_in_kernel, tm=tm, tc=tc, t_seq=t_seq, di=di)
    return pl.pallas_call(
        kern,
        grid=(m // tm,),
        in_specs=[pl.BlockSpec((tm, k), lambda i: (i, 0)), _resident_layer(w, layer),
                  _resident(conv_w.shape), _resident((1, cdim))],
        out_specs=[pl.BlockSpec((tm, n), lambda i: (i, 0)), pl.BlockSpec((1, SUBLANES, cdim), lambda i: (i, 0, 0))],
        out_shape=[jax.ShapeDtypeStruct((m, n), f32), jax.ShapeDtypeStruct((m // tm, SUBLANES, cdim), f32)],
        scratch_shapes=[pltpu.VMEM((SUBLANES, cdim), f32)],
        compiler_params=_cparams(1),
        name="ssd_in_conv",
    )(xn, w, conv_w, conv_b.reshape(1, cdim))


def _ffn_kernel(*refs, tm, tf, t_seq, t_valid, has_state):
    if has_state:
        (act_ref, wo_ref, bo_ref, x_ref, nw1_ref, wu_ref, cw_ref, cb_ref, wd_ref, nw2_ref, buf_ref,
         xo_ref, xn2_ref, nst_ref, carry_ref) = refs
    else:
        (act_ref, wo_ref, bo_ref, x_ref, nw1_ref, wu_ref, cw_ref, cb_ref, wd_ref, nw2_ref,
         xo_ref, xn2_ref, tail_ref, carry_ref) = refs
    i = pl.program_id(0)
    dff = wd_ref.shape[0]

    x1 = jnp.dot(act_ref[...].astype(bf16), wo_ref[...], preferred_element_type=f32) + bo_ref[...] + x_ref[...]
    xn = _rms(x1, nw1_ref[...]).astype(bf16)
    rows8 = lax.broadcasted_iota(jnp.int32, (SUBLANES, 1), 0)
    if not has_state:
        @pl.when(((i * tm) & (t_seq - 1)) == 0)
        def _():
            carry_ref[...] = jnp.zeros_like(carry_ref)
    gs = []
    for c0 in range(0, dff, tf):
        cs = slice(c0, c0 + tf)
        a = jnp.dot(xn, wu_ref[:, cs], preferred_element_type=f32)
        b = jnp.dot(xn, wu_ref[:, dff + c0:dff + c0 + tf], preferred_element_type=f32)
        r1 = pltpu.roll(a, 1, 0)
        r2 = pltpu.roll(a, 2, 0)
        if has_state:
            p1s, p2s = [], []
            for s in range(tm // SUBLANES):
                rs = slice(s * SUBLANES, (s + 1) * SUBLANES)
                b0, b1 = buf_ref[s, 0:1, cs], buf_ref[s, 1:2, cs]
                p1s.append(jnp.where(rows8 == 0, b1, r1[rs, :]))
                p2s.append(jnp.where(rows8 == 0, b0, jnp.where(rows8 == 1, b1, r2[rs, :])))
                nst_ref[s, :, cs] = a[s * SUBLANES + t_valid - 2:s * SUBLANES + t_valid, :]
            p1 = jnp.concatenate(p1s, axis=0)
            p2 = jnp.concatenate(p2s, axis=0)
        else:
            tail_ref[0, :, cs] = a[tm - SUBLANES:, :]
            carry = carry_ref[:, cs]
            top1 = jnp.where(rows8 == 0, carry[7:8, :], r1[:SUBLANES])
            top2 = jnp.where(rows8 == 0, carry[6:7, :], jnp.where(rows8 == 1, carry[7:8, :], r2[:SUBLANES]))
            p1 = jnp.concatenate([top1, r1[SUBLANES:]], axis=0)
            p2 = jnp.concatenate([top2, r2[SUBLANES:]], axis=0)
            carry_ref[:, cs] = a[tm - SUBLANES:, :]
        y = cb_ref[:, cs] + p2 * cw_ref[0:1, cs]
        y = y + p1 * cw_ref[1:2, cs]
        y = y + a * cw_ref[2:3, cs]
        gs.append((_silu(y) * b).astype(bf16))
    g = jnp.concatenate(gs, axis=1)
    xo = jnp.dot(g, wd_ref[...], preferred_element_type=f32) + x1
    xo_ref[...] = xo
    xn2_ref[...] = _rms(xo, nw2_ref[...]).astype(xn2_ref.dtype)


def ffn_call(act, w_o, b_o, x, ffn_norm_w, w_up, conv_w, conv_b, w_down, next_w, t_seq, buf, t_valid, out_dtype, name):
    m, d = x.shape
    ko = act.shape[1]
    dff = w_down[0].shape[1]
    tf = 2 * LANES
    assert dff % tf == 0 and (t_seq & (t_seq - 1)) == 0
    has_state = buf is not None
    tm = min(FFN_TM // 2 if has_state else FFN_TM, m)
    assert (t_seq == SUBLANES and t_valid >= 2 and conv_w.shape[0] == 3) if has_state else (t_seq % tm == 0)
    if b_o is None:
        b_o = jnp.zeros((d,), f32)
    row = lambda w: pl.BlockSpec((tm, w), lambda i: (i, 0))
    in_specs = [row(ko), _resident_layer(*w_o), _resident((1, d)), row(d), _resident((1, d)),
                _resident_layer(*w_up), _resident((3, dff)), _resident((1, dff)),
                _resident_layer(*w_down), _resident((1, d))]
    args = [act, w_o[0], b_o.reshape(1, d), x, ffn_norm_w.reshape(1, d),
            w_up[0], conv_w, conv_b.reshape(1, dff), w_down[0], next_w.reshape(1, d)]
    if has_state:
        nseq = tm // t_seq
        in_specs.append(pl.BlockSpec((None, nseq, 2, dff), lambda i: (buf[1], i, 0, 0)))
        args.append(buf[0])
        third_spec = pl.BlockSpec((nseq, 2, dff), lambda i: (i, 0, 0))
        third_shape = jax.ShapeDtypeStruct((m // t_seq, 2, dff), f32)
    else:
        third_spec = pl.BlockSpec((1, SUBLANES, dff), lambda i: (i, 0, 0))
        third_shape = jax.ShapeDtypeStruct((m // tm, SUBLANES, dff), f32)
    kern = functools.partial(_ffn_kernel, tm=tm, tf=tf, t_seq=t_seq, t_valid=t_valid, has_state=has_state)
    return pl.pallas_call(
        kern,
        grid=(m // tm,),
        in_specs=in_specs,
        out_specs=[row(d), row(d), third_spec],
        out_shape=[jax.ShapeDtypeStruct((m, d), f32), jax.ShapeDtypeStruct((m, d), out_dtype), third_shape],
        scratch_shapes=[pltpu.VMEM((SUBLANES, dff), f32)],
        compiler_params=_cparams(1),
        name=name,
    )(*args)


def _attn_softmax_pv(s, sink_col, vj):
    m = jnp.maximum(jnp.max(s, axis=-1, keepdims=True), sink_col)
    p = jnp.exp(s - m)
    l = jnp.sum(p, axis=-1, keepdims=True) + jnp.exp(sink_col - m)
    return _dot(p, vj) / l


def _sink_col(sinks_ref, j, group, rows_per_head):
    r = lax.broadcasted_iota(jnp.int32, (group * rows_per_head, 1), 0)
    col = jnp.full((group * rows_per_head, 1), sinks_ref[j * group], f32)
    for p in range(1, group):
        col = jnp.where(r >= p * rows_per_head, sinks_ref[j * group + p], col)
    return col


def _attn_prompt_kernel(sinks_ref, q_ref, kvc_ref, kvp_ref, o_ref, *, n_heads):
    n = pl.program_id(1)
    w, hd, kvh = WINDOW, ATT_HEAD_DIM, ATT_KV_HEADS
    group = n_heads // kvh
    kvw = kvh * hd
    assert group == kvh
    kv = jnp.concatenate([kvp_ref[...], kvc_ref[...]], axis=0)
    k2 = kv[:, :kvw]
    v2 = kv[:, kvw:]
    i = lax.broadcasted_iota(jnp.int32, (w, 2 * w), 0)
    c = lax.broadcasted_iota(jnp.int32, (w, 2 * w), 1)
    band = (c > i) & (c <= i + w) & ((n > 0) | (c >= w))
    mask = jnp.concatenate([band] * group, axis=0)
    slot = lax.broadcasted_iota(jnp.int32, (w, kvw), 1) >> (hd.bit_length() - 1)
    for j in range(kvh):
        qg = q_ref[:, j * kvw:(j + 1) * kvw] * (hd ** -0.5)
        qs = jnp.concatenate([jnp.where(slot == j, pltpu.roll(qg, ((j - p) % kvh) * hd, 1), 0)
                              for p in range(group)], axis=0)
        s = jnp.where(mask, _dot_nt(qs, k2), -jnp.inf)
        of = _attn_softmax_pv(s, _sink_col(sinks_ref, j, group, w), v2).astype(o_ref.dtype)
        og = of[0:w, :]
        for p in range(group):
            piece = of[p * w:(p + 1) * w, :]
            og = jnp.where(slot == p, pltpu.roll(piece, ((p - j) % kvh) * hd, 1), og)
        o_ref[:, j * kvw:(j + 1) * kvw] = og


def attn_prompt_call(qkv, sinks, bsz, t, n_heads):
    w, hd, kvh = WINDOW, ATT_HEAD_DIM, ATT_KV_HEADS
    nb = t // w
    qd = n_heads * hd
    kvd = 2 * kvh * hd
    assert qd % kvd == 0
    kvblk = qd // kvd
    kern = functools.partial(_attn_prompt_kernel, n_heads=n_heads)
    return pl.pallas_call(
        kern,
        grid=(bsz, nb),
        in_specs=[pl.BlockSpec(memory_space=pltpu.SMEM),
                  pl.BlockSpec((w, qd), lambda b, n: (b * nb + n, 0)),
                  pl.BlockSpec((w, kvd), lambda b, n: (b * nb + n, kvblk)),
                  pl.BlockSpec((w, kvd), lambda b, n: (b * nb + jnp.maximum(n - 1, 0), kvblk))],
        out_specs=pl.BlockSpec((w, qd), lambda b, n: (b * nb + n, 0)),
        out_shape=jax.ShapeDtypeStruct((bsz * t, qd), bf16),
        compiler_params=_cparams(2),
        name="attn_prompt",
    )(sinks, qkv, qkv, qkv)


def _shifted_window(cache, new_rows, t_valid):
    tp, w = new_rows.shape[0], cache.shape[1]
    hi = new_rows.astype(bf16)
    r1 = new_rows - hi.astype(f32)
    mid = r1.astype(bf16)
    lo = (r1 - mid.astype(f32)).astype(bf16)
    t = lax.broadcasted_iota(jnp.int32, (tp, w), 0)
    col = lax.broadcasted_iota(jnp.int32, (tp, w), 1)
    place = ((col == w - t_valid + t) & (t < t_valid)).astype(bf16)
    placed = _dot_tn(jnp.concatenate([hi, mid, lo], axis=0), jnp.concatenate([place] * 3, axis=0))
    lane = lax.broadcasted_iota(jnp.int32, cache.shape, 1)
    return jnp.where(lane >= w - t_valid, placed, pltpu.roll(cache, w - t_valid, 1))


def _attn_cache_kernel(sinks_ref, q_ref, ck_ref, cv_ref, o_ref, nk_ref, nv_ref, *, n_heads, t_valid, bt):
    w, hd, kvh = WINDOW, ATT_HEAD_DIM, ATT_KV_HEADS
    group = n_heads // kvh
    tp = q_ref.shape[1]
    qd = n_heads * hd
    tq = lax.broadcasted_iota(jnp.int32, (tp, w), 0)
    jc = lax.broadcasted_iota(jnp.int32, (tp, w), 1)
    mc = (jc > tq) & (jc - w + PAST_LEN >= 0)
    tq2 = lax.broadcasted_iota(jnp.int32, (tp, tp), 0)
    un = lax.broadcasted_iota(jnp.int32, (tp, tp), 1)
    mn = (un <= tq2) & (un < t_valid)
    mask_c = jnp.concatenate([mc] * n_heads, axis=0)
    mask_n = jnp.concatenate([mn] * n_heads, axis=0)
    kvw = kvh * hd
    assert group == kvh
    slot = lax.broadcasted_iota(jnp.int32, (tp, kvw), 1) >> (hd.bit_length() - 1)
    sink = _sink_col(sinks_ref, 0, n_heads, tp)

    for bi in range(bt):
        row = q_ref[bi]
        ck = ck_ref[bi]
        cv = cv_ref[bi]
        kn = row[:, qd:qd + kvw]
        vn = row[:, qd + kvw:]
        nk_ref[bi] = _shifted_window(ck, kn, t_valid)
        nv_ref[bi] = _shifted_window(cv, vn, t_valid)
        pieces = []
        for j in range(kvh):
            qg = row[:, j * kvw:(j + 1) * kvw] * (hd ** -0.5)
            for p in range(group):
                pieces.append(jnp.where(slot == j, pltpu.roll(qg, ((j - p) % kvh) * hd, 1), 0.0))
        qs = jnp.concatenate(pieces, axis=0)
        sc = jnp.where(mask_c, _dot(qs, ck), -jnp.inf)
        sn = jnp.where(mask_n, _dot_nt(qs, kn), -jnp.inf)
        m = jnp.maximum(jnp.maximum(jnp.max(sc, axis=-1, keepdims=True),
                                    jnp.max(sn, axis=-1, keepdims=True)), sink)
        pc = jnp.exp(sc - m)
        pn = jnp.exp(sn - m)
        l = jnp.sum(pc, axis=-1, keepdims=True) + jnp.sum(pn, axis=-1, keepdims=True) + jnp.exp(sink - m)
        of = (_dot_nt(pc, cv) + _dot(pn, vn)) / l
        for j in range(kvh):
            og = jnp.zeros((tp, kvw), f32)
            for p in range(group):
                h = j * group + p
                og = og + jnp.where(slot == p, pltpu.roll(of[h * tp:(h + 1) * tp, :], ((p - j) % kvh) * hd, 1), 0.0)
            o_ref[bi, :, j * kvw:(j + 1) * kvw] = og


def attn_cache_call(qkv, cache_k, cache_v, layer, sinks, n_heads, t_valid):
    bsz, tp, nq = qkv.shape
    w = WINDOW
    kd = cache_k.shape[2]
    qd = n_heads * ATT_HEAD_DIM
    bt = 8
    kern = functools.partial(_attn_cache_kernel, n_heads=n_heads, t_valid=t_valid, bt=bt)
    return pl.pallas_call(
        kern,
        grid=(bsz // bt,),
        in_specs=[pl.BlockSpec(memory_space=pltpu.SMEM),
                  pl.BlockSpec((bt, tp, nq), lambda b: (b, 0, 0)),
                  pl.BlockSpec((None, bt, kd, w), lambda b: (layer, b, 0, 0)),
                  pl.BlockSpec((None, bt, kd, w), lambda b: (layer, b, 0, 0))],
        out_specs=[pl.BlockSpec((bt, tp, qd), lambda b: (b, 0, 0)),
                   pl.BlockSpec((bt, kd, w), lambda b: (b, 0, 0)),
                   pl.BlockSpec((bt, kd, w), lambda b: (b, 0, 0))],
        out_shape=[jax.ShapeDtypeStruct((bsz, tp, qd), f32),
                   jax.ShapeDtypeStruct((bsz, kd, w), f32),
                   jax.ShapeDtypeStruct((bsz, kd, w), f32)],
        compiler_params=_cparams(1),
        name="attn_cache",
    )(sinks, qkv, cache_k, cache_v)


def _hgrn_chunk(x, st, lb, nw, tril, diag_mask, valid, grp):
    nh = HG_HEADS
    c_rows = x.shape[0]
    dim = x.shape[1] // 4
    dk = dim // nh
    ng = c_rows // grp
    if valid is not None:
        x = jnp.where(valid, x, 0.0)
    q = _silu(x[:, 0:dim])
    fg = lb + (1.0 - lb) * jax.nn.sigmoid(x[:, dim:2 * dim])
    logf = jnp.log(fg)
    k = 1.0 - fg
    v = x[:, 2 * dim:3 * dim]
    gate = x[:, 3 * dim:4 * dim]
    if valid is not None:
        logf = jnp.where(valid, logf, 0.0)
        k = jnp.where(valid, k, 0.0)
    bcum = _cumsum_rows(logf, tril.astype(bf16))

    def rows_of(vals):
        return jnp.concatenate([jnp.broadcast_to(r, (grp, dk)) for r in vals], axis=0)

    ys, st_out = [], []
    for h in range(nh):
        sl = slice(h * dk, (h + 1) * dk)
        b = bcum[:, sl]
        qh, kh, vh = q[:, sl], k[:, sl], v[:, sl]
        zero = jnp.zeros((1, dk), f32)
        r = [zero] + [b[i * grp - 1:i * grp, :] for i in range(1, ng)] + [b[c_rows - 1:c_rows, :]]
        mid = [b[i * grp + grp // 2 - 1:i * grp + grp // 2, :] for i in range(ng)]
        b_last = r[ng]
        r_start = rows_of(r[:ng])
        r_end = rows_of(r[1:])
        r_mid = rows_of(mid)
        qg = qh * jnp.exp(b - r_start)
        kt = kh * jnp.exp(r_end - b)
        qm = qh * jnp.exp(b - r_mid)
        km = kh * jnp.exp(r_mid - b)
        a = jnp.where(diag_mask, _dot_nt(qm, km), 0.0)
        if ng > 1:
            lhs, rhs = [], []
            for j in range(ng - 1):
                lhs.append(jnp.concatenate(
                    [jnp.zeros((grp, dk), f32) if i <= j else qg[i * grp:(i + 1) * grp, :] * jnp.exp(r[i] - r[j + 1])
                     for i in range(ng)], axis=0))
                rhs.append(jnp.concatenate(
                    [kt[i * grp:(i + 1) * grp, :] if i == j else jnp.zeros((grp, dk), f32) for i in range(ng)], axis=0))
            a = a + _dot_nt(jnp.concatenate(lhs, axis=1), jnp.concatenate(rhs, axis=1))
        q_inter = qg * jnp.exp(r_start)
        o = _dot(a, vh) + _dot_nt(q_inter, st[h])
        k_dec = kt * jnp.exp(b_last - r_end)
        st_out.append(st[h] * jnp.exp(b_last) + _dot_tn(vh, k_dec))
        on = o * lax.rsqrt(jnp.mean(o * o, axis=-1, keepdims=True) + RMS_EPS)
        ys.append(on * nw[:, sl] * _silu(gate[:, sl]))
    return ys, st_out


def _hgrn_kernel(*refs, c_rows, grp, t_valid, t_pad, layer, has_s0, ub, uc):
    if has_s0:
        p_ref, lbl_ref, nw_ref, s0_ref, y_ref, so_ref, st_ref = refs
    else:
        p_ref, lbl_ref, nw_ref, y_ref, so_ref, st_ref = refs
    c = pl.program_id(1)
    nc = pl.num_programs(1)
    nh = HG_HEADS
    dim = p_ref.shape[1] // 4
    dk = dim // nh

    @pl.when(c == 0)
    def _():
        if has_s0:
            for s in range(ub):
                for h in range(nh):
                    st_ref[s, h] = s0_ref[s, h].T
        else:
            st_ref[...] = jnp.zeros_like(st_ref)

    lbl = lbl_ref[...]
    e = jnp.exp(lbl - jnp.max(lbl, axis=0, keepdims=True))
    sm = e / jnp.sum(e, axis=0, keepdims=True)
    lb = jnp.zeros((1, dim), f32)
    for r in range(1, layer + 1):
        lb = lb + sm[r:r + 1, :]

    tril = _tril_mask(c_rows)
    rr = lax.broadcasted_iota(jnp.int32, (c_rows, c_rows), 0)
    ss = lax.broadcasted_iota(jnp.int32, (c_rows, c_rows), 1)
    gshift = grp.bit_length() - 1
    diag_mask = tril & ((rr >> gshift) == (ss >> gshift))
    nw = nw_ref[...]
    rows = lax.broadcasted_iota(jnp.int32, (c_rows, 1), 0)

    for s in range(ub):
        st = [st_ref[s, h] for h in range(nh)]
        for cc in range(uc):
            r0 = (s * uc + cc) * c_rows
            valid = ((c * uc + cc) * c_rows + rows) < t_valid if t_valid < t_pad else None
            ys, st = _hgrn_chunk(p_ref[r0:r0 + c_rows, :], st, lb, nw, tril, diag_mask, valid, grp)
            for h in range(nh):
                y_ref[r0:r0 + c_rows, h * dk:(h + 1) * dk] = ys[h].astype(y_ref.dtype)
        for h in range(nh):
            st_ref[s, h] = st[h]

    @pl.when(c == nc - 1)
    def _():
        for s in range(ub):
            for h in range(nh):
                so_ref[s, h] = st_ref[s, h].T


def _units_per_step(bsz, nc, max_seqs, max_chunks):
    if nc == 1:
        return max(u for u in (1, 2, 4, 8) if u <= max_seqs and bsz % u == 0), 1
    return 1, max(u for u in (1, 2, 4, 8) if u <= max_chunks and nc % u == 0)


def hgrn_call(p, lb_logits, norm_w, s0, bsz, t_pad, t_valid, layer):
    dim = p.shape[1] // 4
    nh = HG_HEADS
    dk = dim // nh
    c_rows = min(128, t_pad)
    grp = min(HG_CHUNK_GROUP, c_rows)
    nc = t_pad // c_rows
    ub, uc = _units_per_step(bsz, nc, 8, 4)
    rows = ub * uc * c_rows
    ncs = nc // uc
    has_s0 = s0 is not None
    in_specs = [pl.BlockSpec((rows, 4 * dim), lambda b, c: (b * ncs + c, 0)),
                pl.BlockSpec(lb_logits.shape, lambda b, c: (0, 0)),
                pl.BlockSpec((1, dim), lambda b, c: (0, 0))]
    args = [p, lb_logits, norm_w.reshape(1, dim)]
    if has_s0:
        in_specs.append(pl.BlockSpec((ub, nh, dk, dk), lambda b, c: (b, 0, 0, 0)))
        args.append(s0)
    kern = functools.partial(_hgrn_kernel, c_rows=c_rows, grp=grp, t_valid=t_valid, t_pad=t_pad,
                             layer=layer, has_s0=has_s0, ub=ub, uc=uc)
    return pl.pallas_call(
        kern,
        grid=(bsz // ub, ncs),
        in_specs=in_specs,
        out_specs=[pl.BlockSpec((rows, dim), lambda b, c: (b * ncs + c, 0)),
                   pl.BlockSpec((ub, nh, dk, dk), lambda b, c: (b, 0, 0, 0))],
        out_shape=[jax.ShapeDtypeStruct((bsz * t_pad, dim), bf16),
                   jax.ShapeDtypeStruct((bsz, nh, dk, dk), f32)],
        scratch_shapes=[pltpu.VMEM((ub, nh, dk, dk), f32)],
        compiler_params=_cparams(2),
        name="hgrn",
    )(*args)


def _expand_heads(w, ex_ref):
    hi = w.astype(bf16)
    r1 = w - hi.astype(f32)
    mid = r1.astype(bf16)
    lo = (r1 - mid.astype(f32)).astype(bf16)
    return jnp.dot(jnp.concatenate([hi, mid, lo], axis=1), ex_ref[...], preferred_element_type=f32)


def _ssd_kernel(*refs, c_rows, t_valid, t_pad, has_s0, ub, uc, conv_done):
    if has_s0:
        (zx_ref, cw_ref, cb_ref, dtb_ref, alog_ref, dsk_ref, nw_ref, ex_ref, s0_ref, buf_ref,
         y_ref, so_ref, nbuf_ref, st_ref, carry_ref) = refs
    else:
        (zx_ref, cw_ref, cb_ref, dtb_ref, alog_ref, dsk_ref, nw_ref, ex_ref,
         y_ref, so_ref, st_ref, carry_ref) = refs
    c = pl.program_id(1)
    nc = pl.num_programs(1)
    hp, ns, ngr = SSD_HEAD_DIM, SSD_STATE, SSD_GROUPS
    di = y_ref.shape[1]
    nh = di // hp
    hpg = nh // ngr
    gw = ngr * ns
    kconv = cw_ref.shape[0]
    pair = 2 * hp
    assert pair == LANES and hpg % 2 == 0

    @pl.when(c == 0)
    def _():
        carry_ref[...] = jnp.zeros_like(carry_ref)
        if has_s0:
            for s in range(ub):
                for r in range(di // LANES):
                    st_ref[s, :, r * LANES:(r + 1) * LANES] = s0_ref[s, r * LANES:(r + 1) * LANES, :].T
                carry_ref[s, SUBLANES - (kconv - 1):, :] = buf_ref[s]
        else:
            st_ref[...] = jnp.zeros_like(st_ref)

    cdim = cw_ref.shape[1]
    rows = lax.broadcasted_iota(jnp.int32, (c_rows, 1), 0)
    rows8 = rows[:SUBLANES]
    cw = cw_ref[...]
    a_neg = -jnp.exp(alog_ref[...])
    tril = _tril_mask(c_rows)
    dsk = dsk_ref[...]
    nw = nw_ref[...]
    lo = lax.broadcasted_iota(jnp.int32, (c_rows, LANES), 1) < hp
    for s in range(ub):
        for cc in range(uc):
            r0 = (s * uc + cc) * c_rows
            valid = ((c * uc + cc) * c_rows + rows) < t_valid if t_valid < t_pad else None
            _ssd_chunk(zx_ref[r0:r0 + c_rows, :], st_ref.at[s], carry_ref.at[s], y_ref.at[r0:r0 + c_rows, :],
                       cw, cb_ref[...], dtb_ref[...], a_neg, dsk, nw, ex_ref, tril, lo, rows8, valid, conv_done)

    @pl.when(c == nc - 1)
    def _():
        for s in range(ub):
            for r in range(di // LANES):
                so_ref[s, r * LANES:(r + 1) * LANES, :] = st_ref[s, :, r * LANES:(r + 1) * LANES].T
            if has_s0:
                last = s * c_rows + t_valid
                nbuf_ref[s] = zx_ref[last - (kconv - 1):last, di:di + cdim]


def _ssd_chunk(zx, st_ref, carry_ref, y_ref, cw, cb, dtb, a_neg, dsk, nw, ex_ref, tril, lo, rows8, valid, conv_done):
    hp, ns, ngr = SSD_HEAD_DIM, SSD_STATE, SSD_GROUPS
    c_rows = zx.shape[0]
    di = y_ref.shape[1]
    nh = di // hp
    hpg = nh // ngr
    gw = ngr * ns
    kconv, cdim = cw.shape
    pair = 2 * hp
    z = zx[:, :di]
    u = zx[:, di:di + cdim]
    if conv_done:
        xbc = u
    else:
        xbc = _silu(_causal_conv(u, carry_ref[...], cw, cb, rows8))
        carry_ref[...] = u[c_rows - SUBLANES:, :]
    dt = jax.nn.softplus(zx[:, di + cdim:] + dtb)
    if valid is not None:
        xbc = jnp.where(valid, xbc, 0.0)
        dt = jnp.where(valid, dt, 0.0)
    xs = xbc[:, :di]
    bm = xbc[:, di:di + gw]
    cm = xbc[:, di + gw:di + 2 * gw]
    acum = _cumsum_rows(dt * a_neg, tril.astype(bf16))

    def head_rows(a):
        if c_rows < LANES:
            a = jnp.concatenate([a, jnp.zeros((LANES - c_rows, LANES), f32)], axis=0)
        return a.T[:, :c_rows]

    acum_t = head_rows(acum)
    dt_t = head_rows(dt)
    a_last = acum[c_rows - 1:c_rows, :]
    xd = xs * _expand_heads(dt * jnp.exp(a_last - acum), ex_ref)

    ys = []
    for g in range(ngr):
        bg = bm[:, g * ns:(g + 1) * ns]
        cg = cm[:, g * ns:(g + 1) * ns]
        cb_mat = _dot_nt(cg, bg)
        gsl = slice(g * hpg * hp, (g + 1) * hpg * hp)
        y_inter = _dot(cg, st_ref[:, gsl])
        decs = []
        for q in range(hpg // 2):
            h1 = g * hpg + 2 * q
            psl = slice(h1 * hp, h1 * hp + pair)
            ms, es = [], []
            for h in (h1, h1 + 1):
                colb = jnp.broadcast_to(acum[:, h:h + 1], (c_rows, LANES))
                seg = colb[:, :c_rows] - acum_t[h:h + 1, :]
                ms.append(cb_mat * jnp.exp(jnp.where(tril, seg, -jnp.inf)) * dt_t[h:h + 1, :])
                es.append(jnp.exp(colb))
            xp = xs[:, psl]
            x_lo = jnp.where(lo, xp, 0.0)
            x_hi = jnp.where(lo, 0.0, xp)
            if (2 * c_rows) % LANES == 0:
                y_intra = _dot(jnp.concatenate(ms, axis=1), jnp.concatenate([x_lo, x_hi], axis=0))
            else:
                y_intra = _dot(ms[0], x_lo) + _dot(ms[1], x_hi)
            y_pair = y_intra + y_inter[:, q * pair:(q + 1) * pair] * jnp.where(lo, es[0], es[1])
            ys.append(y_pair + dsk[:, psl] * xp)
            decs.append(jnp.where(lo[0:1, :], jnp.exp(a_last[:, h1:h1 + 1]), jnp.exp(a_last[:, h1 + 1:h1 + 2])))
        st_ref[:, gsl] = st_ref[:, gsl] * jnp.concatenate(decs, axis=1) + _dot_tn(bg, xd[:, gsl])

    y = jnp.concatenate(ys, axis=1) * _silu(z)
    gdim = di // ngr
    for g in range(ngr):
        sl = slice(g * gdim, (g + 1) * gdim)
        yg = y[:, sl]
        y_ref[:, sl] = (yg * lax.rsqrt(jnp.mean(yg * yg, axis=-1, keepdims=True) + RMS_EPS) * nw[:, sl]).astype(y_ref.dtype)


def ssd_call(zx, conv_w, conv_b, dt_bias, a_log, d_skip, norm_w, s0, conv_buf, bsz, t_pad, t_valid, conv_done):
    hp, ns = SSD_HEAD_DIM, SSD_STATE
    conv_dim = conv_w.shape[1]
    di = zx.shape[1] - conv_dim - LANES
    nh = di // hp
    c_rows = min(128, t_pad)
    nc = t_pad // c_rows
    ub, uc = _units_per_step(bsz, nc, 4, 2)
    rows = ub * uc * c_rows
    ncs = nc // uc
    has_s0 = s0 is not None
    assert not (conv_done and has_s0)
    pad_h = lambda a: jnp.pad(a.reshape(1, nh), ((0, 0), (0, LANES - nh)))
    full = lambda shape: pl.BlockSpec(shape, lambda b, c: (0,) * len(shape))
    sel = (jnp.arange(di)[None, :] // hp == jnp.arange(LANES)[:, None]).astype(bf16)
    expand = jnp.concatenate([sel, sel, sel], axis=0)
    in_specs = [pl.BlockSpec((rows, zx.shape[1]), lambda b, c: (b * ncs + c, 0)),
                full(conv_w.shape), full((1, conv_dim)), full((1, LANES)), full((1, LANES)),
                full((1, di)), full((1, di)), full(expand.shape)]
    args = [zx, conv_w, conv_b.reshape(1, conv_dim), pad_h(dt_bias), pad_h(a_log),
            jnp.repeat(d_skip, hp).reshape(1, di), norm_w.reshape(1, di), expand]
    out_specs = [pl.BlockSpec((rows, di), lambda b, c: (b * ncs + c, 0)),
                 pl.BlockSpec((ub, di, ns), lambda b, c: (b, 0, 0))]
    out_shape = [jax.ShapeDtypeStruct((bsz * t_pad, di), bf16), jax.ShapeDtypeStruct((bsz, di, ns), f32)]
    if has_s0:
        assert nc == 1 and t_valid >= conv_w.shape[0] - 1
        in_specs += [pl.BlockSpec((ub, di, ns), lambda b, c: (b, 0, 0)),
                     pl.BlockSpec((ub,) + conv_buf.shape[1:], lambda b, c: (b, 0, 0))]
        args += [s0.reshape(bsz, di, ns), conv_buf]
        out_specs.append(pl.BlockSpec((ub,) + conv_buf.shape[1:], lambda b, c: (b, 0, 0)))
        out_shape.append(jax.ShapeDtypeStruct(conv_buf.shape, f32))
    kern = functools.partial(_ssd_kernel, c_rows=c_rows, t_valid=t_valid, t_pad=t_pad, has_s0=has_s0, ub=ub, uc=uc,
                             conv_done=conv_done)
    outs = pl.pallas_call(
        kern,
        grid=(bsz // ub, ncs),
        in_specs=in_specs,
        out_specs=out_specs,
        out_shape=out_shape,
        scratch_shapes=[pltpu.VMEM((ub, ns, di), f32), pltpu.VMEM((ub, SUBLANES, conv_dim), f32)],
        compiler_params=_cparams(2),
        name="ssd",
    )(*args)
    return outs[0], outs[1].reshape(bsz, nh, hp, ns), (outs[2] if has_s0 else None)


def _trunk(x, c_k, c_v, s_hg, s_ssm, s_sconv, s_fconv, prm):
    bsz, t, d = x.shape
    depth = prm["norm_mix_w"].shape[0]
    decode = c_k is not None
    tp = -(-t // SUBLANES) * SUBLANES
    if tp != t:
        x = jnp.pad(x, ((0, 0), (0, tp - t), (0, 0)))
    m = bsz * tp
    n_heads = d // ATT_HEAD_DIM
    qd = n_heads * ATT_HEAD_DIM
    kvd = ATT_KV_HEADS * ATT_HEAD_DIM
    x2 = x.reshape(m, d)
    xn, xn_w = x2, prm["norm_mix_w"][0]
    nk, nv, nhg, nssm, nsconv, nfconv = [], [], [], [], [], []
    if decode:
        ck_all = jnp.transpose(c_k, (0, 1, 3, 4, 2)).reshape(c_k.shape[0], bsz, kvd, WINDOW)
        cv_all = jnp.transpose(c_v, (0, 1, 3, 4, 2)).reshape(c_v.shape[0], bsz, kvd, WINDOW)
    slot = [0] * N_MIXERS
    for i in range(depth):
        kind = i % N_MIXERS
        j = slot[kind]
        slot[kind] += 1
        nw_ffn = prm["norm_ffn_w"][i]
        if kind == 0:
            w_qkv, b_qkv = prm["attn_w_qkv"], prm["attn_b_qkv"][j]
            nq = w_qkv.shape[2]
            if decode:
                qkv = proj_call(xn, xn_w, w_qkv, j, b_qkv, f32, "attn_qkv").reshape(bsz, tp, nq)
                o, k_new, v_new = attn_cache_call(qkv, ck_all, cv_all, j, prm["attn_sinks"][j], n_heads, t)
                o = o.reshape(m, qd)
            else:
                qkv = proj_call(xn, xn_w, w_qkv, j, b_qkv, bf16, "attn_qkv")
                o = attn_prompt_call(qkv, prm["attn_sinks"][j], bsz, t, n_heads)
                xn_last = xn.reshape(bsz, t, d)[:, t - WINDOW:].reshape(bsz * WINDOW, d)
                kv_last = proj_call(xn_last, xn_w, w_qkv[:, :, qd:], j, b_qkv[qd:], f32,
                                    "attn_kv_tail").reshape(bsz, WINDOW, 2, kvd)
                to4 = lambda a: a.reshape(bsz, WINDOW, ATT_KV_HEADS, ATT_HEAD_DIM)
                k_new, v_new = to4(kv_last[:, :, 0]), to4(kv_last[:, :, 1])
            nk.append(k_new)
            nv.append(v_new)
            act, w_o, b_o = o, (prm["attn_w_o"], j), prm["attn_b_o"][j]
        elif kind == 1:
            p = proj_call(xn, xn_w, prm["hgrn_w_in"], j, None, f32, "hgrn_in")
            y, s_new = hgrn_call(p, prm["hgrn_lb_logits"], prm["hgrn_norm_w"][j],
                                 None if s_hg is None else s_hg[j], bsz, tp, t, i)
            nhg.append(s_new)
            act, w_o, b_o = y, (prm["hgrn_w_o"], j), None
        else:
            conv_dim = prm["ssd_conv_w"].shape[2]
            di = prm["ssd_w_o"].shape[1]
            kc = prm["ssd_conv_w"].shape[1]
            if s_sconv is None:
                zx, pre = ssd_in_call(xn, prm["ssd_w_in"], j, prm["ssd_conv_w"][j], prm["ssd_conv_b"][j], tp)
                pre = pre.reshape(bsz, tp // min(PROJ_TM, m), SUBLANES, conv_dim)[:, -1, SUBLANES - (kc - 1):]
            else:
                zx = proj_call(xn, xn_w, prm["ssd_w_in"], j, None, f32, "ssd_in")
            y, s_new, conv_new = ssd_call(zx, prm["ssd_conv_w"][j], prm["ssd_conv_b"][j], prm["ssd_dt_bias"][j],
                                prm["ssd_a_log"][j], prm["ssd_d"][j], prm["ssd_norm_w"][j],
                                None if s_ssm is None else s_ssm[j],
                                None if s_sconv is None else s_sconv[j], bsz, tp, t, s_sconv is None)
            nssm.append(s_new)
            nsconv.append(pre if s_sconv is None else conv_new)
            act, w_o, b_o = y, (prm["ssd_w_o"], j), None
        last = i == depth - 1
        next_w = prm["norm_final_w"] if last else prm["norm_mix_w"][i + 1]
        dff = prm["ffn_w_down"].shape[1]
        kf = prm["ffn_conv_w"].shape[1]
        x2, xn, tail = ffn_call(act, w_o, b_o, x2, nw_ffn, (prm["ffn_w_up"], i), prm["ffn_conv_w"][i],
                                prm["ffn_conv_b"][i], (prm["ffn_w_down"], i), next_w, tp,
                                None if s_fconv is None else (s_fconv, i), t, f32 if last else bf16, "ffn")
        xn_w = None
        if s_fconv is not None:
            nfconv.append(tail)
        else:
            tiles_per_seq = tp // tail_tile_rows(m)
            tl = tail.reshape(bsz, tiles_per_seq, SUBLANES, dff)[:, -1]
            nfconv.append(tl[:, SUBLANES - (kf - 1):])
    stack = lambda xs: xs[0][None] if len(xs) == 1 else jnp.stack(xs)
    y = xn.reshape(bsz, tp, d)[:, :t]
    nk, nv = stack(nk), stack(nv)
    if decode:
        unview = lambda a: jnp.transpose(a.reshape(a.shape[0], bsz, ATT_KV_HEADS, ATT_HEAD_DIM, WINDOW), (0, 1, 4, 2, 3))
        nk, nv = unview(nk), unview(nv)
    return (y, nk, nv, stack(nhg), stack(nssm), stack(nsconv), stack(nfconv))


def tail_tile_rows(m):
    return min(FFN_TM, m)


def kernel(x_prompt, x_sample, cache_attn_k, cache_attn_v, state_hgrn, state_ssm, state_ssm_conv, state_ffn_conv, norm_mix_w, norm_ffn_w, norm_final_w, attn_w_qkv, attn_b_qkv, attn_sinks, attn_w_o, attn_b_o, hgrn_w_in, hgrn_lb_logits, hgrn_norm_w, hgrn_w_o, ssd_w_in, ssd_conv_w, ssd_conv_b, ssd_dt_bias, ssd_a_log, ssd_d, ssd_norm_w, ssd_w_o, ffn_w_up, ffn_conv_w, ffn_conv_b, ffn_w_down):
    cast = lambda w: w.astype(bf16)
    ssd_in_pad = (-ssd_w_in.shape[2]) % LANES
    prm = dict(norm_mix_w=norm_mix_w, norm_ffn_w=norm_ffn_w, norm_final_w=norm_final_w,
               attn_w_qkv=cast(attn_w_qkv), attn_b_qkv=attn_b_qkv, attn_sinks=attn_sinks,
               attn_w_o=cast(attn_w_o), attn_b_o=attn_b_o,
               hgrn_w_in=cast(hgrn_w_in), hgrn_lb_logits=hgrn_lb_logits, hgrn_norm_w=hgrn_norm_w,
               hgrn_w_o=cast(hgrn_w_o),
               ssd_w_in=jnp.pad(cast(ssd_w_in), ((0, 0), (0, 0), (0, ssd_in_pad))), ssd_conv_w=ssd_conv_w,
               ssd_conv_b=ssd_conv_b, ssd_dt_bias=ssd_dt_bias,
               ssd_a_log=ssd_a_log, ssd_d=ssd_d, ssd_norm_w=ssd_norm_w, ssd_w_o=cast(ssd_w_o),
               ffn_w_up=cast(ffn_w_up), ffn_conv_w=ffn_conv_w, ffn_conv_b=ffn_conv_b, ffn_w_down=cast(ffn_w_down))
    outs_p = _trunk(x_prompt, None, None, None, None, None, None, prm)
    outs_s = _trunk(x_sample, cache_attn_k, cache_attn_v, state_hgrn, state_ssm, state_ssm_conv,
                    state_ffn_conv, prm)
    return (outs_p[0], outs_s[0]) + outs_p[1:] + outs_s[1:]
```

```python
import functools

import jax
import jax.numpy as jnp
from jax import lax
from jax.experimental import pallas as pl
from jax.experimental.pallas import tpu as pltpu

bf16 = jnp.bfloat16
f32 = jnp.float32

RMS_EPS = 1e-6
N_MIXERS = 3
WINDOW = 128
PAST_LEN = 8192
ATT_HEAD_DIM = 64
ATT_KV_HEADS = 4
HG_HEADS = 8
HG_CHUNK_GROUP = 32
SSD_HEAD_DIM = 64
SSD_STATE = 128
SSD_GROUPS = 4

LANES = 128
SUBLANES = 8
VMEM_LIMIT = 56 * 1024 * 1024
PROJ_TM = 512
FFN_TM = 512


def _cparams(n_axes):
    return pltpu.CompilerParams(dimension_semantics=("arbitrary",) * n_axes,
                                vmem_limit_bytes=VMEM_LIMIT)


def _dot(a, b):
    return jnp.dot(a.astype(bf16), b.astype(bf16), preferred_element_type=f32)


def _dot_nt(a, b):
    return lax.dot_general(a.astype(bf16), b.astype(bf16), (((1,), (1,)), ((), ())),
                           preferred_element_type=f32)


def _dot_tn(a, b):
    return lax.dot_general(a.astype(bf16), b.astype(bf16), (((0,), (0,)), ((), ())),
                           preferred_element_type=f32)


def _silu(x):
    return x * jax.nn.sigmoid(x)


def _rms(x, w):
    ms = jnp.mean(x * x, axis=-1, keepdims=True)
    return x * lax.rsqrt(ms + RMS_EPS) * w


def _cumsum_rows(x, tril):
    hi = x.astype(bf16)
    r1 = x - hi.astype(f32)
    mid = r1.astype(bf16)
    lo = (r1 - mid.astype(f32)).astype(bf16)
    d = functools.partial(jnp.dot, preferred_element_type=f32)
    return d(tril, hi) + d(tril, mid) + d(tril, lo)


def _tril_mask(c):
    r = lax.broadcasted_iota(jnp.int32, (c, c), 0)
    s = lax.broadcasted_iota(jnp.int32, (c, c), 1)
    return r >= s


def _resident(shape):
    return pl.BlockSpec(shape, lambda *_: (0,) * len(shape), pipeline_mode=pl.Buffered(1))


def _resident_layer(w, layer):
    return pl.BlockSpec((None,) + w.shape[1:], lambda *_: (layer, 0, 0), pipeline_mode=pl.Buffered(1))


def _proj_kernel(x_ref, nw_ref, w_ref, b_ref, o_ref):
    x = x_ref[...]
    if x.dtype != bf16:
        x = _rms(x, nw_ref[...]).astype(bf16)
    acc = jnp.dot(x, w_ref[...], preferred_element_type=f32)
    o_ref[...] = (acc + b_ref[...]).astype(o_ref.dtype)


def proj_call(x, norm_w, w, layer, bias, out_dtype, name):
    m, k = x.shape
    n = w.shape[2]
    tm = min(PROJ_TM, m)
    if bias is None:
        bias = jnp.zeros((n,), f32)
    if norm_w is None:
        assert x.dtype == bf16
        norm_w = jnp.ones((k,), f32)
    return pl.pallas_call(
        _proj_kernel,
        grid=(m // tm,),
        in_specs=[pl.BlockSpec((tm, k), lambda i: (i, 0)), _resident((1, k)), _resident_layer(w, layer),
                  _resident((1, n))],
        out_specs=pl.BlockSpec((tm, n), lambda i: (i, 0)),
        out_shape=jax.ShapeDtypeStruct((m, n), out_dtype),
        compiler_params=_cparams(1),
        name=name,
    )(x, norm_w.reshape(1, k), w, bias.reshape(1, n))


def _causal_conv(u, prev, cw, cb, rows8):
    kconv = cw.shape[0]
    yc = cb
    for kk in range(kconv - 1):
        back = kconv - 1 - kk
        sh = pltpu.roll(u, back, 0)
        top = sh[:SUBLANES]
        for r0 in range(back):
            top = jnp.where(rows8 == r0, prev[SUBLANES - back + r0:SUBLANES - back + r0 + 1, :], top)
        sh = jnp.concatenate([top, sh[SUBLANES:]], axis=0) if u.shape[0] > SUBLANES else top
        yc = yc + sh * cw[kk:kk + 1, :]
    return yc + u * cw[kconv - 1:kconv, :]


def _ssd_in_kernel(x_ref, w_ref, cw_ref, cb_ref, o_ref, tail_ref, carry_ref, *, tm, tc, t_seq, di):
    i = pl.program_id(0)
    cdim = cw_ref.shape[1]

    @pl.when(((i * tm) & (t_seq - 1)) == 0)
    def _():
        carry_ref[...] = jnp.zeros_like(carry_ref)

    x = x_ref[...]
    rows8 = lax.broadcasted_iota(jnp.int32, (SUBLANES, 1), 0)
    nz = di // tc
    for idx, c0 in enumerate(range(0, cdim, tc)):
        cs = slice(c0, c0 + tc)
        u = jnp.dot(x, w_ref[:, di + c0:di + c0 + tc], preferred_element_type=f32)
        tail_ref[0, :, cs] = u[tm - SUBLANES:, :]
        yc = _causal_conv(u, carry_ref[:, cs], cw_ref[:, cs], cb_ref[:, cs], rows8)
        carry_ref[:, cs] = u[tm - SUBLANES:, :]
        o_ref[:, di + c0:di + c0 + tc] = _silu(yc)
        if idx < nz:
            o_ref[:, idx * tc:(idx + 1) * tc] = jnp.dot(x, w_ref[:, idx * tc:(idx + 1) * tc],
                                                        preferred_element_type=f32)
    o_ref[:, di + cdim:] = jnp.dot(x, w_ref[:, di + cdim:], preferred_element_type=f32)


def ssd_in_call(xn, w, layer, conv_w, conv_b, t_seq):
    m, k = xn.shape
    n = w.shape[2]
    cdim = conv_w.shape[1]
    di = n - cdim - LANES
    tm = min(PROJ_TM, m)
    tc = 2 * LANES
    assert t_seq % tm == 0 and cdim % tc == 0 and di % tc == 0 and di <= cdim and (t_seq & (t_seq - 1)) == 0
    kern = functools.partial(_ssd_in_kernel, tm=tm, tc=tc, t_seq=t_seq, di=di)
    return pl.pallas_call(
        kern,
        grid=(m // tm,),
        in_specs=[pl.BlockSpec((tm, k), lambda i: (i, 0)), _resident_layer(w, layer),
                  _resident(conv_w.shape), _resident((1, cdim))],
        out_specs=[pl.BlockSpec((tm, n), lambda i: (i, 0)), pl.BlockSpec((1, SUBLANES, cdim), lambda i: (i, 0, 0))],
        out_shape=[jax.ShapeDtypeStruct((m, n), f32), jax.ShapeDtypeStruct((m // tm, SUBLANES, cdim), f32)],
        scratch_shapes=[pltpu.VMEM((SUBLANES, cdim), f32)],
        compiler_params=_cparams(1),
        name="ssd_in_conv",
    )(xn, w, conv_w, conv_b.reshape(1, cdim))


def _ffn_kernel(*refs, tm, tf, t_seq, t_valid, has_state):
    if has_state:
        (act_ref, wo_ref, bo_ref, x_ref, nw1_ref, wu_ref, cw_ref, cb_ref, wd_ref, nw2_ref, buf_ref,
         xo_ref, xn2_ref, nst_ref, carry_ref) = refs
    else:
        (act_ref, wo_ref, bo_ref, x_ref, nw1_ref, wu_ref, cw_ref, cb_ref, wd_ref, nw2_ref,
         xo_ref, xn2_ref, tail_ref, carry_ref) = refs
    i = pl.program_id(0)
    dff = wd_ref.shape[0]

    x1 = jnp.dot(act_ref[...].astype(bf16), wo_ref[...], preferred_element_type=f32) + bo_ref[...] + x_ref[...]
    xn = _rms(x1, nw1_ref[...]).astype(bf16)
    rows8 = lax.broadcasted_iota(jnp.int32, (SUBLANES, 1), 0)
    if not has_state:
        @pl.when(((i * tm) & (t_seq - 1)) == 0)
        def _():
            carry_ref[...] = jnp.zeros_like(carry_ref)
    gs = []
    for c0 in range(0, dff, tf):
        cs = slice(c0, c0 + tf)
        a = jnp.dot(xn, wu_ref[:, cs], preferred_element_type=f32)
        b = jnp.dot(xn, wu_ref[:, dff + c0:dff + c0 + tf], preferred_element_type=f32)
        r1 = pltpu.roll(a, 1, 0)
        r2 = pltpu.roll(a, 2, 0)
        if has_state:
            p1s, p2s = [], []
            for s in range(tm // SUBLANES):
                rs = slice(s * SUBLANES, (s + 1) * SUBLANES)
                b0, b1 = buf_ref[s, 0:1, cs], buf_ref[s, 1:2, cs]
                p1s.append(jnp.where(rows8 == 0, b1, r1[rs, :]))
                p2s.append(jnp.where(rows8 == 0, b0, jnp.where(rows8 == 1, b1, r2[rs, :])))
                nst_ref[s, :, cs] = a[s * SUBLANES + t_valid - 2:s * SUBLANES + t_valid, :]
            p1 = jnp.concatenate(p1s, axis=0)
            p2 = jnp.concatenate(p2s, axis=0)
        else:
            tail_ref[0, :, cs] = a[tm - SUBLANES:, :]
            carry = carry_ref[:, cs]
            top1 = jnp.where(rows8 == 0, carry[7:8, :], r1[:SUBLANES])
            top2 = jnp.where(rows8 == 0, carry[6:7, :], jnp.where(rows8 == 1, carry[7:8, :], r2[:SUBLANES]))
            p1 = jnp.concatenate([top1, r1[SUBLANES:]], axis=0)
            p2 = jnp.concatenate([top2, r2[SUBLANES:]], axis=0)
            carry_ref[:, cs] = a[tm - SUBLANES:, :]
        y = cb_ref[:, cs] + p2 * cw_ref[0:1, cs]
        y = y + p1 * cw_ref[1:2, cs]
        y = y + a * cw_ref[2:3, cs]
        gs.append((_silu(y) * b).astype(bf16))
    g = jnp.concatenate(gs, axis=1)
    xo = jnp.dot(g, wd_ref[...], preferred_element_type=f32) + x1
    xo_ref[...] = xo
    xn2_ref[...] = _rms(xo, nw2_ref[...]).astype(xn2_ref.dtype)


def ffn_call(act, w_o, b_o, x, ffn_norm_w, w_up, conv_w, conv_b, w_down, next_w, t_seq, buf, t_valid, out_dtype, name):
    m, d = x.shape
    ko = act.shape[1]
    dff = w_down[0].shape[1]
    tf = 2 * LANES
    assert dff % tf == 0 and (t_seq & (t_seq - 1)) == 0
    has_state = buf is not None
    tm = min(FFN_TM // 2 if has_state else FFN_TM, m)
    assert (t_seq == SUBLANES and t_valid >= 2 and conv_w.shape[0] == 3) if has_state else (t_seq % tm == 0)
    if b_o is None:
        b_o = jnp.zeros((d,), f32)
    row = lambda w: pl.BlockSpec((tm, w), lambda i: (i, 0))
    in_specs = [row(ko), _resident_layer(*w_o), _resident((1, d)), row(d), _resident((1, d)),
                _resident_layer(*w_up), _resident((3, dff)), _resident((1, dff)),
                _resident_layer(*w_down), _resident((1, d))]
    args = [act, w_o[0], b_o.reshape(1, d), x, ffn_norm_w.reshape(1, d),
            w_up[0], conv_w, conv_b.reshape(1, dff), w_down[0], next_w.reshape(1, d)]
    if has_state:
        nseq = tm // t_seq
        in_specs.append(pl.BlockSpec((None, nseq, 2, dff), lambda i: (buf[1], i, 0, 0)))
        args.append(buf[0])
        third_spec = pl.BlockSpec((nseq, 2, dff), lambda i: (i, 0, 0))
        third_shape = jax.ShapeDtypeStruct((m // t_seq, 2, dff), f32)
    else:
        third_spec = pl.BlockSpec((1, SUBLANES, dff), lambda i: (i, 0, 0))
        third_shape = jax.ShapeDtypeStruct((m // tm, SUBLANES, dff), f32)
    kern = functools.partial(_ffn_kernel, tm=tm, tf=tf, t_seq=t_seq, t_valid=t_valid, has_state=has_state)
    return pl.pallas_call(
        kern,
        grid=(m // tm,),
        in_specs=in_specs,
        out_specs=[row(d), row(d), third_spec],
        out_shape=[jax.ShapeDtypeStruct((m, d), f32), jax.ShapeDtypeStruct((m, d), out_dtype), third_shape],
        scratch_shapes=[pltpu.VMEM((SUBLANES, dff), f32)],
        compiler_params=_cparams(1),
        name=name,
    )(*args)


def _attn_softmax_pv(s, sink_col, vj):
    m = jnp.maximum(jnp.max(s, axis=-1, keepdims=True), sink_col)
    p = jnp.exp(s - m)
    l = jnp.sum(p, axis=-1, keepdims=True) + jnp.exp(sink_col - m)
    return _dot(p, vj) / l


def _sink_col(sinks_ref, j, group, rows_per_head):
    r = lax.broadcasted_iota(jnp.int32, (group * rows_per_head, 1), 0)
    col = jnp.full((group * rows_per_head, 1), sinks_ref[j * group], f32)
    for p in range(1, group):
        col = jnp.where(r >= p * rows_per_head, sinks_ref[j * group + p], col)
    return col


def _attn_prompt_kernel(sinks_ref, q_ref, kvc_ref, kvp_ref, o_ref, *, n_heads):
    n = pl.program_id(1)
    w, hd, kvh = WINDOW, ATT_HEAD_DIM, ATT_KV_HEADS
    group = n_heads // kvh
    kvw = kvh * hd
    assert group == kvh
    kv = jnp.concatenate([kvp_ref[...], kvc_ref[...]], axis=0)
    k2 = kv[:, :kvw]
    v2 = kv[:, kvw:]
    i = lax.broadcasted_iota(jnp.int32, (w, 2 * w), 0)
    c = lax.broadcasted_iota(jnp.int32, (w, 2 * w), 1)
    band = (c > i) & (c <= i + w) & ((n > 0) | (c >= w))
    mask = jnp.concatenate([band] * group, axis=0)
    slot = lax.broadcasted_iota(jnp.int32, (w, kvw), 1) >> (hd.bit_length() - 1)
    for j in range(kvh):
        qg = q_ref[:, j * kvw:(j + 1) * kvw] * (hd ** -0.5)
        qs = jnp.concatenate([jnp.where(slot == j, pltpu.roll(qg, ((j - p) % kvh) * hd, 1), 0)
                              for p in range(group)], axis=0)
        s = jnp.where(mask, _dot_nt(qs, k2), -jnp.inf)
        of = _attn_softmax_pv(s, _sink_col(sinks_ref, j, group, w), v2).astype(o_ref.dtype)
        og = of[0:w, :]
        for p in range(group):
            piece = of[p * w:(p + 1) * w, :]
            og = jnp.where(slot == p, pltpu.roll(piece, ((p - j) % kvh) * hd, 1), og)
        o_ref[:, j * kvw:(j + 1) * kvw] = og


def attn_prompt_call(qkv, sinks, bsz, t, n_heads):
    w, hd, kvh = WINDOW, ATT_HEAD_DIM, ATT_KV_HEADS
    nb = t // w
    qd = n_heads * hd
    kvd = 2 * kvh * hd
    assert qd % kvd == 0
    kvblk = qd // kvd
    kern = functools.partial(_attn_prompt_kernel, n_heads=n_heads)
    return pl.pallas_call(
        kern,
        grid=(bsz, nb),
        in_specs=[pl.BlockSpec(memory_space=pltpu.SMEM),
                  pl.BlockSpec((w, qd), lambda b, n: (b * nb + n, 0)),
                  pl.BlockSpec((w, kvd), lambda b, n: (b * nb + n, kvblk)),
                  pl.BlockSpec((w, kvd), lambda b, n: (b * nb + jnp.maximum(n - 1, 0), kvblk))],
        out_specs=pl.BlockSpec((w, qd), lambda b, n: (b * nb + n, 0)),
        out_shape=jax.ShapeDtypeStruct((bsz * t, qd), bf16),
        compiler_params=_cparams(2),
        name="attn_prompt",
    )(sinks, qkv, qkv, qkv)


def _shifted_window(cache, new_rows, t_valid):
    tp, w = new_rows.shape[0], cache.shape[1]
    hi = new_rows.astype(bf16)
    r1 = new_rows - hi.astype(f32)
    mid = r1.astype(bf16)
    lo = (r1 - mid.astype(f32)).astype(bf16)
    t = lax.broadcasted_iota(jnp.int32, (tp, w), 0)
    col = lax.broadcasted_iota(jnp.int32, (tp, w), 1)
    place = ((col == w - t_valid + t) & (t < t_valid)).astype(bf16)
    placed = _dot_tn(jnp.concatenate([hi, mid, lo], axis=0), jnp.concatenate([place] * 3, axis=0))
    lane = lax.broadcasted_iota(jnp.int32, cache.shape, 1)
    return jnp.where(lane >= w - t_valid, placed, pltpu.roll(cache, w - t_valid, 1))


def _attn_cache_kernel(sinks_ref, q_ref, ck_ref, cv_ref, o_ref, nk_ref, nv_ref, *, n_heads, t_valid, bt):
    w, hd, kvh = WINDOW, ATT_HEAD_DIM, ATT_KV_HEADS
    group = n_heads // kvh
    tp = q_ref.shape[1]
    qd = n_heads * hd
    tq = lax.broadcasted_iota(jnp.int32, (tp, w), 0)
    jc = lax.broadcasted_iota(jnp.int32, (tp, w), 1)
    mc = (jc > tq) & (jc - w + PAST_LEN >= 0)
    tq2 = lax.broadcasted_iota(jnp.int32, (tp, tp), 0)
    un = lax.broadcasted_iota(jnp.int32, (tp, tp), 1)
    mn = (un <= tq2) & (un < t_valid)
    mask_c = jnp.concatenate([mc] * n_heads, axis=0)
    mask_n = jnp.concatenate([mn] * n_heads, axis=0)
    kvw = kvh * hd
    assert group == kvh
    slot = lax.broadcasted_iota(jnp.int32, (tp, kvw), 1) >> (hd.bit_length() - 1)
    sink = _sink_col(sinks_ref, 0, n_heads, tp)

    for bi in range(bt):
        row = q_ref[bi]
        ck = ck_ref[bi]
        cv = cv_ref[bi]
        kn = row[:, qd:qd + kvw]
        vn = row[:, qd + kvw:]
        nk_ref[bi] = _shifted_window(ck, kn, t_valid)
        nv_ref[bi] = _shifted_window(cv, vn, t_valid)
        pieces = []
        for j in range(kvh):
            qg = row[:, j * kvw:(j + 1) * kvw] * (hd ** -0.5)
            for p in range(group):
                pieces.append(jnp.where(slot == j, pltpu.roll(qg, ((j - p) % kvh) * hd, 1), 0.0))
        qs = jnp.concatenate(pieces, axis=0)
        sc = jnp.where(mask_c, _dot(qs, ck), -jnp.inf)
        sn = jnp.where(mask_n, _dot_nt(qs, kn), -jnp.inf)
        m = jnp.maximum(jnp.maximum(jnp.max(sc, axis=-1, keepdims=True),
                                    jnp.max(sn, axis=-1, keepdims=True)), sink)
        pc = jnp.exp(sc - m)
        pn = jnp.exp(sn - m)
        l = jnp.sum(pc, axis=-1, keepdims=True) + jnp.sum(pn, axis=-1, keepdims=True) + jnp.exp(sink - m)
        of = (_dot_nt(pc, cv) + _dot(pn, vn)) / l
        for j in range(kvh):
            og = jnp.zeros((tp, kvw), f32)
            for p in range(group):
                h = j * group + p
                og = og + jnp.where(slot == p, pltpu.roll(of[h * tp:(h + 1) * tp, :], ((p - j) % kvh) * hd, 1), 0.0)
            o_ref[bi, :, j * kvw:(j + 1) * kvw] = og


def attn_cache_call(qkv, cache_k, cache_v, layer, sinks, n_heads, t_valid):
    bsz, tp, nq = qkv.shape
    w = WINDOW
    kd = cache_k.shape[2]
    qd = n_heads * ATT_HEAD_DIM
    bt = 8
    kern = functools.partial(_attn_cache_kernel, n_heads=n_heads, t_valid=t_valid, bt=bt)
    return pl.pallas_call(
        kern,
        grid=(bsz // bt,),
        in_specs=[pl.BlockSpec(memory_space=pltpu.SMEM),
                  pl.BlockSpec((bt, tp, nq), lambda b: (b, 0, 0)),
                  pl.BlockSpec((None, bt, kd, w), lambda b: (layer, b, 0, 0)),
                  pl.BlockSpec((None, bt, kd, w), lambda b: (layer, b, 0, 0))],
        out_specs=[pl.BlockSpec((bt, tp, qd), lambda b: (b, 0, 0)),
                   pl.BlockSpec((bt, kd, w), lambda b: (b, 0, 0)),
                   pl.BlockSpec((bt, kd, w), lambda b: (b, 0, 0))],
        out_shape=[jax.ShapeDtypeStruct((bsz, tp, qd), f32),
                   jax.ShapeDtypeStruct((bsz, kd, w), f32),
                   jax.ShapeDtypeStruct((bsz, kd, w), f32)],
        compiler_params=_cparams(1),
        name="attn_cache",
    )(sinks, qkv, cache_k, cache_v)


def _hgrn_chunk(x, st, lb, nw, tril, diag_mask, valid, grp):
    nh = HG_HEADS
    c_rows = x.shape[0]
    dim = x.shape[1] // 4
    dk = dim // nh
    ng = c_rows // grp
    if valid is not None:
        x = jnp.where(valid, x, 0.0)
    q = _silu(x[:, 0:dim])
    fg = lb + (1.0 - lb) * jax.nn.sigmoid(x[:, dim:2 * dim])
    logf = jnp.log(fg)
    k = 1.0 - fg
    v = x[:, 2 * dim:3 * dim]
    gate = x[:, 3 * dim:4 * dim]
    if valid is not None:
        logf = jnp.where(valid, logf, 0.0)
        k = jnp.where(valid, k, 0.0)
    bcum = _cumsum_rows(logf, tril.astype(bf16))

    def rows_of(vals):
        return jnp.concatenate([jnp.broadcast_to(r, (grp, dk)) for r in vals], axis=0)

    ys, st_out = [], []
    for h in range(nh):
        sl = slice(h * dk, (h + 1) * dk)
        b = bcum[:, sl]
        qh, kh, vh = q[:, sl], k[:, sl], v[:, sl]
        zero = jnp.zeros((1, dk), f32)
        r = [zero] + [b[i * grp - 1:i * grp, :] for i in range(1, ng)] + [b[c_rows - 1:c_rows, :]]
        mid = [b[i * grp + grp // 2 - 1:i * grp + grp // 2, :] for i in range(ng)]
        b_last = r[ng]
        r_start = rows_of(r[:ng])
        r_end = rows_of(r[1:])
        r_mid = rows_of(mid)
        qg = qh * jnp.exp(b - r_start)
        kt = kh * jnp.exp(r_end - b)
        qm = qh * jnp.exp(b - r_mid)
        km = kh * jnp.exp(r_mid - b)
        a = jnp.where(diag_mask, _dot_nt(qm, km), 0.0)
        if ng > 1:
            lhs, rhs = [], []
            for j in range(ng - 1):
                lhs.append(jnp.concatenate(
                    [jnp.zeros((grp, dk), f32) if i <= j else qg[i * grp:(i + 1) * grp, :] * jnp.exp(r[i] - r[j + 1])
                     for i in range(ng)], axis=0))
                rhs.append(jnp.concatenate(
                    [kt[i * grp:(i + 1) * grp, :] if i == j else jnp.zeros((grp, dk), f32) for i in range(ng)], axis=0))
            a = a + _dot_nt(jnp.concatenate(lhs, axis=1), jnp.concatenate(rhs, axis=1))
        q_inter = qg * jnp.exp(r_start)
        o = _dot(a, vh) + _dot_nt(q_inter, st[h])
        k_dec = kt * jnp.exp(b_last - r_end)
        st_out.append(st[h] * jnp.exp(b_last) + _dot_tn(vh, k_dec))
        on = o * lax.rsqrt(jnp.mean(o * o, axis=-1, keepdims=True) + RMS_EPS)
        ys.append(on * nw[:, sl] * _silu(gate[:, sl]))
    return ys, st_out


def _hgrn_kernel(*refs, c_rows, grp, t_valid, t_pad, layer, has_s0, ub, uc):
    if has_s0:
        p_ref, lbl_ref, nw_ref, s0_ref, y_ref, so_ref, st_ref = refs
    else:
        p_ref, lbl_ref, nw_ref, y_ref, so_ref, st_ref = refs
    c = pl.program_id(1)
    nc = pl.num_programs(1)
    nh = HG_HEADS
    dim = p_ref.shape[1] // 4
    dk = dim // nh

    @pl.when(c == 0)
    def _():
        if has_s0:
            for s in range(ub):
                for h in range(nh):
                    st_ref[s, h] = s0_ref[s, h].T
        else:
            st_ref[...] = jnp.zeros_like(st_ref)

    lbl = lbl_ref[...]
    e = jnp.exp(lbl - jnp.max(lbl, axis=0, keepdims=True))
    sm = e / jnp.sum(e, axis=0, keepdims=True)
    lb = jnp.zeros((1, dim), f32)
    for r in range(1, layer + 1):
        lb = lb + sm[r:r + 1, :]

    tril = _tril_mask(c_rows)
    rr = lax.broadcasted_iota(jnp.int32, (c_rows, c_rows), 0)
    ss = lax.broadcasted_iota(jnp.int32, (c_rows, c_rows), 1)
    gshift = grp.bit_length() - 1
    diag_mask = tril & ((rr >> gshift) == (ss >> gshift))
    nw = nw_ref[...]
    rows = lax.broadcasted_iota(jnp.int32, (c_rows, 1), 0)

    for s in range(ub):
        st = [st_ref[s, h] for h in range(nh)]
        for cc in range(uc):
            r0 = (s * uc + cc) * c_rows
            valid = ((c * uc + cc) * c_rows + rows) < t_valid if t_valid < t_pad else None
            ys, st = _hgrn_chunk(p_ref[r0:r0 + c_rows, :], st, lb, nw, tril, diag_mask, valid, grp)
            for h in range(nh):
                y_ref[r0:r0 + c_rows, h * dk:(h + 1) * dk] = ys[h].astype(y_ref.dtype)
        for h in range(nh):
            st_ref[s, h] = st[h]

    @pl.when(c == nc - 1)
    def _():
        for s in range(ub):
            for h in range(nh):
                so_ref[s, h] = st_ref[s, h].T


def _units_per_step(bsz, nc, max_seqs, max_chunks):
    if nc == 1:
        return max(u for u in (1, 2, 4, 8) if u <= max_seqs and bsz % u == 0), 1
    return 1, max(u for u in (1, 2, 4, 8) if u <= max_chunks and nc % u == 0)


def hgrn_call(p, lb_logits, norm_w, s0, bsz, t_pad, t_valid, layer):
    dim = p.shape[1] // 4
    nh = HG_HEADS
    dk = dim // nh
    c_rows = min(128, t_pad)
    grp = min(HG_CHUNK_GROUP, c_rows)
    nc = t_pad // c_rows
    ub, uc = _units_per_step(bsz, nc, 8, 8)
    rows = ub * uc * c_rows
    ncs = nc // uc
    has_s0 = s0 is not None
    in_specs = [pl.BlockSpec((rows, 4 * dim), lambda b, c: (b * ncs + c, 0)),
                pl.BlockSpec(lb_logits.shape, lambda b, c: (0, 0)),
                pl.BlockSpec((1, dim), lambda b, c: (0, 0))]
    args = [p, lb_logits, norm_w.reshape(1, dim)]
    if has_s0:
        in_specs.append(pl.BlockSpec((ub, nh, dk, dk), lambda b, c: (b, 0, 0, 0)))
        args.append(s0)
    kern = functools.partial(_hgrn_kernel, c_rows=c_rows, grp=grp, t_valid=t_valid, t_pad=t_pad,
                             layer=layer, has_s0=has_s0, ub=ub, uc=uc)
    return pl.pallas_call(
        kern,
        grid=(bsz // ub, ncs),
        in_specs=in_specs,
        out_specs=[pl.BlockSpec((rows, dim), lambda b, c: (b * ncs + c, 0)),
                   pl.BlockSpec((ub, nh, dk, dk), lambda b, c: (b, 0, 0, 0))],
        out_shape=[jax.ShapeDtypeStruct((bsz * t_pad, dim), bf16),
                   jax.ShapeDtypeStruct((bsz, nh, dk, dk), f32)],
        scratch_shapes=[pltpu.VMEM((ub, nh, dk, dk), f32)],
        compiler_params=_cparams(2),
        name="hgrn",
    )(*args)


def _expand_heads(w, ex_ref):
    hi = w.astype(bf16)
    r1 = w - hi.astype(f32)
    mid = r1.astype(bf16)
    lo = (r1 - mid.astype(f32)).astype(bf16)
    return jnp.dot(jnp.concatenate([hi, mid, lo], axis=1), ex_ref[...], preferred_element_type=f32)


def _ssd_kernel(*refs, c_rows, t_valid, t_pad, has_s0, ub, uc, conv_done):
    if has_s0:
        (zx_ref, cw_ref, cb_ref, dtb_ref, alog_ref, dsk_ref, nw_ref, ex_ref, s0_ref, buf_ref,
         y_ref, so_ref, nbuf_ref, st_ref, carry_ref) = refs
    else:
        (zx_ref, cw_ref, cb_ref, dtb_ref, alog_ref, dsk_ref, nw_ref, ex_ref,
         y_ref, so_ref, st_ref, carry_ref) = refs
    c = pl.program_id(1)
    nc = pl.num_programs(1)
    hp, ns, ngr = SSD_HEAD_DIM, SSD_STATE, SSD_GROUPS
    di = y_ref.shape[1]
    nh = di // hp
    hpg = nh // ngr
    gw = ngr * ns
    kconv = cw_ref.shape[0]
    pair = 2 * hp
    assert pair == LANES and hpg % 2 == 0

    @pl.when(c == 0)
    def _():
        carry_ref[...] = jnp.zeros_like(carry_ref)
        if has_s0:
            for s in range(ub):
                for r in range(di // LANES):
                    st_ref[s, :, r * LANES:(r + 1) * LANES] = s0_ref[s, r * LANES:(r + 1) * LANES, :].T
                carry_ref[s, SUBLANES - (kconv - 1):, :] = buf_ref[s]
        else:
            st_ref[...] = jnp.zeros_like(st_ref)

    cdim = cw_ref.shape[1]
    rows = lax.broadcasted_iota(jnp.int32, (c_rows, 1), 0)
    rows8 = rows[:SUBLANES]
    cw = cw_ref[...]
    a_neg = -jnp.exp(alog_ref[...])
    tril = _tril_mask(c_rows)
    dsk = dsk_ref[...]
    nw = nw_ref[...]
    lo = lax.broadcasted_iota(jnp.int32, (c_rows, LANES), 1) < hp
    for s in range(ub):
        for cc in range(uc):
            r0 = (s * uc + cc) * c_rows
            valid = ((c * uc + cc) * c_rows + rows) < t_valid if t_valid < t_pad else None
            _ssd_chunk(zx_ref[r0:r0 + c_rows, :], st_ref.at[s], carry_ref.at[s], y_ref.at[r0:r0 + c_rows, :],
                       cw, cb_ref[...], dtb_ref[...], a_neg, dsk, nw, ex_ref, tril, lo, rows8, valid, conv_done)

    @pl.when(c == nc - 1)
    def _():
        for s in range(ub):
            for r in range(di // LANES):
                so_ref[s, r * LANES:(r + 1) * LANES, :] = st_ref[s, :, r * LANES:(r + 1) * LANES].T
            if has_s0:
                last = s * c_rows + t_valid
                nbuf_ref[s] = zx_ref[last - (kconv - 1):last, di:di + cdim]


def _ssd_chunk(zx, st_ref, carry_ref, y_ref, cw, cb, dtb, a_neg, dsk, nw, ex_ref, tril, lo, rows8, valid, conv_done):
    hp, ns, ngr = SSD_HEAD_DIM, SSD_STATE, SSD_GROUPS
    c_rows = zx.shape[0]
    di = y_ref.shape[1]
    nh = di // hp
    hpg = nh // ngr
    gw = ngr * ns
    kconv, cdim = cw.shape
    pair = 2 * hp
    z = zx[:, :di]
    u = zx[:, di:di + cdim]
    if conv_done:
        xbc = u
    else:
        xbc = _silu(_causal_conv(u, carry_ref[...], cw, cb, rows8))
        carry_ref[...] = u[c_rows - SUBLANES:, :]
    dt = jax.nn.softplus(zx[:, di + cdim:] + dtb)
    if valid is not None:
        xbc = jnp.where(valid, xbc, 0.0)
        dt = jnp.where(valid, dt, 0.0)
    xs = xbc[:, :di]
    bm = xbc[:, di:di + gw]
    cm = xbc[:, di + gw:di + 2 * gw]
    acum = _cumsum_rows(dt * a_neg, tril.astype(bf16))

    def head_rows(a):
        if c_rows < LANES:
            a = jnp.concatenate([a, jnp.zeros((LANES - c_rows, LANES), f32)], axis=0)
        return a.T[:, :c_rows]

    acum_t = head_rows(acum)
    dt_t = head_rows(dt)
    a_last = acum[c_rows - 1:c_rows, :]
    xd = xs * _expand_heads(dt * jnp.exp(a_last - acum), ex_ref)

    ys = []
    for g in range(ngr):
        bg = bm[:, g * ns:(g + 1) * ns]
        cg = cm[:, g * ns:(g + 1) * ns]
        cb_mat = _dot_nt(cg, bg)
        gsl = slice(g * hpg * hp, (g + 1) * hpg * hp)
        y_inter = _dot(cg, st_ref[:, gsl])
        decs = []
        for q in range(hpg // 2):
            h1 = g * hpg + 2 * q
            psl = slice(h1 * hp, h1 * hp + pair)
            ms, es = [], []
            for h in (h1, h1 + 1):
                colb = jnp.broadcast_to(acum[:, h:h + 1], (c_rows, LANES))
                seg = colb[:, :c_rows] - acum_t[h:h + 1, :]
                ms.append(cb_mat * jnp.exp(jnp.where(tril, seg, -jnp.inf)) * dt_t[h:h + 1, :])
                es.append(jnp.exp(colb))
            xp = xs[:, psl]
            x_lo = jnp.where(lo, xp, 0.0)
            x_hi = jnp.where(lo, 0.0, xp)
            if (2 * c_rows) % LANES == 0:
                y_intra = _dot(jnp.concatenate(ms, axis=1), jnp.concatenate([x_lo, x_hi], axis=0))
            else:
                y_intra = _dot(ms[0], x_lo) + _dot(ms[1], x_hi)
            y_pair = y_intra + y_inter[:, q * pair:(q + 1) * pair] * jnp.where(lo, es[0], es[1])
            ys.append(y_pair + dsk[:, psl] * xp)
            decs.append(jnp.where(lo[0:1, :], jnp.exp(a_last[:, h1:h1 + 1]), jnp.exp(a_last[:, h1 + 1:h1 + 2])))
        st_ref[:, gsl] = st_ref[:, gsl] * jnp.concatenate(decs, axis=1) + _dot_tn(bg, xd[:, gsl])

    y = jnp.concatenate(ys, axis=1) * _silu(z)
    gdim = di // ngr
    for g in range(ngr):
        sl = slice(g * gdim, (g + 1) * gdim)
        yg = y[:, sl]
        y_ref[:, sl] = (yg * lax.rsqrt(jnp.mean(yg * yg, axis=-1, keepdims=True) + RMS_EPS) * nw[:, sl]).astype(y_ref.dtype)


def ssd_call(zx, conv_w, conv_b, dt_bias, a_log, d_skip, norm_w, s0, conv_buf, bsz, t_pad, t_valid, conv_done):
    hp, ns = SSD_HEAD_DIM, SSD_STATE
    conv_dim = conv_w.shape[1]
    di = zx.shape[1] - conv_dim - LANES
    nh = di // hp
    c_rows = min(128, t_pad)
    nc = t_pad // c_rows
    ub, uc = _units_per_step(bsz, nc, 4, 4)
    rows = ub * uc * c_rows
    ncs = nc // uc
    has_s0 = s0 is not None
    assert not (conv_done and has_s0)
    pad_h = lambda a: jnp.pad(a.reshape(1, nh), ((0, 0), (0, LANES - nh)))
    full = lambda shape: pl.BlockSpec(shape, lambda b, c: (0,) * len(shape))
    sel = (jnp.arange(di)[None, :] // hp == jnp.arange(LANES)[:, None]).astype(bf16)
    expand = jnp.concatenate([sel, sel, sel], axis=0)
    in_specs = [pl.BlockSpec((rows, zx.shape[1]), lambda b, c: (b * ncs + c, 0)),
                full(conv_w.shape), full((1, conv_dim)), full((1, LANES)), full((1, LANES)),
                full((1, di)), full((1, di)), full(expand.shape)]
    args = [zx, conv_w, conv_b.reshape(1, conv_dim), pad_h(dt_bias), pad_h(a_log),
            jnp.repeat(d_skip, hp).reshape(1, di), norm_w.reshape(1, di), expand]
    out_specs = [pl.BlockSpec((rows, di), lambda b, c: (b * ncs + c, 0)),
                 pl.BlockSpec((ub, di, ns), lambda b, c: (b, 0, 0))]
    out_shape = [jax.ShapeDtypeStruct((bsz * t_pad, di), bf16), jax.ShapeDtypeStruct((bsz, di, ns), f32)]
    if has_s0:
        assert nc == 1 and t_valid >= conv_w.shape[0] - 1
        in_specs += [pl.BlockSpec((ub, di, ns), lambda b, c: (b, 0, 0)),
                     pl.BlockSpec((ub,) + conv_buf.shape[1:], lambda b, c: (b, 0, 0))]
        args += [s0.reshape(bsz, di, ns), conv_buf]
        out_specs.append(pl.BlockSpec((ub,) + conv_buf.shape[1:], lambda b, c: (b, 0, 0)))
        out_shape.append(jax.ShapeDtypeStruct(conv_buf.shape, f32))
    kern = functools.partial(_ssd_kernel, c_rows=c_rows, t_valid=t_valid, t_pad=t_pad, has_s0=has_s0, ub=ub, uc=uc,
                             conv_done=conv_done)
    outs = pl.pallas_call(
        kern,
        grid=(bsz // ub, ncs),
        in_specs=in_specs,
        out_specs=out_specs,
        out_shape=out_shape,
        scratch_shapes=[pltpu.VMEM((ub, ns, di), f32), pltpu.VMEM((ub, SUBLANES, conv_dim), f32)],
        compiler_params=_cparams(2),
        name="ssd",
    )(*args)
    return outs[0], outs[1].reshape(bsz, nh, hp, ns), (outs[2] if has_s0 else None)


def _trunk(x, c_k, c_v, s_hg, s_ssm, s_sconv, s_fconv, prm):
    bsz, t, d = x.shape
    depth = prm["norm_mix_w"].shape[0]
    decode = c_k is not None
    tp = -(-t // SUBLANES) * SUBLANES
    if tp != t:
        x = jnp.pad(x, ((0, 0), (0, tp - t), (0, 0)))
    m = bsz * tp
    n_heads = d // ATT_HEAD_DIM
    qd = n_heads * ATT_HEAD_DIM
    kvd = ATT_KV_HEADS * ATT_HEAD_DIM
    x2 = x.reshape(m, d)
    xn, xn_w = x2, prm["norm_mix_w"][0]
    nk, nv, nhg, nssm, nsconv, nfconv = [], [], [], [], [], []
    if decode:
        ck_all = jnp.transpose(c_k, (0, 1, 3, 4, 2)).reshape(c_k.shape[0], bsz, kvd, WINDOW)
        cv_all = jnp.transpose(c_v, (0, 1, 3, 4, 2)).reshape(c_v.shape[0], bsz, kvd, WINDOW)
    slot = [0] * N_MIXERS
    for i in range(depth):
        kind = i % N_MIXERS
        j = slot[kind]
        slot[kind] += 1
        nw_ffn = prm["norm_ffn_w"][i]
        if kind == 0:
            w_qkv, b_qkv = prm["attn_w_qkv"], prm["attn_b_qkv"][j]
            nq = w_qkv.shape[2]
            if decode:
                qkv = proj_call(xn, xn_w, w_qkv, j, b_qkv, f32, "attn_qkv").reshape(bsz, tp, nq)
                o, k_new, v_new = attn_cache_call(qkv, ck_all, cv_all, j, prm["attn_sinks"][j], n_heads, t)
                o = o.reshape(m, qd)
            else:
                qkv = proj_call(xn, xn_w, w_qkv, j, b_qkv, bf16, "attn_qkv")
                o = attn_prompt_call(qkv, prm["attn_sinks"][j], bsz, t, n_heads)
                xn_last = xn.reshape(bsz, t, d)[:, t - WINDOW:].reshape(bsz * WINDOW, d)
                kv_last = proj_call(xn_last, xn_w, w_qkv[:, :, qd:], j, b_qkv[qd:], f32,
                                    "attn_kv_tail").reshape(bsz, WINDOW, 2, kvd)
                to4 = lambda a: a.reshape(bsz, WINDOW, ATT_KV_HEADS, ATT_HEAD_DIM)
                k_new, v_new = to4(kv_last[:, :, 0]), to4(kv_last[:, :, 1])
            nk.append(k_new)
            nv.append(v_new)
            act, w_o, b_o = o, (prm["attn_w_o"], j), prm["attn_b_o"][j]
        elif kind == 1:
            p = proj_call(xn, xn_w, prm["hgrn_w_in"], j, None, f32, "hgrn_in")
            y, s_new = hgrn_call(p, prm["hgrn_lb_logits"], prm["hgrn_norm_w"][j],
                                 None if s_hg is None else s_hg[j], bsz, tp, t, i)
            nhg.append(s_new)
            act, w_o, b_o = y, (prm["hgrn_w_o"], j), None
        else:
            conv_dim = prm["ssd_conv_w"].shape[2]
            di = prm["ssd_w_o"].shape[1]
            kc = prm["ssd_conv_w"].shape[1]
            if s_sconv is None:
                zx, pre = ssd_in_call(xn, prm["ssd_w_in"], j, prm["ssd_conv_w"][j], prm["ssd_conv_b"][j], tp)
                pre = pre.reshape(bsz, tp // min(PROJ_TM, m), SUBLANES, conv_dim)[:, -1, SUBLANES - (kc - 1):]
            else:
                zx = proj_call(xn, xn_w, prm["ssd_w_in"], j, None, f32, "ssd_in")
            y, s_new, conv_new = ssd_call(zx, prm["ssd_conv_w"][j], prm["ssd_conv_b"][j], prm["ssd_dt_bias"][j],
                                prm["ssd_a_log"][j], prm["ssd_d"][j], prm["ssd_norm_w"][j],
                                None if s_ssm is None else s_ssm[j],
                                None if s_sconv is None else s_sconv[j], bsz, tp, t, s_sconv is None)
            nssm.append(s_new)
            nsconv.append(pre if s_sconv is None else conv_new)
            act, w_o, b_o = y, (prm["ssd_w_o"], j), None
        last = i == depth - 1
        next_w = prm["norm_final_w"] if last else prm["norm_mix_w"][i + 1]
        dff = prm["ffn_w_down"].shape[1]
        kf = prm["ffn_conv_w"].shape[1]
        x2, xn, tail = ffn_call(act, w_o, b_o, x2, nw_ffn, (prm["ffn_w_up"], i), prm["ffn_conv_w"][i],
                                prm["ffn_conv_b"][i], (prm["ffn_w_down"], i), next_w, tp,
                                None if s_fconv is None else (s_fconv, i), t, f32 if last else bf16, "ffn")
        xn_w = None
        if s_fconv is not None:
            nfconv.append(tail)
        else:
            tiles_per_seq = tp // tail_tile_rows(m)
            tl = tail.reshape(bsz, tiles_per_seq, SUBLANES, dff)[:, -1]
            nfconv.append(tl[:, SUBLANES - (kf - 1):])
    stack = lambda xs: xs[0][None] if len(xs) == 1 else jnp.stack(xs)
    y = xn.reshape(bsz, tp, d)[:, :t]
    nk, nv = stack(nk), stack(nv)
    if decode:
        unview = lambda a: jnp.transpose(a.reshape(a.shape[0], bsz, ATT_KV_HEADS, ATT_HEAD_DIM, WINDOW), (0, 1, 4, 2, 3))
        nk, nv = unview(nk), unview(nv)
    return (y, nk, nv, stack(nhg), stack(nssm), stack(nsconv), stack(nfconv))


def tail_tile_rows(m):
    return min(FFN_TM, m)


def kernel(x_prompt, x_sample, cache_attn_k, cache_attn_v, state_hgrn, state_ssm, state_ssm_conv, state_ffn_conv, norm_mix_w, norm_ffn_w, norm_final_w, attn_w_qkv, attn_b_qkv, attn_sinks, attn_w_o, attn_b_o, hgrn_w_in, hgrn_lb_logits, hgrn_norm_w, hgrn_w_o, ssd_w_in, ssd_conv_w, ssd_conv_b, ssd_dt_bias, ssd_a_log, ssd_d, ssd_norm_w, ssd_w_o, ffn_w_up, ffn_conv_w, ffn_conv_b, ffn_w_down):
    cast = lambda w: w.astype(bf16)
    ssd_in_pad = (-ssd_w_in.shape[2]) % LANES
    prm = dict(norm_mix_w=norm_mix_w, norm_ffn_w=norm_ffn_w, norm_final_w=norm_final_w,
               attn_w_qkv=cast(attn_w_qkv), attn_b_qkv=attn_b_qkv, attn_sinks=attn_sinks,
               attn_w_o=cast(attn_w_o), attn_b_o=attn_b_o,
               hgrn_w_in=cast(hgrn_w_in), hgrn_lb_logits=hgrn_lb_logits, hgrn_norm_w=hgrn_norm_w,
               hgrn_w_o=cast(hgrn_w_o),
               ssd_w_in=jnp.pad(cast(ssd_w_in), ((0, 0), (0, 0), (0, ssd_in_pad))), ssd_conv_w=ssd_conv_w,
               ssd_conv_b=ssd_conv_b, ssd_dt_bias=ssd_dt_bias,
               ssd_a_log=ssd_a_log, ssd_d=ssd_d, ssd_norm_w=ssd_norm_w, ssd_w_o=cast(ssd_w_o),
               ffn_w_up=cast(ffn_w_up), ffn_conv_w=ffn_conv_w, ffn_conv_b=ffn_conv_b, ffn_w_down=cast(ffn_w_down))
    outs_p = _trunk(x_prompt, None, None, None, None, None, None, prm)
    outs_s = _trunk(x_sample, cache_attn_k, cache_attn_v, state_hgrn, state_ssm, state_ssm_conv,
                    state_ffn_conv, prm)
    return (outs_p[0], outs_s[0]) + outs_p[1:] + outs_s[1:]
```

```python
import functools

import jax
import jax.numpy as jnp
from jax import lax
from jax.experimental import pallas as pl
from jax.experimental.pallas import tpu as pltpu

bf16 = jnp.bfloat16
f32 = jnp.float32

RMS_EPS = 1e-6
N_MIXERS = 3
WINDOW = 128
PAST_LEN = 8192
ATT_HEAD_DIM = 64
ATT_KV_HEADS = 4
HG_HEADS = 8
HG_CHUNK_GROUP = 32
SSD_HEAD_DIM = 64
SSD_STATE = 128
SSD_GROUPS = 4

LANES = 128
SUBLANES = 8
VMEM_LIMIT = 56 * 1024 * 1024
PROJ_TM = 512
FFN_TM = 512


def _cparams(n_axes):
    return pltpu.CompilerParams(dimension_semantics=("arbitrary",) * n_axes,
                                vmem_limit_bytes=VMEM_LIMIT)


def _dot(a, b):
    return jnp.dot(a.astype(bf16), b.astype(bf16), preferred_element_type=f32)


def _dot_nt(a, b):
    return lax.dot_general(a.astype(bf16), b.astype(bf16), (((1,), (1,)), ((), ())),
                           preferred_element_type=f32)


def _dot_tn(a, b):
    return lax.dot_general(a.astype(bf16), b.astype(bf16), (((0,), (0,)), ((), ())),
                           preferred_element_type=f32)


def _silu(x):
    return x * jax.nn.sigmoid(x)


def _rms(x, w):
    ms = jnp.mean(x * x, axis=-1, keepdims=True)
    return x * lax.rsqrt(ms + RMS_EPS) * w


def _cumsum_rows(x, tril):
    hi = x.astype(bf16)
    r1 = x - hi.astype(f32)
    mid = r1.astype(bf16)
    lo = (r1 - mid.astype(f32)).astype(bf16)
    d = functools.partial(jnp.dot, preferred_element_type=f32)
    return d(tril, hi) + d(tril, mid) + d(tril, lo)


def _tril_mask(c):
    r = lax.broadcasted_iota(jnp.int32, (c, c), 0)
    s = lax.broadcasted_iota(jnp.int32, (c, c), 1)
    return r >= s


def _resident(shape):
    return pl.BlockSpec(shape, lambda *_: (0,) * len(shape), pipeline_mode=pl.Buffered(1))


def _resident_layer(w, layer):
    return pl.BlockSpec((None,) + w.shape[1:], lambda *_: (layer, 0, 0), pipeline_mode=pl.Buffered(1))


def _proj_kernel(x_ref, nw_ref, w_ref, b_ref, o_ref):
    x = x_ref[...]
    if x.dtype != bf16:
        x = _rms(x, nw_ref[...]).astype(bf16)
    acc = jnp.dot(x, w_ref[...], preferred_element_type=f32)
    o_ref[...] = (acc + b_ref[...]).astype(o_ref.dtype)


def proj_call(x, norm_w, w, layer, bias, out_dtype, name):
    m, k = x.shape
    n = w.shape[2]
    tm = min(PROJ_TM, m)
    if bias is None:
        bias = jnp.zeros((n,), f32)
    if norm_w is None:
        assert x.dtype == bf16
        norm_w = jnp.ones((k,), f32)
    return pl.pallas_call(
        _proj_kernel,
        grid=(m // tm,),
        in_specs=[pl.BlockSpec((tm, k), lambda i: (i, 0)), _resident((1, k)), _resident_layer(w, layer),
                  _resident((1, n))],
        out_specs=pl.BlockSpec((tm, n), lambda i: (i, 0)),
        out_shape=jax.ShapeDtypeStruct((m, n), out_dtype),
        compiler_params=_cparams(1),
        name=name,
    )(x, norm_w.reshape(1, k), w, bias.reshape(1, n))


def _causal_conv(u, prev, cw, cb, rows8):
    kconv = cw.shape[0]
    yc = cb
    for kk in range(kconv - 1):
        back = kconv - 1 - kk
        sh = pltpu.roll(u, back, 0)
        top = sh[:SUBLANES]
        for r0 in range(back):
            top = jnp.where(rows8 == r0, prev[SUBLANES - back + r0:SUBLANES - back + r0 + 1, :], top)
        sh = jnp.concatenate([top, sh[SUBLANES:]], axis=0) if u.shape[0] > SUBLANES else top
        yc = yc + sh * cw[kk:kk + 1, :]
    return yc + u * cw[kconv - 1:kconv, :]


def _ssd_in_kernel(x_ref, w_ref, cw_ref, cb_ref, o_ref, tail_ref, carry_ref, *, tm, tc, t_seq, di):
    i = pl.program_id(0)
    cdim = cw_ref.shape[1]

    @pl.when(((i * tm) & (t_seq - 1)) == 0)
    def _():
        carry_ref[...] = jnp.zeros_like(carry_ref)

    x = x_ref[...]
    rows8 = lax.broadcasted_iota(jnp.int32, (SUBLANES, 1), 0)
    nz = di // tc
    for idx, c0 in enumerate(range(0, cdim, tc)):
        cs = slice(c0, c0 + tc)
        u = jnp.dot(x, w_ref[:, di + c0:di + c0 + tc], preferred_element_type=f32)
        tail_ref[0, :, cs] = u[tm - SUBLANES:, :]
        yc = _causal_conv(u, carry_ref[:, cs], cw_ref[:, cs], cb_ref[:, cs], rows8)
        carry_ref[:, cs] = u[tm - SUBLANES:, :]
        o_ref[:, di + c0:di + c0 + tc] = _silu(yc)
        if idx < nz:
            o_ref[:, idx * tc:(idx + 1) * tc] = jnp.dot(x, w_ref[:, idx * tc:(idx + 1) * tc],
                                                        preferred_element_type=f32)
    o_ref[:, di + cdim:] = jnp.dot(x, w_ref[:, di + cdim:], preferred_element_type=f32)


def ssd_in_call(xn, w, layer, conv_w, conv_b, t_seq):
    m, k = xn.shape
    n = w.shape[2]
    cdim = conv_w.shape[1]
    di = n - cdim - LANES
    tm = min(PROJ_TM, m)
    tc = 2 * LANES
    assert t_seq % tm == 0 and cdim % tc == 0 and di % tc == 0 and di <= cdim and (t_seq & (t_seq - 1)) == 0
    kern = functools.partial(_ssd_in_kernel, tm=tm, tc=tc, t_seq=t_seq, di=di)
    return pl.pallas_call(
        kern,
        grid=(m // tm,),
        in_specs=[pl.BlockSpec((tm, k), lambda i: (i, 0)), _resident_layer(w, layer),
                  _resident(conv_w.shape), _resident((1, cdim))],
        out_specs=[pl.BlockSpec((tm, n), lambda i: (i, 0)), pl.BlockSpec((1, SUBLANES, cdim), lambda i: (i, 0, 0))],
        out_shape=[jax.ShapeDtypeStruct((m, n), f32), jax.ShapeDtypeStruct((m // tm, SUBLANES, cdim), f32)],
        scratch_shapes=[pltpu.VMEM((SUBLANES, cdim), f32)],
        compiler_params=_cparams(1),
        name="ssd_in_conv",
    )(xn, w, conv_w, conv_b.reshape(1, cdim))


def _ffn_kernel(*refs, tm, tf, t_seq, t_valid, has_state):
    if has_state:
        (act_ref, wo_ref, bo_ref, x_ref, nw1_ref, wu_ref, cw_ref, cb_ref, wd_ref, nw2_ref, buf_ref,
         xo_ref, xn2_ref, nst_ref, carry_ref) = refs
    else:
        (act_ref, wo_ref, bo_ref, x_ref, nw1_ref, wu_ref, cw_ref, cb_ref, wd_ref, nw2_ref,
         xo_ref, xn2_ref, tail_ref, carry_ref) = refs
    i = pl.program_id(0)
    dff = wd_ref.shape[0]

    x1 = jnp.dot(act_ref[...].astype(bf16), wo_ref[...], preferred_element_type=f32) + bo_ref[...] + x_ref[...]
    xn = _rms(x1, nw1_ref[...]).astype(bf16)
    rows8 = lax.broadcasted_iota(jnp.int32, (SUBLANES, 1), 0)
    if not has_state:
        @pl.when(((i * tm) & (t_seq - 1)) == 0)
        def _():
            carry_ref[...] = jnp.zeros_like(carry_ref)
    gs = []
    for c0 in range(0, dff, tf):
        cs = slice(c0, c0 + tf)
        a = jnp.dot(xn, wu_ref[:, cs], preferred_element_type=f32)
        b = jnp.dot(xn, wu_ref[:, dff + c0:dff + c0 + tf], preferred_element_type=f32)
        r1 = pltpu.roll(a, 1, 0)
        r2 = pltpu.roll(a, 2, 0)
        if has_state:
            p1s, p2s = [], []
            for s in range(tm // SUBLANES):
                rs = slice(s * SUBLANES, (s + 1) * SUBLANES)
                b0, b1 = buf_ref[s, 0:1, cs], buf_ref[s, 1:2, cs]
                p1s.append(jnp.where(rows8 == 0, b1, r1[rs, :]))
                p2s.append(jnp.where(rows8 == 0, b0, jnp.where(rows8 == 1, b1, r2[rs, :])))
                nst_ref[s, :, cs] = a[s * SUBLANES + t_valid - 2:s * SUBLANES + t_valid, :]
            p1 = jnp.concatenate(p1s, axis=0)
            p2 = jnp.concatenate(p2s, axis=0)
        else:
            tail_ref[0, :, cs] = a[tm - SUBLANES:, :]
            carry = carry_ref[:, cs]
            top1 = jnp.where(rows8 == 0, carry[7:8, :], r1[:SUBLANES])
            top2 = jnp.where(rows8 == 0, carry[6:7, :], jnp.where(rows8 == 1, carry[7:8, :], r2[:SUBLANES]))
            p1 = jnp.concatenate([top1, r1[SUBLANES:]], axis=0)
            p2 = jnp.concatenate([top2, r2[SUBLANES:]], axis=0)
            carry_ref[:, cs] = a[tm - SUBLANES:, :]
        y = cb_ref[:, cs] + p2 * cw_ref[0:1, cs]
        y = y + p1 * cw_ref[1:2, cs]
        y = y + a * cw_ref[2:3, cs]
        gs.append((_silu(y) * b).astype(bf16))
    g = jnp.concatenate(gs, axis=1)
    xo = jnp.dot(g, wd_ref[...], preferred_element_type=f32) + x1
    xo_ref[...] = xo
    xn2_ref[...] = _rms(xo, nw2_ref[...]).astype(xn2_ref.dtype)


def ffn_call(act, w_o, b_o, x, ffn_norm_w, w_up, conv_w, conv_b, w_down, next_w, t_seq, buf, t_valid, out_dtype, name):
    m, d = x.shape
    ko = act.shape[1]
    dff = w_down[0].shape[1]
    tf = 2 * LANES
    assert dff % tf == 0 and (t_seq & (t_seq - 1)) == 0
    has_state = buf is not None
    tm = min(FFN_TM // 2 if has_state else FFN_TM, m)
    assert (t_seq == SUBLANES and t_valid >= 2 and conv_w.shape[0] == 3) if has_state else (t_seq % tm == 0)
    if b_o is None:
        b_o = jnp.zeros((d,), f32)
    row = lambda w: pl.BlockSpec((tm, w), lambda i: (i, 0))
    in_specs = [row(ko), _resident_layer(*w_o), _resident((1, d)), row(d), _resident((1, d)),
                _resident_layer(*w_up), _resident((3, dff)), _resident((1, dff)),
                _resident_layer(*w_down), _resident((1, d))]
    args = [act, w_o[0], b_o.reshape(1, d), x, ffn_norm_w.reshape(1, d),
            w_up[0], conv_w, conv_b.reshape(1, dff), w_down[0], next_w.reshape(1, d)]
    if has_state:
        nseq = tm // t_seq
        in_specs.append(pl.BlockSpec((None, nseq, 2, dff), lambda i: (buf[1], i, 0, 0)))
        args.append(buf[0])
        third_spec = pl.BlockSpec((nseq, 2, dff), lambda i: (i, 0, 0))
        third_shape = jax.ShapeDtypeStruct((m // t_seq, 2, dff), f32)
    else:
        third_spec = pl.BlockSpec((1, SUBLANES, dff), lambda i: (i, 0, 0))
        third_shape = jax.ShapeDtypeStruct((m // tm, SUBLANES, dff), f32)
    kern = functools.partial(_ffn_kernel, tm=tm, tf=tf, t_seq=t_seq, t_valid=t_valid, has_state=has_state)
    return pl.pallas_call(
        kern,
        grid=(m // tm,),
        in_specs=in_specs,
        out_specs=[row(d), row(d), third_spec],
        out_shape=[jax.ShapeDtypeStruct((m, d), f32), jax.ShapeDtypeStruct((m, d), out_dtype), third_shape],
        scratch_shapes=[pltpu.VMEM((SUBLANES, dff), f32)],
        compiler_params=_cparams(1),
        name=name,
    )(*args)


def _attn_softmax_pv(s, sink_col, vj):
    m = jnp.maximum(jnp.max(s, axis=-1, keepdims=True), sink_col)
    p = jnp.exp(s - m)
    l = jnp.sum(p, axis=-1, keepdims=True) + jnp.exp(sink_col - m)
    return _dot(p, vj) / l


def _sink_col(sinks_ref, j, group, rows_per_head):
    r = lax.broadcasted_iota(jnp.int32, (group * rows_per_head, 1), 0)
    col = jnp.full((group * rows_per_head, 1), sinks_ref[j * group], f32)
    for p in range(1, group):
        col = jnp.where(r >= p * rows_per_head, sinks_ref[j * group + p], col)
    return col


def _attn_prompt_kernel(sinks_ref, q_ref, kvc_ref, kvp_ref, o_ref, *, n_heads):
    n = pl.program_id(1)
    w, hd, kvh = WINDOW, ATT_HEAD_DIM, ATT_KV_HEADS
    group = n_heads // kvh
    kvw = kvh * hd
    assert group == kvh
    kv = jnp.concatenate([kvp_ref[...], kvc_ref[...]], axis=0)
    k2 = kv[:, :kvw]
    v2 = kv[:, kvw:]
    i = lax.broadcasted_iota(jnp.int32, (w, 2 * w), 0)
    c = lax.broadcasted_iota(jnp.int32, (w, 2 * w), 1)
    band = (c > i) & (c <= i + w) & ((n > 0) | (c >= w))
    mask = jnp.concatenate([band] * group, axis=0)
    slot = lax.broadcasted_iota(jnp.int32, (w, kvw), 1) >> (hd.bit_length() - 1)
    for j in range(kvh):
        qg = q_ref[:, j * kvw:(j + 1) * kvw] * (hd ** -0.5)
        qs = jnp.concatenate([jnp.where(slot == j, pltpu.roll(qg, ((j - p) % kvh) * hd, 1), 0)
                              for p in range(group)], axis=0)
        s = jnp.where(mask, _dot_nt(qs, k2), -jnp.inf)
        of = _attn_softmax_pv(s, _sink_col(sinks_ref, j, group, w), v2).astype(o_ref.dtype)
        og = of[0:w, :]
        for p in range(group):
            piece = of[p * w:(p + 1) * w, :]
            og = jnp.where(slot == p, pltpu.roll(piece, ((p - j) % kvh) * hd, 1), og)
        o_ref[:, j * kvw:(j + 1) * kvw] = og


def attn_prompt_call(qkv, sinks, bsz, t, n_heads):
    w, hd, kvh = WINDOW, ATT_HEAD_DIM, ATT_KV_HEADS
    nb = t // w
    qd = n_heads * hd
    kvd = 2 * kvh * hd
    assert qd % kvd == 0
    kvblk = qd // kvd
    kern = functools.partial(_attn_prompt_kernel, n_heads=n_heads)
    return pl.pallas_call(
        kern,
        grid=(bsz, nb),
        in_specs=[pl.BlockSpec(memory_space=pltpu.SMEM),
                  pl.BlockSpec((w, qd), lambda b, n: (b * nb + n, 0)),
                  pl.BlockSpec((w, kvd), lambda b, n: (b * nb + n, kvblk)),
                  pl.BlockSpec((w, kvd), lambda b, n: (b * nb + jnp.maximum(n - 1, 0), kvblk))],
        out_specs=pl.BlockSpec((w, qd), lambda b, n: (b * nb + n, 0)),
        out_shape=jax.ShapeDtypeStruct((bsz * t, qd), bf16),
        compiler_params=_cparams(2),
        name="attn_prompt",
    )(sinks, qkv, qkv, qkv)


def _shifted_window(cache, new_rows, t_valid):
    tp, w = new_rows.shape[0], cache.shape[1]
    hi = new_rows.astype(bf16)
    r1 = new_rows - hi.astype(f32)
    mid = r1.astype(bf16)
    lo = (r1 - mid.astype(f32)).astype(bf16)
    t = lax.broadcasted_iota(jnp.int32, (tp, w), 0)
    col = lax.broadcasted_iota(jnp.int32, (tp, w), 1)
    place = ((col == w - t_valid + t) & (t < t_valid)).astype(bf16)
    placed = _dot_tn(jnp.concatenate([hi, mid, lo], axis=0), jnp.concatenate([place] * 3, axis=0))
    lane = lax.broadcasted_iota(jnp.int32, cache.shape, 1)
    return jnp.where(lane >= w - t_valid, placed, pltpu.roll(cache, w - t_valid, 1))


def _attn_cache_kernel(sinks_ref, q_ref, ck_ref, cv_ref, o_ref, nk_ref, nv_ref, *, n_heads, t_valid, bt):
    w, hd, kvh = WINDOW, ATT_HEAD_DIM, ATT_KV_HEADS
    group = n_heads // kvh
    tp = q_ref.shape[1]
    qd = n_heads * hd
    tq = lax.broadcasted_iota(jnp.int32, (tp, w), 0)
    jc = lax.broadcasted_iota(jnp.int32, (tp, w), 1)
    mc = (jc > tq) & (jc - w + PAST_LEN >= 0)
    tq2 = lax.broadcasted_iota(jnp.int32, (tp, tp), 0)
    un = lax.broadcasted_iota(jnp.int32, (tp, tp), 1)
    mn = (un <= tq2) & (un < t_valid)
    mask_c = jnp.concatenate([mc] * n_heads, axis=0)
    mask_n = jnp.concatenate([mn] * n_heads, axis=0)
    kvw = kvh * hd
    assert group == kvh
    slot = lax.broadcasted_iota(jnp.int32, (tp, kvw), 1) >> (hd.bit_length() - 1)
    sink = _sink_col(sinks_ref, 0, n_heads, tp)

    for bi in range(bt):
        row = q_ref[bi]
        ck = ck_ref[bi]
        cv = cv_ref[bi]
        kn = row[:, qd:qd + kvw]
        vn = row[:, qd + kvw:]
        nk_ref[bi] = _shifted_window(ck, kn, t_valid)
        nv_ref[bi] = _shifted_window(cv, vn, t_valid)
        pieces = []
        for j in range(kvh):
            qg = row[:, j * kvw:(j + 1) * kvw] * (hd ** -0.5)
            for p in range(group):
                pieces.append(jnp.where(slot == j, pltpu.roll(qg, ((j - p) % kvh) * hd, 1), 0.0))
        qs = jnp.concatenate(pieces, axis=0)
        sc = jnp.where(mask_c, _dot(qs, ck), -jnp.inf)
        sn = jnp.where(mask_n, _dot_nt(qs, kn), -jnp.inf)
        m = jnp.maximum(jnp.maximum(jnp.max(sc, axis=-1, keepdims=True),
                                    jnp.max(sn, axis=-1, keepdims=True)), sink)
        pc = jnp.exp(sc - m)
        pn = jnp.exp(sn - m)
        l = jnp.sum(pc, axis=-1, keepdims=True) + jnp.sum(pn, axis=-1, keepdims=True) + jnp.exp(sink - m)
        of = (_dot_nt(pc, cv) + _dot(pn, vn)) / l
        for j in range(kvh):
            og = jnp.zeros((tp, kvw), f32)
            for p in range(group):
                h = j * group + p
                og = og + jnp.where(slot == p, pltpu.roll(of[h * tp:(h + 1) * tp, :], ((p - j) % kvh) * hd, 1), 0.0)
            o_ref[bi, :, j * kvw:(j + 1) * kvw] = og


def attn_cache_call(qkv, cache_k, cache_v, layer, sinks, n_heads, t_valid):
    bsz, tp, nq = qkv.shape
    w = WINDOW
    kd = cache_k.shape[2]
    qd = n_heads * ATT_HEAD_DIM
    bt = 8
    kern = functools.partial(_attn_cache_kernel, n_heads=n_heads, t_valid=t_valid, bt=bt)
    return pl.pallas_call(
        kern,
        grid=(bsz // bt,),
        in_specs=[pl.BlockSpec(memory_space=pltpu.SMEM),
                  pl.BlockSpec((bt, tp, nq), lambda b: (b, 0, 0)),
                  pl.BlockSpec((None, bt, kd, w), lambda b: (layer, b, 0, 0)),
                  pl.BlockSpec((None, bt, kd, w), lambda b: (layer, b, 0, 0))],
        out_specs=[pl.BlockSpec((bt, tp, qd), lambda b: (b, 0, 0)),
                   pl.BlockSpec((bt, kd, w), lambda b: (b, 0, 0)),
                   pl.BlockSpec((bt, kd, w), lambda b: (b, 0, 0))],
        out_shape=[jax.ShapeDtypeStruct((bsz, tp, qd), f32),
                   jax.ShapeDtypeStruct((bsz, kd, w), f32),
                   jax.ShapeDtypeStruct((bsz, kd, w), f32)],
        compiler_params=_cparams(1),
        name="attn_cache",
    )(sinks, qkv, cache_k, cache_v)


def _hgrn_chunk(x, st, lb, nw, tril, diag_mask, valid, grp):
    nh = HG_HEADS
    c_rows = x.shape[0]
    dim = x.shape[1] // 4
    dk = dim // nh
    ng = c_rows // grp
    if valid is not None:
        x = jnp.where(valid, x, 0.0)
    q = _silu(x[:, 0:dim])
    fg = lb + (1.0 - lb) * jax.nn.sigmoid(x[:, dim:2 * dim])
    logf = jnp.log(fg)
    k = 1.0 - fg
    v = x[:, 2 * dim:3 * dim]
    gate = x[:, 3 * dim:4 * dim]
    if valid is not None:
        logf = jnp.where(valid, logf, 0.0)
        k = jnp.where(valid, k, 0.0)
    bcum = _cumsum_rows(logf, tril.astype(bf16))

    def rows_of(vals):
        return jnp.concatenate([jnp.broadcast_to(r, (grp, dk)) for r in vals], axis=0)

    ys, st_out = [], []
    for h in range(nh):
        sl = slice(h * dk, (h + 1) * dk)
        b = bcum[:, sl]
        qh, kh, vh = q[:, sl], k[:, sl], v[:, sl]
        zero = jnp.zeros((1, dk), f32)
        r = [zero] + [b[i * grp - 1:i * grp, :] for i in range(1, ng)] + [b[c_rows - 1:c_rows, :]]
        mid = [b[i * grp + grp // 2 - 1:i * grp + grp // 2, :] for i in range(ng)]
        b_last = r[ng]
        r_start = rows_of(r[:ng])
        r_end = rows_of(r[1:])
        r_mid = rows_of(mid)
        qg = qh * jnp.exp(b - r_start)
        kt = kh * jnp.exp(r_end - b)
        qm = qh * jnp.exp(b - r_mid)
        km = kh * jnp.exp(r_mid - b)
        a = jnp.where(diag_mask, _dot_nt(qm, km), 0.0)
        if ng > 1:
            lhs, rhs = [], []
            for j in range(ng - 1):
                lhs.append(jnp.concatenate(
                    [jnp.zeros((grp, dk), f32) if i <= j else qg[i * grp:(i + 1) * grp, :] * jnp.exp(r[i] - r[j + 1])
                     for i in range(ng)], axis=0))
                rhs.append(jnp.concatenate(
                    [kt[i * grp:(i + 1) * grp, :] if i == j else jnp.zeros((grp, dk), f32) for i in range(ng)], axis=0))
            a = a + _dot_nt(jnp.concatenate(lhs, axis=1), jnp.concatenate(rhs, axis=1))
        q_inter = qg * jnp.exp(r_start)
        o = _dot(a, vh) + _dot_nt(q_inter, st[h])
        k_dec = kt * jnp.exp(b_last - r_end)
        st_out.append(st[h] * jnp.exp(b_last) + _dot_tn(vh, k_dec))
        on = o * lax.rsqrt(jnp.mean(o * o, axis=-1, keepdims=True) + RMS_EPS)
        ys.append(on * nw[:, sl] * _silu(gate[:, sl]))
    return ys, st_out


def _hgrn_kernel(*refs, c_rows, grp, t_valid, t_pad, layer, has_s0, ub, uc):
    if has_s0:
        p_ref, lbl_ref, nw_ref, s0_ref, y_ref, so_ref, st_ref = refs
    else:
        p_ref, lbl_ref, nw_ref, y_ref, so_ref, st_ref = refs
    c = pl.program_id(1)
    nc = pl.num_programs(1)
    nh = HG_HEADS
    dim = p_ref.shape[1] // 4
    dk = dim // nh

    @pl.when(c == 0)
    def _():
        if has_s0:
            for s in range(ub):
                for h in range(nh):
                    st_ref[s, h] = s0_ref[s, h].T
        else:
            st_ref[...] = jnp.zeros_like(st_ref)

    lbl = lbl_ref[...]
    e = jnp.exp(lbl - jnp.max(lbl, axis=0, keepdims=True))
    sm = e / jnp.sum(e, axis=0, keepdims=True)
    lb = jnp.zeros((1, dim), f32)
    for r in range(1, layer + 1):
        lb = lb + sm[r:r + 1, :]

    tril = _tril_mask(c_rows)
    rr = lax.broadcasted_iota(jnp.int32, (c_rows, c_rows), 0)
    ss = lax.broadcasted_iota(jnp.int32, (c_rows, c_rows), 1)
    gshift = grp.bit_length() - 1
    diag_mask = tril & ((rr >> gshift) == (ss >> gshift))
    nw = nw_ref[...]
    rows = lax.broadcasted_iota(jnp.int32, (c_rows, 1), 0)

    for s in range(ub):
        st = [st_ref[s, h] for h in range(nh)]
        for cc in range(uc):
            r0 = (s * uc + cc) * c_rows
            valid = ((c * uc + cc) * c_rows + rows) < t_valid if t_valid < t_pad else None
            ys, st = _hgrn_chunk(p_ref[r0:r0 + c_rows, :], st, lb, nw, tril, diag_mask, valid, grp)
            for h in range(nh):
                y_ref[r0:r0 + c_rows, h * dk:(h + 1) * dk] = ys[h].astype(y_ref.dtype)
        for h in range(nh):
            st_ref[s, h] = st[h]

    @pl.when(c == nc - 1)
    def _():
        for s in range(ub):
            for h in range(nh):
                so_ref[s, h] = st_ref[s, h].T


def _units_per_step(bsz, nc, max_seqs, max_chunks):
    if nc == 1:
        return max(u for u in (1, 2, 4, 8) if u <= max_seqs and bsz % u == 0), 1
    return 1, max(u for u in (1, 2, 4, 8) if u <= max_chunks and nc % u == 0)


def hgrn_call(p, lb_logits, norm_w, s0, bsz, t_pad, t_valid, layer):
    dim = p.shape[1] // 4
    nh = HG_HEADS
    dk = dim // nh
    c_rows = min(128, t_pad)
    grp = min(HG_CHUNK_GROUP, c_rows)
    nc = t_pad // c_rows
    ub, uc = _units_per_step(bsz, nc, 8, 8)
    rows = ub * uc * c_rows
    ncs = nc // uc
    has_s0 = s0 is not None
    in_specs = [pl.BlockSpec((rows, 4 * dim), lambda b, c: (b * ncs + c, 0)),
                pl.BlockSpec(lb_logits.shape, lambda b, c: (0, 0)),
                pl.BlockSpec((1, dim), lambda b, c: (0, 0))]
    args = [p, lb_logits, norm_w.reshape(1, dim)]
    if has_s0:
        in_specs.append(pl.BlockSpec((ub, nh, dk, dk), lambda b, c: (b, 0, 0, 0)))
        args.append(s0)
    kern = functools.partial(_hgrn_kernel, c_rows=c_rows, grp=grp, t_valid=t_valid, t_pad=t_pad,
                             layer=layer, has_s0=has_s0, ub=ub, uc=uc)
    return pl.pallas_call(
        kern,
        grid=(bsz // ub, ncs),
        in_specs=in_specs,
        out_specs=[pl.BlockSpec((rows, dim), lambda b, c: (b * ncs + c, 0)),
                   pl.BlockSpec((ub, nh, dk, dk), lambda b, c: (b, 0, 0, 0))],
        out_shape=[jax.ShapeDtypeStruct((bsz * t_pad, dim), bf16),
                   jax.ShapeDtypeStruct((bsz, nh, dk, dk), f32)],
        scratch_shapes=[pltpu.VMEM((ub, nh, dk, dk), f32)],
        compiler_params=_cparams(2),
        name="hgrn",
    )(*args)


def _expand_heads(w, ex_ref):
    hi = w.astype(bf16)
    r1 = w - hi.astype(f32)
    mid = r1.astype(bf16)
    lo = (r1 - mid.astype(f32)).astype(bf16)
    return jnp.dot(jnp.concatenate([hi, mid, lo], axis=1), ex_ref[...], preferred_element_type=f32)


def _ssd_kernel(*refs, c_rows, t_valid, t_pad, has_s0, ub, uc, conv_done):
    if has_s0:
        (zx_ref, cw_ref, cb_ref, dtb_ref, alog_ref, dsk_ref, nw_ref, ex_ref, s0_ref, buf_ref,
         y_ref, so_ref, nbuf_ref, st_ref, carry_ref) = refs
    else:
        (zx_ref, cw_ref, cb_ref, dtb_ref, alog_ref, dsk_ref, nw_ref, ex_ref,
         y_ref, so_ref, st_ref, carry_ref) = refs
    c = pl.program_id(1)
    nc = pl.num_programs(1)
    hp, ns, ngr = SSD_HEAD_DIM, SSD_STATE, SSD_GROUPS
    di = y_ref.shape[1]
    nh = di // hp
    hpg = nh // ngr
    gw = ngr * ns
    kconv = cw_ref.shape[0]
    pair = 2 * hp
    assert pair == LANES and hpg % 2 == 0

    @pl.when(c == 0)
    def _():
        carry_ref[...] = jnp.zeros_like(carry_ref)
        if has_s0:
            for s in range(ub):
                st_ref[s] = s0_ref[s]
                carry_ref[s, SUBLANES - (kconv - 1):, :] = buf_ref[s]
        else:
            st_ref[...] = jnp.zeros_like(st_ref)

    cdim = cw_ref.shape[1]
    rows = lax.broadcasted_iota(jnp.int32, (c_rows, 1), 0)
    rows8 = rows[:SUBLANES]
    cw = cw_ref[...]
    a_neg = -jnp.exp(alog_ref[...])
    tril = _tril_mask(c_rows)
    dsk = dsk_ref[...]
    nw = nw_ref[...]
    lo = lax.broadcasted_iota(jnp.int32, (c_rows, LANES), 1) < hp
    for s in range(ub):
        for cc in range(uc):
            r0 = (s * uc + cc) * c_rows
            valid = ((c * uc + cc) * c_rows + rows) < t_valid if t_valid < t_pad else None
            _ssd_chunk(zx_ref[r0:r0 + c_rows, :], st_ref.at[s], carry_ref.at[s], y_ref.at[r0:r0 + c_rows, :],
                       cw, cb_ref[...], dtb_ref[...], a_neg, dsk, nw, ex_ref, tril, lo, rows8, valid, conv_done)

    @pl.when(c == nc - 1)
    def _():
        for s in range(ub):
            so_ref[s] = st_ref[s]
            if has_s0:
                last = s * c_rows + t_valid
                nbuf_ref[s] = zx_ref[last - (kconv - 1):last, di:di + cdim]


def _ssd_chunk(zx, st_ref, carry_ref, y_ref, cw, cb, dtb, a_neg, dsk, nw, ex_ref, tril, lo, rows8, valid, conv_done):
    hp, ns, ngr = SSD_HEAD_DIM, SSD_STATE, SSD_GROUPS
    c_rows = zx.shape[0]
    di = y_ref.shape[1]
    nh = di // hp
    hpg = nh // ngr
    gw = ngr * ns
    kconv, cdim = cw.shape
    pair = 2 * hp
    z = zx[:, :di]
    u = zx[:, di:di + cdim]
    if conv_done:
        xbc = u
    else:
        xbc = _silu(_causal_conv(u, carry_ref[...], cw, cb, rows8))
        carry_ref[...] = u[c_rows - SUBLANES:, :]
    dt = jax.nn.softplus(zx[:, di + cdim:] + dtb)
    if valid is not None:
        xbc = jnp.where(valid, xbc, 0.0)
        dt = jnp.where(valid, dt, 0.0)
    xs = xbc[:, :di]
    bm = xbc[:, di:di + gw]
    cm = xbc[:, di + gw:di + 2 * gw]
    acum = _cumsum_rows(dt * a_neg, tril.astype(bf16))

    def head_rows(a):
        if c_rows < LANES:
            a = jnp.concatenate([a, jnp.zeros((LANES - c_rows, LANES), f32)], axis=0)
        return a.T[:, :c_rows]

    acum_t = head_rows(acum)
    dt_t = head_rows(dt)
    a_last = acum[c_rows - 1:c_rows, :]
    xd = xs * _expand_heads(dt * jnp.exp(a_last - acum), ex_ref)
    first = lax.broadcasted_iota(jnp.int32, (pair, ns), 0) < hp

    ys = []
    for g in range(ngr):
        bg = bm[:, g * ns:(g + 1) * ns]
        cg = cm[:, g * ns:(g + 1) * ns]
        cb_mat = _dot_nt(cg, bg)
        gsl = slice(g * hpg * hp, (g + 1) * hpg * hp)
        y_inter = _dot_nt(cg, st_ref[gsl, :])
        decs = []
        for q in range(hpg // 2):
            h1 = g * hpg + 2 * q
            psl = slice(h1 * hp, h1 * hp + pair)
            ms, es = [], []
            for h in (h1, h1 + 1):
                colb = jnp.broadcast_to(acum[:, h:h + 1], (c_rows, LANES))
                seg = colb[:, :c_rows] - acum_t[h:h + 1, :]
                ms.append(cb_mat * jnp.exp(jnp.where(tril, seg, -jnp.inf)) * dt_t[h:h + 1, :])
                es.append(jnp.exp(colb))
            xp = xs[:, psl]
            x_lo = jnp.where(lo, xp, 0.0)
            x_hi = jnp.where(lo, 0.0, xp)
            if (2 * c_rows) % LANES == 0:
                y_intra = _dot(jnp.concatenate(ms, axis=1), jnp.concatenate([x_lo, x_hi], axis=0))
            else:
                y_intra = _dot(ms[0], x_lo) + _dot(ms[1], x_hi)
            y_pair = y_intra + y_inter[:, q * pair:(q + 1) * pair] * jnp.where(lo, es[0], es[1])
            ys.append(y_pair + dsk[:, psl] * xp)
            decs.append(jnp.where(first, jnp.exp(a_last[:, h1:h1 + 1]), jnp.exp(a_last[:, h1 + 1:h1 + 2])))
        st_ref[gsl, :] = st_ref[gsl, :] * jnp.concatenate(decs, axis=0) + _dot_tn(xd[:, gsl], bg)

    y = jnp.concatenate(ys, axis=1) * _silu(z)
    gdim = di // ngr
    for g in range(ngr):
        sl = slice(g * gdim, (g + 1) * gdim)
        yg = y[:, sl]
        y_ref[:, sl] = (yg * lax.rsqrt(jnp.mean(yg * yg, axis=-1, keepdims=True) + RMS_EPS) * nw[:, sl]).astype(y_ref.dtype)


def ssd_call(zx, conv_w, conv_b, dt_bias, a_log, d_skip, norm_w, s0, conv_buf, bsz, t_pad, t_valid, conv_done):
    hp, ns = SSD_HEAD_DIM, SSD_STATE
    conv_dim = conv_w.shape[1]
    di = zx.shape[1] - conv_dim - LANES
    nh = di // hp
    c_rows = min(128, t_pad)
    nc = t_pad // c_rows
    ub, uc = _units_per_step(bsz, nc, 4, 4)
    rows = ub * uc * c_rows
    ncs = nc // uc
    has_s0 = s0 is not None
    assert not (conv_done and has_s0)
    pad_h = lambda a: jnp.pad(a.reshape(1, nh), ((0, 0), (0, LANES - nh)))
    full = lambda shape: pl.BlockSpec(shape, lambda b, c: (0,) * len(shape))
    sel = (jnp.arange(di)[None, :] // hp == jnp.arange(LANES)[:, None]).astype(bf16)
    expand = jnp.concatenate([sel, sel, sel], axis=0)
    in_specs = [pl.BlockSpec((rows, zx.shape[1]), lambda b, c: (b * ncs + c, 0)),
                full(conv_w.shape), full((1, conv_dim)), full((1, LANES)), full((1, LANES)),
                full((1, di)), full((1, di)), full(expand.shape)]
    args = [zx, conv_w, conv_b.reshape(1, conv_dim), pad_h(dt_bias), pad_h(a_log),
            jnp.repeat(d_skip, hp).reshape(1, di), norm_w.reshape(1, di), expand]
    out_specs = [pl.BlockSpec((rows, di), lambda b, c: (b * ncs + c, 0)),
                 pl.BlockSpec((ub, di, ns), lambda b, c: (b, 0, 0))]
    out_shape = [jax.ShapeDtypeStruct((bsz * t_pad, di), bf16), jax.ShapeDtypeStruct((bsz, di, ns), f32)]
    if has_s0:
        assert nc == 1 and t_valid >= conv_w.shape[0] - 1
        in_specs += [pl.BlockSpec((ub, di, ns), lambda b, c: (b, 0, 0)),
                     pl.BlockSpec((ub,) + conv_buf.shape[1:], lambda b, c: (b, 0, 0))]
        args += [s0.reshape(bsz, di, ns), conv_buf]
        out_specs.append(pl.BlockSpec((ub,) + conv_buf.shape[1:], lambda b, c: (b, 0, 0)))
        out_shape.append(jax.ShapeDtypeStruct(conv_buf.shape, f32))
    kern = functools.partial(_ssd_kernel, c_rows=c_rows, t_valid=t_valid, t_pad=t_pad, has_s0=has_s0, ub=ub, uc=uc,
                             conv_done=conv_done)
    outs = pl.pallas_call(
        kern,
        grid=(bsz // ub, ncs),
        in_specs=in_specs,
        out_specs=out_specs,
        out_shape=out_shape,
        scratch_shapes=[pltpu.VMEM((ub, di, ns), f32), pltpu.VMEM((ub, SUBLANES, conv_dim), f32)],
        compiler_params=_cparams(2),
        name="ssd",
    )(*args)
    return outs[0], outs[1].reshape(bsz, nh, hp, ns), (outs[2] if has_s0 else None)


def _trunk(x, c_k, c_v, s_hg, s_ssm, s_sconv, s_fconv, prm):
    bsz, t, d = x.shape
    depth = prm["norm_mix_w"].shape[0]
    decode = c_k is not None
    tp = -(-t // SUBLANES) * SUBLANES
    if tp != t:
        x = jnp.pad(x, ((0, 0), (0, tp - t), (0, 0)))
    m = bsz * tp
    n_heads = d // ATT_HEAD_DIM
    qd = n_heads * ATT_HEAD_DIM
    kvd = ATT_KV_HEADS * ATT_HEAD_DIM
    x2 = x.reshape(m, d)
    xn, xn_w = x2, prm["norm_mix_w"][0]
    nk, nv, nhg, nssm, nsconv, nfconv = [], [], [], [], [], []
    if decode:
        ck_all = jnp.transpose(c_k, (0, 1, 3, 4, 2)).reshape(c_k.shape[0], bsz, kvd, WINDOW)
        cv_all = jnp.transpose(c_v, (0, 1, 3, 4, 2)).reshape(c_v.shape[0], bsz, kvd, WINDOW)
    slot = [0] * N_MIXERS
    for i in range(depth):
        kind = i % N_MIXERS
        j = slot[kind]
        slot[kind] += 1
        nw_ffn = prm["norm_ffn_w"][i]
        if kind == 0:
            w_qkv, b_qkv = prm["attn_w_qkv"], prm["attn_b_qkv"][j]
            nq = w_qkv.shape[2]
            if decode:
                qkv = proj_call(xn, xn_w, w_qkv, j, b_qkv, f32, "attn_qkv").reshape(bsz, tp, nq)
                o, k_new, v_new = attn_cache_call(qkv, ck_all, cv_all, j, prm["attn_sinks"][j], n_heads, t)
                o = o.reshape(m, qd)
            else:
                qkv = proj_call(xn, xn_w, w_qkv, j, b_qkv, bf16, "attn_qkv")
                o = attn_prompt_call(qkv, prm["attn_sinks"][j], bsz, t, n_heads)
                xn_last = xn.reshape(bsz, t, d)[:, t - WINDOW:].reshape(bsz * WINDOW, d)
                kv_last = proj_call(xn_last, xn_w, w_qkv[:, :, qd:], j, b_qkv[qd:], f32,
                                    "attn_kv_tail").reshape(bsz, WINDOW, 2, kvd)
                to4 = lambda a: a.reshape(bsz, WINDOW, ATT_KV_HEADS, ATT_HEAD_DIM)
                k_new, v_new = to4(kv_last[:, :, 0]), to4(kv_last[:, :, 1])
            nk.append(k_new)
            nv.append(v_new)
            act, w_o, b_o = o, (prm["attn_w_o"], j), prm["attn_b_o"][j]
        elif kind == 1:
            p = proj_call(xn, xn_w, prm["hgrn_w_in"], j, None, f32, "hgrn_in")
            y, s_new = hgrn_call(p, prm["hgrn_lb_logits"], prm["hgrn_norm_w"][j],
                                 None if s_hg is None else s_hg[j], bsz, tp, t, i)
            nhg.append(s_new)
            act, w_o, b_o = y, (prm["hgrn_w_o"], j), None
        else:
            conv_dim = prm["ssd_conv_w"].shape[2]
            di = prm["ssd_w_o"].shape[1]
            kc = prm["ssd_conv_w"].shape[1]
            if s_sconv is None:
                zx, pre = ssd_in_call(xn, prm["ssd_w_in"], j, prm["ssd_conv_w"][j], prm["ssd_conv_b"][j], tp)
                pre = pre.reshape(bsz, tp // min(PROJ_TM, m), SUBLANES, conv_dim)[:, -1, SUBLANES - (kc - 1):]
            else:
                zx = proj_call(xn, xn_w, prm["ssd_w_in"], j, None, f32, "ssd_in")
            y, s_new, conv_new = ssd_call(zx, prm["ssd_conv_w"][j], prm["ssd_conv_b"][j], prm["ssd_dt_bias"][j],
                                prm["ssd_a_log"][j], prm["ssd_d"][j], prm["ssd_norm_w"][j],
                                None if s_ssm is None else s_ssm[j],
                                None if s_sconv is None else s_sconv[j], bsz, tp, t, s_sconv is None)
            nssm.append(s_new)
            nsconv.append(pre if s_sconv is None else conv_new)
            act, w_o, b_o = y, (prm["ssd_w_o"], j), None
        last = i == depth - 1
        next_w = prm["norm_final_w"] if last else prm["norm_mix_w"][i + 1]
        dff = prm["ffn_w_down"].shape[1]
        kf = prm["ffn_conv_w"].shape[1]
        x2, xn, tail = ffn_call(act, w_o, b_o, x2, nw_ffn, (prm["ffn_w_up"], i), prm["ffn_conv_w"][i],
                                prm["ffn_conv_b"][i], (prm["ffn_w_down"], i), next_w, tp,
                                None if s_fconv is None else (s_fconv, i), t, f32 if last else bf16, "ffn")
        xn_w = None
        if s_fconv is not None:
            nfconv.append(tail)
        else:
            tiles_per_seq = tp // tail_tile_rows(m)
            tl = tail.reshape(bsz, tiles_per_seq, SUBLANES, dff)[:, -1]
            nfconv.append(tl[:, SUBLANES - (kf - 1):])
    stack = lambda xs: xs[0][None] if len(xs) == 1 else jnp.stack(xs)
    y = xn.reshape(bsz, tp, d)[:, :t]
    nk, nv = stack(nk), stack(nv)
    if decode:
        unview = lambda a: jnp.transpose(a.reshape(a.shape[0], bsz, ATT_KV_HEADS, ATT_HEAD_DIM, WINDOW), (0, 1, 4, 2, 3))
        nk, nv = unview(nk), unview(nv)
    return (y, nk, nv, stack(nhg), stack(nssm), stack(nsconv), stack(nfconv))


def tail_tile_rows(m):
    return min(FFN_TM, m)


def kernel(x_prompt, x_sample, cache_attn_k, cache_attn_v, state_hgrn, state_ssm, state_ssm_conv, state_ffn_conv, norm_mix_w, norm_ffn_w, norm_final_w, attn_w_qkv, attn_b_qkv, attn_sinks, attn_w_o, attn_b_o, hgrn_w_in, hgrn_lb_logits, hgrn_norm_w, hgrn_w_o, ssd_w_in, ssd_conv_w, ssd_conv_b, ssd_dt_bias, ssd_a_log, ssd_d, ssd_norm_w, ssd_w_o, ffn_w_up, ffn_conv_w, ffn_conv_b, ffn_w_down):
    cast = lambda w: w.astype(bf16)
    ssd_in_pad = (-ssd_w_in.shape[2]) % LANES
    prm = dict(norm_mix_w=norm_mix_w, norm_ffn_w=norm_ffn_w, norm_final_w=norm_final_w,
               attn_w_qkv=cast(attn_w_qkv), attn_b_qkv=attn_b_qkv, attn_sinks=attn_sinks,
               attn_w_o=cast(attn_w_o), attn_b_o=attn_b_o,
               hgrn_w_in=cast(hgrn_w_in), hgrn_lb_logits=hgrn_lb_logits, hgrn_norm_w=hgrn_norm_w,
               hgrn_w_o=cast(hgrn_w_o),
               ssd_w_in=jnp.pad(cast(ssd_w_in), ((0, 0), (0, 0), (0, ssd_in_pad))), ssd_conv_w=ssd_conv_w,
               ssd_conv_b=ssd_conv_b, ssd_dt_bias=ssd_dt_bias,
               ssd_a_log=ssd_a_log, ssd_d=ssd_d, ssd_norm_w=ssd_norm_w, ssd_w_o=cast(ssd_w_o),
               ffn_w_up=cast(ffn_w_up), ffn_conv_w=ffn_conv_w, ffn_conv_b=ffn_conv_b, ffn_w_down=cast(ffn_w_down))
    outs_p = _trunk(x_prompt, None, None, None, None, None, None, prm)
    outs_s = _trunk(x_sample, cache_attn_k, cache_attn_v, state_hgrn, state_ssm, state_ssm_conv,
                    state_ffn_conv, prm)
    return (outs_p[0], outs_s[0]) + outs_p[1:] + outs_s[1:]
```

```python
import functools

import jax
import jax.numpy as jnp
from jax import lax
from jax.experimental import pallas as pl
from jax.experimental.pallas import tpu as pltpu

bf16 = jnp.bfloat16
f32 = jnp.float32

RMS_EPS = 1e-6
N_MIXERS = 3
WINDOW = 128
PAST_LEN = 8192
ATT_HEAD_DIM = 64
ATT_KV_HEADS = 4
HG_HEADS = 8
HG_CHUNK_GROUP = 32
SSD_HEAD_DIM = 64
SSD_STATE = 128
SSD_GROUPS = 4

LANES = 128
SUBLANES = 8
VMEM_LIMIT = 56 * 1024 * 1024
PROJ_TM = 512
FFN_TM = 512


def _cparams(n_axes):
    return pltpu.CompilerParams(dimension_semantics=("arbitrary",) * n_axes,
                                vmem_limit_bytes=VMEM_LIMIT)


def _dot(a, b):
    return jnp.dot(a.astype(bf16), b.astype(bf16), preferred_element_type=f32)


def _dot_nt(a, b):
    return lax.dot_general(a.astype(bf16), b.astype(bf16), (((1,), (1,)), ((), ())),
                           preferred_element_type=f32)


def _dot_tn(a, b):
    return lax.dot_general(a.astype(bf16), b.astype(bf16), (((0,), (0,)), ((), ())),
                           preferred_element_type=f32)


def _silu(x):
    return x * jax.nn.sigmoid(x)


def _rms(x, w):
    ms = jnp.mean(x * x, axis=-1, keepdims=True)
    return x * lax.rsqrt(ms + RMS_EPS) * w


def _cumsum_rows(x, tril):
    hi = x.astype(bf16)
    r1 = x - hi.astype(f32)
    mid = r1.astype(bf16)
    lo = (r1 - mid.astype(f32)).astype(bf16)
    d = functools.partial(jnp.dot, preferred_element_type=f32)
    return d(tril, hi) + d(tril, mid) + d(tril, lo)


def _tril_mask(c):
    r = lax.broadcasted_iota(jnp.int32, (c, c), 0)
    s = lax.broadcasted_iota(jnp.int32, (c, c), 1)
    return r >= s


def _resident(shape):
    return pl.BlockSpec(shape, lambda *_: (0,) * len(shape), pipeline_mode=pl.Buffered(1))


def _resident_layer(w, layer):
    return pl.BlockSpec((None,) + w.shape[1:], lambda *_: (layer, 0, 0), pipeline_mode=pl.Buffered(1))


def _proj_kernel(x_ref, nw_ref, w_ref, b_ref, o_ref):
    x = x_ref[...]
    if x.dtype != bf16:
        x = _rms(x, nw_ref[...]).astype(bf16)
    acc = jnp.dot(x, w_ref[...], preferred_element_type=f32)
    o_ref[...] = (acc + b_ref[...]).astype(o_ref.dtype)


def proj_call(x, norm_w, w, layer, bias, out_dtype, name):
    m, k = x.shape
    n = w.shape[2]
    tm = min(PROJ_TM, m)
    if bias is None:
        bias = jnp.zeros((n,), f32)
    if norm_w is None:
        assert x.dtype == bf16
        norm_w = jnp.ones((k,), f32)
    return pl.pallas_call(
        _proj_kernel,
        grid=(m // tm,),
        in_specs=[pl.BlockSpec((tm, k), lambda i: (i, 0)), _resident((1, k)), _resident_layer(w, layer),
                  _resident((1, n))],
        out_specs=pl.BlockSpec((tm, n), lambda i: (i, 0)),
        out_shape=jax.ShapeDtypeStruct((m, n), out_dtype),
        compiler_params=_cparams(1),
        name=name,
    )(x, norm_w.reshape(1, k), w, bias.reshape(1, n))


def _causal_conv(u, prev, cw, cb, rows8):
    kconv = cw.shape[0]
    yc = cb
    for kk in range(kconv - 1):
        back = kconv - 1 - kk
        sh = pltpu.roll(u, back, 0)
        top = sh[:SUBLANES]
        for r0 in range(back):
            top = jnp.where(rows8 == r0, prev[SUBLANES - back + r0:SUBLANES - back + r0 + 1, :], top)
        sh = jnp.concatenate([top, sh[SUBLANES:]], axis=0) if u.shape[0] > SUBLANES else top
        yc = yc + sh * cw[kk:kk + 1, :]
    return yc + u * cw[kconv - 1:kconv, :]


def _ssd_in_kernel(x_ref, w_ref, cw_ref, cb_ref, o_ref, tail_ref, carry_ref, *, tm, tc, t_seq, di):
    i = pl.program_id(0)
    cdim = cw_ref.shape[1]

    @pl.when(((i * tm) & (t_seq - 1)) == 0)
    def _():
        carry_ref[...] = jnp.zeros_like(carry_ref)

    x = x_ref[...]
    rows8 = lax.broadcasted_iota(jnp.int32, (SUBLANES, 1), 0)
    nz = di // tc
    for idx, c0 in enumerate(range(0, cdim, tc)):
        cs = slice(c0, c0 + tc)
        u = jnp.dot(x, w_ref[:, di + c0:di + c0 + tc], preferred_element_type=f32)
        tail_ref[0, :, cs] = u[tm - SUBLANES:, :]
        yc = _causal_conv(u, carry_ref[:, cs], cw_ref[:, cs], cb_ref[:, cs], rows8)
        carry_ref[:, cs] = u[tm - SUBLANES:, :]
        o_ref[:, di + c0:di + c0 + tc] = _silu(yc)
        if idx < nz:
            o_ref[:, idx * tc:(idx + 1) * tc] = jnp.dot(x, w_ref[:, idx * tc:(idx + 1) * tc],
                                                        preferred_element_type=f32)
    o_ref[:, di + cdim:] = jnp.dot(x, w_ref[:, di + cdim:], preferred_element_type=f32)


def ssd_in_call(xn, w, layer, conv_w, conv_b, t_seq):
    m, k = xn.shape
    n = w.shape[2]
    cdim = conv_w.shape[1]
    di = n - cdim - LANES
    tm = min(PROJ_TM, m)
    tc = 2 * LANES
    assert t_seq % tm == 0 and cdim % tc == 0 and di % tc == 0 and di <= cdim and (t_seq & (t_seq - 1)) == 0
    kern = functools.partial(_ssd_in_kernel, tm=tm, tc=tc, t_seq=t_seq, di=di)
    return pl.pallas_call(
        kern,
        grid=(m // tm,),
        in_specs=[pl.BlockSpec((tm, k), lambda i: (i, 0)), _resident_layer(w, layer),
                  _resident(conv_w.shape), _resident((1, cdim))],
        out_specs=[pl.BlockSpec((tm, n), lambda i: (i, 0)), pl.BlockSpec((1, SUBLANES, cdim), lambda i: (i, 0, 0))],
        out_shape=[jax.ShapeDtypeStruct((m, n), f32), jax.ShapeDtypeStruct((m // tm, SUBLANES, cdim), f32)],
        scratch_shapes=[pltpu.VMEM((SUBLANES, cdim), f32)],
        compiler_params=_cparams(1),
        name="ssd_in_conv",
    )(xn, w, conv_w, conv_b.reshape(1, cdim))


def _ffn_kernel(*refs, tm, tf, t_seq, t_valid, has_state):
    if has_state:
        (act_ref, wo_ref, bo_ref, x_ref, nw1_ref, wu_ref, cw_ref, cb_ref, wd_ref, nw2_ref, buf_ref,
         xo_ref, xn2_ref, nst_ref, carry_ref) = refs
    else:
        (act_ref, wo_ref, bo_ref, x_ref, nw1_ref, wu_ref, cw_ref, cb_ref, wd_ref, nw2_ref,
         xo_ref, xn2_ref, tail_ref, carry_ref) = refs
    i = pl.program_id(0)
    dff = wd_ref.shape[0]

    x1 = jnp.dot(act_ref[...].astype(bf16), wo_ref[...], preferred_element_type=f32) + bo_ref[...] + x_ref[...]
    xn = _rms(x1, nw1_ref[...]).astype(bf16)
    rows8 = lax.broadcasted_iota(jnp.int32, (SUBLANES, 1), 0)
    if not has_state:
        @pl.when(((i * tm) & (t_seq - 1)) == 0)
        def _():
            carry_ref[...] = jnp.zeros_like(carry_ref)
    gs = []
    for c0 in range(0, dff, tf):
        cs = slice(c0, c0 + tf)
        a = jnp.dot(xn, wu_ref[:, cs], preferred_element_type=f32)
        b = jnp.dot(xn, wu_ref[:, dff + c0:dff + c0 + tf], preferred_element_type=f32)
        r1 = pltpu.roll(a, 1, 0)
        r2 = pltpu.roll(a, 2, 0)
        if has_state:
            p1s, p2s = [], []
            for s in range(tm // SUBLANES):
                rs = slice(s * SUBLANES, (s + 1) * SUBLANES)
                b0, b1 = buf_ref[s, 0:1, cs], buf_ref[s, 1:2, cs]
                p1s.append(jnp.where(rows8 == 0, b1, r1[rs, :]))
                p2s.append(jnp.where(rows8 == 0, b0, jnp.where(rows8 == 1, b1, r2[rs, :])))
                nst_ref[s, :, cs] = a[s * SUBLANES + t_valid - 2:s * SUBLANES + t_valid, :]
            p1 = jnp.concatenate(p1s, axis=0)
            p2 = jnp.concatenate(p2s, axis=0)
        else:
            tail_ref[0, :, cs] = a[tm - SUBLANES:, :]
            carry = carry_ref[:, cs]
            top1 = jnp.where(rows8 == 0, carry[7:8, :], r1[:SUBLANES])
            top2 = jnp.where(rows8 == 0, carry[6:7, :], jnp.where(rows8 == 1, carry[7:8, :], r2[:SUBLANES]))
            p1 = jnp.concatenate([top1, r1[SUBLANES:]], axis=0)
            p2 = jnp.concatenate([top2, r2[SUBLANES:]], axis=0)
            carry_ref[:, cs] = a[tm - SUBLANES:, :]
        y = cb_ref[:, cs] + p2 * cw_ref[0:1, cs]
        y = y + p1 * cw_ref[1:2, cs]
        y = y + a * cw_ref[2:3, cs]
        gs.append((_silu(y) * b).astype(bf16))
    g = jnp.concatenate(gs, axis=1)
    xo = jnp.dot(g, wd_ref[...], preferred_element_type=f32) + x1
    xo_ref[...] = xo
    xn2_ref[...] = _rms(xo, nw2_ref[...]).astype(xn2_ref.dtype)


def ffn_call(act, w_o, b_o, x, ffn_norm_w, w_up, conv_w, conv_b, w_down, next_w, t_seq, buf, t_valid, out_dtype, name):
    m, d = x.shape
    ko = act.shape[1]
    dff = w_down[0].shape[1]
    tf = 2 * LANES
    assert dff % tf == 0 and (t_seq & (t_seq - 1)) == 0
    has_state = buf is not None
    tm = min(FFN_TM // 2 if has_state else FFN_TM, m)
    assert (t_seq == SUBLANES and t_valid >= 2 and conv_w.shape[0] == 3) if has_state else (t_seq % tm == 0)
    if b_o is None:
        b_o = jnp.zeros((d,), f32)
    row = lambda w: pl.BlockSpec((tm, w), lambda i: (i, 0))
    in_specs = [row(ko), _resident_layer(*w_o), _resident((1, d)), row(d), _resident((1, d)),
                _resident_layer(*w_up), _resident((3, dff)), _resident((1, dff)),
                _resident_layer(*w_down), _resident((1, d))]
    args = [act, w_o[0], b_o.reshape(1, d), x, ffn_norm_w.reshape(1, d),
            w_up[0], conv_w, conv_b.reshape(1, dff), w_down[0], next_w.reshape(1, d)]
    if has_state:
        nseq = tm // t_seq
        in_specs.append(pl.BlockSpec((None, nseq, 2, dff), lambda i: (buf[1], i, 0, 0)))
        args.append(buf[0])
        third_spec = pl.BlockSpec((nseq, 2, dff), lambda i: (i, 0, 0))
        third_shape = jax.ShapeDtypeStruct((m // t_seq, 2, dff), f32)
    else:
        third_spec = pl.BlockSpec((1, SUBLANES, dff), lambda i: (i, 0, 0))
        third_shape = jax.ShapeDtypeStruct((m // tm, SUBLANES, dff), f32)
    kern = functools.partial(_ffn_kernel, tm=tm, tf=tf, t_seq=t_seq, t_valid=t_valid, has_state=has_state)
    return pl.pallas_call(
        kern,
        grid=(m // tm,),
        in_specs=in_specs,
        out_specs=[row(d), row(d), third_spec],
        out_shape=[jax.ShapeDtypeStruct((m, d), f32), jax.ShapeDtypeStruct((m, d), out_dtype), third_shape],
        scratch_shapes=[pltpu.VMEM((SUBLANES, dff), f32)],
        compiler_params=_cparams(1),
        name=name,
    )(*args)


def _attn_softmax_pv(s, sink_col, vj):
    m = jnp.maximum(jnp.max(s, axis=-1, keepdims=True), sink_col)
    p = jnp.exp(s - m)
    l = jnp.sum(p, axis=-1, keepdims=True) + jnp.exp(sink_col - m)
    return _dot(p, vj) / l


def _sink_col(sinks_ref, j, group, rows_per_head):
    r = lax.broadcasted_iota(jnp.int32, (group * rows_per_head, 1), 0)
    col = jnp.full((group * rows_per_head, 1), sinks_ref[j * group], f32)
    for p in range(1, group):
        col = jnp.where(r >= p * rows_per_head, sinks_ref[j * group + p], col)
    return col


def _attn_prompt_kernel(sinks_ref, q_ref, kvc_ref, kvp_ref, o_ref, *, n_heads):
    n = pl.program_id(1)
    w, hd, kvh = WINDOW, ATT_HEAD_DIM, ATT_KV_HEADS
    group = n_heads // kvh
    kvw = kvh * hd
    assert group == kvh
    kv = jnp.concatenate([kvp_ref[...], kvc_ref[...]], axis=0)
    k2 = kv[:, :kvw]
    v2 = kv[:, kvw:]
    i = lax.broadcasted_iota(jnp.int32, (w, 2 * w), 0)
    c = lax.broadcasted_iota(jnp.int32, (w, 2 * w), 1)
    band = (c > i) & (c <= i + w) & ((n > 0) | (c >= w))
    mask = jnp.concatenate([band] * group, axis=0)
    slot = lax.broadcasted_iota(jnp.int32, (w, kvw), 1) >> (hd.bit_length() - 1)
    for j in range(kvh):
        qg = q_ref[:, j * kvw:(j + 1) * kvw] * (hd ** -0.5)
        qs = jnp.concatenate([jnp.where(slot == j, pltpu.roll(qg, ((j - p) % kvh) * hd, 1), 0)
                              for p in range(group)], axis=0)
        s = jnp.where(mask, _dot_nt(qs, k2), -jnp.inf)
        of = _attn_softmax_pv(s, _sink_col(sinks_ref, j, group, w), v2).astype(o_ref.dtype)
        og = of[0:w, :]
        for p in range(group):
            piece = of[p * w:(p + 1) * w, :]
            og = jnp.where(slot == p, pltpu.roll(piece, ((p - j) % kvh) * hd, 1), og)
        o_ref[:, j * kvw:(j + 1) * kvw] = og


def attn_prompt_call(qkv, sinks, bsz, t, n_heads):
    w, hd, kvh = WINDOW, ATT_HEAD_DIM, ATT_KV_HEADS
    nb = t // w
    qd = n_heads * hd
    kvd = 2 * kvh * hd
    assert qd % kvd == 0
    kvblk = qd // kvd
    kern = functools.partial(_attn_prompt_kernel, n_heads=n_heads)
    return pl.pallas_call(
        kern,
        grid=(bsz, nb),
        in_specs=[pl.BlockSpec(memory_space=pltpu.SMEM),
                  pl.BlockSpec((w, qd), lambda b, n: (b * nb + n, 0)),
                  pl.BlockSpec((w, kvd), lambda b, n: (b * nb + n, kvblk)),
                  pl.BlockSpec((w, kvd), lambda b, n: (b * nb + jnp.maximum(n - 1, 0), kvblk))],
        out_specs=pl.BlockSpec((w, qd), lambda b, n: (b * nb + n, 0)),
        out_shape=jax.ShapeDtypeStruct((bsz * t, qd), bf16),
        compiler_params=_cparams(2),
        name="attn_prompt",
    )(sinks, qkv, qkv, qkv)


def _shifted_window(cache, new_rows, t_valid):
    tp, w = new_rows.shape[0], cache.shape[1]
    hi = new_rows.astype(bf16)
    r1 = new_rows - hi.astype(f32)
    mid = r1.astype(bf16)
    lo = (r1 - mid.astype(f32)).astype(bf16)
    t = lax.broadcasted_iota(jnp.int32, (tp, w), 0)
    col = lax.broadcasted_iota(jnp.int32, (tp, w), 1)
    place = ((col == w - t_valid + t) & (t < t_valid)).astype(bf16)
    placed = _dot_tn(jnp.concatenate([hi, mid, lo], axis=0), jnp.concatenate([place] * 3, axis=0))
    lane = lax.broadcasted_iota(jnp.int32, cache.shape, 1)
    return jnp.where(lane >= w - t_valid, placed, pltpu.roll(cache, w - t_valid, 1))


def _attn_cache_kernel(sinks_ref, q_ref, ck_ref, cv_ref, o_ref, nk_ref, nv_ref, *, n_heads, t_valid, bt):
    w, hd, kvh = WINDOW, ATT_HEAD_DIM, ATT_KV_HEADS
    group = n_heads // kvh
    tp = q_ref.shape[1]
    qd = n_heads * hd
    tq = lax.broadcasted_iota(jnp.int32, (tp, w), 0)
    jc = lax.broadcasted_iota(jnp.int32, (tp, w), 1)
    mc = (jc > tq) & (jc - w + PAST_LEN >= 0)
    tq2 = lax.broadcasted_iota(jnp.int32, (tp, tp), 0)
    un = lax.broadcasted_iota(jnp.int32, (tp, tp), 1)
    mn = (un <= tq2) & (un < t_valid)
    mask_c = jnp.concatenate([mc] * n_heads, axis=0)
    mask_n = jnp.concatenate([mn] * n_heads, axis=0)
    kvw = kvh * hd
    assert group == kvh
    slot = lax.broadcasted_iota(jnp.int32, (tp, kvw), 1) >> (hd.bit_length() - 1)
    sink = _sink_col(sinks_ref, 0, n_heads, tp)

    for bi in range(bt):
        row = q_ref[bi]
        ck = ck_ref[bi]
        cv = cv_ref[bi]
        kn = row[:, qd:qd + kvw]
        vn = row[:, qd + kvw:]
        win = _shifted_window(jnp.concatenate([ck, cv], axis=0), row[:, qd:], t_valid)
        nk_ref[bi] = win[:kvw]
        nv_ref[bi] = win[kvw:]
        pieces = []
        for j in range(kvh):
            qg = row[:, j * kvw:(j + 1) * kvw] * (hd ** -0.5)
            for p in range(group):
                pieces.append(jnp.where(slot == j, pltpu.roll(qg, ((j - p) % kvh) * hd, 1), 0.0))
        qs = jnp.concatenate(pieces, axis=0)
        sc = jnp.where(mask_c, _dot(qs, ck), -jnp.inf)
        sn = jnp.where(mask_n, _dot_nt(qs, kn), -jnp.inf)
        m = jnp.maximum(jnp.maximum(jnp.max(sc, axis=-1, keepdims=True),
                                    jnp.max(sn, axis=-1, keepdims=True)), sink)
        pc = jnp.exp(sc - m)
        pn = jnp.exp(sn - m)
        l = jnp.sum(pc, axis=-1, keepdims=True) + jnp.sum(pn, axis=-1, keepdims=True) + jnp.exp(sink - m)
        of = (_dot_nt(pc, cv) + _dot(pn, vn)) / l
        for j in range(kvh):
            og = jnp.zeros((tp, kvw), f32)
            for p in range(group):
                h = j * group + p
                og = og + jnp.where(slot == p, pltpu.roll(of[h * tp:(h + 1) * tp, :], ((p - j) % kvh) * hd, 1), 0.0)
            o_ref[bi, :, j * kvw:(j + 1) * kvw] = og


def attn_cache_call(qkv, cache_k, cache_v, layer, sinks, n_heads, t_valid):
    bsz, tp, nq = qkv.shape
    w = WINDOW
    kd = cache_k.shape[2]
    qd = n_heads * ATT_HEAD_DIM
    bt = 8
    kern = functools.partial(_attn_cache_kernel, n_heads=n_heads, t_valid=t_valid, bt=bt)
    return pl.pallas_call(
        kern,
        grid=(bsz // bt,),
        in_specs=[pl.BlockSpec(memory_space=pltpu.SMEM),
                  pl.BlockSpec((bt, tp, nq), lambda b: (b, 0, 0)),
                  pl.BlockSpec((None, bt, kd, w), lambda b: (layer, b, 0, 0)),
                  pl.BlockSpec((None, bt, kd, w), lambda b: (layer, b, 0, 0))],
        out_specs=[pl.BlockSpec((bt, tp, qd), lambda b: (b, 0, 0)),
                   pl.BlockSpec((bt, kd, w), lambda b: (b, 0, 0)),
                   pl.BlockSpec((bt, kd, w), lambda b: (b, 0, 0))],
        out_shape=[jax.ShapeDtypeStruct((bsz, tp, qd), f32),
                   jax.ShapeDtypeStruct((bsz, kd, w), f32),
                   jax.ShapeDtypeStruct((bsz, kd, w), f32)],
        compiler_params=_cparams(1),
        name="attn_cache",
    )(sinks, qkv, cache_k, cache_v)


def _hgrn_chunk(x, st, lb, nw, tril, diag_mask, valid, grp):
    nh = HG_HEADS
    c_rows = x.shape[0]
    dim = x.shape[1] // 4
    dk = dim // nh
    ng = c_rows // grp
    if valid is not None:
        x = jnp.where(valid, x, 0.0)
    q = _silu(x[:, 0:dim])
    fg = lb + (1.0 - lb) * jax.nn.sigmoid(x[:, dim:2 * dim])
    logf = jnp.log(fg)
    k = 1.0 - fg
    v = x[:, 2 * dim:3 * dim]
    gate = x[:, 3 * dim:4 * dim]
    if valid is not None:
        logf = jnp.where(valid, logf, 0.0)
        k = jnp.where(valid, k, 0.0)
    bcum = _cumsum_rows(logf, tril.astype(bf16))

    def rows_of(vals):
        return jnp.concatenate([jnp.broadcast_to(r, (grp, dk)) for r in vals], axis=0)

    ys, st_out = [], []
    for h in range(nh):
        sl = slice(h * dk, (h + 1) * dk)
        b = bcum[:, sl]
        qh, kh, vh = q[:, sl], k[:, sl], v[:, sl]
        zero = jnp.zeros((1, dk), f32)
        r = [zero] + [b[i * grp - 1:i * grp, :] for i in range(1, ng)] + [b[c_rows - 1:c_rows, :]]
        mid = [b[i * grp + grp // 2 - 1:i * grp + grp // 2, :] for i in range(ng)]
        b_last = r[ng]
        r_start = rows_of(r[:ng])
        r_end = rows_of(r[1:])
        r_mid = rows_of(mid)
        qg = qh * jnp.exp(b - r_start)
        kt = kh * jnp.exp(r_end - b)
        qm = qh * jnp.exp(b - r_mid)
        km = kh * jnp.exp(r_mid - b)
        a = jnp.where(diag_mask, _dot_nt(qm, km), 0.0)
        if ng > 1:
            lhs, rhs = [], []
            for j in range(ng - 1):
                lhs.append(jnp.concatenate(
                    [jnp.zeros((grp, dk), f32) if i <= j else qg[i * grp:(i + 1) * grp, :] * jnp.exp(r[i] - r[j + 1])
                     for i in range(ng)], axis=0))
                rhs.append(jnp.concatenate(
                    [kt[i * grp:(i + 1) * grp, :] if i == j else jnp.zeros((grp, dk), f32) for i in range(ng)], axis=0))
            a = a + _dot_nt(jnp.concatenate(lhs, axis=1), jnp.concatenate(rhs, axis=1))
        q_inter = qg * jnp.exp(r_start)
        o = _dot(a, vh) + _dot_nt(q_inter, st[h])
        k_dec = kt * jnp.exp(b_last - r_end)
        st_out.append(st[h] * jnp.exp(b_last) + _dot_tn(vh, k_dec))
        on = o * lax.rsqrt(jnp.mean(o * o, axis=-1, keepdims=True) + RMS_EPS)
        ys.append(on * nw[:, sl] * _silu(gate[:, sl]))
    return ys, st_out


def _hgrn_kernel(*refs, c_rows, grp, t_valid, t_pad, layer, has_s0, ub, uc):
    if has_s0:
        p_ref, lbl_ref, nw_ref, s0_ref, y_ref, so_ref, st_ref = refs
    else:
        p_ref, lbl_ref, nw_ref, y_ref, so_ref, st_ref = refs
    c = pl.program_id(1)
    nc = pl.num_programs(1)
    nh = HG_HEADS
    dim = p_ref.shape[1] // 4
    dk = dim // nh

    @pl.when(c == 0)
    def _():
        if has_s0:
            for s in range(ub):
                for h in range(nh):
                    st_ref[s, h] = s0_ref[s, h].T
        else:
            st_ref[...] = jnp.zeros_like(st_ref)

    lbl = lbl_ref[...]
    e = jnp.exp(lbl - jnp.max(lbl, axis=0, keepdims=True))
    sm = e / jnp.sum(e, axis=0, keepdims=True)
    lb = jnp.zeros((1, dim), f32)
    for r in range(1, layer + 1):
        lb = lb + sm[r:r + 1, :]

    tril = _tril_mask(c_rows)
    rr = lax.broadcasted_iota(jnp.int32, (c_rows, c_rows), 0)
    ss = lax.broadcasted_iota(jnp.int32, (c_rows, c_rows), 1)
    gshift = grp.bit_length() - 1
    diag_mask = tril & ((rr >> gshift) == (ss >> gshift))
    nw = nw_ref[...]
    rows = lax.broadcasted_iota(jnp.int32, (c_rows, 1), 0)

    for s in range(ub):
        st = [st_ref[s, h] for h in range(nh)]
        for cc in range(uc):
            r0 = (s * uc + cc) * c_rows
            valid = ((c * uc + cc) * c_rows + rows) < t_valid if t_valid < t_pad else None
            ys, st = _hgrn_chunk(p_ref[r0:r0 + c_rows, :], st, lb, nw, tril, diag_mask, valid, grp)
            for h in range(nh):
                y_ref[r0:r0 + c_rows, h * dk:(h + 1) * dk] = ys[h].astype(y_ref.dtype)
        for h in range(nh):
            st_ref[s, h] = st[h]

    @pl.when(c == nc - 1)
    def _():
        for s in range(ub):
            for h in range(nh):
                so_ref[s, h] = st_ref[s, h].T


def _units_per_step(bsz, nc, max_seqs, max_chunks):
    if nc == 1:
        return max(u for u in (1, 2, 4, 8) if u <= max_seqs and bsz % u == 0), 1
    return 1, max(u for u in (1, 2, 4, 8) if u <= max_chunks and nc % u == 0)


def hgrn_call(p, lb_logits, norm_w, s0, bsz, t_pad, t_valid, layer):
    dim = p.shape[1] // 4
    nh = HG_HEADS
    dk = dim // nh
    c_rows = min(128, t_pad)
    grp = min(HG_CHUNK_GROUP, c_rows)
    nc = t_pad // c_rows
    ub, uc = _units_per_step(bsz, nc, 8, 8)
    rows = ub * uc * c_rows
    ncs = nc // uc
    has_s0 = s0 is not None
    in_specs = [pl.BlockSpec((rows, 4 * dim), lambda b, c: (b * ncs + c, 0)),
                pl.BlockSpec(lb_logits.shape, lambda b, c: (0, 0)),
                pl.BlockSpec((1, dim), lambda b, c: (0, 0))]
    args = [p, lb_logits, norm_w.reshape(1, dim)]
    if has_s0:
        in_specs.append(pl.BlockSpec((ub, nh, dk, dk), lambda b, c: (b, 0, 0, 0)))
        args.append(s0)
    kern = functools.partial(_hgrn_kernel, c_rows=c_rows, grp=grp, t_valid=t_valid, t_pad=t_pad,
                             layer=layer, has_s0=has_s0, ub=ub, uc=uc)
    return pl.pallas_call(
        kern,
        grid=(bsz // ub, ncs),
        in_specs=in_specs,
        out_specs=[pl.BlockSpec((rows, dim), lambda b, c: (b * ncs + c, 0)),
                   pl.BlockSpec((ub, nh, dk, dk), lambda b, c: (b, 0, 0, 0))],
        out_shape=[jax.ShapeDtypeStruct((bsz * t_pad, dim), bf16),
                   jax.ShapeDtypeStruct((bsz, nh, dk, dk), f32)],
        scratch_shapes=[pltpu.VMEM((ub, nh, dk, dk), f32)],
        compiler_params=_cparams(2),
        name="hgrn",
    )(*args)


def _expand_heads(w, ex_ref):
    hi = w.astype(bf16)
    r1 = w - hi.astype(f32)
    mid = r1.astype(bf16)
    lo = (r1 - mid.astype(f32)).astype(bf16)
    return jnp.dot(jnp.concatenate([hi, mid, lo], axis=1), ex_ref[...], preferred_element_type=f32)


def _ssd_kernel(*refs, c_rows, t_valid, t_pad, has_s0, ub, uc, conv_done):
    if has_s0:
        (zx_ref, cw_ref, cb_ref, dtb_ref, alog_ref, dsk_ref, nw_ref, ex_ref, s0_ref, buf_ref,
         y_ref, so_ref, nbuf_ref, st_ref, carry_ref) = refs
    else:
        (zx_ref, cw_ref, cb_ref, dtb_ref, alog_ref, dsk_ref, nw_ref, ex_ref,
         y_ref, so_ref, st_ref, carry_ref) = refs
    c = pl.program_id(1)
    nc = pl.num_programs(1)
    hp, ns, ngr = SSD_HEAD_DIM, SSD_STATE, SSD_GROUPS
    di = y_ref.shape[1]
    nh = di // hp
    hpg = nh // ngr
    gw = ngr * ns
    kconv = cw_ref.shape[0]
    pair = 2 * hp
    assert pair == LANES and hpg % 2 == 0

    @pl.when(c == 0)
    def _():
        carry_ref[...] = jnp.zeros_like(carry_ref)
        if has_s0:
            for s in range(ub):
                st_ref[s] = s0_ref[s]
                carry_ref[s, SUBLANES - (kconv - 1):, :] = buf_ref[s]
        else:
            st_ref[...] = jnp.zeros_like(st_ref)

    cdim = cw_ref.shape[1]
    rows = lax.broadcasted_iota(jnp.int32, (c_rows, 1), 0)
    rows8 = rows[:SUBLANES]
    cw = cw_ref[...]
    a_neg = -jnp.exp(alog_ref[...])
    tril = _tril_mask(c_rows)
    dsk = dsk_ref[...]
    nw = nw_ref[...]
    lo = lax.broadcasted_iota(jnp.int32, (c_rows, LANES), 1) < hp
    for s in range(ub):
        for cc in range(uc):
            r0 = (s * uc + cc) * c_rows
            valid = ((c * uc + cc) * c_rows + rows) < t_valid if t_valid < t_pad else None
            _ssd_chunk(zx_ref[r0:r0 + c_rows, :], st_ref.at[s], carry_ref.at[s], y_ref.at[r0:r0 + c_rows, :],
                       cw, cb_ref[...], dtb_ref[...], a_neg, dsk, nw, ex_ref, tril, lo, rows8, valid, conv_done)

    @pl.when(c == nc - 1)
    def _():
        for s in range(ub):
            so_ref[s] = st_ref[s]
            if has_s0:
                last = s * c_rows + t_valid
                nbuf_ref[s] = zx_ref[last - (kconv - 1):last, di:di + cdim]


def _ssd_chunk(zx, st_ref, carry_ref, y_ref, cw, cb, dtb, a_neg, dsk, nw, ex_ref, tril, lo, rows8, valid, conv_done):
    hp, ns, ngr = SSD_HEAD_DIM, SSD_STATE, SSD_GROUPS
    c_rows = zx.shape[0]
    di = y_ref.shape[1]
    nh = di // hp
    hpg = nh // ngr
    gw = ngr * ns
    kconv, cdim = cw.shape
    pair = 2 * hp
    z = zx[:, :di]
    u = zx[:, di:di + cdim]
    if conv_done:
        xbc = u
    else:
        xbc = _silu(_causal_conv(u, carry_ref[...], cw, cb, rows8))
        carry_ref[...] = u[c_rows - SUBLANES:, :]
    dt = jax.nn.softplus(zx[:, di + cdim:] + dtb)
    if valid is not None:
        xbc = jnp.where(valid, xbc, 0.0)
        dt = jnp.where(valid, dt, 0.0)
    xs = xbc[:, :di]
    bm = xbc[:, di:di + gw]
    cm = xbc[:, di + gw:di + 2 * gw]
    acum = _cumsum_rows(dt * a_neg, tril.astype(bf16))

    def head_rows(a):
        if c_rows < LANES:
            a = jnp.concatenate([a, jnp.zeros((LANES - c_rows, LANES), f32)], axis=0)
        return a.T[:, :c_rows]

    acum_t = head_rows(acum)
    dt_t = head_rows(dt)
    a_last = acum[c_rows - 1:c_rows, :]
    xd = xs * _expand_heads(dt * jnp.exp(a_last - acum), ex_ref)
    first = lax.broadcasted_iota(jnp.int32, (pair, ns), 0) < hp

    ys = []
    for g in range(ngr):
        bg = bm[:, g * ns:(g + 1) * ns]
        cg = cm[:, g * ns:(g + 1) * ns]
        cb_mat = _dot_nt(cg, bg)
        gsl = slice(g * hpg * hp, (g + 1) * hpg * hp)
        y_inter = _dot_nt(cg, st_ref[gsl, :])
        decs = []
        for q in range(hpg // 2):
            h1 = g * hpg + 2 * q
            psl = slice(h1 * hp, h1 * hp + pair)
            ms, es = [], []
            for h in (h1, h1 + 1):
                colb = jnp.broadcast_to(acum[:, h:h + 1], (c_rows, LANES))
                seg = colb[:, :c_rows] - acum_t[h:h + 1, :]
                ms.append(cb_mat * jnp.exp(jnp.where(tril, seg, -jnp.inf)) * dt_t[h:h + 1, :])
                es.append(jnp.exp(colb))
            xp = xs[:, psl]
            x_lo = jnp.where(lo, xp, 0.0)
            x_hi = jnp.where(lo, 0.0, xp)
            if (2 * c_rows) % LANES == 0:
                y_intra = _dot(jnp.concatenate(ms, axis=1), jnp.concatenate([x_lo, x_hi], axis=0))
            else:
                y_intra = _dot(ms[0], x_lo) + _dot(ms[1], x_hi)
            y_pair = y_intra + y_inter[:, q * pair:(q + 1) * pair] * jnp.where(lo, es[0], es[1])
            ys.append(y_pair + dsk[:, psl] * xp)
            decs.append(jnp.where(first, jnp.exp(a_last[:, h1:h1 + 1]), jnp.exp(a_last[:, h1 + 1:h1 + 2])))
        st_ref[gsl, :] = st_ref[gsl, :] * jnp.concatenate(decs, axis=0) + _dot_tn(xd[:, gsl], bg)

    y = jnp.concatenate(ys, axis=1) * _silu(z)
    gdim = di // ngr
    for g in range(ngr):
        sl = slice(g * gdim, (g + 1) * gdim)
        yg = y[:, sl]
        y_ref[:, sl] = (yg * lax.rsqrt(jnp.mean(yg * yg, axis=-1, keepdims=True) + RMS_EPS) * nw[:, sl]).astype(y_ref.dtype)


def ssd_call(zx, conv_w, conv_b, dt_bias, a_log, d_skip, norm_w, s0, conv_buf, bsz, t_pad, t_valid, conv_done):
    hp, ns = SSD_HEAD_DIM, SSD_STATE
    conv_dim = conv_w.shape[1]
    di = zx.shape[1] - conv_dim - LANES
    nh = di // hp
    c_rows = min(128, t_pad)
    nc = t_pad // c_rows
    ub, uc = _units_per_step(bsz, nc, 4, 4)
    rows = ub * uc * c_rows
    ncs = nc // uc
    has_s0 = s0 is not None
    assert not (conv_done and has_s0)
    pad_h = lambda a: jnp.pad(a.reshape(1, nh), ((0, 0), (0, LANES - nh)))
    full = lambda shape: pl.BlockSpec(shape, lambda b, c: (0,) * len(shape))
    sel = (jnp.arange(di)[None, :] // hp == jnp.arange(LANES)[:, None]).astype(bf16)
    expand = jnp.concatenate([sel, sel, sel], axis=0)
    in_specs = [pl.BlockSpec((rows, zx.shape[1]), lambda b, c: (b * ncs + c, 0)),
                full(conv_w.shape), full((1, conv_dim)), full((1, LANES)), full((1, LANES)),
                full((1, di)), full((1, di)), full(expand.shape)]
    args = [zx, conv_w, conv_b.reshape(1, conv_dim), pad_h(dt_bias), pad_h(a_log),
            jnp.repeat(d_skip, hp).reshape(1, di), norm_w.reshape(1, di), expand]
    out_specs = [pl.BlockSpec((rows, di), lambda b, c: (b * ncs + c, 0)),
                 pl.BlockSpec((ub, di, ns), lambda b, c: (b, 0, 0))]
    out_shape = [jax.ShapeDtypeStruct((bsz * t_pad, di), bf16), jax.ShapeDtypeStruct((bsz, di, ns), f32)]
    if has_s0:
        assert nc == 1 and t_valid >= conv_w.shape[0] - 1
        in_specs += [pl.BlockSpec((ub, di, ns), lambda b, c: (b, 0, 0)),
                     pl.BlockSpec((ub,) + conv_buf.shape[1:], lambda b, c: (b, 0, 0))]
        args += [s0.reshape(bsz, di, ns), conv_buf]
        out_specs.append(pl.BlockSpec((ub,) + conv_buf.shape[1:], lambda b, c: (b, 0, 0)))
        out_shape.append(jax.ShapeDtypeStruct(conv_buf.shape, f32))
    kern = functools.partial(_ssd_kernel, c_rows=c_rows, t_valid=t_valid, t_pad=t_pad, has_s0=has_s0, ub=ub, uc=uc,
                             conv_done=conv_done)
    outs = pl.pallas_call(
        kern,
        grid=(bsz // ub, ncs),
        in_specs=in_specs,
        out_specs=out_specs,
        out_shape=out_shape,
        scratch_shapes=[pltpu.VMEM((ub, di, ns), f32), pltpu.VMEM((ub, SUBLANES, conv_dim), f32)],
        compiler_params=_cparams(2),
        name="ssd",
    )(*args)
    return outs[0], outs[1].reshape(bsz, nh, hp, ns), (outs[2] if has_s0 else None)


def _trunk(x, c_k, c_v, s_hg, s_ssm, s_sconv, s_fconv, prm):
    bsz, t, d = x.shape
    depth = prm["norm_mix_w"].shape[0]
    decode = c_k is not None
    tp = -(-t // SUBLANES) * SUBLANES
    if tp != t:
        x = jnp.pad(x, ((0, 0), (0, tp - t), (0, 0)))
    m = bsz * tp
    n_heads = d // ATT_HEAD_DIM
    qd = n_heads * ATT_HEAD_DIM
    kvd = ATT_KV_HEADS * ATT_HEAD_DIM
    x2 = x.reshape(m, d)
    xn, xn_w = x2, prm["norm_mix_w"][0]
    nk, nv, nhg, nssm, nsconv, nfconv = [], [], [], [], [], []
    if decode:
        ck_all = jnp.transpose(c_k, (0, 1, 3, 4, 2)).reshape(c_k.shape[0], bsz, kvd, WINDOW)
        cv_all = jnp.transpose(c_v, (0, 1, 3, 4, 2)).reshape(c_v.shape[0], bsz, kvd, WINDOW)
    slot = [0] * N_MIXERS
    for i in range(depth):
        kind = i % N_MIXERS
        j = slot[kind]
        slot[kind] += 1
        nw_ffn = prm["norm_ffn_w"][i]
        if kind == 0:
            w_qkv, b_qkv = prm["attn_w_qkv"], prm["attn_b_qkv"][j]
            nq = w_qkv.shape[2]
            if decode:
                qkv = proj_call(xn, xn_w, w_qkv, j, b_qkv, f32, "attn_qkv").reshape(bsz, tp, nq)
                o, k_new, v_new = attn_cache_call(qkv, ck_all, cv_all, j, prm["attn_sinks"][j], n_heads, t)
                o = o.reshape(m, qd)
            else:
                qkv = proj_call(xn, xn_w, w_qkv, j, b_qkv, bf16, "attn_qkv")
                o = attn_prompt_call(qkv, prm["attn_sinks"][j], bsz, t, n_heads)
                xn_last = xn.reshape(bsz, t, d)[:, t - WINDOW:].reshape(bsz * WINDOW, d)
                kv_last = proj_call(xn_last, xn_w, w_qkv[:, :, qd:], j, b_qkv[qd:], f32,
                                    "attn_kv_tail").reshape(bsz, WINDOW, 2, kvd)
                to4 = lambda a: a.reshape(bsz, WINDOW, ATT_KV_HEADS, ATT_HEAD_DIM)
                k_new, v_new = to4(kv_last[:, :, 0]), to4(kv_last[:, :, 1])
            nk.append(k_new)
            nv.append(v_new)
            act, w_o, b_o = o, (prm["attn_w_o"], j), prm["attn_b_o"][j]
        elif kind == 1:
            p = proj_call(xn, xn_w, prm["hgrn_w_in"], j, None, f32, "hgrn_in")
            y, s_new = hgrn_call(p, prm["hgrn_lb_logits"], prm["hgrn_norm_w"][j],
                                 None if s_hg is None else s_hg[j], bsz, tp, t, i)
            nhg.append(s_new)
            act, w_o, b_o = y, (prm["hgrn_w_o"], j), None
        else:
            conv_dim = prm["ssd_conv_w"].shape[2]
            di = prm["ssd_w_o"].shape[1]
            kc = prm["ssd_conv_w"].shape[1]
            if s_sconv is None:
                zx, pre = ssd_in_call(xn, prm["ssd_w_in"], j, prm["ssd_conv_w"][j], prm["ssd_conv_b"][j], tp)
                pre = pre.reshape(bsz, tp // min(PROJ_TM, m), SUBLANES, conv_dim)[:, -1, SUBLANES - (kc - 1):]
            else:
                zx = proj_call(xn, xn_w, prm["ssd_w_in"], j, None, f32, "ssd_in")
            y, s_new, conv_new = ssd_call(zx, prm["ssd_conv_w"][j], prm["ssd_conv_b"][j], prm["ssd_dt_bias"][j],
                                prm["ssd_a_log"][j], prm["ssd_d"][j], prm["ssd_norm_w"][j],
                                None if s_ssm is None else s_ssm[j],
                                None if s_sconv is None else s_sconv[j], bsz, tp, t, s_sconv is None)
            nssm.append(s_new)
            nsconv.append(pre if s_sconv is None else conv_new)
            act, w_o, b_o = y, (prm["ssd_w_o"], j), None
        last = i == depth - 1
        next_w = prm["norm_final_w"] if last else prm["norm_mix_w"][i + 1]
        dff = prm["ffn_w_down"].shape[1]
        kf = prm["ffn_conv_w"].shape[1]
        x2, xn, tail = ffn_call(act, w_o, b_o, x2, nw_ffn, (prm["ffn_w_up"], i), prm["ffn_conv_w"][i],
                                prm["ffn_conv_b"][i], (prm["ffn_w_down"], i), next_w, tp,
                                None if s_fconv is None else (s_fconv, i), t, f32 if last else bf16, "ffn")
        xn_w = None
        if s_fconv is not None:
            nfconv.append(tail)
        else:
            tiles_per_seq = tp // tail_tile_rows(m)
            tl = tail.reshape(bsz, tiles_per_seq, SUBLANES, dff)[:, -1]
            nfconv.append(tl[:, SUBLANES - (kf - 1):])
    stack = lambda xs: xs[0][None] if len(xs) == 1 else jnp.stack(xs)
    y = xn.reshape(bsz, tp, d)[:, :t]
    nk, nv = stack(nk), stack(nv)
    if decode:
        unview = lambda a: jnp.transpose(a.reshape(a.shape[0], bsz, ATT_KV_HEADS, ATT_HEAD_DIM, WINDOW), (0, 1, 4, 2, 3))
        nk, nv = unview(nk), unview(nv)
    return (y, nk, nv, stack(nhg), stack(nssm), stack(nsconv), stack(nfconv))


def tail_tile_rows(m):
    return min(FFN_TM, m)


def kernel(x_prompt, x_sample, cache_attn_k, cache_attn_v, state_hgrn, state_ssm, state_ssm_conv, state_ffn_conv, norm_mix_w, norm_ffn_w, norm_final_w, attn_w_qkv, attn_b_qkv, attn_sinks, attn_w_o, attn_b_o, hgrn_w_in, hgrn_lb_logits, hgrn_norm_w, hgrn_w_o, ssd_w_in, ssd_conv_w, ssd_conv_b, ssd_dt_bias, ssd_a_log, ssd_d, ssd_norm_w, ssd_w_o, ffn_w_up, ffn_conv_w, ffn_conv_b, ffn_w_down):
    cast = lambda w: w.astype(bf16)
    ssd_in_pad = (-ssd_w_in.shape[2]) % LANES
    prm = dict(norm_mix_w=norm_mix_w, norm_ffn_w=norm_ffn_w, norm_final_w=norm_final_w,
               attn_w_qkv=cast(attn_w_qkv), attn_b_qkv=attn_b_qkv, attn_sinks=attn_sinks,
               attn_w_o=cast(attn_w_o), attn_b_o=attn_b_o,
               hgrn_w_in=cast(hgrn_w_in), hgrn_lb_logits=hgrn_lb_logits, hgrn_norm_w=hgrn_norm_w,
               hgrn_w_o=cast(hgrn_w_o),
               ssd_w_in=jnp.pad(cast(ssd_w_in), ((0, 0), (0, 0), (0, ssd_in_pad))), ssd_conv_w=ssd_conv_w,
               ssd_conv_b=ssd_conv_b, ssd_dt_bias=ssd_dt_bias,
               ssd_a_log=ssd_a_log, ssd_d=ssd_d, ssd_norm_w=ssd_norm_w, ssd_w_o=cast(ssd_w_o),
               ffn_w_up=cast(ffn_w_up), ffn_conv_w=ffn_conv_w, ffn_conv_b=ffn_conv_b, ffn_w_down=cast(ffn_w_down))
    outs_p = _trunk(x_prompt, None, None, None, None, None, None, prm)
    outs_s = _trunk(x_sample, cache_attn_k, cache_attn_v, state_hgrn, state_ssm, state_ssm_conv,
                    state_ffn_conv, prm)
    return (outs_p[0], outs_s[0]) + outs_p[1:] + outs_s[1:]
```

```python
import functools

import jax
import jax.numpy as jnp
from jax import lax
from jax.experimental import pallas as pl
from jax.experimental.pallas import tpu as pltpu

bf16 = jnp.bfloat16
f32 = jnp.float32

RMS_EPS = 1e-6
N_MIXERS = 3
WINDOW = 128
PAST_LEN = 8192
ATT_HEAD_DIM = 64
ATT_KV_HEADS = 4
HG_HEADS = 8
HG_CHUNK_GROUP = 32
SSD_HEAD_DIM = 64
SSD_STATE = 128
SSD_GROUPS = 4

LANES = 128
SUBLANES = 8
VMEM_LIMIT = 56 * 1024 * 1024
PROJ_TM = 512
FFN_TM = 512


def _cparams(n_axes):
    return pltpu.CompilerParams(dimension_semantics=("arbitrary",) * n_axes,
                                vmem_limit_bytes=VMEM_LIMIT)


def _dot(a, b):
    return jnp.dot(a.astype(bf16), b.astype(bf16), preferred_element_type=f32)


def _dot_nt(a, b):
    return lax.dot_general(a.astype(bf16), b.astype(bf16), (((1,), (1,)), ((), ())),
                           preferred_element_type=f32)


def _dot_tn(a, b):
    return lax.dot_general(a.astype(bf16), b.astype(bf16), (((0,), (0,)), ((), ())),
                           preferred_element_type=f32)


def _silu(x):
    return x * jax.nn.sigmoid(x)


def _rms(x, w):
    ms = jnp.mean(x * x, axis=-1, keepdims=True)
    return x * lax.rsqrt(ms + RMS_EPS) * w


def _cumsum_rows(x, tril):
    hi = x.astype(bf16)
    r1 = x - hi.astype(f32)
    mid = r1.astype(bf16)
    lo = (r1 - mid.astype(f32)).astype(bf16)
    d = functools.partial(jnp.dot, preferred_element_type=f32)
    return d(tril, hi) + d(tril, mid) + d(tril, lo)


def _tril_mask(c):
    r = lax.broadcasted_iota(jnp.int32, (c, c), 0)
    s = lax.broadcasted_iota(jnp.int32, (c, c), 1)
    return r >= s


def _resident(shape):
    return pl.BlockSpec(shape, lambda *_: (0,) * len(shape), pipeline_mode=pl.Buffered(1))


def _resident_layer(w, layer):
    return pl.BlockSpec((None,) + w.shape[1:], lambda *_: (layer, 0, 0), pipeline_mode=pl.Buffered(1))


def _proj_kernel(x_ref, nw_ref, w_ref, b_ref, o_ref):
    x = x_ref[...]
    if x.dtype != bf16:
        x = _rms(x, nw_ref[...]).astype(bf16)
    acc = jnp.dot(x, w_ref[...], preferred_element_type=f32)
    o_ref[...] = (acc + b_ref[...]).astype(o_ref.dtype)


def proj_call(x, norm_w, w, layer, bias, out_dtype, name):
    m, k = x.shape
    n = w.shape[2]
    tm = min(PROJ_TM, m)
    if bias is None:
        bias = jnp.zeros((n,), f32)
    if norm_w is None:
        assert x.dtype == bf16
        norm_w = jnp.ones((k,), f32)
    return pl.pallas_call(
        _proj_kernel,
        grid=(m // tm,),
        in_specs=[pl.BlockSpec((tm, k), lambda i: (i, 0)), _resident((1, k)), _resident_layer(w, layer),
                  _resident((1, n))],
        out_specs=pl.BlockSpec((tm, n), lambda i: (i, 0)),
        out_shape=jax.ShapeDtypeStruct((m, n), out_dtype),
        compiler_params=_cparams(1),
        name=name,
    )(x, norm_w.reshape(1, k), w, bias.reshape(1, n))


def _causal_conv(u, prev, cw, cb, rows8):
    kconv = cw.shape[0]
    yc = cb
    for kk in range(kconv - 1):
        back = kconv - 1 - kk
        sh = pltpu.roll(u, back, 0)
        top = sh[:SUBLANES]
        for r0 in range(back):
            top = jnp.where(rows8 == r0, prev[SUBLANES - back + r0:SUBLANES - back + r0 + 1, :], top)
        sh = jnp.concatenate([top, sh[SUBLANES:]], axis=0) if u.shape[0] > SUBLANES else top
        yc = yc + sh * cw[kk:kk + 1, :]
    return yc + u * cw[kconv - 1:kconv, :]


def _ssd_in_kernel(x_ref, w_ref, cw_ref, cb_ref, o_ref, tail_ref, carry_ref, *, tm, tc, t_seq, di):
    i = pl.program_id(0)
    cdim = cw_ref.shape[1]

    @pl.when(((i * tm) & (t_seq - 1)) == 0)
    def _():
        carry_ref[...] = jnp.zeros_like(carry_ref)

    x = x_ref[...]
    rows8 = lax.broadcasted_iota(jnp.int32, (SUBLANES, 1), 0)
    nz = di // tc
    for idx, c0 in enumerate(range(0, cdim, tc)):
        cs = slice(c0, c0 + tc)
        u = jnp.dot(x, w_ref[:, di + c0:di + c0 + tc], preferred_element_type=f32)
        tail_ref[0, :, cs] = u[tm - SUBLANES:, :]
        yc = _causal_conv(u, carry_ref[:, cs], cw_ref[:, cs], cb_ref[:, cs], rows8)
        carry_ref[:, cs] = u[tm - SUBLANES:, :]
        o_ref[:, di + c0:di + c0 + tc] = _silu(yc)
        if idx < nz:
            o_ref[:, idx * tc:(idx + 1) * tc] = jnp.dot(x, w_ref[:, idx * tc:(idx + 1) * tc],
                                                        preferred_element_type=f32)
    o_ref[:, di + cdim:] = jnp.dot(x, w_ref[:, di + cdim:], preferred_element_type=f32)


def ssd_in_call(xn, w, layer, conv_w, conv_b, t_seq):
    m, k = xn.shape
    n = w.shape[2]
    cdim = conv_w.shape[1]
    di = n - cdim - LANES
    tm = min(PROJ_TM, m)
    tc = 2 * LANES
    assert t_seq % tm == 0 and cdim % tc == 0 and di % tc == 0 and di <= cdim and (t_seq & (t_seq - 1)) == 0
    kern = functools.partial(_ssd_in_kernel, tm=tm, tc=tc, t_seq=t_seq, di=di)
    return pl.pallas_call(
        kern,
        grid=(m // tm,),
        in_specs=[pl.BlockSpec((tm, k), lambda i: (i, 0)), _resident_layer(w, layer),
                  _resident(conv_w.shape), _resident((1, cdim))],
        out_specs=[pl.BlockSpec((tm, n), lambda i: (i, 0)), pl.BlockSpec((1, SUBLANES, cdim), lambda i: (i, 0, 0))],
        out_shape=[jax.ShapeDtypeStruct((m, n), f32), jax.ShapeDtypeStruct((m // tm, SUBLANES, cdim), f32)],
        scratch_shapes=[pltpu.VMEM((SUBLANES, cdim), f32)],
        compiler_params=_cparams(1),
        name="ssd_in_conv",
    )(xn, w, conv_w, conv_b.reshape(1, cdim))


def _ffn_kernel(*refs, tm, tf, t_seq, t_valid, has_state):
    if has_state:
        (act_ref, wo_ref, bo_ref, x_ref, nw1_ref, wu_ref, cw_ref, cb_ref, wd_ref, nw2_ref, buf_ref,
         xo_ref, xn2_ref, nst_ref, carry_ref) = refs
    else:
        (act_ref, wo_ref, bo_ref, x_ref, nw1_ref, wu_ref, cw_ref, cb_ref, wd_ref, nw2_ref,
         xo_ref, xn2_ref, tail_ref, carry_ref) = refs
    i = pl.program_id(0)
    dff = wd_ref.shape[0]

    x1 = jnp.dot(act_ref[...].astype(bf16), wo_ref[...], preferred_element_type=f32) + bo_ref[...] + x_ref[...]
    xn = _rms(x1, nw1_ref[...]).astype(bf16)
    rows8 = lax.broadcasted_iota(jnp.int32, (SUBLANES, 1), 0)
    if not has_state:
        @pl.when(((i * tm) & (t_seq - 1)) == 0)
        def _():
            carry_ref[...] = jnp.zeros_like(carry_ref)
    gs = []
    for c0 in range(0, dff, tf):
        cs = slice(c0, c0 + tf)
        a = jnp.dot(xn, wu_ref[:, cs], preferred_element_type=f32)
        b = jnp.dot(xn, wu_ref[:, dff + c0:dff + c0 + tf], preferred_element_type=f32)
        r1 = pltpu.roll(a, 1, 0)
        r2 = pltpu.roll(a, 2, 0)
        if has_state:
            p1s, p2s = [], []
            for s in range(tm // SUBLANES):
                rs = slice(s * SUBLANES, (s + 1) * SUBLANES)
                b0, b1 = buf_ref[s, 0:1, cs], buf_ref[s, 1:2, cs]
                p1s.append(jnp.where(rows8 == 0, b1, r1[rs, :]))
                p2s.append(jnp.where(rows8 == 0, b0, jnp.where(rows8 == 1, b1, r2[rs, :])))
                nst_ref[s, :, cs] = a[s * SUBLANES + t_valid - 2:s * SUBLANES + t_valid, :]
            p1 = jnp.concatenate(p1s, axis=0)
            p2 = jnp.concatenate(p2s, axis=0)
        else:
            tail_ref[0, :, cs] = a[tm - SUBLANES:, :]
            carry = carry_ref[:, cs]
            top1 = jnp.where(rows8 == 0, carry[7:8, :], r1[:SUBLANES])
            top2 = jnp.where(rows8 == 0, carry[6:7, :], jnp.where(rows8 == 1, carry[7:8, :], r2[:SUBLANES]))
            p1 = jnp.concatenate([top1, r1[SUBLANES:]], axis=0)
            p2 = jnp.concatenate([top2, r2[SUBLANES:]], axis=0)
            carry_ref[:, cs] = a[tm - SUBLANES:, :]
        y = cb_ref[:, cs] + p2 * cw_ref[0:1, cs]
        y = y + p1 * cw_ref[1:2, cs]
        y = y + a * cw_ref[2:3, cs]
        gs.append((_silu(y) * b).astype(bf16))
    g = jnp.concatenate(gs, axis=1)
    xo = jnp.dot(g, wd_ref[...], preferred_element_type=f32) + x1
    xo_ref[...] = xo
    xn2_ref[...] = _rms(xo, nw2_ref[...]).astype(xn2_ref.dtype)


def ffn_call(act, w_o, b_o, x, ffn_norm_w, w_up, conv_w, conv_b, w_down, next_w, t_seq, buf, t_valid, out_dtype, name):
    m, d = x.shape
    ko = act.shape[1]
    dff = w_down[0].shape[1]
    tf = 2 * LANES
    assert dff % tf == 0 and (t_seq & (t_seq - 1)) == 0
    has_state = buf is not None
    tm = min(FFN_TM // 2 if has_state else FFN_TM, m)
    assert (t_seq == SUBLANES and t_valid >= 2 and conv_w.shape[0] == 3) if has_state else (t_seq % tm == 0)
    if b_o is None:
        b_o = jnp.zeros((d,), f32)
    row = lambda w: pl.BlockSpec((tm, w), lambda i: (i, 0))
    in_specs = [row(ko), _resident_layer(*w_o), _resident((1, d)), row(d), _resident((1, d)),
                _resident_layer(*w_up), _resident((3, dff)), _resident((1, dff)),
                _resident_layer(*w_down), _resident((1, d))]
    args = [act, w_o[0], b_o.reshape(1, d), x, ffn_norm_w.reshape(1, d),
            w_up[0], conv_w, conv_b.reshape(1, dff), w_down[0], next_w.reshape(1, d)]
    if has_state:
        nseq = tm // t_seq
        in_specs.append(pl.BlockSpec((None, nseq, 2, dff), lambda i: (buf[1], i, 0, 0)))
        args.append(buf[0])
        third_spec = pl.BlockSpec((nseq, 2, dff), lambda i: (i, 0, 0))
        third_shape = jax.ShapeDtypeStruct((m // t_seq, 2, dff), f32)
    else:
        third_spec = pl.BlockSpec((1, SUBLANES, dff), lambda i: (i, 0, 0))
        third_shape = jax.ShapeDtypeStruct((m // tm, SUBLANES, dff), f32)
    kern = functools.partial(_ffn_kernel, tm=tm, tf=tf, t_seq=t_seq, t_valid=t_valid, has_state=has_state)
    return pl.pallas_call(
        kern,
        grid=(m // tm,),
        in_specs=in_specs,
        out_specs=[row(d), row(d), third_spec],
        out_shape=[jax.ShapeDtypeStruct((m, d), f32), jax.ShapeDtypeStruct((m, d), out_dtype), third_shape],
        scratch_shapes=[pltpu.VMEM((SUBLANES, dff), f32)],
        compiler_params=_cparams(1),
        name=name,
    )(*args)


def _attn_softmax_pv(s, sink_col, vj):
    m = jnp.maximum(jnp.max(s, axis=-1, keepdims=True), sink_col)
    p = jnp.exp(s - m)
    l = jnp.sum(p, axis=-1, keepdims=True) + jnp.exp(sink_col - m)
    return _dot(p, vj) / l


def _sink_col(sinks_ref, j, group, rows_per_head):
    r = lax.broadcasted_iota(jnp.int32, (group * rows_per_head, 1), 0)
    col = jnp.full((group * rows_per_head, 1), sinks_ref[j * group], f32)
    for p in range(1, group):
        col = jnp.where(r >= p * rows_per_head, sinks_ref[j * group + p], col)
    return col


def _attn_prompt_kernel(sinks_ref, q_ref, kvc_ref, kvp_ref, o_ref, *, n_heads):
    n = pl.program_id(1)
    w, hd, kvh = WINDOW, ATT_HEAD_DIM, ATT_KV_HEADS
    group = n_heads // kvh
    kvw = kvh * hd
    assert group == kvh
    kv = jnp.concatenate([kvp_ref[...], kvc_ref[...]], axis=0)
    k2 = kv[:, :kvw]
    v2 = kv[:, kvw:]
    i = lax.broadcasted_iota(jnp.int32, (w, 2 * w), 0)
    c = lax.broadcasted_iota(jnp.int32, (w, 2 * w), 1)
    band = (c > i) & (c <= i + w) & ((n > 0) | (c >= w))
    mask = jnp.concatenate([band] * group, axis=0)
    slot = lax.broadcasted_iota(jnp.int32, (w, kvw), 1) >> (hd.bit_length() - 1)
    for j in range(kvh):
        qg = q_ref[:, j * kvw:(j + 1) * kvw] * (hd ** -0.5)
        qs = jnp.concatenate([jnp.where(slot == j, pltpu.roll(qg, ((j - p) % kvh) * hd, 1), 0)
                              for p in range(group)], axis=0)
        s = jnp.where(mask, _dot_nt(qs, k2), -jnp.inf)
        of = _attn_softmax_pv(s, _sink_col(sinks_ref, j, group, w), v2).astype(o_ref.dtype)
        og = of[0:w, :]
        for p in range(group):
            piece = of[p * w:(p + 1) * w, :]
            og = jnp.where(slot == p, pltpu.roll(piece, ((p - j) % kvh) * hd, 1), og)
        o_ref[:, j * kvw:(j + 1) * kvw] = og


def attn_prompt_call(qkv, sinks, bsz, t, n_heads):
    w, hd, kvh = WINDOW, ATT_HEAD_DIM, ATT_KV_HEADS
    nb = t // w
    qd = n_heads * hd
    kvd = 2 * kvh * hd
    assert qd % kvd == 0
    kvblk = qd // kvd
    kern = functools.partial(_attn_prompt_kernel, n_heads=n_heads)
    return pl.pallas_call(
        kern,
        grid=(bsz, nb),
        in_specs=[pl.BlockSpec(memory_space=pltpu.SMEM),
                  pl.BlockSpec((w, qd), lambda b, n: (b * nb + n, 0)),
                  pl.BlockSpec((w, kvd), lambda b, n: (b * nb + n, kvblk)),
                  pl.BlockSpec((w, kvd), lambda b, n: (b * nb + jnp.maximum(n - 1, 0), kvblk))],
        out_specs=pl.BlockSpec((w, qd), lambda b, n: (b * nb + n, 0)),
        out_shape=jax.ShapeDtypeStruct((bsz * t, qd), bf16),
        compiler_params=_cparams(2),
        name="attn_prompt",
    )(sinks, qkv, qkv, qkv)


def _shifted_window(cache, new_rows, t_valid):
    tp, w = new_rows.shape[0], cache.shape[1]
    hi = new_rows.astype(bf16)
    r1 = new_rows - hi.astype(f32)
    mid = r1.astype(bf16)
    lo = (r1 - mid.astype(f32)).astype(bf16)
    t = lax.broadcasted_iota(jnp.int32, (tp, w), 0)
    col = lax.broadcasted_iota(jnp.int32, (tp, w), 1)
    place = ((col == w - t_valid + t) & (t < t_valid)).astype(bf16)
    placed = _dot_tn(jnp.concatenate([hi, mid, lo], axis=0), jnp.concatenate([place] * 3, axis=0))
    lane = lax.broadcasted_iota(jnp.int32, cache.shape, 1)
    return jnp.where(lane >= w - t_valid, placed, pltpu.roll(cache, w - t_valid, 1))


def _attn_cache_kernel(*refs, n_heads, t_valid, bt):
    sinks_ref, q_ref, ck_ref, cv_ref = refs[:4]
    o_ref, nk_ref, nv_ref = refs[-3:]
    w, hd, kvh = WINDOW, ATT_HEAD_DIM, ATT_KV_HEADS
    group = n_heads // kvh
    tp = q_ref.shape[1]
    qd = n_heads * hd
    tq = lax.broadcasted_iota(jnp.int32, (tp, w), 0)
    jc = lax.broadcasted_iota(jnp.int32, (tp, w), 1)
    mc = (jc > tq) & (jc - w + PAST_LEN >= 0)
    tq2 = lax.broadcasted_iota(jnp.int32, (tp, tp), 0)
    un = lax.broadcasted_iota(jnp.int32, (tp, tp), 1)
    mn = (un <= tq2) & (un < t_valid)
    mask_c = jnp.concatenate([mc] * n_heads, axis=0)
    mask_n = jnp.concatenate([mn] * n_heads, axis=0)
    kvw = kvh * hd
    assert group == kvh
    slot = lax.broadcasted_iota(jnp.int32, (tp, kvw), 1) >> (hd.bit_length() - 1)
    sink = _sink_col(sinks_ref, 0, n_heads, tp)

    for bi in range(bt):
        row = q_ref[bi]
        ck = ck_ref[bi]
        cv = cv_ref[bi]
        kn = row[:, qd:qd + kvw]
        vn = row[:, qd + kvw:]
        win = _shifted_window(jnp.concatenate([ck, cv], axis=0), row[:, qd:], t_valid)
        nk_ref[bi] = win[:kvw]
        nv_ref[bi] = win[kvw:]
        pieces = []
        for j in range(kvh):
            qg = row[:, j * kvw:(j + 1) * kvw] * (hd ** -0.5)
            for p in range(group):
                pieces.append(jnp.where(slot == j, pltpu.roll(qg, ((j - p) % kvh) * hd, 1), 0.0))
        qs = jnp.concatenate(pieces, axis=0)
        sc = jnp.where(mask_c, _dot(qs, ck), -jnp.inf)
        sn = jnp.where(mask_n, _dot_nt(qs, kn), -jnp.inf)
        m = jnp.maximum(jnp.maximum(jnp.max(sc, axis=-1, keepdims=True),
                                    jnp.max(sn, axis=-1, keepdims=True)), sink)
        pc = jnp.exp(sc - m)
        pn = jnp.exp(sn - m)
        l = jnp.sum(pc, axis=-1, keepdims=True) + jnp.sum(pn, axis=-1, keepdims=True) + jnp.exp(sink - m)
        of = (_dot_nt(pc, cv) + _dot(pn, vn)) / l
        for j in range(kvh):
            og = jnp.zeros((tp, kvw), f32)
            for p in range(group):
                h = j * group + p
                og = og + jnp.where(slot == p, pltpu.roll(of[h * tp:(h + 1) * tp, :], ((p - j) % kvh) * hd, 1), 0.0)
            o_ref[bi, :, j * kvw:(j + 1) * kvw] = og


def attn_cache_call(qkv, cache_k, cache_v, layer, sinks, n_heads, t_valid, stacked):
    bsz, tp, nq = qkv.shape
    w = WINDOW
    kd = cache_k.shape[2]
    qd = n_heads * ATT_HEAD_DIM
    bt = 8
    kern = functools.partial(_attn_cache_kernel, n_heads=n_heads, t_valid=t_valid, bt=bt)
    slab = pl.BlockSpec((None, bt, kd, w), lambda b: (layer, b, 0, 0))
    prev = [] if stacked is None else list(stacked)
    return pl.pallas_call(
        kern,
        grid=(bsz // bt,),
        in_specs=[pl.BlockSpec(memory_space=pltpu.SMEM),
                  pl.BlockSpec((bt, tp, nq), lambda b: (b, 0, 0)), slab, slab]
                 + [pl.BlockSpec(memory_space=pl.ANY)] * len(prev),
        out_specs=[pl.BlockSpec((bt, tp, qd), lambda b: (b, 0, 0)), slab, slab],
        out_shape=[jax.ShapeDtypeStruct((bsz, tp, qd), f32),
                   jax.ShapeDtypeStruct(cache_k.shape, f32),
                   jax.ShapeDtypeStruct(cache_v.shape, f32)],
        input_output_aliases={4: 1, 5: 2} if prev else {},
        compiler_params=_cparams(1),
        name="attn_cache",
    )(sinks, qkv, cache_k, cache_v, *prev)


def _hgrn_chunk(x, st, lb, nw, tril, diag_mask, valid, grp):
    nh = HG_HEADS
    c_rows = x.shape[0]
    dim = x.shape[1] // 4
    dk = dim // nh
    ng = c_rows // grp
    if valid is not None:
        x = jnp.where(valid, x, 0.0)
    q = _silu(x[:, 0:dim])
    fg = lb + (1.0 - lb) * jax.nn.sigmoid(x[:, dim:2 * dim])
    logf = jnp.log(fg)
    k = 1.0 - fg
    v = x[:, 2 * dim:3 * dim]
    gate = x[:, 3 * dim:4 * dim]
    if valid is not None:
        logf = jnp.where(valid, logf, 0.0)
        k = jnp.where(valid, k, 0.0)
    bcum = _cumsum_rows(logf, tril.astype(bf16))

    def rows_of(vals):
        return jnp.concatenate([jnp.broadcast_to(r, (grp, dk)) for r in vals], axis=0)

    ys, st_out = [], []
    for h in range(nh):
        sl = slice(h * dk, (h + 1) * dk)
        b = bcum[:, sl]
        qh, kh, vh = q[:, sl], k[:, sl], v[:, sl]
        zero = jnp.zeros((1, dk), f32)
        r = [zero] + [b[i * grp - 1:i * grp, :] for i in range(1, ng)] + [b[c_rows - 1:c_rows, :]]
        mid = [b[i * grp + grp // 2 - 1:i * grp + grp // 2, :] for i in range(ng)]
        b_last = r[ng]
        r_start = rows_of(r[:ng])
        r_end = rows_of(r[1:])
        r_mid = rows_of(mid)
        qg = qh * jnp.exp(b - r_start)
        kt = kh * jnp.exp(r_end - b)
        qm = qh * jnp.exp(b - r_mid)
        km = kh * jnp.exp(r_mid - b)
        a = jnp.where(diag_mask, _dot_nt(qm, km), 0.0)
        if ng > 1:
            lhs, rhs = [], []
            for j in range(ng - 1):
                lhs.append(jnp.concatenate(
                    [jnp.zeros((grp, dk), f32) if i <= j else qg[i * grp:(i + 1) * grp, :] * jnp.exp(r[i] - r[j + 1])
                     for i in range(ng)], axis=0))
                rhs.append(jnp.concatenate(
                    [kt[i * grp:(i + 1) * grp, :] if i == j else jnp.zeros((grp, dk), f32) for i in range(ng)], axis=0))
            a = a + _dot_nt(jnp.concatenate(lhs, axis=1), jnp.concatenate(rhs, axis=1))
        q_inter = qg * jnp.exp(r_start)
        o = _dot(a, vh) + _dot_nt(q_inter, st[h])
        k_dec = kt * jnp.exp(b_last - r_end)
        st_out.append(st[h] * jnp.exp(b_last) + _dot_tn(vh, k_dec))
        on = o * lax.rsqrt(jnp.mean(o * o, axis=-1, keepdims=True) + RMS_EPS)
        ys.append(on * nw[:, sl] * _silu(gate[:, sl]))
    return ys, st_out


def _hgrn_kernel(*refs, c_rows, grp, t_valid, t_pad, layer, has_s0, ub, uc):
    if has_s0:
        p_ref, lbl_ref, nw_ref, s0_ref, y_ref, so_ref, st_ref = refs
    else:
        p_ref, lbl_ref, nw_ref, y_ref, so_ref, st_ref = refs
    c = pl.program_id(1)
    nc = pl.num_programs(1)
    nh = HG_HEADS
    dim = p_ref.shape[1] // 4
    dk = dim // nh

    @pl.when(c == 0)
    def _():
        if has_s0:
            for s in range(ub):
                for h in range(nh):
                    st_ref[s, h] = s0_ref[s, h].T
        else:
            st_ref[...] = jnp.zeros_like(st_ref)

    lbl = lbl_ref[...]
    e = jnp.exp(lbl - jnp.max(lbl, axis=0, keepdims=True))
    sm = e / jnp.sum(e, axis=0, keepdims=True)
    lb = jnp.zeros((1, dim), f32)
    for r in range(1, layer + 1):
        lb = lb + sm[r:r + 1, :]

    tril = _tril_mask(c_rows)
    rr = lax.broadcasted_iota(jnp.int32, (c_rows, c_rows), 0)
    ss = lax.broadcasted_iota(jnp.int32, (c_rows, c_rows), 1)
    gshift = grp.bit_length() - 1
    diag_mask = tril & ((rr >> gshift) == (ss >> gshift))
    nw = nw_ref[...]
    rows = lax.broadcasted_iota(jnp.int32, (c_rows, 1), 0)

    for s in range(ub):
        st = [st_ref[s, h] for h in range(nh)]
        for cc in range(uc):
            r0 = (s * uc + cc) * c_rows
            valid = ((c * uc + cc) * c_rows + rows) < t_valid if t_valid < t_pad else None
            ys, st = _hgrn_chunk(p_ref[r0:r0 + c_rows, :], st, lb, nw, tril, diag_mask, valid, grp)
            for h in range(nh):
                y_ref[r0:r0 + c_rows, h * dk:(h + 1) * dk] = ys[h].astype(y_ref.dtype)
        for h in range(nh):
            st_ref[s, h] = st[h]

    @pl.when(c == nc - 1)
    def _():
        for s in range(ub):
            for h in range(nh):
                so_ref[s, h] = st_ref[s, h].T


def _units_per_step(bsz, nc, max_seqs, max_chunks):
    if nc == 1:
        return max(u for u in (1, 2, 4, 8) if u <= max_seqs and bsz % u == 0), 1
    return 1, max(u for u in (1, 2, 4, 8) if u <= max_chunks and nc % u == 0)


def hgrn_call(p, lb_logits, norm_w, s0, bsz, t_pad, t_valid, layer):
    dim = p.shape[1] // 4
    nh = HG_HEADS
    dk = dim // nh
    c_rows = min(128, t_pad)
    grp = min(HG_CHUNK_GROUP, c_rows)
    nc = t_pad // c_rows
    ub, uc = _units_per_step(bsz, nc, 8, 8)
    rows = ub * uc * c_rows
    ncs = nc // uc
    has_s0 = s0 is not None
    in_specs = [pl.BlockSpec((rows, 4 * dim), lambda b, c: (b * ncs + c, 0)),
                pl.BlockSpec(lb_logits.shape, lambda b, c: (0, 0)),
                pl.BlockSpec((1, dim), lambda b, c: (0, 0))]
    args = [p, lb_logits, norm_w.reshape(1, dim)]
    if has_s0:
        in_specs.append(pl.BlockSpec((ub, nh, dk, dk), lambda b, c: (b, 0, 0, 0)))
        args.append(s0)
    kern = functools.partial(_hgrn_kernel, c_rows=c_rows, grp=grp, t_valid=t_valid, t_pad=t_pad,
                             layer=layer, has_s0=has_s0, ub=ub, uc=uc)
    return pl.pallas_call(
        kern,
        grid=(bsz // ub, ncs),
        in_specs=in_specs,
        out_specs=[pl.BlockSpec((rows, dim), lambda b, c: (b * ncs + c, 0)),
                   pl.BlockSpec((ub, nh, dk, dk), lambda b, c: (b, 0, 0, 0))],
        out_shape=[jax.ShapeDtypeStruct((bsz * t_pad, dim), bf16),
                   jax.ShapeDtypeStruct((bsz, nh, dk, dk), f32)],
        scratch_shapes=[pltpu.VMEM((ub, nh, dk, dk), f32)],
        compiler_params=_cparams(2),
        name="hgrn",
    )(*args)


def _expand_heads(w, ex_ref):
    hi = w.astype(bf16)
    r1 = w - hi.astype(f32)
    mid = r1.astype(bf16)
    lo = (r1 - mid.astype(f32)).astype(bf16)
    return jnp.dot(jnp.concatenate([hi, mid, lo], axis=1), ex_ref[...], preferred_element_type=f32)


def _ssd_kernel(*refs, c_rows, t_valid, t_pad, has_s0, ub, uc, conv_done):
    if has_s0:
        (zx_ref, cw_ref, cb_ref, dtb_ref, alog_ref, dsk_ref, nw_ref, ex_ref, s0_ref, buf_ref,
         y_ref, so_ref, nbuf_ref, st_ref, carry_ref) = refs
    else:
        (zx_ref, cw_ref, cb_ref, dtb_ref, alog_ref, dsk_ref, nw_ref, ex_ref,
         y_ref, so_ref, st_ref, carry_ref) = refs
    c = pl.program_id(1)
    nc = pl.num_programs(1)
    hp, ns, ngr = SSD_HEAD_DIM, SSD_STATE, SSD_GROUPS
    di = y_ref.shape[1]
    nh = di // hp
    hpg = nh // ngr
    gw = ngr * ns
    kconv = cw_ref.shape[0]
    pair = 2 * hp
    assert pair == LANES and hpg % 2 == 0

    @pl.when(c == 0)
    def _():
        carry_ref[...] = jnp.zeros_like(carry_ref)
        if has_s0:
            for s in range(ub):
                st_ref[s] = s0_ref[s]
                carry_ref[s, SUBLANES - (kconv - 1):, :] = buf_ref[s]
        else:
            st_ref[...] = jnp.zeros_like(st_ref)

    cdim = cw_ref.shape[1]
    rows = lax.broadcasted_iota(jnp.int32, (c_rows, 1), 0)
    rows8 = rows[:SUBLANES]
    cw = cw_ref[...]
    a_neg = -jnp.exp(alog_ref[...])
    tril = _tril_mask(c_rows)
    dsk = dsk_ref[...]
    nw = nw_ref[...]
    lo = lax.broadcasted_iota(jnp.int32, (c_rows, LANES), 1) < hp
    for s in range(ub):
        for cc in range(uc):
            r0 = (s * uc + cc) * c_rows
            valid = ((c * uc + cc) * c_rows + rows) < t_valid if t_valid < t_pad else None
            _ssd_chunk(zx_ref[r0:r0 + c_rows, :], st_ref.at[s], carry_ref.at[s], y_ref.at[r0:r0 + c_rows, :],
                       cw, cb_ref[...], dtb_ref[...], a_neg, dsk, nw, ex_ref, tril, lo, rows8, valid, conv_done)

    @pl.when(c == nc - 1)
    def _():
        for s in range(ub):
            so_ref[s] = st_ref[s]
            if has_s0:
                last = s * c_rows + t_valid
                nbuf_ref[s] = zx_ref[last - (kconv - 1):last, di:di + cdim]


def _ssd_chunk(zx, st_ref, carry_ref, y_ref, cw, cb, dtb, a_neg, dsk, nw, ex_ref, tril, lo, rows8, valid, conv_done):
    hp, ns, ngr = SSD_HEAD_DIM, SSD_STATE, SSD_GROUPS
    c_rows = zx.shape[0]
    di = y_ref.shape[1]
    nh = di // hp
    hpg = nh // ngr
    gw = ngr * ns
    kconv, cdim = cw.shape
    pair = 2 * hp
    z = zx[:, :di]
    u = zx[:, di:di + cdim]
    if conv_done:
        xbc = u
    else:
        xbc = _silu(_causal_conv(u, carry_ref[...], cw, cb, rows8))
        carry_ref[...] = u[c_rows - SUBLANES:, :]
    dt = jax.nn.softplus(zx[:, di + cdim:] + dtb)
    if valid is not None:
        xbc = jnp.where(valid, xbc, 0.0)
        dt = jnp.where(valid, dt, 0.0)
    xs = xbc[:, :di]
    bm = xbc[:, di:di + gw]
    cm = xbc[:, di + gw:di + 2 * gw]
    acum = _cumsum_rows(dt * a_neg, tril.astype(bf16))

    def head_rows(a):
        if c_rows < LANES:
            a = jnp.concatenate([a, jnp.zeros((LANES - c_rows, LANES), f32)], axis=0)
        return a.T[:, :c_rows]

    acum_t = head_rows(acum)
    dt_t = head_rows(dt)
    a_last = acum[c_rows - 1:c_rows, :]
    xd = xs * _expand_heads(dt * jnp.exp(a_last - acum), ex_ref)
    first = lax.broadcasted_iota(jnp.int32, (pair, ns), 0) < hp

    ys = []
    for g in range(ngr):
        bg = bm[:, g * ns:(g + 1) * ns]
        cg = cm[:, g * ns:(g + 1) * ns]
        cb_mat = _dot_nt(cg, bg)
        gsl = slice(g * hpg * hp, (g + 1) * hpg * hp)
        y_inter = _dot_nt(cg, st_ref[gsl, :])
        decs = []
        for q in range(hpg // 2):
            h1 = g * hpg + 2 * q
            psl = slice(h1 * hp, h1 * hp + pair)
            ms, es = [], []
            for h in (h1, h1 + 1):
                colb = jnp.broadcast_to(acum[:, h:h + 1], (c_rows, LANES))
                seg = colb[:, :c_rows] - acum_t[h:h + 1, :]
                ms.append(cb_mat * jnp.exp(jnp.where(tril, seg, -jnp.inf)) * dt_t[h:h + 1, :])
                es.append(jnp.exp(colb))
            xp = xs[:, psl]
            x_lo = jnp.where(lo, xp, 0.0)
            x_hi = jnp.where(lo, 0.0, xp)
            if (2 * c_rows) % LANES == 0:
                y_intra = _dot(jnp.concatenate(ms, axis=1), jnp.concatenate([x_lo, x_hi], axis=0))
            else:
                y_intra = _dot(ms[0], x_lo) + _dot(ms[1], x_hi)
            y_pair = y_intra + y_inter[:, q * pair:(q + 1) * pair] * jnp.where(lo, es[0], es[1])
            ys.append(y_pair + dsk[:, psl] * xp)
            decs.append(jnp.where(first, jnp.exp(a_last[:, h1:h1 + 1]), jnp.exp(a_last[:, h1 + 1:h1 + 2])))
        st_ref[gsl, :] = st_ref[gsl, :] * jnp.concatenate(decs, axis=0) + _dot_tn(xd[:, gsl], bg)

    y = jnp.concatenate(ys, axis=1) * _silu(z)
    gdim = di // ngr
    for g in range(ngr):
        sl = slice(g * gdim, (g + 1) * gdim)
        yg = y[:, sl]
        y_ref[:, sl] = (yg * lax.rsqrt(jnp.mean(yg * yg, axis=-1, keepdims=True) + RMS_EPS) * nw[:, sl]).astype(y_ref.dtype)


def ssd_call(zx, conv_w, conv_b, dt_bias, a_log, d_skip, norm_w, s0, conv_buf, bsz, t_pad, t_valid, conv_done):
    hp, ns = SSD_HEAD_DIM, SSD_STATE
    conv_dim = conv_w.shape[1]
    di = zx.shape[1] - conv_dim - LANES
    nh = di // hp
    c_rows = min(128, t_pad)
    nc = t_pad // c_rows
    ub, uc = _units_per_step(bsz, nc, 4, 4)
    rows = ub * uc * c_rows
    ncs = nc // uc
    has_s0 = s0 is not None
    assert not (conv_done and has_s0)
    pad_h = lambda a: jnp.pad(a.reshape(1, nh), ((0, 0), (0, LANES - nh)))
    full = lambda shape: pl.BlockSpec(shape, lambda b, c: (0,) * len(shape))
    sel = (jnp.arange(di)[None, :] // hp == jnp.arange(LANES)[:, None]).astype(bf16)
    expand = jnp.concatenate([sel, sel, sel], axis=0)
    in_specs = [pl.BlockSpec((rows, zx.shape[1]), lambda b, c: (b * ncs + c, 0)),
                full(conv_w.shape), full((1, conv_dim)), full((1, LANES)), full((1, LANES)),
                full((1, di)), full((1, di)), full(expand.shape)]
    args = [zx, conv_w, conv_b.reshape(1, conv_dim), pad_h(dt_bias), pad_h(a_log),
            jnp.repeat(d_skip, hp).reshape(1, di), norm_w.reshape(1, di), expand]
    out_specs = [pl.BlockSpec((rows, di), lambda b, c: (b * ncs + c, 0)),
                 pl.BlockSpec((ub, di, ns), lambda b, c: (b, 0, 0))]
    out_shape = [jax.ShapeDtypeStruct((bsz * t_pad, di), bf16), jax.ShapeDtypeStruct((bsz, di, ns), f32)]
    if has_s0:
        assert nc == 1 and t_valid >= conv_w.shape[0] - 1
        in_specs += [pl.BlockSpec((ub, di, ns), lambda b, c: (b, 0, 0)),
                     pl.BlockSpec((ub,) + conv_buf.shape[1:], lambda b, c: (b, 0, 0))]
        args += [s0.reshape(bsz, di, ns), conv_buf]
        out_specs.append(pl.BlockSpec((ub,) + conv_buf.shape[1:], lambda b, c: (b, 0, 0)))
        out_shape.append(jax.ShapeDtypeStruct(conv_buf.shape, f32))
    kern = functools.partial(_ssd_kernel, c_rows=c_rows, t_valid=t_valid, t_pad=t_pad, has_s0=has_s0, ub=ub, uc=uc,
                             conv_done=conv_done)
    outs = pl.pallas_call(
        kern,
        grid=(bsz // ub, ncs),
        in_specs=in_specs,
        out_specs=out_specs,
        out_shape=out_shape,
        scratch_shapes=[pltpu.VMEM((ub, di, ns), f32), pltpu.VMEM((ub, SUBLANES, conv_dim), f32)],
        compiler_params=_cparams(2),
        name="ssd",
    )(*args)
    return outs[0], outs[1].reshape(bsz, nh, hp, ns), (outs[2] if has_s0 else None)


def _trunk(x, c_k, c_v, s_hg, s_ssm, s_sconv, s_fconv, prm):
    bsz, t, d = x.shape
    depth = prm["norm_mix_w"].shape[0]
    decode = c_k is not None
    tp = -(-t // SUBLANES) * SUBLANES
    if tp != t:
        x = jnp.pad(x, ((0, 0), (0, tp - t), (0, 0)))
    m = bsz * tp
    n_heads = d // ATT_HEAD_DIM
    qd = n_heads * ATT_HEAD_DIM
    kvd = ATT_KV_HEADS * ATT_HEAD_DIM
    x2 = x.reshape(m, d)
    xn, xn_w = x2, prm["norm_mix_w"][0]
    nk, nv, nhg, nssm, nsconv, nfconv = [], [], [], [], [], []
    if decode:
        ck_all = jnp.transpose(c_k, (0, 1, 3, 4, 2)).reshape(c_k.shape[0], bsz, kvd, WINDOW)
        cv_all = jnp.transpose(c_v, (0, 1, 3, 4, 2)).reshape(c_v.shape[0], bsz, kvd, WINDOW)
        assert c_k.shape[0] == -(-depth // N_MIXERS)
        kv_stacked = None
    slot = [0] * N_MIXERS
    for i in range(depth):
        kind = i % N_MIXERS
        j = slot[kind]
        slot[kind] += 1
        nw_ffn = prm["norm_ffn_w"][i]
        if kind == 0:
            w_qkv, b_qkv = prm["attn_w_qkv"], prm["attn_b_qkv"][j]
            nq = w_qkv.shape[2]
            if decode:
                qkv = proj_call(xn, xn_w, w_qkv, j, b_qkv, f32, "attn_qkv").reshape(bsz, tp, nq)
                o, *kv_stacked = attn_cache_call(qkv, ck_all, cv_all, j, prm["attn_sinks"][j], n_heads, t, kv_stacked)
                o = o.reshape(m, qd)
            else:
                qkv = proj_call(xn, xn_w, w_qkv, j, b_qkv, bf16, "attn_qkv")
                o = attn_prompt_call(qkv, prm["attn_sinks"][j], bsz, t, n_heads)
                xn_last = xn.reshape(bsz, t, d)[:, t - WINDOW:].reshape(bsz * WINDOW, d)
                kv_last = proj_call(xn_last, xn_w, w_qkv[:, :, qd:], j, b_qkv[qd:], f32,
                                    "attn_kv_tail").reshape(bsz, WINDOW, 2, kvd)
                to4 = lambda a: a.reshape(bsz, WINDOW, ATT_KV_HEADS, ATT_HEAD_DIM)
                nk.append(to4(kv_last[:, :, 0]))
                nv.append(to4(kv_last[:, :, 1]))
            act, w_o, b_o = o, (prm["attn_w_o"], j), prm["attn_b_o"][j]
        elif kind == 1:
            p = proj_call(xn, xn_w, prm["hgrn_w_in"], j, None, f32, "hgrn_in")
            y, s_new = hgrn_call(p, prm["hgrn_lb_logits"], prm["hgrn_norm_w"][j],
                                 None if s_hg is None else s_hg[j], bsz, tp, t, i)
            nhg.append(s_new)
            act, w_o, b_o = y, (prm["hgrn_w_o"], j), None
        else:
            conv_dim = prm["ssd_conv_w"].shape[2]
            di = prm["ssd_w_o"].shape[1]
            kc = prm["ssd_conv_w"].shape[1]
            if s_sconv is None:
                zx, pre = ssd_in_call(xn, prm["ssd_w_in"], j, prm["ssd_conv_w"][j], prm["ssd_conv_b"][j], tp)
                pre = pre.reshape(bsz, tp // min(PROJ_TM, m), SUBLANES, conv_dim)[:, -1, SUBLANES - (kc - 1):]
            else:
                zx = proj_call(xn, xn_w, prm["ssd_w_in"], j, None, f32, "ssd_in")
            y, s_new, conv_new = ssd_call(zx, prm["ssd_conv_w"][j], prm["ssd_conv_b"][j], prm["ssd_dt_bias"][j],
                                prm["ssd_a_log"][j], prm["ssd_d"][j], prm["ssd_norm_w"][j],
                                None if s_ssm is None else s_ssm[j],
                                None if s_sconv is None else s_sconv[j], bsz, tp, t, s_sconv is None)
            nssm.append(s_new)
            nsconv.append(pre if s_sconv is None else conv_new)
            act, w_o, b_o = y, (prm["ssd_w_o"], j), None
        last = i == depth - 1
        next_w = prm["norm_final_w"] if last else prm["norm_mix_w"][i + 1]
        dff = prm["ffn_w_down"].shape[1]
        kf = prm["ffn_conv_w"].shape[1]
        x2, xn, tail = ffn_call(act, w_o, b_o, x2, nw_ffn, (prm["ffn_w_up"], i), prm["ffn_conv_w"][i],
                                prm["ffn_conv_b"][i], (prm["ffn_w_down"], i), next_w, tp,
                                None if s_fconv is None else (s_fconv, i), t, f32 if last else bf16, "ffn")
        xn_w = None
        if s_fconv is not None:
            nfconv.append(tail)
        else:
            tiles_per_seq = tp // tail_tile_rows(m)
            tl = tail.reshape(bsz, tiles_per_seq, SUBLANES, dff)[:, -1]
            nfconv.append(tl[:, SUBLANES - (kf - 1):])
    stack = lambda xs: xs[0][None] if len(xs) == 1 else jnp.stack(xs)
    y = xn.reshape(bsz, tp, d)[:, :t]
    if decode:
        unview = lambda a: jnp.transpose(a.reshape(a.shape[0], bsz, ATT_KV_HEADS, ATT_HEAD_DIM, WINDOW), (0, 1, 4, 2, 3))
        nk, nv = unview(kv_stacked[0]), unview(kv_stacked[1])
    else:
        nk, nv = stack(nk), stack(nv)
    return (y, nk, nv, stack(nhg), stack(nssm), stack(nsconv), stack(nfconv))


def tail_tile_rows(m):
    return min(FFN_TM, m)


def kernel(x_prompt, x_sample, cache_attn_k, cache_attn_v, state_hgrn, state_ssm, state_ssm_conv, state_ffn_conv, norm_mix_w, norm_ffn_w, norm_final_w, attn_w_qkv, attn_b_qkv, attn_sinks, attn_w_o, attn_b_o, hgrn_w_in, hgrn_lb_logits, hgrn_norm_w, hgrn_w_o, ssd_w_in, ssd_conv_w, ssd_conv_b, ssd_dt_bias, ssd_a_log, ssd_d, ssd_norm_w, ssd_w_o, ffn_w_up, ffn_conv_w, ffn_conv_b, ffn_w_down):
    cast = lambda w: w.astype(bf16)
    ssd_in_pad = (-ssd_w_in.shape[2]) % LANES
    prm = dict(norm_mix_w=norm_mix_w, norm_ffn_w=norm_ffn_w, norm_final_w=norm_final_w,
               attn_w_qkv=cast(attn_w_qkv), attn_b_qkv=attn_b_qkv, attn_sinks=attn_sinks,
               attn_w_o=cast(attn_w_o), attn_b_o=attn_b_o,
               hgrn_w_in=cast(hgrn_w_in), hgrn_lb_logits=hgrn_lb_logits, hgrn_norm_w=hgrn_norm_w,
               hgrn_w_o=cast(hgrn_w_o),
               ssd_w_in=jnp.pad(cast(ssd_w_in), ((0, 0), (0, 0), (0, ssd_in_pad))), ssd_conv_w=ssd_conv_w,
               ssd_conv_b=ssd_conv_b, ssd_dt_bias=ssd_dt_bias,
               ssd_a_log=ssd_a_log, ssd_d=ssd_d, ssd_norm_w=ssd_norm_w, ssd_w_o=cast(ssd_w_o),
               ffn_w_up=cast(ffn_w_up), ffn_conv_w=ffn_conv_w, ffn_conv_b=ffn_conv_b, ffn_w_down=cast(ffn_w_down))
    outs_p = _trunk(x_prompt, None, None, None, None, None, None, prm)
    outs_s = _trunk(x_sample, cache_attn_k, cache_attn_v, state_hgrn, state_ssm, state_ssm_conv,
                    state_ffn_conv, prm)
    return (outs_p[0], outs_s[0]) + outs_p[1:] + outs_s[1:]
```

```python
import functools

import jax
import jax.numpy as jnp
from jax import lax
from jax.experimental import pallas as pl
from jax.experimental.pallas import tpu as pltpu

bf16 = jnp.bfloat16
f32 = jnp.float32

RMS_EPS = 1e-6
N_MIXERS = 3
WINDOW = 128
PAST_LEN = 8192
ATT_HEAD_DIM = 64
ATT_KV_HEADS = 4
HG_HEADS = 8
HG_CHUNK_GROUP = 32
SSD_HEAD_DIM = 64
SSD_STATE = 128
SSD_GROUPS = 4

LANES = 128
SUBLANES = 8
VMEM_LIMIT = 56 * 1024 * 1024
PROJ_TM = 512
FFN_TM = 512


def _cparams(n_axes):
    return pltpu.CompilerParams(dimension_semantics=("arbitrary",) * n_axes,
                                vmem_limit_bytes=VMEM_LIMIT)


def _dot(a, b):
    return jnp.dot(a.astype(bf16), b.astype(bf16), preferred_element_type=f32)


def _dot_nt(a, b):
    return lax.dot_general(a.astype(bf16), b.astype(bf16), (((1,), (1,)), ((), ())),
                           preferred_element_type=f32)


def _dot_tn(a, b):
    return lax.dot_general(a.astype(bf16), b.astype(bf16), (((0,), (0,)), ((), ())),
                           preferred_element_type=f32)


def _silu(x):
    return x * jax.nn.sigmoid(x)


def _rms(x, w):
    ms = jnp.mean(x * x, axis=-1, keepdims=True)
    return x * lax.rsqrt(ms + RMS_EPS) * w


def _cumsum_rows(x, tril):
    hi = x.astype(bf16)
    r1 = x - hi.astype(f32)
    mid = r1.astype(bf16)
    lo = (r1 - mid.astype(f32)).astype(bf16)
    d = functools.partial(jnp.dot, preferred_element_type=f32)
    return d(tril, hi) + d(tril, mid) + d(tril, lo)


def _tril_mask(c):
    r = lax.broadcasted_iota(jnp.int32, (c, c), 0)
    s = lax.broadcasted_iota(jnp.int32, (c, c), 1)
    return r >= s


def _resident(shape):
    return pl.BlockSpec(shape, lambda *_: (0,) * len(shape), pipeline_mode=pl.Buffered(1))


def _resident_layer(w, layer):
    return pl.BlockSpec((None,) + w.shape[1:], lambda *_: (layer, 0, 0), pipeline_mode=pl.Buffered(1))


def _proj_kernel(x_ref, nw_ref, w_ref, b_ref, o_ref):
    x = x_ref[...]
    if x.dtype != bf16:
        x = _rms(x, nw_ref[...]).astype(bf16)
    acc = jnp.dot(x, w_ref[...], preferred_element_type=f32)
    o_ref[...] = (acc + b_ref[...]).astype(o_ref.dtype)


def proj_call(x, norm_w, w, layer, bias, out_dtype, name):
    m, k = x.shape
    n = w.shape[2]
    tm = min(PROJ_TM, m)
    if bias is None:
        bias = jnp.zeros((n,), f32)
    if norm_w is None:
        assert x.dtype == bf16
        norm_w = jnp.ones((k,), f32)
    return pl.pallas_call(
        _proj_kernel,
        grid=(m // tm,),
        in_specs=[pl.BlockSpec((tm, k), lambda i: (i, 0)), _resident((1, k)), _resident_layer(w, layer),
                  _resident((1, n))],
        out_specs=pl.BlockSpec((tm, n), lambda i: (i, 0)),
        out_shape=jax.ShapeDtypeStruct((m, n), out_dtype),
        compiler_params=_cparams(1),
        name=name,
    )(x, norm_w.reshape(1, k), w, bias.reshape(1, n))


def _causal_conv(u, prev, cw, cb, rows8):
    kconv = cw.shape[0]
    yc = cb
    for kk in range(kconv - 1):
        back = kconv - 1 - kk
        sh = pltpu.roll(u, back, 0)
        top = sh[:SUBLANES]
        for r0 in range(back):
            top = jnp.where(rows8 == r0, prev[SUBLANES - back + r0:SUBLANES - back + r0 + 1, :], top)
        sh = jnp.concatenate([top, sh[SUBLANES:]], axis=0) if u.shape[0] > SUBLANES else top
        yc = yc + sh * cw[kk:kk + 1, :]
    return yc + u * cw[kconv - 1:kconv, :]


def _ssd_in_kernel(x_ref, w_ref, cw_ref, cb_ref, o_ref, tail_ref, carry_ref, *, tm, tc, t_seq, di):
    i = pl.program_id(0)
    cdim = cw_ref.shape[1]

    @pl.when(((i * tm) & (t_seq - 1)) == 0)
    def _():
        carry_ref[...] = jnp.zeros_like(carry_ref)

    x = x_ref[...]
    rows8 = lax.broadcasted_iota(jnp.int32, (SUBLANES, 1), 0)
    nz = di // tc
    for idx, c0 in enumerate(range(0, cdim, tc)):
        cs = slice(c0, c0 + tc)
        u = jnp.dot(x, w_ref[:, di + c0:di + c0 + tc], preferred_element_type=f32)
        tail_ref[0, :, cs] = u[tm - SUBLANES:, :]
        yc = _causal_conv(u, carry_ref[:, cs], cw_ref[:, cs], cb_ref[:, cs], rows8)
        carry_ref[:, cs] = u[tm - SUBLANES:, :]
        o_ref[:, di + c0:di + c0 + tc] = _silu(yc)
        if idx < nz:
            o_ref[:, idx * tc:(idx + 1) * tc] = jnp.dot(x, w_ref[:, idx * tc:(idx + 1) * tc],
                                                        preferred_element_type=f32)
    o_ref[:, di + cdim:] = jnp.dot(x, w_ref[:, di + cdim:], preferred_element_type=f32)


def ssd_in_call(xn, w, layer, conv_w, conv_b, t_seq):
    m, k = xn.shape
    n = w.shape[2]
    cdim = conv_w.shape[1]
    di = n - cdim - LANES
    tm = min(PROJ_TM, m)
    tc = 2 * LANES
    assert t_seq % tm == 0 and cdim % tc == 0 and di % tc == 0 and di <= cdim and (t_seq & (t_seq - 1)) == 0
    kern = functools.partial(_ssd_in_kernel, tm=tm, tc=tc, t_seq=t_seq, di=di)
    return pl.pallas_call(
        kern,
        grid=(m // tm,),
        in_specs=[pl.BlockSpec((tm, k), lambda i: (i, 0)), _resident_layer(w, layer),
                  _resident(conv_w.shape), _resident((1, cdim))],
        out_specs=[pl.BlockSpec((tm, n), lambda i: (i, 0)), pl.BlockSpec((1, SUBLANES, cdim), lambda i: (i, 0, 0))],
        out_shape=[jax.ShapeDtypeStruct((m, n), f32), jax.ShapeDtypeStruct((m // tm, SUBLANES, cdim), f32)],
        scratch_shapes=[pltpu.VMEM((SUBLANES, cdim), f32)],
        compiler_params=_cparams(1),
        name="ssd_in_conv",
    )(xn, w, conv_w, conv_b.reshape(1, cdim))


def _ffn_kernel(*refs, tm, tf, t_seq, t_valid, has_state):
    if has_state:
        act_ref, wo_ref, bo_ref, x_ref, nw1_ref, wu_ref, cw_ref, cb_ref, wd_ref, nw2_ref, buf_ref = refs[:11]
        xo_ref, xn2_ref, nst_ref, carry_ref = refs[-4:]
    else:
        (act_ref, wo_ref, bo_ref, x_ref, nw1_ref, wu_ref, cw_ref, cb_ref, wd_ref, nw2_ref,
         xo_ref, xn2_ref, tail_ref, carry_ref) = refs
    i = pl.program_id(0)
    dff = wd_ref.shape[0]

    x1 = jnp.dot(act_ref[...].astype(bf16), wo_ref[...], preferred_element_type=f32) + bo_ref[...] + x_ref[...]
    xn = _rms(x1, nw1_ref[...]).astype(bf16)
    rows8 = lax.broadcasted_iota(jnp.int32, (SUBLANES, 1), 0)
    if not has_state:
        @pl.when(((i * tm) & (t_seq - 1)) == 0)
        def _():
            carry_ref[...] = jnp.zeros_like(carry_ref)
    gs = []
    for c0 in range(0, dff, tf):
        cs = slice(c0, c0 + tf)
        a = jnp.dot(xn, wu_ref[:, cs], preferred_element_type=f32)
        b = jnp.dot(xn, wu_ref[:, dff + c0:dff + c0 + tf], preferred_element_type=f32)
        r1 = pltpu.roll(a, 1, 0)
        r2 = pltpu.roll(a, 2, 0)
        if has_state:
            p1s, p2s = [], []
            for s in range(tm // SUBLANES):
                rs = slice(s * SUBLANES, (s + 1) * SUBLANES)
                b0, b1 = buf_ref[s, 0:1, cs], buf_ref[s, 1:2, cs]
                p1s.append(jnp.where(rows8 == 0, b1, r1[rs, :]))
                p2s.append(jnp.where(rows8 == 0, b0, jnp.where(rows8 == 1, b1, r2[rs, :])))
                nst_ref[s, :, cs] = a[s * SUBLANES + t_valid - 2:s * SUBLANES + t_valid, :]
            p1 = jnp.concatenate(p1s, axis=0)
            p2 = jnp.concatenate(p2s, axis=0)
        else:
            tail_ref[0, :, cs] = a[tm - SUBLANES:, :]
            carry = carry_ref[:, cs]
            top1 = jnp.where(rows8 == 0, carry[7:8, :], r1[:SUBLANES])
            top2 = jnp.where(rows8 == 0, carry[6:7, :], jnp.where(rows8 == 1, carry[7:8, :], r2[:SUBLANES]))
            p1 = jnp.concatenate([top1, r1[SUBLANES:]], axis=0)
            p2 = jnp.concatenate([top2, r2[SUBLANES:]], axis=0)
            carry_ref[:, cs] = a[tm - SUBLANES:, :]
        y = cb_ref[:, cs] + p2 * cw_ref[0:1, cs]
        y = y + p1 * cw_ref[1:2, cs]
        y = y + a * cw_ref[2:3, cs]
        gs.append((_silu(y) * b).astype(bf16))
    g = jnp.concatenate(gs, axis=1)
    xo = jnp.dot(g, wd_ref[...], preferred_element_type=f32) + x1
    xo_ref[...] = xo
    xn2_ref[...] = _rms(xo, nw2_ref[...]).astype(xn2_ref.dtype)


def ffn_call(act, w_o, b_o, x, ffn_norm_w, w_up, conv_w, conv_b, w_down, next_w, t_seq, buf, t_valid, out_dtype, name,
             stacked=None):
    m, d = x.shape
    ko = act.shape[1]
    dff = w_down[0].shape[1]
    tf = 2 * LANES
    assert dff % tf == 0 and (t_seq & (t_seq - 1)) == 0
    has_state = buf is not None
    tm = min(FFN_TM // 2 if has_state else FFN_TM, m)
    assert (t_seq == SUBLANES and t_valid >= 2 and conv_w.shape[0] == 3) if has_state else (t_seq % tm == 0)
    if b_o is None:
        b_o = jnp.zeros((d,), f32)
    row = lambda w: pl.BlockSpec((tm, w), lambda i: (i, 0))
    in_specs = [row(ko), _resident_layer(*w_o), _resident((1, d)), row(d), _resident((1, d)),
                _resident_layer(*w_up), _resident((3, dff)), _resident((1, dff)),
                _resident_layer(*w_down), _resident((1, d))]
    args = [act, w_o[0], b_o.reshape(1, d), x, ffn_norm_w.reshape(1, d),
            w_up[0], conv_w, conv_b.reshape(1, dff), w_down[0], next_w.reshape(1, d)]
    if has_state:
        nseq = tm // t_seq
        in_specs.append(pl.BlockSpec((None, nseq, 2, dff), lambda i: (buf[1], i, 0, 0)))
        args.append(buf[0])
        third_spec = pl.BlockSpec((None, nseq, 2, dff), lambda i: (buf[1], i, 0, 0))
        third_shape = jax.ShapeDtypeStruct(buf[0].shape, f32)
        if stacked is not None:
            in_specs.append(pl.BlockSpec(memory_space=pl.ANY))
            args.append(stacked)
    else:
        third_spec = pl.BlockSpec((1, SUBLANES, dff), lambda i: (i, 0, 0))
        third_shape = jax.ShapeDtypeStruct((m // tm, SUBLANES, dff), f32)
    kern = functools.partial(_ffn_kernel, tm=tm, tf=tf, t_seq=t_seq, t_valid=t_valid, has_state=has_state)
    return pl.pallas_call(
        kern,
        grid=(m // tm,),
        in_specs=in_specs,
        out_specs=[row(d), row(d), third_spec],
        out_shape=[jax.ShapeDtypeStruct((m, d), f32), jax.ShapeDtypeStruct((m, d), out_dtype), third_shape],
        scratch_shapes=[pltpu.VMEM((SUBLANES, dff), f32)],
        input_output_aliases={len(args) - 1: 2} if stacked is not None else {},
        compiler_params=_cparams(1),
        name=name,
    )(*args)


def _attn_softmax_pv(s, sink_col, vj):
    m = jnp.maximum(jnp.max(s, axis=-1, keepdims=True), sink_col)
    p = jnp.exp(s - m)
    l = jnp.sum(p, axis=-1, keepdims=True) + jnp.exp(sink_col - m)
    return _dot(p, vj) / l


def _sink_col(sinks_ref, j, group, rows_per_head):
    r = lax.broadcasted_iota(jnp.int32, (group * rows_per_head, 1), 0)
    col = jnp.full((group * rows_per_head, 1), sinks_ref[j * group], f32)
    for p in range(1, group):
        col = jnp.where(r >= p * rows_per_head, sinks_ref[j * group + p], col)
    return col


def _attn_prompt_kernel(sinks_ref, q_ref, kvc_ref, kvp_ref, o_ref, *, n_heads):
    n = pl.program_id(1)
    w, hd, kvh = WINDOW, ATT_HEAD_DIM, ATT_KV_HEADS
    group = n_heads // kvh
    kvw = kvh * hd
    assert group == kvh
    kv = jnp.concatenate([kvp_ref[...], kvc_ref[...]], axis=0)
    k2 = kv[:, :kvw]
    v2 = kv[:, kvw:]
    i = lax.broadcasted_iota(jnp.int32, (w, 2 * w), 0)
    c = lax.broadcasted_iota(jnp.int32, (w, 2 * w), 1)
    band = (c > i) & (c <= i + w) & ((n > 0) | (c >= w))
    mask = jnp.concatenate([band] * group, axis=0)
    slot = lax.broadcasted_iota(jnp.int32, (w, kvw), 1) >> (hd.bit_length() - 1)
    for j in range(kvh):
        qg = q_ref[:, j * kvw:(j + 1) * kvw] * (hd ** -0.5)
        qs = jnp.concatenate([jnp.where(slot == j, pltpu.roll(qg, ((j - p) % kvh) * hd, 1), 0)
                              for p in range(group)], axis=0)
        s = jnp.where(mask, _dot_nt(qs, k2), -jnp.inf)
        of = _attn_softmax_pv(s, _sink_col(sinks_ref, j, group, w), v2).astype(o_ref.dtype)
        og = of[0:w, :]
        for p in range(group):
            piece = of[p * w:(p + 1) * w, :]
            og = jnp.where(slot == p, pltpu.roll(piece, ((p - j) % kvh) * hd, 1), og)
        o_ref[:, j * kvw:(j + 1) * kvw] = og


def attn_prompt_call(qkv, sinks, bsz, t, n_heads):
    w, hd, kvh = WINDOW, ATT_HEAD_DIM, ATT_KV_HEADS
    nb = t // w
    qd = n_heads * hd
    kvd = 2 * kvh * hd
    assert qd % kvd == 0
    kvblk = qd // kvd
    kern = functools.partial(_attn_prompt_kernel, n_heads=n_heads)
    return pl.pallas_call(
        kern,
        grid=(bsz, nb),
        in_specs=[pl.BlockSpec(memory_space=pltpu.SMEM),
                  pl.BlockSpec((w, qd), lambda b, n: (b * nb + n, 0)),
                  pl.BlockSpec((w, kvd), lambda b, n: (b * nb + n, kvblk)),
                  pl.BlockSpec((w, kvd), lambda b, n: (b * nb + jnp.maximum(n - 1, 0), kvblk))],
        out_specs=pl.BlockSpec((w, qd), lambda b, n: (b * nb + n, 0)),
        out_shape=jax.ShapeDtypeStruct((bsz * t, qd), bf16),
        compiler_params=_cparams(2),
        name="attn_prompt",
    )(sinks, qkv, qkv, qkv)


def _shifted_window(cache, new_rows, t_valid):
    tp, w = new_rows.shape[0], cache.shape[1]
    hi = new_rows.astype(bf16)
    r1 = new_rows - hi.astype(f32)
    mid = r1.astype(bf16)
    lo = (r1 - mid.astype(f32)).astype(bf16)
    t = lax.broadcasted_iota(jnp.int32, (tp, w), 0)
    col = lax.broadcasted_iota(jnp.int32, (tp, w), 1)
    place = ((col == w - t_valid + t) & (t < t_valid)).astype(bf16)
    placed = _dot_tn(jnp.concatenate([hi, mid, lo], axis=0), jnp.concatenate([place] * 3, axis=0))
    lane = lax.broadcasted_iota(jnp.int32, cache.shape, 1)
    return jnp.where(lane >= w - t_valid, placed, pltpu.roll(cache, w - t_valid, 1))


def _attn_cache_kernel(*refs, n_heads, t_valid, bt):
    sinks_ref, q_ref, ck_ref, cv_ref = refs[:4]
    o_ref, nk_ref, nv_ref = refs[-3:]
    w, hd, kvh = WINDOW, ATT_HEAD_DIM, ATT_KV_HEADS
    group = n_heads // kvh
    tp = q_ref.shape[1]
    qd = n_heads * hd
    tq = lax.broadcasted_iota(jnp.int32, (tp, w), 0)
    jc = lax.broadcasted_iota(jnp.int32, (tp, w), 1)
    mc = (jc > tq) & (jc - w + PAST_LEN >= 0)
    tq2 = lax.broadcasted_iota(jnp.int32, (tp, tp), 0)
    un = lax.broadcasted_iota(jnp.int32, (tp, tp), 1)
    mn = (un <= tq2) & (un < t_valid)
    mask_c = jnp.concatenate([mc] * n_heads, axis=0)
    mask_n = jnp.concatenate([mn] * n_heads, axis=0)
    kvw = kvh * hd
    assert group == kvh
    slot = lax.broadcasted_iota(jnp.int32, (tp, kvw), 1) >> (hd.bit_length() - 1)
    sink = _sink_col(sinks_ref, 0, n_heads, tp)

    for bi in range(bt):
        row = q_ref[bi]
        ck = ck_ref[bi]
        cv = cv_ref[bi]
        kn = row[:, qd:qd + kvw]
        vn = row[:, qd + kvw:]
        win = _shifted_window(jnp.concatenate([ck, cv], axis=0), row[:, qd:], t_valid)
        nk_ref[bi] = win[:kvw]
        nv_ref[bi] = win[kvw:]
        pieces = []
        for j in range(kvh):
            qg = row[:, j * kvw:(j + 1) * kvw] * (hd ** -0.5)
            for p in range(group):
                pieces.append(jnp.where(slot == j, pltpu.roll(qg, ((j - p) % kvh) * hd, 1), 0.0))
        qs = jnp.concatenate(pieces, axis=0)
        sc = jnp.where(mask_c, _dot(qs, ck), -jnp.inf)
        sn = jnp.where(mask_n, _dot_nt(qs, kn), -jnp.inf)
        m = jnp.maximum(jnp.maximum(jnp.max(sc, axis=-1, keepdims=True),
                                    jnp.max(sn, axis=-1, keepdims=True)), sink)
        pc = jnp.exp(sc - m)
        pn = jnp.exp(sn - m)
        l = jnp.sum(pc, axis=-1, keepdims=True) + jnp.sum(pn, axis=-1, keepdims=True) + jnp.exp(sink - m)
        of = (_dot_nt(pc, cv) + _dot(pn, vn)) / l
        for j in range(kvh):
            og = jnp.zeros((tp, kvw), f32)
            for p in range(group):
                h = j * group + p
                og = og + jnp.where(slot == p, pltpu.roll(of[h * tp:(h + 1) * tp, :], ((p - j) % kvh) * hd, 1), 0.0)
            o_ref[bi, :, j * kvw:(j + 1) * kvw] = og


def attn_cache_call(qkv, cache_k, cache_v, layer, sinks, n_heads, t_valid, stacked):
    bsz, tp, nq = qkv.shape
    w = WINDOW
    kd = cache_k.shape[2]
    qd = n_heads * ATT_HEAD_DIM
    bt = 8
    kern = functools.partial(_attn_cache_kernel, n_heads=n_heads, t_valid=t_valid, bt=bt)
    slab = pl.BlockSpec((None, bt, kd, w), lambda b: (layer, b, 0, 0))
    prev = [] if stacked is None else list(stacked)
    return pl.pallas_call(
        kern,
        grid=(bsz // bt,),
        in_specs=[pl.BlockSpec(memory_space=pltpu.SMEM),
                  pl.BlockSpec((bt, tp, nq), lambda b: (b, 0, 0)), slab, slab]
                 + [pl.BlockSpec(memory_space=pl.ANY)] * len(prev),
        out_specs=[pl.BlockSpec((bt, tp, qd), lambda b: (b, 0, 0)), slab, slab],
        out_shape=[jax.ShapeDtypeStruct((bsz, tp, qd), f32),
                   jax.ShapeDtypeStruct(cache_k.shape, f32),
                   jax.ShapeDtypeStruct(cache_v.shape, f32)],
        input_output_aliases={4: 1, 5: 2} if prev else {},
        compiler_params=_cparams(1),
        name="attn_cache",
    )(sinks, qkv, cache_k, cache_v, *prev)


def _hgrn_chunk(x, st, lb, nw, tril, diag_mask, valid, grp):
    nh = HG_HEADS
    c_rows = x.shape[0]
    dim = x.shape[1] // 4
    dk = dim // nh
    ng = c_rows // grp
    if valid is not None:
        x = jnp.where(valid, x, 0.0)
    q = _silu(x[:, 0:dim])
    fg = lb + (1.0 - lb) * jax.nn.sigmoid(x[:, dim:2 * dim])
    logf = jnp.log(fg)
    k = 1.0 - fg
    v = x[:, 2 * dim:3 * dim]
    gate = x[:, 3 * dim:4 * dim]
    if valid is not None:
        logf = jnp.where(valid, logf, 0.0)
        k = jnp.where(valid, k, 0.0)
    bcum = _cumsum_rows(logf, tril.astype(bf16))

    def rows_of(vals):
        return jnp.concatenate([jnp.broadcast_to(r, (grp, dk)) for r in vals], axis=0)

    ys, st_out = [], []
    for h in range(nh):
        sl = slice(h * dk, (h + 1) * dk)
        b = bcum[:, sl]
        qh, kh, vh = q[:, sl], k[:, sl], v[:, sl]
        zero = jnp.zeros((1, dk), f32)
        r = [zero] + [b[i * grp - 1:i * grp, :] for i in range(1, ng)] + [b[c_rows - 1:c_rows, :]]
        mid = [b[i * grp + grp // 2 - 1:i * grp + grp // 2, :] for i in range(ng)]
        b_last = r[ng]
        r_start = rows_of(r[:ng])
        r_end = rows_of(r[1:])
        r_mid = rows_of(mid)
        qg = qh * jnp.exp(b - r_start)
        kt = kh * jnp.exp(r_end - b)
        qm = qh * jnp.exp(b - r_mid)
        km = kh * jnp.exp(r_mid - b)
        a = jnp.where(diag_mask, _dot_nt(qm, km), 0.0)
        if ng > 1:
            lhs, rhs = [], []
            for j in range(ng - 1):
                lhs.append(jnp.concatenate(
                    [jnp.zeros((grp, dk), f32) if i <= j else qg[i * grp:(i + 1) * grp, :] * jnp.exp(r[i] - r[j + 1])
                     for i in range(ng)], axis=0))
                rhs.append(jnp.concatenate(
                    [kt[i * grp:(i + 1) * grp, :] if i == j else jnp.zeros((grp, dk), f32) for i in range(ng)], axis=0))
            a = a + _dot_nt(jnp.concatenate(lhs, axis=1), jnp.concatenate(rhs, axis=1))
        q_inter = qg * jnp.exp(r_start)
        o = _dot(a, vh) + _dot_nt(q_inter, st[h])
        k_dec = kt * jnp.exp(b_last - r_end)
        st_out.append(st[h] * jnp.exp(b_last) + _dot_tn(vh, k_dec))
        on = o * lax.rsqrt(jnp.mean(o * o, axis=-1, keepdims=True) + RMS_EPS)
        ys.append(on * nw[:, sl] * _silu(gate[:, sl]))
    return ys, st_out


def _hgrn_kernel(*refs, c_rows, grp, t_valid, t_pad, layer, has_s0, ub, uc):
    if has_s0:
        p_ref, lbl_ref, nw_ref, s0_ref, y_ref, so_ref, st_ref = refs
    else:
        p_ref, lbl_ref, nw_ref, y_ref, so_ref, st_ref = refs
    c = pl.program_id(1)
    nc = pl.num_programs(1)
    nh = HG_HEADS
    dim = p_ref.shape[1] // 4
    dk = dim // nh

    @pl.when(c == 0)
    def _():
        if has_s0:
            for s in range(ub):
                for h in range(nh):
                    st_ref[s, h] = s0_ref[s, h].T
        else:
            st_ref[...] = jnp.zeros_like(st_ref)

    lbl = lbl_ref[...]
    e = jnp.exp(lbl - jnp.max(lbl, axis=0, keepdims=True))
    sm = e / jnp.sum(e, axis=0, keepdims=True)
    lb = jnp.zeros((1, dim), f32)
    for r in range(1, layer + 1):
        lb = lb + sm[r:r + 1, :]

    tril = _tril_mask(c_rows)
    rr = lax.broadcasted_iota(jnp.int32, (c_rows, c_rows), 0)
    ss = lax.broadcasted_iota(jnp.int32, (c_rows, c_rows), 1)
    gshift = grp.bit_length() - 1
    diag_mask = tril & ((rr >> gshift) == (ss >> gshift))
    nw = nw_ref[...]
    rows = lax.broadcasted_iota(jnp.int32, (c_rows, 1), 0)

    for s in range(ub):
        st = [st_ref[s, h] for h in range(nh)]
        for cc in range(uc):
            r0 = (s * uc + cc) * c_rows
            valid = ((c * uc + cc) * c_rows + rows) < t_valid if t_valid < t_pad else None
            ys, st = _hgrn_chunk(p_ref[r0:r0 + c_rows, :], st, lb, nw, tril, diag_mask, valid, grp)
            for h in range(nh):
                y_ref[r0:r0 + c_rows, h * dk:(h + 1) * dk] = ys[h].astype(y_ref.dtype)
        for h in range(nh):
            st_ref[s, h] = st[h]

    @pl.when(c == nc - 1)
    def _():
        for s in range(ub):
            for h in range(nh):
                so_ref[s, h] = st_ref[s, h].T


def _units_per_step(bsz, nc, max_seqs, max_chunks):
    if nc == 1:
        return max(u for u in (1, 2, 4, 8) if u <= max_seqs and bsz % u == 0), 1
    return 1, max(u for u in (1, 2, 4, 8) if u <= max_chunks and nc % u == 0)


def hgrn_call(p, lb_logits, norm_w, s0, bsz, t_pad, t_valid, layer):
    dim = p.shape[1] // 4
    nh = HG_HEADS
    dk = dim // nh
    c_rows = min(128, t_pad)
    grp = min(HG_CHUNK_GROUP, c_rows)
    nc = t_pad // c_rows
    ub, uc = _units_per_step(bsz, nc, 8, 8)
    rows = ub * uc * c_rows
    ncs = nc // uc
    has_s0 = s0 is not None
    in_specs = [pl.BlockSpec((rows, 4 * dim), lambda b, c: (b * ncs + c, 0)),
                pl.BlockSpec(lb_logits.shape, lambda b, c: (0, 0)),
                pl.BlockSpec((1, dim), lambda b, c: (0, 0))]
    args = [p, lb_logits, norm_w.reshape(1, dim)]
    if has_s0:
        in_specs.append(pl.BlockSpec((ub, nh, dk, dk), lambda b, c: (b, 0, 0, 0)))
        args.append(s0)
    kern = functools.partial(_hgrn_kernel, c_rows=c_rows, grp=grp, t_valid=t_valid, t_pad=t_pad,
                             layer=layer, has_s0=has_s0, ub=ub, uc=uc)
    return pl.pallas_call(
        kern,
        grid=(bsz // ub, ncs),
        in_specs=in_specs,
        out_specs=[pl.BlockSpec((rows, dim), lambda b, c: (b * ncs + c, 0)),
                   pl.BlockSpec((ub, nh, dk, dk), lambda b, c: (b, 0, 0, 0))],
        out_shape=[jax.ShapeDtypeStruct((bsz * t_pad, dim), bf16),
                   jax.ShapeDtypeStruct((bsz, nh, dk, dk), f32)],
        scratch_shapes=[pltpu.VMEM((ub, nh, dk, dk), f32)],
        compiler_params=_cparams(2),
        name="hgrn",
    )(*args)


def _expand_heads(w, ex_ref):
    hi = w.astype(bf16)
    r1 = w - hi.astype(f32)
    mid = r1.astype(bf16)
    lo = (r1 - mid.astype(f32)).astype(bf16)
    return jnp.dot(jnp.concatenate([hi, mid, lo], axis=1), ex_ref[...], preferred_element_type=f32)


def _ssd_kernel(*refs, c_rows, t_valid, t_pad, has_s0, ub, uc, conv_done):
    if has_s0:
        (zx_ref, cw_ref, cb_ref, dtb_ref, alog_ref, dsk_ref, nw_ref, ex_ref, s0_ref, buf_ref,
         y_ref, so_ref, nbuf_ref, st_ref, carry_ref) = refs
    else:
        (zx_ref, cw_ref, cb_ref, dtb_ref, alog_ref, dsk_ref, nw_ref, ex_ref,
         y_ref, so_ref, st_ref, carry_ref) = refs
    c = pl.program_id(1)
    nc = pl.num_programs(1)
    hp, ns, ngr = SSD_HEAD_DIM, SSD_STATE, SSD_GROUPS
    di = y_ref.shape[1]
    nh = di // hp
    hpg = nh // ngr
    gw = ngr * ns
    kconv = cw_ref.shape[0]
    pair = 2 * hp
    assert pair == LANES and hpg % 2 == 0

    @pl.when(c == 0)
    def _():
        carry_ref[...] = jnp.zeros_like(carry_ref)
        if has_s0:
            for s in range(ub):
                st_ref[s] = s0_ref[s]
                carry_ref[s, SUBLANES - (kconv - 1):, :] = buf_ref[s]
        else:
            st_ref[...] = jnp.zeros_like(st_ref)

    cdim = cw_ref.shape[1]
    rows = lax.broadcasted_iota(jnp.int32, (c_rows, 1), 0)
    rows8 = rows[:SUBLANES]
    cw = cw_ref[...]
    a_neg = -jnp.exp(alog_ref[...])
    tril = _tril_mask(c_rows)
    dsk = dsk_ref[...]
    nw = nw_ref[...]
    lo = lax.broadcasted_iota(jnp.int32, (c_rows, LANES), 1) < hp
    for s in range(ub):
        for cc in range(uc):
            r0 = (s * uc + cc) * c_rows
            valid = ((c * uc + cc) * c_rows + rows) < t_valid if t_valid < t_pad else None
            _ssd_chunk(zx_ref[r0:r0 + c_rows, :], st_ref.at[s], carry_ref.at[s], y_ref.at[r0:r0 + c_rows, :],
                       cw, cb_ref[...], dtb_ref[...], a_neg, dsk, nw, ex_ref, tril, lo, rows8, valid, conv_done)

    @pl.when(c == nc - 1)
    def _():
        for s in range(ub):
            so_ref[s] = st_ref[s]
            if has_s0:
                last = s * c_rows + t_valid
                nbuf_ref[s] = zx_ref[last - (kconv - 1):last, di:di + cdim]


def _ssd_chunk(zx, st_ref, carry_ref, y_ref, cw, cb, dtb, a_neg, dsk, nw, ex_ref, tril, lo, rows8, valid, conv_done):
    hp, ns, ngr = SSD_HEAD_DIM, SSD_STATE, SSD_GROUPS
    c_rows = zx.shape[0]
    di = y_ref.shape[1]
    nh = di // hp
    hpg = nh // ngr
    gw = ngr * ns
    kconv, cdim = cw.shape
    pair = 2 * hp
    z = zx[:, :di]
    u = zx[:, di:di + cdim]
    if conv_done:
        xbc = u
    else:
        xbc = _silu(_causal_conv(u, carry_ref[...], cw, cb, rows8))
        carry_ref[...] = u[c_rows - SUBLANES:, :]
    dt = jax.nn.softplus(zx[:, di + cdim:] + dtb)
    if valid is not None:
        xbc = jnp.where(valid, xbc, 0.0)
        dt = jnp.where(valid, dt, 0.0)
    xs = xbc[:, :di]
    bm = xbc[:, di:di + gw]
    cm = xbc[:, di + gw:di + 2 * gw]
    acum = _cumsum_rows(dt * a_neg, tril.astype(bf16))

    def head_rows(a):
        if c_rows < LANES:
            a = jnp.concatenate([a, jnp.zeros((LANES - c_rows, LANES), f32)], axis=0)
        return a.T[:, :c_rows]

    acum_t = head_rows(acum)
    dt_t = head_rows(dt)
    a_last = acum[c_rows - 1:c_rows, :]
    xd = xs * _expand_heads(dt * jnp.exp(a_last - acum), ex_ref)
    first = lax.broadcasted_iota(jnp.int32, (pair, ns), 0) < hp

    ys = []
    for g in range(ngr):
        bg = bm[:, g * ns:(g + 1) * ns]
        cg = cm[:, g * ns:(g + 1) * ns]
        cb_mat = _dot_nt(cg, bg)
        gsl = slice(g * hpg * hp, (g + 1) * hpg * hp)
        y_inter = _dot_nt(cg, st_ref[gsl, :])
        decs = []
        for q in range(hpg // 2):
            h1 = g * hpg + 2 * q
            psl = slice(h1 * hp, h1 * hp + pair)
            ms, es = [], []
            for h in (h1, h1 + 1):
                colb = jnp.broadcast_to(acum[:, h:h + 1], (c_rows, LANES))
                seg = colb[:, :c_rows] - acum_t[h:h + 1, :]
                ms.append(cb_mat * jnp.exp(jnp.where(tril, seg, -jnp.inf)) * dt_t[h:h + 1, :])
                es.append(jnp.exp(colb))
            xp = xs[:, psl]
            x_lo = jnp.where(lo, xp, 0.0)
            x_hi = jnp.where(lo, 0.0, xp)
            if (2 * c_rows) % LANES == 0:
                y_intra = _dot(jnp.concatenate(ms, axis=1), jnp.concatenate([x_lo, x_hi], axis=0))
            else:
                y_intra = _dot(ms[0], x_lo) + _dot(ms[1], x_hi)
            y_pair = y_intra + y_inter[:, q * pair:(q + 1) * pair] * jnp.where(lo, es[0], es[1])
            ys.append(y_pair + dsk[:, psl] * xp)
            decs.append(jnp.where(first, jnp.exp(a_last[:, h1:h1 + 1]), jnp.exp(a_last[:, h1 + 1:h1 + 2])))
        st_ref[gsl, :] = st_ref[gsl, :] * jnp.concatenate(decs, axis=0) + _dot_tn(xd[:, gsl], bg)

    y = jnp.concatenate(ys, axis=1) * _silu(z)
    gdim = di // ngr
    for g in range(ngr):
        sl = slice(g * gdim, (g + 1) * gdim)
        yg = y[:, sl]
        y_ref[:, sl] = (yg * lax.rsqrt(jnp.mean(yg * yg, axis=-1, keepdims=True) + RMS_EPS) * nw[:, sl]).astype(y_ref.dtype)


def ssd_call(zx, conv_w, conv_b, dt_bias, a_log, d_skip, norm_w, s0, conv_buf, bsz, t_pad, t_valid, conv_done):
    hp, ns = SSD_HEAD_DIM, SSD_STATE
    conv_dim = conv_w.shape[1]
    di = zx.shape[1] - conv_dim - LANES
    nh = di // hp
    c_rows = min(128, t_pad)
    nc = t_pad // c_rows
    ub, uc = _units_per_step(bsz, nc, 4, 4)
    rows = ub * uc * c_rows
    ncs = nc // uc
    has_s0 = s0 is not None
    assert not (conv_done and has_s0)
    pad_h = lambda a: jnp.pad(a.reshape(1, nh), ((0, 0), (0, LANES - nh)))
    full = lambda shape: pl.BlockSpec(shape, lambda b, c: (0,) * len(shape))
    sel = (jnp.arange(di)[None, :] // hp == jnp.arange(LANES)[:, None]).astype(bf16)
    expand = jnp.concatenate([sel, sel, sel], axis=0)
    in_specs = [pl.BlockSpec((rows, zx.shape[1]), lambda b, c: (b * ncs + c, 0)),
                full(conv_w.shape), full((1, conv_dim)), full((1, LANES)), full((1, LANES)),
                full((1, di)), full((1, di)), full(expand.shape)]
    args = [zx, conv_w, conv_b.reshape(1, conv_dim), pad_h(dt_bias), pad_h(a_log),
            jnp.repeat(d_skip, hp).reshape(1, di), norm_w.reshape(1, di), expand]
    out_specs = [pl.BlockSpec((rows, di), lambda b, c: (b * ncs + c, 0)),
                 pl.BlockSpec((ub, di, ns), lambda b, c: (b, 0, 0))]
    out_shape = [jax.ShapeDtypeStruct((bsz * t_pad, di), bf16), jax.ShapeDtypeStruct((bsz, di, ns), f32)]
    if has_s0:
        assert nc == 1 and t_valid >= conv_w.shape[0] - 1
        in_specs += [pl.BlockSpec((ub, di, ns), lambda b, c: (b, 0, 0)),
                     pl.BlockSpec((ub,) + conv_buf.shape[1:], lambda b, c: (b, 0, 0))]
        args += [s0.reshape(bsz, di, ns), conv_buf]
        out_specs.append(pl.BlockSpec((ub,) + conv_buf.shape[1:], lambda b, c: (b, 0, 0)))
        out_shape.append(jax.ShapeDtypeStruct(conv_buf.shape, f32))
    kern = functools.partial(_ssd_kernel, c_rows=c_rows, t_valid=t_valid, t_pad=t_pad, has_s0=has_s0, ub=ub, uc=uc,
                             conv_done=conv_done)
    outs = pl.pallas_call(
        kern,
        grid=(bsz // ub, ncs),
        in_specs=in_specs,
        out_specs=out_specs,
        out_shape=out_shape,
        scratch_shapes=[pltpu.VMEM((ub, di, ns), f32), pltpu.VMEM((ub, SUBLANES, conv_dim), f32)],
        compiler_params=_cparams(2),
        name="ssd",
    )(*args)
    return outs[0], outs[1].reshape(bsz, nh, hp, ns), (outs[2] if has_s0 else None)


def _trunk(x, c_k, c_v, s_hg, s_ssm, s_sconv, s_fconv, prm):
    bsz, t, d = x.shape
    depth = prm["norm_mix_w"].shape[0]
    decode = c_k is not None
    tp = -(-t // SUBLANES) * SUBLANES
    if tp != t:
        x = jnp.pad(x, ((0, 0), (0, tp - t), (0, 0)))
    m = bsz * tp
    n_heads = d // ATT_HEAD_DIM
    qd = n_heads * ATT_HEAD_DIM
    kvd = ATT_KV_HEADS * ATT_HEAD_DIM
    x2 = x.reshape(m, d)
    xn, xn_w = x2, prm["norm_mix_w"][0]
    nk, nv, nhg, nssm, nsconv, nfconv = [], [], [], [], [], []
    if decode:
        ck_all = jnp.transpose(c_k, (0, 1, 3, 4, 2)).reshape(c_k.shape[0], bsz, kvd, WINDOW)
        cv_all = jnp.transpose(c_v, (0, 1, 3, 4, 2)).reshape(c_v.shape[0], bsz, kvd, WINDOW)
        assert c_k.shape[0] == -(-depth // N_MIXERS)
        kv_stacked = None
    assert s_fconv is None or s_fconv.shape[0] == depth
    nf_stacked = None
    slot = [0] * N_MIXERS
    for i in range(depth):
        kind = i % N_MIXERS
        j = slot[kind]
        slot[kind] += 1
        nw_ffn = prm["norm_ffn_w"][i]
        if kind == 0:
            w_qkv, b_qkv = prm["attn_w_qkv"], prm["attn_b_qkv"][j]
            nq = w_qkv.shape[2]
            if decode:
                qkv = proj_call(xn, xn_w, w_qkv, j, b_qkv, f32, "attn_qkv").reshape(bsz, tp, nq)
                o, *kv_stacked = attn_cache_call(qkv, ck_all, cv_all, j, prm["attn_sinks"][j], n_heads, t, kv_stacked)
                o = o.reshape(m, qd)
            else:
                qkv = proj_call(xn, xn_w, w_qkv, j, b_qkv, bf16, "attn_qkv")
                o = attn_prompt_call(qkv, prm["attn_sinks"][j], bsz, t, n_heads)
                xn_last = xn.reshape(bsz, t, d)[:, t - WINDOW:].reshape(bsz * WINDOW, d)
                kv_last = proj_call(xn_last, xn_w, w_qkv[:, :, qd:], j, b_qkv[qd:], f32,
                                    "attn_kv_tail").reshape(bsz, WINDOW, 2, kvd)
                to4 = lambda a: a.reshape(bsz, WINDOW, ATT_KV_HEADS, ATT_HEAD_DIM)
                nk.append(to4(kv_last[:, :, 0]))
                nv.append(to4(kv_last[:, :, 1]))
            act, w_o, b_o = o, (prm["attn_w_o"], j), prm["attn_b_o"][j]
        elif kind == 1:
            p = proj_call(xn, xn_w, prm["hgrn_w_in"], j, None, f32, "hgrn_in")
            y, s_new = hgrn_call(p, prm["hgrn_lb_logits"], prm["hgrn_norm_w"][j],
                                 None if s_hg is None else s_hg[j], bsz, tp, t, i)
            nhg.append(s_new)
            act, w_o, b_o = y, (prm["hgrn_w_o"], j), None
        else:
            conv_dim = prm["ssd_conv_w"].shape[2]
            di = prm["ssd_w_o"].shape[1]
            kc = prm["ssd_conv_w"].shape[1]
            if s_sconv is None:
                zx, pre = ssd_in_call(xn, prm["ssd_w_in"], j, prm["ssd_conv_w"][j], prm["ssd_conv_b"][j], tp)
                pre = pre.reshape(bsz, tp // min(PROJ_TM, m), SUBLANES, conv_dim)[:, -1, SUBLANES - (kc - 1):]
            else:
                zx = proj_call(xn, xn_w, prm["ssd_w_in"], j, None, f32, "ssd_in")
            y, s_new, conv_new = ssd_call(zx, prm["ssd_conv_w"][j], prm["ssd_conv_b"][j], prm["ssd_dt_bias"][j],
                                prm["ssd_a_log"][j], prm["ssd_d"][j], prm["ssd_norm_w"][j],
                                None if s_ssm is None else s_ssm[j],
                                None if s_sconv is None else s_sconv[j], bsz, tp, t, s_sconv is None)
            nssm.append(s_new)
            nsconv.append(pre if s_sconv is None else conv_new)
            act, w_o, b_o = y, (prm["ssd_w_o"], j), None
        last = i == depth - 1
        next_w = prm["norm_final_w"] if last else prm["norm_mix_w"][i + 1]
        dff = prm["ffn_w_down"].shape[1]
        kf = prm["ffn_conv_w"].shape[1]
        x2, xn, tail = ffn_call(act, w_o, b_o, x2, nw_ffn, (prm["ffn_w_up"], i), prm["ffn_conv_w"][i],
                                prm["ffn_conv_b"][i], (prm["ffn_w_down"], i), next_w, tp,
                                None if s_fconv is None else (s_fconv, i), t, f32 if last else bf16, "ffn", nf_stacked)
        xn_w = None
        if s_fconv is not None:
            nf_stacked = tail
        else:
            tiles_per_seq = tp // tail_tile_rows(m)
            tl = tail.reshape(bsz, tiles_per_seq, SUBLANES, dff)[:, -1]
            nfconv.append(tl[:, SUBLANES - (kf - 1):])
    stack = lambda xs: xs[0][None] if len(xs) == 1 else jnp.stack(xs)
    y = xn.reshape(bsz, tp, d)[:, :t]
    if decode:
        unview = lambda a: jnp.transpose(a.reshape(a.shape[0], bsz, ATT_KV_HEADS, ATT_HEAD_DIM, WINDOW), (0, 1, 4, 2, 3))
        nk, nv = unview(kv_stacked[0]), unview(kv_stacked[1])
    else:
        nk, nv = stack(nk), stack(nv)
    return (y, nk, nv, stack(nhg), stack(nssm), stack(nsconv), stack(nfconv) if s_fconv is None else nf_stacked)


def tail_tile_rows(m):
    return min(FFN_TM, m)


def kernel(x_prompt, x_sample, cache_attn_k, cache_attn_v, state_hgrn, state_ssm, state_ssm_conv, state_ffn_conv, norm_mix_w, norm_ffn_w, norm_final_w, attn_w_qkv, attn_b_qkv, attn_sinks, attn_w_o, attn_b_o, hgrn_w_in, hgrn_lb_logits, hgrn_norm_w, hgrn_w_o, ssd_w_in, ssd_conv_w, ssd_conv_b, ssd_dt_bias, ssd_a_log, ssd_d, ssd_norm_w, ssd_w_o, ffn_w_up, ffn_conv_w, ffn_conv_b, ffn_w_down):
    cast = lambda w: w.astype(bf16)
    ssd_in_pad = (-ssd_w_in.shape[2]) % LANES
    prm = dict(norm_mix_w=norm_mix_w, norm_ffn_w=norm_ffn_w, norm_final_w=norm_final_w,
               attn_w_qkv=cast(attn_w_qkv), attn_b_qkv=attn_b_qkv, attn_sinks=attn_sinks,
               attn_w_o=cast(attn_w_o), attn_b_o=attn_b_o,
               hgrn_w_in=cast(hgrn_w_in), hgrn_lb_logits=hgrn_lb_logits, hgrn_norm_w=hgrn_norm_w,
               hgrn_w_o=cast(hgrn_w_o),
               ssd_w_in=jnp.pad(cast(ssd_w_in), ((0, 0), (0, 0), (0, ssd_in_pad))), ssd_conv_w=ssd_conv_w,
               ssd_conv_b=ssd_conv_b, ssd_dt_bias=ssd_dt_bias,
               ssd_a_log=ssd_a_log, ssd_d=ssd_d, ssd_norm_w=ssd_norm_w, ssd_w_o=cast(ssd_w_o),
               ffn_w_up=cast(ffn_w_up), ffn_conv_w=ffn_conv_w, ffn_conv_b=ffn_conv_b, ffn_w_down=cast(ffn_w_down))
    outs_p = _trunk(x_prompt, None, None, None, None, None, None, prm)
    outs_s = _trunk(x_sample, cache_attn_k, cache_attn_v, state_hgrn, state_ssm, state_ssm_conv,
                    state_ffn_conv, prm)
    return (outs_p[0], outs_s[0]) + outs_p[1:] + outs_s[1:]
```

```python
import functools

import jax
import jax.numpy as jnp
from jax import lax
from jax.experimental import pallas as pl
from jax.experimental.pallas import tpu as pltpu

bf16 = jnp.bfloat16
f32 = jnp.float32

RMS_EPS = 1e-6
N_MIXERS = 3
WINDOW = 128
PAST_LEN = 8192
ATT_HEAD_DIM = 64
ATT_KV_HEADS = 4
HG_HEADS = 8
HG_CHUNK_GROUP = 32
SSD_HEAD_DIM = 64
SSD_STATE = 128
SSD_GROUPS = 4

LANES = 128
SUBLANES = 8
VMEM_LIMIT = 56 * 1024 * 1024
PROJ_TM = 512
FFN_TM = 512


def _cparams(n_axes):
    return pltpu.CompilerParams(dimension_semantics=("arbitrary",) * n_axes,
                                vmem_limit_bytes=VMEM_LIMIT)


def _dot(a, b):
    return jnp.dot(a.astype(bf16), b.astype(bf16), preferred_element_type=f32)


def _dot_nt(a, b):
    return lax.dot_general(a.astype(bf16), b.astype(bf16), (((1,), (1,)), ((), ())),
                           preferred_element_type=f32)


def _dot_tn(a, b):
    return lax.dot_general(a.astype(bf16), b.astype(bf16), (((0,), (0,)), ((), ())),
                           preferred_element_type=f32)


def _silu(x):
    return x * jax.nn.sigmoid(x)


def _rms(x, w):
    ms = jnp.mean(x * x, axis=-1, keepdims=True)
    return x * lax.rsqrt(ms + RMS_EPS) * w


def _cumsum_rows(x, tril):
    hi = x.astype(bf16)
    r1 = x - hi.astype(f32)
    mid = r1.astype(bf16)
    lo = (r1 - mid.astype(f32)).astype(bf16)
    d = functools.partial(jnp.dot, preferred_element_type=f32)
    return d(tril, hi) + d(tril, mid) + d(tril, lo)


def _tril_mask(c):
    r = lax.broadcasted_iota(jnp.int32, (c, c), 0)
    s = lax.broadcasted_iota(jnp.int32, (c, c), 1)
    return r >= s


def _resident(shape):
    return pl.BlockSpec(shape, lambda *_: (0,) * len(shape), pipeline_mode=pl.Buffered(1))


def _resident_layer(w, layer):
    return pl.BlockSpec((None,) + w.shape[1:], lambda *_: (layer, 0, 0), pipeline_mode=pl.Buffered(1))


def _proj_kernel(x_ref, nw_ref, w_ref, b_ref, o_ref):
    x = x_ref[...]
    if x.dtype != bf16:
        x = _rms(x, nw_ref[...]).astype(bf16)
    acc = jnp.dot(x, w_ref[...], preferred_element_type=f32)
    o_ref[...] = (acc + b_ref[...]).astype(o_ref.dtype)


def proj_call(x, norm_w, w, layer, bias, out_dtype, name):
    m, k = x.shape
    n = w.shape[2]
    tm = min(PROJ_TM, m)
    if bias is None:
        bias = jnp.zeros((n,), f32)
    if norm_w is None:
        assert x.dtype == bf16
        norm_w = jnp.ones((k,), f32)
    return pl.pallas_call(
        _proj_kernel,
        grid=(m // tm,),
        in_specs=[pl.BlockSpec((tm, k), lambda i: (i, 0)), _resident((1, k)), _resident_layer(w, layer),
                  _resident((1, n))],
        out_specs=pl.BlockSpec((tm, n), lambda i: (i, 0)),
        out_shape=jax.ShapeDtypeStruct((m, n), out_dtype),
        compiler_params=_cparams(1),
        name=name,
    )(x, norm_w.reshape(1, k), w, bias.reshape(1, n))


def _causal_conv(u, prev, cw, cb, rows8):
    kconv = cw.shape[0]
    yc = cb
    for kk in range(kconv - 1):
        back = kconv - 1 - kk
        sh = pltpu.roll(u, back, 0)
        top = sh[:SUBLANES]
        for r0 in range(back):
            top = jnp.where(rows8 == r0, prev[SUBLANES - back + r0:SUBLANES - back + r0 + 1, :], top)
        sh = jnp.concatenate([top, sh[SUBLANES:]], axis=0) if u.shape[0] > SUBLANES else top
        yc = yc + sh * cw[kk:kk + 1, :]
    return yc + u * cw[kconv - 1:kconv, :]


def _ssd_in_kernel(x_ref, w_ref, cw_ref, cb_ref, o_ref, tail_ref, carry_ref, *, tm, tc, t_seq, di):
    i = pl.program_id(0)
    cdim = cw_ref.shape[1]

    @pl.when(((i * tm) & (t_seq - 1)) == 0)
    def _():
        carry_ref[...] = jnp.zeros_like(carry_ref)

    x = x_ref[...]
    rows8 = lax.broadcasted_iota(jnp.int32, (SUBLANES, 1), 0)
    nz = di // tc
    for idx, c0 in enumerate(range(0, cdim, tc)):
        cs = slice(c0, c0 + tc)
        u = jnp.dot(x, w_ref[:, di + c0:di + c0 + tc], preferred_element_type=f32)
        tail_ref[0, :, cs] = u[tm - SUBLANES:, :]
        yc = _causal_conv(u, carry_ref[:, cs], cw_ref[:, cs], cb_ref[:, cs], rows8)
        carry_ref[:, cs] = u[tm - SUBLANES:, :]
        o_ref[:, di + c0:di + c0 + tc] = _silu(yc)
        if idx < nz:
            o_ref[:, idx * tc:(idx + 1) * tc] = jnp.dot(x, w_ref[:, idx * tc:(idx + 1) * tc],
                                                        preferred_element_type=f32)
    o_ref[:, di + cdim:] = jnp.dot(x, w_ref[:, di + cdim:], preferred_element_type=f32)


def ssd_in_call(xn, w, layer, conv_w, conv_b, t_seq):
    m, k = xn.shape
    n = w.shape[2]
    cdim = conv_w.shape[1]
    di = n - cdim - LANES
    tm = min(PROJ_TM, m)
    tc = 2 * LANES
    assert t_seq % tm == 0 and cdim % tc == 0 and di % tc == 0 and di <= cdim and (t_seq & (t_seq - 1)) == 0
    kern = functools.partial(_ssd_in_kernel, tm=tm, tc=tc, t_seq=t_seq, di=di)
    return pl.pallas_call(
        kern,
        grid=(m // tm,),
        in_specs=[pl.BlockSpec((tm, k), lambda i: (i, 0)), _resident_layer(w, layer),
                  _resident(conv_w.shape), _resident((1, cdim))],
        out_specs=[pl.BlockSpec((tm, n), lambda i: (i, 0)), pl.BlockSpec((1, SUBLANES, cdim), lambda i: (i, 0, 0))],
        out_shape=[jax.ShapeDtypeStruct((m, n), f32), jax.ShapeDtypeStruct((m // tm, SUBLANES, cdim), f32)],
        scratch_shapes=[pltpu.VMEM((SUBLANES, cdim), f32)],
        compiler_params=_cparams(1),
        name="ssd_in_conv",
    )(xn, w, conv_w, conv_b.reshape(1, cdim))


def _ffn_kernel(*refs, tm, tf, t_seq, t_valid, has_state):
    if has_state:
        act_ref, wo_ref, bo_ref, x_ref, nw1_ref, wu_ref, cw_ref, cb_ref, wd_ref, nw2_ref, buf_ref = refs[:11]
        xo_ref, xn2_ref, nst_ref, carry_ref = refs[-4:]
    else:
        (act_ref, wo_ref, bo_ref, x_ref, nw1_ref, wu_ref, cw_ref, cb_ref, wd_ref, nw2_ref,
         xo_ref, xn2_ref, tail_ref, carry_ref) = refs
    i = pl.program_id(0)
    dff = wd_ref.shape[0]

    x1 = jnp.dot(act_ref[...].astype(bf16), wo_ref[...], preferred_element_type=f32) + bo_ref[...] + x_ref[...]
    xn = _rms(x1, nw1_ref[...]).astype(bf16)
    rows8 = lax.broadcasted_iota(jnp.int32, (SUBLANES, 1), 0)
    if not has_state:
        @pl.when(((i * tm) & (t_seq - 1)) == 0)
        def _():
            carry_ref[...] = jnp.zeros_like(carry_ref)
    gs = []
    for c0 in range(0, dff, tf):
        cs = slice(c0, c0 + tf)
        a = jnp.dot(xn, wu_ref[:, cs], preferred_element_type=f32)
        b = jnp.dot(xn, wu_ref[:, dff + c0:dff + c0 + tf], preferred_element_type=f32)
        r1 = pltpu.roll(a, 1, 0)
        r2 = pltpu.roll(a, 2, 0)
        if has_state:
            p1s, p2s = [], []
            for s in range(tm // SUBLANES):
                rs = slice(s * SUBLANES, (s + 1) * SUBLANES)
                b0, b1 = buf_ref[s, 0:1, cs], buf_ref[s, 1:2, cs]
                p1s.append(jnp.where(rows8 == 0, b1, r1[rs, :]))
                p2s.append(jnp.where(rows8 == 0, b0, jnp.where(rows8 == 1, b1, r2[rs, :])))
                nst_ref[s, :, cs] = a[s * SUBLANES + t_valid - 2:s * SUBLANES + t_valid, :]
            p1 = jnp.concatenate(p1s, axis=0)
            p2 = jnp.concatenate(p2s, axis=0)
        else:
            tail_ref[0, :, cs] = a[tm - SUBLANES:, :]
            carry = carry_ref[:, cs]
            top1 = jnp.where(rows8 == 0, carry[7:8, :], r1[:SUBLANES])
            top2 = jnp.where(rows8 == 0, carry[6:7, :], jnp.where(rows8 == 1, carry[7:8, :], r2[:SUBLANES]))
            p1 = jnp.concatenate([top1, r1[SUBLANES:]], axis=0)
            p2 = jnp.concatenate([top2, r2[SUBLANES:]], axis=0)
            carry_ref[:, cs] = a[tm - SUBLANES:, :]
        y = cb_ref[:, cs] + p2 * cw_ref[0:1, cs]
        y = y + p1 * cw_ref[1:2, cs]
        y = y + a * cw_ref[2:3, cs]
        gs.append((_silu(y) * b).astype(bf16))
    g = jnp.concatenate(gs, axis=1)
    xo = jnp.dot(g, wd_ref[...], preferred_element_type=f32) + x1
    xo_ref[...] = xo
    xn2_ref[...] = _rms(xo, nw2_ref[...]).astype(xn2_ref.dtype)


def ffn_call(act, w_o, b_o, x, ffn_norm_w, w_up, conv_w, conv_b, w_down, next_w, t_seq, buf, t_valid, out_dtype, name,
             stacked=None):
    m, d = x.shape
    ko = act.shape[1]
    dff = w_down[0].shape[1]
    tf = 2 * LANES
    assert dff % tf == 0 and (t_seq & (t_seq - 1)) == 0
    has_state = buf is not None
    tm = min(FFN_TM // 2 if has_state else FFN_TM, m)
    assert (t_seq == SUBLANES and t_valid >= 2 and conv_w.shape[0] == 3) if has_state else (t_seq % tm == 0)
    if b_o is None:
        b_o = jnp.zeros((d,), f32)
    row = lambda w: pl.BlockSpec((tm, w), lambda i: (i, 0))
    in_specs = [row(ko), _resident_layer(*w_o), _resident((1, d)), row(d), _resident((1, d)),
                _resident_layer(*w_up), _resident((3, dff)), _resident((1, dff)),
                _resident_layer(*w_down), _resident((1, d))]
    args = [act, w_o[0], b_o.reshape(1, d), x, ffn_norm_w.reshape(1, d),
            w_up[0], conv_w, conv_b.reshape(1, dff), w_down[0], next_w.reshape(1, d)]
    if has_state:
        nseq = tm // t_seq
        in_specs.append(pl.BlockSpec((None, nseq, 2, dff), lambda i: (buf[1], i, 0, 0)))
        args.append(buf[0])
        third_spec = pl.BlockSpec((None, nseq, 2, dff), lambda i: (buf[1], i, 0, 0))
        third_shape = jax.ShapeDtypeStruct(buf[0].shape, f32)
        if stacked is not None:
            in_specs.append(pl.BlockSpec(memory_space=pl.ANY))
            args.append(stacked)
    else:
        third_spec = pl.BlockSpec((1, SUBLANES, dff), lambda i: (i, 0, 0))
        third_shape = jax.ShapeDtypeStruct((m // tm, SUBLANES, dff), f32)
    kern = functools.partial(_ffn_kernel, tm=tm, tf=tf, t_seq=t_seq, t_valid=t_valid, has_state=has_state)
    return pl.pallas_call(
        kern,
        grid=(m // tm,),
        in_specs=in_specs,
        out_specs=[row(d), row(d), third_spec],
        out_shape=[jax.ShapeDtypeStruct((m, d), f32), jax.ShapeDtypeStruct((m, d), out_dtype), third_shape],
        scratch_shapes=[pltpu.VMEM((SUBLANES, dff), f32)],
        input_output_aliases={len(args) - 1: 2} if stacked is not None else {},
        compiler_params=_cparams(1),
        name=name,
    )(*args)


def _attn_softmax_pv(s, sink_col, vj):
    m = jnp.maximum(jnp.max(s, axis=-1, keepdims=True), sink_col)
    p = jnp.exp(s - m)
    l = jnp.sum(p, axis=-1, keepdims=True) + jnp.exp(sink_col - m)
    return _dot(p, vj) / l


def _sink_col(sinks_ref, j, group, rows_per_head):
    r = lax.broadcasted_iota(jnp.int32, (group * rows_per_head, 1), 0)
    col = jnp.full((group * rows_per_head, 1), sinks_ref[j * group], f32)
    for p in range(1, group):
        col = jnp.where(r >= p * rows_per_head, sinks_ref[j * group + p], col)
    return col


def _attn_prompt_kernel(sinks_ref, q_ref, kvc_ref, kvp_ref, o_ref, *, n_heads):
    n = pl.program_id(1)
    w, hd, kvh = WINDOW, ATT_HEAD_DIM, ATT_KV_HEADS
    group = n_heads // kvh
    kvw = kvh * hd
    assert group == kvh
    kv = jnp.concatenate([kvp_ref[...], kvc_ref[...]], axis=0)
    k2 = kv[:, :kvw]
    v2 = kv[:, kvw:]
    i = lax.broadcasted_iota(jnp.int32, (w, 2 * w), 0)
    c = lax.broadcasted_iota(jnp.int32, (w, 2 * w), 1)
    band = (c > i) & (c <= i + w) & ((n > 0) | (c >= w))
    mask = jnp.concatenate([band] * group, axis=0)
    slot = lax.broadcasted_iota(jnp.int32, (w, kvw), 1) >> (hd.bit_length() - 1)
    for j in range(kvh):
        qg = q_ref[:, j * kvw:(j + 1) * kvw] * (hd ** -0.5)
        qs = jnp.concatenate([jnp.where(slot == j, pltpu.roll(qg, ((j - p) % kvh) * hd, 1), 0)
                              for p in range(group)], axis=0)
        s = jnp.where(mask, _dot_nt(qs, k2), -jnp.inf)
        of = _attn_softmax_pv(s, _sink_col(sinks_ref, j, group, w), v2).astype(o_ref.dtype)
        og = of[0:w, :]
        for p in range(group):
            piece = of[p * w:(p + 1) * w, :]
            og = jnp.where(slot == p, pltpu.roll(piece, ((p - j) % kvh) * hd, 1), og)
        o_ref[:, j * kvw:(j + 1) * kvw] = og


def attn_prompt_call(qkv, sinks, bsz, t, n_heads):
    w, hd, kvh = WINDOW, ATT_HEAD_DIM, ATT_KV_HEADS
    nb = t // w
    qd = n_heads * hd
    kvd = 2 * kvh * hd
    assert qd % kvd == 0
    kvblk = qd // kvd
    kern = functools.partial(_attn_prompt_kernel, n_heads=n_heads)
    return pl.pallas_call(
        kern,
        grid=(bsz, nb),
        in_specs=[pl.BlockSpec(memory_space=pltpu.SMEM),
                  pl.BlockSpec((w, qd), lambda b, n: (b * nb + n, 0)),
                  pl.BlockSpec((w, kvd), lambda b, n: (b * nb + n, kvblk)),
                  pl.BlockSpec((w, kvd), lambda b, n: (b * nb + jnp.maximum(n - 1, 0), kvblk))],
        out_specs=pl.BlockSpec((w, qd), lambda b, n: (b * nb + n, 0)),
        out_shape=jax.ShapeDtypeStruct((bsz * t, qd), bf16),
        compiler_params=_cparams(2),
        name="attn_prompt",
    )(sinks, qkv, qkv, qkv)


def _shifted_window(cache, new_rows, t_valid):
    tp, w = new_rows.shape[0], cache.shape[1]
    hi = new_rows.astype(bf16)
    r1 = new_rows - hi.astype(f32)
    mid = r1.astype(bf16)
    lo = (r1 - mid.astype(f32)).astype(bf16)
    t = lax.broadcasted_iota(jnp.int32, (tp, w), 0)
    col = lax.broadcasted_iota(jnp.int32, (tp, w), 1)
    place = ((col == w - t_valid + t) & (t < t_valid)).astype(bf16)
    placed = _dot_tn(jnp.concatenate([hi, mid, lo], axis=0), jnp.concatenate([place] * 3, axis=0))
    lane = lax.broadcasted_iota(jnp.int32, cache.shape, 1)
    return jnp.where(lane >= w - t_valid, placed, pltpu.roll(cache, w - t_valid, 1))


def _attn_cache_kernel(*refs, n_heads, t_valid, bt):
    sinks_ref, q_ref, ck_ref, cv_ref = refs[:4]
    o_ref, nk_ref, nv_ref = refs[-3:]
    w, hd, kvh = WINDOW, ATT_HEAD_DIM, ATT_KV_HEADS
    group = n_heads // kvh
    tp = q_ref.shape[1]
    qd = n_heads * hd
    tq = lax.broadcasted_iota(jnp.int32, (tp, w), 0)
    jc = lax.broadcasted_iota(jnp.int32, (tp, w), 1)
    mc = (jc > tq) & (jc - w + PAST_LEN >= 0)
    tq2 = lax.broadcasted_iota(jnp.int32, (tp, tp), 0)
    un = lax.broadcasted_iota(jnp.int32, (tp, tp), 1)
    mn = (un <= tq2) & (un < t_valid)
    mask_c = jnp.concatenate([mc] * n_heads, axis=0)
    mask_n = jnp.concatenate([mn] * n_heads, axis=0)
    kvw = kvh * hd
    assert group == kvh
    slot = lax.broadcasted_iota(jnp.int32, (tp, kvw), 1) >> (hd.bit_length() - 1)
    sink = _sink_col(sinks_ref, 0, n_heads, tp)

    for bi in range(bt):
        row = q_ref[bi]
        ck = ck_ref[bi]
        cv = cv_ref[bi]
        kn = row[:, qd:qd + kvw]
        vn = row[:, qd + kvw:]
        win = _shifted_window(jnp.concatenate([ck, cv], axis=0), row[:, qd:], t_valid)
        nk_ref[bi] = win[:kvw]
        nv_ref[bi] = win[kvw:]
        pieces = []
        for j in range(kvh):
            qg = row[:, j * kvw:(j + 1) * kvw] * (hd ** -0.5)
            for p in range(group):
                pieces.append(jnp.where(slot == j, pltpu.roll(qg, ((j - p) % kvh) * hd, 1), 0.0))
        qs = jnp.concatenate(pieces, axis=0)
        sc = jnp.where(mask_c, _dot(qs, ck), -jnp.inf)
        sn = jnp.where(mask_n, _dot_nt(qs, kn), -jnp.inf)
        m = jnp.maximum(jnp.maximum(jnp.max(sc, axis=-1, keepdims=True),
                                    jnp.max(sn, axis=-1, keepdims=True)), sink)
        pc = jnp.exp(sc - m)
        pn = jnp.exp(sn - m)
        l = jnp.sum(pc, axis=-1, keepdims=True) + jnp.sum(pn, axis=-1, keepdims=True) + jnp.exp(sink - m)
        of = (_dot_nt(pc, cv) + _dot(pn, vn)) / l
        for j in range(kvh):
            og = jnp.zeros((tp, kvw), f32)
            for p in range(group):
                h = j * group + p
                og = og + jnp.where(slot == p, pltpu.roll(of[h * tp:(h + 1) * tp, :], ((p - j) % kvh) * hd, 1), 0.0)
            o_ref[bi, :, j * kvw:(j + 1) * kvw] = og


def attn_cache_call(qkv, cache_k, cache_v, layer, sinks, n_heads, t_valid, stacked):
    bsz, tp, nq = qkv.shape
    w = WINDOW
    kd = cache_k.shape[2]
    qd = n_heads * ATT_HEAD_DIM
    bt = 8
    kern = functools.partial(_attn_cache_kernel, n_heads=n_heads, t_valid=t_valid, bt=bt)
    slab = pl.BlockSpec((None, bt, kd, w), lambda b: (layer, b, 0, 0))
    prev = [] if stacked is None else list(stacked)
    return pl.pallas_call(
        kern,
        grid=(bsz // bt,),
        in_specs=[pl.BlockSpec(memory_space=pltpu.SMEM),
                  pl.BlockSpec((bt, tp, nq), lambda b: (b, 0, 0)), slab, slab]
                 + [pl.BlockSpec(memory_space=pl.ANY)] * len(prev),
        out_specs=[pl.BlockSpec((bt, tp, qd), lambda b: (b, 0, 0)), slab, slab],
        out_shape=[jax.ShapeDtypeStruct((bsz, tp, qd), f32),
                   jax.ShapeDtypeStruct(cache_k.shape, f32),
                   jax.ShapeDtypeStruct(cache_v.shape, f32)],
        input_output_aliases={4: 1, 5: 2} if prev else {},
        compiler_params=_cparams(1),
        name="attn_cache",
    )(sinks, qkv, cache_k, cache_v, *prev)


def _hgrn_chunk(x, st, lb, nw, tril, diag_mask, valid, grp, natural=False):
    nh = HG_HEADS
    c_rows = x.shape[0]
    dim = x.shape[1] // 4
    dk = dim // nh
    ng = c_rows // grp
    if valid is not None:
        x = jnp.where(valid, x, 0.0)
    q = _silu(x[:, 0:dim])
    fg = lb + (1.0 - lb) * jax.nn.sigmoid(x[:, dim:2 * dim])
    logf = jnp.log(fg)
    k = 1.0 - fg
    v = x[:, 2 * dim:3 * dim]
    gate = x[:, 3 * dim:4 * dim]
    if valid is not None:
        logf = jnp.where(valid, logf, 0.0)
        k = jnp.where(valid, k, 0.0)
    bcum = _cumsum_rows(logf, tril.astype(bf16))

    def rows_of(vals):
        return jnp.concatenate([jnp.broadcast_to(r, (grp, dk)) for r in vals], axis=0)

    ys, st_out = [], []
    for h in range(nh):
        sl = slice(h * dk, (h + 1) * dk)
        b = bcum[:, sl]
        qh, kh, vh = q[:, sl], k[:, sl], v[:, sl]
        zero = jnp.zeros((1, dk), f32)
        r = [zero] + [b[i * grp - 1:i * grp, :] for i in range(1, ng)] + [b[c_rows - 1:c_rows, :]]
        mid = [b[i * grp + grp // 2 - 1:i * grp + grp // 2, :] for i in range(ng)]
        b_last = r[ng]
        r_start = rows_of(r[:ng])
        r_end = rows_of(r[1:])
        r_mid = rows_of(mid)
        qg = qh * jnp.exp(b - r_start)
        kt = kh * jnp.exp(r_end - b)
        qm = qh * jnp.exp(b - r_mid)
        km = kh * jnp.exp(r_mid - b)
        a = jnp.where(diag_mask, _dot_nt(qm, km), 0.0)
        if ng > 1:
            lhs, rhs = [], []
            for j in range(ng - 1):
                lhs.append(jnp.concatenate(
                    [jnp.zeros((grp, dk), f32) if i <= j else qg[i * grp:(i + 1) * grp, :] * jnp.exp(r[i] - r[j + 1])
                     for i in range(ng)], axis=0))
                rhs.append(jnp.concatenate(
                    [kt[i * grp:(i + 1) * grp, :] if i == j else jnp.zeros((grp, dk), f32) for i in range(ng)], axis=0))
            a = a + _dot_nt(jnp.concatenate(lhs, axis=1), jnp.concatenate(rhs, axis=1))
        q_inter = qg * jnp.exp(r_start)
        k_dec = kt * jnp.exp(b_last - r_end)
        if natural:
            eye = lax.broadcasted_iota(jnp.int32, (dk, dk), 0) == lax.broadcasted_iota(jnp.int32, (dk, dk), 1)
            dec_col = jnp.sum(jnp.where(eye, jnp.exp(b_last), 0.0), axis=1, keepdims=True)
            o = _dot(a, vh) + _dot(q_inter, st[h])
            st_out.append(st[h] * dec_col + _dot_tn(k_dec, vh))
        else:
            o = _dot(a, vh) + _dot_nt(q_inter, st[h])
            st_out.append(st[h] * jnp.exp(b_last) + _dot_tn(vh, k_dec))
        on = o * lax.rsqrt(jnp.mean(o * o, axis=-1, keepdims=True) + RMS_EPS)
        ys.append(on * nw[:, sl] * _silu(gate[:, sl]))
    return ys, st_out


def _hgrn_kernel(*refs, c_rows, grp, t_valid, t_pad, layer, has_s0, ub, uc):
    if has_s0:
        p_ref, lbl_ref, nw_ref, s0_ref, y_ref, so_ref, st_ref = refs
    else:
        p_ref, lbl_ref, nw_ref, y_ref, so_ref, st_ref = refs
    c = pl.program_id(1)
    nc = pl.num_programs(1)
    nh = HG_HEADS
    dim = p_ref.shape[1] // 4
    dk = dim // nh

    @pl.when(c == 0)
    def _():
        if has_s0:
            for s in range(ub):
                for h in range(nh):
                    st_ref[s, h] = s0_ref[s, h]
        else:
            st_ref[...] = jnp.zeros_like(st_ref)

    lbl = lbl_ref[...]
    e = jnp.exp(lbl - jnp.max(lbl, axis=0, keepdims=True))
    sm = e / jnp.sum(e, axis=0, keepdims=True)
    lb = jnp.zeros((1, dim), f32)
    for r in range(1, layer + 1):
        lb = lb + sm[r:r + 1, :]

    tril = _tril_mask(c_rows)
    rr = lax.broadcasted_iota(jnp.int32, (c_rows, c_rows), 0)
    ss = lax.broadcasted_iota(jnp.int32, (c_rows, c_rows), 1)
    gshift = grp.bit_length() - 1
    diag_mask = tril & ((rr >> gshift) == (ss >> gshift))
    nw = nw_ref[...]
    rows = lax.broadcasted_iota(jnp.int32, (c_rows, 1), 0)

    for s in range(ub):
        st = [st_ref[s, h] for h in range(nh)]
        for cc in range(uc):
            r0 = (s * uc + cc) * c_rows
            valid = ((c * uc + cc) * c_rows + rows) < t_valid if t_valid < t_pad else None
            ys, st = _hgrn_chunk(p_ref[r0:r0 + c_rows, :], st, lb, nw, tril, diag_mask, valid, grp, has_s0)
            for h in range(nh):
                y_ref[r0:r0 + c_rows, h * dk:(h + 1) * dk] = ys[h].astype(y_ref.dtype)
        for h in range(nh):
            st_ref[s, h] = st[h]

    @pl.when(c == nc - 1)
    def _():
        for s in range(ub):
            for h in range(nh):
                so_ref[s, h] = st_ref[s, h] if has_s0 else st_ref[s, h].T


def _units_per_step(bsz, nc, max_seqs, max_chunks):
    if nc == 1:
        return max(u for u in (1, 2, 4, 8) if u <= max_seqs and bsz % u == 0), 1
    return 1, max(u for u in (1, 2, 4, 8) if u <= max_chunks and nc % u == 0)


def hgrn_call(p, lb_logits, norm_w, s0, bsz, t_pad, t_valid, layer):
    dim = p.shape[1] // 4
    nh = HG_HEADS
    dk = dim // nh
    c_rows = min(128, t_pad)
    grp = min(HG_CHUNK_GROUP, c_rows)
    nc = t_pad // c_rows
    ub, uc = _units_per_step(bsz, nc, 8, 8)
    rows = ub * uc * c_rows
    ncs = nc // uc
    has_s0 = s0 is not None
    in_specs = [pl.BlockSpec((rows, 4 * dim), lambda b, c: (b * ncs + c, 0)),
                pl.BlockSpec(lb_logits.shape, lambda b, c: (0, 0)),
                pl.BlockSpec((1, dim), lambda b, c: (0, 0))]
    args = [p, lb_logits, norm_w.reshape(1, dim)]
    if has_s0:
        in_specs.append(pl.BlockSpec((ub, nh, dk, dk), lambda b, c: (b, 0, 0, 0)))
        args.append(s0)
    kern = functools.partial(_hgrn_kernel, c_rows=c_rows, grp=grp, t_valid=t_valid, t_pad=t_pad,
                             layer=layer, has_s0=has_s0, ub=ub, uc=uc)
    return pl.pallas_call(
        kern,
        grid=(bsz // ub, ncs),
        in_specs=in_specs,
        out_specs=[pl.BlockSpec((rows, dim), lambda b, c: (b * ncs + c, 0)),
                   pl.BlockSpec((ub, nh, dk, dk), lambda b, c: (b, 0, 0, 0))],
        out_shape=[jax.ShapeDtypeStruct((bsz * t_pad, dim), bf16),
                   jax.ShapeDtypeStruct((bsz, nh, dk, dk), f32)],
        scratch_shapes=[pltpu.VMEM((ub, nh, dk, dk), f32)],
        compiler_params=_cparams(2),
        name="hgrn",
    )(*args)


def _expand_heads(w, ex_ref):
    hi = w.astype(bf16)
    r1 = w - hi.astype(f32)
    mid = r1.astype(bf16)
    lo = (r1 - mid.astype(f32)).astype(bf16)
    return jnp.dot(jnp.concatenate([hi, mid, lo], axis=1), ex_ref[...], preferred_element_type=f32)


def _ssd_kernel(*refs, c_rows, t_valid, t_pad, has_s0, ub, uc, conv_done):
    if has_s0:
        (zx_ref, cw_ref, cb_ref, dtb_ref, alog_ref, dsk_ref, nw_ref, ex_ref, s0_ref, buf_ref,
         y_ref, so_ref, nbuf_ref, st_ref, carry_ref) = refs
    else:
        (zx_ref, cw_ref, cb_ref, dtb_ref, alog_ref, dsk_ref, nw_ref, ex_ref,
         y_ref, so_ref, st_ref, carry_ref) = refs
    c = pl.program_id(1)
    nc = pl.num_programs(1)
    hp, ns, ngr = SSD_HEAD_DIM, SSD_STATE, SSD_GROUPS
    di = y_ref.shape[1]
    nh = di // hp
    hpg = nh // ngr
    gw = ngr * ns
    kconv = cw_ref.shape[0]
    pair = 2 * hp
    assert pair == LANES and hpg % 2 == 0

    @pl.when(c == 0)
    def _():
        carry_ref[...] = jnp.zeros_like(carry_ref)
        if has_s0:
            for s in range(ub):
                st_ref[s] = s0_ref[s]
                carry_ref[s, SUBLANES - (kconv - 1):, :] = buf_ref[s]
        else:
            st_ref[...] = jnp.zeros_like(st_ref)

    cdim = cw_ref.shape[1]
    rows = lax.broadcasted_iota(jnp.int32, (c_rows, 1), 0)
    rows8 = rows[:SUBLANES]
    cw = cw_ref[...]
    a_neg = -jnp.exp(alog_ref[...])
    tril = _tril_mask(c_rows)
    dsk = dsk_ref[...]
    nw = nw_ref[...]
    lo = lax.broadcasted_iota(jnp.int32, (c_rows, LANES), 1) < hp
    for s in range(ub):
        for cc in range(uc):
            r0 = (s * uc + cc) * c_rows
            valid = ((c * uc + cc) * c_rows + rows) < t_valid if t_valid < t_pad else None
            _ssd_chunk(zx_ref[r0:r0 + c_rows, :], st_ref.at[s], carry_ref.at[s], y_ref.at[r0:r0 + c_rows, :],
                       cw, cb_ref[...], dtb_ref[...], a_neg, dsk, nw, ex_ref, tril, lo, rows8, valid, conv_done)

    @pl.when(c == nc - 1)
    def _():
        for s in range(ub):
            so_ref[s] = st_ref[s]
            if has_s0:
                last = s * c_rows + t_valid
                nbuf_ref[s] = zx_ref[last - (kconv - 1):last, di:di + cdim]


def _ssd_chunk(zx, st_ref, carry_ref, y_ref, cw, cb, dtb, a_neg, dsk, nw, ex_ref, tril, lo, rows8, valid, conv_done):
    hp, ns, ngr = SSD_HEAD_DIM, SSD_STATE, SSD_GROUPS
    c_rows = zx.shape[0]
    di = y_ref.shape[1]
    nh = di // hp
    hpg = nh // ngr
    gw = ngr * ns
    kconv, cdim = cw.shape
    pair = 2 * hp
    z = zx[:, :di]
    u = zx[:, di:di + cdim]
    if conv_done:
        xbc = u
    else:
        xbc = _silu(_causal_conv(u, carry_ref[...], cw, cb, rows8))
        carry_ref[...] = u[c_rows - SUBLANES:, :]
    dt = jax.nn.softplus(zx[:, di + cdim:] + dtb)
    if valid is not None:
        xbc = jnp.where(valid, xbc, 0.0)
        dt = jnp.where(valid, dt, 0.0)
    xs = xbc[:, :di]
    bm = xbc[:, di:di + gw]
    cm = xbc[:, di + gw:di + 2 * gw]
    acum = _cumsum_rows(dt * a_neg, tril.astype(bf16))

    def head_rows(a):
        if c_rows < LANES:
            a = jnp.concatenate([a, jnp.zeros((LANES - c_rows, LANES), f32)], axis=0)
        return a.T[:, :c_rows]

    acum_t = head_rows(acum)
    dt_t = head_rows(dt)
    a_last = acum[c_rows - 1:c_rows, :]
    xd = xs * _expand_heads(dt * jnp.exp(a_last - acum), ex_ref)
    first = lax.broadcasted_iota(jnp.int32, (pair, ns), 0) < hp

    ys = []
    for g in range(ngr):
        bg = bm[:, g * ns:(g + 1) * ns]
        cg = cm[:, g * ns:(g + 1) * ns]
        cb_mat = _dot_nt(cg, bg)
        gsl = slice(g * hpg * hp, (g + 1) * hpg * hp)
        y_inter = _dot_nt(cg, st_ref[gsl, :])
        decs = []
        for q in range(hpg // 2):
            h1 = g * hpg + 2 * q
            psl = slice(h1 * hp, h1 * hp + pair)
            ms, es = [], []
            for h in (h1, h1 + 1):
                colb = jnp.broadcast_to(acum[:, h:h + 1], (c_rows, LANES))
                seg = colb[:, :c_rows] - acum_t[h:h + 1, :]
                ms.append(cb_mat * jnp.exp(jnp.where(tril, seg, -jnp.inf)) * dt_t[h:h + 1, :])
                es.append(jnp.exp(colb))
            xp = xs[:, psl]
            x_lo = jnp.where(lo, xp, 0.0)
            x_hi = jnp.where(lo, 0.0, xp)
            if (2 * c_rows) % LANES == 0:
                y_intra = _dot(jnp.concatenate(ms, axis=1), jnp.concatenate([x_lo, x_hi], axis=0))
            else:
                y_intra = _dot(ms[0], x_lo) + _dot(ms[1], x_hi)
            y_pair = y_intra + y_inter[:, q * pair:(q + 1) * pair] * jnp.where(lo, es[0], es[1])
            ys.append(y_pair + dsk[:, psl] * xp)
            decs.append(jnp.where(first, jnp.exp(a_last[:, h1:h1 + 1]), jnp.exp(a_last[:, h1 + 1:h1 + 2])))
        st_ref[gsl, :] = st_ref[gsl, :] * jnp.concatenate(decs, axis=0) + _dot_tn(xd[:, gsl], bg)

    y = jnp.concatenate(ys, axis=1) * _silu(z)
    gdim = di // ngr
    for g in range(ngr):
        sl = slice(g * gdim, (g + 1) * gdim)
        yg = y[:, sl]
        y_ref[:, sl] = (yg * lax.rsqrt(jnp.mean(yg * yg, axis=-1, keepdims=True) + RMS_EPS) * nw[:, sl]).astype(y_ref.dtype)


def ssd_call(zx, conv_w, conv_b, dt_bias, a_log, d_skip, norm_w, s0, conv_buf, bsz, t_pad, t_valid, conv_done):
    hp, ns = SSD_HEAD_DIM, SSD_STATE
    conv_dim = conv_w.shape[1]
    di = zx.shape[1] - conv_dim - LANES
    nh = di // hp
    c_rows = min(128, t_pad)
    nc = t_pad // c_rows
    ub, uc = _units_per_step(bsz, nc, 4, 4)
    rows = ub * uc * c_rows
    ncs = nc // uc
    has_s0 = s0 is not None
    assert not (conv_done and has_s0)
    pad_h = lambda a: jnp.pad(a.reshape(1, nh), ((0, 0), (0, LANES - nh)))
    full = lambda shape: pl.BlockSpec(shape, lambda b, c: (0,) * len(shape))
    sel = (jnp.arange(di)[None, :] // hp == jnp.arange(LANES)[:, None]).astype(bf16)
    expand = jnp.concatenate([sel, sel, sel], axis=0)
    in_specs = [pl.BlockSpec((rows, zx.shape[1]), lambda b, c: (b * ncs + c, 0)),
                full(conv_w.shape), full((1, conv_dim)), full((1, LANES)), full((1, LANES)),
                full((1, di)), full((1, di)), full(expand.shape)]
    args = [zx, conv_w, conv_b.reshape(1, conv_dim), pad_h(dt_bias), pad_h(a_log),
            jnp.repeat(d_skip, hp).reshape(1, di), norm_w.reshape(1, di), expand]
    out_specs = [pl.BlockSpec((rows, di), lambda b, c: (b * ncs + c, 0)),
                 pl.BlockSpec((ub, di, ns), lambda b, c: (b, 0, 0))]
    out_shape = [jax.ShapeDtypeStruct((bsz * t_pad, di), bf16), jax.ShapeDtypeStruct((bsz, di, ns), f32)]
    if has_s0:
        assert nc == 1 and t_valid >= conv_w.shape[0] - 1
        in_specs += [pl.BlockSpec((ub, di, ns), lambda b, c: (b, 0, 0)),
                     pl.BlockSpec((ub,) + conv_buf.shape[1:], lambda b, c: (b, 0, 0))]
        args += [s0.reshape(bsz, di, ns), conv_buf]
        out_specs.append(pl.BlockSpec((ub,) + conv_buf.shape[1:], lambda b, c: (b, 0, 0)))
        out_shape.append(jax.ShapeDtypeStruct(conv_buf.shape, f32))
    kern = functools.partial(_ssd_kernel, c_rows=c_rows, t_valid=t_valid, t_pad=t_pad, has_s0=has_s0, ub=ub, uc=uc,
                             conv_done=conv_done)
    outs = pl.pallas_call(
        kern,
        grid=(bsz // ub, ncs),
        in_specs=in_specs,
        out_specs=out_specs,
        out_shape=out_shape,
        scratch_shapes=[pltpu.VMEM((ub, di, ns), f32), pltpu.VMEM((ub, SUBLANES, conv_dim), f32)],
        compiler_params=_cparams(2),
        name="ssd",
    )(*args)
    return outs[0], outs[1].reshape(bsz, nh, hp, ns), (outs[2] if has_s0 else None)


def _trunk(x, c_k, c_v, s_hg, s_ssm, s_sconv, s_fconv, prm):
    bsz, t, d = x.shape
    depth = prm["norm_mix_w"].shape[0]
    decode = c_k is not None
    tp = -(-t // SUBLANES) * SUBLANES
    if tp != t:
        x = jnp.pad(x, ((0, 0), (0, tp - t), (0, 0)))
    m = bsz * tp
    n_heads = d // ATT_HEAD_DIM
    qd = n_heads * ATT_HEAD_DIM
    kvd = ATT_KV_HEADS * ATT_HEAD_DIM
    x2 = x.reshape(m, d)
    xn, xn_w = x2, prm["norm_mix_w"][0]
    nk, nv, nhg, nssm, nsconv, nfconv = [], [], [], [], [], []
    if decode:
        ck_all = jnp.transpose(c_k, (0, 1, 3, 4, 2)).reshape(c_k.shape[0], bsz, kvd, WINDOW)
        cv_all = jnp.transpose(c_v, (0, 1, 3, 4, 2)).reshape(c_v.shape[0], bsz, kvd, WINDOW)
        assert c_k.shape[0] == -(-depth // N_MIXERS)
        kv_stacked = None
    assert s_fconv is None or s_fconv.shape[0] == depth
    nf_stacked = None
    slot = [0] * N_MIXERS
    for i in range(depth):
        kind = i % N_MIXERS
        j = slot[kind]
        slot[kind] += 1
        nw_ffn = prm["norm_ffn_w"][i]
        if kind == 0:
            w_qkv, b_qkv = prm["attn_w_qkv"], prm["attn_b_qkv"][j]
            nq = w_qkv.shape[2]
            if decode:
                qkv = proj_call(xn, xn_w, w_qkv, j, b_qkv, f32, "attn_qkv").reshape(bsz, tp, nq)
                o, *kv_stacked = attn_cache_call(qkv, ck_all, cv_all, j, prm["attn_sinks"][j], n_heads, t, kv_stacked)
                o = o.reshape(m, qd)
            else:
                qkv = proj_call(xn, xn_w, w_qkv, j, b_qkv, bf16, "attn_qkv")
                o = attn_prompt_call(qkv, prm["attn_sinks"][j], bsz, t, n_heads)
                xn_last = xn.reshape(bsz, t, d)[:, t - WINDOW:].reshape(bsz * WINDOW, d)
                kv_last = proj_call(xn_last, xn_w, w_qkv[:, :, qd:], j, b_qkv[qd:], f32,
                                    "attn_kv_tail").reshape(bsz, WINDOW, 2, kvd)
                to4 = lambda a: a.reshape(bsz, WINDOW, ATT_KV_HEADS, ATT_HEAD_DIM)
                nk.append(to4(kv_last[:, :, 0]))
                nv.append(to4(kv_last[:, :, 1]))
            act, w_o, b_o = o, (prm["attn_w_o"], j), prm["attn_b_o"][j]
        elif kind == 1:
            p = proj_call(xn, xn_w, prm["hgrn_w_in"], j, None, f32, "hgrn_in")
            y, s_new = hgrn_call(p, prm["hgrn_lb_logits"], prm["hgrn_norm_w"][j],
                                 None if s_hg is None else s_hg[j], bsz, tp, t, i)
            nhg.append(s_new)
            act, w_o, b_o = y, (prm["hgrn_w_o"], j), None
        else:
            conv_dim = prm["ssd_conv_w"].shape[2]
            di = prm["ssd_w_o"].shape[1]
            kc = prm["ssd_conv_w"].shape[1]
            if s_sconv is None:
                zx, pre = ssd_in_call(xn, prm["ssd_w_in"], j, prm["ssd_conv_w"][j], prm["ssd_conv_b"][j], tp)
                pre = pre.reshape(bsz, tp // min(PROJ_TM, m), SUBLANES, conv_dim)[:, -1, SUBLANES - (kc - 1):]
            else:
                zx = proj_call(xn, xn_w, prm["ssd_w_in"], j, None, f32, "ssd_in")
            y, s_new, conv_new = ssd_call(zx, prm["ssd_conv_w"][j], prm["ssd_conv_b"][j], prm["ssd_dt_bias"][j],
                                prm["ssd_a_log"][j], prm["ssd_d"][j], prm["ssd_norm_w"][j],
                                None if s_ssm is None else s_ssm[j],
                                None if s_sconv is None else s_sconv[j], bsz, tp, t, s_sconv is None)
            nssm.append(s_new)
            nsconv.append(pre if s_sconv is None else conv_new)
            act, w_o, b_o = y, (prm["ssd_w_o"], j), None
        last = i == depth - 1
        next_w = prm["norm_final_w"] if last else prm["norm_mix_w"][i + 1]
        dff = prm["ffn_w_down"].shape[1]
        kf = prm["ffn_conv_w"].shape[1]
        x2, xn, tail = ffn_call(act, w_o, b_o, x2, nw_ffn, (prm["ffn_w_up"], i), prm["ffn_conv_w"][i],
                                prm["ffn_conv_b"][i], (prm["ffn_w_down"], i), next_w, tp,
                                None if s_fconv is None else (s_fconv, i), t, f32 if last else bf16, "ffn", nf_stacked)
        xn_w = None
        if s_fconv is not None:
            nf_stacked = tail
        else:
            tiles_per_seq = tp // tail_tile_rows(m)
            tl = tail.reshape(bsz, tiles_per_seq, SUBLANES, dff)[:, -1]
            nfconv.append(tl[:, SUBLANES - (kf - 1):])
    stack = lambda xs: xs[0][None] if len(xs) == 1 else jnp.stack(xs)
    y = xn.reshape(bsz, tp, d)[:, :t]
    if decode:
        unview = lambda a: jnp.transpose(a.reshape(a.shape[0], bsz, ATT_KV_HEADS, ATT_HEAD_DIM, WINDOW), (0, 1, 4, 2, 3))
        nk, nv = unview(kv_stacked[0]), unview(kv_stacked[1])
    else:
        nk, nv = stack(nk), stack(nv)
    return (y, nk, nv, stack(nhg), stack(nssm), stack(nsconv), stack(nfconv) if s_fconv is None else nf_stacked)


def tail_tile_rows(m):
    return min(FFN_TM, m)


def kernel(x_prompt, x_sample, cache_attn_k, cache_attn_v, state_hgrn, state_ssm, state_ssm_conv, state_ffn_conv, norm_mix_w, norm_ffn_w, norm_final_w, attn_w_qkv, attn_b_qkv, attn_sinks, attn_w_o, attn_b_o, hgrn_w_in, hgrn_lb_logits, hgrn_norm_w, hgrn_w_o, ssd_w_in, ssd_conv_w, ssd_conv_b, ssd_dt_bias, ssd_a_log, ssd_d, ssd_norm_w, ssd_w_o, ffn_w_up, ffn_conv_w, ffn_conv_b, ffn_w_down):
    cast = lambda w: w.astype(bf16)
    ssd_in_pad = (-ssd_w_in.shape[2]) % LANES
    prm = dict(norm_mix_w=norm_mix_w, norm_ffn_w=norm_ffn_w, norm_final_w=norm_final_w,
               attn_w_qkv=cast(attn_w_qkv), attn_b_qkv=attn_b_qkv, attn_sinks=attn_sinks,
               attn_w_o=cast(attn_w_o), attn_b_o=attn_b_o,
               hgrn_w_in=cast(hgrn_w_in), hgrn_lb_logits=hgrn_lb_logits, hgrn_norm_w=hgrn_norm_w,
               hgrn_w_o=cast(hgrn_w_o),
               ssd_w_in=jnp.pad(cast(ssd_w_in), ((0, 0), (0, 0), (0, ssd_in_pad))), ssd_conv_w=ssd_conv_w,
               ssd_conv_b=ssd_conv_b, ssd_dt_bias=ssd_dt_bias,
               ssd_a_log=ssd_a_log, ssd_d=ssd_d, ssd_norm_w=ssd_norm_w, ssd_w_o=cast(ssd_w_o),
               ffn_w_up=cast(ffn_w_up), ffn_conv_w=ffn_conv_w, ffn_conv_b=ffn_conv_b, ffn_w_down=cast(ffn_w_down))
    outs_p = _trunk(x_prompt, None, None, None, None, None, None, prm)
    outs_s = _trunk(x_sample, cache_attn_k, cache_attn_v, state_hgrn, state_ssm, state_ssm_conv,
                    state_ffn_conv, prm)
    return (outs_p[0], outs_s[0]) + outs_p[1:] + outs_s[1:]
```
